```python
import math
import jax, jax.numpy as jnp
from jax import lax
import numpy as np

D_MODEL = 2048
BATCH = 8
SEQ = 2048
DEPTH = 4

N_MIXERS = 3
HEAD_DIM = 128
N_HEADS = D_MODEL // HEAD_DIM
ATTN_WIDTH = N_HEADS * HEAD_DIM
BLOCK_Q = 128
D_FF = 4 * D_MODEL
MLA_Q_RANK = 3 * D_MODEL // 8
MLA_KV_RANK = D_MODEL // 4
MLA_NOPE = 128
MLA_ROPE = 64
MLA_V = 128
MLA_QK = MLA_NOPE + MLA_ROPE
ROPE_THETA = 10000.0
EPS = 1e-6
N_SB = (DEPTH + 2) // 3
N_FOX = (DEPTH + 1) // 3
N_MLA = DEPTH // 3

kernel_name = "interleaved_sb_fox_mla_trunk"


def rmsnorm(x, g):
    xf = x.astype(jnp.float32)
    y = xf * lax.rsqrt(jnp.mean(xf * xf, axis=-1, keepdims=True) + EPS)
    return (y * g.astype(jnp.float32)).astype(x.dtype)


def sweep_query_blocks(block_fn, seq_len):
    outs = [block_fn(qs, qs + BLOCK_Q) for qs in range(0, seq_len, BLOCK_Q)]
    return jnp.concatenate(outs, axis=2)


def split_heads_qkv(qkv, b, s):
    t = qkv.reshape(b, s, 3, N_HEADS, HEAD_DIM).transpose(2, 0, 3, 1, 4)
    return t[0], t[1], t[2]


def merge_heads(o):
    b, h, s, d = o.shape
    return o.transpose(0, 2, 1, 3).reshape(b, s, h * d)


def stick_breaking_block(q_blk, k_pre, v_pre, q_start):
    n_q, n_k = q_blk.shape[2], k_pre.shape[2]
    z = jnp.einsum('bhtd,bhsd->bhts', q_blk, k_pre).astype(jnp.float32) * (1.0 / math.sqrt(HEAD_DIM))
    strict = jnp.arange(n_k)[None, :] < (q_start + jnp.arange(n_q))[:, None]
    log_keep = jnp.where(strict, jax.nn.log_sigmoid(-z), 0.0)
    later = lax.cumsum(log_keep, axis=3, reverse=True) - log_keep
    a = jnp.where(strict, jnp.exp(jax.nn.log_sigmoid(z) + later), 0.0)
    return jnp.einsum('bhts,bhsd->bhtd', a.astype(v_pre.dtype), v_pre)


def causal_softmax_block(q_blk, k_pre, v_pre, q_start, scale, bias=None):
    n_q, n_k = q_blk.shape[2], k_pre.shape[2]
    logits = jnp.einsum('bhtd,bhsd->bhts', q_blk, k_pre).astype(jnp.float32) * scale
    if bias is not None:
        logits = logits + bias
    causal = jnp.arange(n_k)[None, :] <= (q_start + jnp.arange(n_q))[:, None]
    p = jax.nn.softmax(jnp.where(causal, logits, -jnp.inf), axis=-1)
    return jnp.einsum('bhts,bhsd->bhtd', p.astype(v_pre.dtype), v_pre)


def stick_breaking_mixer(h, w_in, w_out):
    b, s, _ = h.shape
    q, k, v = split_heads_qkv(h @ w_in, b, s)
    o = sweep_query_blocks(
        lambda qs, qe: stick_breaking_block(q[:, :, qs:qe], k[:, :, :qe], v[:, :, :qe], qs), s)
    return merge_heads(o) @ w_out


def forgetting_mixer(h, w_in, b_f, q_gain, k_gain, w_out):
    b, s, _ = h.shape
    proj = h @ w_in
    q, k, v = split_heads_qkv(proj[..., :3 * ATTN_WIDTH], b, s)
    log_f = jax.nn.log_sigmoid((proj[..., 3 * ATTN_WIDTH:] + b_f).astype(jnp.float32))
    cf = jnp.cumsum(log_f, axis=1).transpose(0, 2, 1)
    q = rmsnorm(q, q_gain)
    k = rmsnorm(k, k_gain)
    scale = 1.0 / math.sqrt(HEAD_DIM)

    def blk(qs, qe):
        bias = cf[:, :, qs:qe, None] - cf[:, :, None, :qe]
        return causal_softmax_block(q[:, :, qs:qe], k[:, :, :qe], v[:, :, :qe], qs, scale, bias)

    return merge_heads(sweep_query_blocks(blk, s)) @ w_out


def rope(x, positions):
    half = x.shape[-1] // 2
    inv_freq = ROPE_THETA ** (-jnp.arange(0, half, dtype=jnp.float32) * 2.0 / x.shape[-1])
    ang = positions.astype(jnp.float32)[..., None] * inv_freq
    if x.ndim == 4:
        ang = ang[:, :, None, :]
    cos, sin = jnp.cos(ang), jnp.sin(ang)
    xf = x.astype(jnp.float32)
    x1, x2 = xf[..., :half], xf[..., half:]
    return jnp.concatenate([x1 * cos - x2 * sin, x1 * sin + x2 * cos], axis=-1).astype(x.dtype)


def mla_mixer(h, positions, w_in, q_norm, kv_norm, w_uq, w_ukv, q_gain, k_gain, w_out):
    b, s, _ = h.shape
    down = h @ w_in
    c_q = rmsnorm(down[..., :MLA_Q_RANK], q_norm)
    c_kv = rmsnorm(down[..., MLA_Q_RANK:MLA_Q_RANK + MLA_KV_RANK], kv_norm)
    k_rope = rope(down[..., MLA_Q_RANK + MLA_KV_RANK:], positions)
    q = (c_q @ w_uq).reshape(b, s, N_HEADS, MLA_QK)
    q = jnp.concatenate([q[..., :MLA_NOPE], rope(q[..., MLA_NOPE:], positions)], axis=-1)
    kv = (c_kv @ w_ukv).reshape(b, s, N_HEADS, MLA_NOPE + MLA_V)
    k = jnp.concatenate(
        [kv[..., :MLA_NOPE], jnp.broadcast_to(k_rope[:, :, None, :], (b, s, N_HEADS, MLA_ROPE))], axis=-1)
    v = kv[..., MLA_NOPE:].transpose(0, 2, 1, 3)
    q = rmsnorm(q, q_gain).transpose(0, 2, 1, 3)
    k = rmsnorm(k, k_gain).transpose(0, 2, 1, 3)
    scale = 1.0 / math.sqrt(MLA_QK)
    o = sweep_query_blocks(
        lambda qs, qe: causal_softmax_block(q[:, :, qs:qe], k[:, :, :qe], v[:, :, :qe], qs, scale), s)
    return merge_heads(o) @ w_out


def sq_relu_mlp(h, w1, w2):
    a = jax.nn.relu(h @ w1)
    return (a * a) @ w2


def _fwd_setup_inputs(seed: int = 0) -> dict:
    key = jax.random.key(seed)
    ks = jax.random.split(key, 24)
    f32 = jnp.float32

    def nrm(k, shape, fan_in):
        return jax.random.normal(k, shape, f32) * (fan_in ** -0.5)

    def gain(k, shape):
        return 1.0 + 0.02 * jax.random.normal(k, shape, f32)

    return {
        "x": jax.random.normal(ks[0], (BATCH, SEQ, D_MODEL), f32),
        "positions": jnp.broadcast_to(jnp.arange(SEQ, dtype=jnp.int32)[None, :], (BATCH, SEQ)),
        "mix_norm": gain(ks[1], (DEPTH, D_MODEL)),
        "mlp_norm": gain(ks[2], (DEPTH, D_MODEL)),
        "sb_w_in": nrm(ks[3], (N_SB, D_MODEL, 3 * ATTN_WIDTH), D_MODEL),
        "sb_w_out": nrm(ks[4], (N_SB, ATTN_WIDTH, D_MODEL), ATTN_WIDTH),
        "fox_w_in": nrm(ks[5], (N_FOX, D_MODEL, 3 * ATTN_WIDTH + N_HEADS), D_MODEL),
        "fox_b_f": jax.random.uniform(ks[6], (N_FOX, N_HEADS), f32, 1.0, 5.0),
        "fox_q_gain": gain(ks[7], (N_FOX, HEAD_DIM)),
        "fox_k_gain": gain(ks[8], (N_FOX, HEAD_DIM)),
        "fox_w_out": nrm(ks[9], (N_FOX, ATTN_WIDTH, D_MODEL), ATTN_WIDTH),
        "mla_w_in": nrm(ks[10], (N_MLA, D_MODEL, MLA_Q_RANK + MLA_KV_RANK + MLA_ROPE), D_MODEL),
        "mla_q_norm": gain(ks[11], (N_MLA, MLA_Q_RANK)),
        "mla_kv_norm": gain(ks[12], (N_MLA, MLA_KV_RANK)),
        "mla_w_uq": nrm(ks[13], (N_MLA, MLA_Q_RANK, N_HEADS * MLA_QK), MLA_Q_RANK),
        "mla_w_ukv": nrm(ks[14], (N_MLA, MLA_KV_RANK, N_HEADS * (MLA_NOPE + MLA_V)), MLA_KV_RANK),
        "mla_q_gain": gain(ks[15], (N_MLA, MLA_QK)),
        "mla_k_gain": gain(ks[16], (N_MLA, MLA_QK)),
        "mla_w_out": nrm(ks[17], (N_MLA, N_HEADS * MLA_V, D_MODEL), N_HEADS * MLA_V),
        "mlp_w1": nrm(ks[18], (DEPTH, D_MODEL, D_FF), D_MODEL),
        "mlp_w2": nrm(ks[19], (DEPTH, D_FF, D_MODEL), D_FF),
    }


def _fwd_reference(x, positions, mix_norm, mlp_norm, sb_w_in, sb_w_out, fox_w_in, fox_b_f, fox_q_gain,
              fox_k_gain, fox_w_out, mla_w_in, mla_q_norm, mla_kv_norm, mla_w_uq, mla_w_ukv,
              mla_q_gain, mla_k_gain, mla_w_out, mlp_w1, mlp_w2):
    for i in range(DEPTH):
        kind, j = i % N_MIXERS, i // N_MIXERS
        h = rmsnorm(x, mix_norm[i])
        if kind == 0:
            y = stick_breaking_mixer(h, sb_w_in[j], sb_w_out[j])
        elif kind == 1:
            y = forgetting_mixer(h, fox_w_in[j], fox_b_f[j], fox_q_gain[j], fox_k_gain[j], fox_w_out[j])
        else:
            y = mla_mixer(h, positions, mla_w_in[j], mla_q_norm[j], mla_kv_norm[j], mla_w_uq[j],
                          mla_w_ukv[j], mla_q_gain[j], mla_k_gain[j], mla_w_out[j])
        x = x + y
        x = x + sq_relu_mlp(rmsnorm(x, mlp_norm[i]), mlp_w1[i], mlp_w2[i])
    return x


import jax as _jax
import jax.numpy as _jnp

TWIN_FORMAT = 'train_step'
FWD_PARAMS = ['x', 'positions', 'mix_norm', 'mlp_norm', 'sb_w_in', 'sb_w_out', 'fox_w_in', 'fox_b_f', 'fox_q_gain', 'fox_k_gain', 'fox_w_out', 'mla_w_in', 'mla_q_norm', 'mla_kv_norm', 'mla_w_uq', 'mla_w_ukv', 'mla_q_gain', 'mla_k_gain', 'mla_w_out', 'mlp_w1', 'mlp_w2']
TWIN_WEIGHTS = ['mix_norm', 'mlp_norm', 'sb_w_in', 'sb_w_out', 'fox_w_in', 'fox_b_f', 'fox_q_gain', 'fox_k_gain', 'fox_w_out', 'mla_w_in', 'mla_q_norm', 'mla_kv_norm', 'mla_w_uq', 'mla_w_ukv', 'mla_q_gain', 'mla_k_gain', 'mla_w_out', 'mlp_w1', 'mlp_w2']
TWIN_DIFF_INPUT = 'x'
TWIN_INPUTS = ['x', 'positions', 'mix_norm', 'mlp_norm', 'sb_w_in', 'sb_w_out', 'fox_w_in', 'fox_b_f', 'fox_q_gain', 'fox_k_gain', 'fox_w_out', 'mla_w_in', 'mla_q_norm', 'mla_kv_norm', 'mla_w_uq', 'mla_w_ukv', 'mla_q_gain', 'mla_k_gain', 'mla_w_out', 'mlp_w1', 'mlp_w2', 'loss_target', 'm_mix_norm', 'm_mlp_norm', 'm_sb_w_in', 'm_sb_w_out', 'm_fox_w_in', 'm_fox_b_f', 'm_fox_q_gain', 'm_fox_k_gain', 'm_fox_w_out', 'm_mla_w_in', 'm_mla_q_norm', 'm_mla_kv_norm', 'm_mla_w_uq', 'm_mla_w_ukv', 'm_mla_q_gain', 'm_mla_k_gain', 'm_mla_w_out', 'm_mlp_w1', 'm_mlp_w2', 'v_mix_norm', 'v_mlp_norm', 'v_sb_w_in', 'v_sb_w_out', 'v_fox_w_in', 'v_fox_b_f', 'v_fox_q_gain', 'v_fox_k_gain', 'v_fox_w_out', 'v_mla_w_in', 'v_mla_q_norm', 'v_mla_kv_norm', 'v_mla_w_uq', 'v_mla_w_ukv', 'v_mla_q_gain', 'v_mla_k_gain', 'v_mla_w_out', 'v_mlp_w1', 'v_mlp_w2']
TWIN_OUTPUTS = ['loss', 'grad_x', 'grad_mix_norm', 'grad_mlp_norm', 'grad_sb_w_in', 'grad_sb_w_out', 'grad_fox_w_in', 'grad_fox_b_f', 'grad_fox_q_gain', 'grad_fox_k_gain', 'grad_fox_w_out', 'grad_mla_w_in', 'grad_mla_q_norm', 'grad_mla_kv_norm', 'grad_mla_w_uq', 'grad_mla_w_ukv', 'grad_mla_q_gain', 'grad_mla_k_gain', 'grad_mla_w_out', 'grad_mlp_w1', 'grad_mlp_w2', 'delta_mix_norm', 'delta_mlp_norm', 'delta_sb_w_in', 'delta_sb_w_out', 'delta_fox_w_in', 'delta_fox_b_f', 'delta_fox_q_gain', 'delta_fox_k_gain', 'delta_fox_w_out', 'delta_mla_w_in', 'delta_mla_q_norm', 'delta_mla_kv_norm', 'delta_mla_w_uq', 'delta_mla_w_ukv', 'delta_mla_q_gain', 'delta_mla_k_gain', 'delta_mla_w_out', 'delta_mlp_w1', 'delta_mlp_w2', 'new_m_mix_norm', 'new_m_mlp_norm', 'new_m_sb_w_in', 'new_m_sb_w_out', 'new_m_fox_w_in', 'new_m_fox_b_f', 'new_m_fox_q_gain', 'new_m_fox_k_gain', 'new_m_fox_w_out', 'new_m_mla_w_in', 'new_m_mla_q_norm', 'new_m_mla_kv_norm', 'new_m_mla_w_uq', 'new_m_mla_w_ukv', 'new_m_mla_q_gain', 'new_m_mla_k_gain', 'new_m_mla_w_out', 'new_m_mlp_w1', 'new_m_mlp_w2', 'new_v_mix_norm', 'new_v_mlp_norm', 'new_v_sb_w_in', 'new_v_sb_w_out', 'new_v_fox_w_in', 'new_v_fox_b_f', 'new_v_fox_q_gain', 'new_v_fox_k_gain', 'new_v_fox_w_out', 'new_v_mla_w_in', 'new_v_mla_q_norm', 'new_v_mla_kv_norm', 'new_v_mla_w_uq', 'new_v_mla_w_ukv', 'new_v_mla_q_gain', 'new_v_mla_k_gain', 'new_v_mla_w_out', 'new_v_mlp_w1', 'new_v_mlp_w2']
TWIN_LEAF_KINDS = {'loss': 'loss', 'grad_x': 'grad_x', 'grad_mix_norm': 'grad_w', 'grad_mlp_norm': 'grad_w', 'grad_sb_w_in': 'grad_w', 'grad_sb_w_out': 'grad_w', 'grad_fox_w_in': 'grad_w', 'grad_fox_b_f': 'grad_w', 'grad_fox_q_gain': 'grad_w', 'grad_fox_k_gain': 'grad_w', 'grad_fox_w_out': 'grad_w', 'grad_mla_w_in': 'grad_w', 'grad_mla_q_norm': 'grad_w', 'grad_mla_kv_norm': 'grad_w', 'grad_mla_w_uq': 'grad_w', 'grad_mla_w_ukv': 'grad_w', 'grad_mla_q_gain': 'grad_w', 'grad_mla_k_gain': 'grad_w', 'grad_mla_w_out': 'grad_w', 'grad_mlp_w1': 'grad_w', 'grad_mlp_w2': 'grad_w', 'delta_mix_norm': 'delta_w', 'delta_mlp_norm': 'delta_w', 'delta_sb_w_in': 'delta_w', 'delta_sb_w_out': 'delta_w', 'delta_fox_w_in': 'delta_w', 'delta_fox_b_f': 'delta_w', 'delta_fox_q_gain': 'delta_w', 'delta_fox_k_gain': 'delta_w', 'delta_fox_w_out': 'delta_w', 'delta_mla_w_in': 'delta_w', 'delta_mla_q_norm': 'delta_w', 'delta_mla_kv_norm': 'delta_w', 'delta_mla_w_uq': 'delta_w', 'delta_mla_w_ukv': 'delta_w', 'delta_mla_q_gain': 'delta_w', 'delta_mla_k_gain': 'delta_w', 'delta_mla_w_out': 'delta_w', 'delta_mlp_w1': 'delta_w', 'delta_mlp_w2': 'delta_w', 'new_m_mix_norm': 'new_m', 'new_m_mlp_norm': 'new_m', 'new_m_sb_w_in': 'new_m', 'new_m_sb_w_out': 'new_m', 'new_m_fox_w_in': 'new_m', 'new_m_fox_b_f': 'new_m', 'new_m_fox_q_gain': 'new_m', 'new_m_fox_k_gain': 'new_m', 'new_m_fox_w_out': 'new_m', 'new_m_mla_w_in': 'new_m', 'new_m_mla_q_norm': 'new_m', 'new_m_mla_kv_norm': 'new_m', 'new_m_mla_w_uq': 'new_m', 'new_m_mla_w_ukv': 'new_m', 'new_m_mla_q_gain': 'new_m', 'new_m_mla_k_gain': 'new_m', 'new_m_mla_w_out': 'new_m', 'new_m_mlp_w1': 'new_m', 'new_m_mlp_w2': 'new_m', 'new_v_mix_norm': 'new_v', 'new_v_mlp_norm': 'new_v', 'new_v_sb_w_in': 'new_v', 'new_v_sb_w_out': 'new_v', 'new_v_fox_w_in': 'new_v', 'new_v_fox_b_f': 'new_v', 'new_v_fox_q_gain': 'new_v', 'new_v_fox_k_gain': 'new_v', 'new_v_fox_w_out': 'new_v', 'new_v_mla_w_in': 'new_v', 'new_v_mla_q_norm': 'new_v', 'new_v_mla_kv_norm': 'new_v', 'new_v_mla_w_uq': 'new_v', 'new_v_mla_w_ukv': 'new_v', 'new_v_mla_q_gain': 'new_v', 'new_v_mla_k_gain': 'new_v', 'new_v_mla_w_out': 'new_v', 'new_v_mlp_w1': 'new_v', 'new_v_mlp_w2': 'new_v'}


def _forward(args):
    return _fwd_reference(*[args[k] for k in FWD_PARAMS])


def _output_shape():
    out = _jax.eval_shape(lambda: _forward(_fwd_setup_inputs(0)))
    return out.shape, out.dtype

N_MICROBATCH = 1
ADAM_LR = 0.001
ADAM_B1 = 0.9
ADAM_B2 = 0.999
ADAM_EPS = 1e-08
ADAM_WD = 0.01
ADAM_STEP = 10
PER_EXAMPLE_BATCH_AXIS = {'x': 0, 'positions': 0, 'loss_target': 0}
SHARED_INPUTS = []
_WEIGHT_DTYPES = {'mix_norm': _jnp.float32, 'mlp_norm': _jnp.float32, 'sb_w_in': _jnp.float32, 'sb_w_out': _jnp.float32, 'fox_w_in': _jnp.float32, 'fox_b_f': _jnp.float32, 'fox_q_gain': _jnp.float32, 'fox_k_gain': _jnp.float32, 'fox_w_out': _jnp.float32, 'mla_w_in': _jnp.float32, 'mla_q_norm': _jnp.float32, 'mla_kv_norm': _jnp.float32, 'mla_w_uq': _jnp.float32, 'mla_w_ukv': _jnp.float32, 'mla_q_gain': _jnp.float32, 'mla_k_gain': _jnp.float32, 'mla_w_out': _jnp.float32, 'mlp_w1': _jnp.float32, 'mlp_w2': _jnp.float32}
MOMENT_SCALE = {'mix_norm': 6.176773e+00, 'mlp_norm': 2.567061e+01, 'sb_w_in': 2.228990e+00, 'sb_w_out': 3.538603e+00, 'fox_w_in': 3.349129e+00, 'fox_b_f': 2.932106e+01, 'fox_q_gain': 2.935149e+00, 'fox_k_gain': 2.929030e+00, 'fox_w_out': 5.932490e+00, 'mla_w_in': 7.551298e+00, 'mla_q_norm': 2.470593e-01, 'mla_kv_norm': 1.340645e+01, 'mla_w_uq': 1.429121e-01, 'mla_w_ukv': 3.997841e+00, 'mla_q_gain': 5.583010e-01, 'mla_k_gain': 5.565171e-01, 'mla_w_out': 5.455371e+00, 'mlp_w1': 2.519122e+00, 'mlp_w2': 9.492600e+00}


def _to_microbatches(a, axis):
    t = _jnp.moveaxis(a, axis, 0)
    t = t.reshape((N_MICROBATCH, t.shape[0] // N_MICROBATCH) + t.shape[1:])
    return _jnp.moveaxis(t, 1, axis + 1)


def setup_inputs(seed: int = 0) -> dict:
    inp = _fwd_setup_inputs(seed)
    key = _jax.random.fold_in(_jax.random.key(seed), 7919)
    shape, _ = _output_shape()
    out = dict(inp)
    out["loss_target"] = _jax.random.normal(_jax.random.fold_in(key, 0), shape, _jnp.float32)
    for i, name in enumerate(TWIN_WEIGHTS):
        w = inp[name].astype(_jnp.float32)
        if MOMENT_SCALE is None:
            s = _jnp.sqrt(_jnp.mean(_jnp.square(w)) + 1e-30)
        else:
            s = MOMENT_SCALE[name]
        km, kv = _jax.random.split(_jax.random.fold_in(key, i + 1))
        out[name] = w
        out["m_" + name] = s * _jax.random.normal(km, w.shape, _jnp.float32)
        out["v_" + name] = (s * s) * _jax.random.uniform(kv, w.shape, _jnp.float32, 0.5, 1.5)
    if N_MICROBATCH > 1:
        for name, axis in PER_EXAMPLE_BATCH_AXIS.items():
            out[name] = _to_microbatches(out[name], axis)
    return {'x': out['x'], 'positions': out['positions'], 'mix_norm': out['mix_norm'], 'mlp_norm': out['mlp_norm'], 'sb_w_in': out['sb_w_in'], 'sb_w_out': out['sb_w_out'], 'fox_w_in': out['fox_w_in'], 'fox_b_f': out['fox_b_f'], 'fox_q_gain': out['fox_q_gain'], 'fox_k_gain': out['fox_k_gain'], 'fox_w_out': out['fox_w_out'], 'mla_w_in': out['mla_w_in'], 'mla_q_norm': out['mla_q_norm'], 'mla_kv_norm': out['mla_kv_norm'], 'mla_w_uq': out['mla_w_uq'], 'mla_w_ukv': out['mla_w_ukv'], 'mla_q_gain': out['mla_q_gain'], 'mla_k_gain': out['mla_k_gain'], 'mla_w_out': out['mla_w_out'], 'mlp_w1': out['mlp_w1'], 'mlp_w2': out['mlp_w2'], 'loss_target': out['loss_target'], 'm_mix_norm': out['m_mix_norm'], 'm_mlp_norm': out['m_mlp_norm'], 'm_sb_w_in': out['m_sb_w_in'], 'm_sb_w_out': out['m_sb_w_out'], 'm_fox_w_in': out['m_fox_w_in'], 'm_fox_b_f': out['m_fox_b_f'], 'm_fox_q_gain': out['m_fox_q_gain'], 'm_fox_k_gain': out['m_fox_k_gain'], 'm_fox_w_out': out['m_fox_w_out'], 'm_mla_w_in': out['m_mla_w_in'], 'm_mla_q_norm': out['m_mla_q_norm'], 'm_mla_kv_norm': out['m_mla_kv_norm'], 'm_mla_w_uq': out['m_mla_w_uq'], 'm_mla_w_ukv': out['m_mla_w_ukv'], 'm_mla_q_gain': out['m_mla_q_gain'], 'm_mla_k_gain': out['m_mla_k_gain'], 'm_mla_w_out': out['m_mla_w_out'], 'm_mlp_w1': out['m_mlp_w1'], 'm_mlp_w2': out['m_mlp_w2'], 'v_mix_norm': out['v_mix_norm'], 'v_mlp_norm': out['v_mlp_norm'], 'v_sb_w_in': out['v_sb_w_in'], 'v_sb_w_out': out['v_sb_w_out'], 'v_fox_w_in': out['v_fox_w_in'], 'v_fox_b_f': out['v_fox_b_f'], 'v_fox_q_gain': out['v_fox_q_gain'], 'v_fox_k_gain': out['v_fox_k_gain'], 'v_fox_w_out': out['v_fox_w_out'], 'v_mla_w_in': out['v_mla_w_in'], 'v_mla_q_norm': out['v_mla_q_norm'], 'v_mla_kv_norm': out['v_mla_kv_norm'], 'v_mla_w_uq': out['v_mla_w_uq'], 'v_mla_w_ukv': out['v_mla_w_ukv'], 'v_mla_q_gain': out['v_mla_q_gain'], 'v_mla_k_gain': out['v_mla_k_gain'], 'v_mla_w_out': out['v_mla_w_out'], 'v_mlp_w1': out['v_mlp_w1'], 'v_mlp_w2': out['v_mlp_w2']}


def _loss(weights, diff, rest, loss_target):
    with _jax.named_scope("forward"):
        args = {**rest, TWIN_DIFF_INPUT: diff, **{k: w.astype(_WEIGHT_DTYPES[k]) for k, w in weights.items()}}
        y = _forward(args)
    with _jax.named_scope("loss_head"):
        err = _jnp.square(y.astype(_jnp.float32) - loss_target)
        return 0.5 * _jnp.sum(_jnp.mean(err, axis=-1)) if err.ndim else 0.5 * err


def _adamw(w, g, m, v):
    m = ADAM_B1 * m + (1.0 - ADAM_B1) * g
    v = ADAM_B2 * v + (1.0 - ADAM_B2) * _jnp.square(g)
    m_hat = m / (1.0 - ADAM_B1 ** ADAM_STEP)
    v_hat = v / (1.0 - ADAM_B2 ** ADAM_STEP)
    delta = -ADAM_LR * (m_hat / (_jnp.sqrt(v_hat) + ADAM_EPS) + ADAM_WD * w)
    return delta, m, v


def reference(x, positions, mix_norm, mlp_norm, sb_w_in, sb_w_out, fox_w_in, fox_b_f, fox_q_gain, fox_k_gain, fox_w_out, mla_w_in, mla_q_norm, mla_kv_norm, mla_w_uq, mla_w_ukv, mla_q_gain, mla_k_gain, mla_w_out, mlp_w1, mlp_w2, loss_target, m_mix_norm, m_mlp_norm, m_sb_w_in, m_sb_w_out, m_fox_w_in, m_fox_b_f, m_fox_q_gain, m_fox_k_gain, m_fox_w_out, m_mla_w_in, m_mla_q_norm, m_mla_kv_norm, m_mla_w_uq, m_mla_w_ukv, m_mla_q_gain, m_mla_k_gain, m_mla_w_out, m_mlp_w1, m_mlp_w2, v_mix_norm, v_mlp_norm, v_sb_w_in, v_sb_w_out, v_fox_w_in, v_fox_b_f, v_fox_q_gain, v_fox_k_gain, v_fox_w_out, v_mla_w_in, v_mla_q_norm, v_mla_kv_norm, v_mla_w_uq, v_mla_w_ukv, v_mla_q_gain, v_mla_k_gain, v_mla_w_out, v_mlp_w1, v_mlp_w2):
    given = dict(x=x, positions=positions, mix_norm=mix_norm, mlp_norm=mlp_norm, sb_w_in=sb_w_in, sb_w_out=sb_w_out, fox_w_in=fox_w_in, fox_b_f=fox_b_f, fox_q_gain=fox_q_gain, fox_k_gain=fox_k_gain, fox_w_out=fox_w_out, mla_w_in=mla_w_in, mla_q_norm=mla_q_norm, mla_kv_norm=mla_kv_norm, mla_w_uq=mla_w_uq, mla_w_ukv=mla_w_ukv, mla_q_gain=mla_q_gain, mla_k_gain=mla_k_gain, mla_w_out=mla_w_out, mlp_w1=mlp_w1, mlp_w2=mlp_w2, loss_target=loss_target, m_mix_norm=m_mix_norm, m_mlp_norm=m_mlp_norm, m_sb_w_in=m_sb_w_in, m_sb_w_out=m_sb_w_out, m_fox_w_in=m_fox_w_in, m_fox_b_f=m_fox_b_f, m_fox_q_gain=m_fox_q_gain, m_fox_k_gain=m_fox_k_gain, m_fox_w_out=m_fox_w_out, m_mla_w_in=m_mla_w_in, m_mla_q_norm=m_mla_q_norm, m_mla_kv_norm=m_mla_kv_norm, m_mla_w_uq=m_mla_w_uq, m_mla_w_ukv=m_mla_w_ukv, m_mla_q_gain=m_mla_q_gain, m_mla_k_gain=m_mla_k_gain, m_mla_w_out=m_mla_w_out, m_mlp_w1=m_mlp_w1, m_mlp_w2=m_mlp_w2, v_mix_norm=v_mix_norm, v_mlp_norm=v_mlp_norm, v_sb_w_in=v_sb_w_in, v_sb_w_out=v_sb_w_out, v_fox_w_in=v_fox_w_in, v_fox_b_f=v_fox_b_f, v_fox_q_gain=v_fox_q_gain, v_fox_k_gain=v_fox_k_gain, v_fox_w_out=v_fox_w_out, v_mla_w_in=v_mla_w_in, v_mla_q_norm=v_mla_q_norm, v_mla_kv_norm=v_mla_kv_norm, v_mla_w_uq=v_mla_w_uq, v_mla_w_ukv=v_mla_w_ukv, v_mla_q_gain=v_mla_q_gain, v_mla_k_gain=v_mla_k_gain, v_mla_w_out=v_mla_w_out, v_mlp_w1=v_mlp_w1, v_mlp_w2=v_mlp_w2)
    weights = {n: given[n] for n in TWIN_WEIGHTS}
    shared = {n: given[n] for n in SHARED_INPUTS}
    per_example = {n: given[n] for n in ['x', 'positions']}
    grad_fn = _jax.value_and_grad(_loss, argnums=(0, 1))

    def one_microbatch(ex, loss_target):
        ex = dict(ex)
        diff = ex.pop(TWIN_DIFF_INPUT)
        return grad_fn(weights, diff, {**shared, **ex}, loss_target)

    if N_MICROBATCH == 1:
        loss, (grad_w, grad_x) = one_microbatch(per_example, given["loss_target"])
    else:
        def body(carry, xs):
            loss_sum, grad_sum = carry
            l_k, (gw_k, gx_k) = one_microbatch(xs[0], xs[1])
            with _jax.named_scope("update"):
                return (loss_sum + l_k, _jax.tree.map(_jnp.add, grad_sum, gw_k)), gx_k

        init = (_jnp.zeros((), _jnp.float32), _jax.tree.map(_jnp.zeros_like, weights))
        (loss, grad_w), grad_x = _jax.lax.scan(body, init, (per_example, given["loss_target"]))
    with _jax.named_scope("update"):
        delta_w, new_m, new_v = {}, {}, {}
        for n in TWIN_WEIGHTS:
            delta_w[n], new_m[n], new_v[n] = _adamw(weights[n], grad_w[n], given["m_" + n], given["v_" + n])
    return (loss, grad_x, *[grad_w[n] for n in TWIN_WEIGHTS], *[delta_w[n] for n in TWIN_WEIGHTS],
            *[new_m[n] for n in TWIN_WEIGHTS], *[new_v[n] for n in TWIN_WEIGHTS])
```

```python
import functools
import math

import jax
import jax.numpy as jnp
from jax import lax
from jax.experimental import pallas as pl
from jax.experimental.pallas import tpu as pltpu

F32 = jnp.float32
BF16 = jnp.bfloat16

HEAD_DIM = 128
MLA_NOPE = 128
MLA_ROPE = 64
MLA_V = 128
MLA_QK = MLA_NOPE + MLA_ROPE
MLA_QK_PAD = 256
LANE = 128
ROPE_THETA = 10000.0
EPS = 1e-6
ADAM_LR = 0.001
ADAM_B1 = 0.9
ADAM_B2 = 0.999
ADAM_EPS = 1e-08
ADAM_WD = 0.01
ADAM_STEP = 10
N_DEV = 8
N_CHIP = 4
NEG = -1e30
VMEM_LIMIT = 56 * 1024 * 1024
ATT_BLOCK = 256
MESH = pl.DeviceIdType.MESH
ANY = pl.BlockSpec(memory_space=pl.ANY)


def _params(sem):
    return pltpu.CompilerParams(dimension_semantics=sem, vmem_limit_bytes=VMEM_LIMIT)


def _tile(dim, pref):
    if dim <= pref:
        return dim
    t = pref
    while dim % t:
        t -= LANE
    return t


def _dot(a, b, dims):
    return lax.dot_general(a, b, (dims, ((), ())), preferred_element_type=F32)


NN = ((1,), (0,))
NT = ((1,), (1,))
TN = ((0,), (0,))


def _mm(a, b, *, mode="nn", out_dtype=F32, res=None, act=None, z=None, name,
        tm=1024, tn=512, tk=512):
    if mode == "nn":
        (M, K), (_, N) = a.shape, b.shape
    elif mode == "nt":
        (M, K), (N, _) = a.shape, b.shape
    else:
        (K, M), (_, N) = a.shape, b.shape
    tm, tn, tk = _tile(M, tm), _tile(N, tn), _tile(K, tk)
    nk = K // tk
    if mode == "tn":
        a_spec = pl.BlockSpec((tk, tm), lambda i, j, k: (k, i))
    else:
        a_spec = pl.BlockSpec((tm, tk), lambda i, j, k: (i, k))
    if mode == "nt":
        b_spec = pl.BlockSpec((tn, tk), lambda i, j, k: (j, k))
    else:
        b_spec = pl.BlockSpec((tk, tn), lambda i, j, k: (k, j))
    dims = {"nn": NN, "nt": NT, "tn": TN}[mode]
    o_spec = pl.BlockSpec((tm, tn), lambda i, j, k: (i, j))
    in_specs, args = [a_spec, b_spec], [a, b]
    if res is not None:
        in_specs.append(o_spec)
        args.append(res)
    if act == "drelu2":
        in_specs.append(o_spec)
        args.append(z)
    if act == "relu2":
        out_shape = (jax.ShapeDtypeStruct((M, N), F32), jax.ShapeDtypeStruct((M, N), BF16))
        out_specs = (o_spec, o_spec)
    else:
        out_shape = jax.ShapeDtypeStruct((M, N), out_dtype)
        out_specs = o_spec
    has_res, has_z = res is not None, act == "drelu2"

    def body(*refs):
        a_ref, b_ref = refs[0], refs[1]
        idx = 2
        res_ref = z_ref = None
        if has_res:
            res_ref = refs[idx]
            idx += 1
        if has_z:
            z_ref = refs[idx]
            idx += 1
        outs = refs[idx:-1]
        acc = refs[-1]
        k = pl.program_id(2)

        @pl.when(k == 0)
        def _():
            acc[...] = jnp.zeros_like(acc)

        acc[...] += _dot(a_ref[...], b_ref[...], dims)

        @pl.when(k == nk - 1)
        def _():
            r = acc[...]
            if has_res:
                r = r + res_ref[...]
            if act == "relu2":
                outs[0][...] = r
                rr = jnp.maximum(r, 0.0)
                outs[1][...] = (rr * rr).astype(BF16)
            elif act == "drelu2":
                outs[0][...] = (r * (2.0 * jnp.maximum(z_ref[...], 0.0))).astype(out_dtype)
            else:
                outs[0][...] = r.astype(out_dtype)

    return pl.pallas_call(
        body, name=name, grid=(M // tm, N // tn, nk), in_specs=in_specs, out_specs=out_specs,
        out_shape=out_shape, scratch_shapes=[pltpu.VMEM((tm, tn), F32)],
        compiler_params=_params(("parallel", "parallel", "arbitrary")),
    )(*args)


def _rmsnorm_fwd(x, g, *, name, tm=256):
    T, n = x.shape
    tm = _tile(T, tm)

    def body(x_ref, g_ref, o_ref):
        xf = x_ref[...]
        r = lax.rsqrt(jnp.mean(xf * xf, axis=-1, keepdims=True) + EPS)
        o_ref[...] = (xf * r * g_ref[...]).astype(BF16)

    return pl.pallas_call(
        body, name=name, grid=(T // tm,),
        in_specs=[pl.BlockSpec((tm, n), lambda i: (i, 0)), pl.BlockSpec((1, n), lambda i: (0, 0))],
        out_specs=pl.BlockSpec((tm, n), lambda i: (i, 0)),
        out_shape=jax.ShapeDtypeStruct((T, n), BF16),
        compiler_params=_params(("parallel",)),
    )(x, g)


def _rmsnorm_bwd(x, g, dy, dx_in=None, *, name, want_f32=True, tm=256):
    T, n = x.shape
    tm = _tile(T, tm)
    has_in = dx_in is not None
    row = pl.BlockSpec((tm, n), lambda i: (i, 0))
    vec = pl.BlockSpec((1, n), lambda i: (0, 0))

    def body(*refs):
        x_ref, g_ref, dy_ref = refs[:3]
        idx = 3
        in_ref = None
        if has_in:
            in_ref = refs[idx]
            idx += 1
        outs = refs[idx:]
        xf = x_ref[...]
        r = lax.rsqrt(jnp.mean(xf * xf, axis=-1, keepdims=True) + EPS)
        dyf = dy_ref[...]
        t = dyf * g_ref[...]
        dx = r * t - xf * (r * r * r * jnp.mean(t * xf, axis=-1, keepdims=True))
        if has_in:
            dx = dx + in_ref[...]
        o = 0
        if want_f32:
            outs[0][...] = dx
            o = 1
        outs[o][...] = dx.astype(BF16)
        dg_ref = outs[o + 1]

        @pl.when(pl.program_id(0) == 0)
        def _():
            dg_ref[...] = jnp.zeros_like(dg_ref)

        dg_ref[...] += jnp.sum(dyf * xf * r, axis=0, keepdims=True)

    in_specs = [row, vec, row] + ([row] if has_in else [])
    out_specs, out_shape = [], []
    if want_f32:
        out_specs.append(row)
        out_shape.append(jax.ShapeDtypeStruct((T, n), F32))
    out_specs += [row, vec]
    out_shape += [jax.ShapeDtypeStruct((T, n), BF16), jax.ShapeDtypeStruct((1, n), F32)]
    args = [x, g, dy] + ([dx_in] if has_in else [])
    return pl.pallas_call(
        body, name=name, grid=(T // tm,), in_specs=in_specs, out_specs=tuple(out_specs),
        out_shape=tuple(out_shape), compiler_params=_params(("arbitrary",)),
    )(*args)


def _loss_head(y, target, *, tm=256):
    T, n = y.shape
    tm = _tile(T, tm)
    row = pl.BlockSpec((tm, n), lambda i: (i, 0))

    def body(y_ref, t_ref, dy_ref, dyb_ref, loss_ref):
        err = y_ref[...] - t_ref[...]
        dy = err * (1.0 / n)
        dy_ref[...] = dy
        dyb_ref[...] = dy.astype(BF16)

        @pl.when(pl.program_id(0) == 0)
        def _():
            loss_ref[...] = jnp.zeros_like(loss_ref)

        part = 0.5 * jnp.sum(jnp.mean(err * err, axis=-1, keepdims=True), axis=0, keepdims=True)
        loss_ref[...] += part

    return pl.pallas_call(
        body, name="loss_head", grid=(T // tm,), in_specs=[row, row],
        out_specs=(row, row, pl.BlockSpec((1, 1), lambda i: (0, 0))),
        out_shape=(jax.ShapeDtypeStruct((T, n), F32), jax.ShapeDtypeStruct((T, n), BF16),
                   jax.ShapeDtypeStruct((1, 1), F32)),
        compiler_params=_params(("arbitrary",)),
    )(y, target)


def _split3(v):
    hi = v.astype(BF16)
    r1 = v - hi.astype(F32)
    mid = r1.astype(BF16)
    lo = (r1 - mid.astype(F32)).astype(BF16)
    return hi, mid, lo


def _tri_right(v, tri):
    hi, mid, lo = _split3(v)
    return _dot(hi, tri, NN) + _dot(mid, tri, NN) + _dot(lo, tri, NN)


def _tri_left(tri, v):
    hi, mid, lo = _split3(v)
    return _dot(tri, hi, NN) + _dot(tri, mid, NN) + _dot(tri, lo, NN)


def _iota2(shape, dim):
    return lax.broadcasted_iota(jnp.int32, shape, dim)


def _log_sigmoid(z):
    return jnp.minimum(z, 0.0) - jnp.log1p(jnp.exp(-jnp.abs(z)))


def _head_spec(rows, width, off):
    return pl.BlockSpec((rows, width), lambda h, i: (i, off + h))


def _full_head_spec(T, width, off):
    return pl.BlockSpec((T, width), lambda h, i: (0, off + h))


def _sb_fwd(qkv, H):
    T = qkv.shape[0]
    B = _tile(T, ATT_BLOCK)
    nq = T // B
    scale = 1.0 / math.sqrt(HEAD_DIM)

    def body(q_ref, k_ref, v_ref, o_ref, c_ref):
        i = pl.program_id(1)
        q = q_ref[...]
        rows = _iota2((B, B), 0)
        cols = _iota2((B, B), 1)
        tri = (rows > cols).astype(BF16)

        def step(n, carry):
            acc, run = carry
            j = i - n
            ks = pl.multiple_of(j * B, B)
            kb = k_ref[pl.ds(ks, B), :]
            vb = v_ref[pl.ds(ks, B), :]
            z = _dot(q, kb, NT) * scale
            mask = (ks + cols) < (i * B + rows)
            ls = _log_sigmoid(z)
            lk = jnp.where(mask, ls - z, 0.0)
            later = _tri_right(lk, tri)
            a = jnp.where(mask, jnp.exp(ls + later + run), 0.0)
            acc = acc + _dot(a.astype(BF16), vb, NN)
            run = run + jnp.sum(lk, axis=1, keepdims=True)
            return acc, run

        acc, run = lax.fori_loop(
            0, i + 1, step, (jnp.zeros((B, HEAD_DIM), F32), jnp.zeros((B, 1), F32)))
        o_ref[...] = acc.astype(BF16)
        c_ref[...] = run

    return pl.pallas_call(
        body, name="sb_attn_fwd", grid=(H, nq),
        in_specs=[_head_spec(B, HEAD_DIM, 0), _full_head_spec(T, HEAD_DIM, H),
                  _full_head_spec(T, HEAD_DIM, 2 * H)],
        out_specs=(_head_spec(B, HEAD_DIM, 0), pl.BlockSpec((None, B, 1), lambda h, i: (h, i, 0))),
        out_shape=(jax.ShapeDtypeStruct((T, H * HEAD_DIM), BF16), jax.ShapeDtypeStruct((H, T, 1), F32)),
        compiler_params=_params(("parallel", "arbitrary")),
    )(qkv, qkv, qkv)


def _sb_bwd(qkv, do, ctot, H):
    T = qkv.shape[0]
    B = _tile(T, ATT_BLOCK)
    nq = T // B
    scale = 1.0 / math.sqrt(HEAD_DIM)

    def body(q_ref, k_ref, v_ref, do_ref, c_ref, dq_ref, dk_ref, dv_ref, dk_acc, dv_acc):
        i = pl.program_id(1)

        @pl.when(i == 0)
        def _():
            dk_acc[...] = jnp.zeros_like(dk_acc)
            dv_acc[...] = jnp.zeros_like(dv_acc)

        q = q_ref[...]
        do_b = do_ref[...]
        ctot_b = c_ref[...]
        rows = _iota2((B, B), 0)
        cols = _iota2((B, B), 1)
        tri_incl = (rows <= cols).astype(BF16)
        tri_strict = (rows < cols).astype(BF16)

        def step(j, carry):
            dq, lpre, gpre = carry
            ks = pl.multiple_of(j * B, B)
            kb = k_ref[pl.ds(ks, B), :]
            vb = v_ref[pl.ds(ks, B), :]
            z = _dot(q, kb, NT) * scale
            mask = (ks + cols) < (i * B + rows)
            ls = _log_sigmoid(z)
            lk_all = ls - z
            lk = jnp.where(mask, lk_all, 0.0)
            later = ctot_b - lpre - _tri_right(lk, tri_incl)
            a = jnp.where(mask, jnp.exp(ls + later), 0.0)
            da = _dot(do_b, vb, NT)
            g = a * da
            gex = gpre + _tri_right(g, tri_strict)
            dz = jnp.where(mask, g * jnp.exp(lk_all) - jnp.exp(ls) * gex, 0.0) * scale
            dzb = dz.astype(BF16)
            dq = dq + _dot(dzb, kb, NN)
            dk_acc[pl.ds(ks, B), :] += _dot(dzb, q, TN)
            dv_acc[pl.ds(ks, B), :] += _dot(a.astype(BF16), do_b, TN)
            lpre = lpre + jnp.sum(lk, axis=1, keepdims=True)
            gpre = gpre + jnp.sum(g, axis=1, keepdims=True)
            return dq, lpre, gpre

        dq, _, _ = lax.fori_loop(
            0, i + 1, step,
            (jnp.zeros((B, HEAD_DIM), F32), jnp.zeros((B, 1), F32), jnp.zeros((B, 1), F32)))
        dq_ref[...] = dq.astype(BF16)

        @pl.when(i == nq - 1)
        def _():
            dk_ref[...] = dk_acc[...].astype(BF16)
            dv_ref[...] = dv_acc[...].astype(BF16)

    W = H * HEAD_DIM
    return pl.pallas_call(
        body, name="sb_attn_bwd", grid=(H, nq),
        in_specs=[_head_spec(B, HEAD_DIM, 0), _full_head_spec(T, HEAD_DIM, H),
                  _full_head_spec(T, HEAD_DIM, 2 * H), _head_spec(B, HEAD_DIM, 0),
                  pl.BlockSpec((None, B, 1), lambda h, i: (h, i, 0))],
        out_specs=(_head_spec(B, HEAD_DIM, 0), _full_head_spec(T, HEAD_DIM, 0),
                   _full_head_spec(T, HEAD_DIM, 0)),
        out_shape=tuple(jax.ShapeDtypeStruct((T, W), BF16) for _ in range(3)),
        scratch_shapes=[pltpu.VMEM((T, HEAD_DIM), F32), pltpu.VMEM((T, HEAD_DIM), F32)],
        compiler_params=_params(("parallel", "arbitrary")),
    )(qkv, qkv, qkv, do, ctot)


def _softmax_fwd(q, k, v, cf, H, dqk, scale, *, name):
    T = q.shape[0]
    B = _tile(T, ATT_BLOCK)
    nq = T // B
    has_cf = cf is not None

    def body(*refs):
        q_ref, k_ref, v_ref = refs[:3]
        idx = 3
        if has_cf:
            cfc_ref, cfr_ref = refs[3], refs[4]
            idx = 5
        o_ref, of_ref, lse_ref = refs[idx:idx + 3]
        i = pl.program_id(1)
        qb = q_ref[...]
        rows = _iota2((B, B), 0)
        cols = _iota2((B, B), 1)

        def step(j, carry):
            m, l, acc = carry
            ks = pl.multiple_of(j * B, B)
            kb = k_ref[pl.ds(ks, B), :]
            vb = v_ref[pl.ds(ks, B), :]
            s = _dot(qb, kb, NT) * scale
            if has_cf:
                s = s + (cfc_ref[...] - cfr_ref[:, pl.ds(ks, B)])
            s = jnp.where((ks + cols) <= (i * B + rows), s, NEG)
            m_new = jnp.maximum(m, jnp.max(s, axis=1, keepdims=True))
            alpha = jnp.exp(m - m_new)
            p = jnp.exp(s - m_new)
            l = alpha * l + jnp.sum(p, axis=1, keepdims=True)
            acc = alpha * acc + _dot(p.astype(BF16), vb, NN)
            return m_new, l, acc

        m, l, acc = lax.fori_loop(
            0, i + 1, step,
            (jnp.full((B, 1), NEG, F32), jnp.zeros((B, 1), F32), jnp.zeros((B, HEAD_DIM), F32)))
        o = acc / l
        o_ref[...] = o.astype(BF16)
        of_ref[...] = o
        lse_ref[...] = m + jnp.log(l)

    stat = pl.BlockSpec((None, B, 1), lambda h, i: (h, i, 0))
    in_specs = [_head_spec(B, dqk, 0), _full_head_spec(T, dqk, 0), _full_head_spec(T, HEAD_DIM, 0)]
    args = [q, k, v]
    if has_cf:
        in_specs += [stat, pl.BlockSpec((None, 1, T), lambda h, i: (h, 0, 0))]
        args += list(cf)
    W = H * HEAD_DIM
    return pl.pallas_call(
        body, name=name, grid=(H, nq), in_specs=in_specs,
        out_specs=(_head_spec(B, HEAD_DIM, 0), _head_spec(B, HEAD_DIM, 0), stat),
        out_shape=(jax.ShapeDtypeStruct((T, W), BF16), jax.ShapeDtypeStruct((T, W), F32),
                   jax.ShapeDtypeStruct((H, T, 1), F32)),
        compiler_params=_params(("parallel", "arbitrary")),
    )(*args)


def _softmax_bwd(q, k, v, cf, do, o, lse, H, dqk, scale, *, name):
    T = q.shape[0]
    B = _tile(T, ATT_BLOCK)
    nq = T // B
    has_cf = cf is not None

    def body(*refs):
        q_ref, k_ref, v_ref, do_ref, o_ref, lse_ref = refs[:6]
        idx = 6
        if has_cf:
            cfc_ref, cfr_ref = refs[6], refs[7]
            idx = 8
        dq_ref, dk_ref, dv_ref = refs[idx:idx + 3]
        idx += 3
        if has_cf:
            dcc_ref, dcr_ref = refs[idx], refs[idx + 1]
        i = pl.program_id(1)

        @pl.when(i == 0)
        def _():
            dk_ref[...] = jnp.zeros_like(dk_ref)
            dv_ref[...] = jnp.zeros_like(dv_ref)
            if has_cf:
                dcr_ref[...] = jnp.zeros_like(dcr_ref)

        qb = q_ref[...]
        do_b = do_ref[...]
        lse_b = lse_ref[...]
        delta = jnp.sum(do_b.astype(F32) * o_ref[...], axis=1, keepdims=True)
        rows = _iota2((B, B), 0)
        cols = _iota2((B, B), 1)

        def step(j, carry):
            dq, rs = carry
            ks = pl.multiple_of(j * B, B)
            kb = k_ref[pl.ds(ks, B), :]
            vb = v_ref[pl.ds(ks, B), :]
            s = _dot(qb, kb, NT) * scale
            if has_cf:
                s = s + (cfc_ref[...] - cfr_ref[:, pl.ds(ks, B)])
            p = jnp.where((ks + cols) <= (i * B + rows), jnp.exp(s - lse_b), 0.0)
            dp = _dot(do_b, vb, NT)
            ds = p * (dp - delta)
            dsb = (ds * scale).astype(BF16)
            dq = dq + _dot(dsb, kb, NN)
            dk_ref[pl.ds(ks, B), :] += _dot(dsb, qb, TN)
            dv_ref[pl.ds(ks, B), :] += _dot(p.astype(BF16), do_b, TN)
            if has_cf:
                rs = rs + jnp.sum(ds, axis=1, keepdims=True)
                dcr_ref[:, pl.ds(ks, B)] -= jnp.sum(ds, axis=0, keepdims=True)
            return dq, rs

        dq, rs = lax.fori_loop(0, i + 1, step, (jnp.zeros((B, dqk), F32), jnp.zeros((B, 1), F32)))
        dq_ref[...] = dq
        if has_cf:
            dcc_ref[...] = rs

    stat = pl.BlockSpec((None, B, 1), lambda h, i: (h, i, 0))
    rowstat = pl.BlockSpec((None, 1, T), lambda h, i: (h, 0, 0))
    in_specs = [_head_spec(B, dqk, 0), _full_head_spec(T, dqk, 0), _full_head_spec(T, HEAD_DIM, 0),
                _head_spec(B, HEAD_DIM, 0), _head_spec(B, HEAD_DIM, 0), stat]
    args = [q, k, v, do, o, lse]
    out_specs = [_head_spec(B, dqk, 0), _full_head_spec(T, dqk, 0), _full_head_spec(T, HEAD_DIM, 0)]
    out_shape = [jax.ShapeDtypeStruct((T, H * dqk), F32), jax.ShapeDtypeStruct((T, H * dqk), F32),
                 jax.ShapeDtypeStruct((T, H * HEAD_DIM), F32)]
    if has_cf:
        in_specs += [stat, rowstat]
        args += list(cf)
        out_specs += [stat, rowstat]
        out_shape += [jax.ShapeDtypeStruct((H, T, 1), F32), jax.ShapeDtypeStruct((H, 1, T), F32)]
    return pl.pallas_call(
        body, name=name, grid=(H, nq), in_specs=in_specs, out_specs=tuple(out_specs),
        out_shape=tuple(out_shape), compiler_params=_params(("parallel", "arbitrary")),
    )(*args)


def _headnorm(x, g):
    r = lax.rsqrt(jnp.mean(x * x, axis=-1, keepdims=True) + EPS)
    return x * r * g, r


def _headnorm_bwd(x, g, dy, n):
    r = lax.rsqrt(jnp.sum(x * x, axis=-1, keepdims=True) * (1.0 / n) + EPS)
    t = dy * g
    dx = r * t - x * (r * r * r * (jnp.sum(t * x, axis=-1, keepdims=True) * (1.0 / n)))
    dg = jnp.sum(dy * x * r, axis=0, keepdims=True)
    return dx, dg


def _fox_prep_fwd(qkv, gq, gk, H, *, tm=512):
    T = qkv.shape[0]
    tm = _tile(T, tm)

    def body(q_ref, k_ref, v_ref, gq_ref, gk_ref, qn_ref, kn_ref, vb_ref):
        qn_ref[...] = _headnorm(q_ref[...], gq_ref[...])[0].astype(BF16)
        kn_ref[...] = _headnorm(k_ref[...], gk_ref[...])[0].astype(BF16)
        vb_ref[...] = v_ref[...].astype(BF16)

    def blk(off):
        return pl.BlockSpec((tm, HEAD_DIM), lambda i, h: (i, off + h))

    vec = pl.BlockSpec((1, HEAD_DIM), lambda i, h: (0, 0))
    W = H * HEAD_DIM
    return pl.pallas_call(
        body, name="fox_prep_fwd", grid=(T // tm, H),
        in_specs=[blk(0), blk(H), blk(2 * H), vec, vec], out_specs=(blk(0), blk(0), blk(0)),
        out_shape=tuple(jax.ShapeDtypeStruct((T, W), BF16) for _ in range(3)),
        compiler_params=_params(("parallel", "parallel")),
    )(qkv, qkv, qkv, gq, gk)


def _fox_prep_bwd(qkv, gq, gk, dqn, dkn, dv, H, *, tm=512):
    T = qkv.shape[0]
    tm = _tile(T, tm)

    def body(q_ref, k_ref, gq_ref, gk_ref, dqn_ref, dkn_ref, dv_ref,
             dq_ref, dk_ref, dvb_ref, dgq_ref, dgk_ref):
        @pl.when((pl.program_id(0) == 0) & (pl.program_id(1) == 0))
        def _():
            dgq_ref[...] = jnp.zeros_like(dgq_ref)
            dgk_ref[...] = jnp.zeros_like(dgk_ref)

        dq, dgq = _headnorm_bwd(q_ref[...], gq_ref[...], dqn_ref[...], HEAD_DIM)
        dk, dgk = _headnorm_bwd(k_ref[...], gk_ref[...], dkn_ref[...], HEAD_DIM)
        dq_ref[...] = dq.astype(BF16)
        dk_ref[...] = dk.astype(BF16)
        dvb_ref[...] = dv_ref[...].astype(BF16)
        dgq_ref[...] += dgq
        dgk_ref[...] += dgk

    def blk(off):
        return pl.BlockSpec((tm, HEAD_DIM), lambda i, h: (i, off + h))

    vec = pl.BlockSpec((1, HEAD_DIM), lambda i, h: (0, 0))
    W = H * HEAD_DIM
    return pl.pallas_call(
        body, name="fox_prep_bwd", grid=(T // tm, H),
        in_specs=[blk(0), blk(H), vec, vec, blk(0), blk(0), blk(0)],
        out_specs=(blk(0), blk(0), blk(0), vec, vec),
        out_shape=tuple(jax.ShapeDtypeStruct((T, W), BF16) for _ in range(3))
        + (jax.ShapeDtypeStruct((1, HEAD_DIM), F32), jax.ShapeDtypeStruct((1, HEAD_DIM), F32)),
        compiler_params=_params(("arbitrary", "arbitrary")),
    )(qkv, qkv, gq, gk, dqn, dkn, dv)


def _fox_gate_fwd(flog, bf, *, tm=256):
    T = flog.shape[0]
    tm = _tile(T, tm)

    def body(f_ref, b_ref, cf_ref, carry):
        @pl.when(pl.program_id(0) == 0)
        def _():
            carry[...] = jnp.zeros_like(carry)

        lf = _log_sigmoid(f_ref[...] + b_ref[...])
        tri = (_iota2((tm, tm), 1) <= _iota2((tm, tm), 0)).astype(BF16)
        cf_ref[...] = carry[...] + _tri_left(tri, lf)
        carry[...] += jnp.sum(lf, axis=0, keepdims=True)

    return pl.pallas_call(
        body, name="fox_gate_fwd", grid=(T // tm,),
        in_specs=[pl.BlockSpec((tm, LANE), lambda i: (i, 0)), pl.BlockSpec((1, LANE), lambda i: (0, 0))],
        out_specs=pl.BlockSpec((tm, LANE), lambda i: (i, 0)),
        out_shape=jax.ShapeDtypeStruct((T, LANE), F32),
        scratch_shapes=[pltpu.VMEM((1, LANE), F32)],
        compiler_params=_params(("arbitrary",)),
    )(flog, bf)


def _fox_gate_bwd(flog, bf, dcf, *, tm=256):
    T = flog.shape[0]
    tm = _tile(T, tm)
    nt = T // tm

    def body(f_ref, b_ref, dcf_ref, df_ref, db_ref, carry):
        @pl.when(pl.program_id(0) == 0)
        def _():
            carry[...] = jnp.zeros_like(carry)
            db_ref[...] = jnp.zeros_like(db_ref)

        d = dcf_ref[...]
        tri = (_iota2((tm, tm), 1) >= _iota2((tm, tm), 0)).astype(BF16)
        dlf = carry[...] + _tri_left(tri, d)
        carry[...] += jnp.sum(d, axis=0, keepdims=True)
        xg = f_ref[...] + b_ref[...]
        e = jnp.exp(-jnp.abs(xg))
        sig_neg = jnp.where(xg >= 0.0, e, 1.0) / (1.0 + e)
        df = dlf * sig_neg
        df_ref[...] = df.astype(BF16)
        db_ref[...] += jnp.sum(df, axis=0, keepdims=True)

    rev = pl.BlockSpec((tm, LANE), lambda i: (nt - 1 - i, 0))
    vec = pl.BlockSpec((1, LANE), lambda i: (0, 0))
    return pl.pallas_call(
        body, name="fox_gate_bwd", grid=(nt,), in_specs=[rev, vec, rev], out_specs=(rev, vec),
        out_shape=(jax.ShapeDtypeStruct((T, LANE), BF16), jax.ShapeDtypeStruct((1, LANE), F32)),
        scratch_shapes=[pltpu.VMEM((1, LANE), F32)],
        compiler_params=_params(("arbitrary",)),
    )(flog, bf, dcf)


def _rope_tables(positions):
    half = MLA_ROPE // 2
    inv_freq = ROPE_THETA ** (-jnp.arange(0, half, dtype=F32) * 2.0 / MLA_ROPE)
    ang = positions.astype(F32)[:, None] * inv_freq
    cos, sin = jnp.cos(ang), jnp.sin(ang)
    zero = jnp.zeros_like(cos)
    pad = jnp.zeros((positions.shape[0], LANE - MLA_ROPE), F32)
    cos_t = jnp.concatenate([cos, cos, pad], axis=1)
    sin_up = jnp.concatenate([zero, sin, pad], axis=1)
    sin_dn = jnp.concatenate([-sin, zero, pad], axis=1)
    return cos_t, sin_up, sin_dn


def _rope(x, cos_t, sin_up, sin_dn):
    half = MLA_ROPE // 2
    return x * cos_t + pltpu.roll(x, half, 1) * sin_up + pltpu.roll(x, LANE - half, 1) * sin_dn


def _rope_t(d, cos_t, sin_up, sin_dn):
    half = MLA_ROPE // 2
    return d * cos_t + pltpu.roll(d * sin_up, LANE - half, 1) + pltpu.roll(d * sin_dn, half, 1)


def _norm192(xcat, g):
    r = lax.rsqrt(jnp.sum(xcat * xcat, axis=-1, keepdims=True) * (1.0 / MLA_QK) + EPS)
    return xcat * r * g


def _mla_prep_fwd(qfull, kv, kr, tabs, gq, gk, H, *, tm=512):
    T = qfull.shape[0]
    tm = _tile(T, tm)

    def body(q_ref, kv_ref, kr_ref, c_ref, su_ref, sd_ref, gq_ref, gk_ref, qf_ref, kf_ref, v_ref):
        tabs_b = (c_ref[...], su_ref[...], sd_ref[...])
        qb = q_ref[...]
        qcat = jnp.concatenate([qb[:, :MLA_NOPE], _rope(qb[:, MLA_NOPE:], *tabs_b)], axis=1)
        qf_ref[...] = _norm192(qcat, gq_ref[...]).astype(BF16)
        kvb = kv_ref[...]
        kcat = jnp.concatenate([kvb[:, :MLA_NOPE], _rope(kr_ref[...], *tabs_b)], axis=1)
        kf_ref[...] = _norm192(kcat, gk_ref[...]).astype(BF16)
        v_ref[...] = kvb[:, MLA_NOPE:].astype(BF16)

    head = pl.BlockSpec((tm, MLA_QK_PAD), lambda i, h: (i, h))
    tok = pl.BlockSpec((tm, LANE), lambda i, h: (i, 0))
    vec = pl.BlockSpec((1, MLA_QK_PAD), lambda i, h: (0, 0))
    return pl.pallas_call(
        body, name="mla_prep_fwd", grid=(T // tm, H),
        in_specs=[head, head, tok, tok, tok, tok, vec, vec],
        out_specs=(head, head, pl.BlockSpec((tm, MLA_V), lambda i, h: (i, h))),
        out_shape=(jax.ShapeDtypeStruct((T, H * MLA_QK_PAD), BF16),
                   jax.ShapeDtypeStruct((T, H * MLA_QK_PAD), BF16),
                   jax.ShapeDtypeStruct((T, H * MLA_V), BF16)),
        compiler_params=_params(("parallel", "parallel")),
    )(qfull, kv, kr, *tabs, gq, gk)


def _mla_prep_bwd(qfull, kv, kr, tabs, gq, gk, dqf, dkf, dv, H, *, tm=512):
    T = qfull.shape[0]
    tm = _tile(T, tm)

    def body(q_ref, kv_ref, kr_ref, c_ref, su_ref, sd_ref, gq_ref, gk_ref, dqf_ref, dkf_ref, dv_ref,
             dq_ref, dkv_ref, dkr_ref, dgq_ref, dgk_ref, kr_acc):
        h = pl.program_id(1)

        @pl.when((pl.program_id(0) == 0) & (h == 0))
        def _():
            dgq_ref[...] = jnp.zeros_like(dgq_ref)
            dgk_ref[...] = jnp.zeros_like(dgk_ref)

        @pl.when(h == 0)
        def _():
            kr_acc[...] = jnp.zeros_like(kr_acc)

        tabs_b = (c_ref[...], su_ref[...], sd_ref[...])
        qb = q_ref[...]
        qcat = jnp.concatenate([qb[:, :MLA_NOPE], _rope(qb[:, MLA_NOPE:], *tabs_b)], axis=1)
        dqcat, dgq = _headnorm_bwd(qcat, gq_ref[...], dqf_ref[...], MLA_QK)
        dq_ref[...] = jnp.concatenate(
            [dqcat[:, :MLA_NOPE], _rope_t(dqcat[:, MLA_NOPE:], *tabs_b)], axis=1).astype(BF16)
        dgq_ref[...] += dgq
        kvb = kv_ref[...]
        kcat = jnp.concatenate([kvb[:, :MLA_NOPE], _rope(kr_ref[...], *tabs_b)], axis=1)
        dkcat, dgk = _headnorm_bwd(kcat, gk_ref[...], dkf_ref[...], MLA_QK)
        dkv_ref[...] = jnp.concatenate([dkcat[:, :MLA_NOPE], dv_ref[...]], axis=1).astype(BF16)
        dgk_ref[...] += dgk
        kr_acc[...] += dkcat[:, MLA_NOPE:]

        @pl.when(h == H - 1)
        def _():
            dkr_ref[...] = _rope_t(kr_acc[...], *tabs_b).astype(BF16)

    head = pl.BlockSpec((tm, MLA_QK_PAD), lambda i, h: (i, h))
    tok = pl.BlockSpec((tm, LANE), lambda i, h: (i, 0))
    vec = pl.BlockSpec((1, MLA_QK_PAD), lambda i, h: (0, 0))
    return pl.pallas_call(
        body, name="mla_prep_bwd", grid=(T // tm, H),
        in_specs=[head, head, tok, tok, tok, tok, vec, vec, head, head,
                  pl.BlockSpec((tm, MLA_V), lambda i, h: (i, h))],
        out_specs=(head, head, tok, vec, vec),
        out_shape=(jax.ShapeDtypeStruct((T, H * MLA_QK_PAD), BF16),
                   jax.ShapeDtypeStruct((T, H * MLA_QK_PAD), BF16),
                   jax.ShapeDtypeStruct((T, LANE), BF16),
                   jax.ShapeDtypeStruct((1, MLA_QK_PAD), F32), jax.ShapeDtypeStruct((1, MLA_QK_PAD), F32)),
        scratch_shapes=[pltpu.VMEM((tm, LANE), F32)],
        compiler_params=_params(("arbitrary", "arbitrary")),
    )(qfull, kv, kr, *tabs, gq, gk, dqf, dkf, dv)


def _cf_layouts(cf, H):
    cfh = cf[:, :H].T
    return cfh[:, :, None], cfh[:, None, :]


def _layer_fwd(i, x, P, tabs, H):
    kind, j = i % 3, i // 3
    s = {"x_in": x}
    h = _rmsnorm_fwd(x, P["mix_norm"][i:i + 1], name="mix_norm_fwd")
    s["h"] = h
    if kind == 0:
        w = P["sb"][j]
        qkv = _mm(h, w["w_in"], out_dtype=BF16, name="sb_qkv")
        o, ctot = _sb_fwd(qkv, H)
        s.update(qkv=qkv, o=o, ctot=ctot)
    elif kind == 1:
        w = P["fox"][j]
        qkv = _mm(h, w["w_qkv"], name="fox_qkv")
        flog = _mm(h, w["w_f"], name="fox_gate_proj")
        qn, kn, vb = _fox_prep_fwd(qkv, w["gq"], w["gk"], H)
        cf = _cf_layouts(_fox_gate_fwd(flog, w["bf"]), H)
        o, of, lse = _softmax_fwd(qn, kn, vb, cf, H, HEAD_DIM, 1.0 / math.sqrt(HEAD_DIM), name="fox_attn_fwd")
        s.update(qkv=qkv, flog=flog, qn=qn, kn=kn, vb=vb, cf=cf, o=o, of=of, lse=lse)
    else:
        w = P["mla"][j]
        dq = _mm(h, w["w_dq"], name="mla_down_q")
        dkv = _mm(h, w["w_dkv"], name="mla_down_kv")
        kr = _mm(h, w["w_dr"], name="mla_down_rope")
        cq = _rmsnorm_fwd(dq, w["q_norm"], name="mla_q_norm_fwd")
        ckv = _rmsnorm_fwd(dkv, w["kv_norm"], name="mla_kv_norm_fwd")
        qfull = _mm(cq, w["w_uq"], name="mla_up_q")
        kv = _mm(ckv, w["w_ukv"], name="mla_up_kv")
        qf, kf, v = _mla_prep_fwd(qfull, kv, kr, tabs, w["gq"], w["gk"], H)
        o, of, lse = _softmax_fwd(qf, kf, v, None, H, MLA_QK_PAD, 1.0 / math.sqrt(MLA_QK), name="mla_attn_fwd")
        s.update(dq=dq, dkv=dkv, kr=kr, cq=cq, ckv=ckv, qfull=qfull, kv=kv, qf=qf, kf=kf, v=v,
                 o=o, of=of, lse=lse)
    x = _mm(s["o"], w["w_out"], res=x, name="mixer_out")
    s["x_mid"] = x
    h2 = _rmsnorm_fwd(x, P["mlp_norm"][i:i + 1], name="mlp_norm_fwd")
    z, u = _mm(h2, P["mlp"][i]["w1"], act="relu2", name="mlp_up")
    x = _mm(u, P["mlp"][i]["w2"], res=x, name="mlp_down")
    s.update(h2=h2, z=z, u=u)
    return x, s


def _layer_bwd(i, dx, dxb, s, P, tabs, H):
    kind, j = i % 3, i // 3
    G = {}
    wm = P["mlp"][i]
    dz = _mm(dxb, wm["w2"], mode="nt", act="drelu2", z=s["z"], out_dtype=BF16, name="mlp_down_bwd")
    G["w2"] = _mm(s["u"], dxb, mode="tn", out_dtype=BF16, name="mlp_w2_grad")
    G["w1"] = _mm(s["h2"], dz, mode="tn", out_dtype=BF16, name="mlp_w1_grad")
    dh2 = _mm(dz, wm["w1"], mode="nt", name="mlp_up_bwd")
    dx, dxb, G["mlp_norm"] = _rmsnorm_bwd(s["x_mid"], P["mlp_norm"][i:i + 1], dh2, dx, name="mlp_norm_bwd")
    h = s["h"]
    if kind == 0:
        w = P["sb"][j]
        do = _mm(dxb, w["w_out"], mode="nt", out_dtype=BF16, name="mixer_out_bwd")
        G["w_out"] = _mm(s["o"], dxb, mode="tn", out_dtype=BF16, name="mixer_out_grad")
        dq, dk, dv = _sb_bwd(s["qkv"], do, s["ctot"], H)
        dproj = jnp.concatenate([dq, dk, dv], axis=1)
        G["w_in"] = _mm(h, dproj, mode="tn", out_dtype=BF16, name="sb_qkv_grad")
        dh = _mm(dproj, w["w_in"], mode="nt", name="sb_qkv_bwd")
    elif kind == 1:
        w = P["fox"][j]
        do = _mm(dxb, w["w_out"], mode="nt", out_dtype=BF16, name="mixer_out_bwd")
        G["w_out"] = _mm(s["o"], dxb, mode="tn", out_dtype=BF16, name="mixer_out_grad")
        dqn, dkn, dv, dcc, dcr = _softmax_bwd(
            s["qn"], s["kn"], s["vb"], s["cf"], do, s["of"], s["lse"], H, HEAD_DIM,
            1.0 / math.sqrt(HEAD_DIM), name="fox_attn_bwd")
        dq, dk, dvb, G["gq"], G["gk"] = _fox_prep_bwd(s["qkv"], w["gq"], w["gk"], dqn, dkn, dv, H)
        dcf = (dcc[:, :, 0] + dcr[:, 0, :]).T
        dcf = jnp.pad(dcf, ((0, 0), (0, LANE - H)))
        dflog, G["bf"] = _fox_gate_bwd(s["flog"], w["bf"], dcf)
        dproj = jnp.concatenate([dq, dk, dvb], axis=1)
        G["w_qkv"] = _mm(h, dproj, mode="tn", out_dtype=BF16, name="fox_qkv_grad")
        G["w_f"] = _mm(h, dflog, mode="tn", out_dtype=BF16, name="fox_gate_grad")
        dh = _mm(dproj, w["w_qkv"], mode="nt", name="fox_qkv_bwd")
        dh = _mm(dflog, w["w_f"], mode="nt", res=dh, name="fox_gate_bwd_proj")
    else:
        w = P["mla"][j]
        do = _mm(dxb, w["w_out"], mode="nt", out_dtype=BF16, name="mixer_out_bwd")
        G["w_out"] = _mm(s["o"], dxb, mode="tn", out_dtype=BF16, name="mixer_out_grad")
        dqf, dkf, dv = _softmax_bwd(
            s["qf"], s["kf"], s["v"], None, do, s["of"], s["lse"], H, MLA_QK_PAD,
            1.0 / math.sqrt(MLA_QK), name="mla_attn_bwd")
        dqfull, dkv, dkr, G["gq"], G["gk"] = _mla_prep_bwd(
            s["qfull"], s["kv"], s["kr"], tabs, w["gq"], w["gk"], dqf, dkf, dv, H)
        G["w_uq"] = _mm(s["cq"], dqfull, mode="tn", out_dtype=BF16, name="mla_up_q_grad")
        G["w_ukv"] = _mm(s["ckv"], dkv, mode="tn", out_dtype=BF16, name="mla_up_kv_grad")
        dcq = _mm(dqfull, w["w_uq"], mode="nt", name="mla_up_q_bwd")
        dckv = _mm(dkv, w["w_ukv"], mode="nt", name="mla_up_kv_bwd")
        ddq, G["q_norm"] = _rmsnorm_bwd(s["dq"], w["q_norm"], dcq, want_f32=False, name="mla_q_norm_bwd")
        ddkv, G["kv_norm"] = _rmsnorm_bwd(s["dkv"], w["kv_norm"], dckv, want_f32=False, name="mla_kv_norm_bwd")
        G["w_dq"] = _mm(h, ddq, mode="tn", out_dtype=BF16, name="mla_down_q_grad")
        G["w_dkv"] = _mm(h, ddkv, mode="tn", out_dtype=BF16, name="mla_down_kv_grad")
        G["w_dr"] = _mm(h, dkr, mode="tn", out_dtype=BF16, name="mla_down_rope_grad")
        dh = _mm(ddq, w["w_dq"], mode="nt", name="mla_down_q_bwd")
        dh = _mm(ddkv, w["w_dkv"], mode="nt", res=dh, name="mla_down_kv_bwd")
        dh = _mm(dkr, w["w_dr"], mode="nt", res=dh, name="mla_down_rope_bwd")
    dx, dxb, G["mix_norm"] = _rmsnorm_bwd(s["x_in"], P["mix_norm"][i:i + 1], dh, dx, name="mix_norm_bwd")
    return dx, dxb, G


def _local_step(x, positions, loss_target, P, depth, H):
    tabs = _rope_tables(positions)
    saved = []
    for i in range(depth):
        x, s = _layer_fwd(i, x, P, tabs, H)
        saved.append(s)
    dx, dxb, loss = _loss_head(x, loss_target)
    grads = [None] * depth
    for i in reversed(range(depth)):
        dx, dxb, grads[i] = _layer_bwd(i, dx, dxb, saved[i], P, tabs, H)
    return loss, dx, grads


def _pad_cols(a, n):
    return jnp.pad(a, ((0, 0), (0, n - a.shape[1])))


def _prepare(full, H):
    W = H * HEAD_DIM
    P = {"mix_norm": full["mix_norm"], "mlp_norm": full["mlp_norm"], "sb": [], "fox": [], "mla": [], "mlp": []}
    for j in range(len(full["sb_w_in"])):
        P["sb"].append({"w_in": full["sb_w_in"][j], "w_out": full["sb_w_out"][j]})
    for j in range(len(full["fox_w_in"])):
        w = full["fox_w_in"][j]
        P["fox"].append({
            "w_qkv": w[:, :3 * W], "w_f": _pad_cols(w[:, 3 * W:], LANE),
            "bf": _pad_cols(full["fox_b_f"][j:j + 1], LANE),
            "gq": full["fox_q_gain"][j:j + 1], "gk": full["fox_k_gain"][j:j + 1],
            "w_out": full["fox_w_out"][j]})
    for j in range(len(full["mla_w_in"])):
        w = full["mla_w_in"][j]
        rq, rkv = full["mla_q_norm"].shape[1], full["mla_kv_norm"].shape[1]
        w_uq = full["mla_w_uq"][j].reshape(rq, H, MLA_QK)
        w_uq = jnp.pad(w_uq, ((0, 0), (0, 0), (0, MLA_QK_PAD - MLA_QK))).reshape(rq, H * MLA_QK_PAD)
        P["mla"].append({
            "w_dq": w[:, :rq], "w_dkv": w[:, rq:rq + rkv], "w_dr": _pad_cols(w[:, rq + rkv:], LANE),
            "q_norm": full["mla_q_norm"][j:j + 1], "kv_norm": full["mla_kv_norm"][j:j + 1],
            "w_uq": w_uq, "w_ukv": full["mla_w_ukv"][j],
            "gq": _pad_cols(full["mla_q_gain"][j:j + 1], MLA_QK_PAD),
            "gk": _pad_cols(full["mla_k_gain"][j:j + 1], MLA_QK_PAD),
            "w_out": full["mla_w_out"][j]})
    for i in range(len(full["mlp_w1"])):
        P["mlp"].append({"w1": full["mlp_w1"][i], "w2": full["mlp_w2"][i]})
    return P


def _unprepare(grads, H):
    out = {k: [] for k in (
        "mix_norm", "mlp_norm", "sb_w_in", "sb_w_out", "fox_w_in", "fox_b_f", "fox_q_gain", "fox_k_gain",
        "fox_w_out", "mla_w_in", "mla_q_norm", "mla_kv_norm", "mla_w_uq", "mla_w_ukv", "mla_q_gain",
        "mla_k_gain", "mla_w_out", "mlp_w1", "mlp_w2")}
    for i, G in enumerate(grads):
        kind = i % 3
        out["mix_norm"].append(G["mix_norm"])
        out["mlp_norm"].append(G["mlp_norm"])
        out["mlp_w1"].append(G["w1"])
        out["mlp_w2"].append(G["w2"])
        if kind == 0:
            out["sb_w_in"].append(G["w_in"])
            out["sb_w_out"].append(G["w_out"])
        elif kind == 1:
            out["fox_w_in"].append(jnp.concatenate([G["w_qkv"], G["w_f"][:, :H]], axis=1))
            out["fox_b_f"].append(G["bf"][:, :H])
            out["fox_q_gain"].append(G["gq"])
            out["fox_k_gain"].append(G["gk"])
            out["fox_w_out"].append(G["w_out"])
        else:
            rq = G["w_uq"].shape[0]
            out["mla_w_in"].append(jnp.concatenate([G["w_dq"], G["w_dkv"], G["w_dr"][:, :MLA_ROPE]], axis=1))
            out["mla_q_norm"].append(G["q_norm"])
            out["mla_kv_norm"].append(G["kv_norm"])
            out["mla_w_uq"].append(
                G["w_uq"].reshape(rq, H, MLA_QK_PAD)[:, :, :MLA_QK].reshape(rq, H * MLA_QK))
            out["mla_w_ukv"].append(G["w_ukv"])
            out["mla_q_gain"].append(G["gq"][:, :MLA_QK])
            out["mla_k_gain"].append(G["gk"][:, :MLA_QK])
            out["mla_w_out"].append(G["w_out"])
    return out


class _Place:
    def __init__(self, x, y, c):
        self.x, self.y, self.c = x, y, c
        self.dev = 4 * x + 2 * y + c
        self.chip = 2 * x + y
        self.id = (x, y, c)


def _peers(me, kind):
    if kind == "ici":
        return [_Place(1 - me.x, me.y, me.c), _Place(me.x, 1 - me.y, me.c), _Place(1 - me.x, 1 - me.y, me.c)]
    return [_Place(me.x, me.y, 1 - me.c)]


def _exchange(name, kind, operands, out_shapes, aliases, n_remote, n_local, plan):
    n_in, n_out = len(operands), len(out_shapes)

    def body(*refs):
        in_refs, out_refs = refs[:n_in], refs[n_in:n_in + n_out]
        send_sems, recv_sems, local_sems = refs[n_in + n_out:]
        me = _Place(lax.axis_index("x"), lax.axis_index("y"), lax.axis_index("c"))
        peers = _peers(me, kind)
        remote, local = plan(me, peers, in_refs, out_refs)
        assert len(remote) == n_remote and len(local) == n_local
        sends = []
        for n, (src, dst, k, _) in enumerate(remote):
            cp = pltpu.make_async_remote_copy(
                src_ref=src, dst_ref=dst, send_sem=send_sems.at[n], recv_sem=recv_sems.at[n],
                device_id=peers[k].id, device_id_type=MESH)
            cp.start()
            sends.append(cp)
        own = []
        for n, (src, dst) in enumerate(local):
            cp = pltpu.make_async_copy(src, dst, local_sems.at[n])
            cp.start()
            own.append(cp)
        for n, (src, _, k, landing) in enumerate(remote):
            pltpu.make_async_remote_copy(
                src_ref=src, dst_ref=landing, send_sem=send_sems.at[n], recv_sem=recv_sems.at[n],
                device_id=peers[k].id, device_id_type=MESH).wait_recv()
        for cp in sends:
            cp.wait_send()
        for cp in own:
            cp.wait()

    outs = pl.pallas_call(
        body, name=name, in_specs=[ANY] * n_in, out_specs=tuple([ANY] * n_out),
        out_shape=tuple(out_shapes), input_output_aliases=aliases,
        scratch_shapes=[pltpu.SemaphoreType.DMA((n_remote,)), pltpu.SemaphoreType.DMA((n_remote,)),
                        pltpu.SemaphoreType.DMA((max(n_local, 1),))],
    )(*operands)
    return list(outs)


def _window(ref, kind, d, shard_shape):
    r, c = shard_shape
    if kind == "col":
        return ref.at[:, pl.ds(pl.multiple_of(d * c, c), c)]
    return ref.at[pl.ds(pl.multiple_of(d * r, r), r), :]


def _full_shape(kind, shard_shape):
    r, c = shard_shape
    return (r, N_DEV * c) if kind == "col" else (N_DEV * r, c)


def _all_gather(shards, kinds, tag):
    n = len(shards)
    shapes = [s.shape for s in shards]
    out_shapes = [jax.ShapeDtypeStruct(_full_shape(k, s.shape), s.dtype) for k, s in zip(kinds, shards)]

    def plan_ici(me, peers, ins, outs):
        remote, local = [], []
        for a in range(n):
            mine = _window(outs[a], kinds[a], me.dev, shapes[a])
            for k, p in enumerate(peers):
                remote.append((ins[a], mine, k, _window(outs[a], kinds[a], p.dev, shapes[a])))
            local.append((ins[a], mine))
        return remote, local

    full = _exchange("gather_ici_" + tag, "ici", list(shards), out_shapes, {}, 3 * n, n, plan_ici)

    def plan_d2d(me, peers, ins, outs):
        remote = []
        for a in range(n):
            for ch in range(N_CHIP):
                held = _window(outs[a], kinds[a], 2 * ch + me.c, shapes[a])
                remote.append((held, held, 0, _window(outs[a], kinds[a], 2 * ch + 1 - me.c, shapes[a])))
        return remote, []

    return _exchange("gather_d2d_" + tag, "d2d", full, out_shapes, {a: a for a in range(n)},
                     N_CHIP * n, 0, plan_d2d)


def _pair_add(g, r4, kind, shard_shape, core, *, name):
    r, c = shard_shape
    tr = _row_tile(r, c)
    nb = r // tr
    if kind == "all":
        def body_all(core_ref, g_ref, r_ref, o_ref):
            o_ref[...] = g_ref[...] + r_ref[...]

        spec = pl.BlockSpec((tr, c), lambda i, core_ref: (i, 0))
        return pl.pallas_call(
            body_all, name=name,
            grid_spec=pltpu.PrefetchScalarGridSpec(
                num_scalar_prefetch=1, grid=(nb,), in_specs=[spec, spec], out_specs=spec),
            out_shape=jax.ShapeDtypeStruct((r, c), F32),
            compiler_params=_params(("parallel",)),
        )(core, g, r4)

    def body(core_ref, g_ref, r_ref, o_ref):
        o_ref[...] = (g_ref[...].astype(F32) + r_ref[...].astype(F32)).astype(BF16)

    if kind == "col":
        g_spec = pl.BlockSpec((tr, c), lambda ch, i, core_ref: (i, 2 * ch + core_ref[0]))
    else:
        g_spec = pl.BlockSpec((tr, c), lambda ch, i, core_ref: ((2 * ch + core_ref[0]) * nb + i, 0))
    slot = pl.BlockSpec((None, tr, c), lambda ch, i, core_ref: (ch, i, 0))
    return pl.pallas_call(
        body, name=name,
        grid_spec=pltpu.PrefetchScalarGridSpec(
            num_scalar_prefetch=1, grid=(N_CHIP, nb), in_specs=[g_spec, slot], out_specs=slot),
        out_shape=jax.ShapeDtypeStruct((N_CHIP, r, c), BF16),
        compiler_params=_params(("parallel", "parallel")),
    )(core, g, r4)


def _reduce_scatter(grads, kinds, shard_shapes, core, tag):
    n = len(grads)
    land = []
    for g, k, s in zip(grads, kinds, shard_shapes):
        land.append(jax.ShapeDtypeStruct(s if k == "all" else (N_CHIP,) + tuple(s), g.dtype))
    n_rem = sum(1 if k == "all" else N_CHIP for k in kinds)

    def plan_d2d(me, peers, ins, outs):
        remote = []
        for a in range(n):
            if kinds[a] == "all":
                remote.append((ins[a], outs[a], 0, outs[a]))
                continue
            for ch in range(N_CHIP):
                remote.append((_window(ins[a], kinds[a], 2 * ch + 1 - me.c, shard_shapes[a]),
                               outs[a].at[ch], 0, outs[a].at[ch]))
        return remote, []

    got = _exchange("scatter_d2d_" + tag, "d2d", list(grads), land, {}, n_rem, 0, plan_d2d)
    sums = [_pair_add(g, r4, k, s, core, name="pair_add_" + tag)
            for g, r4, k, s in zip(grads, got, kinds, shard_shapes)]
    land2 = [jax.ShapeDtypeStruct((N_CHIP,) + tuple(s), F32 if k == "all" else BF16)
             for k, s in zip(kinds, shard_shapes)]

    def plan_ici(me, peers, ins, outs):
        remote, local = [], []
        for a in range(n):
            for k, p in enumerate(peers):
                src = ins[a] if kinds[a] == "all" else ins[a].at[p.chip]
                remote.append((src, outs[a].at[me.chip], k, outs[a].at[p.chip]))
            src = ins[a] if kinds[a] == "all" else ins[a].at[me.chip]
            local.append((src, outs[a].at[me.chip]))
        return remote, local

    return _exchange("scatter_ici_" + tag, "ici", sums, land2, {}, 3 * n, n, plan_ici)


def _row_tile(r, c, limit=131072):
    best = None
    for t in range(8, r + 1, 8):
        if r % t == 0 and t * c <= limit:
            best = t
    return best if best is not None else r


def _adam_math(w, g, m, v):
    m = ADAM_B1 * m + (1.0 - ADAM_B1) * g
    v = ADAM_B2 * v + (1.0 - ADAM_B2) * (g * g)
    m_hat = m / (1.0 - ADAM_B1 ** ADAM_STEP)
    v_hat = v / (1.0 - ADAM_B2 ** ADAM_STEP)
    delta = -ADAM_LR * (m_hat / (jnp.sqrt(v_hat) + ADAM_EPS) + ADAM_WD * w)
    return delta, m, v


def _adamw(w, m, v, parts, layer, prev, *, name):
    L, r, c = w.shape
    tr = _row_tile(r, c)
    has_prev = prev is not None

    def body(*refs):
        w_ref, m_ref, v_ref, p_ref = refs[:4]
        g_ref, d_ref, nm_ref, nv_ref = refs[-4:]
        g = (p_ref[0].astype(F32) + p_ref[1].astype(F32)) + p_ref[2].astype(F32) + p_ref[3].astype(F32)
        delta, nm, nv = _adam_math(w_ref[...], g, m_ref[...], v_ref[...])
        g_ref[...] = g
        d_ref[...] = delta
        nm_ref[...] = nm
        nv_ref[...] = nv

    slab = pl.BlockSpec((None, tr, c), lambda i: (layer, i, 0))
    in_specs = [slab, slab, slab, pl.BlockSpec((N_CHIP, tr, c), lambda i: (0, i, 0))]
    args = [w, m, v, parts]
    aliases = {}
    if has_prev:
        in_specs += [ANY] * 4
        args += list(prev)
        aliases = {4 + n: n for n in range(4)}
    return pl.pallas_call(
        body, name=name, grid=(r // tr,), in_specs=in_specs, out_specs=(slab,) * 4,
        out_shape=tuple(jax.ShapeDtypeStruct((L, r, c), F32) for _ in range(4)),
        input_output_aliases=aliases, compiler_params=_params(("parallel",)),
    )(*args)


def _adamw_small(w, g, m, v):
    def body(w_ref, g_ref, m_ref, v_ref, d_ref, nm_ref, nv_ref):
        d_ref[...], nm_ref[...], nv_ref[...] = _adam_math(w_ref[...], g_ref[...], m_ref[...], v_ref[...])

    return pl.pallas_call(
        body, name="adamw_small", out_shape=tuple(jax.ShapeDtypeStruct(w.shape, F32) for _ in range(3)),
    )(w, g, m, v)


def _sum4(parts):
    def body(p_ref, o_ref):
        o_ref[...] = (p_ref[0] + p_ref[1]) + p_ref[2] + p_ref[3]

    return pl.pallas_call(
        body, name="sum_chips", out_shape=jax.ShapeDtypeStruct(parts.shape[1:], F32))(parts)


_WEIGHTS = ["mix_norm", "mlp_norm", "sb_w_in", "sb_w_out", "fox_w_in", "fox_b_f", "fox_q_gain", "fox_k_gain",
            "fox_w_out", "mla_w_in", "mla_q_norm", "mla_kv_norm", "mla_w_uq", "mla_w_ukv", "mla_q_gain",
            "mla_k_gain", "mla_w_out", "mlp_w1", "mlp_w2"]
_BIG = {"sb_w_in": "col", "sb_w_out": "row", "fox_w_in": "row", "fox_w_out": "row", "mla_w_in": "row",
        "mla_w_uq": "col", "mla_w_ukv": "col", "mla_w_out": "row", "mlp_w1": "col", "mlp_w2": "row"}


def _layer_big(i):
    kind, j = i % 3, i // 3
    mixer = {0: ["sb_w_in", "sb_w_out"], 1: ["fox_w_in", "fox_w_out"],
             2: ["mla_w_in", "mla_w_uq", "mla_w_ukv", "mla_w_out"]}[kind]
    return [(n, j) for n in mixer] + [("mlp_w1", i), ("mlp_w2", i)]


def _stack_to_cols(a):
    r = a.shape[0] // N_DEV
    return a.reshape(N_DEV, r, a.shape[1]).transpose(1, 0, 2).reshape(r, N_DEV * a.shape[1])


def _cols_to_stack(a):
    c = a.shape[1] // N_DEV
    return a.reshape(a.shape[0], N_DEV, c).transpose(1, 0, 2).reshape(N_DEV * a.shape[0], c)


def _pack_rows(rows, width):
    rows = [jnp.pad(r.reshape(-1).astype(F32), (0, width - r.size)) for r in rows]
    pad = (-len(rows)) % 8
    rows += [jnp.zeros((width,), F32)] * pad
    return jnp.stack(rows)


def kernel(x, positions, mix_norm, mlp_norm, sb_w_in, sb_w_out, fox_w_in, fox_b_f, fox_q_gain, fox_k_gain, fox_w_out, mla_w_in, mla_q_norm, mla_kv_norm, mla_w_uq, mla_w_ukv, mla_q_gain, mla_k_gain, mla_w_out, mlp_w1, mlp_w2, loss_target, m_mix_norm, m_mlp_norm, m_sb_w_in, m_sb_w_out, m_fox_w_in, m_fox_b_f, m_fox_q_gain, m_fox_k_gain, m_fox_w_out, m_mla_w_in, m_mla_q_norm, m_mla_kv_norm, m_mla_w_uq, m_mla_w_ukv, m_mla_q_gain, m_mla_k_gain, m_mla_w_out, m_mlp_w1, m_mlp_w2, v_mix_norm, v_mlp_norm, v_sb_w_in, v_sb_w_out, v_fox_w_in, v_fox_b_f, v_fox_q_gain, v_fox_k_gain, v_fox_w_out, v_mla_w_in, v_mla_q_norm, v_mla_kv_norm, v_mla_w_uq, v_mla_w_ukv, v_mla_q_gain, v_mla_k_gain, v_mla_w_out, v_mlp_w1, v_mlp_w2):
    w_in = dict(zip(_WEIGHTS, (mix_norm, mlp_norm, sb_w_in, sb_w_out, fox_w_in, fox_b_f, fox_q_gain, fox_k_gain, fox_w_out, mla_w_in, mla_q_norm, mla_kv_norm, mla_w_uq, mla_w_ukv, mla_q_gain, mla_k_gain, mla_w_out, mlp_w1, mlp_w2)))
    m_in = dict(zip(_WEIGHTS, (m_mix_norm, m_mlp_norm, m_sb_w_in, m_sb_w_out, m_fox_w_in, m_fox_b_f, m_fox_q_gain, m_fox_k_gain, m_fox_w_out, m_mla_w_in, m_mla_q_norm, m_mla_kv_norm, m_mla_w_uq, m_mla_w_ukv, m_mla_q_gain, m_mla_k_gain, m_mla_w_out, m_mlp_w1, m_mlp_w2)))
    v_in = dict(zip(_WEIGHTS, (v_mix_norm, v_mlp_norm, v_sb_w_in, v_sb_w_out, v_fox_w_in, v_fox_b_f, v_fox_q_gain, v_fox_k_gain, v_fox_w_out, v_mla_w_in, v_mla_q_norm, v_mla_kv_norm, v_mla_w_uq, v_mla_w_ukv, v_mla_q_gain, v_mla_k_gain, v_mla_w_out, v_mlp_w1, v_mlp_w2)))
    depth, D = mix_norm.shape
    H = D // HEAD_DIM
    n_mla = mla_w_in.shape[0]
    dev = 4 * lax.axis_index("x") + 2 * lax.axis_index("y") + lax.axis_index("c")
    core = lax.axis_index("c").astype(jnp.int32).reshape(1)

    full = {n: [None] * w_in[n].shape[0] for n in _BIG}
    for i in range(depth):
        names = _layer_big(i)
        shards = [w_in[n][j].astype(BF16) for n, j in names]
        kinds = [_BIG[n] for n, _ in names]
        if i % 3 == 2:
            jm = i // 3
            shards.append(_pack_rows([mla_q_norm[jm], mla_kv_norm[jm]], LANE))
            kinds.append("row")
        got = _all_gather(shards, kinds, f"l{i}")
        for (n, j), a in zip(names, got):
            full[n][j] = _stack_to_cols(a) if n == "fox_w_in" else a
        if i % 3 == 2:
            tiles = got[-1].reshape(N_DEV, 8, LANE)
            nq, nkv = mla_q_norm.shape[1], mla_kv_norm.shape[1]
            full.setdefault("mla_q_norm", [None] * n_mla)[jm] = tiles[:, 0, :nq].reshape(-1)
            full.setdefault("mla_kv_norm", [None] * n_mla)[jm] = tiles[:, 1, :nkv].reshape(-1)
    for n in ("mla_q_norm", "mla_kv_norm"):
        full[n] = jnp.stack(full[n]) if n in full else w_in[n]
    for n in ("mix_norm", "mlp_norm", "fox_b_f", "fox_q_gain", "fox_k_gain", "mla_q_gain", "mla_k_gain"):
        full[n] = w_in[n]

    P = _prepare(full, H)
    loss, grad_x, layer_grads = _local_step(x[0], positions[0], loss_target[0], P, depth, H)
    G = _unprepare(layer_grads, H)
    loss = lax.psum(loss[0, 0], ("x", "y", "c"))

    results = {}
    for i in reversed(range(depth)):
        names = _layer_big(i)
        grads, kinds, shapes = [], [], []
        for n, j in names:
            g = G[n][j]
            grads.append(_cols_to_stack(g) if n == "fox_w_in" else g)
            kinds.append(_BIG[n])
            shapes.append(w_in[n].shape[1:])
        parts = _reduce_scatter(grads, kinds, shapes, core, f"l{i}")
        for (n, j), p in zip(names, parts):
            results[n] = _adamw(w_in[n], m_in[n], v_in[n], p, j, results.get(n), name="adamw_" + n)

    small = [n for n in _WEIGHTS if n not in _BIG]
    rows = []
    for n in small:
        for j in range(w_in[n].shape[0]):
            rows.append(G[n][j])
    sg = _pack_rows(rows, D)
    (parts,) = _reduce_scatter([sg], ["all"], [sg.shape], core, "small")
    sgr = _sum4(parts)
    g_rows, w_rows, m_rows, v_rows, at = [], [], [], [], 0
    for n in small:
        for j in range(w_in[n].shape[0]):
            width = w_in[n].shape[1]
            if n in ("mla_q_norm", "mla_kv_norm"):
                g_rows.append(lax.dynamic_slice(sgr[at], (dev * width,), (width,)))
            else:
                g_rows.append(sgr[at, :width])
            w_rows.append(w_in[n][j])
            m_rows.append(m_in[n][j])
            v_rows.append(v_in[n][j])
            at += 1
    g_pack = _pack_rows(g_rows, D)
    d_pack, nm_pack, nv_pack = _adamw_small(
        _pack_rows(w_rows, D), g_pack, _pack_rows(m_rows, D), _pack_rows(v_rows, D))
    at = 0
    for n in small:
        L, width = w_in[n].shape
        results[n] = tuple(p[at:at + L, :width] for p in (g_pack, d_pack, nm_pack, nv_pack))
        at += L

    out = [loss, grad_x[None]]
    for part in range(4):
        out += [results[n][part] for n in _WEIGHTS]
    return tuple(out)
```

```python
import functools
import math

import jax
import jax.numpy as jnp
from jax import lax
from jax.experimental import pallas as pl
from jax.experimental.pallas import tpu as pltpu

F32 = jnp.float32
BF16 = jnp.bfloat16

HEAD_DIM = 128
MLA_NOPE = 128
MLA_ROPE = 64
MLA_V = 128
MLA_QK = MLA_NOPE + MLA_ROPE
MLA_QK_PAD = 256
LANE = 128
ROPE_THETA = 10000.0
EPS = 1e-6
ADAM_LR = 0.001
ADAM_B1 = 0.9
ADAM_B2 = 0.999
ADAM_EPS = 1e-08
ADAM_WD = 0.01
ADAM_STEP = 10
N_DEV = 8
N_CHIP = 4
NEG = -1e30
SB_DEAD = -104.0
VMEM_LIMIT = 56 * 1024 * 1024
ATT_BLOCK = 256
MESH = pl.DeviceIdType.MESH
ANY = pl.BlockSpec(memory_space=pl.ANY)


def _params(sem):
    return pltpu.CompilerParams(dimension_semantics=sem, vmem_limit_bytes=VMEM_LIMIT)


def _tile(dim, pref):
    if dim <= pref:
        return dim
    t = pref
    while dim % t:
        t -= LANE
    return t


def _dot(a, b, dims):
    return lax.dot_general(a, b, (dims, ((), ())), preferred_element_type=F32)


NN = ((1,), (0,))
NT = ((1,), (1,))
TN = ((0,), (0,))


def _mm(a, b, *, mode="nn", out_dtype=F32, res=None, act=None, z=None, name,
        tm=1024, tn=512, tk=2048):
    if mode == "nn":
        (M, K), (_, N) = a.shape, b.shape
    elif mode == "nt":
        (M, K), (N, _) = a.shape, b.shape
    else:
        (K, M), (_, N) = a.shape, b.shape
    tm, tn, tk = _tile(M, tm), _tile(N, tn), _tile(K, tk)
    nk = K // tk
    if mode == "tn":
        a_spec = pl.BlockSpec((tk, tm), lambda i, j, k: (k, i))
    else:
        a_spec = pl.BlockSpec((tm, tk), lambda i, j, k: (i, k))
    if mode == "nt":
        b_spec = pl.BlockSpec((tn, tk), lambda i, j, k: (j, k))
    else:
        b_spec = pl.BlockSpec((tk, tn), lambda i, j, k: (k, j))
    dims = {"nn": NN, "nt": NT, "tn": TN}[mode]
    o_spec = pl.BlockSpec((tm, tn), lambda i, j, k: (i, j))
    in_specs, args = [a_spec, b_spec], [a, b]
    if res is not None:
        in_specs.append(o_spec)
        args.append(res)
    if act == "drelu2":
        in_specs.append(o_spec)
        args.append(z)
    if act == "relu2":
        out_shape = (jax.ShapeDtypeStruct((M, N), F32), jax.ShapeDtypeStruct((M, N), BF16))
        out_specs = (o_spec, o_spec)
    else:
        out_shape = jax.ShapeDtypeStruct((M, N), out_dtype)
        out_specs = o_spec
    has_res, has_z = res is not None, act == "drelu2"
    n_out = 2 if act == "relu2" else 1

    def body(*refs):
        a_ref, b_ref = refs[0], refs[1]
        idx = 2
        res_ref = z_ref = None
        if has_res:
            res_ref = refs[idx]
            idx += 1
        if has_z:
            z_ref = refs[idx]
            idx += 1
        outs = refs[idx:idx + n_out]

        def finish(r):
            if has_res:
                r = r + res_ref[...]
            if act == "relu2":
                outs[0][...] = r
                rr = jnp.maximum(r, 0.0)
                outs[1][...] = (rr * rr).astype(BF16)
            elif act == "drelu2":
                outs[0][...] = (r * (2.0 * jnp.maximum(z_ref[...], 0.0))).astype(out_dtype)
            else:
                outs[0][...] = r.astype(out_dtype)

        if nk == 1:
            finish(_dot(a_ref[...], b_ref[...], dims))
            return
        acc = refs[-1]
        k = pl.program_id(2)

        @pl.when(k == 0)
        def _():
            acc[...] = _dot(a_ref[...], b_ref[...], dims)

        @pl.when((k > 0) & (k < nk - 1))
        def _():
            acc[...] += _dot(a_ref[...], b_ref[...], dims)

        @pl.when(k == nk - 1)
        def _():
            finish(acc[...] + _dot(a_ref[...], b_ref[...], dims))

    return pl.pallas_call(
        body, name=name, grid=(M // tm, N // tn, nk), in_specs=in_specs, out_specs=out_specs,
        out_shape=out_shape, scratch_shapes=[pltpu.VMEM((tm, tn), F32)] if nk > 1 else [],
        compiler_params=_params(("parallel", "parallel", "arbitrary")),
    )(*args)


def _rmsnorm_fwd(x, g, *, name, tm=256):
    T, n = x.shape
    tm = _tile(T, tm)

    def body(x_ref, g_ref, o_ref):
        xf = x_ref[...]
        r = lax.rsqrt(jnp.mean(xf * xf, axis=-1, keepdims=True) + EPS)
        o_ref[...] = (xf * r * g_ref[...]).astype(BF16)

    return pl.pallas_call(
        body, name=name, grid=(T // tm,),
        in_specs=[pl.BlockSpec((tm, n), lambda i: (i, 0)), pl.BlockSpec((1, n), lambda i: (0, 0))],
        out_specs=pl.BlockSpec((tm, n), lambda i: (i, 0)),
        out_shape=jax.ShapeDtypeStruct((T, n), BF16),
        compiler_params=_params(("parallel",)),
    )(x, g)


def _rmsnorm_bwd(x, g, dy, dx_in=None, *, name, want_f32=True, tm=256):
    T, n = x.shape
    tm = _tile(T, tm)
    has_in = dx_in is not None
    row = pl.BlockSpec((tm, n), lambda i: (i, 0))
    vec = pl.BlockSpec((1, n), lambda i: (0, 0))

    def body(*refs):
        x_ref, g_ref, dy_ref = refs[:3]
        idx = 3
        in_ref = None
        if has_in:
            in_ref = refs[idx]
            idx += 1
        outs = refs[idx:]
        xf = x_ref[...]
        r = lax.rsqrt(jnp.mean(xf * xf, axis=-1, keepdims=True) + EPS)
        dyf = dy_ref[...]
        t = dyf * g_ref[...]
        dx = r * t - xf * (r * r * r * jnp.mean(t * xf, axis=-1, keepdims=True))
        if has_in:
            dx = dx + in_ref[...]
        o = 0
        if want_f32:
            outs[0][...] = dx
            o = 1
        outs[o][...] = dx.astype(BF16)
        dg_ref = outs[o + 1]

        @pl.when(pl.program_id(0) == 0)
        def _():
            dg_ref[...] = jnp.zeros_like(dg_ref)

        dg_ref[...] += jnp.sum(dyf * xf * r, axis=0, keepdims=True)

    in_specs = [row, vec, row] + ([row] if has_in else [])
    out_specs, out_shape = [], []
    if want_f32:
        out_specs.append(row)
        out_shape.append(jax.ShapeDtypeStruct((T, n), F32))
    out_specs += [row, vec]
    out_shape += [jax.ShapeDtypeStruct((T, n), BF16), jax.ShapeDtypeStruct((1, n), F32)]
    args = [x, g, dy] + ([dx_in] if has_in else [])
    return pl.pallas_call(
        body, name=name, grid=(T // tm,), in_specs=in_specs, out_specs=tuple(out_specs),
        out_shape=tuple(out_shape), compiler_params=_params(("arbitrary",)),
    )(*args)


def _loss_head(y, target, *, tm=256):
    T, n = y.shape
    tm = _tile(T, tm)
    row = pl.BlockSpec((tm, n), lambda i: (i, 0))

    def body(y_ref, t_ref, dy_ref, dyb_ref, loss_ref):
        err = y_ref[...] - t_ref[...]
        dy = err * (1.0 / n)
        dy_ref[...] = dy
        dyb_ref[...] = dy.astype(BF16)

        @pl.when(pl.program_id(0) == 0)
        def _():
            loss_ref[...] = jnp.zeros_like(loss_ref)

        part = 0.5 * jnp.sum(jnp.mean(err * err, axis=-1, keepdims=True), axis=0, keepdims=True)
        loss_ref[...] += part

    return pl.pallas_call(
        body, name="loss_head", grid=(T // tm,), in_specs=[row, row],
        out_specs=(row, row, pl.BlockSpec((1, 1), lambda i: (0, 0))),
        out_shape=(jax.ShapeDtypeStruct((T, n), F32), jax.ShapeDtypeStruct((T, n), BF16),
                   jax.ShapeDtypeStruct((1, 1), F32)),
        compiler_params=_params(("arbitrary",)),
    )(y, target)


def _split3(v):
    hi = v.astype(BF16)
    r1 = v - hi.astype(F32)
    mid = r1.astype(BF16)
    lo = (r1 - mid.astype(F32)).astype(BF16)
    return hi, mid, lo


def _tri_right(v, tri):
    hi, mid, lo = _split3(v)
    return _dot(hi, tri, NN) + _dot(mid, tri, NN) + _dot(lo, tri, NN)


def _tri_left(tri, v):
    hi, mid, lo = _split3(v)
    return _dot(tri, hi, NN) + _dot(tri, mid, NN) + _dot(tri, lo, NN)


def _iota2(shape, dim):
    return lax.broadcasted_iota(jnp.int32, shape, dim)


def _log_sigmoid(z):
    return jnp.minimum(z, 0.0) - jnp.log1p(jnp.exp(-jnp.abs(z)))


def _head_spec(rows, width, off):
    return pl.BlockSpec((rows, width), lambda h, i: (i, off + h))


def _full_head_spec(T, width, off):
    return pl.BlockSpec((T, width), lambda h, i: (0, off + h))


def _sb_fwd(qkv, H):
    T = qkv.shape[0]
    B = _tile(T, ATT_BLOCK)
    nq = T // B
    scale = 1.0 / math.sqrt(HEAD_DIM)

    def body(q_ref, k_ref, v_ref, o_ref, c_ref, n_ref):
        h = pl.program_id(0)
        i = pl.program_id(1)
        q = q_ref[...]
        rows = _iota2((B, B), 0)
        cols = _iota2((B, B), 1)
        tri = (rows > cols).astype(BF16)

        def step(carry):
            n, _, acc, run = carry
            j = i - n
            ks = pl.multiple_of(j * B, B)
            kb = k_ref[pl.ds(ks, B), :]
            vb = v_ref[pl.ds(ks, B), :]
            z = _dot(q, kb, NT) * scale
            mask = (ks + cols) < (i * B + rows)
            ls = _log_sigmoid(z)
            lk = jnp.where(mask, ls - z, 0.0)
            later = _tri_right(lk, tri)
            a = jnp.where(mask, jnp.exp(ls + later), 0.0)
            acc = acc + jnp.exp(run) * _dot(a.astype(BF16), vb, NN)
            run = run + jnp.sum(lk, axis=1, keepdims=True)
            return n + 1, jnp.max(run), acc, run

        n, _, acc, run = lax.while_loop(
            lambda c: (c[0] <= i) & (c[1] > SB_DEAD), step,
            (jnp.int32(0), jnp.float32(0.0), jnp.zeros((B, HEAD_DIM), F32), jnp.zeros((B, 1), F32)))
        o_ref[...] = acc.astype(BF16)
        c_ref[...] = run
        n_ref[h, i] = n

    return pl.pallas_call(
        body, name="sb_attn_fwd", grid=(H, nq),
        in_specs=[_head_spec(B, HEAD_DIM, 0), _full_head_spec(T, HEAD_DIM, H),
                  _full_head_spec(T, HEAD_DIM, 2 * H)],
        out_specs=(_head_spec(B, HEAD_DIM, 0), pl.BlockSpec((None, B, 1), lambda h, i: (h, i, 0)),
                   pl.BlockSpec(memory_space=pltpu.SMEM)),
        out_shape=(jax.ShapeDtypeStruct((T, H * HEAD_DIM), BF16), jax.ShapeDtypeStruct((H, T, 1), F32),
                   jax.ShapeDtypeStruct((H, nq), jnp.int32)),
        compiler_params=_params(("arbitrary", "arbitrary")),
    )(qkv, qkv, qkv)


def _sb_bwd(qkv, do, ctot, nblk, H):
    T = qkv.shape[0]
    B = _tile(T, ATT_BLOCK)
    nq = T // B
    scale = 1.0 / math.sqrt(HEAD_DIM)

    def body(n_ref, q_ref, k_ref, v_ref, do_ref, c_ref, dq_ref, dk_ref, dv_ref, dk_acc, dv_acc):
        i = pl.program_id(1)
        first = jnp.clip(i + 1 - n_ref[pl.program_id(0), i], 0, i)

        @pl.when(i == 0)
        def _():
            dk_acc[...] = jnp.zeros_like(dk_acc)
            dv_acc[...] = jnp.zeros_like(dv_acc)

        q = q_ref[...]
        do_b = do_ref[...]
        ctot_b = c_ref[...]
        rows = _iota2((B, B), 0)
        cols = _iota2((B, B), 1)
        tri_incl = (rows <= cols).astype(BF16)
        tri_strict = (rows < cols).astype(BF16)

        def step(j, carry):
            dq, lpre, gpre = carry
            ks = pl.multiple_of(j * B, B)
            kb = k_ref[pl.ds(ks, B), :]
            vb = v_ref[pl.ds(ks, B), :]
            z = _dot(q, kb, NT) * scale
            mask = (ks + cols) < (i * B + rows)
            ls = _log_sigmoid(z)
            lk_all = ls - z
            lk = jnp.where(mask, lk_all, 0.0)
            later = ctot_b - lpre - _tri_right(lk, tri_incl)
            a = jnp.where(mask, jnp.exp(ls + later), 0.0)
            da = _dot(do_b, vb, NT)
            g = a * da
            gex = gpre + _tri_right(g, tri_strict)
            dz = jnp.where(mask, g * jnp.exp(lk_all) - jnp.exp(ls) * gex, 0.0) * scale
            dzb = dz.astype(BF16)
            dq = dq + _dot(dzb, kb, NN)
            dk_acc[pl.ds(ks, B), :] += _dot(dzb, q, TN)
            dv_acc[pl.ds(ks, B), :] += _dot(a.astype(BF16), do_b, TN)
            lpre = lpre + jnp.sum(lk, axis=1, keepdims=True)
            gpre = gpre + jnp.sum(g, axis=1, keepdims=True)
            return dq, lpre, gpre

        dq, _, _ = lax.fori_loop(
            first, i + 1, step,
            (jnp.zeros((B, HEAD_DIM), F32), jnp.zeros((B, 1), F32), jnp.zeros((B, 1), F32)))
        dq_ref[...] = dq.astype(BF16)

        @pl.when(i == nq - 1)
        def _():
            dk_ref[...] = dk_acc[...].astype(BF16)
            dv_ref[...] = dv_acc[...].astype(BF16)

    def blk(off):
        return pl.BlockSpec((B, HEAD_DIM), lambda h, i, n: (i, off + h))

    def whole(off):
        return pl.BlockSpec((T, HEAD_DIM), lambda h, i, n: (0, off + h))

    W = H * HEAD_DIM
    return pl.pallas_call(
        body, name="sb_attn_bwd",
        grid_spec=pltpu.PrefetchScalarGridSpec(
            num_scalar_prefetch=1, grid=(H, nq),
            in_specs=[blk(0), whole(H), whole(2 * H), blk(0),
                      pl.BlockSpec((None, B, 1), lambda h, i, n: (h, i, 0))],
            out_specs=(blk(0), whole(0), whole(0)),
            scratch_shapes=[pltpu.VMEM((T, HEAD_DIM), F32), pltpu.VMEM((T, HEAD_DIM), F32)]),
        out_shape=tuple(jax.ShapeDtypeStruct((T, W), BF16) for _ in range(3)),
        compiler_params=_params(("parallel", "arbitrary")),
    )(nblk, qkv, qkv, qkv, do, ctot)


def _softmax_fwd(q, k, v, cf, H, dqk, scale, *, name):
    T = q.shape[0]
    B = _tile(T, ATT_BLOCK)
    nq = T // B
    has_cf = cf is not None

    def body(*refs):
        q_ref, k_ref, v_ref = refs[:3]
        idx = 3
        if has_cf:
            cfc_ref, cfr_ref = refs[3], refs[4]
            idx = 5
        o_ref, of_ref, lse_ref = refs[idx:idx + 3]
        i = pl.program_id(1)
        qb = q_ref[...]
        rows = _iota2((B, B), 0)
        cols = _iota2((B, B), 1)

        def step(j, carry):
            m, l, acc = carry
            ks = pl.multiple_of(j * B, B)
            kb = k_ref[pl.ds(ks, B), :]
            vb = v_ref[pl.ds(ks, B), :]
            s = _dot(qb, kb, NT) * scale
            if has_cf:
                s = s + (cfc_ref[...] - cfr_ref[:, pl.ds(ks, B)])
            s = jnp.where((ks + cols) <= (i * B + rows), s, NEG)
            m_new = jnp.maximum(m, jnp.max(s, axis=1, keepdims=True))
            alpha = jnp.exp(m - m_new)
            p = jnp.exp(s - m_new)
            l = alpha * l + jnp.sum(p, axis=1, keepdims=True)
            acc = alpha * acc + _dot(p.astype(BF16), vb, NN)
            return m_new, l, acc

        m, l, acc = lax.fori_loop(
            0, i + 1, step,
            (jnp.full((B, 1), NEG, F32), jnp.zeros((B, 1), F32), jnp.zeros((B, HEAD_DIM), F32)))
        o = acc / l
        o_ref[...] = o.astype(BF16)
        of_ref[...] = o
        lse_ref[...] = m + jnp.log(l)

    stat = pl.BlockSpec((None, B, 1), lambda h, i: (h, i, 0))
    in_specs = [_head_spec(B, dqk, 0), _full_head_spec(T, dqk, 0), _full_head_spec(T, HEAD_DIM, 0)]
    args = [q, k, v]
    if has_cf:
        in_specs += [stat, pl.BlockSpec((None, 1, T), lambda h, i: (h, 0, 0))]
        args += list(cf)
    W = H * HEAD_DIM
    return pl.pallas_call(
        body, name=name, grid=(H, nq), in_specs=in_specs,
        out_specs=(_head_spec(B, HEAD_DIM, 0), _head_spec(B, HEAD_DIM, 0), stat),
        out_shape=(jax.ShapeDtypeStruct((T, W), BF16), jax.ShapeDtypeStruct((T, W), F32),
                   jax.ShapeDtypeStruct((H, T, 1), F32)),
        compiler_params=_params(("parallel", "arbitrary")),
    )(*args)


def _softmax_bwd(q, k, v, cf, do, o, lse, H, dqk, scale, *, name):
    T = q.shape[0]
    B = _tile(T, ATT_BLOCK)
    nq = T // B
    has_cf = cf is not None

    def body(*refs):
        q_ref, k_ref, v_ref, do_ref, o_ref, lse_ref = refs[:6]
        idx = 6
        if has_cf:
            cfc_ref, cfr_ref = refs[6], refs[7]
            idx = 8
        dq_ref, dk_ref, dv_ref = refs[idx:idx + 3]
        idx += 3
        if has_cf:
            dcc_ref, dcr_ref = refs[idx], refs[idx + 1]
        i = pl.program_id(1)

        @pl.when(i == 0)
        def _():
            dk_ref[...] = jnp.zeros_like(dk_ref)
            dv_ref[...] = jnp.zeros_like(dv_ref)
            if has_cf:
                dcr_ref[...] = jnp.zeros_like(dcr_ref)

        qb = q_ref[...]
        do_b = do_ref[...]
        lse_b = lse_ref[...]
        delta = jnp.sum(do_b.astype(F32) * o_ref[...], axis=1, keepdims=True)
        rows = _iota2((B, B), 0)
        cols = _iota2((B, B), 1)

        def step(j, carry):
            dq, rs = carry
            ks = pl.multiple_of(j * B, B)
            kb = k_ref[pl.ds(ks, B), :]
            vb = v_ref[pl.ds(ks, B), :]
            s = _dot(qb, kb, NT) * scale
            if has_cf:
                s = s + (cfc_ref[...] - cfr_ref[:, pl.ds(ks, B)])
            p = jnp.where((ks + cols) <= (i * B + rows), jnp.exp(s - lse_b), 0.0)
            dp = _dot(do_b, vb, NT)
            ds = p * (dp - delta)
            dsb = (ds * scale).astype(BF16)
            dq = dq + _dot(dsb, kb, NN)
            dk_ref[pl.ds(ks, B), :] += _dot(dsb, qb, TN)
            dv_ref[pl.ds(ks, B), :] += _dot(p.astype(BF16), do_b, TN)
            if has_cf:
                rs = rs + jnp.sum(ds, axis=1, keepdims=True)
                dcr_ref[:, pl.ds(ks, B)] -= jnp.sum(ds, axis=0, keepdims=True)
            return dq, rs

        dq, rs = lax.fori_loop(0, i + 1, step, (jnp.zeros((B, dqk), F32), jnp.zeros((B, 1), F32)))
        dq_ref[...] = dq
        if has_cf:
            dcc_ref[...] = rs

    stat = pl.BlockSpec((None, B, 1), lambda h, i: (h, i, 0))
    rowstat = pl.BlockSpec((None, 1, T), lambda h, i: (h, 0, 0))
    in_specs = [_head_spec(B, dqk, 0), _full_head_spec(T, dqk, 0), _full_head_spec(T, HEAD_DIM, 0),
                _head_spec(B, HEAD_DIM, 0), _head_spec(B, HEAD_DIM, 0), stat]
    args = [q, k, v, do, o, lse]
    out_specs = [_head_spec(B, dqk, 0), _full_head_spec(T, dqk, 0), _full_head_spec(T, HEAD_DIM, 0)]
    out_shape = [jax.ShapeDtypeStruct((T, H * dqk), F32), jax.ShapeDtypeStruct((T, H * dqk), F32),
                 jax.ShapeDtypeStruct((T, H * HEAD_DIM), F32)]
    if has_cf:
        in_specs += [stat, rowstat]
        args += list(cf)
        out_specs += [stat, rowstat]
        out_shape += [jax.ShapeDtypeStruct((H, T, 1), F32), jax.ShapeDtypeStruct((H, 1, T), F32)]
    return pl.pallas_call(
        body, name=name, grid=(H, nq), in_specs=in_specs, out_specs=tuple(out_specs),
        out_shape=tuple(out_shape), compiler_params=_params(("parallel", "arbitrary")),
    )(*args)


def _headnorm(x, g):
    r = lax.rsqrt(jnp.mean(x * x, axis=-1, keepdims=True) + EPS)
    return x * r * g, r


def _headnorm_bwd(x, g, dy, n):
    r = lax.rsqrt(jnp.sum(x * x, axis=-1, keepdims=True) * (1.0 / n) + EPS)
    t = dy * g
    dx = r * t - x * (r * r * r * (jnp.sum(t * x, axis=-1, keepdims=True) * (1.0 / n)))
    dg = jnp.sum(dy * x * r, axis=0, keepdims=True)
    return dx, dg


def _fox_prep_fwd(qkv, gq, gk, H, *, tm=512):
    T = qkv.shape[0]
    tm = _tile(T, tm)

    def body(q_ref, k_ref, v_ref, gq_ref, gk_ref, qn_ref, kn_ref, vb_ref):
        qn_ref[...] = _headnorm(q_ref[...], gq_ref[...])[0].astype(BF16)
        kn_ref[...] = _headnorm(k_ref[...], gk_ref[...])[0].astype(BF16)
        vb_ref[...] = v_ref[...].astype(BF16)

    def blk(off):
        return pl.BlockSpec((tm, HEAD_DIM), lambda i, h: (i, off + h))

    vec = pl.BlockSpec((1, HEAD_DIM), lambda i, h: (0, 0))
    W = H * HEAD_DIM
    return pl.pallas_call(
        body, name="fox_prep_fwd", grid=(T // tm, H),
        in_specs=[blk(0), blk(H), blk(2 * H), vec, vec], out_specs=(blk(0), blk(0), blk(0)),
        out_shape=tuple(jax.ShapeDtypeStruct((T, W), BF16) for _ in range(3)),
        compiler_params=_params(("parallel", "parallel")),
    )(qkv, qkv, qkv, gq, gk)


def _fox_prep_bwd(qkv, gq, gk, dqn, dkn, dv, H, *, tm=512):
    T = qkv.shape[0]
    tm = _tile(T, tm)

    def body(q_ref, k_ref, gq_ref, gk_ref, dqn_ref, dkn_ref, dv_ref,
             dq_ref, dk_ref, dvb_ref, dgq_ref, dgk_ref):
        @pl.when((pl.program_id(0) == 0) & (pl.program_id(1) == 0))
        def _():
            dgq_ref[...] = jnp.zeros_like(dgq_ref)
            dgk_ref[...] = jnp.zeros_like(dgk_ref)

        dq, dgq = _headnorm_bwd(q_ref[...], gq_ref[...], dqn_ref[...], HEAD_DIM)
        dk, dgk = _headnorm_bwd(k_ref[...], gk_ref[...], dkn_ref[...], HEAD_DIM)
        dq_ref[...] = dq.astype(BF16)
        dk_ref[...] = dk.astype(BF16)
        dvb_ref[...] = dv_ref[...].astype(BF16)
        dgq_ref[...] += dgq
        dgk_ref[...] += dgk

    def blk(off):
        return pl.BlockSpec((tm, HEAD_DIM), lambda i, h: (i, off + h))

    vec = pl.BlockSpec((1, HEAD_DIM), lambda i, h: (0, 0))
    W = H * HEAD_DIM
    return pl.pallas_call(
        body, name="fox_prep_bwd", grid=(T // tm, H),
        in_specs=[blk(0), blk(H), vec, vec, blk(0), blk(0), blk(0)],
        out_specs=(blk(0), blk(0), blk(0), vec, vec),
        out_shape=tuple(jax.ShapeDtypeStruct((T, W), BF16) for _ in range(3))
        + (jax.ShapeDtypeStruct((1, HEAD_DIM), F32), jax.ShapeDtypeStruct((1, HEAD_DIM), F32)),
        compiler_params=_params(("arbitrary", "arbitrary")),
    )(qkv, qkv, gq, gk, dqn, dkn, dv)


def _fox_gate_fwd(flog, bf, *, tm=256):
    T = flog.shape[0]
    tm = _tile(T, tm)

    def body(f_ref, b_ref, cf_ref, carry):
        @pl.when(pl.program_id(0) == 0)
        def _():
            carry[...] = jnp.zeros_like(carry)

        lf = _log_sigmoid(f_ref[...] + b_ref[...])
        tri = (_iota2((tm, tm), 1) <= _iota2((tm, tm), 0)).astype(BF16)
        cf_ref[...] = carry[...] + _tri_left(tri, lf)
        carry[...] += jnp.sum(lf, axis=0, keepdims=True)

    return pl.pallas_call(
        body, name="fox_gate_fwd", grid=(T // tm,),
        in_specs=[pl.BlockSpec((tm, LANE), lambda i: (i, 0)), pl.BlockSpec((1, LANE), lambda i: (0, 0))],
        out_specs=pl.BlockSpec((tm, LANE), lambda i: (i, 0)),
        out_shape=jax.ShapeDtypeStruct((T, LANE), F32),
        scratch_shapes=[pltpu.VMEM((1, LANE), F32)],
        compiler_params=_params(("arbitrary",)),
    )(flog, bf)


def _fox_gate_bwd(flog, bf, dcf, *, tm=256):
    T = flog.shape[0]
    tm = _tile(T, tm)
    nt = T // tm

    def body(f_ref, b_ref, dcf_ref, df_ref, db_ref, carry):
        @pl.when(pl.program_id(0) == 0)
        def _():
            carry[...] = jnp.zeros_like(carry)
            db_ref[...] = jnp.zeros_like(db_ref)

        d = dcf_ref[...]
        tri = (_iota2((tm, tm), 1) >= _iota2((tm, tm), 0)).astype(BF16)
        dlf = carry[...] + _tri_left(tri, d)
        carry[...] += jnp.sum(d, axis=0, keepdims=True)
        xg = f_ref[...] + b_ref[...]
        e = jnp.exp(-jnp.abs(xg))
        sig_neg = jnp.where(xg >= 0.0, e, 1.0) / (1.0 + e)
        df = dlf * sig_neg
        df_ref[...] = df.astype(BF16)
        db_ref[...] += jnp.sum(df, axis=0, keepdims=True)

    rev = pl.BlockSpec((tm, LANE), lambda i: (nt - 1 - i, 0))
    vec = pl.BlockSpec((1, LANE), lambda i: (0, 0))
    return pl.pallas_call(
        body, name="fox_gate_bwd", grid=(nt,), in_specs=[rev, vec, rev], out_specs=(rev, vec),
        out_shape=(jax.ShapeDtypeStruct((T, LANE), BF16), jax.ShapeDtypeStruct((1, LANE), F32)),
        scratch_shapes=[pltpu.VMEM((1, LANE), F32)],
        compiler_params=_params(("arbitrary",)),
    )(flog, bf, dcf)


def _rope_tables(positions):
    half = MLA_ROPE // 2
    inv_freq = ROPE_THETA ** (-jnp.arange(0, half, dtype=F32) * 2.0 / MLA_ROPE)
    ang = positions.astype(F32)[:, None] * inv_freq
    cos, sin = jnp.cos(ang), jnp.sin(ang)
    zero = jnp.zeros_like(cos)
    pad = jnp.zeros((positions.shape[0], LANE - MLA_ROPE), F32)
    cos_t = jnp.concatenate([cos, cos, pad], axis=1)
    sin_up = jnp.concatenate([zero, sin, pad], axis=1)
    sin_dn = jnp.concatenate([-sin, zero, pad], axis=1)
    return cos_t, sin_up, sin_dn


def _rope(x, cos_t, sin_up, sin_dn):
    half = MLA_ROPE // 2
    return x * cos_t + pltpu.roll(x, half, 1) * sin_up + pltpu.roll(x, LANE - half, 1) * sin_dn


def _rope_t(d, cos_t, sin_up, sin_dn):
    half = MLA_ROPE // 2
    return d * cos_t + pltpu.roll(d * sin_up, LANE - half, 1) + pltpu.roll(d * sin_dn, half, 1)


def _norm192(xcat, g):
    r = lax.rsqrt(jnp.sum(xcat * xcat, axis=-1, keepdims=True) * (1.0 / MLA_QK) + EPS)
    return xcat * r * g


def _mla_prep_fwd(qfull, kv, kr, tabs, gq, gk, H, *, tm=512):
    T = qfull.shape[0]
    tm = _tile(T, tm)

    def body(q_ref, kv_ref, kr_ref, c_ref, su_ref, sd_ref, gq_ref, gk_ref, qf_ref, kf_ref, v_ref):
        tabs_b = (c_ref[...], su_ref[...], sd_ref[...])
        qb = q_ref[...]
        qcat = jnp.concatenate([qb[:, :MLA_NOPE], _rope(qb[:, MLA_NOPE:], *tabs_b)], axis=1)
        qf_ref[...] = _norm192(qcat, gq_ref[...]).astype(BF16)
        kvb = kv_ref[...]
        kcat = jnp.concatenate([kvb[:, :MLA_NOPE], _rope(kr_ref[...], *tabs_b)], axis=1)
        kf_ref[...] = _norm192(kcat, gk_ref[...]).astype(BF16)
        v_ref[...] = kvb[:, MLA_NOPE:].astype(BF16)

    head = pl.BlockSpec((tm, MLA_QK_PAD), lambda i, h: (i, h))
    tok = pl.BlockSpec((tm, LANE), lambda i, h: (i, 0))
    vec = pl.BlockSpec((1, MLA_QK_PAD), lambda i, h: (0, 0))
    return pl.pallas_call(
        body, name="mla_prep_fwd", grid=(T // tm, H),
        in_specs=[head, head, tok, tok, tok, tok, vec, vec],
        out_specs=(head, head, pl.BlockSpec((tm, MLA_V), lambda i, h: (i, h))),
        out_shape=(jax.ShapeDtypeStruct((T, H * MLA_QK_PAD), BF16),
                   jax.ShapeDtypeStruct((T, H * MLA_QK_PAD), BF16),
                   jax.ShapeDtypeStruct((T, H * MLA_V), BF16)),
        compiler_params=_params(("parallel", "parallel")),
    )(qfull, kv, kr, *tabs, gq, gk)


def _mla_prep_bwd(qfull, kv, kr, tabs, gq, gk, dqf, dkf, dv, H, *, tm=512):
    T = qfull.shape[0]
    tm = _tile(T, tm)

    def body(q_ref, kv_ref, kr_ref, c_ref, su_ref, sd_ref, gq_ref, gk_ref, dqf_ref, dkf_ref, dv_ref,
             dq_ref, dkv_ref, dkr_ref, dgq_ref, dgk_ref, kr_acc):
        h = pl.program_id(1)

        @pl.when((pl.program_id(0) == 0) & (h == 0))
        def _():
            dgq_ref[...] = jnp.zeros_like(dgq_ref)
            dgk_ref[...] = jnp.zeros_like(dgk_ref)

        @pl.when(h == 0)
        def _():
            kr_acc[...] = jnp.zeros_like(kr_acc)

        tabs_b = (c_ref[...], su_ref[...], sd_ref[...])
        qb = q_ref[...]
        qcat = jnp.concatenate([qb[:, :MLA_NOPE], _rope(qb[:, MLA_NOPE:], *tabs_b)], axis=1)
        dqcat, dgq = _headnorm_bwd(qcat, gq_ref[...], dqf_ref[...], MLA_QK)
        dq_ref[...] = jnp.concatenate(
            [dqcat[:, :MLA_NOPE], _rope_t(dqcat[:, MLA_NOPE:], *tabs_b)], axis=1).astype(BF16)
        dgq_ref[...] += dgq
        kvb = kv_ref[...]
        kcat = jnp.concatenate([kvb[:, :MLA_NOPE], _rope(kr_ref[...], *tabs_b)], axis=1)
        dkcat, dgk = _headnorm_bwd(kcat, gk_ref[...], dkf_ref[...], MLA_QK)
        dkv_ref[...] = jnp.concatenate([dkcat[:, :MLA_NOPE], dv_ref[...]], axis=1).astype(BF16)
        dgk_ref[...] += dgk
        kr_acc[...] += dkcat[:, MLA_NOPE:]

        @pl.when(h == H - 1)
        def _():
            dkr_ref[...] = _rope_t(kr_acc[...], *tabs_b).astype(BF16)

    head = pl.BlockSpec((tm, MLA_QK_PAD), lambda i, h: (i, h))
    tok = pl.BlockSpec((tm, LANE), lambda i, h: (i, 0))
    vec = pl.BlockSpec((1, MLA_QK_PAD), lambda i, h: (0, 0))
    return pl.pallas_call(
        body, name="mla_prep_bwd", grid=(T // tm, H),
        in_specs=[head, head, tok, tok, tok, tok, vec, vec, head, head,
                  pl.BlockSpec((tm, MLA_V), lambda i, h: (i, h))],
        out_specs=(head, head, tok, vec, vec),
        out_shape=(jax.ShapeDtypeStruct((T, H * MLA_QK_PAD), BF16),
                   jax.ShapeDtypeStruct((T, H * MLA_QK_PAD), BF16),
                   jax.ShapeDtypeStruct((T, LANE), BF16),
                   jax.ShapeDtypeStruct((1, MLA_QK_PAD), F32), jax.ShapeDtypeStruct((1, MLA_QK_PAD), F32)),
        scratch_shapes=[pltpu.VMEM((tm, LANE), F32)],
        compiler_params=_params(("arbitrary", "arbitrary")),
    )(qfull, kv, kr, *tabs, gq, gk, dqf, dkf, dv)


def _cf_layouts(cf, H):
    cfh = cf[:, :H].T
    return cfh[:, :, None], cfh[:, None, :]


def _layer_fwd(i, x, P, tabs, H):
    kind, j = i % 3, i // 3
    s = {"x_in": x}
    h = _rmsnorm_fwd(x, P["mix_norm"][i:i + 1], name="mix_norm_fwd")
    s["h"] = h
    if kind == 0:
        w = P["sb"][j]
        qkv = _mm(h, w["w_in"], out_dtype=BF16, name="sb_qkv")
        o, ctot, nblk = _sb_fwd(qkv, H)
        s.update(qkv=qkv, o=o, ctot=ctot, nblk=nblk)
    elif kind == 1:
        w = P["fox"][j]
        qkv = _mm(h, w["w_qkv"], name="fox_qkv")
        flog = _mm(h, w["w_f"], name="fox_gate_proj")
        qn, kn, vb = _fox_prep_fwd(qkv, w["gq"], w["gk"], H)
        cf = _cf_layouts(_fox_gate_fwd(flog, w["bf"]), H)
        o, of, lse = _softmax_fwd(qn, kn, vb, cf, H, HEAD_DIM, 1.0 / math.sqrt(HEAD_DIM), name="fox_attn_fwd")
        s.update(qkv=qkv, flog=flog, qn=qn, kn=kn, vb=vb, cf=cf, o=o, of=of, lse=lse)
    else:
        w = P["mla"][j]
        dq = _mm(h, w["w_dq"], name="mla_down_q")
        dkv = _mm(h, w["w_dkv"], name="mla_down_kv")
        kr = _mm(h, w["w_dr"], name="mla_down_rope")
        cq = _rmsnorm_fwd(dq, w["q_norm"], name="mla_q_norm_fwd")
        ckv = _rmsnorm_fwd(dkv, w["kv_norm"], name="mla_kv_norm_fwd")
        qfull = _mm(cq, w["w_uq"], name="mla_up_q")
        kv = _mm(ckv, w["w_ukv"], name="mla_up_kv")
        qf, kf, v = _mla_prep_fwd(qfull, kv, kr, tabs, w["gq"], w["gk"], H)
        o, of, lse = _softmax_fwd(qf, kf, v, None, H, MLA_QK_PAD, 1.0 / math.sqrt(MLA_QK), name="mla_attn_fwd")
        s.update(dq=dq, dkv=dkv, kr=kr, cq=cq, ckv=ckv, qfull=qfull, kv=kv, qf=qf, kf=kf, v=v,
                 o=o, of=of, lse=lse)
    x = _mm(s["o"], w["w_out"], res=x, name="mixer_out")
    s["x_mid"] = x
    h2 = _rmsnorm_fwd(x, P["mlp_norm"][i:i + 1], name="mlp_norm_fwd")
    z, u = _mm(h2, P["mlp"][i]["w1"], act="relu2", name="mlp_up")
    x = _mm(u, P["mlp"][i]["w2"], res=x, name="mlp_down")
    s.update(h2=h2, z=z, u=u)
    return x, s


def _layer_bwd(i, dx, dxb, s, P, tabs, H):
    kind, j = i % 3, i // 3
    G = {}
    wm = P["mlp"][i]
    dz = _mm(dxb, wm["w2"], mode="nt", act="drelu2", z=s["z"], out_dtype=BF16, name="mlp_down_bwd")
    G["w2"] = _mm(s["u"], dxb, mode="tn", out_dtype=BF16, name="mlp_w2_grad")
    G["w1"] = _mm(s["h2"], dz, mode="tn", out_dtype=BF16, name="mlp_w1_grad")
    dh2 = _mm(dz, wm["w1"], mode="nt", name="mlp_up_bwd")
    dx, dxb, G["mlp_norm"] = _rmsnorm_bwd(s["x_mid"], P["mlp_norm"][i:i + 1], dh2, dx, name="mlp_norm_bwd")
    h = s["h"]
    if kind == 0:
        w = P["sb"][j]
        do = _mm(dxb, w["w_out"], mode="nt", out_dtype=BF16, name="mixer_out_bwd")
        G["w_out"] = _mm(s["o"], dxb, mode="tn", out_dtype=BF16, name="mixer_out_grad")
        dq, dk, dv = _sb_bwd(s["qkv"], do, s["ctot"], s["nblk"], H)
        dproj = jnp.concatenate([dq, dk, dv], axis=1)
        G["w_in"] = _mm(h, dproj, mode="tn", out_dtype=BF16, name="sb_qkv_grad")
        dh = _mm(dproj, w["w_in"], mode="nt", name="sb_qkv_bwd")
    elif kind == 1:
        w = P["fox"][j]
        do = _mm(dxb, w["w_out"], mode="nt", out_dtype=BF16, name="mixer_out_bwd")
        G["w_out"] = _mm(s["o"], dxb, mode="tn", out_dtype=BF16, name="mixer_out_grad")
        dqn, dkn, dv, dcc, dcr = _softmax_bwd(
            s["qn"], s["kn"], s["vb"], s["cf"], do, s["of"], s["lse"], H, HEAD_DIM,
            1.0 / math.sqrt(HEAD_DIM), name="fox_attn_bwd")
        dq, dk, dvb, G["gq"], G["gk"] = _fox_prep_bwd(s["qkv"], w["gq"], w["gk"], dqn, dkn, dv, H)
        dcf = (dcc[:, :, 0] + dcr[:, 0, :]).T
        dcf = jnp.pad(dcf, ((0, 0), (0, LANE - H)))
        dflog, G["bf"] = _fox_gate_bwd(s["flog"], w["bf"], dcf)
        dproj = jnp.concatenate([dq, dk, dvb], axis=1)
        G["w_qkv"] = _mm(h, dproj, mode="tn", out_dtype=BF16, name="fox_qkv_grad")
        G["w_f"] = _mm(h, dflog, mode="tn", out_dtype=BF16, name="fox_gate_grad")
        dh = _mm(dproj, w["w_qkv"], mode="nt", name="fox_qkv_bwd")
        dh = _mm(dflog, w["w_f"], mode="nt", res=dh, name="fox_gate_bwd_proj")
    else:
        w = P["mla"][j]
        do = _mm(dxb, w["w_out"], mode="nt", out_dtype=BF16, name="mixer_out_bwd")
        G["w_out"] = _mm(s["o"], dxb, mode="tn", out_dtype=BF16, name="mixer_out_grad")
        dqf, dkf, dv = _softmax_bwd(
            s["qf"], s["kf"], s["v"], None, do, s["of"], s["lse"], H, MLA_QK_PAD,
            1.0 / math.sqrt(MLA_QK), name="mla_attn_bwd")
        dqfull, dkv, dkr, G["gq"], G["gk"] = _mla_prep_bwd(
            s["qfull"], s["kv"], s["kr"], tabs, w["gq"], w["gk"], dqf, dkf, dv, H)
        G["w_uq"] = _mm(s["cq"], dqfull, mode="tn", out_dtype=BF16, name="mla_up_q_grad")
        G["w_ukv"] = _mm(s["ckv"], dkv, mode="tn", out_dtype=BF16, name="mla_up_kv_grad")
        dcq = _mm(dqfull, w["w_uq"], mode="nt", name="mla_up_q_bwd")
        dckv = _mm(dkv, w["w_ukv"], mode="nt", name="mla_up_kv_bwd")
        ddq, G["q_norm"] = _rmsnorm_bwd(s["dq"], w["q_norm"], dcq, want_f32=False, name="mla_q_norm_bwd")
        ddkv, G["kv_norm"] = _rmsnorm_bwd(s["dkv"], w["kv_norm"], dckv, want_f32=False, name="mla_kv_norm_bwd")
        G["w_dq"] = _mm(h, ddq, mode="tn", out_dtype=BF16, name="mla_down_q_grad")
        G["w_dkv"] = _mm(h, ddkv, mode="tn", out_dtype=BF16, name="mla_down_kv_grad")
        G["w_dr"] = _mm(h, dkr, mode="tn", out_dtype=BF16, name="mla_down_rope_grad")
        dh = _mm(ddq, w["w_dq"], mode="nt", name="mla_down_q_bwd")
        dh = _mm(ddkv, w["w_dkv"], mode="nt", res=dh, name="mla_down_kv_bwd")
        dh = _mm(dkr, w["w_dr"], mode="nt", res=dh, name="mla_down_rope_bwd")
    dx, dxb, G["mix_norm"] = _rmsnorm_bwd(s["x_in"], P["mix_norm"][i:i + 1], dh, dx, name="mix_norm_bwd")
    return dx, dxb, G


def _local_step(x, positions, loss_target, P, depth, H):
    tabs = _rope_tables(positions)
    saved = []
    for i in range(depth):
        x, s = _layer_fwd(i, x, P, tabs, H)
        saved.append(s)
    dx, dxb, loss = _loss_head(x, loss_target)
    grads = [None] * depth
    for i in reversed(range(depth)):
        dx, dxb, grads[i] = _layer_bwd(i, dx, dxb, saved[i], P, tabs, H)
    return loss, dx, grads


def _pad_cols(a, n):
    return jnp.pad(a, ((0, 0), (0, n - a.shape[1])))


def _prepare(full, H):
    W = H * HEAD_DIM
    P = {"mix_norm": full["mix_norm"], "mlp_norm": full["mlp_norm"], "sb": [], "fox": [], "mla": [], "mlp": []}
    for j in range(len(full["sb_w_in"])):
        P["sb"].append({"w_in": full["sb_w_in"][j], "w_out": full["sb_w_out"][j]})
    for j in range(len(full["fox_w_in"])):
        w = full["fox_w_in"][j]
        P["fox"].append({
            "w_qkv": w[:, :3 * W], "w_f": _pad_cols(w[:, 3 * W:], LANE),
            "bf": _pad_cols(full["fox_b_f"][j:j + 1], LANE),
            "gq": full["fox_q_gain"][j:j + 1], "gk": full["fox_k_gain"][j:j + 1],
            "w_out": full["fox_w_out"][j]})
    for j in range(len(full["mla_w_in"])):
        w = full["mla_w_in"][j]
        rq, rkv = full["mla_q_norm"].shape[1], full["mla_kv_norm"].shape[1]
        w_uq = full["mla_w_uq"][j].reshape(rq, H, MLA_QK)
        w_uq = jnp.pad(w_uq, ((0, 0), (0, 0), (0, MLA_QK_PAD - MLA_QK))).reshape(rq, H * MLA_QK_PAD)
        P["mla"].append({
            "w_dq": w[:, :rq], "w_dkv": w[:, rq:rq + rkv], "w_dr": _pad_cols(w[:, rq + rkv:], LANE),
            "q_norm": full["mla_q_norm"][j:j + 1], "kv_norm": full["mla_kv_norm"][j:j + 1],
            "w_uq": w_uq, "w_ukv": full["mla_w_ukv"][j],
            "gq": _pad_cols(full["mla_q_gain"][j:j + 1], MLA_QK_PAD),
            "gk": _pad_cols(full["mla_k_gain"][j:j + 1], MLA_QK_PAD),
            "w_out": full["mla_w_out"][j]})
    for i in range(len(full["mlp_w1"])):
        P["mlp"].append({"w1": full["mlp_w1"][i], "w2": full["mlp_w2"][i]})
    return P


def _unprepare(grads, H):
    out = {k: [] for k in (
        "mix_norm", "mlp_norm", "sb_w_in", "sb_w_out", "fox_w_in", "fox_b_f", "fox_q_gain", "fox_k_gain",
        "fox_w_out", "mla_w_in", "mla_q_norm", "mla_kv_norm", "mla_w_uq", "mla_w_ukv", "mla_q_gain",
        "mla_k_gain", "mla_w_out", "mlp_w1", "mlp_w2")}
    for i, G in enumerate(grads):
        kind = i % 3
        out["mix_norm"].append(G["mix_norm"])
        out["mlp_norm"].append(G["mlp_norm"])
        out["mlp_w1"].append(G["w1"])
        out["mlp_w2"].append(G["w2"])
        if kind == 0:
            out["sb_w_in"].append(G["w_in"])
            out["sb_w_out"].append(G["w_out"])
        elif kind == 1:
            out["fox_w_in"].append(jnp.concatenate([G["w_qkv"], G["w_f"][:, :H]], axis=1))
            out["fox_b_f"].append(G["bf"][:, :H])
            out["fox_q_gain"].append(G["gq"])
            out["fox_k_gain"].append(G["gk"])
            out["fox_w_out"].append(G["w_out"])
        else:
            rq = G["w_uq"].shape[0]
            out["mla_w_in"].append(jnp.concatenate([G["w_dq"], G["w_dkv"], G["w_dr"][:, :MLA_ROPE]], axis=1))
            out["mla_q_norm"].append(G["q_norm"])
            out["mla_kv_norm"].append(G["kv_norm"])
            out["mla_w_uq"].append(
                G["w_uq"].reshape(rq, H, MLA_QK_PAD)[:, :, :MLA_QK].reshape(rq, H * MLA_QK))
            out["mla_w_ukv"].append(G["w_ukv"])
            out["mla_q_gain"].append(G["gq"][:, :MLA_QK])
            out["mla_k_gain"].append(G["gk"][:, :MLA_QK])
            out["mla_w_out"].append(G["w_out"])
    return out


class _Place:
    def __init__(self, x, y, c):
        self.x, self.y, self.c = x, y, c
        self.dev = 4 * x + 2 * y + c
        self.chip = 2 * x + y
        self.id = (x, y, c)


def _peers(me, kind):
    if kind == "ici":
        return [_Place(1 - me.x, me.y, me.c), _Place(me.x, 1 - me.y, me.c), _Place(1 - me.x, 1 - me.y, me.c)]
    return [_Place(me.x, me.y, 1 - me.c)]


def _exchange(name, kind, operands, out_shapes, aliases, n_remote, n_local, plan):
    n_in, n_out = len(operands), len(out_shapes)

    def body(*refs):
        in_refs, out_refs = refs[:n_in], refs[n_in:n_in + n_out]
        send_sems, recv_sems, local_sems = refs[n_in + n_out:]
        me = _Place(lax.axis_index("x"), lax.axis_index("y"), lax.axis_index("c"))
        peers = _peers(me, kind)
        remote, local = plan(me, peers, in_refs, out_refs)
        assert len(remote) == n_remote and len(local) == n_local
        sends = []
        for n, (src, dst, k, _) in enumerate(remote):
            cp = pltpu.make_async_remote_copy(
                src_ref=src, dst_ref=dst, send_sem=send_sems.at[n], recv_sem=recv_sems.at[n],
                device_id=peers[k].id, device_id_type=MESH)
            cp.start()
            sends.append(cp)
        own = []
        for n, (src, dst) in enumerate(local):
            cp = pltpu.make_async_copy(src, dst, local_sems.at[n])
            cp.start()
            own.append(cp)
        for n, (src, _, k, landing) in enumerate(remote):
            pltpu.make_async_remote_copy(
                src_ref=src, dst_ref=landing, send_sem=send_sems.at[n], recv_sem=recv_sems.at[n],
                device_id=peers[k].id, device_id_type=MESH).wait_recv()
        for cp in sends:
            cp.wait_send()
        for cp in own:
            cp.wait()

    outs = pl.pallas_call(
        body, name=name, in_specs=[ANY] * n_in, out_specs=tuple([ANY] * n_out),
        out_shape=tuple(out_shapes), input_output_aliases=aliases,
        scratch_shapes=[pltpu.SemaphoreType.DMA((n_remote,)), pltpu.SemaphoreType.DMA((n_remote,)),
                        pltpu.SemaphoreType.DMA((max(n_local, 1),))],
    )(*operands)
    return list(outs)


def _window(ref, kind, d, shard_shape):
    r, c = shard_shape
    if kind == "col":
        return ref.at[:, pl.ds(pl.multiple_of(d * c, c), c)]
    return ref.at[pl.ds(pl.multiple_of(d * r, r), r), :]


def _full_shape(kind, shard_shape):
    r, c = shard_shape
    return (r, N_DEV * c) if kind == "col" else (N_DEV * r, c)


def _all_gather(shards, kinds, tag):
    n = len(shards)
    shapes = [s.shape for s in shards]
    out_shapes = [jax.ShapeDtypeStruct(_full_shape(k, s.shape), s.dtype) for k, s in zip(kinds, shards)]

    def plan_ici(me, peers, ins, outs):
        remote, local = [], []
        for a in range(n):
            mine = _window(outs[a], kinds[a], me.dev, shapes[a])
            for k, p in enumerate(peers):
                remote.append((ins[a], mine, k, _window(outs[a], kinds[a], p.dev, shapes[a])))
            local.append((ins[a], mine))
        return remote, local

    full = _exchange("gather_ici_" + tag, "ici", list(shards), out_shapes, {}, 3 * n, n, plan_ici)

    def plan_d2d(me, peers, ins, outs):
        remote = []
        for a in range(n):
            for ch in range(N_CHIP):
                held = _window(outs[a], kinds[a], 2 * ch + me.c, shapes[a])
                remote.append((held, held, 0, _window(outs[a], kinds[a], 2 * ch + 1 - me.c, shapes[a])))
        return remote, []

    return _exchange("gather_d2d_" + tag, "d2d", full, out_shapes, {a: a for a in range(n)},
                     N_CHIP * n, 0, plan_d2d)


def _pair_add(g, r4, kind, shard_shape, core, *, name):
    r, c = shard_shape
    tr = _row_tile(r, c)
    nb = r // tr
    if kind == "all":
        def body_all(core_ref, g_ref, r_ref, o_ref):
            o_ref[...] = g_ref[...] + r_ref[...]

        spec = pl.BlockSpec((tr, c), lambda i, core_ref: (i, 0))
        return pl.pallas_call(
            body_all, name=name,
            grid_spec=pltpu.PrefetchScalarGridSpec(
                num_scalar_prefetch=1, grid=(nb,), in_specs=[spec, spec], out_specs=spec),
            out_shape=jax.ShapeDtypeStruct((r, c), F32),
            compiler_params=_params(("parallel",)),
        )(core, g, r4)

    def body(core_ref, g_ref, r_ref, o_ref):
        o_ref[...] = (g_ref[...].astype(F32) + r_ref[...].astype(F32)).astype(BF16)

    if kind == "col":
        g_spec = pl.BlockSpec((tr, c), lambda ch, i, core_ref: (i, 2 * ch + core_ref[0]))
    else:
        g_spec = pl.BlockSpec((tr, c), lambda ch, i, core_ref: ((2 * ch + core_ref[0]) * nb + i, 0))
    slot = pl.BlockSpec((None, tr, c), lambda ch, i, core_ref: (ch, i, 0))
    return pl.pallas_call(
        body, name=name,
        grid_spec=pltpu.PrefetchScalarGridSpec(
            num_scalar_prefetch=1, grid=(N_CHIP, nb), in_specs=[g_spec, slot], out_specs=slot),
        out_shape=jax.ShapeDtypeStruct((N_CHIP, r, c), BF16),
        compiler_params=_params(("parallel", "parallel")),
    )(core, g, r4)


def _reduce_scatter(grads, kinds, shard_shapes, core, tag):
    n = len(grads)
    land = []
    for g, k, s in zip(grads, kinds, shard_shapes):
        land.append(jax.ShapeDtypeStruct(s if k == "all" else (N_CHIP,) + tuple(s), g.dtype))
    n_rem = sum(1 if k == "all" else N_CHIP for k in kinds)

    def plan_d2d(me, peers, ins, outs):
        remote = []
        for a in range(n):
            if kinds[a] == "all":
                remote.append((ins[a], outs[a], 0, outs[a]))
                continue
            for ch in range(N_CHIP):
                remote.append((_window(ins[a], kinds[a], 2 * ch + 1 - me.c, shard_shapes[a]),
                               outs[a].at[ch], 0, outs[a].at[ch]))
        return remote, []

    got = _exchange("scatter_d2d_" + tag, "d2d", list(grads), land, {}, n_rem, 0, plan_d2d)
    sums = [_pair_add(g, r4, k, s, core, name="pair_add_" + tag)
            for g, r4, k, s in zip(grads, got, kinds, shard_shapes)]
    land2 = [jax.ShapeDtypeStruct((N_CHIP,) + tuple(s), F32 if k == "all" else BF16)
             for k, s in zip(kinds, shard_shapes)]

    def plan_ici(me, peers, ins, outs):
        remote, local = [], []
        for a in range(n):
            for k, p in enumerate(peers):
                src = ins[a] if kinds[a] == "all" else ins[a].at[p.chip]
                remote.append((src, outs[a].at[me.chip], k, outs[a].at[p.chip]))
            src = ins[a] if kinds[a] == "all" else ins[a].at[me.chip]
            local.append((src, outs[a].at[me.chip]))
        return remote, local

    return _exchange("scatter_ici_" + tag, "ici", sums, land2, {}, 3 * n, n, plan_ici)


def _row_tile(r, c, limit=131072):
    best = None
    for t in range(8, r + 1, 8):
        if r % t == 0 and t * c <= limit:
            best = t
    return best if best is not None else r


def _adam_math(w, g, m, v):
    m = ADAM_B1 * m + (1.0 - ADAM_B1) * g
    v = ADAM_B2 * v + (1.0 - ADAM_B2) * (g * g)
    m_hat = m / (1.0 - ADAM_B1 ** ADAM_STEP)
    v_hat = v / (1.0 - ADAM_B2 ** ADAM_STEP)
    delta = -ADAM_LR * (m_hat / (jnp.sqrt(v_hat) + ADAM_EPS) + ADAM_WD * w)
    return delta, m, v


def _adamw(w, m, v, parts, layer, prev, *, name):
    L, r, c = w.shape
    tr = _row_tile(r, c)
    has_prev = prev is not None

    def body(*refs):
        w_ref, m_ref, v_ref, p_ref = refs[:4]
        g_ref, d_ref, nm_ref, nv_ref = refs[-4:]
        g = (p_ref[0].astype(F32) + p_ref[1].astype(F32)) + p_ref[2].astype(F32) + p_ref[3].astype(F32)
        delta, nm, nv = _adam_math(w_ref[...], g, m_ref[...], v_ref[...])
        g_ref[...] = g
        d_ref[...] = delta
        nm_ref[...] = nm
        nv_ref[...] = nv

    slab = pl.BlockSpec((None, tr, c), lambda i: (layer, i, 0))
    in_specs = [slab, slab, slab, pl.BlockSpec((N_CHIP, tr, c), lambda i: (0, i, 0))]
    args = [w, m, v, parts]
    aliases = {}
    if has_prev:
        in_specs += [ANY] * 4
        args += list(prev)
        aliases = {4 + n: n for n in range(4)}
    return pl.pallas_call(
        body, name=name, grid=(r // tr,), in_specs=in_specs, out_specs=(slab,) * 4,
        out_shape=tuple(jax.ShapeDtypeStruct((L, r, c), F32) for _ in range(4)),
        input_output_aliases=aliases, compiler_params=_params(("parallel",)),
    )(*args)


def _adamw_small(w, g, m, v):
    def body(w_ref, g_ref, m_ref, v_ref, d_ref, nm_ref, nv_ref):
        d_ref[...], nm_ref[...], nv_ref[...] = _adam_math(w_ref[...], g_ref[...], m_ref[...], v_ref[...])

    return pl.pallas_call(
        body, name="adamw_small", out_shape=tuple(jax.ShapeDtypeStruct(w.shape, F32) for _ in range(3)),
    )(w, g, m, v)


def _sum4(parts):
    def body(p_ref, o_ref):
        o_ref[...] = (p_ref[0] + p_ref[1]) + p_ref[2] + p_ref[3]

    return pl.pallas_call(
        body, name="sum_chips", out_shape=jax.ShapeDtypeStruct(parts.shape[1:], F32))(parts)


_WEIGHTS = ["mix_norm", "mlp_norm", "sb_w_in", "sb_w_out", "fox_w_in", "fox_b_f", "fox_q_gain", "fox_k_gain",
            "fox_w_out", "mla_w_in", "mla_q_norm", "mla_kv_norm", "mla_w_uq", "mla_w_ukv", "mla_q_gain",
            "mla_k_gain", "mla_w_out", "mlp_w1", "mlp_w2"]
_BIG = {"sb_w_in": "col", "sb_w_out": "row", "fox_w_in": "row", "fox_w_out": "row", "mla_w_in": "row",
        "mla_w_uq": "col", "mla_w_ukv": "col", "mla_w_out": "row", "mlp_w1": "col", "mlp_w2": "row"}


def _layer_big(i):
    kind, j = i % 3, i // 3
    mixer = {0: ["sb_w_in", "sb_w_out"], 1: ["fox_w_in", "fox_w_out"],
             2: ["mla_w_in", "mla_w_uq", "mla_w_ukv", "mla_w_out"]}[kind]
    return [(n, j) for n in mixer] + [("mlp_w1", i), ("mlp_w2", i)]


def _stack_to_cols(a):
    r = a.shape[0] // N_DEV
    return a.reshape(N_DEV, r, a.shape[1]).transpose(1, 0, 2).reshape(r, N_DEV * a.shape[1])


def _cols_to_stack(a):
    c = a.shape[1] // N_DEV
    return a.reshape(a.shape[0], N_DEV, c).transpose(1, 0, 2).reshape(N_DEV * a.shape[0], c)


def _pack_rows(rows, width):
    rows = [jnp.pad(r.reshape(-1).astype(F32), (0, width - r.size)) for r in rows]
    pad = (-len(rows)) % 8
    rows += [jnp.zeros((width,), F32)] * pad
    return jnp.stack(rows)


def kernel(x, positions, mix_norm, mlp_norm, sb_w_in, sb_w_out, fox_w_in, fox_b_f, fox_q_gain, fox_k_gain, fox_w_out, mla_w_in, mla_q_norm, mla_kv_norm, mla_w_uq, mla_w_ukv, mla_q_gain, mla_k_gain, mla_w_out, mlp_w1, mlp_w2, loss_target, m_mix_norm, m_mlp_norm, m_sb_w_in, m_sb_w_out, m_fox_w_in, m_fox_b_f, m_fox_q_gain, m_fox_k_gain, m_fox_w_out, m_mla_w_in, m_mla_q_norm, m_mla_kv_norm, m_mla_w_uq, m_mla_w_ukv, m_mla_q_gain, m_mla_k_gain, m_mla_w_out, m_mlp_w1, m_mlp_w2, v_mix_norm, v_mlp_norm, v_sb_w_in, v_sb_w_out, v_fox_w_in, v_fox_b_f, v_fox_q_gain, v_fox_k_gain, v_fox_w_out, v_mla_w_in, v_mla_q_norm, v_mla_kv_norm, v_mla_w_uq, v_mla_w_ukv, v_mla_q_gain, v_mla_k_gain, v_mla_w_out, v_mlp_w1, v_mlp_w2):
    w_in = dict(zip(_WEIGHTS, (mix_norm, mlp_norm, sb_w_in, sb_w_out, fox_w_in, fox_b_f, fox_q_gain, fox_k_gain, fox_w_out, mla_w_in, mla_q_norm, mla_kv_norm, mla_w_uq, mla_w_ukv, mla_q_gain, mla_k_gain, mla_w_out, mlp_w1, mlp_w2)))
    m_in = dict(zip(_WEIGHTS, (m_mix_norm, m_mlp_norm, m_sb_w_in, m_sb_w_out, m_fox_w_in, m_fox_b_f, m_fox_q_gain, m_fox_k_gain, m_fox_w_out, m_mla_w_in, m_mla_q_norm, m_mla_kv_norm, m_mla_w_uq, m_mla_w_ukv, m_mla_q_gain, m_mla_k_gain, m_mla_w_out, m_mlp_w1, m_mlp_w2)))
    v_in = dict(zip(_WEIGHTS, (v_mix_norm, v_mlp_norm, v_sb_w_in, v_sb_w_out, v_fox_w_in, v_fox_b_f, v_fox_q_gain, v_fox_k_gain, v_fox_w_out, v_mla_w_in, v_mla_q_norm, v_mla_kv_norm, v_mla_w_uq, v_mla_w_ukv, v_mla_q_gain, v_mla_k_gain, v_mla_w_out, v_mlp_w1, v_mlp_w2)))
    depth, D = mix_norm.shape
    H = D // HEAD_DIM
    n_mla = mla_w_in.shape[0]
    dev = 4 * lax.axis_index("x") + 2 * lax.axis_index("y") + lax.axis_index("c")
    core = lax.axis_index("c").astype(jnp.int32).reshape(1)

    full = {n: [None] * w_in[n].shape[0] for n in _BIG}
    for i in range(depth):
        names = _layer_big(i)
        shards = [w_in[n][j].astype(BF16) for n, j in names]
        kinds = [_BIG[n] for n, _ in names]
        if i % 3 == 2:
            jm = i // 3
            shards.append(_pack_rows([mla_q_norm[jm], mla_kv_norm[jm]], LANE))
            kinds.append("row")
        got = _all_gather(shards, kinds, f"l{i}")
        for (n, j), a in zip(names, got):
            full[n][j] = _stack_to_cols(a) if n == "fox_w_in" else a
        if i % 3 == 2:
            tiles = got[-1].reshape(N_DEV, 8, LANE)
            nq, nkv = mla_q_norm.shape[1], mla_kv_norm.shape[1]
            full.setdefault("mla_q_norm", [None] * n_mla)[jm] = tiles[:, 0, :nq].reshape(-1)
            full.setdefault("mla_kv_norm", [None] * n_mla)[jm] = tiles[:, 1, :nkv].reshape(-1)
    for n in ("mla_q_norm", "mla_kv_norm"):
        full[n] = jnp.stack(full[n]) if n in full else w_in[n]
    for n in ("mix_norm", "mlp_norm", "fox_b_f", "fox_q_gain", "fox_k_gain", "mla_q_gain", "mla_k_gain"):
        full[n] = w_in[n]

    P = _prepare(full, H)
    loss, grad_x, layer_grads = _local_step(x[0], positions[0], loss_target[0], P, depth, H)
    G = _unprepare(layer_grads, H)
    loss = lax.psum(loss[0, 0], ("x", "y", "c"))

    results = {}
    for i in reversed(range(depth)):
        names = _layer_big(i)
        grads, kinds, shapes = [], [], []
        for n, j in names:
            g = G[n][j]
            grads.append(_cols_to_stack(g) if n == "fox_w_in" else g)
            kinds.append(_BIG[n])
            shapes.append(w_in[n].shape[1:])
        parts = _reduce_scatter(grads, kinds, shapes, core, f"l{i}")
        for (n, j), p in zip(names, parts):
            results[n] = _adamw(w_in[n], m_in[n], v_in[n], p, j, results.get(n), name="adamw_" + n)

    small = [n for n in _WEIGHTS if n not in _BIG]
    rows = []
    for n in small:
        for j in range(w_in[n].shape[0]):
            rows.append(G[n][j])
    sg = _pack_rows(rows, D)
    (parts,) = _reduce_scatter([sg], ["all"], [sg.shape], core, "small")
    sgr = _sum4(parts)
    g_rows, w_rows, m_rows, v_rows, at = [], [], [], [], 0
    for n in small:
        for j in range(w_in[n].shape[0]):
            width = w_in[n].shape[1]
            if n in ("mla_q_norm", "mla_kv_norm"):
                g_rows.append(lax.dynamic_slice(sgr[at], (dev * width,), (width,)))
            else:
                g_rows.append(sgr[at, :width])
            w_rows.append(w_in[n][j])
            m_rows.append(m_in[n][j])
            v_rows.append(v_in[n][j])
            at += 1
    g_pack = _pack_rows(g_rows, D)
    d_pack, nm_pack, nv_pack = _adamw_small(
        _pack_rows(w_rows, D), g_pack, _pack_rows(m_rows, D), _pack_rows(v_rows, D))
    at = 0
    for n in small:
        L, width = w_in[n].shape
        results[n] = tuple(p[at:at + L, :width] for p in (g_pack, d_pack, nm_pack, nv_pack))
        at += L

    out = [loss, grad_x[None]]
    for part in range(4):
        out += [results[n][part] for n in _WEIGHTS]
    return tuple(out)
```

```python
import functools
import math

import jax
import jax.numpy as jnp
from jax import lax
from jax.experimental import pallas as pl
from jax.experimental.pallas import tpu as pltpu

F32 = jnp.float32
BF16 = jnp.bfloat16

HEAD_DIM = 128
MLA_NOPE = 128
MLA_ROPE = 64
MLA_V = 128
MLA_QK = MLA_NOPE + MLA_ROPE
MLA_QK_PAD = 256
LANE = 128
ROPE_THETA = 10000.0
EPS = 1e-6
ADAM_LR = 0.001
ADAM_B1 = 0.9
ADAM_B2 = 0.999
ADAM_EPS = 1e-08
ADAM_WD = 0.01
ADAM_STEP = 10
N_DEV = 8
N_CHIP = 4
NEG = -1e30
VMEM_LIMIT = 56 * 1024 * 1024
ATT_BLOCK = 256
MESH = pl.DeviceIdType.MESH
ANY = pl.BlockSpec(memory_space=pl.ANY)


def _params(sem):
    return pltpu.CompilerParams(dimension_semantics=sem, vmem_limit_bytes=VMEM_LIMIT)


def _tile(dim, pref):
    if dim <= pref:
        return dim
    t = pref
    while dim % t:
        t -= LANE
    return t


def _dot(a, b, dims):
    return lax.dot_general(a, b, (dims, ((), ())), preferred_element_type=F32)


NN = ((1,), (0,))
NT = ((1,), (1,))
TN = ((0,), (0,))


def _mm(a, b, *, mode="nn", out_dtype=F32, res=None, act=None, z=None, name,
        tm=1024, tn=512, tk=2048, deps=()):
    if mode == "nn":
        (M, K), (_, N) = a.shape, b.shape
    elif mode == "nt":
        (M, K), (N, _) = a.shape, b.shape
    else:
        (K, M), (_, N) = a.shape, b.shape
    tm, tn, tk = _tile(M, tm), _tile(N, tn), _tile(K, tk)
    nk = K // tk
    if mode == "tn":
        a_spec = pl.BlockSpec((tk, tm), lambda i, j, k: (k, i))
    else:
        a_spec = pl.BlockSpec((tm, tk), lambda i, j, k: (i, k))
    if mode == "nt":
        b_spec = pl.BlockSpec((tn, tk), lambda i, j, k: (j, k))
    else:
        b_spec = pl.BlockSpec((tk, tn), lambda i, j, k: (k, j))
    dims = {"nn": NN, "nt": NT, "tn": TN}[mode]
    o_spec = pl.BlockSpec((tm, tn), lambda i, j, k: (i, j))
    in_specs, args = [a_spec, b_spec], [a, b]
    if res is not None:
        in_specs.append(o_spec)
        args.append(res)
    if act == "drelu2":
        in_specs.append(o_spec)
        args.append(z)
    if act == "relu2":
        out_shape = (jax.ShapeDtypeStruct((M, N), F32), jax.ShapeDtypeStruct((M, N), BF16))
        out_specs = (o_spec, o_spec)
    else:
        out_shape = jax.ShapeDtypeStruct((M, N), out_dtype)
        out_specs = o_spec
    has_res, has_z = res is not None, act == "drelu2"
    n_out = 2 if act == "relu2" else 1
    in_specs += [ANY] * len(deps)
    args += list(deps)

    def body(*refs):
        a_ref, b_ref = refs[0], refs[1]
        idx = 2
        res_ref = z_ref = None
        if has_res:
            res_ref = refs[idx]
            idx += 1
        if has_z:
            z_ref = refs[idx]
            idx += 1
        idx += len(deps)
        outs = refs[idx:idx + n_out]

        def finish(r):
            if has_res:
                r = r + res_ref[...]
            if act == "relu2":
                outs[0][...] = r
                rr = jnp.maximum(r, 0.0)
                outs[1][...] = (rr * rr).astype(BF16)
            elif act == "drelu2":
                outs[0][...] = (r * (2.0 * jnp.maximum(z_ref[...], 0.0))).astype(out_dtype)
            else:
                outs[0][...] = r.astype(out_dtype)

        if nk == 1:
            finish(_dot(a_ref[...], b_ref[...], dims))
            return
        acc = refs[-1]
        k = pl.program_id(2)

        @pl.when(k == 0)
        def _():
            acc[...] = _dot(a_ref[...], b_ref[...], dims)

        @pl.when((k > 0) & (k < nk - 1))
        def _():
            acc[...] += _dot(a_ref[...], b_ref[...], dims)

        @pl.when(k == nk - 1)
        def _():
            finish(acc[...] + _dot(a_ref[...], b_ref[...], dims))

    return pl.pallas_call(
        body, name=name, grid=(M // tm, N // tn, nk), in_specs=in_specs, out_specs=out_specs,
        out_shape=out_shape, scratch_shapes=[pltpu.VMEM((tm, tn), F32)] if nk > 1 else [],
        compiler_params=_params(("parallel", "parallel", "arbitrary")),
    )(*args)


def _rmsnorm_fwd(x, g, *, name, tm=256):
    T, n = x.shape
    tm = _tile(T, tm)

    def body(x_ref, g_ref, o_ref):
        xf = x_ref[...]
        r = lax.rsqrt(jnp.mean(xf * xf, axis=-1, keepdims=True) + EPS)
        o_ref[...] = (xf * r * g_ref[...]).astype(BF16)

    return pl.pallas_call(
        body, name=name, grid=(T // tm,),
        in_specs=[pl.BlockSpec((tm, n), lambda i: (i, 0)), pl.BlockSpec((1, n), lambda i: (0, 0))],
        out_specs=pl.BlockSpec((tm, n), lambda i: (i, 0)),
        out_shape=jax.ShapeDtypeStruct((T, n), BF16),
        compiler_params=_params(("parallel",)),
    )(x, g)


def _rmsnorm_bwd(x, g, dy, dx_in=None, *, name, want_f32=True, tm=256):
    T, n = x.shape
    tm = _tile(T, tm)
    has_in = dx_in is not None
    row = pl.BlockSpec((tm, n), lambda i: (i, 0))
    vec = pl.BlockSpec((1, n), lambda i: (0, 0))

    def body(*refs):
        x_ref, g_ref, dy_ref = refs[:3]
        idx = 3
        in_ref = None
        if has_in:
            in_ref = refs[idx]
            idx += 1
        outs = refs[idx:]
        xf = x_ref[...]
        r = lax.rsqrt(jnp.mean(xf * xf, axis=-1, keepdims=True) + EPS)
        dyf = dy_ref[...]
        t = dyf * g_ref[...]
        dx = r * t - xf * (r * r * r * jnp.mean(t * xf, axis=-1, keepdims=True))
        if has_in:
            dx = dx + in_ref[...]
        o = 0
        if want_f32:
            outs[0][...] = dx
            o = 1
        outs[o][...] = dx.astype(BF16)
        dg_ref = outs[o + 1]

        @pl.when(pl.program_id(0) == 0)
        def _():
            dg_ref[...] = jnp.zeros_like(dg_ref)

        dg_ref[...] += jnp.sum(dyf * xf * r, axis=0, keepdims=True)

    in_specs = [row, vec, row] + ([row] if has_in else [])
    out_specs, out_shape = [], []
    if want_f32:
        out_specs.append(row)
        out_shape.append(jax.ShapeDtypeStruct((T, n), F32))
    out_specs += [row, vec]
    out_shape += [jax.ShapeDtypeStruct((T, n), BF16), jax.ShapeDtypeStruct((1, n), F32)]
    args = [x, g, dy] + ([dx_in] if has_in else [])
    return pl.pallas_call(
        body, name=name, grid=(T // tm,), in_specs=in_specs, out_specs=tuple(out_specs),
        out_shape=tuple(out_shape), compiler_params=_params(("arbitrary",)),
    )(*args)


def _loss_head(y, target, *, tm=256):
    T, n = y.shape
    tm = _tile(T, tm)
    row = pl.BlockSpec((tm, n), lambda i: (i, 0))

    def body(y_ref, t_ref, dy_ref, dyb_ref, loss_ref):
        err = y_ref[...] - t_ref[...]
        dy = err * (1.0 / n)
        dy_ref[...] = dy
        dyb_ref[...] = dy.astype(BF16)

        @pl.when(pl.program_id(0) == 0)
        def _():
            loss_ref[...] = jnp.zeros_like(loss_ref)

        part = 0.5 * jnp.sum(jnp.mean(err * err, axis=-1, keepdims=True), axis=0, keepdims=True)
        loss_ref[...] += part

    return pl.pallas_call(
        body, name="loss_head", grid=(T // tm,), in_specs=[row, row],
        out_specs=(row, row, pl.BlockSpec((1, 1), lambda i: (0, 0))),
        out_shape=(jax.ShapeDtypeStruct((T, n), F32), jax.ShapeDtypeStruct((T, n), BF16),
                   jax.ShapeDtypeStruct((1, 1), F32)),
        compiler_params=_params(("arbitrary",)),
    )(y, target)


def _split3(v):
    hi = v.astype(BF16)
    r1 = v - hi.astype(F32)
    mid = r1.astype(BF16)
    lo = (r1 - mid.astype(F32)).astype(BF16)
    return hi, mid, lo


def _tri_right(v, tri):
    hi, mid, lo = _split3(v)
    return _dot(hi, tri, NN) + _dot(mid, tri, NN) + _dot(lo, tri, NN)


def _tri_left(tri, v):
    hi, mid, lo = _split3(v)
    return _dot(tri, hi, NN) + _dot(tri, mid, NN) + _dot(tri, lo, NN)


def _iota2(shape, dim):
    return lax.broadcasted_iota(jnp.int32, shape, dim)


def _log_sigmoid(z):
    return jnp.minimum(z, 0.0) - jnp.log1p(jnp.exp(-jnp.abs(z)))


def _head_spec(rows, width, off):
    return pl.BlockSpec((rows, width), lambda h, i: (i, off + h))


def _full_head_spec(T, width, off):
    return pl.BlockSpec((T, width), lambda h, i: (0, off + h))


def _sb_fwd(qkv, H):
    T = qkv.shape[0]
    B = _tile(T, ATT_BLOCK)
    nq = T // B
    scale = 1.0 / math.sqrt(HEAD_DIM)

    def body(q_ref, k_ref, v_ref, o_ref, c_ref):
        i = pl.program_id(1)
        q = q_ref[...]
        rows = _iota2((B, B), 0)
        cols = _iota2((B, B), 1)
        tri = (rows > cols).astype(BF16)

        def step(n, carry):
            acc, run = carry
            j = i - n
            ks = pl.multiple_of(j * B, B)
            kb = k_ref[pl.ds(ks, B), :]
            vb = v_ref[pl.ds(ks, B), :]
            z = _dot(q, kb, NT) * scale
            mask = (ks + cols) < (i * B + rows)
            ls = _log_sigmoid(z)
            lk = jnp.where(mask, ls - z, 0.0)
            later = _tri_right(lk, tri)
            a = jnp.where(mask, jnp.exp(ls + later), 0.0)
            acc = acc + jnp.exp(run) * _dot(a.astype(BF16), vb, NN)
            run = run + jnp.sum(lk, axis=1, keepdims=True)
            return acc, run

        acc, run = lax.fori_loop(
            0, i + 1, step, (jnp.zeros((B, HEAD_DIM), F32), jnp.zeros((B, 1), F32)))
        o_ref[...] = acc.astype(BF16)
        c_ref[...] = run

    return pl.pallas_call(
        body, name="sb_attn_fwd", grid=(H, nq),
        in_specs=[_head_spec(B, HEAD_DIM, 0), _full_head_spec(T, HEAD_DIM, H),
                  _full_head_spec(T, HEAD_DIM, 2 * H)],
        out_specs=(_head_spec(B, HEAD_DIM, 0), pl.BlockSpec((None, B, 1), lambda h, i: (h, i, 0))),
        out_shape=(jax.ShapeDtypeStruct((T, H * HEAD_DIM), BF16), jax.ShapeDtypeStruct((H, T, 1), F32)),
        compiler_params=_params(("parallel", "arbitrary")),
    )(qkv, qkv, qkv)


def _sb_bwd(qkv, do, ctot, H):
    T = qkv.shape[0]
    B = _tile(T, ATT_BLOCK)
    nq = T // B
    scale = 1.0 / math.sqrt(HEAD_DIM)

    def body(q_ref, k_ref, v_ref, do_ref, c_ref, dq_ref, dk_ref, dv_ref, dk_acc, dv_acc):
        i = pl.program_id(1)

        @pl.when(i == 0)
        def _():
            dk_acc[...] = jnp.zeros_like(dk_acc)
            dv_acc[...] = jnp.zeros_like(dv_acc)

        q = q_ref[...]
        do_b = do_ref[...]
        ctot_b = c_ref[...]
        rows = _iota2((B, B), 0)
        cols = _iota2((B, B), 1)
        tri_incl = (rows <= cols).astype(BF16)
        tri_strict = (rows < cols).astype(BF16)

        def step(j, carry):
            dq, lpre, gpre = carry
            ks = pl.multiple_of(j * B, B)
            kb = k_ref[pl.ds(ks, B), :]
            vb = v_ref[pl.ds(ks, B), :]
            z = _dot(q, kb, NT) * scale
            mask = (ks + cols) < (i * B + rows)
            ls = _log_sigmoid(z)
            lk_all = ls - z
            lk = jnp.where(mask, lk_all, 0.0)
            later = ctot_b - lpre - _tri_right(lk, tri_incl)
            a = jnp.where(mask, jnp.exp(ls + later), 0.0)
            da = _dot(do_b, vb, NT)
            g = a * da
            gex = gpre + _tri_right(g, tri_strict)
            dz = jnp.where(mask, g * jnp.exp(lk_all) - jnp.exp(ls) * gex, 0.0) * scale
            dzb = dz.astype(BF16)
            dq = dq + _dot(dzb, kb, NN)
            dk_acc[pl.ds(ks, B), :] += _dot(dzb, q, TN)
            dv_acc[pl.ds(ks, B), :] += _dot(a.astype(BF16), do_b, TN)
            lpre = lpre + jnp.sum(lk, axis=1, keepdims=True)
            gpre = gpre + jnp.sum(g, axis=1, keepdims=True)
            return dq, lpre, gpre

        dq, _, _ = lax.fori_loop(
            0, i + 1, step,
            (jnp.zeros((B, HEAD_DIM), F32), jnp.zeros((B, 1), F32), jnp.zeros((B, 1), F32)))
        dq_ref[...] = dq.astype(BF16)

        @pl.when(i == nq - 1)
        def _():
            dk_ref[...] = dk_acc[...].astype(BF16)
            dv_ref[...] = dv_acc[...].astype(BF16)

    W = H * HEAD_DIM
    return pl.pallas_call(
        body, name="sb_attn_bwd", grid=(H, nq),
        in_specs=[_head_spec(B, HEAD_DIM, 0), _full_head_spec(T, HEAD_DIM, H),
                  _full_head_spec(T, HEAD_DIM, 2 * H), _head_spec(B, HEAD_DIM, 0),
                  pl.BlockSpec((None, B, 1), lambda h, i: (h, i, 0))],
        out_specs=(_head_spec(B, HEAD_DIM, 0), _full_head_spec(T, HEAD_DIM, 0),
                   _full_head_spec(T, HEAD_DIM, 0)),
        out_shape=tuple(jax.ShapeDtypeStruct((T, W), BF16) for _ in range(3)),
        scratch_shapes=[pltpu.VMEM((T, HEAD_DIM), F32), pltpu.VMEM((T, HEAD_DIM), F32)],
        compiler_params=_params(("parallel", "arbitrary")),
    )(qkv, qkv, qkv, do, ctot)


def _softmax_fwd(q, k, v, cf, H, dqk, scale, *, name):
    T = q.shape[0]
    B = _tile(T, ATT_BLOCK)
    nq = T // B
    has_cf = cf is not None

    def body(*refs):
        q_ref, k_ref, v_ref = refs[:3]
        idx = 3
        if has_cf:
            cfc_ref, cfr_ref = refs[3], refs[4]
            idx = 5
        o_ref, of_ref, lse_ref = refs[idx:idx + 3]
        i = pl.program_id(1)
        qb = q_ref[...]
        rows = _iota2((B, B), 0)
        cols = _iota2((B, B), 1)

        def step(j, carry):
            m, l, acc = carry
            ks = pl.multiple_of(j * B, B)
            kb = k_ref[pl.ds(ks, B), :]
            vb = v_ref[pl.ds(ks, B), :]
            s = _dot(qb, kb, NT) * scale
            if has_cf:
                s = s + (cfc_ref[...] - cfr_ref[:, pl.ds(ks, B)])
            s = jnp.where((ks + cols) <= (i * B + rows), s, NEG)
            m_new = jnp.maximum(m, jnp.max(s, axis=1, keepdims=True))
            alpha = jnp.exp(m - m_new)
            p = jnp.exp(s - m_new)
            l = alpha * l + jnp.sum(p, axis=1, keepdims=True)
            acc = alpha * acc + _dot(p.astype(BF16), vb, NN)
            return m_new, l, acc

        m, l, acc = lax.fori_loop(
            0, i + 1, step,
            (jnp.full((B, 1), NEG, F32), jnp.zeros((B, 1), F32), jnp.zeros((B, HEAD_DIM), F32)))
        o = acc / l
        o_ref[...] = o.astype(BF16)
        of_ref[...] = o
        lse_ref[...] = m + jnp.log(l)

    stat = pl.BlockSpec((None, B, 1), lambda h, i: (h, i, 0))
    in_specs = [_head_spec(B, dqk, 0), _full_head_spec(T, dqk, 0), _full_head_spec(T, HEAD_DIM, 0)]
    args = [q, k, v]
    if has_cf:
        in_specs += [stat, pl.BlockSpec((None, 1, T), lambda h, i: (h, 0, 0))]
        args += list(cf)
    W = H * HEAD_DIM
    return pl.pallas_call(
        body, name=name, grid=(H, nq), in_specs=in_specs,
        out_specs=(_head_spec(B, HEAD_DIM, 0), _head_spec(B, HEAD_DIM, 0), stat),
        out_shape=(jax.ShapeDtypeStruct((T, W), BF16), jax.ShapeDtypeStruct((T, W), F32),
                   jax.ShapeDtypeStruct((H, T, 1), F32)),
        compiler_params=_params(("parallel", "arbitrary")),
    )(*args)


def _softmax_bwd(q, k, v, cf, do, o, lse, H, dqk, scale, *, name):
    T = q.shape[0]
    B = _tile(T, ATT_BLOCK)
    nq = T // B
    has_cf = cf is not None

    def body(*refs):
        q_ref, k_ref, v_ref, do_ref, o_ref, lse_ref = refs[:6]
        idx = 6
        if has_cf:
            cfc_ref, cfr_ref = refs[6], refs[7]
            idx = 8
        dq_ref, dk_ref, dv_ref = refs[idx:idx + 3]
        idx += 3
        if has_cf:
            dcc_ref, dcr_ref = refs[idx], refs[idx + 1]
        i = pl.program_id(1)

        @pl.when(i == 0)
        def _():
            dk_ref[...] = jnp.zeros_like(dk_ref)
            dv_ref[...] = jnp.zeros_like(dv_ref)
            if has_cf:
                dcr_ref[...] = jnp.zeros_like(dcr_ref)

        qb = q_ref[...]
        do_b = do_ref[...]
        lse_b = lse_ref[...]
        delta = jnp.sum(do_b.astype(F32) * o_ref[...], axis=1, keepdims=True)
        rows = _iota2((B, B), 0)
        cols = _iota2((B, B), 1)

        def step(j, carry):
            dq, rs = carry
            ks = pl.multiple_of(j * B, B)
            kb = k_ref[pl.ds(ks, B), :]
            vb = v_ref[pl.ds(ks, B), :]
            s = _dot(qb, kb, NT) * scale
            if has_cf:
                s = s + (cfc_ref[...] - cfr_ref[:, pl.ds(ks, B)])
            p = jnp.where((ks + cols) <= (i * B + rows), jnp.exp(s - lse_b), 0.0)
            dp = _dot(do_b, vb, NT)
            ds = p * (dp - delta)
            dsb = (ds * scale).astype(BF16)
            dq = dq + _dot(dsb, kb, NN)
            dk_ref[pl.ds(ks, B), :] += _dot(dsb, qb, TN)
            dv_ref[pl.ds(ks, B), :] += _dot(p.astype(BF16), do_b, TN)
            if has_cf:
                rs = rs + jnp.sum(ds, axis=1, keepdims=True)
                dcr_ref[:, pl.ds(ks, B)] -= jnp.sum(ds, axis=0, keepdims=True)
            return dq, rs

        dq, rs = lax.fori_loop(0, i + 1, step, (jnp.zeros((B, dqk), F32), jnp.zeros((B, 1), F32)))
        dq_ref[...] = dq
        if has_cf:
            dcc_ref[...] = rs

    stat = pl.BlockSpec((None, B, 1), lambda h, i: (h, i, 0))
    rowstat = pl.BlockSpec((None, 1, T), lambda h, i: (h, 0, 0))
    in_specs = [_head_spec(B, dqk, 0), _full_head_spec(T, dqk, 0), _full_head_spec(T, HEAD_DIM, 0),
                _head_spec(B, HEAD_DIM, 0), _head_spec(B, HEAD_DIM, 0), stat]
    args = [q, k, v, do, o, lse]
    out_specs = [_head_spec(B, dqk, 0), _full_head_spec(T, dqk, 0), _full_head_spec(T, HEAD_DIM, 0)]
    out_shape = [jax.ShapeDtypeStruct((T, H * dqk), F32), jax.ShapeDtypeStruct((T, H * dqk), F32),
                 jax.ShapeDtypeStruct((T, H * HEAD_DIM), F32)]
    if has_cf:
        in_specs += [stat, rowstat]
        args += list(cf)
        out_specs += [stat, rowstat]
        out_shape += [jax.ShapeDtypeStruct((H, T, 1), F32), jax.ShapeDtypeStruct((H, 1, T), F32)]
    return pl.pallas_call(
        body, name=name, grid=(H, nq), in_specs=in_specs, out_specs=tuple(out_specs),
        out_shape=tuple(out_shape), compiler_params=_params(("parallel", "arbitrary")),
    )(*args)


def _headnorm(x, g):
    r = lax.rsqrt(jnp.mean(x * x, axis=-1, keepdims=True) + EPS)
    return x * r * g, r


def _headnorm_bwd(x, g, dy, n):
    r = lax.rsqrt(jnp.sum(x * x, axis=-1, keepdims=True) * (1.0 / n) + EPS)
    t = dy * g
    dx = r * t - x * (r * r * r * (jnp.sum(t * x, axis=-1, keepdims=True) * (1.0 / n)))
    dg = jnp.sum(dy * x * r, axis=0, keepdims=True)
    return dx, dg


def _fox_prep_fwd(qkv, gq, gk, H, *, tm=512):
    T = qkv.shape[0]
    tm = _tile(T, tm)

    def body(q_ref, k_ref, v_ref, gq_ref, gk_ref, qn_ref, kn_ref, vb_ref):
        qn_ref[...] = _headnorm(q_ref[...], gq_ref[...])[0].astype(BF16)
        kn_ref[...] = _headnorm(k_ref[...], gk_ref[...])[0].astype(BF16)
        vb_ref[...] = v_ref[...].astype(BF16)

    def blk(off):
        return pl.BlockSpec((tm, HEAD_DIM), lambda i, h: (i, off + h))

    vec = pl.BlockSpec((1, HEAD_DIM), lambda i, h: (0, 0))
    W = H * HEAD_DIM
    return pl.pallas_call(
        body, name="fox_prep_fwd", grid=(T // tm, H),
        in_specs=[blk(0), blk(H), blk(2 * H), vec, vec], out_specs=(blk(0), blk(0), blk(0)),
        out_shape=tuple(jax.ShapeDtypeStruct((T, W), BF16) for _ in range(3)),
        compiler_params=_params(("parallel", "parallel")),
    )(qkv, qkv, qkv, gq, gk)


def _fox_prep_bwd(qkv, gq, gk, dqn, dkn, dv, H, *, tm=512):
    T = qkv.shape[0]
    tm = _tile(T, tm)

    def body(q_ref, k_ref, gq_ref, gk_ref, dqn_ref, dkn_ref, dv_ref,
             dq_ref, dk_ref, dvb_ref, dgq_ref, dgk_ref):
        @pl.when((pl.program_id(0) == 0) & (pl.program_id(1) == 0))
        def _():
            dgq_ref[...] = jnp.zeros_like(dgq_ref)
            dgk_ref[...] = jnp.zeros_like(dgk_ref)

        dq, dgq = _headnorm_bwd(q_ref[...], gq_ref[...], dqn_ref[...], HEAD_DIM)
        dk, dgk = _headnorm_bwd(k_ref[...], gk_ref[...], dkn_ref[...], HEAD_DIM)
        dq_ref[...] = dq.astype(BF16)
        dk_ref[...] = dk.astype(BF16)
        dvb_ref[...] = dv_ref[...].astype(BF16)
        dgq_ref[...] += dgq
        dgk_ref[...] += dgk

    def blk(off):
        return pl.BlockSpec((tm, HEAD_DIM), lambda i, h: (i, off + h))

    vec = pl.BlockSpec((1, HEAD_DIM), lambda i, h: (0, 0))
    W = H * HEAD_DIM
    return pl.pallas_call(
        body, name="fox_prep_bwd", grid=(T // tm, H),
        in_specs=[blk(0), blk(H), vec, vec, blk(0), blk(0), blk(0)],
        out_specs=(blk(0), blk(0), blk(0), vec, vec),
        out_shape=tuple(jax.ShapeDtypeStruct((T, W), BF16) for _ in range(3))
        + (jax.ShapeDtypeStruct((1, HEAD_DIM), F32), jax.ShapeDtypeStruct((1, HEAD_DIM), F32)),
        compiler_params=_params(("arbitrary", "arbitrary")),
    )(qkv, qkv, gq, gk, dqn, dkn, dv)


def _fox_gate_fwd(flog, bf, *, tm=256):
    T = flog.shape[0]
    tm = _tile(T, tm)

    def body(f_ref, b_ref, cf_ref, carry):
        @pl.when(pl.program_id(0) == 0)
        def _():
            carry[...] = jnp.zeros_like(carry)

        lf = _log_sigmoid(f_ref[...] + b_ref[...])
        tri = (_iota2((tm, tm), 1) <= _iota2((tm, tm), 0)).astype(BF16)
        cf_ref[...] = carry[...] + _tri_left(tri, lf)
        carry[...] += jnp.sum(lf, axis=0, keepdims=True)

    return pl.pallas_call(
        body, name="fox_gate_fwd", grid=(T // tm,),
        in_specs=[pl.BlockSpec((tm, LANE), lambda i: (i, 0)), pl.BlockSpec((1, LANE), lambda i: (0, 0))],
        out_specs=pl.BlockSpec((tm, LANE), lambda i: (i, 0)),
        out_shape=jax.ShapeDtypeStruct((T, LANE), F32),
        scratch_shapes=[pltpu.VMEM((1, LANE), F32)],
        compiler_params=_params(("arbitrary",)),
    )(flog, bf)


def _fox_gate_bwd(flog, bf, dcf, *, tm=256):
    T = flog.shape[0]
    tm = _tile(T, tm)
    nt = T // tm

    def body(f_ref, b_ref, dcf_ref, df_ref, db_ref, carry):
        @pl.when(pl.program_id(0) == 0)
        def _():
            carry[...] = jnp.zeros_like(carry)
            db_ref[...] = jnp.zeros_like(db_ref)

        d = dcf_ref[...]
        tri = (_iota2((tm, tm), 1) >= _iota2((tm, tm), 0)).astype(BF16)
        dlf = carry[...] + _tri_left(tri, d)
        carry[...] += jnp.sum(d, axis=0, keepdims=True)
        xg = f_ref[...] + b_ref[...]
        e = jnp.exp(-jnp.abs(xg))
        sig_neg = jnp.where(xg >= 0.0, e, 1.0) / (1.0 + e)
        df = dlf * sig_neg
        df_ref[...] = df.astype(BF16)
        db_ref[...] += jnp.sum(df, axis=0, keepdims=True)

    rev = pl.BlockSpec((tm, LANE), lambda i: (nt - 1 - i, 0))
    vec = pl.BlockSpec((1, LANE), lambda i: (0, 0))
    return pl.pallas_call(
        body, name="fox_gate_bwd", grid=(nt,), in_specs=[rev, vec, rev], out_specs=(rev, vec),
        out_shape=(jax.ShapeDtypeStruct((T, LANE), BF16), jax.ShapeDtypeStruct((1, LANE), F32)),
        scratch_shapes=[pltpu.VMEM((1, LANE), F32)],
        compiler_params=_params(("arbitrary",)),
    )(flog, bf, dcf)


def _rope_tables(positions):
    half = MLA_ROPE // 2
    inv_freq = ROPE_THETA ** (-jnp.arange(0, half, dtype=F32) * 2.0 / MLA_ROPE)
    ang = positions.astype(F32)[:, None] * inv_freq
    cos, sin = jnp.cos(ang), jnp.sin(ang)
    zero = jnp.zeros_like(cos)
    pad = jnp.zeros((positions.shape[0], LANE - MLA_ROPE), F32)
    cos_t = jnp.concatenate([cos, cos, pad], axis=1)
    sin_up = jnp.concatenate([zero, sin, pad], axis=1)
    sin_dn = jnp.concatenate([-sin, zero, pad], axis=1)
    return cos_t, sin_up, sin_dn


def _rope(x, cos_t, sin_up, sin_dn):
    half = MLA_ROPE // 2
    return x * cos_t + pltpu.roll(x, half, 1) * sin_up + pltpu.roll(x, LANE - half, 1) * sin_dn


def _rope_t(d, cos_t, sin_up, sin_dn):
    half = MLA_ROPE // 2
    return d * cos_t + pltpu.roll(d * sin_up, LANE - half, 1) + pltpu.roll(d * sin_dn, half, 1)


def _norm192(xcat, g):
    r = lax.rsqrt(jnp.sum(xcat * xcat, axis=-1, keepdims=True) * (1.0 / MLA_QK) + EPS)
    return xcat * r * g


def _mla_prep_fwd(qfull, kv, kr, tabs, gq, gk, H, *, tm=512):
    T = qfull.shape[0]
    tm = _tile(T, tm)

    def body(q_ref, kv_ref, kr_ref, c_ref, su_ref, sd_ref, gq_ref, gk_ref, qf_ref, kf_ref, v_ref):
        tabs_b = (c_ref[...], su_ref[...], sd_ref[...])
        qb = q_ref[...]
        qcat = jnp.concatenate([qb[:, :MLA_NOPE], _rope(qb[:, MLA_NOPE:], *tabs_b)], axis=1)
        qf_ref[...] = _norm192(qcat, gq_ref[...]).astype(BF16)
        kvb = kv_ref[...]
        kcat = jnp.concatenate([kvb[:, :MLA_NOPE], _rope(kr_ref[...], *tabs_b)], axis=1)
        kf_ref[...] = _norm192(kcat, gk_ref[...]).astype(BF16)
        v_ref[...] = kvb[:, MLA_NOPE:].astype(BF16)

    head = pl.BlockSpec((tm, MLA_QK_PAD), lambda i, h: (i, h))
    tok = pl.BlockSpec((tm, LANE), lambda i, h: (i, 0))
    vec = pl.BlockSpec((1, MLA_QK_PAD), lambda i, h: (0, 0))
    return pl.pallas_call(
        body, name="mla_prep_fwd", grid=(T // tm, H),
        in_specs=[head, head, tok, tok, tok, tok, vec, vec],
        out_specs=(head, head, pl.BlockSpec((tm, MLA_V), lambda i, h: (i, h))),
        out_shape=(jax.ShapeDtypeStruct((T, H * MLA_QK_PAD), BF16),
                   jax.ShapeDtypeStruct((T, H * MLA_QK_PAD), BF16),
                   jax.ShapeDtypeStruct((T, H * MLA_V), BF16)),
        compiler_params=_params(("parallel", "parallel")),
    )(qfull, kv, kr, *tabs, gq, gk)


def _mla_prep_bwd(qfull, kv, kr, tabs, gq, gk, dqf, dkf, dv, H, *, tm=512):
    T = qfull.shape[0]
    tm = _tile(T, tm)

    def body(q_ref, kv_ref, kr_ref, c_ref, su_ref, sd_ref, gq_ref, gk_ref, dqf_ref, dkf_ref, dv_ref,
             dq_ref, dkv_ref, dkr_ref, dgq_ref, dgk_ref, kr_acc):
        h = pl.program_id(1)

        @pl.when((pl.program_id(0) == 0) & (h == 0))
        def _():
            dgq_ref[...] = jnp.zeros_like(dgq_ref)
            dgk_ref[...] = jnp.zeros_like(dgk_ref)

        @pl.when(h == 0)
        def _():
            kr_acc[...] = jnp.zeros_like(kr_acc)

        tabs_b = (c_ref[...], su_ref[...], sd_ref[...])
        qb = q_ref[...]
        qcat = jnp.concatenate([qb[:, :MLA_NOPE], _rope(qb[:, MLA_NOPE:], *tabs_b)], axis=1)
        dqcat, dgq = _headnorm_bwd(qcat, gq_ref[...], dqf_ref[...], MLA_QK)
        dq_ref[...] = jnp.concatenate(
            [dqcat[:, :MLA_NOPE], _rope_t(dqcat[:, MLA_NOPE:], *tabs_b)], axis=1).astype(BF16)
        dgq_ref[...] += dgq
        kvb = kv_ref[...]
        kcat = jnp.concatenate([kvb[:, :MLA_NOPE], _rope(kr_ref[...], *tabs_b)], axis=1)
        dkcat, dgk = _headnorm_bwd(kcat, gk_ref[...], dkf_ref[...], MLA_QK)
        dkv_ref[...] = jnp.concatenate([dkcat[:, :MLA_NOPE], dv_ref[...]], axis=1).astype(BF16)
        dgk_ref[...] += dgk
        kr_acc[...] += dkcat[:, MLA_NOPE:]

        @pl.when(h == H - 1)
        def _():
            dkr_ref[...] = _rope_t(kr_acc[...], *tabs_b).astype(BF16)

    head = pl.BlockSpec((tm, MLA_QK_PAD), lambda i, h: (i, h))
    tok = pl.BlockSpec((tm, LANE), lambda i, h: (i, 0))
    vec = pl.BlockSpec((1, MLA_QK_PAD), lambda i, h: (0, 0))
    return pl.pallas_call(
        body, name="mla_prep_bwd", grid=(T // tm, H),
        in_specs=[head, head, tok, tok, tok, tok, vec, vec, head, head,
                  pl.BlockSpec((tm, MLA_V), lambda i, h: (i, h))],
        out_specs=(head, head, tok, vec, vec),
        out_shape=(jax.ShapeDtypeStruct((T, H * MLA_QK_PAD), BF16),
                   jax.ShapeDtypeStruct((T, H * MLA_QK_PAD), BF16),
                   jax.ShapeDtypeStruct((T, LANE), BF16),
                   jax.ShapeDtypeStruct((1, MLA_QK_PAD), F32), jax.ShapeDtypeStruct((1, MLA_QK_PAD), F32)),
        scratch_shapes=[pltpu.VMEM((tm, LANE), F32)],
        compiler_params=_params(("arbitrary", "arbitrary")),
    )(qfull, kv, kr, *tabs, gq, gk, dqf, dkf, dv)


def _cf_layouts(cf, H):
    cfh = cf[:, :H].T
    return cfh[:, :, None], cfh[:, None, :]


def _layer_fwd(i, x, P, tabs, H, hook=None):
    kind, j = i % 3, i // 3
    s = {"x_in": x}
    h = _rmsnorm_fwd(x, P["mix_norm"][i:i + 1], name="mix_norm_fwd")
    s["h"] = h
    if kind == 0:
        w = P["sb"][j]
        qkv = _mm(h, w["w_in"], out_dtype=BF16, name="sb_qkv")
        o, ctot = _sb_fwd(qkv, H)
        s.update(qkv=qkv, o=o, ctot=ctot)
    elif kind == 1:
        w = P["fox"][j]
        qkv = _mm(h, w["w_qkv"], name="fox_qkv")
        flog = _mm(h, w["w_f"], name="fox_gate_proj")
        qn, kn, vb = _fox_prep_fwd(qkv, w["gq"], w["gk"], H)
        cf = _cf_layouts(_fox_gate_fwd(flog, w["bf"]), H)
        o, of, lse = _softmax_fwd(qn, kn, vb, cf, H, HEAD_DIM, 1.0 / math.sqrt(HEAD_DIM), name="fox_attn_fwd")
        s.update(qkv=qkv, flog=flog, qn=qn, kn=kn, vb=vb, cf=cf, o=o, of=of, lse=lse)
    else:
        w = P["mla"][j]
        dq = _mm(h, w["w_dq"], name="mla_down_q")
        dkv = _mm(h, w["w_dkv"], name="mla_down_kv")
        kr = _mm(h, w["w_dr"], name="mla_down_rope")
        cq = _rmsnorm_fwd(dq, w["q_norm"], name="mla_q_norm_fwd")
        ckv = _rmsnorm_fwd(dkv, w["kv_norm"], name="mla_kv_norm_fwd")
        qfull = _mm(cq, w["w_uq"], name="mla_up_q")
        kv = _mm(ckv, w["w_ukv"], name="mla_up_kv")
        qf, kf, v = _mla_prep_fwd(qfull, kv, kr, tabs, w["gq"], w["gk"], H)
        o, of, lse = _softmax_fwd(qf, kf, v, None, H, MLA_QK_PAD, 1.0 / math.sqrt(MLA_QK), name="mla_attn_fwd")
        s.update(dq=dq, dkv=dkv, kr=kr, cq=cq, ckv=ckv, qfull=qfull, kv=kv, qf=qf, kf=kf, v=v,
                 o=o, of=of, lse=lse)
    x = _mm(s["o"], w["w_out"], res=x, name="mixer_out")
    s["x_mid"] = x
    deps = hook(x) if hook is not None else ()
    h2 = _rmsnorm_fwd(x, P["mlp_norm"][i:i + 1], name="mlp_norm_fwd")
    z, u = _mm(h2, P["mlp"][i]["w1"], act="relu2", name="mlp_up", deps=deps)
    x = _mm(u, P["mlp"][i]["w2"], res=x, name="mlp_down")
    s.update(h2=h2, z=z, u=u)
    return x, s


def _layer_bwd(i, dx, dxb, s, P, tabs, H, deps=(), hook=None):
    kind, j = i % 3, i // 3
    G = {}
    wm = P["mlp"][i]
    dz = _mm(dxb, wm["w2"], mode="nt", act="drelu2", z=s["z"], out_dtype=BF16, name="mlp_down_bwd",
             deps=deps)
    G["w2"] = _mm(s["u"], dxb, mode="tn", out_dtype=BF16, name="mlp_w2_grad")
    G["w1"] = _mm(s["h2"], dz, mode="tn", out_dtype=BF16, name="mlp_w1_grad")
    dh2 = _mm(dz, wm["w1"], mode="nt", name="mlp_up_bwd")
    dx, dxb, G["mlp_norm"] = _rmsnorm_bwd(s["x_mid"], P["mlp_norm"][i:i + 1], dh2, dx, name="mlp_norm_bwd")
    h = s["h"]
    w = P[("sb", "fox", "mla")[kind]][j]
    do = _mm(dxb, w["w_out"], mode="nt", out_dtype=BF16, name="mixer_out_bwd",
             deps=hook(dxb) if hook is not None else ())
    if kind == 0:
        G["w_out"] = _mm(s["o"], dxb, mode="tn", out_dtype=BF16, name="mixer_out_grad")
        dq, dk, dv = _sb_bwd(s["qkv"], do, s["ctot"], H)
        dproj = jnp.concatenate([dq, dk, dv], axis=1)
        G["w_in"] = _mm(h, dproj, mode="tn", out_dtype=BF16, name="sb_qkv_grad")
        dh = _mm(dproj, w["w_in"], mode="nt", name="sb_qkv_bwd")
    elif kind == 1:
        G["w_out"] = _mm(s["o"], dxb, mode="tn", out_dtype=BF16, name="mixer_out_grad")
        dqn, dkn, dv, dcc, dcr = _softmax_bwd(
            s["qn"], s["kn"], s["vb"], s["cf"], do, s["of"], s["lse"], H, HEAD_DIM,
            1.0 / math.sqrt(HEAD_DIM), name="fox_attn_bwd")
        dq, dk, dvb, G["gq"], G["gk"] = _fox_prep_bwd(s["qkv"], w["gq"], w["gk"], dqn, dkn, dv, H)
        dcf = (dcc[:, :, 0] + dcr[:, 0, :]).T
        dcf = jnp.pad(dcf, ((0, 0), (0, LANE - H)))
        dflog, G["bf"] = _fox_gate_bwd(s["flog"], w["bf"], dcf)
        dproj = jnp.concatenate([dq, dk, dvb], axis=1)
        G["w_qkv"] = _mm(h, dproj, mode="tn", out_dtype=BF16, name="fox_qkv_grad")
        G["w_f"] = _mm(h, dflog, mode="tn", out_dtype=BF16, name="fox_gate_grad")
        dh = _mm(dproj, w["w_qkv"], mode="nt", name="fox_qkv_bwd")
        dh = _mm(dflog, w["w_f"], mode="nt", res=dh, name="fox_gate_bwd_proj")
    else:
        G["w_out"] = _mm(s["o"], dxb, mode="tn", out_dtype=BF16, name="mixer_out_grad")
        dqf, dkf, dv = _softmax_bwd(
            s["qf"], s["kf"], s["v"], None, do, s["of"], s["lse"], H, MLA_QK_PAD,
            1.0 / math.sqrt(MLA_QK), name="mla_attn_bwd")
        dqfull, dkv, dkr, G["gq"], G["gk"] = _mla_prep_bwd(
            s["qfull"], s["kv"], s["kr"], tabs, w["gq"], w["gk"], dqf, dkf, dv, H)
        G["w_uq"] = _mm(s["cq"], dqfull, mode="tn", out_dtype=BF16, name="mla_up_q_grad")
        G["w_ukv"] = _mm(s["ckv"], dkv, mode="tn", out_dtype=BF16, name="mla_up_kv_grad")
        dcq = _mm(dqfull, w["w_uq"], mode="nt", name="mla_up_q_bwd")
        dckv = _mm(dkv, w["w_ukv"], mode="nt", name="mla_up_kv_bwd")
        ddq, G["q_norm"] = _rmsnorm_bwd(s["dq"], w["q_norm"], dcq, want_f32=False, name="mla_q_norm_bwd")
        ddkv, G["kv_norm"] = _rmsnorm_bwd(s["dkv"], w["kv_norm"], dckv, want_f32=False, name="mla_kv_norm_bwd")
        G["w_dq"] = _mm(h, ddq, mode="tn", out_dtype=BF16, name="mla_down_q_grad")
        G["w_dkv"] = _mm(h, ddkv, mode="tn", out_dtype=BF16, name="mla_down_kv_grad")
        G["w_dr"] = _mm(h, dkr, mode="tn", out_dtype=BF16, name="mla_down_rope_grad")
        dh = _mm(ddq, w["w_dq"], mode="nt", name="mla_down_q_bwd")
        dh = _mm(ddkv, w["w_dkv"], mode="nt", res=dh, name="mla_down_kv_bwd")
        dh = _mm(dkr, w["w_dr"], mode="nt", res=dh, name="mla_down_rope_bwd")
    dx, dxb, G["mix_norm"] = _rmsnorm_bwd(s["x_in"], P["mix_norm"][i:i + 1], dh, dx, name="mix_norm_bwd")
    return dx, dxb, G


def _local_step(x, positions, loss_target, P, depth, H):
    tabs = _rope_tables(positions)
    saved = []
    for i in range(depth):
        x, s = _layer_fwd(i, x, P, tabs, H)
        saved.append(s)
    dx, dxb, loss = _loss_head(x, loss_target)
    grads = [None] * depth
    for i in reversed(range(depth)):
        dx, dxb, grads[i] = _layer_bwd(i, dx, dxb, saved[i], P, tabs, H)
    return loss, dx, grads


def _pad_cols(a, n):
    return jnp.pad(a, ((0, 0), (0, n - a.shape[1])))


def _prepare_layer(P, full, i, H):
    W = H * HEAD_DIM
    kind, j = i % 3, i // 3
    P["mlp"][i] = {"w1": full["mlp_w1"][i], "w2": full["mlp_w2"][i]}
    if kind == 0:
        P["sb"][j] = {"w_in": full["sb_w_in"][j], "w_out": full["sb_w_out"][j]}
    elif kind == 1:
        w = full["fox_w_in"][j]
        P["fox"][j] = {
            "w_qkv": w[:, :3 * W], "w_f": _pad_cols(w[:, 3 * W:], LANE),
            "bf": _pad_cols(full["fox_b_f"][j:j + 1], LANE),
            "gq": full["fox_q_gain"][j:j + 1], "gk": full["fox_k_gain"][j:j + 1],
            "w_out": full["fox_w_out"][j]}
    else:
        w = full["mla_w_in"][j]
        q_norm, kv_norm = full["mla_q_norm"][j], full["mla_kv_norm"][j]
        rq, rkv = q_norm.shape[0], kv_norm.shape[0]
        w_uq = full["mla_w_uq"][j].reshape(rq, H, MLA_QK)
        w_uq = jnp.pad(w_uq, ((0, 0), (0, 0), (0, MLA_QK_PAD - MLA_QK))).reshape(rq, H * MLA_QK_PAD)
        P["mla"][j] = {
            "w_dq": w[:, :rq], "w_dkv": w[:, rq:rq + rkv], "w_dr": _pad_cols(w[:, rq + rkv:], LANE),
            "q_norm": q_norm[None], "kv_norm": kv_norm[None],
            "w_uq": w_uq, "w_ukv": full["mla_w_ukv"][j],
            "gq": _pad_cols(full["mla_q_gain"][j:j + 1], MLA_QK_PAD),
            "gk": _pad_cols(full["mla_k_gain"][j:j + 1], MLA_QK_PAD),
            "w_out": full["mla_w_out"][j]}


def _prepare(full, H):
    depth = len(full["mlp_w1"])
    P = {"mix_norm": full["mix_norm"], "mlp_norm": full["mlp_norm"], "mlp": [None] * depth,
         "sb": [None] * len(full["sb_w_in"]), "fox": [None] * len(full["fox_w_in"]),
         "mla": [None] * len(full["mla_w_in"])}
    for i in range(depth):
        _prepare_layer(P, full, i, H)
    return P


def _layer_grads(i, G, H):
    kind = i % 3
    out = {"mix_norm": G["mix_norm"], "mlp_norm": G["mlp_norm"], "mlp_w1": G["w1"], "mlp_w2": G["w2"]}
    if kind == 0:
        out.update(sb_w_in=G["w_in"], sb_w_out=G["w_out"])
    elif kind == 1:
        out.update(fox_w_in=jnp.concatenate([G["w_qkv"], G["w_f"][:, :H]], axis=1), fox_b_f=G["bf"][:, :H],
                   fox_q_gain=G["gq"], fox_k_gain=G["gk"], fox_w_out=G["w_out"])
    else:
        rq = G["w_uq"].shape[0]
        out.update(
            mla_w_in=jnp.concatenate([G["w_dq"], G["w_dkv"], G["w_dr"][:, :MLA_ROPE]], axis=1),
            mla_q_norm=G["q_norm"], mla_kv_norm=G["kv_norm"],
            mla_w_uq=G["w_uq"].reshape(rq, H, MLA_QK_PAD)[:, :, :MLA_QK].reshape(rq, H * MLA_QK),
            mla_w_ukv=G["w_ukv"], mla_q_gain=G["gq"][:, :MLA_QK], mla_k_gain=G["gk"][:, :MLA_QK],
            mla_w_out=G["w_out"])
    return out


def _layer_of(name, j):
    if name.startswith("sb_"):
        return 3 * j
    if name.startswith("fox_"):
        return 3 * j + 1
    if name.startswith("mla_"):
        return 3 * j + 2
    return j


def _unprepare(grads, H):
    out = {}
    for i, G in enumerate(grads):
        for n, g in _layer_grads(i, G, H).items():
            out.setdefault(n, []).append(g)
    return out


class _Place:
    def __init__(self, x, y, c):
        self.x, self.y, self.c = x, y, c
        self.dev = 4 * x + 2 * y + c
        self.chip = 2 * x + y
        self.id = (x, y, c)


def _peers(me, kind):
    if kind == "ici":
        return [_Place(1 - me.x, me.y, me.c), _Place(me.x, 1 - me.y, me.c), _Place(1 - me.x, 1 - me.y, me.c)]
    return [_Place(me.x, me.y, 1 - me.c)]


def _exchange(name, kind, operands, out_shapes, aliases, n_remote, n_local, plan):
    n_in, n_out = len(operands), len(out_shapes)

    def body(*refs):
        in_refs, out_refs = refs[:n_in], refs[n_in:n_in + n_out]
        send_sems, recv_sems, local_sems = refs[n_in + n_out:]
        me = _Place(lax.axis_index("x"), lax.axis_index("y"), lax.axis_index("c"))
        peers = _peers(me, kind)
        remote, local = plan(me, peers, in_refs, out_refs)
        assert len(remote) == n_remote and len(local) == n_local
        sends = []
        for n, (src, dst, k, _) in enumerate(remote):
            cp = pltpu.make_async_remote_copy(
                src_ref=src, dst_ref=dst, send_sem=send_sems.at[n], recv_sem=recv_sems.at[n],
                device_id=peers[k].id, device_id_type=MESH)
            cp.start()
            sends.append(cp)
        own = []
        for n, (src, dst) in enumerate(local):
            cp = pltpu.make_async_copy(src, dst, local_sems.at[n])
            cp.start()
            own.append(cp)
        for n, (src, _, k, landing) in enumerate(remote):
            pltpu.make_async_remote_copy(
                src_ref=src, dst_ref=landing, send_sem=send_sems.at[n], recv_sem=recv_sems.at[n],
                device_id=peers[k].id, device_id_type=MESH).wait_recv()
        for cp in sends:
            cp.wait_send()
        for cp in own:
            cp.wait()

    outs = pl.pallas_call(
        body, name=name, in_specs=[ANY] * n_in, out_specs=tuple([ANY] * n_out),
        out_shape=tuple(out_shapes), input_output_aliases=aliases,
        scratch_shapes=[pltpu.SemaphoreType.DMA((n_remote,)), pltpu.SemaphoreType.DMA((n_remote,)),
                        pltpu.SemaphoreType.DMA((max(n_local, 1),))],
    )(*operands)
    return list(outs)


def _window(ref, kind, d, shard_shape):
    r, c = shard_shape
    if kind == "col":
        return ref.at[:, pl.ds(pl.multiple_of(d * c, c), c)]
    return ref.at[pl.ds(pl.multiple_of(d * r, r), r), :]


def _full_shape(kind, shard_shape):
    r, c = shard_shape
    return (r, N_DEV * c) if kind == "col" else (N_DEV * r, c)


HBM = pl.BlockSpec(memory_space=pltpu.HBM)
SEM = pl.BlockSpec(memory_space=pltpu.SEMAPHORE)
EFFECT = pltpu.SideEffectType.DATAFLOW_SIDE_EFFECTING


class _Flight:
    def __init__(self, name, kind, n_remote, plan, send_sems, recv_sems, bufs, token):
        self.name, self.kind, self.n_remote, self.plan = name, kind, n_remote, plan
        self.send_sems, self.recv_sems, self.bufs, self.token = send_sems, recv_sems, bufs, token


def _start_copies(name, kind, bufs, n_remote, plan, after=None):
    nb = len(bufs)
    n_after = 0 if after is None else 1

    def body(*refs):
        in_refs = refs[:nb]
        send_sems, recv_sems = refs[nb + n_after], refs[nb + n_after + 1]
        token = refs[2 * nb + n_after + 2]
        me = _Place(lax.axis_index("x"), lax.axis_index("y"), lax.axis_index("c"))
        peers = _peers(me, kind)
        for n, (src, dst, k, _) in enumerate(plan(me, peers, in_refs)):
            pltpu.make_async_remote_copy(
                src_ref=src, dst_ref=dst, send_sem=send_sems.at[n], recv_sem=recv_sems.at[n],
                device_id=peers[k].id, device_id_type=MESH).start()
        token[...] = jnp.zeros_like(token)

    outs = pl.pallas_call(
        body, name=name, in_specs=[HBM] * nb + [ANY] * n_after,
        out_specs=(SEM, SEM) + (HBM,) * nb + (pl.BlockSpec(memory_space=pltpu.VMEM),),
        out_shape=(pltpu.SemaphoreType.DMA((n_remote,)), pltpu.SemaphoreType.DMA((n_remote,)))
        + tuple(pltpu.HBM(b.shape, b.dtype) for b in bufs) + (jax.ShapeDtypeStruct((8, LANE), F32),),
        input_output_aliases={a: 2 + a for a in range(nb)},
        compiler_params=pltpu.CompilerParams(has_side_effects=EFFECT),
    )(*[pltpu.with_memory_space_constraint(b, pltpu.HBM) for b in bufs], *([after] if n_after else []))
    return _Flight(name, kind, n_remote, plan, outs[0], outs[1], list(outs[2:2 + nb]), outs[2 + nb])


def _wait_copies(flight, after):
    nb = len(flight.bufs)
    plan, kind = flight.plan, flight.kind

    def body(*refs):
        in_refs = refs[:nb]
        send_sems, recv_sems = refs[nb], refs[nb + 1]
        me = _Place(lax.axis_index("x"), lax.axis_index("y"), lax.axis_index("c"))
        peers = _peers(me, kind)
        for n, (src, _, k, landing) in enumerate(plan(me, peers, in_refs)):
            cp = pltpu.make_async_remote_copy(
                src_ref=src, dst_ref=landing, send_sem=send_sems.at[n], recv_sem=recv_sems.at[n],
                device_id=peers[k].id, device_id_type=MESH)
            cp.wait_send()
            cp.wait_recv()

    outs = pl.pallas_call(
        body, name=flight.name.replace("_start", "_wait"),
        in_specs=[HBM] * nb + [SEM, SEM, ANY], out_specs=(HBM,) * nb,
        out_shape=tuple(pltpu.HBM(b.shape, b.dtype) for b in flight.bufs),
        input_output_aliases={a: a for a in range(nb)},
        compiler_params=pltpu.CompilerParams(has_side_effects=EFFECT),
    )(*flight.bufs, flight.send_sems, flight.recv_sems, after)
    return list(outs)


def _place_own(shards, kinds, tag):
    n = len(shards)

    def body(*refs):
        dev = 4 * lax.axis_index("x") + 2 * lax.axis_index("y") + lax.axis_index("c")
        sem = refs[2 * n]
        cps = [pltpu.make_async_copy(refs[a], _window(refs[n + a], kinds[a], dev, shards[a].shape), sem.at[a])
               for a in range(n)]
        for cp in cps:
            cp.start()
        for cp in cps:
            cp.wait()

    return list(pl.pallas_call(
        body, name="place_own_" + tag, in_specs=[ANY] * n, out_specs=(ANY,) * n,
        out_shape=tuple(jax.ShapeDtypeStruct(_full_shape(k, s.shape), s.dtype) for k, s in zip(kinds, shards)),
        scratch_shapes=[pltpu.SemaphoreType.DMA((n,))],
    )(*shards))


def _gather_ici_start(shards, kinds, tag, after=None):
    n = len(shards)
    shapes = [s.shape for s in shards]

    def plan(me, peers, refs):
        remote = []
        for a in range(n):
            mine = _window(refs[n + a], kinds[a], me.dev, shapes[a])
            for k, p in enumerate(peers):
                remote.append((refs[a], mine, k, _window(refs[n + a], kinds[a], p.dev, shapes[a])))
        return remote

    return _start_copies("gather_ici_start_" + tag, "ici", list(shards) + _place_own(shards, kinds, tag),
                         3 * n, plan, after)


def _gather_d2d_start(fulls, kinds, shapes, tag):
    n = len(fulls)

    def plan(me, peers, refs):
        remote = []
        for a in range(n):
            for ch in range(N_CHIP):
                held = _window(refs[a], kinds[a], 2 * ch + me.c, shapes[a])
                remote.append((held, held, 0, _window(refs[a], kinds[a], 2 * ch + 1 - me.c, shapes[a])))
        return remote

    return _start_copies("gather_d2d_start_" + tag, "d2d", fulls, N_CHIP * n, plan)


def _scatter_d2d_start(grads, kinds, shapes, tag):
    n = len(grads)
    lands = [lax.empty((N_CHIP,) + tuple(s), g.dtype) for g, s in zip(grads, shapes)]

    def plan(me, peers, refs):
        remote = []
        for a in range(n):
            for ch in range(N_CHIP):
                remote.append((_window(refs[a], kinds[a], 2 * ch + 1 - me.c, shapes[a]),
                               refs[n + a].at[ch], 0, refs[n + a].at[ch]))
        return remote

    return _start_copies("scatter_d2d_start_" + tag, "d2d", list(grads) + lands, N_CHIP * n, plan)


def _scatter_ici_start(sums, shapes, tag):
    n = len(sums)
    lands = [lax.empty((N_CHIP - 1,) + tuple(s), BF16) for s in shapes]

    def plan(me, peers, refs):
        remote = []
        for a in range(n):
            for k, p in enumerate(peers):
                remote.append((refs[a].at[p.chip], refs[n + a].at[k], k, refs[n + a].at[k]))
        return remote

    return _start_copies("scatter_ici_start_" + tag, "ici", list(sums) + lands, 3 * n, plan)


def _pair_add(g, r4, kind, shard_shape, core, *, name):
    r, c = shard_shape
    tr = _row_tile(r, c)
    nb = r // tr
    if kind == "all":
        def body_all(core_ref, g_ref, r_ref, o_ref):
            o_ref[...] = g_ref[...] + r_ref[...]

        spec = pl.BlockSpec((tr, c), lambda i, core_ref: (i, 0))
        return pl.pallas_call(
            body_all, name=name,
            grid_spec=pltpu.PrefetchScalarGridSpec(
                num_scalar_prefetch=1, grid=(nb,), in_specs=[spec, spec], out_specs=spec),
            out_shape=jax.ShapeDtypeStruct((r, c), F32),
            compiler_params=_params(("parallel",)),
        )(core, g, r4)

    def body(core_ref, g_ref, r_ref, o_ref):
        o_ref[...] = (g_ref[...].astype(F32) + r_ref[...].astype(F32)).astype(BF16)

    if kind == "col":
        g_spec = pl.BlockSpec((tr, c), lambda ch, i, core_ref: (i, 2 * ch + core_ref[0]))
    else:
        g_spec = pl.BlockSpec((tr, c), lambda ch, i, core_ref: ((2 * ch + core_ref[0]) * nb + i, 0))
    slot = pl.BlockSpec((None, tr, c), lambda ch, i, core_ref: (ch, i, 0))
    return pl.pallas_call(
        body, name=name,
        grid_spec=pltpu.PrefetchScalarGridSpec(
            num_scalar_prefetch=1, grid=(N_CHIP, nb), in_specs=[g_spec, slot], out_specs=slot),
        out_shape=jax.ShapeDtypeStruct((N_CHIP, r, c), BF16),
        compiler_params=_params(("parallel", "parallel")),
    )(core, g, r4)


def _reduce_scatter(grads, kinds, shard_shapes, core, tag):
    n = len(grads)
    land = []
    for g, k, s in zip(grads, kinds, shard_shapes):
        land.append(jax.ShapeDtypeStruct(s if k == "all" else (N_CHIP,) + tuple(s), g.dtype))
    n_rem = sum(1 if k == "all" else N_CHIP for k in kinds)

    def plan_d2d(me, peers, ins, outs):
        remote = []
        for a in range(n):
            if kinds[a] == "all":
                remote.append((ins[a], outs[a], 0, outs[a]))
                continue
            for ch in range(N_CHIP):
                remote.append((_window(ins[a], kinds[a], 2 * ch + 1 - me.c, shard_shapes[a]),
                               outs[a].at[ch], 0, outs[a].at[ch]))
        return remote, []

    got = _exchange("scatter_d2d_" + tag, "d2d", list(grads), land, {}, n_rem, 0, plan_d2d)
    sums = [_pair_add(g, r4, k, s, core, name="pair_add_" + tag)
            for g, r4, k, s in zip(grads, got, kinds, shard_shapes)]
    land2 = [jax.ShapeDtypeStruct((N_CHIP,) + tuple(s), F32 if k == "all" else BF16)
             for k, s in zip(kinds, shard_shapes)]

    def plan_ici(me, peers, ins, outs):
        remote, local = [], []
        for a in range(n):
            for k, p in enumerate(peers):
                src = ins[a] if kinds[a] == "all" else ins[a].at[p.chip]
                remote.append((src, outs[a].at[me.chip], k, outs[a].at[p.chip]))
            src = ins[a] if kinds[a] == "all" else ins[a].at[me.chip]
            local.append((src, outs[a].at[me.chip]))
        return remote, local

    return _exchange("scatter_ici_" + tag, "ici", sums, land2, {}, 3 * n, n, plan_ici)


def _row_tile(r, c, limit=131072):
    best = None
    for t in range(8, r + 1, 8):
        if r % t == 0 and t * c <= limit:
            best = t
    return best if best is not None else r


def _adam_math(w, g, m, v):
    m = ADAM_B1 * m + (1.0 - ADAM_B1) * g
    v = ADAM_B2 * v + (1.0 - ADAM_B2) * (g * g)
    m_hat = m / (1.0 - ADAM_B1 ** ADAM_STEP)
    v_hat = v / (1.0 - ADAM_B2 ** ADAM_STEP)
    delta = -ADAM_LR * (m_hat / (jnp.sqrt(v_hat) + ADAM_EPS) + ADAM_WD * w)
    return delta, m, v


def _adamw(w, m, v, sums, lands, chip, layer, prev, *, name):
    L, r, c = w.shape
    tr = _row_tile(r, c)
    has_prev = prev is not None

    def body(chip_ref, *refs):
        w_ref, m_ref, v_ref, s_ref, l_ref = refs[:5]
        g_ref, d_ref, nm_ref, nv_ref = refs[-4:]
        g = (s_ref[...].astype(F32) + l_ref[0].astype(F32)) + l_ref[1].astype(F32) + l_ref[2].astype(F32)
        delta, nm, nv = _adam_math(w_ref[...], g, m_ref[...], v_ref[...])
        g_ref[...] = g
        d_ref[...] = delta
        nm_ref[...] = nm
        nv_ref[...] = nv

    slab = pl.BlockSpec((None, tr, c), lambda i, chip_ref: (layer, i, 0))
    in_specs = [slab, slab, slab, pl.BlockSpec((None, tr, c), lambda i, chip_ref: (chip_ref[0], i, 0)),
                pl.BlockSpec((N_CHIP - 1, tr, c), lambda i, chip_ref: (0, i, 0))]
    args = [chip, w, m, v, sums, lands]
    aliases = {}
    if has_prev:
        in_specs += [ANY] * 4
        args += list(prev)
        aliases = {6 + n: n for n in range(4)}
    return pl.pallas_call(
        body, name=name,
        grid_spec=pltpu.PrefetchScalarGridSpec(
            num_scalar_prefetch=1, grid=(r // tr,), in_specs=in_specs, out_specs=(slab,) * 4),
        out_shape=tuple(jax.ShapeDtypeStruct((L, r, c), F32) for _ in range(4)),
        input_output_aliases=aliases, compiler_params=_params(("parallel",)),
    )(*args)


def _adamw_small(w, g, m, v):
    def body(w_ref, g_ref, m_ref, v_ref, d_ref, nm_ref, nv_ref):
        d_ref[...], nm_ref[...], nv_ref[...] = _adam_math(w_ref[...], g_ref[...], m_ref[...], v_ref[...])

    return pl.pallas_call(
        body, name="adamw_small", out_shape=tuple(jax.ShapeDtypeStruct(w.shape, F32) for _ in range(3)),
    )(w, g, m, v)


def _sum4(parts):
    def body(p_ref, o_ref):
        o_ref[...] = (p_ref[0] + p_ref[1]) + p_ref[2] + p_ref[3]

    return pl.pallas_call(
        body, name="sum_chips", out_shape=jax.ShapeDtypeStruct(parts.shape[1:], F32))(parts)


_WEIGHTS = ["mix_norm", "mlp_norm", "sb_w_in", "sb_w_out", "fox_w_in", "fox_b_f", "fox_q_gain", "fox_k_gain",
            "fox_w_out", "mla_w_in", "mla_q_norm", "mla_kv_norm", "mla_w_uq", "mla_w_ukv", "mla_q_gain",
            "mla_k_gain", "mla_w_out", "mlp_w1", "mlp_w2"]
_BIG = {"sb_w_in": "col", "sb_w_out": "row", "fox_w_in": "row", "fox_w_out": "row", "mla_w_in": "row",
        "mla_w_uq": "col", "mla_w_ukv": "col", "mla_w_out": "row", "mlp_w1": "col", "mlp_w2": "row"}


def _layer_big(i):
    kind, j = i % 3, i // 3
    mixer = {0: ["sb_w_in", "sb_w_out"], 1: ["fox_w_in", "fox_w_out"],
             2: ["mla_w_in", "mla_w_uq", "mla_w_ukv", "mla_w_out"]}[kind]
    return [(n, j) for n in mixer] + [("mlp_w1", i), ("mlp_w2", i)]


def _stack_to_cols(a):
    r = a.shape[0] // N_DEV
    return a.reshape(N_DEV, r, a.shape[1]).transpose(1, 0, 2).reshape(r, N_DEV * a.shape[1])


def _cols_to_stack(a):
    c = a.shape[1] // N_DEV
    return a.reshape(a.shape[0], N_DEV, c).transpose(1, 0, 2).reshape(N_DEV * a.shape[0], c)


def _pack_rows(rows, width):
    rows = [jnp.pad(r.reshape(-1).astype(F32), (0, width - r.size)) for r in rows]
    pad = (-len(rows)) % 8
    rows += [jnp.zeros((width,), F32)] * pad
    return jnp.stack(rows)


def kernel(x, positions, mix_norm, mlp_norm, sb_w_in, sb_w_out, fox_w_in, fox_b_f, fox_q_gain, fox_k_gain, fox_w_out, mla_w_in, mla_q_norm, mla_kv_norm, mla_w_uq, mla_w_ukv, mla_q_gain, mla_k_gain, mla_w_out, mlp_w1, mlp_w2, loss_target, m_mix_norm, m_mlp_norm, m_sb_w_in, m_sb_w_out, m_fox_w_in, m_fox_b_f, m_fox_q_gain, m_fox_k_gain, m_fox_w_out, m_mla_w_in, m_mla_q_norm, m_mla_kv_norm, m_mla_w_uq, m_mla_w_ukv, m_mla_q_gain, m_mla_k_gain, m_mla_w_out, m_mlp_w1, m_mlp_w2, v_mix_norm, v_mlp_norm, v_sb_w_in, v_sb_w_out, v_fox_w_in, v_fox_b_f, v_fox_q_gain, v_fox_k_gain, v_fox_w_out, v_mla_w_in, v_mla_q_norm, v_mla_kv_norm, v_mla_w_uq, v_mla_w_ukv, v_mla_q_gain, v_mla_k_gain, v_mla_w_out, v_mlp_w1, v_mlp_w2):
    w_in = dict(zip(_WEIGHTS, (mix_norm, mlp_norm, sb_w_in, sb_w_out, fox_w_in, fox_b_f, fox_q_gain, fox_k_gain, fox_w_out, mla_w_in, mla_q_norm, mla_kv_norm, mla_w_uq, mla_w_ukv, mla_q_gain, mla_k_gain, mla_w_out, mlp_w1, mlp_w2)))
    m_in = dict(zip(_WEIGHTS, (m_mix_norm, m_mlp_norm, m_sb_w_in, m_sb_w_out, m_fox_w_in, m_fox_b_f, m_fox_q_gain, m_fox_k_gain, m_fox_w_out, m_mla_w_in, m_mla_q_norm, m_mla_kv_norm, m_mla_w_uq, m_mla_w_ukv, m_mla_q_gain, m_mla_k_gain, m_mla_w_out, m_mlp_w1, m_mlp_w2)))
    v_in = dict(zip(_WEIGHTS, (v_mix_norm, v_mlp_norm, v_sb_w_in, v_sb_w_out, v_fox_w_in, v_fox_b_f, v_fox_q_gain, v_fox_k_gain, v_fox_w_out, v_mla_w_in, v_mla_q_norm, v_mla_kv_norm, v_mla_w_uq, v_mla_w_ukv, v_mla_q_gain, v_mla_k_gain, v_mla_w_out, v_mlp_w1, v_mlp_w2)))
    depth, D = mix_norm.shape
    H = D // HEAD_DIM
    n_mla = mla_w_in.shape[0]
    dev = 4 * lax.axis_index("x") + 2 * lax.axis_index("y") + lax.axis_index("c")
    core = lax.axis_index("c").astype(jnp.int32).reshape(1)

    chip = (2 * lax.axis_index("x") + lax.axis_index("y")).astype(jnp.int32).reshape(1)
    nq, nkv = mla_q_norm.shape[1], mla_kv_norm.shape[1]

    kinds_of, shapes_of, ici = [], [], []
    for i in range(depth):
        names = _layer_big(i)
        shards = [w_in[n][j].astype(BF16) for n, j in names]
        kinds = [_BIG[n] for n, _ in names]
        if i % 3 == 2:
            shards.append(_pack_rows([mla_q_norm[i // 3], mla_kv_norm[i // 3]], LANE))
            kinds.append("row")
        kinds_of.append(kinds)
        shapes_of.append([s.shape for s in shards])
        ici.append(_gather_ici_start(shards, kinds, f"l{i}", ici[-1].token if ici else None))

    full = {n: [None] * w_in[n].shape[0] for n in _WEIGHTS}
    for n in ("mix_norm", "mlp_norm", "fox_b_f", "fox_q_gain", "fox_k_gain", "mla_q_gain", "mla_k_gain"):
        full[n] = w_in[n]
    P = {"mix_norm": mix_norm, "mlp_norm": mlp_norm, "mlp": [None] * depth,
         "sb": [None] * sb_w_in.shape[0], "fox": [None] * fox_w_in.shape[0], "mla": [None] * n_mla}

    def forward_to_sibling(i, after):
        n = len(kinds_of[i])
        arrived = _wait_copies(ici[i], after)[n:]
        return _gather_d2d_start(arrived, kinds_of[i], shapes_of[i], f"l{i}")

    def finish_gather(i, flight, after):
        got = _wait_copies(flight, after)
        for (n, j), a in zip(_layer_big(i), got):
            full[n][j] = _stack_to_cols(a) if n == "fox_w_in" else a
        if i % 3 == 2:
            tiles = got[-1].reshape(N_DEV, 8, LANE)
            full["mla_q_norm"][i // 3] = tiles[:, 0, :nq].reshape(-1)
            full["mla_kv_norm"][i // 3] = tiles[:, 1, :nkv].reshape(-1)
        _prepare_layer(P, full, i, H)

    tabs = _rope_tables(positions[0])
    d2d = forward_to_sibling(0, ici[-1].token)
    finish_gather(0, d2d, d2d.token)
    xs, saved = x[0], []
    for i in range(depth):
        nxt = {}

        def hook(x_mid, i=i, nxt=nxt):
            if i + 1 == depth:
                return ()
            nxt["d2d"] = forward_to_sibling(i + 1, x_mid)
            return (nxt["d2d"].token,)

        xs, s = _layer_fwd(i, xs, P, tabs, H, hook)
        saved.append(s)
        if i + 1 < depth:
            finish_gather(i + 1, nxt["d2d"], xs)
    dx, dxb, loss = _loss_head(xs, loss_target[0])
    loss = lax.psum(loss[0, 0], ("x", "y", "c"))

    def big_grads(i, Gl):
        grads = [(_cols_to_stack(Gl[n]) if n == "fox_w_in" else Gl[n]) for n, _ in _layer_big(i)]
        kinds = [_BIG[n] for n, _ in _layer_big(i)]
        shapes = [w_in[n].shape[1:] for n, _ in _layer_big(i)]
        return grads, kinds, shapes

    def pair_and_send(i, flight, after):
        grads, kinds, shapes = big_grads(i, G[i])
        got = _wait_copies(flight, after)
        sums = [_pair_add(g, r4, k, s, core, name=f"pair_add_l{i}")
                for g, r4, k, s in zip(got[:len(grads)], got[len(grads):], kinds, shapes)]
        return _scatter_ici_start(sums, shapes, f"l{i}")

    G, to_sibling, to_chips, deps = [None] * depth, None, [None] * depth, ()
    for i in reversed(range(depth)):
        def hook(dxb_mid, i=i):
            if i + 1 == depth:
                return ()
            to_chips[i + 1] = pair_and_send(i + 1, to_sibling, dxb_mid)
            return (to_chips[i + 1].token,)

        dx, dxb, Gp = _layer_bwd(i, dx, dxb, saved[i], P, tabs, H, deps, hook)
        G[i] = _layer_grads(i, Gp, H)
        grads, kinds, shapes = big_grads(i, G[i])
        to_sibling = _scatter_d2d_start(grads, kinds, shapes, f"l{i}")
        deps = (to_sibling.token,)
    grad_x = dx
    to_chips[0] = pair_and_send(0, to_sibling, to_sibling.token)

    results = {}
    for i in reversed(range(depth)):
        n_big = len(_layer_big(i))
        got = _wait_copies(to_chips[i], to_chips[0].token)
        for a, (n, j) in enumerate(_layer_big(i)):
            results[n] = _adamw(w_in[n], m_in[n], v_in[n], got[a], got[n_big + a], chip, j, results.get(n),
                                name="adamw_" + n)

    small = [n for n in _WEIGHTS if n not in _BIG]
    rows = []
    for n in small:
        for j in range(w_in[n].shape[0]):
            rows.append(G[_layer_of(n, j)][n])
    sg = _pack_rows(rows, D) + to_chips[0].token[0, 0]
    (parts,) = _reduce_scatter([sg], ["all"], [sg.shape], core, "small")
    sgr = _sum4(parts)
    g_rows, w_rows, m_rows, v_rows, at = [], [], [], [], 0
    for n in small:
        for j in range(w_in[n].shape[0]):
            width = w_in[n].shape[1]
            if n in ("mla_q_norm", "mla_kv_norm"):
                g_rows.append(lax.dynamic_slice(sgr[at], (dev * width,), (width,)))
            else:
                g_rows.append(sgr[at, :width])
            w_rows.append(w_in[n][j])
            m_rows.append(m_in[n][j])
            v_rows.append(v_in[n][j])
            at += 1
    g_pack = _pack_rows(g_rows, D)
    d_pack, nm_pack, nv_pack = _adamw_small(
        _pack_rows(w_rows, D), g_pack, _pack_rows(m_rows, D), _pack_rows(v_rows, D))
    at = 0
    for n in small:
        L, width = w_in[n].shape
        results[n] = tuple(p[at:at + L, :width] for p in (g_pack, d_pack, nm_pack, nv_pack))
        at += L

    out = [loss, grad_x[None]]
    for part in range(4):
        out += [results[n][part] for n in _WEIGHTS]
    return tuple(out)
```

```python
import functools
import math

import jax
import jax.numpy as jnp
from jax import lax
from jax.experimental import pallas as pl
from jax.experimental.pallas import tpu as pltpu

F32 = jnp.float32
BF16 = jnp.bfloat16

HEAD_DIM = 128
MLA_NOPE = 128
MLA_ROPE = 64
MLA_V = 128
MLA_QK = MLA_NOPE + MLA_ROPE
MLA_QK_PAD = 256
LANE = 128
ROPE_THETA = 10000.0
EPS = 1e-6
ADAM_LR = 0.001
ADAM_B1 = 0.9
ADAM_B2 = 0.999
ADAM_EPS = 1e-08
ADAM_WD = 0.01
ADAM_STEP = 10
N_DEV = 8
N_CHIP = 4
NEG = -1e30
VMEM_LIMIT = 56 * 1024 * 1024
ATT_BLOCK = 256
MESH = pl.DeviceIdType.MESH
ANY = pl.BlockSpec(memory_space=pl.ANY)


def _params(sem):
    return pltpu.CompilerParams(dimension_semantics=sem, vmem_limit_bytes=VMEM_LIMIT)


def _tile(dim, pref):
    if dim <= pref:
        return dim
    t = pref
    while dim % t:
        t -= LANE
    return t


def _dot(a, b, dims):
    return lax.dot_general(a, b, (dims, ((), ())), preferred_element_type=F32)


NN = ((1,), (0,))
NT = ((1,), (1,))
TN = ((0,), (0,))


def _mm(a, b, *, mode="nn", out_dtype=F32, res=None, act=None, z=None, name,
        tm=1024, tn=512, tk=2048, deps=()):
    if mode == "nn":
        (M, K), (_, N) = a.shape, b.shape
    elif mode == "nt":
        (M, K), (N, _) = a.shape, b.shape
    else:
        (K, M), (_, N) = a.shape, b.shape
    tm, tn, tk = _tile(M, tm), _tile(N, tn), _tile(K, tk)
    nk = K // tk
    if mode == "tn":
        a_spec = pl.BlockSpec((tk, tm), lambda i, j, k: (k, i))
    else:
        a_spec = pl.BlockSpec((tm, tk), lambda i, j, k: (i, k))
    if mode == "nt":
        b_spec = pl.BlockSpec((tn, tk), lambda i, j, k: (j, k))
    else:
        b_spec = pl.BlockSpec((tk, tn), lambda i, j, k: (k, j))
    dims = {"nn": NN, "nt": NT, "tn": TN}[mode]
    o_spec = pl.BlockSpec((tm, tn), lambda i, j, k: (i, j))
    in_specs, args = [a_spec, b_spec], [a, b]
    if res is not None:
        in_specs.append(o_spec)
        args.append(res)
    if act == "drelu2":
        in_specs.append(o_spec)
        args.append(z)
    if act == "relu2":
        out_shape = (jax.ShapeDtypeStruct((M, N), F32), jax.ShapeDtypeStruct((M, N), BF16))
        out_specs = (o_spec, o_spec)
    else:
        out_shape = jax.ShapeDtypeStruct((M, N), out_dtype)
        out_specs = o_spec
    has_res, has_z = res is not None, act == "drelu2"
    n_out = 2 if act == "relu2" else 1
    in_specs += [ANY] * len(deps)
    args += list(deps)

    def body(*refs):
        a_ref, b_ref = refs[0], refs[1]
        idx = 2
        res_ref = z_ref = None
        if has_res:
            res_ref = refs[idx]
            idx += 1
        if has_z:
            z_ref = refs[idx]
            idx += 1
        idx += len(deps)
        outs = refs[idx:idx + n_out]

        def finish(r):
            if has_res:
                r = r + res_ref[...]
            if act == "relu2":
                outs[0][...] = r
                rr = jnp.maximum(r, 0.0)
                outs[1][...] = (rr * rr).astype(BF16)
            elif act == "drelu2":
                outs[0][...] = (r * (2.0 * jnp.maximum(z_ref[...], 0.0))).astype(out_dtype)
            else:
                outs[0][...] = r.astype(out_dtype)

        if nk == 1:
            finish(_dot(a_ref[...], b_ref[...], dims))
            return
        acc = refs[-1]
        k = pl.program_id(2)

        @pl.when(k == 0)
        def _():
            acc[...] = _dot(a_ref[...], b_ref[...], dims)

        @pl.when((k > 0) & (k < nk - 1))
        def _():
            acc[...] += _dot(a_ref[...], b_ref[...], dims)

        @pl.when(k == nk - 1)
        def _():
            finish(acc[...] + _dot(a_ref[...], b_ref[...], dims))

    return pl.pallas_call(
        body, name=name, grid=(M // tm, N // tn, nk), in_specs=in_specs, out_specs=out_specs,
        out_shape=out_shape, scratch_shapes=[pltpu.VMEM((tm, tn), F32)] if nk > 1 else [],
        compiler_params=_params(("parallel", "parallel", "arbitrary")),
    )(*args)


def _rmsnorm_fwd(x, g, *, name, tm=256):
    T, n = x.shape
    tm = _tile(T, tm)

    def body(x_ref, g_ref, o_ref):
        xf = x_ref[...]
        r = lax.rsqrt(jnp.mean(xf * xf, axis=-1, keepdims=True) + EPS)
        o_ref[...] = (xf * r * g_ref[...]).astype(BF16)

    return pl.pallas_call(
        body, name=name, grid=(T // tm,),
        in_specs=[pl.BlockSpec((tm, n), lambda i: (i, 0)), pl.BlockSpec((1, n), lambda i: (0, 0))],
        out_specs=pl.BlockSpec((tm, n), lambda i: (i, 0)),
        out_shape=jax.ShapeDtypeStruct((T, n), BF16),
        compiler_params=_params(("parallel",)),
    )(x, g)


def _rmsnorm_bwd(x, g, dy, dx_in=None, *, name, want_f32=True, tm=256):
    T, n = x.shape
    tm = _tile(T, tm)
    has_in = dx_in is not None
    row = pl.BlockSpec((tm, n), lambda i: (i, 0))
    vec = pl.BlockSpec((1, n), lambda i: (0, 0))

    def body(*refs):
        x_ref, g_ref, dy_ref = refs[:3]
        idx = 3
        in_ref = None
        if has_in:
            in_ref = refs[idx]
            idx += 1
        outs = refs[idx:]
        xf = x_ref[...]
        r = lax.rsqrt(jnp.mean(xf * xf, axis=-1, keepdims=True) + EPS)
        dyf = dy_ref[...]
        t = dyf * g_ref[...]
        dx = r * t - xf * (r * r * r * jnp.mean(t * xf, axis=-1, keepdims=True))
        if has_in:
            dx = dx + in_ref[...]
        o = 0
        if want_f32:
            outs[0][...] = dx
            o = 1
        outs[o][...] = dx.astype(BF16)
        dg_ref = outs[o + 1]

        @pl.when(pl.program_id(0) == 0)
        def _():
            dg_ref[...] = jnp.zeros_like(dg_ref)

        dg_ref[...] += jnp.sum(dyf * xf * r, axis=0, keepdims=True)

    in_specs = [row, vec, row] + ([row] if has_in else [])
    out_specs, out_shape = [], []
    if want_f32:
        out_specs.append(row)
        out_shape.append(jax.ShapeDtypeStruct((T, n), F32))
    out_specs += [row, vec]
    out_shape += [jax.ShapeDtypeStruct((T, n), BF16), jax.ShapeDtypeStruct((1, n), F32)]
    args = [x, g, dy] + ([dx_in] if has_in else [])
    return pl.pallas_call(
        body, name=name, grid=(T // tm,), in_specs=in_specs, out_specs=tuple(out_specs),
        out_shape=tuple(out_shape), compiler_params=_params(("arbitrary",)),
    )(*args)


def _loss_head(y, target, *, tm=256):
    T, n = y.shape
    tm = _tile(T, tm)
    row = pl.BlockSpec((tm, n), lambda i: (i, 0))

    def body(y_ref, t_ref, dy_ref, dyb_ref, loss_ref):
        err = y_ref[...] - t_ref[...]
        dy = err * (1.0 / n)
        dy_ref[...] = dy
        dyb_ref[...] = dy.astype(BF16)

        @pl.when(pl.program_id(0) == 0)
        def _():
            loss_ref[...] = jnp.zeros_like(loss_ref)

        part = 0.5 * jnp.sum(jnp.mean(err * err, axis=-1, keepdims=True), axis=0, keepdims=True)
        loss_ref[...] += part

    return pl.pallas_call(
        body, name="loss_head", grid=(T // tm,), in_specs=[row, row],
        out_specs=(row, row, pl.BlockSpec((1, 1), lambda i: (0, 0))),
        out_shape=(jax.ShapeDtypeStruct((T, n), F32), jax.ShapeDtypeStruct((T, n), BF16),
                   jax.ShapeDtypeStruct((1, 1), F32)),
        compiler_params=_params(("arbitrary",)),
    )(y, target)


def _split3(v):
    hi = v.astype(BF16)
    r1 = v - hi.astype(F32)
    mid = r1.astype(BF16)
    lo = (r1 - mid.astype(F32)).astype(BF16)
    return hi, mid, lo


def _tri_right(v, tri):
    hi, mid, lo = _split3(v)
    return _dot(hi, tri, NN) + _dot(mid, tri, NN) + _dot(lo, tri, NN)


def _tri_left(tri, v):
    hi, mid, lo = _split3(v)
    return _dot(tri, hi, NN) + _dot(tri, mid, NN) + _dot(tri, lo, NN)


def _iota2(shape, dim):
    return lax.broadcasted_iota(jnp.int32, shape, dim)


def _log_sigmoid(z):
    return jnp.minimum(z, 0.0) - jnp.log1p(jnp.exp(-jnp.abs(z)))


def _head_spec(rows, width, off):
    return pl.BlockSpec((rows, width), lambda h, i: (i, off + h))


def _full_head_spec(T, width, off):
    return pl.BlockSpec((T, width), lambda h, i: (0, off + h))


def _sb_fwd(qkv, H):
    T = qkv.shape[0]
    B = _tile(T, ATT_BLOCK)
    nq = T // B
    scale = 1.0 / math.sqrt(HEAD_DIM)

    def body(q_ref, k_ref, v_ref, o_ref, c_ref):
        i = pl.program_id(1)
        q = q_ref[...]
        rows = _iota2((B, B), 0)
        cols = _iota2((B, B), 1)
        tri = (rows > cols).astype(BF16)

        def step(n, carry):
            acc, run = carry
            j = i - n
            ks = pl.multiple_of(j * B, B)
            kb = k_ref[pl.ds(ks, B), :]
            vb = v_ref[pl.ds(ks, B), :]
            z = _dot(q, kb, NT) * scale
            mask = (ks + cols) < (i * B + rows)
            ls = _log_sigmoid(z)
            lk = jnp.where(mask, ls - z, 0.0)
            later = _tri_right(lk, tri)
            a = jnp.where(mask, jnp.exp(ls + later), 0.0)
            acc = acc + jnp.exp(run) * _dot(a.astype(BF16), vb, NN)
            run = run + jnp.sum(lk, axis=1, keepdims=True)
            return acc, run

        acc, run = lax.fori_loop(
            0, i + 1, step, (jnp.zeros((B, HEAD_DIM), F32), jnp.zeros((B, 1), F32)))
        o_ref[...] = acc.astype(BF16)
        c_ref[...] = run

    return pl.pallas_call(
        body, name="sb_attn_fwd", grid=(H, nq),
        in_specs=[_head_spec(B, HEAD_DIM, 0), _full_head_spec(T, HEAD_DIM, H),
                  _full_head_spec(T, HEAD_DIM, 2 * H)],
        out_specs=(_head_spec(B, HEAD_DIM, 0), pl.BlockSpec((None, B, 1), lambda h, i: (h, i, 0))),
        out_shape=(jax.ShapeDtypeStruct((T, H * HEAD_DIM), BF16), jax.ShapeDtypeStruct((H, T, 1), F32)),
        compiler_params=_params(("parallel", "arbitrary")),
    )(qkv, qkv, qkv)


def _sb_bwd(qkv, do, ctot, H):
    T = qkv.shape[0]
    B = _tile(T, ATT_BLOCK)
    nq = T // B
    scale = 1.0 / math.sqrt(HEAD_DIM)

    def body(q_ref, k_ref, v_ref, do_ref, c_ref, dq_ref, dk_ref, dv_ref, dk_acc, dv_acc):
        i = pl.program_id(1)

        @pl.when(i == 0)
        def _():
            dk_acc[...] = jnp.zeros_like(dk_acc)
            dv_acc[...] = jnp.zeros_like(dv_acc)

        q = q_ref[...]
        do_b = do_ref[...]
        ctot_b = c_ref[...]
        rows = _iota2((B, B), 0)
        cols = _iota2((B, B), 1)
        tri_incl = (rows <= cols).astype(BF16)
        tri_strict = (rows < cols).astype(BF16)

        def step(j, carry):
            dq, lpre, gpre = carry
            ks = pl.multiple_of(j * B, B)
            kb = k_ref[pl.ds(ks, B), :]
            vb = v_ref[pl.ds(ks, B), :]
            z = _dot(q, kb, NT) * scale
            mask = (ks + cols) < (i * B + rows)
            ls = _log_sigmoid(z)
            lk_all = ls - z
            lk = jnp.where(mask, lk_all, 0.0)
            later = ctot_b - lpre - _tri_right(lk, tri_incl)
            a = jnp.where(mask, jnp.exp(ls + later), 0.0)
            da = _dot(do_b, vb, NT)
            g = a * da
            gex = gpre + _tri_right(g, tri_strict)
            dz = jnp.where(mask, g * jnp.exp(lk_all) - jnp.exp(ls) * gex, 0.0) * scale
            dzb = dz.astype(BF16)
            dq = dq + _dot(dzb, kb, NN)
            dk_acc[pl.ds(ks, B), :] += _dot(dzb, q, TN)
            dv_acc[pl.ds(ks, B), :] += _dot(a.astype(BF16), do_b, TN)
            lpre = lpre + jnp.sum(lk, axis=1, keepdims=True)
            gpre = gpre + jnp.sum(g, axis=1, keepdims=True)
            return dq, lpre, gpre

        dq, _, _ = lax.fori_loop(
            0, i + 1, step,
            (jnp.zeros((B, HEAD_DIM), F32), jnp.zeros((B, 1), F32), jnp.zeros((B, 1), F32)))
        dq_ref[...] = dq.astype(BF16)

        @pl.when(i == nq - 1)
        def _():
            dk_ref[...] = dk_acc[...].astype(BF16)
            dv_ref[...] = dv_acc[...].astype(BF16)

    W = H * HEAD_DIM
    return pl.pallas_call(
        body, name="sb_attn_bwd", grid=(H, nq),
        in_specs=[_head_spec(B, HEAD_DIM, 0), _full_head_spec(T, HEAD_DIM, H),
                  _full_head_spec(T, HEAD_DIM, 2 * H), _head_spec(B, HEAD_DIM, 0),
                  pl.BlockSpec((None, B, 1), lambda h, i: (h, i, 0))],
        out_specs=(_head_spec(B, HEAD_DIM, 0), _full_head_spec(T, HEAD_DIM, 0),
                   _full_head_spec(T, HEAD_DIM, 0)),
        out_shape=tuple(jax.ShapeDtypeStruct((T, W), BF16) for _ in range(3)),
        scratch_shapes=[pltpu.VMEM((T, HEAD_DIM), F32), pltpu.VMEM((T, HEAD_DIM), F32)],
        compiler_params=_params(("parallel", "arbitrary")),
    )(qkv, qkv, qkv, do, ctot)


def _softmax_fwd(q, k, v, cf, H, dqk, scale, *, name):
    T = q.shape[0]
    B = _tile(T, ATT_BLOCK)
    nq = T // B
    has_cf = cf is not None

    def body(*refs):
        q_ref, k_ref, v_ref = refs[:3]
        idx = 3
        if has_cf:
            cfc_ref, cfr_ref = refs[3], refs[4]
            idx = 5
        o_ref, of_ref, lse_ref = refs[idx:idx + 3]
        i = pl.program_id(1)
        qb = q_ref[...]
        rows = _iota2((B, B), 0)
        cols = _iota2((B, B), 1)

        def step(j, carry):
            m, l, acc = carry
            ks = pl.multiple_of(j * B, B)
            kb = k_ref[pl.ds(ks, B), :]
            vb = v_ref[pl.ds(ks, B), :]
            s = _dot(qb, kb, NT) * scale
            if has_cf:
                s = s + (cfc_ref[...] - cfr_ref[:, pl.ds(ks, B)])
            s = jnp.where((ks + cols) <= (i * B + rows), s, NEG)
            m_new = jnp.maximum(m, jnp.max(s, axis=1, keepdims=True))
            alpha = jnp.exp(m - m_new)
            p = jnp.exp(s - m_new)
            l = alpha * l + jnp.sum(p, axis=1, keepdims=True)
            acc = alpha * acc + _dot(p.astype(BF16), vb, NN)
            return m_new, l, acc

        m, l, acc = lax.fori_loop(
            0, i + 1, step,
            (jnp.full((B, 1), NEG, F32), jnp.zeros((B, 1), F32), jnp.zeros((B, HEAD_DIM), F32)))
        o = acc / l
        o_ref[...] = o.astype(BF16)
        of_ref[...] = o
        lse_ref[...] = m + jnp.log(l)

    stat = pl.BlockSpec((None, B, 1), lambda h, i: (h, i, 0))
    in_specs = [_head_spec(B, dqk, 0), _full_head_spec(T, dqk, 0), _full_head_spec(T, HEAD_DIM, 0)]
    args = [q, k, v]
    if has_cf:
        in_specs += [stat, pl.BlockSpec((None, 1, T), lambda h, i: (h, 0, 0))]
        args += list(cf)
    W = H * HEAD_DIM
    return pl.pallas_call(
        body, name=name, grid=(H, nq), in_specs=in_specs,
        out_specs=(_head_spec(B, HEAD_DIM, 0), _head_spec(B, HEAD_DIM, 0), stat),
        out_shape=(jax.ShapeDtypeStruct((T, W), BF16), jax.ShapeDtypeStruct((T, W), F32),
                   jax.ShapeDtypeStruct((H, T, 1), F32)),
        compiler_params=_params(("parallel", "arbitrary")),
    )(*args)


def _softmax_bwd(q, k, v, cf, do, o, lse, H, dqk, scale, *, name):
    T = q.shape[0]
    B = _tile(T, ATT_BLOCK)
    nq = T // B
    has_cf = cf is not None

    def body(*refs):
        q_ref, k_ref, v_ref, do_ref, o_ref, lse_ref = refs[:6]
        idx = 6
        if has_cf:
            cfc_ref, cfr_ref = refs[6], refs[7]
            idx = 8
        dq_ref, dk_ref, dv_ref = refs[idx:idx + 3]
        idx += 3
        if has_cf:
            dcc_ref, dcr_ref = refs[idx], refs[idx + 1]
        i = pl.program_id(1)

        @pl.when(i == 0)
        def _():
            dk_ref[...] = jnp.zeros_like(dk_ref)
            dv_ref[...] = jnp.zeros_like(dv_ref)
            if has_cf:
                dcr_ref[...] = jnp.zeros_like(dcr_ref)

        qb = q_ref[...]
        do_b = do_ref[...]
        lse_b = lse_ref[...]
        delta = jnp.sum(do_b.astype(F32) * o_ref[...], axis=1, keepdims=True)
        rows = _iota2((B, B), 0)
        cols = _iota2((B, B), 1)

        def step(j, carry):
            dq, rs = carry
            ks = pl.multiple_of(j * B, B)
            kb = k_ref[pl.ds(ks, B), :]
            vb = v_ref[pl.ds(ks, B), :]
            s = _dot(qb, kb, NT) * scale
            if has_cf:
                s = s + (cfc_ref[...] - cfr_ref[:, pl.ds(ks, B)])
            p = jnp.where((ks + cols) <= (i * B + rows), jnp.exp(s - lse_b), 0.0)
            dp = _dot(do_b, vb, NT)
            ds = p * (dp - delta)
            dsb = (ds * scale).astype(BF16)
            dq = dq + _dot(dsb, kb, NN)
            dk_ref[pl.ds(ks, B), :] += _dot(dsb, qb, TN)
            dv_ref[pl.ds(ks, B), :] += _dot(p.astype(BF16), do_b, TN)
            if has_cf:
                rs = rs + jnp.sum(ds, axis=1, keepdims=True)
                dcr_ref[:, pl.ds(ks, B)] -= jnp.sum(ds, axis=0, keepdims=True)
            return dq, rs

        dq, rs = lax.fori_loop(0, i + 1, step, (jnp.zeros((B, dqk), F32), jnp.zeros((B, 1), F32)))
        dq_ref[...] = dq
        if has_cf:
            dcc_ref[...] = rs

    stat = pl.BlockSpec((None, B, 1), lambda h, i: (h, i, 0))
    rowstat = pl.BlockSpec((None, 1, T), lambda h, i: (h, 0, 0))
    in_specs = [_head_spec(B, dqk, 0), _full_head_spec(T, dqk, 0), _full_head_spec(T, HEAD_DIM, 0),
                _head_spec(B, HEAD_DIM, 0), _head_spec(B, HEAD_DIM, 0), stat]
    args = [q, k, v, do, o, lse]
    out_specs = [_head_spec(B, dqk, 0), _full_head_spec(T, dqk, 0), _full_head_spec(T, HEAD_DIM, 0)]
    out_shape = [jax.ShapeDtypeStruct((T, H * dqk), F32), jax.ShapeDtypeStruct((T, H * dqk), F32),
                 jax.ShapeDtypeStruct((T, H * HEAD_DIM), F32)]
    if has_cf:
        in_specs += [stat, rowstat]
        args += list(cf)
        out_specs += [stat, rowstat]
        out_shape += [jax.ShapeDtypeStruct((H, T, 1), F32), jax.ShapeDtypeStruct((H, 1, T), F32)]
    return pl.pallas_call(
        body, name=name, grid=(H, nq), in_specs=in_specs, out_specs=tuple(out_specs),
        out_shape=tuple(out_shape), compiler_params=_params(("parallel", "arbitrary")),
    )(*args)


def _headnorm(x, g):
    r = lax.rsqrt(jnp.mean(x * x, axis=-1, keepdims=True) + EPS)
    return x * r * g, r


def _headnorm_bwd(x, g, dy, n):
    r = lax.rsqrt(jnp.sum(x * x, axis=-1, keepdims=True) * (1.0 / n) + EPS)
    t = dy * g
    dx = r * t - x * (r * r * r * (jnp.sum(t * x, axis=-1, keepdims=True) * (1.0 / n)))
    dg = jnp.sum(dy * x * r, axis=0, keepdims=True)
    return dx, dg


def _fox_prep_fwd(qkv, gq, gk, H, *, tm=512):
    T = qkv.shape[0]
    tm = _tile(T, tm)

    def body(q_ref, k_ref, v_ref, gq_ref, gk_ref, qn_ref, kn_ref, vb_ref):
        qn_ref[...] = _headnorm(q_ref[...], gq_ref[...])[0].astype(BF16)
        kn_ref[...] = _headnorm(k_ref[...], gk_ref[...])[0].astype(BF16)
        vb_ref[...] = v_ref[...].astype(BF16)

    def blk(off):
        return pl.BlockSpec((tm, HEAD_DIM), lambda i, h: (i, off + h))

    vec = pl.BlockSpec((1, HEAD_DIM), lambda i, h: (0, 0))
    W = H * HEAD_DIM
    return pl.pallas_call(
        body, name="fox_prep_fwd", grid=(T // tm, H),
        in_specs=[blk(0), blk(H), blk(2 * H), vec, vec], out_specs=(blk(0), blk(0), blk(0)),
        out_shape=tuple(jax.ShapeDtypeStruct((T, W), BF16) for _ in range(3)),
        compiler_params=_params(("parallel", "parallel")),
    )(qkv, qkv, qkv, gq, gk)


def _fox_prep_bwd(qkv, gq, gk, dqn, dkn, dv, H, *, tm=512):
    T = qkv.shape[0]
    tm = _tile(T, tm)

    def body(q_ref, k_ref, gq_ref, gk_ref, dqn_ref, dkn_ref, dv_ref,
             dq_ref, dk_ref, dvb_ref, dgq_ref, dgk_ref):
        @pl.when((pl.program_id(0) == 0) & (pl.program_id(1) == 0))
        def _():
            dgq_ref[...] = jnp.zeros_like(dgq_ref)
            dgk_ref[...] = jnp.zeros_like(dgk_ref)

        dq, dgq = _headnorm_bwd(q_ref[...], gq_ref[...], dqn_ref[...], HEAD_DIM)
        dk, dgk = _headnorm_bwd(k_ref[...], gk_ref[...], dkn_ref[...], HEAD_DIM)
        dq_ref[...] = dq.astype(BF16)
        dk_ref[...] = dk.astype(BF16)
        dvb_ref[...] = dv_ref[...].astype(BF16)
        dgq_ref[...] += dgq
        dgk_ref[...] += dgk

    def blk(off):
        return pl.BlockSpec((tm, HEAD_DIM), lambda i, h: (i, off + h))

    vec = pl.BlockSpec((1, HEAD_DIM), lambda i, h: (0, 0))
    W = H * HEAD_DIM
    return pl.pallas_call(
        body, name="fox_prep_bwd", grid=(T // tm, H),
        in_specs=[blk(0), blk(H), vec, vec, blk(0), blk(0), blk(0)],
        out_specs=(blk(0), blk(0), blk(0), vec, vec),
        out_shape=tuple(jax.ShapeDtypeStruct((T, W), BF16) for _ in range(3))
        + (jax.ShapeDtypeStruct((1, HEAD_DIM), F32), jax.ShapeDtypeStruct((1, HEAD_DIM), F32)),
        compiler_params=_params(("arbitrary", "arbitrary")),
    )(qkv, qkv, gq, gk, dqn, dkn, dv)


def _fox_gate_fwd(flog, bf, *, tm=256):
    T = flog.shape[0]
    tm = _tile(T, tm)

    def body(f_ref, b_ref, cf_ref, carry):
        @pl.when(pl.program_id(0) == 0)
        def _():
            carry[...] = jnp.zeros_like(carry)

        lf = _log_sigmoid(f_ref[...] + b_ref[...])
        tri = (_iota2((tm, tm), 1) <= _iota2((tm, tm), 0)).astype(BF16)
        cf_ref[...] = carry[...] + _tri_left(tri, lf)
        carry[...] += jnp.sum(lf, axis=0, keepdims=True)

    return pl.pallas_call(
        body, name="fox_gate_fwd", grid=(T // tm,),
        in_specs=[pl.BlockSpec((tm, LANE), lambda i: (i, 0)), pl.BlockSpec((1, LANE), lambda i: (0, 0))],
        out_specs=pl.BlockSpec((tm, LANE), lambda i: (i, 0)),
        out_shape=jax.ShapeDtypeStruct((T, LANE), F32),
        scratch_shapes=[pltpu.VMEM((1, LANE), F32)],
        compiler_params=_params(("arbitrary",)),
    )(flog, bf)


def _fox_gate_bwd(flog, bf, dcf, *, tm=256):
    T = flog.shape[0]
    tm = _tile(T, tm)
    nt = T // tm

    def body(f_ref, b_ref, dcf_ref, df_ref, db_ref, carry):
        @pl.when(pl.program_id(0) == 0)
        def _():
            carry[...] = jnp.zeros_like(carry)
            db_ref[...] = jnp.zeros_like(db_ref)

        d = dcf_ref[...]
        tri = (_iota2((tm, tm), 1) >= _iota2((tm, tm), 0)).astype(BF16)
        dlf = carry[...] + _tri_left(tri, d)
        carry[...] += jnp.sum(d, axis=0, keepdims=True)
        xg = f_ref[...] + b_ref[...]
        e = jnp.exp(-jnp.abs(xg))
        sig_neg = jnp.where(xg >= 0.0, e, 1.0) / (1.0 + e)
        df = dlf * sig_neg
        df_ref[...] = df.astype(BF16)
        db_ref[...] += jnp.sum(df, axis=0, keepdims=True)

    rev = pl.BlockSpec((tm, LANE), lambda i: (nt - 1 - i, 0))
    vec = pl.BlockSpec((1, LANE), lambda i: (0, 0))
    return pl.pallas_call(
        body, name="fox_gate_bwd", grid=(nt,), in_specs=[rev, vec, rev], out_specs=(rev, vec),
        out_shape=(jax.ShapeDtypeStruct((T, LANE), BF16), jax.ShapeDtypeStruct((1, LANE), F32)),
        scratch_shapes=[pltpu.VMEM((1, LANE), F32)],
        compiler_params=_params(("arbitrary",)),
    )(flog, bf, dcf)


def _rope_tables(positions):
    half = MLA_ROPE // 2
    inv_freq = ROPE_THETA ** (-jnp.arange(0, half, dtype=F32) * 2.0 / MLA_ROPE)
    ang = positions.astype(F32)[:, None] * inv_freq
    cos, sin = jnp.cos(ang), jnp.sin(ang)
    zero = jnp.zeros_like(cos)
    pad = jnp.zeros((positions.shape[0], LANE - MLA_ROPE), F32)
    cos_t = jnp.concatenate([cos, cos, pad], axis=1)
    sin_up = jnp.concatenate([zero, sin, pad], axis=1)
    sin_dn = jnp.concatenate([-sin, zero, pad], axis=1)
    return cos_t, sin_up, sin_dn


def _rope(x, cos_t, sin_up, sin_dn):
    half = MLA_ROPE // 2
    return x * cos_t + pltpu.roll(x, half, 1) * sin_up + pltpu.roll(x, LANE - half, 1) * sin_dn


def _rope_t(d, cos_t, sin_up, sin_dn):
    half = MLA_ROPE // 2
    return d * cos_t + pltpu.roll(d * sin_up, LANE - half, 1) + pltpu.roll(d * sin_dn, half, 1)


def _norm192(xcat, g):
    r = lax.rsqrt(jnp.sum(xcat * xcat, axis=-1, keepdims=True) * (1.0 / MLA_QK) + EPS)
    return xcat * r * g


def _mla_prep_fwd(qfull, kv, kr, tabs, gq, gk, H, *, tm=512):
    T = qfull.shape[0]
    tm = _tile(T, tm)

    def body(q_ref, kv_ref, kr_ref, c_ref, su_ref, sd_ref, gq_ref, gk_ref, qf_ref, kf_ref, v_ref):
        tabs_b = (c_ref[...], su_ref[...], sd_ref[...])
        qb = q_ref[...]
        qcat = jnp.concatenate([qb[:, :MLA_NOPE], _rope(qb[:, MLA_NOPE:], *tabs_b)], axis=1)
        qf_ref[...] = _norm192(qcat, gq_ref[...]).astype(BF16)
        kvb = kv_ref[...]
        kcat = jnp.concatenate([kvb[:, :MLA_NOPE], _rope(kr_ref[...], *tabs_b)], axis=1)
        kf_ref[...] = _norm192(kcat, gk_ref[...]).astype(BF16)
        v_ref[...] = kvb[:, MLA_NOPE:].astype(BF16)

    head = pl.BlockSpec((tm, MLA_QK_PAD), lambda i, h: (i, h))
    tok = pl.BlockSpec((tm, LANE), lambda i, h: (i, 0))
    vec = pl.BlockSpec((1, MLA_QK_PAD), lambda i, h: (0, 0))
    return pl.pallas_call(
        body, name="mla_prep_fwd", grid=(T // tm, H),
        in_specs=[head, head, tok, tok, tok, tok, vec, vec],
        out_specs=(head, head, pl.BlockSpec((tm, MLA_V), lambda i, h: (i, h))),
        out_shape=(jax.ShapeDtypeStruct((T, H * MLA_QK_PAD), BF16),
                   jax.ShapeDtypeStruct((T, H * MLA_QK_PAD), BF16),
                   jax.ShapeDtypeStruct((T, H * MLA_V), BF16)),
        compiler_params=_params(("parallel", "parallel")),
    )(qfull, kv, kr, *tabs, gq, gk)


def _mla_prep_bwd(qfull, kv, kr, tabs, gq, gk, dqf, dkf, dv, H, *, tm=512):
    T = qfull.shape[0]
    tm = _tile(T, tm)

    def body(q_ref, kv_ref, kr_ref, c_ref, su_ref, sd_ref, gq_ref, gk_ref, dqf_ref, dkf_ref, dv_ref,
             dq_ref, dkv_ref, dkr_ref, dgq_ref, dgk_ref, kr_acc):
        h = pl.program_id(1)

        @pl.when((pl.program_id(0) == 0) & (h == 0))
        def _():
            dgq_ref[...] = jnp.zeros_like(dgq_ref)
            dgk_ref[...] = jnp.zeros_like(dgk_ref)

        @pl.when(h == 0)
        def _():
            kr_acc[...] = jnp.zeros_like(kr_acc)

        tabs_b = (c_ref[...], su_ref[...], sd_ref[...])
        qb = q_ref[...]
        qcat = jnp.concatenate([qb[:, :MLA_NOPE], _rope(qb[:, MLA_NOPE:], *tabs_b)], axis=1)
        dqcat, dgq = _headnorm_bwd(qcat, gq_ref[...], dqf_ref[...], MLA_QK)
        dq_ref[...] = jnp.concatenate(
            [dqcat[:, :MLA_NOPE], _rope_t(dqcat[:, MLA_NOPE:], *tabs_b)], axis=1).astype(BF16)
        dgq_ref[...] += dgq
        kvb = kv_ref[...]
        kcat = jnp.concatenate([kvb[:, :MLA_NOPE], _rope(kr_ref[...], *tabs_b)], axis=1)
        dkcat, dgk = _headnorm_bwd(kcat, gk_ref[...], dkf_ref[...], MLA_QK)
        dkv_ref[...] = jnp.concatenate([dkcat[:, :MLA_NOPE], dv_ref[...]], axis=1).astype(BF16)
        dgk_ref[...] += dgk
        kr_acc[...] += dkcat[:, MLA_NOPE:]

        @pl.when(h == H - 1)
        def _():
            dkr_ref[...] = _rope_t(kr_acc[...], *tabs_b).astype(BF16)

    head = pl.BlockSpec((tm, MLA_QK_PAD), lambda i, h: (i, h))
    tok = pl.BlockSpec((tm, LANE), lambda i, h: (i, 0))
    vec = pl.BlockSpec((1, MLA_QK_PAD), lambda i, h: (0, 0))
    return pl.pallas_call(
        body, name="mla_prep_bwd", grid=(T // tm, H),
        in_specs=[head, head, tok, tok, tok, tok, vec, vec, head, head,
                  pl.BlockSpec((tm, MLA_V), lambda i, h: (i, h))],
        out_specs=(head, head, tok, vec, vec),
        out_shape=(jax.ShapeDtypeStruct((T, H * MLA_QK_PAD), BF16),
                   jax.ShapeDtypeStruct((T, H * MLA_QK_PAD), BF16),
                   jax.ShapeDtypeStruct((T, LANE), BF16),
                   jax.ShapeDtypeStruct((1, MLA_QK_PAD), F32), jax.ShapeDtypeStruct((1, MLA_QK_PAD), F32)),
        scratch_shapes=[pltpu.VMEM((tm, LANE), F32)],
        compiler_params=_params(("arbitrary", "arbitrary")),
    )(qfull, kv, kr, *tabs, gq, gk, dqf, dkf, dv)


def _cf_layouts(cf, H):
    cfh = cf[:, :H].T
    return cfh[:, :, None], cfh[:, None, :]


def _layer_fwd(i, x, P, tabs, H, hook=None):
    kind, j = i % 3, i // 3
    s = {"x_in": x}
    h = _rmsnorm_fwd(x, P["mix_norm"][i:i + 1], name="mix_norm_fwd")
    s["h"] = h
    if kind == 0:
        w = P["sb"][j]
        qkv = _mm(h, w["w_in"], out_dtype=BF16, name="sb_qkv")
        o, ctot = _sb_fwd(qkv, H)
        s.update(qkv=qkv, o=o, ctot=ctot)
    elif kind == 1:
        w = P["fox"][j]
        qkv = _mm(h, w["w_qkv"], name="fox_qkv")
        flog = _mm(h, w["w_f"], name="fox_gate_proj")
        qn, kn, vb = _fox_prep_fwd(qkv, w["gq"], w["gk"], H)
        cf = _cf_layouts(_fox_gate_fwd(flog, w["bf"]), H)
        o, of, lse = _softmax_fwd(qn, kn, vb, cf, H, HEAD_DIM, 1.0 / math.sqrt(HEAD_DIM), name="fox_attn_fwd")
        s.update(qkv=qkv, flog=flog, qn=qn, kn=kn, vb=vb, cf=cf, o=o, of=of, lse=lse)
    else:
        w = P["mla"][j]
        dq = _mm(h, w["w_dq"], name="mla_down_q")
        dkv = _mm(h, w["w_dkv"], name="mla_down_kv")
        kr = _mm(h, w["w_dr"], name="mla_down_rope")
        cq = _rmsnorm_fwd(dq, w["q_norm"], name="mla_q_norm_fwd")
        ckv = _rmsnorm_fwd(dkv, w["kv_norm"], name="mla_kv_norm_fwd")
        qfull = _mm(cq, w["w_uq"], name="mla_up_q")
        kv = _mm(ckv, w["w_ukv"], name="mla_up_kv")
        qf, kf, v = _mla_prep_fwd(qfull, kv, kr, tabs, w["gq"], w["gk"], H)
        o, of, lse = _softmax_fwd(qf, kf, v, None, H, MLA_QK_PAD, 1.0 / math.sqrt(MLA_QK), name="mla_attn_fwd")
        s.update(dq=dq, dkv=dkv, kr=kr, cq=cq, ckv=ckv, qfull=qfull, kv=kv, qf=qf, kf=kf, v=v,
                 o=o, of=of, lse=lse)
    x = _mm(s["o"], w["w_out"], res=x, name="mixer_out")
    s["x_mid"] = x
    deps = hook(x) if hook is not None else ()
    h2 = _rmsnorm_fwd(x, P["mlp_norm"][i:i + 1], name="mlp_norm_fwd")
    z, u = _mm(h2, P["mlp"][i]["w1"], act="relu2", name="mlp_up", deps=deps)
    x = _mm(u, P["mlp"][i]["w2"], res=x, name="mlp_down")
    s.update(h2=h2, z=z, u=u)
    return x, s


def _layer_bwd(i, dx, dxb, s, P, tabs, H, deps=(), hook=None):
    kind, j = i % 3, i // 3
    G = {}
    wm = P["mlp"][i]
    dz = _mm(dxb, wm["w2"], mode="nt", act="drelu2", z=s["z"], out_dtype=BF16, name="mlp_down_bwd",
             deps=deps)
    G["w2"] = _mm(s["u"], dxb, mode="tn", out_dtype=BF16, name="mlp_w2_grad")
    G["w1"] = _mm(s["h2"], dz, mode="tn", out_dtype=BF16, name="mlp_w1_grad")
    dh2 = _mm(dz, wm["w1"], mode="nt", name="mlp_up_bwd")
    dx, dxb, G["mlp_norm"] = _rmsnorm_bwd(s["x_mid"], P["mlp_norm"][i:i + 1], dh2, dx, name="mlp_norm_bwd")
    h = s["h"]
    w = P[("sb", "fox", "mla")[kind]][j]
    do = _mm(dxb, w["w_out"], mode="nt", out_dtype=BF16, name="mixer_out_bwd",
             deps=hook(dxb, G) if hook is not None else ())
    if kind == 0:
        G["w_out"] = _mm(s["o"], dxb, mode="tn", out_dtype=BF16, name="mixer_out_grad")
        dq, dk, dv = _sb_bwd(s["qkv"], do, s["ctot"], H)
        dproj = jnp.concatenate([dq, dk, dv], axis=1)
        G["w_in"] = _mm(h, dproj, mode="tn", out_dtype=BF16, name="sb_qkv_grad")
        dh = _mm(dproj, w["w_in"], mode="nt", name="sb_qkv_bwd")
    elif kind == 1:
        G["w_out"] = _mm(s["o"], dxb, mode="tn", out_dtype=BF16, name="mixer_out_grad")
        dqn, dkn, dv, dcc, dcr = _softmax_bwd(
            s["qn"], s["kn"], s["vb"], s["cf"], do, s["of"], s["lse"], H, HEAD_DIM,
            1.0 / math.sqrt(HEAD_DIM), name="fox_attn_bwd")
        dq, dk, dvb, G["gq"], G["gk"] = _fox_prep_bwd(s["qkv"], w["gq"], w["gk"], dqn, dkn, dv, H)
        dcf = (dcc[:, :, 0] + dcr[:, 0, :]).T
        dcf = jnp.pad(dcf, ((0, 0), (0, LANE - H)))
        dflog, G["bf"] = _fox_gate_bwd(s["flog"], w["bf"], dcf)
        dproj = jnp.concatenate([dq, dk, dvb], axis=1)
        G["w_qkv"] = _mm(h, dproj, mode="tn", out_dtype=BF16, name="fox_qkv_grad")
        G["w_f"] = _mm(h, dflog, mode="tn", out_dtype=BF16, name="fox_gate_grad")
        dh = _mm(dproj, w["w_qkv"], mode="nt", name="fox_qkv_bwd")
        dh = _mm(dflog, w["w_f"], mode="nt", res=dh, name="fox_gate_bwd_proj")
    else:
        G["w_out"] = _mm(s["o"], dxb, mode="tn", out_dtype=BF16, name="mixer_out_grad")
        dqf, dkf, dv = _softmax_bwd(
            s["qf"], s["kf"], s["v"], None, do, s["of"], s["lse"], H, MLA_QK_PAD,
            1.0 / math.sqrt(MLA_QK), name="mla_attn_bwd")
        dqfull, dkv, dkr, G["gq"], G["gk"] = _mla_prep_bwd(
            s["qfull"], s["kv"], s["kr"], tabs, w["gq"], w["gk"], dqf, dkf, dv, H)
        G["w_uq"] = _mm(s["cq"], dqfull, mode="tn", out_dtype=BF16, name="mla_up_q_grad")
        G["w_ukv"] = _mm(s["ckv"], dkv, mode="tn", out_dtype=BF16, name="mla_up_kv_grad")
        dcq = _mm(dqfull, w["w_uq"], mode="nt", name="mla_up_q_bwd")
        dckv = _mm(dkv, w["w_ukv"], mode="nt", name="mla_up_kv_bwd")
        ddq, G["q_norm"] = _rmsnorm_bwd(s["dq"], w["q_norm"], dcq, want_f32=False, name="mla_q_norm_bwd")
        ddkv, G["kv_norm"] = _rmsnorm_bwd(s["dkv"], w["kv_norm"], dckv, want_f32=False, name="mla_kv_norm_bwd")
        G["w_dq"] = _mm(h, ddq, mode="tn", out_dtype=BF16, name="mla_down_q_grad")
        G["w_dkv"] = _mm(h, ddkv, mode="tn", out_dtype=BF16, name="mla_down_kv_grad")
        G["w_dr"] = _mm(h, dkr, mode="tn", out_dtype=BF16, name="mla_down_rope_grad")
        dh = _mm(ddq, w["w_dq"], mode="nt", name="mla_down_q_bwd")
        dh = _mm(ddkv, w["w_dkv"], mode="nt", res=dh, name="mla_down_kv_bwd")
        dh = _mm(dkr, w["w_dr"], mode="nt", res=dh, name="mla_down_rope_bwd")
    dx, dxb, G["mix_norm"] = _rmsnorm_bwd(s["x_in"], P["mix_norm"][i:i + 1], dh, dx, name="mix_norm_bwd")
    return dx, dxb, G


def _local_step(x, positions, loss_target, P, depth, H):
    tabs = _rope_tables(positions)
    saved = []
    for i in range(depth):
        x, s = _layer_fwd(i, x, P, tabs, H)
        saved.append(s)
    dx, dxb, loss = _loss_head(x, loss_target)
    grads = [None] * depth
    for i in reversed(range(depth)):
        dx, dxb, grads[i] = _layer_bwd(i, dx, dxb, saved[i], P, tabs, H)
    return loss, dx, grads


def _pad_cols(a, n):
    return jnp.pad(a, ((0, 0), (0, n - a.shape[1])))


def _prepare_layer(P, full, i, H, parts=("mix", "mlp")):
    W = H * HEAD_DIM
    kind, j = i % 3, i // 3
    if "mlp" in parts:
        P["mlp"][i] = {"w1": full["mlp_w1"][i], "w2": full["mlp_w2"][i]}
    if "mix" not in parts:
        return
    if kind == 0:
        P["sb"][j] = {"w_in": full["sb_w_in"][j], "w_out": full["sb_w_out"][j]}
    elif kind == 1:
        w = full["fox_w_in"][j]
        P["fox"][j] = {
            "w_qkv": w[:, :3 * W], "w_f": _pad_cols(w[:, 3 * W:], LANE),
            "bf": _pad_cols(full["fox_b_f"][j:j + 1], LANE),
            "gq": full["fox_q_gain"][j:j + 1], "gk": full["fox_k_gain"][j:j + 1],
            "w_out": full["fox_w_out"][j]}
    else:
        w = full["mla_w_in"][j]
        q_norm, kv_norm = full["mla_q_norm"][j], full["mla_kv_norm"][j]
        rq, rkv = q_norm.shape[0], kv_norm.shape[0]
        w_uq = full["mla_w_uq"][j].reshape(rq, H, MLA_QK)
        w_uq = jnp.pad(w_uq, ((0, 0), (0, 0), (0, MLA_QK_PAD - MLA_QK))).reshape(rq, H * MLA_QK_PAD)
        P["mla"][j] = {
            "w_dq": w[:, :rq], "w_dkv": w[:, rq:rq + rkv], "w_dr": _pad_cols(w[:, rq + rkv:], LANE),
            "q_norm": q_norm[None], "kv_norm": kv_norm[None],
            "w_uq": w_uq, "w_ukv": full["mla_w_ukv"][j],
            "gq": _pad_cols(full["mla_q_gain"][j:j + 1], MLA_QK_PAD),
            "gk": _pad_cols(full["mla_k_gain"][j:j + 1], MLA_QK_PAD),
            "w_out": full["mla_w_out"][j]}


def _prepare(full, H):
    depth = len(full["mlp_w1"])
    P = {"mix_norm": full["mix_norm"], "mlp_norm": full["mlp_norm"], "mlp": [None] * depth,
         "sb": [None] * len(full["sb_w_in"]), "fox": [None] * len(full["fox_w_in"]),
         "mla": [None] * len(full["mla_w_in"])}
    for i in range(depth):
        _prepare_layer(P, full, i, H)
    return P


def _layer_grads(i, G, H):
    kind = i % 3
    out = {"mix_norm": G["mix_norm"], "mlp_norm": G["mlp_norm"], "mlp_w1": G["w1"], "mlp_w2": G["w2"]}
    if kind == 0:
        out.update(sb_w_in=G["w_in"], sb_w_out=G["w_out"])
    elif kind == 1:
        out.update(fox_w_in=jnp.concatenate([G["w_qkv"], G["w_f"][:, :H]], axis=1), fox_b_f=G["bf"][:, :H],
                   fox_q_gain=G["gq"], fox_k_gain=G["gk"], fox_w_out=G["w_out"])
    else:
        rq = G["w_uq"].shape[0]
        out.update(
            mla_w_in=jnp.concatenate([G["w_dq"], G["w_dkv"], G["w_dr"][:, :MLA_ROPE]], axis=1),
            mla_q_norm=G["q_norm"], mla_kv_norm=G["kv_norm"],
            mla_w_uq=G["w_uq"].reshape(rq, H, MLA_QK_PAD)[:, :, :MLA_QK].reshape(rq, H * MLA_QK),
            mla_w_ukv=G["w_ukv"], mla_q_gain=G["gq"][:, :MLA_QK], mla_k_gain=G["gk"][:, :MLA_QK],
            mla_w_out=G["w_out"])
    return out


def _layer_of(name, j):
    if name.startswith("sb_"):
        return 3 * j
    if name.startswith("fox_"):
        return 3 * j + 1
    if name.startswith("mla_"):
        return 3 * j + 2
    return j


def _unprepare(grads, H):
    out = {}
    for i, G in enumerate(grads):
        for n, g in _layer_grads(i, G, H).items():
            out.setdefault(n, []).append(g)
    return out


class _Place:
    def __init__(self, x, y, c):
        self.x, self.y, self.c = x, y, c
        self.dev = 4 * x + 2 * y + c
        self.chip = 2 * x + y
        self.id = (x, y, c)


def _peers(me, kind):
    if kind == "ici":
        return [_Place(1 - me.x, me.y, me.c), _Place(me.x, 1 - me.y, me.c), _Place(1 - me.x, 1 - me.y, me.c)]
    return [_Place(me.x, me.y, 1 - me.c)]


def _exchange(name, kind, operands, out_shapes, aliases, n_remote, n_local, plan):
    n_in, n_out = len(operands), len(out_shapes)

    def body(*refs):
        in_refs, out_refs = refs[:n_in], refs[n_in:n_in + n_out]
        send_sems, recv_sems, local_sems = refs[n_in + n_out:]
        me = _Place(lax.axis_index("x"), lax.axis_index("y"), lax.axis_index("c"))
        peers = _peers(me, kind)
        remote, local = plan(me, peers, in_refs, out_refs)
        assert len(remote) == n_remote and len(local) == n_local
        sends = []
        for n, (src, dst, k, _) in enumerate(remote):
            cp = pltpu.make_async_remote_copy(
                src_ref=src, dst_ref=dst, send_sem=send_sems.at[n], recv_sem=recv_sems.at[n],
                device_id=peers[k].id, device_id_type=MESH)
            cp.start()
            sends.append(cp)
        own = []
        for n, (src, dst) in enumerate(local):
            cp = pltpu.make_async_copy(src, dst, local_sems.at[n])
            cp.start()
            own.append(cp)
        for n, (src, _, k, landing) in enumerate(remote):
            pltpu.make_async_remote_copy(
                src_ref=src, dst_ref=landing, send_sem=send_sems.at[n], recv_sem=recv_sems.at[n],
                device_id=peers[k].id, device_id_type=MESH).wait_recv()
        for cp in sends:
            cp.wait_send()
        for cp in own:
            cp.wait()

    outs = pl.pallas_call(
        body, name=name, in_specs=[ANY] * n_in, out_specs=tuple([ANY] * n_out),
        out_shape=tuple(out_shapes), input_output_aliases=aliases,
        scratch_shapes=[pltpu.SemaphoreType.DMA((n_remote,)), pltpu.SemaphoreType.DMA((n_remote,)),
                        pltpu.SemaphoreType.DMA((max(n_local, 1),))],
    )(*operands)
    return list(outs)


def _window(ref, kind, d, shard_shape):
    r, c = shard_shape
    if kind == "col":
        return ref.at[:, pl.ds(pl.multiple_of(d * c, c), c)]
    return ref.at[pl.ds(pl.multiple_of(d * r, r), r), :]


def _full_shape(kind, shard_shape):
    r, c = shard_shape
    return (r, N_DEV * c) if kind == "col" else (N_DEV * r, c)


HBM = pl.BlockSpec(memory_space=pltpu.HBM)
SEM = pl.BlockSpec(memory_space=pltpu.SEMAPHORE)
EFFECT = pltpu.SideEffectType.DATAFLOW_SIDE_EFFECTING


class _Flight:
    def __init__(self, name, kind, n_remote, plan, send_sems, recv_sems, bufs, token):
        self.name, self.kind, self.n_remote, self.plan = name, kind, n_remote, plan
        self.send_sems, self.recv_sems, self.bufs, self.token = send_sems, recv_sems, bufs, token


def _start_copies(name, kind, bufs, n_remote, plan, after=None):
    nb = len(bufs)
    n_after = 0 if after is None else 1

    def body(*refs):
        in_refs = refs[:nb]
        send_sems, recv_sems = refs[nb + n_after], refs[nb + n_after + 1]
        token = refs[2 * nb + n_after + 2]
        me = _Place(lax.axis_index("x"), lax.axis_index("y"), lax.axis_index("c"))
        peers = _peers(me, kind)
        for n, (src, dst, k, _) in enumerate(plan(me, peers, in_refs)):
            pltpu.make_async_remote_copy(
                src_ref=src, dst_ref=dst, send_sem=send_sems.at[n], recv_sem=recv_sems.at[n],
                device_id=peers[k].id, device_id_type=MESH).start()
        token[...] = jnp.zeros_like(token)

    outs = pl.pallas_call(
        body, name=name, in_specs=[HBM] * nb + [ANY] * n_after,
        out_specs=(SEM, SEM) + (HBM,) * nb + (pl.BlockSpec(memory_space=pltpu.VMEM),),
        out_shape=(pltpu.SemaphoreType.DMA((n_remote,)), pltpu.SemaphoreType.DMA((n_remote,)))
        + tuple(pltpu.HBM(b.shape, b.dtype) for b in bufs) + (jax.ShapeDtypeStruct((8, LANE), F32),),
        input_output_aliases={a: 2 + a for a in range(nb)},
        compiler_params=pltpu.CompilerParams(has_side_effects=EFFECT),
    )(*[pltpu.with_memory_space_constraint(b, pltpu.HBM) for b in bufs], *([after] if n_after else []))
    return _Flight(name, kind, n_remote, plan, outs[0], outs[1], list(outs[2:2 + nb]), outs[2 + nb])


def _wait_copies(flight, after):
    nb = len(flight.bufs)
    plan, kind = flight.plan, flight.kind

    def body(*refs):
        in_refs = refs[:nb]
        send_sems, recv_sems = refs[nb], refs[nb + 1]
        me = _Place(lax.axis_index("x"), lax.axis_index("y"), lax.axis_index("c"))
        peers = _peers(me, kind)
        for n, (src, _, k, landing) in enumerate(plan(me, peers, in_refs)):
            cp = pltpu.make_async_remote_copy(
                src_ref=src, dst_ref=landing, send_sem=send_sems.at[n], recv_sem=recv_sems.at[n],
                device_id=peers[k].id, device_id_type=MESH)
            cp.wait_send()
            cp.wait_recv()

    outs = pl.pallas_call(
        body, name=flight.name.replace("_start", "_wait"),
        in_specs=[HBM] * nb + [SEM, SEM, ANY], out_specs=(HBM,) * nb,
        out_shape=tuple(pltpu.HBM(b.shape, b.dtype) for b in flight.bufs),
        input_output_aliases={a: a for a in range(nb)},
        compiler_params=pltpu.CompilerParams(has_side_effects=EFFECT),
    )(*flight.bufs, flight.send_sems, flight.recv_sems, after)
    return list(outs)


def _place_own(shards, kinds, tag):
    n = len(shards)

    def body(*refs):
        dev = 4 * lax.axis_index("x") + 2 * lax.axis_index("y") + lax.axis_index("c")
        sem = refs[2 * n]
        cps = [pltpu.make_async_copy(refs[a], _window(refs[n + a], kinds[a], dev, shards[a].shape), sem.at[a])
               for a in range(n)]
        for cp in cps:
            cp.start()
        for cp in cps:
            cp.wait()

    return list(pl.pallas_call(
        body, name="place_own_" + tag, in_specs=[ANY] * n, out_specs=(ANY,) * n,
        out_shape=tuple(jax.ShapeDtypeStruct(_full_shape(k, s.shape), s.dtype) for k, s in zip(kinds, shards)),
        scratch_shapes=[pltpu.SemaphoreType.DMA((n,))],
    )(*shards))


def _gather_ici_start(shards, fulls, kinds, tag, after=None):
    n = len(shards)
    shapes = [s.shape for s in shards]

    def plan(me, peers, refs):
        remote = []
        for a in range(n):
            mine = _window(refs[n + a], kinds[a], me.dev, shapes[a])
            for k, p in enumerate(peers):
                remote.append((refs[a], mine, k, _window(refs[n + a], kinds[a], p.dev, shapes[a])))
        return remote

    return _start_copies("gather_ici_start_" + tag, "ici", list(shards) + list(fulls), 3 * n, plan, after)


def _gather_d2d_start(fulls, kinds, shapes, tag):
    n = len(fulls)

    def plan(me, peers, refs):
        remote = []
        for a in range(n):
            for ch in range(N_CHIP):
                held = _window(refs[a], kinds[a], 2 * ch + me.c, shapes[a])
                remote.append((held, held, 0, _window(refs[a], kinds[a], 2 * ch + 1 - me.c, shapes[a])))
        return remote

    return _start_copies("gather_d2d_start_" + tag, "d2d", fulls, N_CHIP * n, plan)


def _scatter_d2d_start(grads, kinds, shapes, tag):
    n = len(grads)
    lands = [lax.empty((N_CHIP,) + tuple(s), g.dtype) for g, s in zip(grads, shapes)]

    def plan(me, peers, refs):
        remote = []
        for a in range(n):
            for ch in range(N_CHIP):
                remote.append((_window(refs[a], kinds[a], 2 * ch + 1 - me.c, shapes[a]),
                               refs[n + a].at[ch], 0, refs[n + a].at[ch]))
        return remote

    return _start_copies("scatter_d2d_start_" + tag, "d2d", list(grads) + lands, N_CHIP * n, plan)


def _scatter_ici_start(sums, shapes, tag, after=None):
    n = len(sums)
    lands = [lax.empty((N_CHIP - 1,) + tuple(s), BF16) for s in shapes]

    def plan(me, peers, refs):
        remote = []
        for a in range(n):
            for k, p in enumerate(peers):
                remote.append((refs[a].at[p.chip], refs[n + a].at[k], k, refs[n + a].at[k]))
        return remote

    return _start_copies("scatter_ici_start_" + tag, "ici", list(sums) + lands, 3 * n, plan, after)


def _pair_add(g, r4, kind, shard_shape, core, *, name):
    r, c = shard_shape
    tr = _row_tile(r, c, 524288)
    nb = r // tr
    if kind == "all":
        def body_all(core_ref, g_ref, r_ref, o_ref):
            o_ref[...] = g_ref[...] + r_ref[...]

        spec = pl.BlockSpec((tr, c), lambda i, core_ref: (i, 0))
        return pl.pallas_call(
            body_all, name=name,
            grid_spec=pltpu.PrefetchScalarGridSpec(
                num_scalar_prefetch=1, grid=(nb,), in_specs=[spec, spec], out_specs=spec),
            out_shape=jax.ShapeDtypeStruct((r, c), F32),
            compiler_params=_params(("parallel",)),
        )(core, g, r4)

    def body(core_ref, g_ref, r_ref, o_ref):
        o_ref[...] = (g_ref[...].astype(F32) + r_ref[...].astype(F32)).astype(BF16)

    if kind == "col":
        g_spec = pl.BlockSpec((tr, c), lambda ch, i, core_ref: (i, 2 * ch + core_ref[0]))
    else:
        g_spec = pl.BlockSpec((tr, c), lambda ch, i, core_ref: ((2 * ch + core_ref[0]) * nb + i, 0))
    slot = pl.BlockSpec((None, tr, c), lambda ch, i, core_ref: (ch, i, 0))
    return pl.pallas_call(
        body, name=name,
        grid_spec=pltpu.PrefetchScalarGridSpec(
            num_scalar_prefetch=1, grid=(N_CHIP, nb), in_specs=[g_spec, slot], out_specs=slot),
        out_shape=jax.ShapeDtypeStruct((N_CHIP, r, c), BF16),
        compiler_params=_params(("parallel", "parallel")),
    )(core, g, r4)


def _reduce_scatter(grads, kinds, shard_shapes, core, tag):
    n = len(grads)
    land = []
    for g, k, s in zip(grads, kinds, shard_shapes):
        land.append(jax.ShapeDtypeStruct(s if k == "all" else (N_CHIP,) + tuple(s), g.dtype))
    n_rem = sum(1 if k == "all" else N_CHIP for k in kinds)

    def plan_d2d(me, peers, ins, outs):
        remote = []
        for a in range(n):
            if kinds[a] == "all":
                remote.append((ins[a], outs[a], 0, outs[a]))
                continue
            for ch in range(N_CHIP):
                remote.append((_window(ins[a], kinds[a], 2 * ch + 1 - me.c, shard_shapes[a]),
                               outs[a].at[ch], 0, outs[a].at[ch]))
        return remote, []

    got = _exchange("scatter_d2d_" + tag, "d2d", list(grads), land, {}, n_rem, 0, plan_d2d)
    sums = [_pair_add(g, r4, k, s, core, name="pair_add_" + tag)
            for g, r4, k, s in zip(grads, got, kinds, shard_shapes)]
    land2 = [jax.ShapeDtypeStruct((N_CHIP,) + tuple(s), F32 if k == "all" else BF16)
             for k, s in zip(kinds, shard_shapes)]

    def plan_ici(me, peers, ins, outs):
        remote, local = [], []
        for a in range(n):
            for k, p in enumerate(peers):
                src = ins[a] if kinds[a] == "all" else ins[a].at[p.chip]
                remote.append((src, outs[a].at[me.chip], k, outs[a].at[p.chip]))
            src = ins[a] if kinds[a] == "all" else ins[a].at[me.chip]
            local.append((src, outs[a].at[me.chip]))
        return remote, local

    return _exchange("scatter_ici_" + tag, "ici", sums, land2, {}, 3 * n, n, plan_ici)


def _row_tile(r, c, limit=262144):
    best = None
    for t in range(8, r + 1, 8):
        if r % t == 0 and t * c <= limit:
            best = t
    return best if best is not None else r


def _adam_math(w, g, m, v):
    m = ADAM_B1 * m + (1.0 - ADAM_B1) * g
    v = ADAM_B2 * v + (1.0 - ADAM_B2) * (g * g)
    m_hat = m / (1.0 - ADAM_B1 ** ADAM_STEP)
    v_hat = v / (1.0 - ADAM_B2 ** ADAM_STEP)
    delta = -ADAM_LR * (m_hat / (jnp.sqrt(v_hat) + ADAM_EPS) + ADAM_WD * w)
    return delta, m, v


def _adamw(w, m, v, sums, lands, chip, layer, prev, *, name):
    L, r, c = w.shape
    tr = _row_tile(r, c)
    has_prev = prev is not None

    def body(chip_ref, *refs):
        w_ref, m_ref, v_ref, s_ref, l_ref = refs[:5]
        g_ref, d_ref, nm_ref, nv_ref = refs[-4:]
        g = (s_ref[...].astype(F32) + l_ref[0].astype(F32)) + l_ref[1].astype(F32) + l_ref[2].astype(F32)
        delta, nm, nv = _adam_math(w_ref[...], g, m_ref[...], v_ref[...])
        g_ref[...] = g
        d_ref[...] = delta
        nm_ref[...] = nm
        nv_ref[...] = nv

    slab = pl.BlockSpec((None, tr, c), lambda i, chip_ref: (layer, i, 0))
    in_specs = [slab, slab, slab, pl.BlockSpec((None, tr, c), lambda i, chip_ref: (chip_ref[0], i, 0)),
                pl.BlockSpec((N_CHIP - 1, tr, c), lambda i, chip_ref: (0, i, 0))]
    args = [chip, w, m, v, sums, lands]
    aliases = {}
    if has_prev:
        in_specs += [ANY] * 4
        args += list(prev)
        aliases = {6 + n: n for n in range(4)}
    return pl.pallas_call(
        body, name=name,
        grid_spec=pltpu.PrefetchScalarGridSpec(
            num_scalar_prefetch=1, grid=(r // tr,), in_specs=in_specs, out_specs=(slab,) * 4),
        out_shape=tuple(jax.ShapeDtypeStruct((L, r, c), F32) for _ in range(4)),
        input_output_aliases=aliases, compiler_params=_params(("parallel",)),
    )(*args)


def _adamw_small(w, g, m, v):
    def body(w_ref, g_ref, m_ref, v_ref, d_ref, nm_ref, nv_ref):
        d_ref[...], nm_ref[...], nv_ref[...] = _adam_math(w_ref[...], g_ref[...], m_ref[...], v_ref[...])

    return pl.pallas_call(
        body, name="adamw_small", out_shape=tuple(jax.ShapeDtypeStruct(w.shape, F32) for _ in range(3)),
    )(w, g, m, v)


def _sum4(parts):
    def body(p_ref, o_ref):
        o_ref[...] = (p_ref[0] + p_ref[1]) + p_ref[2] + p_ref[3]

    return pl.pallas_call(
        body, name="sum_chips", out_shape=jax.ShapeDtypeStruct(parts.shape[1:], F32))(parts)


_WEIGHTS = ["mix_norm", "mlp_norm", "sb_w_in", "sb_w_out", "fox_w_in", "fox_b_f", "fox_q_gain", "fox_k_gain",
            "fox_w_out", "mla_w_in", "mla_q_norm", "mla_kv_norm", "mla_w_uq", "mla_w_ukv", "mla_q_gain",
            "mla_k_gain", "mla_w_out", "mlp_w1", "mlp_w2"]
_BIG = {"sb_w_in": "col", "sb_w_out": "row", "fox_w_in": "row", "fox_w_out": "row", "mla_w_in": "row",
        "mla_w_uq": "col", "mla_w_ukv": "col", "mla_w_out": "row", "mlp_w1": "col", "mlp_w2": "row"}


def _layer_big(i):
    kind, j = i % 3, i // 3
    mixer = {0: ["sb_w_in", "sb_w_out"], 1: ["fox_w_in", "fox_w_out"],
             2: ["mla_w_in", "mla_w_uq", "mla_w_ukv", "mla_w_out"]}[kind]
    return [(n, j) for n in mixer] + [("mlp_w1", i), ("mlp_w2", i)]


def _stack_to_cols(a):
    r = a.shape[0] // N_DEV
    return a.reshape(N_DEV, r, a.shape[1]).transpose(1, 0, 2).reshape(r, N_DEV * a.shape[1])


def _cols_to_stack(a):
    c = a.shape[1] // N_DEV
    return a.reshape(a.shape[0], N_DEV, c).transpose(1, 0, 2).reshape(N_DEV * a.shape[0], c)


def _pack_rows(rows, width):
    rows = [jnp.pad(r.reshape(-1).astype(F32), (0, width - r.size)) for r in rows]
    pad = (-len(rows)) % 8
    rows += [jnp.zeros((width,), F32)] * pad
    return jnp.stack(rows)


def kernel(x, positions, mix_norm, mlp_norm, sb_w_in, sb_w_out, fox_w_in, fox_b_f, fox_q_gain, fox_k_gain, fox_w_out, mla_w_in, mla_q_norm, mla_kv_norm, mla_w_uq, mla_w_ukv, mla_q_gain, mla_k_gain, mla_w_out, mlp_w1, mlp_w2, loss_target, m_mix_norm, m_mlp_norm, m_sb_w_in, m_sb_w_out, m_fox_w_in, m_fox_b_f, m_fox_q_gain, m_fox_k_gain, m_fox_w_out, m_mla_w_in, m_mla_q_norm, m_mla_kv_norm, m_mla_w_uq, m_mla_w_ukv, m_mla_q_gain, m_mla_k_gain, m_mla_w_out, m_mlp_w1, m_mlp_w2, v_mix_norm, v_mlp_norm, v_sb_w_in, v_sb_w_out, v_fox_w_in, v_fox_b_f, v_fox_q_gain, v_fox_k_gain, v_fox_w_out, v_mla_w_in, v_mla_q_norm, v_mla_kv_norm, v_mla_w_uq, v_mla_w_ukv, v_mla_q_gain, v_mla_k_gain, v_mla_w_out, v_mlp_w1, v_mlp_w2):
    w_in = dict(zip(_WEIGHTS, (mix_norm, mlp_norm, sb_w_in, sb_w_out, fox_w_in, fox_b_f, fox_q_gain, fox_k_gain, fox_w_out, mla_w_in, mla_q_norm, mla_kv_norm, mla_w_uq, mla_w_ukv, mla_q_gain, mla_k_gain, mla_w_out, mlp_w1, mlp_w2)))
    m_in = dict(zip(_WEIGHTS, (m_mix_norm, m_mlp_norm, m_sb_w_in, m_sb_w_out, m_fox_w_in, m_fox_b_f, m_fox_q_gain, m_fox_k_gain, m_fox_w_out, m_mla_w_in, m_mla_q_norm, m_mla_kv_norm, m_mla_w_uq, m_mla_w_ukv, m_mla_q_gain, m_mla_k_gain, m_mla_w_out, m_mlp_w1, m_mlp_w2)))
    v_in = dict(zip(_WEIGHTS, (v_mix_norm, v_mlp_norm, v_sb_w_in, v_sb_w_out, v_fox_w_in, v_fox_b_f, v_fox_q_gain, v_fox_k_gain, v_fox_w_out, v_mla_w_in, v_mla_q_norm, v_mla_kv_norm, v_mla_w_uq, v_mla_w_ukv, v_mla_q_gain, v_mla_k_gain, v_mla_w_out, v_mlp_w1, v_mlp_w2)))
    depth, D = mix_norm.shape
    H = D // HEAD_DIM
    n_mla = mla_w_in.shape[0]
    dev = 4 * lax.axis_index("x") + 2 * lax.axis_index("y") + lax.axis_index("c")
    core = lax.axis_index("c").astype(jnp.int32).reshape(1)

    chip = (2 * lax.axis_index("x") + lax.axis_index("y")).astype(jnp.int32).reshape(1)
    nq, nkv = mla_q_norm.shape[1], mla_kv_norm.shape[1]

    units = []
    for i in range(depth):
        big = _layer_big(i)
        units += [(i, ("mix",), big[:-2]), (i, ("mlp",), big[-2:])] if i == 0 else [(i, ("mix", "mlp"), big)]

    def unit_of(i):
        return i + 1 if i else 0

    def tag_of(u):
        i, parts, _ = units[u]
        return f"l{i}" if len(parts) == 2 else f"l{i}_{parts[0]}"

    shards_of, kinds_of = [], []
    for i, parts, names in units:
        shards = [w_in[n][j].astype(BF16) for n, j in names]
        kinds = [_BIG[n] for n, _ in names]
        if i % 3 == 2:
            shards.append(_pack_rows([mla_q_norm[i // 3], mla_kv_norm[i // 3]], LANE))
            kinds.append("row")
        shards_of.append(shards)
        kinds_of.append(kinds)
    shapes_of = [[s.shape for s in shards] for shards in shards_of]
    placed = _place_own(sum(shards_of, []), sum(kinds_of, []), "all")
    ici, at = [], 0
    for u, shards in enumerate(shards_of):
        ici.append(_gather_ici_start(shards, placed[at:at + len(shards)], kinds_of[u], tag_of(u),
                                     ici[-1].token if ici else None))
        at += len(shards)

    full = {n: [None] * w_in[n].shape[0] for n in _WEIGHTS}
    for n in ("mix_norm", "mlp_norm", "fox_b_f", "fox_q_gain", "fox_k_gain", "mla_q_gain", "mla_k_gain"):
        full[n] = w_in[n]
    P = {"mix_norm": mix_norm, "mlp_norm": mlp_norm, "mlp": [None] * depth,
         "sb": [None] * sb_w_in.shape[0], "fox": [None] * fox_w_in.shape[0], "mla": [None] * n_mla}

    def forward_to_sibling(u, after):
        n = len(kinds_of[u])
        arrived = _wait_copies(ici[u], after)[n:]
        return _gather_d2d_start(arrived, kinds_of[u], shapes_of[u], tag_of(u))

    def finish_gather(u, flight, after):
        i, parts, names = units[u]
        got = _wait_copies(flight, after)
        for (n, j), a in zip(names, got):
            full[n][j] = _stack_to_cols(a) if n == "fox_w_in" else a
        if i % 3 == 2:
            tiles = got[-1].reshape(N_DEV, 8, LANE)
            full["mla_q_norm"][i // 3] = tiles[:, 0, :nq].reshape(-1)
            full["mla_kv_norm"][i // 3] = tiles[:, 1, :nkv].reshape(-1)
        _prepare_layer(P, full, i, H, parts)

    tabs = _rope_tables(positions[0])
    d2d = forward_to_sibling(0, ici[-1].token)
    finish_gather(0, d2d, d2d.token)
    xs, saved = x[0], []
    for i in range(depth):
        nxt = {}

        def hook(x_mid, i=i, nxt=nxt):
            if i == 0:
                mlp_d2d = forward_to_sibling(1, x_mid)
                finish_gather(1, mlp_d2d, mlp_d2d.token)
                return ()
            if i + 1 == depth:
                return ()
            nxt["d2d"] = forward_to_sibling(unit_of(i + 1), x_mid)
            return (nxt["d2d"].token,)

        xs, s = _layer_fwd(i, xs, P, tabs, H, hook)
        saved.append(s)
        if i == 0 and depth > 1:
            nxt["d2d"] = forward_to_sibling(unit_of(1), xs)
            finish_gather(unit_of(1), nxt["d2d"], nxt["d2d"].token)
        elif i + 1 < depth:
            finish_gather(unit_of(i + 1), nxt["d2d"], xs)
    dx, dxb, loss = _loss_head(xs, loss_target[0])
    loss = lax.psum(loss[0, 0], ("x", "y", "c"))

    def unit_meta(u):
        names = units[u][2]
        return [_BIG[n] for n, _ in names], [w_in[n].shape[1:] for n, _ in names]

    def to_sibling_start(u, Gl):
        grads = [(_cols_to_stack(Gl[n]) if n == "fox_w_in" else Gl[n]) for n, _ in units[u][2]]
        return _scatter_d2d_start(grads, *unit_meta(u), tag_of(u))

    def pair_and_send(u, flight, after, before=None):
        n = len(units[u][2])
        kinds, shapes = unit_meta(u)
        got = _wait_copies(flight, after)
        sums = [_pair_add(g, r4, k, s, core, name="pair_add_" + tag_of(u))
                for g, r4, k, s in zip(got[:n], got[n:], kinds, shapes)]
        return _scatter_ici_start(sums, shapes, tag_of(u), before)

    G, to_chips, deps, flying = [None] * depth, [None] * len(units), (), {}
    for i in reversed(range(depth)):
        def hook(dxb_mid, Gp, i=i):
            out = ()
            if i + 1 < depth:
                u = unit_of(i + 1)
                to_chips[u] = pair_and_send(u, flying.pop(u), dxb_mid)
                out = (to_chips[u].token,)
            if i == 0:
                flying[1] = to_sibling_start(1, {"mlp_w1": Gp["w1"], "mlp_w2": Gp["w2"]})
                out = out + (flying[1].token,)
            return out

        dx, dxb, Gp = _layer_bwd(i, dx, dxb, saved[i], P, tabs, H, deps, hook)
        G[i] = _layer_grads(i, Gp, H)
        flying[unit_of(i)] = to_sibling_start(unit_of(i), G[i])
        deps = (flying[unit_of(i)].token,)
    grad_x = dx

    small = [n for n in _WEIGHTS if n not in _BIG]
    rows = []
    for n in small:
        for j in range(w_in[n].shape[0]):
            rows.append(G[_layer_of(n, j)][n])
    sg = _pack_rows(rows, D)
    (parts,) = _reduce_scatter([sg], ["all"], [sg.shape], core, "small")
    sgr = _sum4(parts)
    to_chips[1] = pair_and_send(1, flying.pop(1), dx, sgr)
    to_chips[0] = pair_and_send(0, flying.pop(0), to_chips[1].token)

    results = {}
    for u in reversed(range(len(units))):
        names = units[u][2]
        got = _wait_copies(to_chips[u], to_chips[0].token)
        for a, (n, j) in enumerate(names):
            results[n] = _adamw(w_in[n], m_in[n], v_in[n], got[a], got[len(names) + a], chip, j, results.get(n),
                                name="adamw_" + n)

    g_rows, w_rows, m_rows, v_rows, at = [], [], [], [], 0
    for n in small:
        for j in range(w_in[n].shape[0]):
            width = w_in[n].shape[1]
            if n in ("mla_q_norm", "mla_kv_norm"):
                g_rows.append(lax.dynamic_slice(sgr[at], (dev * width,), (width,)))
            else:
                g_rows.append(sgr[at, :width])
            w_rows.append(w_in[n][j])
            m_rows.append(m_in[n][j])
            v_rows.append(v_in[n][j])
            at += 1
    g_pack = _pack_rows(g_rows, D)
    d_pack, nm_pack, nv_pack = _adamw_small(
        _pack_rows(w_rows, D), g_pack, _pack_rows(m_rows, D), _pack_rows(v_rows, D))
    at = 0
    for n in small:
        L, width = w_in[n].shape
        results[n] = tuple(p[at:at + L, :width] for p in (g_pack, d_pack, nm_pack, nv_pack))
        at += L

    out = [loss, grad_x[None]]
    for part in range(4):
        out += [results[n][part] for n in _WEIGHTS]
    return tuple(out)
```

```python
import functools
import math

import jax
import jax.numpy as jnp
from jax import lax
from jax.experimental import pallas as pl
from jax.experimental.pallas import tpu as pltpu

F32 = jnp.float32
BF16 = jnp.bfloat16

HEAD_DIM = 128
MLA_NOPE = 128
MLA_ROPE = 64
MLA_V = 128
MLA_QK = MLA_NOPE + MLA_ROPE
MLA_QK_PAD = 256
LANE = 128
ROPE_THETA = 10000.0
EPS = 1e-6
ADAM_LR = 0.001
ADAM_B1 = 0.9
ADAM_B2 = 0.999
ADAM_EPS = 1e-08
ADAM_WD = 0.01
ADAM_STEP = 10
N_DEV = 8
N_CHIP = 4
NEG = -1e30
VMEM_LIMIT = 56 * 1024 * 1024
ATT_BLOCK = 256
ATT_STRIP = 32
ATT_HEADS = 2
MESH = pl.DeviceIdType.MESH
ANY = pl.BlockSpec(memory_space=pl.ANY)


def _params(sem):
    return pltpu.CompilerParams(dimension_semantics=sem, vmem_limit_bytes=VMEM_LIMIT)


def _tile(dim, pref):
    if dim <= pref:
        return dim
    t = pref
    while dim % t:
        t -= LANE
    return t


def _dot(a, b, dims):
    return lax.dot_general(a, b, (dims, ((), ())), preferred_element_type=F32)


NN = ((1,), (0,))
NT = ((1,), (1,))
TN = ((0,), (0,))


def _mm(a, b, *, mode="nn", out_dtype=F32, res=None, act=None, z=None, name,
        tm=1024, tn=512, tk=2048, deps=()):
    if mode == "nn":
        (M, K), (_, N) = a.shape, b.shape
    elif mode == "nt":
        (M, K), (N, _) = a.shape, b.shape
    else:
        (K, M), (_, N) = a.shape, b.shape
    tm, tn, tk = _tile(M, tm), _tile(N, tn), _tile(K, tk)
    nk = K // tk
    if mode == "tn":
        a_spec = pl.BlockSpec((tk, tm), lambda i, j, k: (k, i))
    else:
        a_spec = pl.BlockSpec((tm, tk), lambda i, j, k: (i, k))
    if mode == "nt":
        b_spec = pl.BlockSpec((tn, tk), lambda i, j, k: (j, k))
    else:
        b_spec = pl.BlockSpec((tk, tn), lambda i, j, k: (k, j))
    dims = {"nn": NN, "nt": NT, "tn": TN}[mode]
    o_spec = pl.BlockSpec((tm, tn), lambda i, j, k: (i, j))
    in_specs, args = [a_spec, b_spec], [a, b]
    if res is not None:
        in_specs.append(o_spec)
        args.append(res)
    if act == "drelu2":
        in_specs.append(o_spec)
        args.append(z)
    if act == "relu2":
        out_shape = (jax.ShapeDtypeStruct((M, N), F32), jax.ShapeDtypeStruct((M, N), BF16))
        out_specs = (o_spec, o_spec)
    else:
        out_shape = jax.ShapeDtypeStruct((M, N), out_dtype)
        out_specs = o_spec
    has_res, has_z = res is not None, act == "drelu2"
    n_out = 2 if act == "relu2" else 1
    in_specs += [ANY] * len(deps)
    args += list(deps)

    def body(*refs):
        a_ref, b_ref = refs[0], refs[1]
        idx = 2
        res_ref = z_ref = None
        if has_res:
            res_ref = refs[idx]
            idx += 1
        if has_z:
            z_ref = refs[idx]
            idx += 1
        idx += len(deps)
        outs = refs[idx:idx + n_out]

        def finish(r):
            if has_res:
                r = r + res_ref[...]
            if act == "relu2":
                outs[0][...] = r
                rr = jnp.maximum(r, 0.0)
                outs[1][...] = (rr * rr).astype(BF16)
            elif act == "drelu2":
                outs[0][...] = (r * (2.0 * jnp.maximum(z_ref[...], 0.0))).astype(out_dtype)
            else:
                outs[0][...] = r.astype(out_dtype)

        if nk == 1:
            finish(_dot(a_ref[...], b_ref[...], dims))
            return
        acc = refs[-1]
        k = pl.program_id(2)

        @pl.when(k == 0)
        def _():
            acc[...] = _dot(a_ref[...], b_ref[...], dims)

        @pl.when((k > 0) & (k < nk - 1))
        def _():
            acc[...] += _dot(a_ref[...], b_ref[...], dims)

        @pl.when(k == nk - 1)
        def _():
            finish(acc[...] + _dot(a_ref[...], b_ref[...], dims))

    return pl.pallas_call(
        body, name=name, grid=(M // tm, N // tn, nk), in_specs=in_specs, out_specs=out_specs,
        out_shape=out_shape, scratch_shapes=[pltpu.VMEM((tm, tn), F32)] if nk > 1 else [],
        compiler_params=_params(("parallel", "parallel", "arbitrary")),
    )(*args)


def _rmsnorm_fwd(x, g, *, name, tm=256):
    T, n = x.shape
    tm = _tile(T, tm)

    def body(x_ref, g_ref, o_ref):
        xf = x_ref[...]
        r = lax.rsqrt(jnp.mean(xf * xf, axis=-1, keepdims=True) + EPS)
        o_ref[...] = (xf * r * g_ref[...]).astype(BF16)

    return pl.pallas_call(
        body, name=name, grid=(T // tm,),
        in_specs=[pl.BlockSpec((tm, n), lambda i: (i, 0)), pl.BlockSpec((1, n), lambda i: (0, 0))],
        out_specs=pl.BlockSpec((tm, n), lambda i: (i, 0)),
        out_shape=jax.ShapeDtypeStruct((T, n), BF16),
        compiler_params=_params(("parallel",)),
    )(x, g)


def _rmsnorm_bwd(x, g, dy, dx_in=None, *, name, want_f32=True, tm=256):
    T, n = x.shape
    tm = _tile(T, tm)
    has_in = dx_in is not None
    row = pl.BlockSpec((tm, n), lambda i: (i, 0))
    vec = pl.BlockSpec((1, n), lambda i: (0, 0))

    def body(*refs):
        x_ref, g_ref, dy_ref = refs[:3]
        idx = 3
        in_ref = None
        if has_in:
            in_ref = refs[idx]
            idx += 1
        outs = refs[idx:]
        xf = x_ref[...]
        r = lax.rsqrt(jnp.mean(xf * xf, axis=-1, keepdims=True) + EPS)
        dyf = dy_ref[...]
        t = dyf * g_ref[...]
        dx = r * t - xf * (r * r * r * jnp.mean(t * xf, axis=-1, keepdims=True))
        if has_in:
            dx = dx + in_ref[...]
        o = 0
        if want_f32:
            outs[0][...] = dx
            o = 1
        outs[o][...] = dx.astype(BF16)
        dg_ref = outs[o + 1]

        @pl.when(pl.program_id(0) == 0)
        def _():
            dg_ref[...] = jnp.zeros_like(dg_ref)

        dg_ref[...] += jnp.sum(dyf * xf * r, axis=0, keepdims=True)

    in_specs = [row, vec, row] + ([row] if has_in else [])
    out_specs, out_shape = [], []
    if want_f32:
        out_specs.append(row)
        out_shape.append(jax.ShapeDtypeStruct((T, n), F32))
    out_specs += [row, vec]
    out_shape += [jax.ShapeDtypeStruct((T, n), BF16), jax.ShapeDtypeStruct((1, n), F32)]
    args = [x, g, dy] + ([dx_in] if has_in else [])
    return pl.pallas_call(
        body, name=name, grid=(T // tm,), in_specs=in_specs, out_specs=tuple(out_specs),
        out_shape=tuple(out_shape), compiler_params=_params(("arbitrary",)),
    )(*args)


def _loss_head(y, target, *, tm=256):
    T, n = y.shape
    tm = _tile(T, tm)
    row = pl.BlockSpec((tm, n), lambda i: (i, 0))

    def body(y_ref, t_ref, dy_ref, dyb_ref, loss_ref):
        err = y_ref[...] - t_ref[...]
        dy = err * (1.0 / n)
        dy_ref[...] = dy
        dyb_ref[...] = dy.astype(BF16)

        @pl.when(pl.program_id(0) == 0)
        def _():
            loss_ref[...] = jnp.zeros_like(loss_ref)

        part = 0.5 * jnp.sum(jnp.mean(err * err, axis=-1, keepdims=True), axis=0, keepdims=True)
        loss_ref[...] += part

    return pl.pallas_call(
        body, name="loss_head", grid=(T // tm,), in_specs=[row, row],
        out_specs=(row, row, pl.BlockSpec((1, 1), lambda i: (0, 0))),
        out_shape=(jax.ShapeDtypeStruct((T, n), F32), jax.ShapeDtypeStruct((T, n), BF16),
                   jax.ShapeDtypeStruct((1, 1), F32)),
        compiler_params=_params(("arbitrary",)),
    )(y, target)


def _split3(v):
    hi = v.astype(BF16)
    r1 = v - hi.astype(F32)
    mid = r1.astype(BF16)
    lo = (r1 - mid.astype(F32)).astype(BF16)
    return hi, mid, lo


def _tri_right(v, tri):
    hi, mid, lo = _split3(v)
    return _dot(hi, tri, NN) + _dot(mid, tri, NN) + _dot(lo, tri, NN)


def _tri_left(tri, v):
    hi, mid, lo = _split3(v)
    return _dot(tri, hi, NN) + _dot(tri, mid, NN) + _dot(tri, lo, NN)


def _iota2(shape, dim):
    return lax.broadcasted_iota(jnp.int32, shape, dim)


def _log_sigmoid(z):
    return jnp.minimum(z, 0.0) - jnp.log1p(jnp.exp(-jnp.abs(z)))


def _strip(B):
    return min(ATT_STRIP, B)


def _head_spec(rows, width, off):
    return pl.BlockSpec((rows, width), lambda h, i: (i, off + h))


def _full_head_spec(T, width, off):
    return pl.BlockSpec((T, width), lambda h, i: (0, off + h))


def _sb_fwd(qkv, H):
    T = qkv.shape[0]
    B = _tile(T, ATT_BLOCK)
    nq = T // B
    scale = 1.0 / math.sqrt(HEAD_DIM)

    S = _strip(B)
    HG = ATT_HEADS if H % ATT_HEADS == 0 else 1
    HD = HEAD_DIM

    def body(q_ref, k_ref, v_ref, o_ref, c_ref, z_s, ls_s, parts_s, a_s, sum_s, tri_s):
        i = pl.program_id(1)
        tri = (_iota2((B, B), 0) > _iota2((B, B), 1)).astype(BF16)
        for p in range(3):
            tri_s[pl.ds(p * B, B), :] = tri
        rows = _iota2((S, B), 0)
        cols = _iota2((S, B), 1)

        def step(n, carry):
            j = i - n
            ks = pl.multiple_of(j * B, B)
            out = []
            for g in range(HG):
                acc, run = carry[g]
                q = q_ref[:, pl.ds(g * HD, HD)]
                kb = k_ref[pl.ds(ks, B), pl.ds(g * HD, HD)]
                vb = v_ref[pl.ds(ks, B), pl.ds(g * HD, HD)]
                z_s[g] = _dot(q, kb, NT)
                for r in range(0, B, S):
                    z = z_s[g, pl.ds(r, S), :] * scale
                    mask = (ks + cols) < (i * B + r + rows)
                    ls = _log_sigmoid(z)
                    lk = jnp.where(mask, ls - z, 0.0)
                    hi, mid, lo = _split3(lk)
                    parts_s[g, pl.ds(r, S), pl.ds(0, B)] = hi
                    parts_s[g, pl.ds(r, S), pl.ds(B, B)] = mid
                    parts_s[g, pl.ds(r, S), pl.ds(2 * B, B)] = lo
                    ls_s[g, pl.ds(r, S), :] = jnp.where(mask, ls, NEG)
                    sum_s[g, pl.ds(r, S), :] = jnp.sum(lk, axis=1, keepdims=True)
                z_s[g] = _dot(parts_s[g], tri_s[...], NN)
                for r in range(0, B, S):
                    a_s[g, pl.ds(r, S), :] = jnp.exp(
                        ls_s[g, pl.ds(r, S), :] + z_s[g, pl.ds(r, S), :]).astype(BF16)
                acc = acc + jnp.exp(run) * _dot(a_s[g], vb, NN)
                out.append((acc, run + sum_s[g]))
            return tuple(out)

        init = tuple((jnp.zeros((B, HD), F32), jnp.zeros((B, 1), F32)) for _ in range(HG))
        res = lax.fori_loop(0, i + 1, step, init)
        for g in range(HG):
            o_ref[:, pl.ds(g * HD, HD)] = res[g][0].astype(BF16)
            c_ref[g] = res[g][1]

    return pl.pallas_call(
        body, name="sb_attn_fwd", grid=(H // HG, nq),
        in_specs=[_head_spec(B, HG * HD, 0), _full_head_spec(T, HG * HD, H // HG),
                  _full_head_spec(T, HG * HD, 2 * H // HG)],
        out_specs=(_head_spec(B, HG * HD, 0), pl.BlockSpec((HG, B, 1), lambda h, i: (h, i, 0))),
        out_shape=(jax.ShapeDtypeStruct((T, H * HD), BF16), jax.ShapeDtypeStruct((H, T, 1), F32)),
        scratch_shapes=[pltpu.VMEM((HG, B, B), F32), pltpu.VMEM((HG, B, B), F32),
                        pltpu.VMEM((HG, B, 3 * B), BF16), pltpu.VMEM((HG, B, B), BF16),
                        pltpu.VMEM((HG, B, 1), F32), pltpu.VMEM((3 * B, B), BF16)],
        compiler_params=_params(("parallel", "arbitrary")),
    )(qkv, qkv, qkv)


def _sb_bwd(qkv, do, ctot, H):
    T = qkv.shape[0]
    B = _tile(T, ATT_BLOCK)
    nq = T // B
    scale = 1.0 / math.sqrt(HEAD_DIM)

    S = _strip(B)

    def body(q_ref, k_ref, v_ref, do_ref, c_ref, dq_ref, dk_ref, dv_ref, dk_acc, dv_acc,
             z_s, ls_s, da_s, g_s, parts_s, a_s, dz_s, lsum_s, gsum_s, tri_incl_s, tri_strict_s):
        i = pl.program_id(1)

        @pl.when(i == 0)
        def _():
            dk_acc[...] = jnp.zeros_like(dk_acc)
            dv_acc[...] = jnp.zeros_like(dv_acc)

        q = q_ref[...]
        do_b = do_ref[...]
        tri_incl = (_iota2((B, B), 0) <= _iota2((B, B), 1)).astype(BF16)
        tri_strict = (_iota2((B, B), 0) < _iota2((B, B), 1)).astype(BF16)
        for p in range(3):
            tri_incl_s[pl.ds(p * B, B), :] = tri_incl
            tri_strict_s[pl.ds(p * B, B), :] = tri_strict
        rows = _iota2((S, B), 0)
        cols = _iota2((S, B), 1)

        def put_parts(r, v):
            hi, mid, lo = _split3(v)
            parts_s[pl.ds(r, S), pl.ds(0, B)] = hi
            parts_s[pl.ds(r, S), pl.ds(B, B)] = mid
            parts_s[pl.ds(r, S), pl.ds(2 * B, B)] = lo

        def step(j, carry):
            dq, lpre, gpre = carry
            ks = pl.multiple_of(j * B, B)
            kb = k_ref[pl.ds(ks, B), :]
            vb = v_ref[pl.ds(ks, B), :]
            z_s[...] = _dot(q, kb, NT)
            da_s[...] = _dot(do_b, vb, NT)
            for r in range(0, B, S):
                z = z_s[pl.ds(r, S), :] * scale
                mask = (ks + cols) < (i * B + r + rows)
                ls = _log_sigmoid(z)
                lk = jnp.where(mask, ls - z, 0.0)
                put_parts(r, lk)
                ls_s[pl.ds(r, S), :] = jnp.where(mask, ls, NEG)
                lsum_s[pl.ds(r, S), :] = jnp.sum(lk, axis=1, keepdims=True)
            z_s_incl = _dot(parts_s[...], tri_incl_s[...], NN)
            base = c_ref[...] - lpre
            for r in range(0, B, S):
                later = base[r:r + S] - z_s_incl[r:r + S]
                a = jnp.exp(ls_s[pl.ds(r, S), :] + later)
                g = a * da_s[pl.ds(r, S), :]
                a_s[pl.ds(r, S), :] = a.astype(BF16)
                g_s[pl.ds(r, S), :] = g
                put_parts(r, g)
                gsum_s[pl.ds(r, S), :] = jnp.sum(g, axis=1, keepdims=True)
            gex_blk = _dot(parts_s[...], tri_strict_s[...], NN)
            for r in range(0, B, S):
                ls = ls_s[pl.ds(r, S), :]
                gex = gpre[r:r + S] + gex_blk[r:r + S]
                dz = g_s[pl.ds(r, S), :] * jnp.exp(ls - z_s[pl.ds(r, S), :] * scale) - jnp.exp(ls) * gex
                dz_s[pl.ds(r, S), :] = (dz * scale).astype(BF16)
            dzb = dz_s[...]
            dq = dq + _dot(dzb, kb, NN)
            dk_acc[pl.ds(ks, B), :] += _dot(dzb, q, TN)
            dv_acc[pl.ds(ks, B), :] += _dot(a_s[...], do_b, TN)
            return dq, lpre + lsum_s[...], gpre + gsum_s[...]

        dq, _, _ = lax.fori_loop(
            0, i + 1, step,
            (jnp.zeros((B, HEAD_DIM), F32), jnp.zeros((B, 1), F32), jnp.zeros((B, 1), F32)))
        dq_ref[...] = dq.astype(BF16)

        @pl.when(i == nq - 1)
        def _():
            dk_ref[...] = dk_acc[...].astype(BF16)
            dv_ref[...] = dv_acc[...].astype(BF16)

    W = H * HEAD_DIM
    return pl.pallas_call(
        body, name="sb_attn_bwd", grid=(H, nq),
        in_specs=[_head_spec(B, HEAD_DIM, 0), _full_head_spec(T, HEAD_DIM, H),
                  _full_head_spec(T, HEAD_DIM, 2 * H), _head_spec(B, HEAD_DIM, 0),
                  pl.BlockSpec((None, B, 1), lambda h, i: (h, i, 0))],
        out_specs=(_head_spec(B, HEAD_DIM, 0), _full_head_spec(T, HEAD_DIM, 0),
                   _full_head_spec(T, HEAD_DIM, 0)),
        out_shape=tuple(jax.ShapeDtypeStruct((T, W), BF16) for _ in range(3)),
        scratch_shapes=[pltpu.VMEM((T, HEAD_DIM), F32), pltpu.VMEM((T, HEAD_DIM), F32)]
        + [pltpu.VMEM((B, B), F32)] * 4
        + [pltpu.VMEM((B, 3 * B), BF16), pltpu.VMEM((B, B), BF16), pltpu.VMEM((B, B), BF16),
           pltpu.VMEM((B, 1), F32), pltpu.VMEM((B, 1), F32),
           pltpu.VMEM((3 * B, B), BF16), pltpu.VMEM((3 * B, B), BF16)],
        compiler_params=_params(("parallel", "arbitrary")),
    )(qkv, qkv, qkv, do, ctot)


def _softmax_fwd(q, k, v, cf, H, dqk, scale, *, name):
    T = q.shape[0]
    B = _tile(T, ATT_BLOCK)
    nq = T // B
    has_cf = cf is not None

    def body(*refs):
        q_ref, k_ref, v_ref = refs[:3]
        idx = 3
        if has_cf:
            cfc_ref, cfr_ref = refs[3], refs[4]
            idx = 5
        o_ref, of_ref, lse_ref = refs[idx:idx + 3]
        i = pl.program_id(1)
        qb = q_ref[...]
        rows = _iota2((B, B), 0)
        cols = _iota2((B, B), 1)

        def step(j, carry):
            m, l, acc = carry
            ks = pl.multiple_of(j * B, B)
            kb = k_ref[pl.ds(ks, B), :]
            vb = v_ref[pl.ds(ks, B), :]
            s = _dot(qb, kb, NT) * scale
            if has_cf:
                s = s + (cfc_ref[...] - cfr_ref[:, pl.ds(ks, B)])
            s = jnp.where((ks + cols) <= (i * B + rows), s, NEG)
            m_new = jnp.maximum(m, jnp.max(s, axis=1, keepdims=True))
            alpha = jnp.exp(m - m_new)
            p = jnp.exp(s - m_new)
            l = alpha * l + jnp.sum(p, axis=1, keepdims=True)
            acc = alpha * acc + _dot(p.astype(BF16), vb, NN)
            return m_new, l, acc

        m, l, acc = lax.fori_loop(
            0, i + 1, step,
            (jnp.full((B, 1), NEG, F32), jnp.zeros((B, 1), F32), jnp.zeros((B, HEAD_DIM), F32)))
        o = acc / l
        o_ref[...] = o.astype(BF16)
        of_ref[...] = o
        lse_ref[...] = m + jnp.log(l)

    stat = pl.BlockSpec((None, B, 1), lambda h, i: (h, i, 0))
    in_specs = [_head_spec(B, dqk, 0), _full_head_spec(T, dqk, 0), _full_head_spec(T, HEAD_DIM, 0)]
    args = [q, k, v]
    if has_cf:
        in_specs += [stat, pl.BlockSpec((None, 1, T), lambda h, i: (h, 0, 0))]
        args += list(cf)
    W = H * HEAD_DIM
    return pl.pallas_call(
        body, name=name, grid=(H, nq), in_specs=in_specs,
        out_specs=(_head_spec(B, HEAD_DIM, 0), _head_spec(B, HEAD_DIM, 0), stat),
        out_shape=(jax.ShapeDtypeStruct((T, W), BF16), jax.ShapeDtypeStruct((T, W), F32),
                   jax.ShapeDtypeStruct((H, T, 1), F32)),
        compiler_params=_params(("parallel", "arbitrary")),
    )(*args)


def _softmax_bwd(q, k, v, cf, do, o, lse, H, dqk, scale, *, name):
    T = q.shape[0]
    B = _tile(T, ATT_BLOCK)
    nq = T // B
    has_cf = cf is not None

    def body(*refs):
        q_ref, k_ref, v_ref, do_ref, o_ref, lse_ref = refs[:6]
        idx = 6
        if has_cf:
            cfc_ref, cfr_ref = refs[6], refs[7]
            idx = 8
        dq_ref, dk_ref, dv_ref = refs[idx:idx + 3]
        idx += 3
        if has_cf:
            dcc_ref, dcr_ref = refs[idx], refs[idx + 1]
        i = pl.program_id(1)

        @pl.when(i == 0)
        def _():
            dk_ref[...] = jnp.zeros_like(dk_ref)
            dv_ref[...] = jnp.zeros_like(dv_ref)
            if has_cf:
                dcr_ref[...] = jnp.zeros_like(dcr_ref)

        qb = q_ref[...]
        do_b = do_ref[...]
        lse_b = lse_ref[...]
        delta = jnp.sum(do_b.astype(F32) * o_ref[...], axis=1, keepdims=True)
        rows = _iota2((B, B), 0)
        cols = _iota2((B, B), 1)

        def step(j, carry):
            dq, rs = carry
            ks = pl.multiple_of(j * B, B)
            kb = k_ref[pl.ds(ks, B), :]
            vb = v_ref[pl.ds(ks, B), :]
            s = _dot(qb, kb, NT) * scale
            if has_cf:
                s = s + (cfc_ref[...] - cfr_ref[:, pl.ds(ks, B)])
            p = jnp.where((ks + cols) <= (i * B + rows), jnp.exp(s - lse_b), 0.0)
            dp = _dot(do_b, vb, NT)
            ds = p * (dp - delta)
            dsb = (ds * scale).astype(BF16)
            dq = dq + _dot(dsb, kb, NN)
            dk_ref[pl.ds(ks, B), :] += _dot(dsb, qb, TN)
            dv_ref[pl.ds(ks, B), :] += _dot(p.astype(BF16), do_b, TN)
            if has_cf:
                rs = rs + jnp.sum(ds, axis=1, keepdims=True)
                dcr_ref[:, pl.ds(ks, B)] -= jnp.sum(ds, axis=0, keepdims=True)
            return dq, rs

        dq, rs = lax.fori_loop(0, i + 1, step, (jnp.zeros((B, dqk), F32), jnp.zeros((B, 1), F32)))
        dq_ref[...] = dq
        if has_cf:
            dcc_ref[...] = rs

    stat = pl.BlockSpec((None, B, 1), lambda h, i: (h, i, 0))
    rowstat = pl.BlockSpec((None, 1, T), lambda h, i: (h, 0, 0))
    in_specs = [_head_spec(B, dqk, 0), _full_head_spec(T, dqk, 0), _full_head_spec(T, HEAD_DIM, 0),
                _head_spec(B, HEAD_DIM, 0), _head_spec(B, HEAD_DIM, 0), stat]
    args = [q, k, v, do, o, lse]
    out_specs = [_head_spec(B, dqk, 0), _full_head_spec(T, dqk, 0), _full_head_spec(T, HEAD_DIM, 0)]
    out_shape = [jax.ShapeDtypeStruct((T, H * dqk), F32), jax.ShapeDtypeStruct((T, H * dqk), F32),
                 jax.ShapeDtypeStruct((T, H * HEAD_DIM), F32)]
    if has_cf:
        in_specs += [stat, rowstat]
        args += list(cf)
        out_specs += [stat, rowstat]
        out_shape += [jax.ShapeDtypeStruct((H, T, 1), F32), jax.ShapeDtypeStruct((H, 1, T), F32)]
    return pl.pallas_call(
        body, name=name, grid=(H, nq), in_specs=in_specs, out_specs=tuple(out_specs),
        out_shape=tuple(out_shape), compiler_params=_params(("parallel", "arbitrary")),
    )(*args)


def _headnorm(x, g):
    r = lax.rsqrt(jnp.mean(x * x, axis=-1, keepdims=True) + EPS)
    return x * r * g, r


def _headnorm_bwd(x, g, dy, n):
    r = lax.rsqrt(jnp.sum(x * x, axis=-1, keepdims=True) * (1.0 / n) + EPS)
    t = dy * g
    dx = r * t - x * (r * r * r * (jnp.sum(t * x, axis=-1, keepdims=True) * (1.0 / n)))
    dg = jnp.sum(dy * x * r, axis=0, keepdims=True)
    return dx, dg


def _fox_prep_fwd(qkv, gq, gk, H, *, tm=512):
    T = qkv.shape[0]
    tm = _tile(T, tm)

    def body(q_ref, k_ref, v_ref, gq_ref, gk_ref, qn_ref, kn_ref, vb_ref):
        qn_ref[...] = _headnorm(q_ref[...], gq_ref[...])[0].astype(BF16)
        kn_ref[...] = _headnorm(k_ref[...], gk_ref[...])[0].astype(BF16)
        vb_ref[...] = v_ref[...].astype(BF16)

    def blk(off):
        return pl.BlockSpec((tm, HEAD_DIM), lambda i, h: (i, off + h))

    vec = pl.BlockSpec((1, HEAD_DIM), lambda i, h: (0, 0))
    W = H * HEAD_DIM
    return pl.pallas_call(
        body, name="fox_prep_fwd", grid=(T // tm, H),
        in_specs=[blk(0), blk(H), blk(2 * H), vec, vec], out_specs=(blk(0), blk(0), blk(0)),
        out_shape=tuple(jax.ShapeDtypeStruct((T, W), BF16) for _ in range(3)),
        compiler_params=_params(("parallel", "parallel")),
    )(qkv, qkv, qkv, gq, gk)


def _fox_prep_bwd(qkv, gq, gk, dqn, dkn, dv, H, *, tm=512):
    T = qkv.shape[0]
    tm = _tile(T, tm)

    def body(q_ref, k_ref, gq_ref, gk_ref, dqn_ref, dkn_ref, dv_ref,
             dq_ref, dk_ref, dvb_ref, dgq_ref, dgk_ref):
        @pl.when((pl.program_id(0) == 0) & (pl.program_id(1) == 0))
        def _():
            dgq_ref[...] = jnp.zeros_like(dgq_ref)
            dgk_ref[...] = jnp.zeros_like(dgk_ref)

        dq, dgq = _headnorm_bwd(q_ref[...], gq_ref[...], dqn_ref[...], HEAD_DIM)
        dk, dgk = _headnorm_bwd(k_ref[...], gk_ref[...], dkn_ref[...], HEAD_DIM)
        dq_ref[...] = dq.astype(BF16)
        dk_ref[...] = dk.astype(BF16)
        dvb_ref[...] = dv_ref[...].astype(BF16)
        dgq_ref[...] += dgq
        dgk_ref[...] += dgk

    def blk(off):
        return pl.BlockSpec((tm, HEAD_DIM), lambda i, h: (i, off + h))

    vec = pl.BlockSpec((1, HEAD_DIM), lambda i, h: (0, 0))
    W = H * HEAD_DIM
    return pl.pallas_call(
        body, name="fox_prep_bwd", grid=(T // tm, H),
        in_specs=[blk(0), blk(H), vec, vec, blk(0), blk(0), blk(0)],
        out_specs=(blk(0), blk(0), blk(0), vec, vec),
        out_shape=tuple(jax.ShapeDtypeStruct((T, W), BF16) for _ in range(3))
        + (jax.ShapeDtypeStruct((1, HEAD_DIM), F32), jax.ShapeDtypeStruct((1, HEAD_DIM), F32)),
        compiler_params=_params(("arbitrary", "arbitrary")),
    )(qkv, qkv, gq, gk, dqn, dkn, dv)


def _fox_gate_fwd(flog, bf, *, tm=256):
    T = flog.shape[0]
    tm = _tile(T, tm)

    def body(f_ref, b_ref, cf_ref, carry):
        @pl.when(pl.program_id(0) == 0)
        def _():
            carry[...] = jnp.zeros_like(carry)

        lf = _log_sigmoid(f_ref[...] + b_ref[...])
        tri = (_iota2((tm, tm), 1) <= _iota2((tm, tm), 0)).astype(BF16)
        cf_ref[...] = carry[...] + _tri_left(tri, lf)
        carry[...] += jnp.sum(lf, axis=0, keepdims=True)

    return pl.pallas_call(
        body, name="fox_gate_fwd", grid=(T // tm,),
        in_specs=[pl.BlockSpec((tm, LANE), lambda i: (i, 0)), pl.BlockSpec((1, LANE), lambda i: (0, 0))],
        out_specs=pl.BlockSpec((tm, LANE), lambda i: (i, 0)),
        out_shape=jax.ShapeDtypeStruct((T, LANE), F32),
        scratch_shapes=[pltpu.VMEM((1, LANE), F32)],
        compiler_params=_params(("arbitrary",)),
    )(flog, bf)


def _fox_gate_bwd(flog, bf, dcf, *, tm=256):
    T = flog.shape[0]
    tm = _tile(T, tm)
    nt = T // tm

    def body(f_ref, b_ref, dcf_ref, df_ref, db_ref, carry):
        @pl.when(pl.program_id(0) == 0)
        def _():
            carry[...] = jnp.zeros_like(carry)
            db_ref[...] = jnp.zeros_like(db_ref)

        d = dcf_ref[...]
        tri = (_iota2((tm, tm), 1) >= _iota2((tm, tm), 0)).astype(BF16)
        dlf = carry[...] + _tri_left(tri, d)
        carry[...] += jnp.sum(d, axis=0, keepdims=True)
        xg = f_ref[...] + b_ref[...]
        e = jnp.exp(-jnp.abs(xg))
        sig_neg = jnp.where(xg >= 0.0, e, 1.0) / (1.0 + e)
        df = dlf * sig_neg
        df_ref[...] = df.astype(BF16)
        db_ref[...] += jnp.sum(df, axis=0, keepdims=True)

    rev = pl.BlockSpec((tm, LANE), lambda i: (nt - 1 - i, 0))
    vec = pl.BlockSpec((1, LANE), lambda i: (0, 0))
    return pl.pallas_call(
        body, name="fox_gate_bwd", grid=(nt,), in_specs=[rev, vec, rev], out_specs=(rev, vec),
        out_shape=(jax.ShapeDtypeStruct((T, LANE), BF16), jax.ShapeDtypeStruct((1, LANE), F32)),
        scratch_shapes=[pltpu.VMEM((1, LANE), F32)],
        compiler_params=_params(("arbitrary",)),
    )(flog, bf, dcf)


def _rope_tables(positions):
    half = MLA_ROPE // 2
    inv_freq = ROPE_THETA ** (-jnp.arange(0, half, dtype=F32) * 2.0 / MLA_ROPE)
    ang = positions.astype(F32)[:, None] * inv_freq
    cos, sin = jnp.cos(ang), jnp.sin(ang)
    zero = jnp.zeros_like(cos)
    pad = jnp.zeros((positions.shape[0], LANE - MLA_ROPE), F32)
    cos_t = jnp.concatenate([cos, cos, pad], axis=1)
    sin_up = jnp.concatenate([zero, sin, pad], axis=1)
    sin_dn = jnp.concatenate([-sin, zero, pad], axis=1)
    return cos_t, sin_up, sin_dn


def _rope(x, cos_t, sin_up, sin_dn):
    half = MLA_ROPE // 2
    return x * cos_t + pltpu.roll(x, half, 1) * sin_up + pltpu.roll(x, LANE - half, 1) * sin_dn


def _rope_t(d, cos_t, sin_up, sin_dn):
    half = MLA_ROPE // 2
    return d * cos_t + pltpu.roll(d * sin_up, LANE - half, 1) + pltpu.roll(d * sin_dn, half, 1)


def _norm192(xcat, g):
    r = lax.rsqrt(jnp.sum(xcat * xcat, axis=-1, keepdims=True) * (1.0 / MLA_QK) + EPS)
    return xcat * r * g


def _mla_prep_fwd(qfull, kv, kr, tabs, gq, gk, H, *, tm=512):
    T = qfull.shape[0]
    tm = _tile(T, tm)

    def body(q_ref, kv_ref, kr_ref, c_ref, su_ref, sd_ref, gq_ref, gk_ref, qf_ref, kf_ref, v_ref):
        tabs_b = (c_ref[...], su_ref[...], sd_ref[...])
        qb = q_ref[...]
        qcat = jnp.concatenate([qb[:, :MLA_NOPE], _rope(qb[:, MLA_NOPE:], *tabs_b)], axis=1)
        qf_ref[...] = _norm192(qcat, gq_ref[...]).astype(BF16)
        kvb = kv_ref[...]
        kcat = jnp.concatenate([kvb[:, :MLA_NOPE], _rope(kr_ref[...], *tabs_b)], axis=1)
        kf_ref[...] = _norm192(kcat, gk_ref[...]).astype(BF16)
        v_ref[...] = kvb[:, MLA_NOPE:].astype(BF16)

    head = pl.BlockSpec((tm, MLA_QK_PAD), lambda i, h: (i, h))
    tok = pl.BlockSpec((tm, LANE), lambda i, h: (i, 0))
    vec = pl.BlockSpec((1, MLA_QK_PAD), lambda i, h: (0, 0))
    return pl.pallas_call(
        body, name="mla_prep_fwd", grid=(T // tm, H),
        in_specs=[head, head, tok, tok, tok, tok, vec, vec],
        out_specs=(head, head, pl.BlockSpec((tm, MLA_V), lambda i, h: (i, h))),
        out_shape=(jax.ShapeDtypeStruct((T, H * MLA_QK_PAD), BF16),
                   jax.ShapeDtypeStruct((T, H * MLA_QK_PAD), BF16),
                   jax.ShapeDtypeStruct((T, H * MLA_V), BF16)),
        compiler_params=_params(("parallel", "parallel")),
    )(qfull, kv, kr, *tabs, gq, gk)


def _mla_prep_bwd(qfull, kv, kr, tabs, gq, gk, dqf, dkf, dv, H, *, tm=512):
    T = qfull.shape[0]
    tm = _tile(T, tm)

    def body(q_ref, kv_ref, kr_ref, c_ref, su_ref, sd_ref, gq_ref, gk_ref, dqf_ref, dkf_ref, dv_ref,
             dq_ref, dkv_ref, dkr_ref, dgq_ref, dgk_ref, kr_acc):
        h = pl.program_id(1)

        @pl.when((pl.program_id(0) == 0) & (h == 0))
        def _():
            dgq_ref[...] = jnp.zeros_like(dgq_ref)
            dgk_ref[...] = jnp.zeros_like(dgk_ref)

        @pl.when(h == 0)
        def _():
            kr_acc[...] = jnp.zeros_like(kr_acc)

        tabs_b = (c_ref[...], su_ref[...], sd_ref[...])
        qb = q_ref[...]
        qcat = jnp.concatenate([qb[:, :MLA_NOPE], _rope(qb[:, MLA_NOPE:], *tabs_b)], axis=1)
        dqcat, dgq = _headnorm_bwd(qcat, gq_ref[...], dqf_ref[...], MLA_QK)
        dq_ref[...] = jnp.concatenate(
            [dqcat[:, :MLA_NOPE], _rope_t(dqcat[:, MLA_NOPE:], *tabs_b)], axis=1).astype(BF16)
        dgq_ref[...] += dgq
        kvb = kv_ref[...]
        kcat = jnp.concatenate([kvb[:, :MLA_NOPE], _rope(kr_ref[...], *tabs_b)], axis=1)
        dkcat, dgk = _headnorm_bwd(kcat, gk_ref[...], dkf_ref[...], MLA_QK)
        dkv_ref[...] = jnp.concatenate([dkcat[:, :MLA_NOPE], dv_ref[...]], axis=1).astype(BF16)
        dgk_ref[...] += dgk
        kr_acc[...] += dkcat[:, MLA_NOPE:]

        @pl.when(h == H - 1)
        def _():
            dkr_ref[...] = _rope_t(kr_acc[...], *tabs_b).astype(BF16)

    head = pl.BlockSpec((tm, MLA_QK_PAD), lambda i, h: (i, h))
    tok = pl.BlockSpec((tm, LANE), lambda i, h: (i, 0))
    vec = pl.BlockSpec((1, MLA_QK_PAD), lambda i, h: (0, 0))
    return pl.pallas_call(
        body, name="mla_prep_bwd", grid=(T // tm, H),
        in_specs=[head, head, tok, tok, tok, tok, vec, vec, head, head,
                  pl.BlockSpec((tm, MLA_V), lambda i, h: (i, h))],
        out_specs=(head, head, tok, vec, vec),
        out_shape=(jax.ShapeDtypeStruct((T, H * MLA_QK_PAD), BF16),
                   jax.ShapeDtypeStruct((T, H * MLA_QK_PAD), BF16),
                   jax.ShapeDtypeStruct((T, LANE), BF16),
                   jax.ShapeDtypeStruct((1, MLA_QK_PAD), F32), jax.ShapeDtypeStruct((1, MLA_QK_PAD), F32)),
        scratch_shapes=[pltpu.VMEM((tm, LANE), F32)],
        compiler_params=_params(("arbitrary", "arbitrary")),
    )(qfull, kv, kr, *tabs, gq, gk, dqf, dkf, dv)


def _cf_layouts(cf, H):
    cfh = cf[:, :H].T
    return cfh[:, :, None], cfh[:, None, :]


def _layer_fwd(i, x, P, tabs, H, hook=None):
    kind, j = i % 3, i // 3
    s = {"x_in": x}
    h = _rmsnorm_fwd(x, P["mix_norm"][i:i + 1], name="mix_norm_fwd")
    s["h"] = h
    if kind == 0:
        w = P["sb"][j]
        qkv = _mm(h, w["w_in"], out_dtype=BF16, name="sb_qkv")
        o, ctot = _sb_fwd(qkv, H)
        s.update(qkv=qkv, o=o, ctot=ctot)
    elif kind == 1:
        w = P["fox"][j]
        qkv = _mm(h, w["w_qkv"], name="fox_qkv")
        flog = _mm(h, w["w_f"], name="fox_gate_proj")
        qn, kn, vb = _fox_prep_fwd(qkv, w["gq"], w["gk"], H)
        cf = _cf_layouts(_fox_gate_fwd(flog, w["bf"]), H)
        o, of, lse = _softmax_fwd(qn, kn, vb, cf, H, HEAD_DIM, 1.0 / math.sqrt(HEAD_DIM), name="fox_attn_fwd")
        s.update(qkv=qkv, flog=flog, qn=qn, kn=kn, vb=vb, cf=cf, o=o, of=of, lse=lse)
    else:
        w = P["mla"][j]
        dq = _mm(h, w["w_dq"], name="mla_down_q")
        dkv = _mm(h, w["w_dkv"], name="mla_down_kv")
        kr = _mm(h, w["w_dr"], name="mla_down_rope")
        cq = _rmsnorm_fwd(dq, w["q_norm"], name="mla_q_norm_fwd")
        ckv = _rmsnorm_fwd(dkv, w["kv_norm"], name="mla_kv_norm_fwd")
        qfull = _mm(cq, w["w_uq"], name="mla_up_q")
        kv = _mm(ckv, w["w_ukv"], name="mla_up_kv")
        qf, kf, v = _mla_prep_fwd(qfull, kv, kr, tabs, w["gq"], w["gk"], H)
        o, of, lse = _softmax_fwd(qf, kf, v, None, H, MLA_QK_PAD, 1.0 / math.sqrt(MLA_QK), name="mla_attn_fwd")
        s.update(dq=dq, dkv=dkv, kr=kr, cq=cq, ckv=ckv, qfull=qfull, kv=kv, qf=qf, kf=kf, v=v,
                 o=o, of=of, lse=lse)
    x = _mm(s["o"], w["w_out"], res=x, name="mixer_out")
    s["x_mid"] = x
    deps = hook(x) if hook is not None else ()
    h2 = _rmsnorm_fwd(x, P["mlp_norm"][i:i + 1], name="mlp_norm_fwd")
    z, u = _mm(h2, P["mlp"][i]["w1"], act="relu2", name="mlp_up", deps=deps)
    x = _mm(u, P["mlp"][i]["w2"], res=x, name="mlp_down")
    s.update(h2=h2, z=z, u=u)
    return x, s


def _layer_bwd(i, dx, dxb, s, P, tabs, H, deps=(), hook=None):
    kind, j = i % 3, i // 3
    G = {}
    wm = P["mlp"][i]
    dz = _mm(dxb, wm["w2"], mode="nt", act="drelu2", z=s["z"], out_dtype=BF16, name="mlp_down_bwd",
             deps=deps)
    G["w2"] = _mm(s["u"], dxb, mode="tn", out_dtype=BF16, name="mlp_w2_grad")
    G["w1"] = _mm(s["h2"], dz, mode="tn", out_dtype=BF16, name="mlp_w1_grad")
    dh2 = _mm(dz, wm["w1"], mode="nt", name="mlp_up_bwd")
    dx, dxb, G["mlp_norm"] = _rmsnorm_bwd(s["x_mid"], P["mlp_norm"][i:i + 1], dh2, dx, name="mlp_norm_bwd")
    h = s["h"]
    w = P[("sb", "fox", "mla")[kind]][j]
    do = _mm(dxb, w["w_out"], mode="nt", out_dtype=BF16, name="mixer_out_bwd",
             deps=hook(dxb, G) if hook is not None else ())
    if kind == 0:
        G["w_out"] = _mm(s["o"], dxb, mode="tn", out_dtype=BF16, name="mixer_out_grad")
        dq, dk, dv = _sb_bwd(s["qkv"], do, s["ctot"], H)
        dproj = jnp.concatenate([dq, dk, dv], axis=1)
        G["w_in"] = _mm(h, dproj, mode="tn", out_dtype=BF16, name="sb_qkv_grad")
        dh = _mm(dproj, w["w_in"], mode="nt", name="sb_qkv_bwd")
    elif kind == 1:
        G["w_out"] = _mm(s["o"], dxb, mode="tn", out_dtype=BF16, name="mixer_out_grad")
        dqn, dkn, dv, dcc, dcr = _softmax_bwd(
            s["qn"], s["kn"], s["vb"], s["cf"], do, s["of"], s["lse"], H, HEAD_DIM,
            1.0 / math.sqrt(HEAD_DIM), name="fox_attn_bwd")
        dq, dk, dvb, G["gq"], G["gk"] = _fox_prep_bwd(s["qkv"], w["gq"], w["gk"], dqn, dkn, dv, H)
        dcf = (dcc[:, :, 0] + dcr[:, 0, :]).T
        dcf = jnp.pad(dcf, ((0, 0), (0, LANE - H)))
        dflog, G["bf"] = _fox_gate_bwd(s["flog"], w["bf"], dcf)
        dproj = jnp.concatenate([dq, dk, dvb], axis=1)
        G["w_qkv"] = _mm(h, dproj, mode="tn", out_dtype=BF16, name="fox_qkv_grad")
        G["w_f"] = _mm(h, dflog, mode="tn", out_dtype=BF16, name="fox_gate_grad")
        dh = _mm(dproj, w["w_qkv"], mode="nt", name="fox_qkv_bwd")
        dh = _mm(dflog, w["w_f"], mode="nt", res=dh, name="fox_gate_bwd_proj")
    else:
        G["w_out"] = _mm(s["o"], dxb, mode="tn", out_dtype=BF16, name="mixer_out_grad")
        dqf, dkf, dv = _softmax_bwd(
            s["qf"], s["kf"], s["v"], None, do, s["of"], s["lse"], H, MLA_QK_PAD,
            1.0 / math.sqrt(MLA_QK), name="mla_attn_bwd")
        dqfull, dkv, dkr, G["gq"], G["gk"] = _mla_prep_bwd(
            s["qfull"], s["kv"], s["kr"], tabs, w["gq"], w["gk"], dqf, dkf, dv, H)
        G["w_uq"] = _mm(s["cq"], dqfull, mode="tn", out_dtype=BF16, name="mla_up_q_grad")
        G["w_ukv"] = _mm(s["ckv"], dkv, mode="tn", out_dtype=BF16, name="mla_up_kv_grad")
        dcq = _mm(dqfull, w["w_uq"], mode="nt", name="mla_up_q_bwd")
        dckv = _mm(dkv, w["w_ukv"], mode="nt", name="mla_up_kv_bwd")
        ddq, G["q_norm"] = _rmsnorm_bwd(s["dq"], w["q_norm"], dcq, want_f32=False, name="mla_q_norm_bwd")
        ddkv, G["kv_norm"] = _rmsnorm_bwd(s["dkv"], w["kv_norm"], dckv, want_f32=False, name="mla_kv_norm_bwd")
        G["w_dq"] = _mm(h, ddq, mode="tn", out_dtype=BF16, name="mla_down_q_grad")
        G["w_dkv"] = _mm(h, ddkv, mode="tn", out_dtype=BF16, name="mla_down_kv_grad")
        G["w_dr"] = _mm(h, dkr, mode="tn", out_dtype=BF16, name="mla_down_rope_grad")
        dh = _mm(ddq, w["w_dq"], mode="nt", name="mla_down_q_bwd")
        dh = _mm(ddkv, w["w_dkv"], mode="nt", res=dh, name="mla_down_kv_bwd")
        dh = _mm(dkr, w["w_dr"], mode="nt", res=dh, name="mla_down_rope_bwd")
    dx, dxb, G["mix_norm"] = _rmsnorm_bwd(s["x_in"], P["mix_norm"][i:i + 1], dh, dx, name="mix_norm_bwd")
    return dx, dxb, G


def _local_step(x, positions, loss_target, P, depth, H):
    tabs = _rope_tables(positions)
    saved = []
    for i in range(depth):
        x, s = _layer_fwd(i, x, P, tabs, H)
        saved.append(s)
    dx, dxb, loss = _loss_head(x, loss_target)
    grads = [None] * depth
    for i in reversed(range(depth)):
        dx, dxb, grads[i] = _layer_bwd(i, dx, dxb, saved[i], P, tabs, H)
    return loss, dx, grads


def _pad_cols(a, n):
    return jnp.pad(a, ((0, 0), (0, n - a.shape[1])))


def _prepare_layer(P, full, i, H, parts=("mix", "mlp")):
    W = H * HEAD_DIM
    kind, j = i % 3, i // 3
    if "mlp" in parts:
        P["mlp"][i] = {"w1": full["mlp_w1"][i], "w2": full["mlp_w2"][i]}
    if "mix" not in parts:
        return
    if kind == 0:
        P["sb"][j] = {"w_in": full["sb_w_in"][j], "w_out": full["sb_w_out"][j]}
    elif kind == 1:
        w = full["fox_w_in"][j]
        P["fox"][j] = {
            "w_qkv": w[:, :3 * W], "w_f": _pad_cols(w[:, 3 * W:], LANE),
            "bf": _pad_cols(full["fox_b_f"][j:j + 1], LANE),
            "gq": full["fox_q_gain"][j:j + 1], "gk": full["fox_k_gain"][j:j + 1],
            "w_out": full["fox_w_out"][j]}
    else:
        w = full["mla_w_in"][j]
        q_norm, kv_norm = full["mla_q_norm"][j], full["mla_kv_norm"][j]
        rq, rkv = q_norm.shape[0], kv_norm.shape[0]
        w_uq = full["mla_w_uq"][j].reshape(rq, H, MLA_QK)
        w_uq = jnp.pad(w_uq, ((0, 0), (0, 0), (0, MLA_QK_PAD - MLA_QK))).reshape(rq, H * MLA_QK_PAD)
        P["mla"][j] = {
            "w_dq": w[:, :rq], "w_dkv": w[:, rq:rq + rkv], "w_dr": _pad_cols(w[:, rq + rkv:], LANE),
            "q_norm": q_norm[None], "kv_norm": kv_norm[None],
            "w_uq": w_uq, "w_ukv": full["mla_w_ukv"][j],
            "gq": _pad_cols(full["mla_q_gain"][j:j + 1], MLA_QK_PAD),
            "gk": _pad_cols(full["mla_k_gain"][j:j + 1], MLA_QK_PAD),
            "w_out": full["mla_w_out"][j]}


def _prepare(full, H):
    depth = len(full["mlp_w1"])
    P = {"mix_norm": full["mix_norm"], "mlp_norm": full["mlp_norm"], "mlp": [None] * depth,
         "sb": [None] * len(full["sb_w_in"]), "fox": [None] * len(full["fox_w_in"]),
         "mla": [None] * len(full["mla_w_in"])}
    for i in range(depth):
        _prepare_layer(P, full, i, H)
    return P


def _layer_grads(i, G, H):
    kind = i % 3
    out = {"mix_norm": G["mix_norm"], "mlp_norm": G["mlp_norm"], "mlp_w1": G["w1"], "mlp_w2": G["w2"]}
    if kind == 0:
        out.update(sb_w_in=G["w_in"], sb_w_out=G["w_out"])
    elif kind == 1:
        out.update(fox_w_in=jnp.concatenate([G["w_qkv"], G["w_f"][:, :H]], axis=1), fox_b_f=G["bf"][:, :H],
                   fox_q_gain=G["gq"], fox_k_gain=G["gk"], fox_w_out=G["w_out"])
    else:
        rq = G["w_uq"].shape[0]
        out.update(
            mla_w_in=jnp.concatenate([G["w_dq"], G["w_dkv"], G["w_dr"][:, :MLA_ROPE]], axis=1),
            mla_q_norm=G["q_norm"], mla_kv_norm=G["kv_norm"],
            mla_w_uq=G["w_uq"].reshape(rq, H, MLA_QK_PAD)[:, :, :MLA_QK].reshape(rq, H * MLA_QK),
            mla_w_ukv=G["w_ukv"], mla_q_gain=G["gq"][:, :MLA_QK], mla_k_gain=G["gk"][:, :MLA_QK],
            mla_w_out=G["w_out"])
    return out


def _layer_of(name, j):
    if name.startswith("sb_"):
        return 3 * j
    if name.startswith("fox_"):
        return 3 * j + 1
    if name.startswith("mla_"):
        return 3 * j + 2
    return j


def _unprepare(grads, H):
    out = {}
    for i, G in enumerate(grads):
        for n, g in _layer_grads(i, G, H).items():
            out.setdefault(n, []).append(g)
    return out


class _Place:
    def __init__(self, x, y, c):
        self.x, self.y, self.c = x, y, c
        self.dev = 4 * x + 2 * y + c
        self.chip = 2 * x + y
        self.id = (x, y, c)


def _peers(me, kind):
    if kind == "ici":
        return [_Place(1 - me.x, me.y, me.c), _Place(me.x, 1 - me.y, me.c), _Place(1 - me.x, 1 - me.y, me.c)]
    return [_Place(me.x, me.y, 1 - me.c)]


def _exchange(name, kind, operands, out_shapes, aliases, n_remote, n_local, plan):
    n_in, n_out = len(operands), len(out_shapes)

    def body(*refs):
        in_refs, out_refs = refs[:n_in], refs[n_in:n_in + n_out]
        send_sems, recv_sems, local_sems = refs[n_in + n_out:]
        me = _Place(lax.axis_index("x"), lax.axis_index("y"), lax.axis_index("c"))
        peers = _peers(me, kind)
        remote, local = plan(me, peers, in_refs, out_refs)
        assert len(remote) == n_remote and len(local) == n_local
        sends = []
        for n, (src, dst, k, _) in enumerate(remote):
            cp = pltpu.make_async_remote_copy(
                src_ref=src, dst_ref=dst, send_sem=send_sems.at[n], recv_sem=recv_sems.at[n],
                device_id=peers[k].id, device_id_type=MESH)
            cp.start()
            sends.append(cp)
        own = []
        for n, (src, dst) in enumerate(local):
            cp = pltpu.make_async_copy(src, dst, local_sems.at[n])
            cp.start()
            own.append(cp)
        for n, (src, _, k, landing) in enumerate(remote):
            pltpu.make_async_remote_copy(
                src_ref=src, dst_ref=landing, send_sem=send_sems.at[n], recv_sem=recv_sems.at[n],
                device_id=peers[k].id, device_id_type=MESH).wait_recv()
        for cp in sends:
            cp.wait_send()
        for cp in own:
            cp.wait()

    outs = pl.pallas_call(
        body, name=name, in_specs=[ANY] * n_in, out_specs=tuple([ANY] * n_out),
        out_shape=tuple(out_shapes), input_output_aliases=aliases,
        scratch_shapes=[pltpu.SemaphoreType.DMA((n_remote,)), pltpu.SemaphoreType.DMA((n_remote,)),
                        pltpu.SemaphoreType.DMA((max(n_local, 1),))],
    )(*operands)
    return list(outs)


def _window(ref, kind, d, shard_shape):
    r, c = shard_shape
    if kind == "col":
        return ref.at[:, pl.ds(pl.multiple_of(d * c, c), c)]
    return ref.at[pl.ds(pl.multiple_of(d * r, r), r), :]


def _full_shape(kind, shard_shape):
    r, c = shard_shape
    return (r, N_DEV * c) if kind == "col" else (N_DEV * r, c)


HBM = pl.BlockSpec(memory_space=pltpu.HBM)
SEM = pl.BlockSpec(memory_space=pltpu.SEMAPHORE)
EFFECT = pltpu.SideEffectType.DATAFLOW_SIDE_EFFECTING


class _Flight:
    def __init__(self, name, kind, n_remote, plan, send_sems, recv_sems, bufs, token):
        self.name, self.kind, self.n_remote, self.plan = name, kind, n_remote, plan
        self.send_sems, self.recv_sems, self.bufs, self.token = send_sems, recv_sems, bufs, token


def _start_copies(name, kind, bufs, n_remote, plan, after=None):
    nb = len(bufs)
    n_after = 0 if after is None else 1

    def body(*refs):
        in_refs = refs[:nb]
        send_sems, recv_sems = refs[nb + n_after], refs[nb + n_after + 1]
        token = refs[2 * nb + n_after + 2]
        me = _Place(lax.axis_index("x"), lax.axis_index("y"), lax.axis_index("c"))
        peers = _peers(me, kind)
        for n, (src, dst, k, _) in enumerate(plan(me, peers, in_refs)):
            pltpu.make_async_remote_copy(
                src_ref=src, dst_ref=dst, send_sem=send_sems.at[n], recv_sem=recv_sems.at[n],
                device_id=peers[k].id, device_id_type=MESH).start()
        token[...] = jnp.zeros_like(token)

    outs = pl.pallas_call(
        body, name=name, in_specs=[HBM] * nb + [ANY] * n_after,
        out_specs=(SEM, SEM) + (HBM,) * nb + (pl.BlockSpec(memory_space=pltpu.VMEM),),
        out_shape=(pltpu.SemaphoreType.DMA((n_remote,)), pltpu.SemaphoreType.DMA((n_remote,)))
        + tuple(pltpu.HBM(b.shape, b.dtype) for b in bufs) + (jax.ShapeDtypeStruct((8, LANE), F32),),
        input_output_aliases={a: 2 + a for a in range(nb)},
        compiler_params=pltpu.CompilerParams(has_side_effects=EFFECT),
    )(*[pltpu.with_memory_space_constraint(b, pltpu.HBM) for b in bufs], *([after] if n_after else []))
    return _Flight(name, kind, n_remote, plan, outs[0], outs[1], list(outs[2:2 + nb]), outs[2 + nb])


def _wait_copies(flight, after):
    nb = len(flight.bufs)
    plan, kind = flight.plan, flight.kind

    def body(*refs):
        in_refs = refs[:nb]
        send_sems, recv_sems = refs[nb], refs[nb + 1]
        me = _Place(lax.axis_index("x"), lax.axis_index("y"), lax.axis_index("c"))
        peers = _peers(me, kind)
        for n, (src, _, k, landing) in enumerate(plan(me, peers, in_refs)):
            cp = pltpu.make_async_remote_copy(
                src_ref=src, dst_ref=landing, send_sem=send_sems.at[n], recv_sem=recv_sems.at[n],
                device_id=peers[k].id, device_id_type=MESH)
            cp.wait_send()
            cp.wait_recv()

    outs = pl.pallas_call(
        body, name=flight.name.replace("_start", "_wait"),
        in_specs=[HBM] * nb + [SEM, SEM, ANY], out_specs=(HBM,) * nb,
        out_shape=tuple(pltpu.HBM(b.shape, b.dtype) for b in flight.bufs),
        input_output_aliases={a: a for a in range(nb)},
        compiler_params=pltpu.CompilerParams(has_side_effects=EFFECT),
    )(*flight.bufs, flight.send_sems, flight.recv_sems, after)
    return list(outs)


def _own_in_place(shard, kind, dev):
    r, c = shard.shape
    at = (0, dev * c) if kind == "col" else (dev * r, 0)
    return lax.dynamic_update_slice(lax.empty(_full_shape(kind, shard.shape), shard.dtype), shard, at)


def _gather_ici_start(fulls, kinds, shapes, tag, after=None):
    n = len(fulls)

    def plan(me, peers, refs):
        remote = []
        for a in range(n):
            mine = _window(refs[a], kinds[a], me.dev, shapes[a])
            for k, p in enumerate(peers):
                remote.append((mine, mine, k, _window(refs[a], kinds[a], p.dev, shapes[a])))
        return remote

    return _start_copies("gather_ici_start_" + tag, "ici", list(fulls), 3 * n, plan, after)


def _gather_d2d_start(fulls, kinds, shapes, tag):
    n = len(fulls)

    def plan(me, peers, refs):
        remote = []
        for a in range(n):
            for ch in range(N_CHIP):
                held = _window(refs[a], kinds[a], 2 * ch + me.c, shapes[a])
                remote.append((held, held, 0, _window(refs[a], kinds[a], 2 * ch + 1 - me.c, shapes[a])))
        return remote

    return _start_copies("gather_d2d_start_" + tag, "d2d", fulls, N_CHIP * n, plan)


def _scatter_d2d_start(grads, kinds, shapes, tag):
    n = len(grads)
    lands = [lax.empty((N_CHIP,) + tuple(s), g.dtype) for g, s in zip(grads, shapes)]

    def plan(me, peers, refs):
        remote = []
        for a in range(n):
            for ch in range(N_CHIP):
                remote.append((_window(refs[a], kinds[a], 2 * ch + 1 - me.c, shapes[a]),
                               refs[n + a].at[ch], 0, refs[n + a].at[ch]))
        return remote

    return _start_copies("scatter_d2d_start_" + tag, "d2d", list(grads) + lands, N_CHIP * n, plan)


def _scatter_ici_start(sums, shapes, tag, after=None):
    n = len(sums)
    lands = [lax.empty((N_CHIP - 1,) + tuple(s), BF16) for s in shapes]

    def plan(me, peers, refs):
        remote = []
        for a in range(n):
            for k, p in enumerate(peers):
                remote.append((refs[a].at[p.chip], refs[n + a].at[k], k, refs[n + a].at[k]))
        return remote

    return _start_copies("scatter_ici_start_" + tag, "ici", list(sums) + lands, 3 * n, plan, after)


def _pair_add(g, r4, kind, shard_shape, core, *, name):
    r, c = shard_shape
    tr = _row_tile(r, c, 524288)
    nb = r // tr
    if kind == "all":
        def body_all(core_ref, g_ref, r_ref, o_ref):
            o_ref[...] = g_ref[...] + r_ref[...]

        spec = pl.BlockSpec((tr, c), lambda i, core_ref: (i, 0))
        return pl.pallas_call(
            body_all, name=name,
            grid_spec=pltpu.PrefetchScalarGridSpec(
                num_scalar_prefetch=1, grid=(nb,), in_specs=[spec, spec], out_specs=spec),
            out_shape=jax.ShapeDtypeStruct((r, c), F32),
            compiler_params=_params(("parallel",)),
        )(core, g, r4)

    def body(core_ref, g_ref, r_ref, o_ref):
        o_ref[...] = (g_ref[...].astype(F32) + r_ref[...].astype(F32)).astype(BF16)

    if kind == "col":
        g_spec = pl.BlockSpec((tr, c), lambda ch, i, core_ref: (i, 2 * ch + core_ref[0]))
    else:
        g_spec = pl.BlockSpec((tr, c), lambda ch, i, core_ref: ((2 * ch + core_ref[0]) * nb + i, 0))
    slot = pl.BlockSpec((None, tr, c), lambda ch, i, core_ref: (ch, i, 0))
    return pl.pallas_call(
        body, name=name,
        grid_spec=pltpu.PrefetchScalarGridSpec(
            num_scalar_prefetch=1, grid=(N_CHIP, nb), in_specs=[g_spec, slot], out_specs=slot),
        out_shape=jax.ShapeDtypeStruct((N_CHIP, r, c), BF16),
        compiler_params=_params(("parallel", "parallel")),
    )(core, g, r4)


def _reduce_scatter(grads, kinds, shard_shapes, core, tag):
    n = len(grads)
    land = []
    for g, k, s in zip(grads, kinds, shard_shapes):
        land.append(jax.ShapeDtypeStruct(s if k == "all" else (N_CHIP,) + tuple(s), g.dtype))
    n_rem = sum(1 if k == "all" else N_CHIP for k in kinds)

    def plan_d2d(me, peers, ins, outs):
        remote = []
        for a in range(n):
            if kinds[a] == "all":
                remote.append((ins[a], outs[a], 0, outs[a]))
                continue
            for ch in range(N_CHIP):
                remote.append((_window(ins[a], kinds[a], 2 * ch + 1 - me.c, shard_shapes[a]),
                               outs[a].at[ch], 0, outs[a].at[ch]))
        return remote, []

    got = _exchange("scatter_d2d_" + tag, "d2d", list(grads), land, {}, n_rem, 0, plan_d2d)
    sums = [_pair_add(g, r4, k, s, core, name="pair_add_" + tag)
            for g, r4, k, s in zip(grads, got, kinds, shard_shapes)]
    land2 = [jax.ShapeDtypeStruct((N_CHIP,) + tuple(s), F32 if k == "all" else BF16)
             for k, s in zip(kinds, shard_shapes)]

    def plan_ici(me, peers, ins, outs):
        remote, local = [], []
        for a in range(n):
            for k, p in enumerate(peers):
                src = ins[a] if kinds[a] == "all" else ins[a].at[p.chip]
                remote.append((src, outs[a].at[me.chip], k, outs[a].at[p.chip]))
            src = ins[a] if kinds[a] == "all" else ins[a].at[me.chip]
            local.append((src, outs[a].at[me.chip]))
        return remote, local

    return _exchange("scatter_ici_" + tag, "ici", sums, land2, {}, 3 * n, n, plan_ici)


def _row_tile(r, c, limit=262144):
    best = None
    for t in range(8, r + 1, 8):
        if r % t == 0 and t * c <= limit:
            best = t
    return best if best is not None else r


def _adam_math(w, g, m, v):
    m = ADAM_B1 * m + (1.0 - ADAM_B1) * g
    v = ADAM_B2 * v + (1.0 - ADAM_B2) * (g * g)
    m_hat = m / (1.0 - ADAM_B1 ** ADAM_STEP)
    v_hat = v / (1.0 - ADAM_B2 ** ADAM_STEP)
    delta = -ADAM_LR * (m_hat / (jnp.sqrt(v_hat) + ADAM_EPS) + ADAM_WD * w)
    return delta, m, v


def _adamw(w, m, v, sums, lands, chip, layer, prev, *, name):
    L, r, c = w.shape
    tr = _row_tile(r, c)
    has_prev = prev is not None

    def body(chip_ref, *refs):
        w_ref, m_ref, v_ref, s_ref, l_ref = refs[:5]
        g_ref, d_ref, nm_ref, nv_ref = refs[-4:]
        g = (s_ref[...].astype(F32) + l_ref[0].astype(F32)) + l_ref[1].astype(F32) + l_ref[2].astype(F32)
        delta, nm, nv = _adam_math(w_ref[...], g, m_ref[...], v_ref[...])
        g_ref[...] = g
        d_ref[...] = delta
        nm_ref[...] = nm
        nv_ref[...] = nv

    slab = pl.BlockSpec((None, tr, c), lambda i, chip_ref: (layer, i, 0))
    in_specs = [slab, slab, slab, pl.BlockSpec((None, tr, c), lambda i, chip_ref: (chip_ref[0], i, 0)),
                pl.BlockSpec((N_CHIP - 1, tr, c), lambda i, chip_ref: (0, i, 0))]
    args = [chip, w, m, v, sums, lands]
    aliases = {}
    if has_prev:
        in_specs += [ANY] * 4
        args += list(prev)
        aliases = {6 + n: n for n in range(4)}
    return pl.pallas_call(
        body, name=name,
        grid_spec=pltpu.PrefetchScalarGridSpec(
            num_scalar_prefetch=1, grid=(r // tr,), in_specs=in_specs, out_specs=(slab,) * 4),
        out_shape=tuple(jax.ShapeDtypeStruct((L, r, c), F32) for _ in range(4)),
        input_output_aliases=aliases, compiler_params=_params(("parallel",)),
    )(*args)


def _adamw_small(w, g, m, v):
    def body(w_ref, g_ref, m_ref, v_ref, d_ref, nm_ref, nv_ref):
        d_ref[...], nm_ref[...], nv_ref[...] = _adam_math(w_ref[...], g_ref[...], m_ref[...], v_ref[...])

    return pl.pallas_call(
        body, name="adamw_small", out_shape=tuple(jax.ShapeDtypeStruct(w.shape, F32) for _ in range(3)),
    )(w, g, m, v)


def _sum4(parts):
    def body(p_ref, o_ref):
        o_ref[...] = (p_ref[0] + p_ref[1]) + p_ref[2] + p_ref[3]

    return pl.pallas_call(
        body, name="sum_chips", out_shape=jax.ShapeDtypeStruct(parts.shape[1:], F32))(parts)


_WEIGHTS = ["mix_norm", "mlp_norm", "sb_w_in", "sb_w_out", "fox_w_in", "fox_b_f", "fox_q_gain", "fox_k_gain",
            "fox_w_out", "mla_w_in", "mla_q_norm", "mla_kv_norm", "mla_w_uq", "mla_w_ukv", "mla_q_gain",
            "mla_k_gain", "mla_w_out", "mlp_w1", "mlp_w2"]
_BIG = {"sb_w_in": "col", "sb_w_out": "row", "fox_w_in": "row", "fox_w_out": "row", "mla_w_in": "row",
        "mla_w_uq": "col", "mla_w_ukv": "col", "mla_w_out": "row", "mlp_w1": "col", "mlp_w2": "row"}


def _layer_big(i):
    kind, j = i % 3, i // 3
    mixer = {0: ["sb_w_in", "sb_w_out"], 1: ["fox_w_in", "fox_w_out"],
             2: ["mla_w_in", "mla_w_uq", "mla_w_ukv", "mla_w_out"]}[kind]
    return [(n, j) for n in mixer] + [("mlp_w1", i), ("mlp_w2", i)]


def _stack_to_cols(a):
    r = a.shape[0] // N_DEV
    return a.reshape(N_DEV, r, a.shape[1]).transpose(1, 0, 2).reshape(r, N_DEV * a.shape[1])


def _cols_to_stack(a):
    c = a.shape[1] // N_DEV
    return a.reshape(a.shape[0], N_DEV, c).transpose(1, 0, 2).reshape(N_DEV * a.shape[0], c)


def _pack_rows(rows, width):
    rows = [jnp.pad(r.reshape(-1).astype(F32), (0, width - r.size)) for r in rows]
    pad = (-len(rows)) % 8
    rows += [jnp.zeros((width,), F32)] * pad
    return jnp.stack(rows)


def kernel(x, positions, mix_norm, mlp_norm, sb_w_in, sb_w_out, fox_w_in, fox_b_f, fox_q_gain, fox_k_gain, fox_w_out, mla_w_in, mla_q_norm, mla_kv_norm, mla_w_uq, mla_w_ukv, mla_q_gain, mla_k_gain, mla_w_out, mlp_w1, mlp_w2, loss_target, m_mix_norm, m_mlp_norm, m_sb_w_in, m_sb_w_out, m_fox_w_in, m_fox_b_f, m_fox_q_gain, m_fox_k_gain, m_fox_w_out, m_mla_w_in, m_mla_q_norm, m_mla_kv_norm, m_mla_w_uq, m_mla_w_ukv, m_mla_q_gain, m_mla_k_gain, m_mla_w_out, m_mlp_w1, m_mlp_w2, v_mix_norm, v_mlp_norm, v_sb_w_in, v_sb_w_out, v_fox_w_in, v_fox_b_f, v_fox_q_gain, v_fox_k_gain, v_fox_w_out, v_mla_w_in, v_mla_q_norm, v_mla_kv_norm, v_mla_w_uq, v_mla_w_ukv, v_mla_q_gain, v_mla_k_gain, v_mla_w_out, v_mlp_w1, v_mlp_w2):
    w_in = dict(zip(_WEIGHTS, (mix_norm, mlp_norm, sb_w_in, sb_w_out, fox_w_in, fox_b_f, fox_q_gain, fox_k_gain, fox_w_out, mla_w_in, mla_q_norm, mla_kv_norm, mla_w_uq, mla_w_ukv, mla_q_gain, mla_k_gain, mla_w_out, mlp_w1, mlp_w2)))
    m_in = dict(zip(_WEIGHTS, (m_mix_norm, m_mlp_norm, m_sb_w_in, m_sb_w_out, m_fox_w_in, m_fox_b_f, m_fox_q_gain, m_fox_k_gain, m_fox_w_out, m_mla_w_in, m_mla_q_norm, m_mla_kv_norm, m_mla_w_uq, m_mla_w_ukv, m_mla_q_gain, m_mla_k_gain, m_mla_w_out, m_mlp_w1, m_mlp_w2)))
    v_in = dict(zip(_WEIGHTS, (v_mix_norm, v_mlp_norm, v_sb_w_in, v_sb_w_out, v_fox_w_in, v_fox_b_f, v_fox_q_gain, v_fox_k_gain, v_fox_w_out, v_mla_w_in, v_mla_q_norm, v_mla_kv_norm, v_mla_w_uq, v_mla_w_ukv, v_mla_q_gain, v_mla_k_gain, v_mla_w_out, v_mlp_w1, v_mlp_w2)))
    depth, D = mix_norm.shape
    H = D // HEAD_DIM
    n_mla = mla_w_in.shape[0]
    dev = 4 * lax.axis_index("x") + 2 * lax.axis_index("y") + lax.axis_index("c")
    core = lax.axis_index("c").astype(jnp.int32).reshape(1)

    chip = (2 * lax.axis_index("x") + lax.axis_index("y")).astype(jnp.int32).reshape(1)
    nq, nkv = mla_q_norm.shape[1], mla_kv_norm.shape[1]

    units = []
    for i in range(depth):
        big = _layer_big(i)
        units += [(i, ("mix",), big[:-2]), (i, ("mlp",), big[-2:])] if i == 0 else [(i, ("mix", "mlp"), big)]

    def unit_of(i):
        return i + 1 if i else 0

    def tag_of(u):
        i, parts, _ = units[u]
        return f"l{i}" if len(parts) == 2 else f"l{i}_{parts[0]}"

    kinds_of, shapes_of, ici = [], [], []
    for i, parts, names in units:
        shards = [w_in[n][j].astype(BF16) for n, j in names]
        kinds = [_BIG[n] for n, _ in names]
        if i % 3 == 2:
            shards.append(_pack_rows([mla_q_norm[i // 3], mla_kv_norm[i // 3]], LANE))
            kinds.append("row")
        kinds_of.append(kinds)
        shapes_of.append([s.shape for s in shards])
        fulls = [_own_in_place(s, k, dev) for s, k in zip(shards, kinds)]
        ici.append(_gather_ici_start(fulls, kinds, shapes_of[-1], tag_of(len(ici)),
                                     ici[-1].token if ici else None))

    full = {n: [None] * w_in[n].shape[0] for n in _WEIGHTS}
    for n in ("mix_norm", "mlp_norm", "fox_b_f", "fox_q_gain", "fox_k_gain", "mla_q_gain", "mla_k_gain"):
        full[n] = w_in[n]
    P = {"mix_norm": mix_norm, "mlp_norm": mlp_norm, "mlp": [None] * depth,
         "sb": [None] * sb_w_in.shape[0], "fox": [None] * fox_w_in.shape[0], "mla": [None] * n_mla}

    def forward_to_sibling(u, after):
        arrived = _wait_copies(ici[u], after)
        return _gather_d2d_start(arrived, kinds_of[u], shapes_of[u], tag_of(u))

    def finish_gather(u, flight, after):
        i, parts, names = units[u]
        got = _wait_copies(flight, after)
        for (n, j), a in zip(names, got):
            full[n][j] = _stack_to_cols(a) if n == "fox_w_in" else a
        if i % 3 == 2:
            tiles = got[-1].reshape(N_DEV, 8, LANE)
            full["mla_q_norm"][i // 3] = tiles[:, 0, :nq].reshape(-1)
            full["mla_kv_norm"][i // 3] = tiles[:, 1, :nkv].reshape(-1)
        _prepare_layer(P, full, i, H, parts)

    tabs = _rope_tables(positions[0])
    d2d = forward_to_sibling(0, ici[-1].token)
    finish_gather(0, d2d, d2d.token)
    xs, saved = x[0], []
    for i in range(depth):
        nxt = {}

        def hook(x_mid, i=i, nxt=nxt):
            if i == 0:
                mlp_d2d = forward_to_sibling(1, x_mid)
                finish_gather(1, mlp_d2d, mlp_d2d.token)
                return ()
            if i + 1 == depth:
                return ()
            nxt["d2d"] = forward_to_sibling(unit_of(i + 1), x_mid)
            return (nxt["d2d"].token,)

        xs, s = _layer_fwd(i, xs, P, tabs, H, hook)
        saved.append(s)
        if i == 0 and depth > 1:
            nxt["d2d"] = forward_to_sibling(unit_of(1), xs)
            finish_gather(unit_of(1), nxt["d2d"], nxt["d2d"].token)
        elif i + 1 < depth:
            finish_gather(unit_of(i + 1), nxt["d2d"], xs)
    dx, dxb, loss = _loss_head(xs, loss_target[0])
    loss = lax.psum(loss[0, 0], ("x", "y", "c"))

    def unit_meta(u):
        names = units[u][2]
        return [_BIG[n] for n, _ in names], [w_in[n].shape[1:] for n, _ in names]

    def to_sibling_start(u, Gl):
        grads = [(_cols_to_stack(Gl[n]) if n == "fox_w_in" else Gl[n]) for n, _ in units[u][2]]
        return _scatter_d2d_start(grads, *unit_meta(u), tag_of(u))

    def pair_and_send(u, flight, after, before=None):
        n = len(units[u][2])
        kinds, shapes = unit_meta(u)
        got = _wait_copies(flight, after)
        sums = [_pair_add(g, r4, k, s, core, name="pair_add_" + tag_of(u))
                for g, r4, k, s in zip(got[:n], got[n:], kinds, shapes)]
        return _scatter_ici_start(sums, shapes, tag_of(u), before)

    G, to_chips, deps, flying = [None] * depth, [None] * len(units), (), {}
    for i in reversed(range(depth)):
        def hook(dxb_mid, Gp, i=i):
            out = ()
            if i + 1 < depth:
                u = unit_of(i + 1)
                to_chips[u] = pair_and_send(u, flying.pop(u), dxb_mid)
                out = (to_chips[u].token,)
            if i == 0:
                flying[1] = to_sibling_start(1, {"mlp_w1": Gp["w1"], "mlp_w2": Gp["w2"]})
                out = out + (flying[1].token,)
            return out

        dx, dxb, Gp = _layer_bwd(i, dx, dxb, saved[i], P, tabs, H, deps, hook)
        G[i] = _layer_grads(i, Gp, H)
        flying[unit_of(i)] = to_sibling_start(unit_of(i), G[i])
        deps = (flying[unit_of(i)].token,)
    grad_x = dx

    small = [n for n in _WEIGHTS if n not in _BIG]
    rows = []
    for n in small:
        for j in range(w_in[n].shape[0]):
            rows.append(G[_layer_of(n, j)][n])
    sg = _pack_rows(rows, D)
    (parts,) = _reduce_scatter([sg], ["all"], [sg.shape], core, "small")
    sgr = _sum4(parts)
    to_chips[1] = pair_and_send(1, flying.pop(1), dx, sgr)
    to_chips[0] = pair_and_send(0, flying.pop(0), to_chips[1].token)

    results = {}
    for u in reversed(range(len(units))):
        names = units[u][2]
        got = _wait_copies(to_chips[u], to_chips[0].token)
        for a, (n, j) in enumerate(names):
            results[n] = _adamw(w_in[n], m_in[n], v_in[n], got[a], got[len(names) + a], chip, j, results.get(n),
                                name="adamw_" + n)

    g_rows, w_rows, m_rows, v_rows, at = [], [], [], [], 0
    for n in small:
        for j in range(w_in[n].shape[0]):
            width = w_in[n].shape[1]
            if n in ("mla_q_norm", "mla_kv_norm"):
                g_rows.append(lax.dynamic_slice(sgr[at], (dev * width,), (width,)))
            else:
                g_rows.append(sgr[at, :width])
            w_rows.append(w_in[n][j])
            m_rows.append(m_in[n][j])
            v_rows.append(v_in[n][j])
            at += 1
    g_pack = _pack_rows(g_rows, D)
    d_pack, nm_pack, nv_pack = _adamw_small(
        _pack_rows(w_rows, D), g_pack, _pack_rows(m_rows, D), _pack_rows(v_rows, D))
    at = 0
    for n in small:
        L, width = w_in[n].shape
        results[n] = tuple(p[at:at + L, :width] for p in (g_pack, d_pack, nm_pack, nv_pack))
        at += L

    out = [loss, grad_x[None]]
    for part in range(4):
        out += [results[n][part] for n in _WEIGHTS]
    return tuple(out)
```

```python
import functools
import math

import jax
import jax.numpy as jnp
from jax import lax
from jax.experimental import pallas as pl
from jax.experimental.pallas import tpu as pltpu

F32 = jnp.float32
BF16 = jnp.bfloat16

HEAD_DIM = 128
MLA_NOPE = 128
MLA_ROPE = 64
MLA_V = 128
MLA_QK = MLA_NOPE + MLA_ROPE
MLA_QK_PAD = 256
LANE = 128
ROPE_THETA = 10000.0
EPS = 1e-6
ADAM_LR = 0.001
ADAM_B1 = 0.9
ADAM_B2 = 0.999
ADAM_EPS = 1e-08
ADAM_WD = 0.01
ADAM_STEP = 10
N_DEV = 8
N_CHIP = 4
NEG = -1e30
VMEM_LIMIT = 56 * 1024 * 1024
ATT_BLOCK = 256
ATT_STRIP = 32
ATT_HEADS = 2
SB_PARTS = 2
MESH = pl.DeviceIdType.MESH
ANY = pl.BlockSpec(memory_space=pl.ANY)


def _params(sem):
    return pltpu.CompilerParams(dimension_semantics=sem, vmem_limit_bytes=VMEM_LIMIT)


def _tile(dim, pref):
    if dim <= pref:
        return dim
    t = pref
    while dim % t:
        t -= LANE
    return t


def _dot(a, b, dims):
    return lax.dot_general(a, b, (dims, ((), ())), preferred_element_type=F32)


NN = ((1,), (0,))
NT = ((1,), (1,))
TN = ((0,), (0,))


def _mm(a, b, *, mode="nn", out_dtype=F32, res=None, act=None, z=None, name,
        tm=1024, tn=1024, tk=2048, deps=()):
    if mode == "nn":
        (M, K), (_, N) = a.shape, b.shape
    elif mode == "nt":
        (M, K), (N, _) = a.shape, b.shape
    else:
        (K, M), (_, N) = a.shape, b.shape
    tm, tn, tk = _tile(M, tm), _tile(N, tn), _tile(K, tk)
    nk = K // tk
    if mode == "tn":
        a_spec = pl.BlockSpec((tk, tm), lambda i, j, k: (k, i))
    else:
        a_spec = pl.BlockSpec((tm, tk), lambda i, j, k: (i, k))
    if mode == "nt":
        b_spec = pl.BlockSpec((tn, tk), lambda i, j, k: (j, k))
    else:
        b_spec = pl.BlockSpec((tk, tn), lambda i, j, k: (k, j))
    dims = {"nn": NN, "nt": NT, "tn": TN}[mode]
    o_spec = pl.BlockSpec((tm, tn), lambda i, j, k: (i, j))
    in_specs, args = [a_spec, b_spec], [a, b]
    if res is not None:
        in_specs.append(o_spec)
        args.append(res)
    if act == "drelu2":
        in_specs.append(o_spec)
        args.append(z)
    if act == "relu2":
        out_shape = (jax.ShapeDtypeStruct((M, N), F32), jax.ShapeDtypeStruct((M, N), BF16))
        out_specs = (o_spec, o_spec)
    else:
        out_shape = jax.ShapeDtypeStruct((M, N), out_dtype)
        out_specs = o_spec
    has_res, has_z = res is not None, act == "drelu2"
    n_out = 2 if act == "relu2" else 1
    in_specs += [ANY] * len(deps)
    args += list(deps)

    def body(*refs):
        a_ref, b_ref = refs[0], refs[1]
        idx = 2
        res_ref = z_ref = None
        if has_res:
            res_ref = refs[idx]
            idx += 1
        if has_z:
            z_ref = refs[idx]
            idx += 1
        idx += len(deps)
        outs = refs[idx:idx + n_out]

        def finish(r):
            if has_res:
                r = r + res_ref[...]
            if act == "relu2":
                outs[0][...] = r
                rr = jnp.maximum(r, 0.0)
                outs[1][...] = (rr * rr).astype(BF16)
            elif act == "drelu2":
                outs[0][...] = (r * (2.0 * jnp.maximum(z_ref[...], 0.0))).astype(out_dtype)
            else:
                outs[0][...] = r.astype(out_dtype)

        if nk == 1:
            finish(_dot(a_ref[...], b_ref[...], dims))
            return
        acc = refs[-1]
        k = pl.program_id(2)

        @pl.when(k == 0)
        def _():
            acc[...] = _dot(a_ref[...], b_ref[...], dims)

        @pl.when((k > 0) & (k < nk - 1))
        def _():
            acc[...] += _dot(a_ref[...], b_ref[...], dims)

        @pl.when(k == nk - 1)
        def _():
            finish(acc[...] + _dot(a_ref[...], b_ref[...], dims))

    return pl.pallas_call(
        body, name=name, grid=(M // tm, N // tn, nk), in_specs=in_specs, out_specs=out_specs,
        out_shape=out_shape, scratch_shapes=[pltpu.VMEM((tm, tn), F32)] if nk > 1 else [],
        compiler_params=_params(("parallel", "parallel", "arbitrary")),
    )(*args)


def _rmsnorm_fwd(x, g, *, name, tm=256):
    T, n = x.shape
    tm = _tile(T, tm)

    def body(x_ref, g_ref, o_ref):
        xf = x_ref[...]
        r = lax.rsqrt(jnp.mean(xf * xf, axis=-1, keepdims=True) + EPS)
        o_ref[...] = (xf * r * g_ref[...]).astype(BF16)

    return pl.pallas_call(
        body, name=name, grid=(T // tm,),
        in_specs=[pl.BlockSpec((tm, n), lambda i: (i, 0)), pl.BlockSpec((1, n), lambda i: (0, 0))],
        out_specs=pl.BlockSpec((tm, n), lambda i: (i, 0)),
        out_shape=jax.ShapeDtypeStruct((T, n), BF16),
        compiler_params=_params(("parallel",)),
    )(x, g)


def _rmsnorm_bwd(x, g, dy, dx_in=None, *, name, want_f32=True, tm=256):
    T, n = x.shape
    tm = _tile(T, tm)
    has_in = dx_in is not None
    row = pl.BlockSpec((tm, n), lambda i: (i, 0))
    vec = pl.BlockSpec((1, n), lambda i: (0, 0))

    def body(*refs):
        x_ref, g_ref, dy_ref = refs[:3]
        idx = 3
        in_ref = None
        if has_in:
            in_ref = refs[idx]
            idx += 1
        outs = refs[idx:]
        xf = x_ref[...]
        r = lax.rsqrt(jnp.mean(xf * xf, axis=-1, keepdims=True) + EPS)
        dyf = dy_ref[...]
        t = dyf * g_ref[...]
        dx = r * t - xf * (r * r * r * jnp.mean(t * xf, axis=-1, keepdims=True))
        if has_in:
            dx = dx + in_ref[...]
        o = 0
        if want_f32:
            outs[0][...] = dx
            o = 1
        outs[o][...] = dx.astype(BF16)
        dg_ref = outs[o + 1]

        @pl.when(pl.program_id(0) == 0)
        def _():
            dg_ref[...] = jnp.zeros_like(dg_ref)

        dg_ref[...] += jnp.sum(dyf * xf * r, axis=0, keepdims=True)

    in_specs = [row, vec, row] + ([row] if has_in else [])
    out_specs, out_shape = [], []
    if want_f32:
        out_specs.append(row)
        out_shape.append(jax.ShapeDtypeStruct((T, n), F32))
    out_specs += [row, vec]
    out_shape += [jax.ShapeDtypeStruct((T, n), BF16), jax.ShapeDtypeStruct((1, n), F32)]
    args = [x, g, dy] + ([dx_in] if has_in else [])
    return pl.pallas_call(
        body, name=name, grid=(T // tm,), in_specs=in_specs, out_specs=tuple(out_specs),
        out_shape=tuple(out_shape), compiler_params=_params(("arbitrary",)),
    )(*args)


def _loss_head(y, target, *, tm=256):
    T, n = y.shape
    tm = _tile(T, tm)
    row = pl.BlockSpec((tm, n), lambda i: (i, 0))

    def body(y_ref, t_ref, dy_ref, dyb_ref, loss_ref):
        err = y_ref[...] - t_ref[...]
        dy = err * (1.0 / n)
        dy_ref[...] = dy
        dyb_ref[...] = dy.astype(BF16)

        @pl.when(pl.program_id(0) == 0)
        def _():
            loss_ref[...] = jnp.zeros_like(loss_ref)

        part = 0.5 * jnp.sum(jnp.mean(err * err, axis=-1, keepdims=True), axis=0, keepdims=True)
        loss_ref[...] += part

    return pl.pallas_call(
        body, name="loss_head", grid=(T // tm,), in_specs=[row, row],
        out_specs=(row, row, pl.BlockSpec((1, 1), lambda i: (0, 0))),
        out_shape=(jax.ShapeDtypeStruct((T, n), F32), jax.ShapeDtypeStruct((T, n), BF16),
                   jax.ShapeDtypeStruct((1, 1), F32)),
        compiler_params=_params(("arbitrary",)),
    )(y, target)


def _split(v, n):
    parts = []
    for _ in range(n - 1):
        p = v.astype(BF16)
        parts.append(p)
        v = v - p.astype(F32)
    parts.append(v.astype(BF16))
    return parts


def _tri_left(tri, v):
    hi, mid, lo = _split(v, 3)
    return _dot(tri, hi, NN) + _dot(tri, mid, NN) + _dot(tri, lo, NN)


def _iota2(shape, dim):
    return lax.broadcasted_iota(jnp.int32, shape, dim)


def _log_sigmoid(z):
    return jnp.minimum(z, 0.0) - jnp.log1p(jnp.exp(-jnp.abs(z)))


def _log_sigmoid_abs(z):
    return jnp.minimum(z, 0.0) - jnp.log(1.0 + jnp.exp(-jnp.abs(z)))


def _strip(B):
    return min(ATT_STRIP, B)


def _head_spec(rows, width, off):
    return pl.BlockSpec((rows, width), lambda h, i: (i, off + h))


def _full_head_spec(T, width, off):
    return pl.BlockSpec((T, width), lambda h, i: (0, off + h))


def _sb_fwd(qkv, H):
    T = qkv.shape[0]
    B = _tile(T, ATT_BLOCK)
    nq = T // B
    scale = 1.0 / math.sqrt(HEAD_DIM)

    S = _strip(B)
    HG = ATT_HEADS if H % ATT_HEADS == 0 else 1
    HD = HEAD_DIM

    def body(q_ref, k_ref, v_ref, o_ref, c_ref, z_s, ls_s, parts_s, a_s, sum_s, tri_s):
        i = pl.program_id(1)
        tri = (_iota2((B, B), 0) > _iota2((B, B), 1)).astype(BF16)
        for p in range(SB_PARTS):
            tri_s[pl.ds(p * B, B), :] = tri
        rows = _iota2((S, B), 0)
        cols = _iota2((S, B), 1)

        def step(n, carry):
            j = i - n
            ks = pl.multiple_of(j * B, B)
            out = []
            for g in range(HG):
                acc, run = carry[g]
                q = q_ref[:, pl.ds(g * HD, HD)]
                kb = k_ref[pl.ds(ks, B), pl.ds(g * HD, HD)]
                vb = v_ref[pl.ds(ks, B), pl.ds(g * HD, HD)]
                z_s[g] = _dot(q, kb, NT)
                for r in range(0, B, S):
                    z = z_s[g, pl.ds(r, S), :] * scale
                    mask = (ks + cols) < (i * B + r + rows)
                    ls = _log_sigmoid_abs(z)
                    lk = jnp.where(mask, ls - z, 0.0)
                    for p, part in enumerate(_split(lk, SB_PARTS)):
                        parts_s[g, pl.ds(r, S), pl.ds(p * B, B)] = part
                    ls_s[g, pl.ds(r, S), :] = jnp.where(mask, ls, NEG)
                    sum_s[g, pl.ds(r, S), :] = jnp.sum(lk, axis=1, keepdims=True)
                z_s[g] = _dot(parts_s[g], tri_s[...], NN)
                for r in range(0, B, S):
                    a_s[g, pl.ds(r, S), :] = jnp.exp(
                        ls_s[g, pl.ds(r, S), :] + z_s[g, pl.ds(r, S), :]).astype(BF16)
                acc = acc + jnp.exp(run) * _dot(a_s[g], vb, NN)
                out.append((acc, run + sum_s[g]))
            return tuple(out)

        init = tuple((jnp.zeros((B, HD), F32), jnp.zeros((B, 1), F32)) for _ in range(HG))
        res = lax.fori_loop(0, i + 1, step, init)
        for g in range(HG):
            o_ref[:, pl.ds(g * HD, HD)] = res[g][0].astype(BF16)
            c_ref[g] = res[g][1]

    return pl.pallas_call(
        body, name="sb_attn_fwd", grid=(H // HG, nq),
        in_specs=[_head_spec(B, HG * HD, 0), _full_head_spec(T, HG * HD, H // HG),
                  _full_head_spec(T, HG * HD, 2 * H // HG)],
        out_specs=(_head_spec(B, HG * HD, 0), pl.BlockSpec((HG, B, 1), lambda h, i: (h, i, 0))),
        out_shape=(jax.ShapeDtypeStruct((T, H * HD), BF16), jax.ShapeDtypeStruct((H, T, 1), F32)),
        scratch_shapes=[pltpu.VMEM((HG, B, B), F32), pltpu.VMEM((HG, B, B), F32),
                        pltpu.VMEM((HG, B, SB_PARTS * B), BF16), pltpu.VMEM((HG, B, B), BF16),
                        pltpu.VMEM((HG, B, 1), F32), pltpu.VMEM((SB_PARTS * B, B), BF16)],
        compiler_params=_params(("parallel", "arbitrary")),
    )(qkv, qkv, qkv)


def _sb_bwd(qkv, do, ctot, H):
    T = qkv.shape[0]
    B = _tile(T, ATT_BLOCK)
    nq = T // B
    scale = 1.0 / math.sqrt(HEAD_DIM)

    S = _strip(B)

    def body(q_ref, k_ref, v_ref, do_ref, c_ref, dq_ref, dk_ref, dv_ref, dk_acc, dv_acc,
             z_s, ls_s, da_s, g_s, parts_s, a_s, dz_s, lsum_s, gsum_s, tri_incl_s, tri_strict_s):
        i = pl.program_id(1)

        @pl.when(i == 0)
        def _():
            dk_acc[...] = jnp.zeros_like(dk_acc)
            dv_acc[...] = jnp.zeros_like(dv_acc)

        q = q_ref[...]
        do_b = do_ref[...]
        tri_incl = (_iota2((B, B), 0) <= _iota2((B, B), 1)).astype(BF16)
        tri_strict = (_iota2((B, B), 0) < _iota2((B, B), 1)).astype(BF16)
        for p in range(SB_PARTS):
            tri_incl_s[pl.ds(p * B, B), :] = tri_incl
            tri_strict_s[pl.ds(p * B, B), :] = tri_strict
        rows = _iota2((S, B), 0)
        cols = _iota2((S, B), 1)

        def put_parts(r, v):
            for p, part in enumerate(_split(v, SB_PARTS)):
                parts_s[pl.ds(r, S), pl.ds(p * B, B)] = part

        def step(j, carry):
            dq, lpre, gpre = carry
            ks = pl.multiple_of(j * B, B)
            kb = k_ref[pl.ds(ks, B), :]
            vb = v_ref[pl.ds(ks, B), :]
            z_s[...] = _dot(q, kb, NT)
            da_s[...] = _dot(do_b, vb, NT)
            for r in range(0, B, S):
                z = z_s[pl.ds(r, S), :] * scale
                mask = (ks + cols) < (i * B + r + rows)
                ls = _log_sigmoid_abs(z)
                lk = jnp.where(mask, ls - z, 0.0)
                put_parts(r, lk)
                ls_s[pl.ds(r, S), :] = jnp.where(mask, ls, NEG)
                lsum_s[pl.ds(r, S), :] = jnp.sum(lk, axis=1, keepdims=True)
            z_s_incl = _dot(parts_s[...], tri_incl_s[...], NN)
            base = c_ref[...] - lpre
            for r in range(0, B, S):
                later = base[r:r + S] - z_s_incl[r:r + S]
                a = jnp.exp(ls_s[pl.ds(r, S), :] + later)
                g = a * da_s[pl.ds(r, S), :]
                a_s[pl.ds(r, S), :] = a.astype(BF16)
                g_s[pl.ds(r, S), :] = g
                put_parts(r, g)
                gsum_s[pl.ds(r, S), :] = jnp.sum(g, axis=1, keepdims=True)
            gex_blk = _dot(parts_s[...], tri_strict_s[...], NN)
            for r in range(0, B, S):
                ls = ls_s[pl.ds(r, S), :]
                gex = gpre[r:r + S] + gex_blk[r:r + S]
                dz = g_s[pl.ds(r, S), :] * jnp.exp(ls - z_s[pl.ds(r, S), :] * scale) - jnp.exp(ls) * gex
                dz_s[pl.ds(r, S), :] = (dz * scale).astype(BF16)
            dzb = dz_s[...]
            dq = dq + _dot(dzb, kb, NN)
            dk_acc[pl.ds(ks, B), :] += _dot(dzb, q, TN)
            dv_acc[pl.ds(ks, B), :] += _dot(a_s[...], do_b, TN)
            return dq, lpre + lsum_s[...], gpre + gsum_s[...]

        dq, _, _ = lax.fori_loop(
            0, i + 1, step,
            (jnp.zeros((B, HEAD_DIM), F32), jnp.zeros((B, 1), F32), jnp.zeros((B, 1), F32)))
        dq_ref[...] = dq.astype(BF16)

        @pl.when(i == nq - 1)
        def _():
            dk_ref[...] = dk_acc[...].astype(BF16)
            dv_ref[...] = dv_acc[...].astype(BF16)

    W = H * HEAD_DIM
    return pl.pallas_call(
        body, name="sb_attn_bwd", grid=(H, nq),
        in_specs=[_head_spec(B, HEAD_DIM, 0), _full_head_spec(T, HEAD_DIM, H),
                  _full_head_spec(T, HEAD_DIM, 2 * H), _head_spec(B, HEAD_DIM, 0),
                  pl.BlockSpec((None, B, 1), lambda h, i: (h, i, 0))],
        out_specs=(_head_spec(B, HEAD_DIM, 0), _full_head_spec(T, HEAD_DIM, 0),
                   _full_head_spec(T, HEAD_DIM, 0)),
        out_shape=tuple(jax.ShapeDtypeStruct((T, W), BF16) for _ in range(3)),
        scratch_shapes=[pltpu.VMEM((T, HEAD_DIM), F32), pltpu.VMEM((T, HEAD_DIM), F32)]
        + [pltpu.VMEM((B, B), F32)] * 4
        + [pltpu.VMEM((B, SB_PARTS * B), BF16), pltpu.VMEM((B, B), BF16), pltpu.VMEM((B, B), BF16),
           pltpu.VMEM((B, 1), F32), pltpu.VMEM((B, 1), F32),
           pltpu.VMEM((SB_PARTS * B, B), BF16), pltpu.VMEM((SB_PARTS * B, B), BF16)],
        compiler_params=_params(("parallel", "arbitrary")),
    )(qkv, qkv, qkv, do, ctot)


def _softmax_fwd(q, k, v, cf, H, dqk, scale, *, name):
    T = q.shape[0]
    B = _tile(T, ATT_BLOCK)
    nq = T // B
    has_cf = cf is not None

    def body(*refs):
        q_ref, k_ref, v_ref = refs[:3]
        idx = 3
        if has_cf:
            cfc_ref, cfr_ref = refs[3], refs[4]
            idx = 5
        o_ref, of_ref, lse_ref = refs[idx:idx + 3]
        i = pl.program_id(1)
        qb = q_ref[...]
        rows = _iota2((B, B), 0)
        cols = _iota2((B, B), 1)

        def step(j, carry):
            m, l, acc = carry
            ks = pl.multiple_of(j * B, B)
            kb = k_ref[pl.ds(ks, B), :]
            vb = v_ref[pl.ds(ks, B), :]
            s = _dot(qb, kb, NT) * scale
            if has_cf:
                s = s + (cfc_ref[...] - cfr_ref[:, pl.ds(ks, B)])
            s = jnp.where((ks + cols) <= (i * B + rows), s, NEG)
            m_new = jnp.maximum(m, jnp.max(s, axis=1, keepdims=True))
            alpha = jnp.exp(m - m_new)
            p = jnp.exp(s - m_new)
            l = alpha * l + jnp.sum(p, axis=1, keepdims=True)
            acc = alpha * acc + _dot(p.astype(BF16), vb, NN)
            return m_new, l, acc

        m, l, acc = lax.fori_loop(
            0, i + 1, step,
            (jnp.full((B, 1), NEG, F32), jnp.zeros((B, 1), F32), jnp.zeros((B, HEAD_DIM), F32)))
        o = acc / l
        o_ref[...] = o.astype(BF16)
        of_ref[...] = o
        lse_ref[...] = m + jnp.log(l)

    stat = pl.BlockSpec((None, B, 1), lambda h, i: (h, i, 0))
    in_specs = [_head_spec(B, dqk, 0), _full_head_spec(T, dqk, 0), _full_head_spec(T, HEAD_DIM, 0)]
    args = [q, k, v]
    if has_cf:
        in_specs += [stat, pl.BlockSpec((None, 1, T), lambda h, i: (h, 0, 0))]
        args += list(cf)
    W = H * HEAD_DIM
    return pl.pallas_call(
        body, name=name, grid=(H, nq), in_specs=in_specs,
        out_specs=(_head_spec(B, HEAD_DIM, 0), _head_spec(B, HEAD_DIM, 0), stat),
        out_shape=(jax.ShapeDtypeStruct((T, W), BF16), jax.ShapeDtypeStruct((T, W), F32),
                   jax.ShapeDtypeStruct((H, T, 1), F32)),
        compiler_params=_params(("parallel", "arbitrary")),
    )(*args)


def _softmax_bwd(q, k, v, cf, do, o, lse, H, dqk, scale, *, name):
    T = q.shape[0]
    B = _tile(T, ATT_BLOCK)
    nq = T // B
    has_cf = cf is not None

    def body(*refs):
        q_ref, k_ref, v_ref, do_ref, o_ref, lse_ref = refs[:6]
        idx = 6
        if has_cf:
            cfc_ref, cfr_ref = refs[6], refs[7]
            idx = 8
        dq_ref, dk_ref, dv_ref = refs[idx:idx + 3]
        idx += 3
        if has_cf:
            dcc_ref, dcr_ref = refs[idx], refs[idx + 1]
        i = pl.program_id(1)

        @pl.when(i == 0)
        def _():
            dk_ref[...] = jnp.zeros_like(dk_ref)
            dv_ref[...] = jnp.zeros_like(dv_ref)
            if has_cf:
                dcr_ref[...] = jnp.zeros_like(dcr_ref)

        qb = q_ref[...]
        do_b = do_ref[...]
        lse_b = lse_ref[...]
        delta = jnp.sum(do_b.astype(F32) * o_ref[...], axis=1, keepdims=True)
        rows = _iota2((B, B), 0)
        cols = _iota2((B, B), 1)

        def step(j, carry):
            dq, rs = carry
            ks = pl.multiple_of(j * B, B)
            kb = k_ref[pl.ds(ks, B), :]
            vb = v_ref[pl.ds(ks, B), :]
            s = _dot(qb, kb, NT) * scale
            if has_cf:
                s = s + (cfc_ref[...] - cfr_ref[:, pl.ds(ks, B)])
            p = jnp.where((ks + cols) <= (i * B + rows), jnp.exp(s - lse_b), 0.0)
            dp = _dot(do_b, vb, NT)
            ds = p * (dp - delta)
            dsb = (ds * scale).astype(BF16)
            dq = dq + _dot(dsb, kb, NN)
            dk_ref[pl.ds(ks, B), :] += _dot(dsb, qb, TN)
            dv_ref[pl.ds(ks, B), :] += _dot(p.astype(BF16), do_b, TN)
            if has_cf:
                rs = rs + jnp.sum(ds, axis=1, keepdims=True)
                dcr_ref[:, pl.ds(ks, B)] -= jnp.sum(ds, axis=0, keepdims=True)
            return dq, rs

        dq, rs = lax.fori_loop(0, i + 1, step, (jnp.zeros((B, dqk), F32), jnp.zeros((B, 1), F32)))
        dq_ref[...] = dq
        if has_cf:
            dcc_ref[...] = rs

    stat = pl.BlockSpec((None, B, 1), lambda h, i: (h, i, 0))
    rowstat = pl.BlockSpec((None, 1, T), lambda h, i: (h, 0, 0))
    in_specs = [_head_spec(B, dqk, 0), _full_head_spec(T, dqk, 0), _full_head_spec(T, HEAD_DIM, 0),
                _head_spec(B, HEAD_DIM, 0), _head_spec(B, HEAD_DIM, 0), stat]
    args = [q, k, v, do, o, lse]
    out_specs = [_head_spec(B, dqk, 0), _full_head_spec(T, dqk, 0), _full_head_spec(T, HEAD_DIM, 0)]
    out_shape = [jax.ShapeDtypeStruct((T, H * dqk), F32), jax.ShapeDtypeStruct((T, H * dqk), F32),
                 jax.ShapeDtypeStruct((T, H * HEAD_DIM), F32)]
    if has_cf:
        in_specs += [stat, rowstat]
        args += list(cf)
        out_specs += [stat, rowstat]
        out_shape += [jax.ShapeDtypeStruct((H, T, 1), F32), jax.ShapeDtypeStruct((H, 1, T), F32)]
    return pl.pallas_call(
        body, name=name, grid=(H, nq), in_specs=in_specs, out_specs=tuple(out_specs),
        out_shape=tuple(out_shape), compiler_params=_params(("parallel", "arbitrary")),
    )(*args)


def _headnorm(x, g):
    r = lax.rsqrt(jnp.mean(x * x, axis=-1, keepdims=True) + EPS)
    return x * r * g, r


def _headnorm_bwd(x, g, dy, n):
    r = lax.rsqrt(jnp.sum(x * x, axis=-1, keepdims=True) * (1.0 / n) + EPS)
    t = dy * g
    dx = r * t - x * (r * r * r * (jnp.sum(t * x, axis=-1, keepdims=True) * (1.0 / n)))
    dg = jnp.sum(dy * x * r, axis=0, keepdims=True)
    return dx, dg


def _fox_prep_fwd(qkv, gq, gk, H, *, tm=512):
    T = qkv.shape[0]
    tm = _tile(T, tm)

    def body(q_ref, k_ref, v_ref, gq_ref, gk_ref, qn_ref, kn_ref, vb_ref):
        qn_ref[...] = _headnorm(q_ref[...], gq_ref[...])[0].astype(BF16)
        kn_ref[...] = _headnorm(k_ref[...], gk_ref[...])[0].astype(BF16)
        vb_ref[...] = v_ref[...].astype(BF16)

    def blk(off):
        return pl.BlockSpec((tm, HEAD_DIM), lambda i, h: (i, off + h))

    vec = pl.BlockSpec((1, HEAD_DIM), lambda i, h: (0, 0))
    W = H * HEAD_DIM
    return pl.pallas_call(
        body, name="fox_prep_fwd", grid=(T // tm, H),
        in_specs=[blk(0), blk(H), blk(2 * H), vec, vec], out_specs=(blk(0), blk(0), blk(0)),
        out_shape=tuple(jax.ShapeDtypeStruct((T, W), BF16) for _ in range(3)),
        compiler_params=_params(("parallel", "parallel")),
    )(qkv, qkv, qkv, gq, gk)


def _fox_prep_bwd(qkv, gq, gk, dqn, dkn, dv, H, *, tm=512):
    T = qkv.shape[0]
    tm = _tile(T, tm)

    def body(q_ref, k_ref, gq_ref, gk_ref, dqn_ref, dkn_ref, dv_ref,
             dq_ref, dk_ref, dvb_ref, dgq_ref, dgk_ref):
        @pl.when((pl.program_id(0) == 0) & (pl.program_id(1) == 0))
        def _():
            dgq_ref[...] = jnp.zeros_like(dgq_ref)
            dgk_ref[...] = jnp.zeros_like(dgk_ref)

        dq, dgq = _headnorm_bwd(q_ref[...], gq_ref[...], dqn_ref[...], HEAD_DIM)
        dk, dgk = _headnorm_bwd(k_ref[...], gk_ref[...], dkn_ref[...], HEAD_DIM)
        dq_ref[...] = dq.astype(BF16)
        dk_ref[...] = dk.astype(BF16)
        dvb_ref[...] = dv_ref[...].astype(BF16)
        dgq_ref[...] += dgq
        dgk_ref[...] += dgk

    def blk(off):
        return pl.BlockSpec((tm, HEAD_DIM), lambda i, h: (i, off + h))

    vec = pl.BlockSpec((1, HEAD_DIM), lambda i, h: (0, 0))
    W = H * HEAD_DIM
    return pl.pallas_call(
        body, name="fox_prep_bwd", grid=(T // tm, H),
        in_specs=[blk(0), blk(H), vec, vec, blk(0), blk(0), blk(0)],
        out_specs=(blk(0), blk(0), blk(0), vec, vec),
        out_shape=tuple(jax.ShapeDtypeStruct((T, W), BF16) for _ in range(3))
        + (jax.ShapeDtypeStruct((1, HEAD_DIM), F32), jax.ShapeDtypeStruct((1, HEAD_DIM), F32)),
        compiler_params=_params(("arbitrary", "arbitrary")),
    )(qkv, qkv, gq, gk, dqn, dkn, dv)


def _fox_gate_fwd(flog, bf, *, tm=256):
    T = flog.shape[0]
    tm = _tile(T, tm)

    def body(f_ref, b_ref, cf_ref, carry):
        @pl.when(pl.program_id(0) == 0)
        def _():
            carry[...] = jnp.zeros_like(carry)

        lf = _log_sigmoid(f_ref[...] + b_ref[...])
        tri = (_iota2((tm, tm), 1) <= _iota2((tm, tm), 0)).astype(BF16)
        cf_ref[...] = carry[...] + _tri_left(tri, lf)
        carry[...] += jnp.sum(lf, axis=0, keepdims=True)

    return pl.pallas_call(
        body, name="fox_gate_fwd", grid=(T // tm,),
        in_specs=[pl.BlockSpec((tm, LANE), lambda i: (i, 0)), pl.BlockSpec((1, LANE), lambda i: (0, 0))],
        out_specs=pl.BlockSpec((tm, LANE), lambda i: (i, 0)),
        out_shape=jax.ShapeDtypeStruct((T, LANE), F32),
        scratch_shapes=[pltpu.VMEM((1, LANE), F32)],
        compiler_params=_params(("arbitrary",)),
    )(flog, bf)


def _fox_gate_bwd(flog, bf, dcf, *, tm=256):
    T = flog.shape[0]
    tm = _tile(T, tm)
    nt = T // tm

    def body(f_ref, b_ref, dcf_ref, df_ref, db_ref, carry):
        @pl.when(pl.program_id(0) == 0)
        def _():
            carry[...] = jnp.zeros_like(carry)
            db_ref[...] = jnp.zeros_like(db_ref)

        d = dcf_ref[...]
        tri = (_iota2((tm, tm), 1) >= _iota2((tm, tm), 0)).astype(BF16)
        dlf = carry[...] + _tri_left(tri, d)
        carry[...] += jnp.sum(d, axis=0, keepdims=True)
        xg = f_ref[...] + b_ref[...]
        e = jnp.exp(-jnp.abs(xg))
        sig_neg = jnp.where(xg >= 0.0, e, 1.0) / (1.0 + e)
        df = dlf * sig_neg
        df_ref[...] = df.astype(BF16)
        db_ref[...] += jnp.sum(df, axis=0, keepdims=True)

    rev = pl.BlockSpec((tm, LANE), lambda i: (nt - 1 - i, 0))
    vec = pl.BlockSpec((1, LANE), lambda i: (0, 0))
    return pl.pallas_call(
        body, name="fox_gate_bwd", grid=(nt,), in_specs=[rev, vec, rev], out_specs=(rev, vec),
        out_shape=(jax.ShapeDtypeStruct((T, LANE), BF16), jax.ShapeDtypeStruct((1, LANE), F32)),
        scratch_shapes=[pltpu.VMEM((1, LANE), F32)],
        compiler_params=_params(("arbitrary",)),
    )(flog, bf, dcf)


def _rope_tables(positions):
    half = MLA_ROPE // 2
    inv_freq = ROPE_THETA ** (-jnp.arange(0, half, dtype=F32) * 2.0 / MLA_ROPE)
    ang = positions.astype(F32)[:, None] * inv_freq
    cos, sin = jnp.cos(ang), jnp.sin(ang)
    zero = jnp.zeros_like(cos)
    pad = jnp.zeros((positions.shape[0], LANE - MLA_ROPE), F32)
    cos_t = jnp.concatenate([cos, cos, pad], axis=1)
    sin_up = jnp.concatenate([zero, sin, pad], axis=1)
    sin_dn = jnp.concatenate([-sin, zero, pad], axis=1)
    return cos_t, sin_up, sin_dn


def _rope(x, cos_t, sin_up, sin_dn):
    half = MLA_ROPE // 2
    return x * cos_t + pltpu.roll(x, half, 1) * sin_up + pltpu.roll(x, LANE - half, 1) * sin_dn


def _rope_t(d, cos_t, sin_up, sin_dn):
    half = MLA_ROPE // 2
    return d * cos_t + pltpu.roll(d * sin_up, LANE - half, 1) + pltpu.roll(d * sin_dn, half, 1)


def _norm192(xcat, g):
    r = lax.rsqrt(jnp.sum(xcat * xcat, axis=-1, keepdims=True) * (1.0 / MLA_QK) + EPS)
    return xcat * r * g


def _mla_prep_fwd(qfull, kv, kr, tabs, gq, gk, H, *, tm=512):
    T = qfull.shape[0]
    tm = _tile(T, tm)

    def body(q_ref, kv_ref, kr_ref, c_ref, su_ref, sd_ref, gq_ref, gk_ref, qf_ref, kf_ref, v_ref):
        tabs_b = (c_ref[...], su_ref[...], sd_ref[...])
        qb = q_ref[...]
        qcat = jnp.concatenate([qb[:, :MLA_NOPE], _rope(qb[:, MLA_NOPE:], *tabs_b)], axis=1)
        qf_ref[...] = _norm192(qcat, gq_ref[...]).astype(BF16)
        kvb = kv_ref[...]
        kcat = jnp.concatenate([kvb[:, :MLA_NOPE], _rope(kr_ref[...], *tabs_b)], axis=1)
        kf_ref[...] = _norm192(kcat, gk_ref[...]).astype(BF16)
        v_ref[...] = kvb[:, MLA_NOPE:].astype(BF16)

    head = pl.BlockSpec((tm, MLA_QK_PAD), lambda i, h: (i, h))
    tok = pl.BlockSpec((tm, LANE), lambda i, h: (i, 0))
    vec = pl.BlockSpec((1, MLA_QK_PAD), lambda i, h: (0, 0))
    return pl.pallas_call(
        body, name="mla_prep_fwd", grid=(T // tm, H),
        in_specs=[head, head, tok, tok, tok, tok, vec, vec],
        out_specs=(head, head, pl.BlockSpec((tm, MLA_V), lambda i, h: (i, h))),
        out_shape=(jax.ShapeDtypeStruct((T, H * MLA_QK_PAD), BF16),
                   jax.ShapeDtypeStruct((T, H * MLA_QK_PAD), BF16),
                   jax.ShapeDtypeStruct((T, H * MLA_V), BF16)),
        compiler_params=_params(("parallel", "parallel")),
    )(qfull, kv, kr, *tabs, gq, gk)


def _mla_prep_bwd(qfull, kv, kr, tabs, gq, gk, dqf, dkf, dv, H, *, tm=512):
    T = qfull.shape[0]
    tm = _tile(T, tm)

    def body(q_ref, kv_ref, kr_ref, c_ref, su_ref, sd_ref, gq_ref, gk_ref, dqf_ref, dkf_ref, dv_ref,
             dq_ref, dkv_ref, dkr_ref, dgq_ref, dgk_ref, kr_acc):
        h = pl.program_id(1)

        @pl.when((pl.program_id(0) == 0) & (h == 0))
        def _():
            dgq_ref[...] = jnp.zeros_like(dgq_ref)
            dgk_ref[...] = jnp.zeros_like(dgk_ref)

        @pl.when(h == 0)
        def _():
            kr_acc[...] = jnp.zeros_like(kr_acc)

        tabs_b = (c_ref[...], su_ref[...], sd_ref[...])
        qb = q_ref[...]
        qcat = jnp.concatenate([qb[:, :MLA_NOPE], _rope(qb[:, MLA_NOPE:], *tabs_b)], axis=1)
        dqcat, dgq = _headnorm_bwd(qcat, gq_ref[...], dqf_ref[...], MLA_QK)
        dq_ref[...] = jnp.concatenate(
            [dqcat[:, :MLA_NOPE], _rope_t(dqcat[:, MLA_NOPE:], *tabs_b)], axis=1).astype(BF16)
        dgq_ref[...] += dgq
        kvb = kv_ref[...]
        kcat = jnp.concatenate([kvb[:, :MLA_NOPE], _rope(kr_ref[...], *tabs_b)], axis=1)
        dkcat, dgk = _headnorm_bwd(kcat, gk_ref[...], dkf_ref[...], MLA_QK)
        dkv_ref[...] = jnp.concatenate([dkcat[:, :MLA_NOPE], dv_ref[...]], axis=1).astype(BF16)
        dgk_ref[...] += dgk
        kr_acc[...] += dkcat[:, MLA_NOPE:]

        @pl.when(h == H - 1)
        def _():
            dkr_ref[...] = _rope_t(kr_acc[...], *tabs_b).astype(BF16)

    head = pl.BlockSpec((tm, MLA_QK_PAD), lambda i, h: (i, h))
    tok = pl.BlockSpec((tm, LANE), lambda i, h: (i, 0))
    vec = pl.BlockSpec((1, MLA_QK_PAD), lambda i, h: (0, 0))
    return pl.pallas_call(
        body, name="mla_prep_bwd", grid=(T // tm, H),
        in_specs=[head, head, tok, tok, tok, tok, vec, vec, head, head,
                  pl.BlockSpec((tm, MLA_V), lambda i, h: (i, h))],
        out_specs=(head, head, tok, vec, vec),
        out_shape=(jax.ShapeDtypeStruct((T, H * MLA_QK_PAD), BF16),
                   jax.ShapeDtypeStruct((T, H * MLA_QK_PAD), BF16),
                   jax.ShapeDtypeStruct((T, LANE), BF16),
                   jax.ShapeDtypeStruct((1, MLA_QK_PAD), F32), jax.ShapeDtypeStruct((1, MLA_QK_PAD), F32)),
        scratch_shapes=[pltpu.VMEM((tm, LANE), F32)],
        compiler_params=_params(("arbitrary", "arbitrary")),
    )(qfull, kv, kr, *tabs, gq, gk, dqf, dkf, dv)


def _cf_layouts(cf, H):
    cfh = cf[:, :H].T
    return cfh[:, :, None], cfh[:, None, :]


def _layer_fwd(i, x, P, tabs, H, hook=None):
    kind, j = i % 3, i // 3
    s = {"x_in": x}
    h = _rmsnorm_fwd(x, P["mix_norm"][i:i + 1], name="mix_norm_fwd")
    s["h"] = h
    if kind == 0:
        w = P["sb"][j]
        qkv = _mm(h, w["w_in"], out_dtype=BF16, name="sb_qkv")
        o, ctot = _sb_fwd(qkv, H)
        s.update(qkv=qkv, o=o, ctot=ctot)
    elif kind == 1:
        w = P["fox"][j]
        qkv = _mm(h, w["w_qkv"], name="fox_qkv")
        flog = _mm(h, w["w_f"], name="fox_gate_proj")
        qn, kn, vb = _fox_prep_fwd(qkv, w["gq"], w["gk"], H)
        cf = _cf_layouts(_fox_gate_fwd(flog, w["bf"]), H)
        o, of, lse = _softmax_fwd(qn, kn, vb, cf, H, HEAD_DIM, 1.0 / math.sqrt(HEAD_DIM), name="fox_attn_fwd")
        s.update(qkv=qkv, flog=flog, qn=qn, kn=kn, vb=vb, cf=cf, o=o, of=of, lse=lse)
    else:
        w = P["mla"][j]
        dq = _mm(h, w["w_dq"], name="mla_down_q")
        dkv = _mm(h, w["w_dkv"], name="mla_down_kv")
        kr = _mm(h, w["w_dr"], name="mla_down_rope")
        cq = _rmsnorm_fwd(dq, w["q_norm"], name="mla_q_norm_fwd")
        ckv = _rmsnorm_fwd(dkv, w["kv_norm"], name="mla_kv_norm_fwd")
        qfull = _mm(cq, w["w_uq"], name="mla_up_q")
        kv = _mm(ckv, w["w_ukv"], name="mla_up_kv")
        qf, kf, v = _mla_prep_fwd(qfull, kv, kr, tabs, w["gq"], w["gk"], H)
        o, of, lse = _softmax_fwd(qf, kf, v, None, H, MLA_QK_PAD, 1.0 / math.sqrt(MLA_QK), name="mla_attn_fwd")
        s.update(dq=dq, dkv=dkv, kr=kr, cq=cq, ckv=ckv, qfull=qfull, kv=kv, qf=qf, kf=kf, v=v,
                 o=o, of=of, lse=lse)
    x = _mm(s["o"], w["w_out"], res=x, name="mixer_out")
    s["x_mid"] = x
    deps = hook(x) if hook is not None else ()
    h2 = _rmsnorm_fwd(x, P["mlp_norm"][i:i + 1], name="mlp_norm_fwd")
    z, u = _mm(h2, P["mlp"][i]["w1"], act="relu2", name="mlp_up", deps=deps)
    x = _mm(u, P["mlp"][i]["w2"], res=x, name="mlp_down")
    s.update(h2=h2, z=z, u=u)
    return x, s


def _layer_bwd(i, dx, dxb, s, P, tabs, H, deps=(), hook=None):
    kind, j = i % 3, i // 3
    G = {}
    wm = P["mlp"][i]
    dz = _mm(dxb, wm["w2"], mode="nt", act="drelu2", z=s["z"], out_dtype=BF16, name="mlp_down_bwd",
             deps=deps)
    G["w2"] = _mm(s["u"], dxb, mode="tn", out_dtype=BF16, name="mlp_w2_grad")
    G["w1"] = _mm(s["h2"], dz, mode="tn", out_dtype=BF16, name="mlp_w1_grad")
    dh2 = _mm(dz, wm["w1"], mode="nt", name="mlp_up_bwd")
    dx, dxb, G["mlp_norm"] = _rmsnorm_bwd(s["x_mid"], P["mlp_norm"][i:i + 1], dh2, dx, name="mlp_norm_bwd")
    h = s["h"]
    w = P[("sb", "fox", "mla")[kind]][j]
    do = _mm(dxb, w["w_out"], mode="nt", out_dtype=BF16, name="mixer_out_bwd",
             deps=hook(dxb, G) if hook is not None else ())
    if kind == 0:
        G["w_out"] = _mm(s["o"], dxb, mode="tn", out_dtype=BF16, name="mixer_out_grad")
        dq, dk, dv = _sb_bwd(s["qkv"], do, s["ctot"], H)
        dproj = jnp.concatenate([dq, dk, dv], axis=1)
        G["w_in"] = _mm(h, dproj, mode="tn", out_dtype=BF16, name="sb_qkv_grad")
        dh = _mm(dproj, w["w_in"], mode="nt", name="sb_qkv_bwd")
    elif kind == 1:
        G["w_out"] = _mm(s["o"], dxb, mode="tn", out_dtype=BF16, name="mixer_out_grad")
        dqn, dkn, dv, dcc, dcr = _softmax_bwd(
            s["qn"], s["kn"], s["vb"], s["cf"], do, s["of"], s["lse"], H, HEAD_DIM,
            1.0 / math.sqrt(HEAD_DIM), name="fox_attn_bwd")
        dq, dk, dvb, G["gq"], G["gk"] = _fox_prep_bwd(s["qkv"], w["gq"], w["gk"], dqn, dkn, dv, H)
        dcf = (dcc[:, :, 0] + dcr[:, 0, :]).T
        dcf = jnp.pad(dcf, ((0, 0), (0, LANE - H)))
        dflog, G["bf"] = _fox_gate_bwd(s["flog"], w["bf"], dcf)
        dproj = jnp.concatenate([dq, dk, dvb], axis=1)
        G["w_qkv"] = _mm(h, dproj, mode="tn", out_dtype=BF16, name="fox_qkv_grad")
        G["w_f"] = _mm(h, dflog, mode="tn", out_dtype=BF16, name="fox_gate_grad")
        dh = _mm(dproj, w["w_qkv"], mode="nt", name="fox_qkv_bwd")
        dh = _mm(dflog, w["w_f"], mode="nt", res=dh, name="fox_gate_bwd_proj")
    else:
        G["w_out"] = _mm(s["o"], dxb, mode="tn", out_dtype=BF16, name="mixer_out_grad")
        dqf, dkf, dv = _softmax_bwd(
            s["qf"], s["kf"], s["v"], None, do, s["of"], s["lse"], H, MLA_QK_PAD,
            1.0 / math.sqrt(MLA_QK), name="mla_attn_bwd")
        dqfull, dkv, dkr, G["gq"], G["gk"] = _mla_prep_bwd(
            s["qfull"], s["kv"], s["kr"], tabs, w["gq"], w["gk"], dqf, dkf, dv, H)
        G["w_uq"] = _mm(s["cq"], dqfull, mode="tn", out_dtype=BF16, name="mla_up_q_grad")
        G["w_ukv"] = _mm(s["ckv"], dkv, mode="tn", out_dtype=BF16, name="mla_up_kv_grad")
        dcq = _mm(dqfull, w["w_uq"], mode="nt", name="mla_up_q_bwd")
        dckv = _mm(dkv, w["w_ukv"], mode="nt", name="mla_up_kv_bwd")
        ddq, G["q_norm"] = _rmsnorm_bwd(s["dq"], w["q_norm"], dcq, want_f32=False, name="mla_q_norm_bwd")
        ddkv, G["kv_norm"] = _rmsnorm_bwd(s["dkv"], w["kv_norm"], dckv, want_f32=False, name="mla_kv_norm_bwd")
        G["w_dq"] = _mm(h, ddq, mode="tn", out_dtype=BF16, name="mla_down_q_grad")
        G["w_dkv"] = _mm(h, ddkv, mode="tn", out_dtype=BF16, name="mla_down_kv_grad")
        G["w_dr"] = _mm(h, dkr, mode="tn", out_dtype=BF16, name="mla_down_rope_grad")
        dh = _mm(ddq, w["w_dq"], mode="nt", name="mla_down_q_bwd")
        dh = _mm(ddkv, w["w_dkv"], mode="nt", res=dh, name="mla_down_kv_bwd")
        dh = _mm(dkr, w["w_dr"], mode="nt", res=dh, name="mla_down_rope_bwd")
    dx, dxb, G["mix_norm"] = _rmsnorm_bwd(s["x_in"], P["mix_norm"][i:i + 1], dh, dx, name="mix_norm_bwd")
    return dx, dxb, G


def _local_step(x, positions, loss_target, P, depth, H):
    tabs = _rope_tables(positions)
    saved = []
    for i in range(depth):
        x, s = _layer_fwd(i, x, P, tabs, H)
        saved.append(s)
    dx, dxb, loss = _loss_head(x, loss_target)
    grads = [None] * depth
    for i in reversed(range(depth)):
        dx, dxb, grads[i] = _layer_bwd(i, dx, dxb, saved[i], P, tabs, H)
    return loss, dx, grads


def _pad_cols(a, n):
    return jnp.pad(a, ((0, 0), (0, n - a.shape[1])))


def _prepare_layer(P, full, i, H, parts=("mix", "mlp")):
    W = H * HEAD_DIM
    kind, j = i % 3, i // 3
    if "mlp" in parts:
        P["mlp"][i] = {"w1": full["mlp_w1"][i], "w2": full["mlp_w2"][i]}
    if "mix" not in parts:
        return
    if kind == 0:
        P["sb"][j] = {"w_in": full["sb_w_in"][j], "w_out": full["sb_w_out"][j]}
    elif kind == 1:
        w = full["fox_w_in"][j]
        P["fox"][j] = {
            "w_qkv": w[:, :3 * W], "w_f": _pad_cols(w[:, 3 * W:], LANE),
            "bf": _pad_cols(full["fox_b_f"][j:j + 1], LANE),
            "gq": full["fox_q_gain"][j:j + 1], "gk": full["fox_k_gain"][j:j + 1],
            "w_out": full["fox_w_out"][j]}
    else:
        w = full["mla_w_in"][j]
        q_norm, kv_norm = full["mla_q_norm"][j], full["mla_kv_norm"][j]
        rq, rkv = q_norm.shape[0], kv_norm.shape[0]
        w_uq = full["mla_w_uq"][j].reshape(rq, H, MLA_QK)
        w_uq = jnp.pad(w_uq, ((0, 0), (0, 0), (0, MLA_QK_PAD - MLA_QK))).reshape(rq, H * MLA_QK_PAD)
        P["mla"][j] = {
            "w_dq": w[:, :rq], "w_dkv": w[:, rq:rq + rkv], "w_dr": _pad_cols(w[:, rq + rkv:], LANE),
            "q_norm": q_norm[None], "kv_norm": kv_norm[None],
            "w_uq": w_uq, "w_ukv": full["mla_w_ukv"][j],
            "gq": _pad_cols(full["mla_q_gain"][j:j + 1], MLA_QK_PAD),
            "gk": _pad_cols(full["mla_k_gain"][j:j + 1], MLA_QK_PAD),
            "w_out": full["mla_w_out"][j]}


def _prepare(full, H):
    depth = len(full["mlp_w1"])
    P = {"mix_norm": full["mix_norm"], "mlp_norm": full["mlp_norm"], "mlp": [None] * depth,
         "sb": [None] * len(full["sb_w_in"]), "fox": [None] * len(full["fox_w_in"]),
         "mla": [None] * len(full["mla_w_in"])}
    for i in range(depth):
        _prepare_layer(P, full, i, H)
    return P


def _layer_grads(i, G, H):
    kind = i % 3
    out = {"mix_norm": G["mix_norm"], "mlp_norm": G["mlp_norm"], "mlp_w1": G["w1"], "mlp_w2": G["w2"]}
    if kind == 0:
        out.update(sb_w_in=G["w_in"], sb_w_out=G["w_out"])
    elif kind == 1:
        out.update(fox_w_in=jnp.concatenate([G["w_qkv"], G["w_f"][:, :H]], axis=1), fox_b_f=G["bf"][:, :H],
                   fox_q_gain=G["gq"], fox_k_gain=G["gk"], fox_w_out=G["w_out"])
    else:
        rq = G["w_uq"].shape[0]
        out.update(
            mla_w_in=jnp.concatenate([G["w_dq"], G["w_dkv"], G["w_dr"][:, :MLA_ROPE]], axis=1),
            mla_q_norm=G["q_norm"], mla_kv_norm=G["kv_norm"],
            mla_w_uq=G["w_uq"].reshape(rq, H, MLA_QK_PAD)[:, :, :MLA_QK].reshape(rq, H * MLA_QK),
            mla_w_ukv=G["w_ukv"], mla_q_gain=G["gq"][:, :MLA_QK], mla_k_gain=G["gk"][:, :MLA_QK],
            mla_w_out=G["w_out"])
    return out


def _layer_of(name, j):
    if name.startswith("sb_"):
        return 3 * j
    if name.startswith("fox_"):
        return 3 * j + 1
    if name.startswith("mla_"):
        return 3 * j + 2
    return j


def _unprepare(grads, H):
    out = {}
    for i, G in enumerate(grads):
        for n, g in _layer_grads(i, G, H).items():
            out.setdefault(n, []).append(g)
    return out


class _Place:
    def __init__(self, x, y, c):
        self.x, self.y, self.c = x, y, c
        self.dev = 4 * x + 2 * y + c
        self.chip = 2 * x + y
        self.id = (x, y, c)


def _peers(me, kind):
    if kind == "ici":
        return [_Place(1 - me.x, me.y, me.c), _Place(me.x, 1 - me.y, me.c), _Place(1 - me.x, 1 - me.y, me.c)]
    return [_Place(me.x, me.y, 1 - me.c)]


def _exchange(name, kind, operands, out_shapes, aliases, n_remote, n_local, plan):
    n_in, n_out = len(operands), len(out_shapes)

    def body(*refs):
        in_refs, out_refs = refs[:n_in], refs[n_in:n_in + n_out]
        send_sems, recv_sems, local_sems = refs[n_in + n_out:]
        me = _Place(lax.axis_index("x"), lax.axis_index("y"), lax.axis_index("c"))
        peers = _peers(me, kind)
        remote, local = plan(me, peers, in_refs, out_refs)
        assert len(remote) == n_remote and len(local) == n_local
        sends = []
        for n, (src, dst, k, _) in enumerate(remote):
            cp = pltpu.make_async_remote_copy(
                src_ref=src, dst_ref=dst, send_sem=send_sems.at[n], recv_sem=recv_sems.at[n],
                device_id=peers[k].id, device_id_type=MESH)
            cp.start()
            sends.append(cp)
        own = []
        for n, (src, dst) in enumerate(local):
            cp = pltpu.make_async_copy(src, dst, local_sems.at[n])
            cp.start()
            own.append(cp)
        for n, (src, _, k, landing) in enumerate(remote):
            pltpu.make_async_remote_copy(
                src_ref=src, dst_ref=landing, send_sem=send_sems.at[n], recv_sem=recv_sems.at[n],
                device_id=peers[k].id, device_id_type=MESH).wait_recv()
        for cp in sends:
            cp.wait_send()
        for cp in own:
            cp.wait()

    outs = pl.pallas_call(
        body, name=name, in_specs=[ANY] * n_in, out_specs=tuple([ANY] * n_out),
        out_shape=tuple(out_shapes), input_output_aliases=aliases,
        scratch_shapes=[pltpu.SemaphoreType.DMA((n_remote,)), pltpu.SemaphoreType.DMA((n_remote,)),
                        pltpu.SemaphoreType.DMA((max(n_local, 1),))],
    )(*operands)
    return list(outs)


def _window(ref, kind, d, shard_shape):
    r, c = shard_shape
    if kind == "col":
        return ref.at[:, pl.ds(pl.multiple_of(d * c, c), c)]
    return ref.at[pl.ds(pl.multiple_of(d * r, r), r), :]


def _full_shape(kind, shard_shape):
    r, c = shard_shape
    return (r, N_DEV * c) if kind == "col" else (N_DEV * r, c)


HBM = pl.BlockSpec(memory_space=pltpu.HBM)
SEM = pl.BlockSpec(memory_space=pltpu.SEMAPHORE)
EFFECT = pltpu.SideEffectType.DATAFLOW_SIDE_EFFECTING


class _Flight:
    def __init__(self, name, kind, n_remote, plan, send_sems, recv_sems, bufs, token):
        self.name, self.kind, self.n_remote, self.plan = name, kind, n_remote, plan
        self.send_sems, self.recv_sems, self.bufs, self.token = send_sems, recv_sems, bufs, token


def _start_copies(name, kind, bufs, n_remote, plan, after=None):
    nb = len(bufs)
    n_after = 0 if after is None else 1

    def body(*refs):
        in_refs = refs[:nb]
        send_sems, recv_sems = refs[nb + n_after], refs[nb + n_after + 1]
        token = refs[2 * nb + n_after + 2]
        me = _Place(lax.axis_index("x"), lax.axis_index("y"), lax.axis_index("c"))
        peers = _peers(me, kind)
        for n, (src, dst, k, _) in enumerate(plan(me, peers, in_refs)):
            pltpu.make_async_remote_copy(
                src_ref=src, dst_ref=dst, send_sem=send_sems.at[n], recv_sem=recv_sems.at[n],
                device_id=peers[k].id, device_id_type=MESH).start()
        token[...] = jnp.zeros_like(token)

    outs = pl.pallas_call(
        body, name=name, in_specs=[HBM] * nb + [ANY] * n_after,
        out_specs=(SEM, SEM) + (HBM,) * nb + (pl.BlockSpec(memory_space=pltpu.VMEM),),
        out_shape=(pltpu.SemaphoreType.DMA((n_remote,)), pltpu.SemaphoreType.DMA((n_remote,)))
        + tuple(pltpu.HBM(b.shape, b.dtype) for b in bufs) + (jax.ShapeDtypeStruct((8, LANE), F32),),
        input_output_aliases={a: 2 + a for a in range(nb)},
        compiler_params=pltpu.CompilerParams(has_side_effects=EFFECT),
    )(*[pltpu.with_memory_space_constraint(b, pltpu.HBM) for b in bufs], *([after] if n_after else []))
    return _Flight(name, kind, n_remote, plan, outs[0], outs[1], list(outs[2:2 + nb]), outs[2 + nb])


def _wait_copies(flight, after):
    nb = len(flight.bufs)
    plan, kind = flight.plan, flight.kind

    def body(*refs):
        in_refs = refs[:nb]
        send_sems, recv_sems = refs[nb], refs[nb + 1]
        me = _Place(lax.axis_index("x"), lax.axis_index("y"), lax.axis_index("c"))
        peers = _peers(me, kind)
        for n, (src, _, k, landing) in enumerate(plan(me, peers, in_refs)):
            cp = pltpu.make_async_remote_copy(
                src_ref=src, dst_ref=landing, send_sem=send_sems.at[n], recv_sem=recv_sems.at[n],
                device_id=peers[k].id, device_id_type=MESH)
            cp.wait_send()
            cp.wait_recv()

    outs = pl.pallas_call(
        body, name=flight.name.replace("_start", "_wait"),
        in_specs=[HBM] * nb + [SEM, SEM, ANY], out_specs=(HBM,) * nb,
        out_shape=tuple(pltpu.HBM(b.shape, b.dtype) for b in flight.bufs),
        input_output_aliases={a: a for a in range(nb)},
        compiler_params=pltpu.CompilerParams(has_side_effects=EFFECT),
    )(*flight.bufs, flight.send_sems, flight.recv_sems, after)
    return list(outs)


def _own_in_place(shard, kind, dev, dtype):
    r, c = shard.shape
    tr = _row_tile(r, c)
    nb = r // tr

    def body(dev_ref, s_ref, o_ref):
        o_ref[...] = s_ref[...].astype(dtype)

    if kind == "col":
        o_spec = pl.BlockSpec((tr, c), lambda i, dev_ref: (i, dev_ref[0]))
    else:
        o_spec = pl.BlockSpec((tr, c), lambda i, dev_ref: (dev_ref[0] * nb + i, 0))
    return pl.pallas_call(
        body, name="own_in_place",
        grid_spec=pltpu.PrefetchScalarGridSpec(
            num_scalar_prefetch=1, grid=(nb,),
            in_specs=[pl.BlockSpec((tr, c), lambda i, dev_ref: (i, 0))], out_specs=o_spec),
        out_shape=jax.ShapeDtypeStruct(_full_shape(kind, shard.shape), dtype),
        compiler_params=_params(("parallel",)),
    )(dev, shard)


def _gather_ici_start(fulls, kinds, shapes, tag, after=None):
    n = len(fulls)

    def plan(me, peers, refs):
        remote = []
        for a in range(n):
            mine = _window(refs[a], kinds[a], me.dev, shapes[a])
            for k, p in enumerate(peers):
                remote.append((mine, mine, k, _window(refs[a], kinds[a], p.dev, shapes[a])))
        return remote

    return _start_copies("gather_ici_start_" + tag, "ici", list(fulls), 3 * n, plan, after)


def _gather_d2d_start(fulls, kinds, shapes, tag):
    n = len(fulls)

    def plan(me, peers, refs):
        remote = []
        for a in range(n):
            for ch in range(N_CHIP):
                held = _window(refs[a], kinds[a], 2 * ch + me.c, shapes[a])
                remote.append((held, held, 0, _window(refs[a], kinds[a], 2 * ch + 1 - me.c, shapes[a])))
        return remote

    return _start_copies("gather_d2d_start_" + tag, "d2d", fulls, N_CHIP * n, plan)


def _scatter_d2d_start(grads, kinds, shapes, tag):
    n = len(grads)
    lands = [lax.empty((N_CHIP,) + tuple(s), g.dtype) for g, s in zip(grads, shapes)]

    def plan(me, peers, refs):
        remote = []
        for a in range(n):
            for ch in range(N_CHIP):
                remote.append((_window(refs[a], kinds[a], 2 * ch + 1 - me.c, shapes[a]),
                               refs[n + a].at[ch], 0, refs[n + a].at[ch]))
        return remote

    return _start_copies("scatter_d2d_start_" + tag, "d2d", list(grads) + lands, N_CHIP * n, plan)


def _scatter_ici_start(sums, shapes, tag, after=None):
    n = len(sums)
    lands = [lax.empty((N_CHIP - 1,) + tuple(s), BF16) for s in shapes]

    def plan(me, peers, refs):
        remote = []
        for a in range(n):
            for k, p in enumerate(peers):
                remote.append((refs[a].at[p.chip], refs[n + a].at[k], k, refs[n + a].at[k]))
        return remote

    return _start_copies("scatter_ici_start_" + tag, "ici", list(sums) + lands, 3 * n, plan, after)


def _pair_add(g, r4, kind, shard_shape, core, *, name):
    r, c = shard_shape
    tr = _row_tile(r, c, 524288)
    nb = r // tr
    if kind == "all":
        def body_all(core_ref, g_ref, r_ref, o_ref):
            o_ref[...] = g_ref[...] + r_ref[...]

        spec = pl.BlockSpec((tr, c), lambda i, core_ref: (i, 0))
        return pl.pallas_call(
            body_all, name=name,
            grid_spec=pltpu.PrefetchScalarGridSpec(
                num_scalar_prefetch=1, grid=(nb,), in_specs=[spec, spec], out_specs=spec),
            out_shape=jax.ShapeDtypeStruct((r, c), F32),
            compiler_params=_params(("parallel",)),
        )(core, g, r4)

    def body(core_ref, g_ref, r_ref, o_ref):
        o_ref[...] = (g_ref[...].astype(F32) + r_ref[...].astype(F32)).astype(BF16)

    if kind == "col":
        g_spec = pl.BlockSpec((tr, c), lambda ch, i, core_ref: (i, 2 * ch + core_ref[0]))
    else:
        g_spec = pl.BlockSpec((tr, c), lambda ch, i, core_ref: ((2 * ch + core_ref[0]) * nb + i, 0))
    slot = pl.BlockSpec((None, tr, c), lambda ch, i, core_ref: (ch, i, 0))
    return pl.pallas_call(
        body, name=name,
        grid_spec=pltpu.PrefetchScalarGridSpec(
            num_scalar_prefetch=1, grid=(N_CHIP, nb), in_specs=[g_spec, slot], out_specs=slot),
        out_shape=jax.ShapeDtypeStruct((N_CHIP, r, c), BF16),
        compiler_params=_params(("parallel", "parallel")),
    )(core, g, r4)


def _reduce_scatter(grads, kinds, shard_shapes, core, tag):
    n = len(grads)
    land = []
    for g, k, s in zip(grads, kinds, shard_shapes):
        land.append(jax.ShapeDtypeStruct(s if k == "all" else (N_CHIP,) + tuple(s), g.dtype))
    n_rem = sum(1 if k == "all" else N_CHIP for k in kinds)

    def plan_d2d(me, peers, ins, outs):
        remote = []
        for a in range(n):
            if kinds[a] == "all":
                remote.append((ins[a], outs[a], 0, outs[a]))
                continue
            for ch in range(N_CHIP):
                remote.append((_window(ins[a], kinds[a], 2 * ch + 1 - me.c, shard_shapes[a]),
                               outs[a].at[ch], 0, outs[a].at[ch]))
        return remote, []

    got = _exchange("scatter_d2d_" + tag, "d2d", list(grads), land, {}, n_rem, 0, plan_d2d)
    sums = [_pair_add(g, r4, k, s, core, name="pair_add_" + tag)
            for g, r4, k, s in zip(grads, got, kinds, shard_shapes)]
    land2 = [jax.ShapeDtypeStruct((N_CHIP,) + tuple(s), F32 if k == "all" else BF16)
             for k, s in zip(kinds, shard_shapes)]

    def plan_ici(me, peers, ins, outs):
        remote, local = [], []
        for a in range(n):
            for k, p in enumerate(peers):
                src = ins[a] if kinds[a] == "all" else ins[a].at[p.chip]
                remote.append((src, outs[a].at[me.chip], k, outs[a].at[p.chip]))
            src = ins[a] if kinds[a] == "all" else ins[a].at[me.chip]
            local.append((src, outs[a].at[me.chip]))
        return remote, local

    return _exchange("scatter_ici_" + tag, "ici", sums, land2, {}, 3 * n, n, plan_ici)


def _row_tile(r, c, limit=262144):
    best = None
    for t in range(8, r + 1, 8):
        if r % t == 0 and t * c <= limit:
            best = t
    return best if best is not None else r


def _adam_math(w, g, m, v):
    m = ADAM_B1 * m + (1.0 - ADAM_B1) * g
    v = ADAM_B2 * v + (1.0 - ADAM_B2) * (g * g)
    m_hat = m / (1.0 - ADAM_B1 ** ADAM_STEP)
    v_hat = v / (1.0 - ADAM_B2 ** ADAM_STEP)
    delta = -ADAM_LR * (m_hat / (jnp.sqrt(v_hat) + ADAM_EPS) + ADAM_WD * w)
    return delta, m, v


def _adamw(w, m, v, sums, lands, chip, layer, prev, *, name):
    L, r, c = w.shape
    tr = _row_tile(r, c)
    has_prev = prev is not None

    def body(chip_ref, *refs):
        w_ref, m_ref, v_ref, s_ref, l_ref = refs[:5]
        g_ref, d_ref, nm_ref, nv_ref = refs[-4:]
        g = (s_ref[...].astype(F32) + l_ref[0].astype(F32)) + l_ref[1].astype(F32) + l_ref[2].astype(F32)
        delta, nm, nv = _adam_math(w_ref[...], g, m_ref[...], v_ref[...])
        g_ref[...] = g
        d_ref[...] = delta
        nm_ref[...] = nm
        nv_ref[...] = nv

    slab = pl.BlockSpec((None, tr, c), lambda i, chip_ref: (layer, i, 0))
    in_specs = [slab, slab, slab, pl.BlockSpec((None, tr, c), lambda i, chip_ref: (chip_ref[0], i, 0)),
                pl.BlockSpec((N_CHIP - 1, tr, c), lambda i, chip_ref: (0, i, 0))]
    args = [chip, w, m, v, sums, lands]
    aliases = {}
    if has_prev:
        in_specs += [ANY] * 4
        args += list(prev)
        aliases = {6 + n: n for n in range(4)}
    return pl.pallas_call(
        body, name=name,
        grid_spec=pltpu.PrefetchScalarGridSpec(
            num_scalar_prefetch=1, grid=(r // tr,), in_specs=in_specs, out_specs=(slab,) * 4),
        out_shape=tuple(jax.ShapeDtypeStruct((L, r, c), F32) for _ in range(4)),
        input_output_aliases=aliases, compiler_params=_params(("parallel",)),
    )(*args)


def _adamw_small(w, g, m, v):
    def body(w_ref, g_ref, m_ref, v_ref, d_ref, nm_ref, nv_ref):
        d_ref[...], nm_ref[...], nv_ref[...] = _adam_math(w_ref[...], g_ref[...], m_ref[...], v_ref[...])

    return pl.pallas_call(
        body, name="adamw_small", out_shape=tuple(jax.ShapeDtypeStruct(w.shape, F32) for _ in range(3)),
    )(w, g, m, v)


def _sum4(parts):
    def body(p_ref, o_ref):
        o_ref[...] = (p_ref[0] + p_ref[1]) + p_ref[2] + p_ref[3]

    return pl.pallas_call(
        body, name="sum_chips", out_shape=jax.ShapeDtypeStruct(parts.shape[1:], F32))(parts)


_WEIGHTS = ["mix_norm", "mlp_norm", "sb_w_in", "sb_w_out", "fox_w_in", "fox_b_f", "fox_q_gain", "fox_k_gain",
            "fox_w_out", "mla_w_in", "mla_q_norm", "mla_kv_norm", "mla_w_uq", "mla_w_ukv", "mla_q_gain",
            "mla_k_gain", "mla_w_out", "mlp_w1", "mlp_w2"]
_BIG = {"sb_w_in": "col", "sb_w_out": "row", "fox_w_in": "row", "fox_w_out": "row", "mla_w_in": "row",
        "mla_w_uq": "col", "mla_w_ukv": "col", "mla_w_out": "row", "mlp_w1": "col", "mlp_w2": "row"}


def _layer_big(i):
    kind, j = i % 3, i // 3
    mixer = {0: ["sb_w_in", "sb_w_out"], 1: ["fox_w_in", "fox_w_out"],
             2: ["mla_w_in", "mla_w_uq", "mla_w_ukv", "mla_w_out"]}[kind]
    return [(n, j) for n in mixer] + [("mlp_w1", i), ("mlp_w2", i)]


def _stack_to_cols(a):
    r = a.shape[0] // N_DEV
    return a.reshape(N_DEV, r, a.shape[1]).transpose(1, 0, 2).reshape(r, N_DEV * a.shape[1])


def _cols_to_stack(a):
    c = a.shape[1] // N_DEV
    return a.reshape(a.shape[0], N_DEV, c).transpose(1, 0, 2).reshape(N_DEV * a.shape[0], c)


def _pack_rows(rows, width):
    rows = [jnp.pad(r.reshape(-1).astype(F32), (0, width - r.size)) for r in rows]
    pad = (-len(rows)) % 8
    rows += [jnp.zeros((width,), F32)] * pad
    return jnp.stack(rows)


def kernel(x, positions, mix_norm, mlp_norm, sb_w_in, sb_w_out, fox_w_in, fox_b_f, fox_q_gain, fox_k_gain, fox_w_out, mla_w_in, mla_q_norm, mla_kv_norm, mla_w_uq, mla_w_ukv, mla_q_gain, mla_k_gain, mla_w_out, mlp_w1, mlp_w2, loss_target, m_mix_norm, m_mlp_norm, m_sb_w_in, m_sb_w_out, m_fox_w_in, m_fox_b_f, m_fox_q_gain, m_fox_k_gain, m_fox_w_out, m_mla_w_in, m_mla_q_norm, m_mla_kv_norm, m_mla_w_uq, m_mla_w_ukv, m_mla_q_gain, m_mla_k_gain, m_mla_w_out, m_mlp_w1, m_mlp_w2, v_mix_norm, v_mlp_norm, v_sb_w_in, v_sb_w_out, v_fox_w_in, v_fox_b_f, v_fox_q_gain, v_fox_k_gain, v_fox_w_out, v_mla_w_in, v_mla_q_norm, v_mla_kv_norm, v_mla_w_uq, v_mla_w_ukv, v_mla_q_gain, v_mla_k_gain, v_mla_w_out, v_mlp_w1, v_mlp_w2):
    w_in = dict(zip(_WEIGHTS, (mix_norm, mlp_norm, sb_w_in, sb_w_out, fox_w_in, fox_b_f, fox_q_gain, fox_k_gain, fox_w_out, mla_w_in, mla_q_norm, mla_kv_norm, mla_w_uq, mla_w_ukv, mla_q_gain, mla_k_gain, mla_w_out, mlp_w1, mlp_w2)))
    m_in = dict(zip(_WEIGHTS, (m_mix_norm, m_mlp_norm, m_sb_w_in, m_sb_w_out, m_fox_w_in, m_fox_b_f, m_fox_q_gain, m_fox_k_gain, m_fox_w_out, m_mla_w_in, m_mla_q_norm, m_mla_kv_norm, m_mla_w_uq, m_mla_w_ukv, m_mla_q_gain, m_mla_k_gain, m_mla_w_out, m_mlp_w1, m_mlp_w2)))
    v_in = dict(zip(_WEIGHTS, (v_mix_norm, v_mlp_norm, v_sb_w_in, v_sb_w_out, v_fox_w_in, v_fox_b_f, v_fox_q_gain, v_fox_k_gain, v_fox_w_out, v_mla_w_in, v_mla_q_norm, v_mla_kv_norm, v_mla_w_uq, v_mla_w_ukv, v_mla_q_gain, v_mla_k_gain, v_mla_w_out, v_mlp_w1, v_mlp_w2)))
    depth, D = mix_norm.shape
    H = D // HEAD_DIM
    n_mla = mla_w_in.shape[0]
    dev = 4 * lax.axis_index("x") + 2 * lax.axis_index("y") + lax.axis_index("c")
    core = lax.axis_index("c").astype(jnp.int32).reshape(1)
    dev_arr = dev.astype(jnp.int32).reshape(1)

    chip = (2 * lax.axis_index("x") + lax.axis_index("y")).astype(jnp.int32).reshape(1)
    nq, nkv = mla_q_norm.shape[1], mla_kv_norm.shape[1]

    units = []
    for i in range(depth):
        big = _layer_big(i)
        units += [(i, ("mix",), big[:-2]), (i, ("mlp",), big[-2:])] if i == 0 else [(i, ("mix", "mlp"), big)]

    def unit_of(i):
        return i + 1 if i else 0

    def tag_of(u):
        i, parts, _ = units[u]
        return f"l{i}" if len(parts) == 2 else f"l{i}_{parts[0]}"

    kinds_of, shapes_of, ici = [], [], []
    for i, parts, names in units:
        shards = [w_in[n][j] for n, j in names]
        kinds = [_BIG[n] for n, _ in names]
        fulls = [_own_in_place(s, k, dev_arr, BF16) for s, k in zip(shards, kinds)]
        if i % 3 == 2:
            shards.append(_pack_rows([mla_q_norm[i // 3], mla_kv_norm[i // 3]], LANE))
            kinds.append("row")
            fulls.append(_own_in_place(shards[-1], "row", dev_arr, F32))
        kinds_of.append(kinds)
        shapes_of.append([s.shape for s in shards])
        ici.append(_gather_ici_start(fulls, kinds, shapes_of[-1], tag_of(len(ici)),
                                     ici[-1].token if ici else None))

    full = {n: [None] * w_in[n].shape[0] for n in _WEIGHTS}
    for n in ("mix_norm", "mlp_norm", "fox_b_f", "fox_q_gain", "fox_k_gain", "mla_q_gain", "mla_k_gain"):
        full[n] = w_in[n]
    P = {"mix_norm": mix_norm, "mlp_norm": mlp_norm, "mlp": [None] * depth,
         "sb": [None] * sb_w_in.shape[0], "fox": [None] * fox_w_in.shape[0], "mla": [None] * n_mla}

    def forward_to_sibling(u, after):
        arrived = _wait_copies(ici[u], after)
        return _gather_d2d_start(arrived, kinds_of[u], shapes_of[u], tag_of(u))

    def finish_gather(u, flight, after):
        i, parts, names = units[u]
        got = _wait_copies(flight, after)
        for (n, j), a in zip(names, got):
            full[n][j] = _stack_to_cols(a) if n == "fox_w_in" else a
        if i % 3 == 2:
            tiles = got[-1].reshape(N_DEV, 8, LANE)
            full["mla_q_norm"][i // 3] = tiles[:, 0, :nq].reshape(-1)
            full["mla_kv_norm"][i // 3] = tiles[:, 1, :nkv].reshape(-1)
        _prepare_layer(P, full, i, H, parts)

    tabs = _rope_tables(positions[0])
    d2d = forward_to_sibling(0, ici[-1].token)
    finish_gather(0, d2d, d2d.token)
    xs, saved = x[0], []
    for i in range(depth):
        nxt = {}

        def hook(x_mid, i=i, nxt=nxt):
            if i == 0:
                mlp_d2d = forward_to_sibling(1, x_mid)
                finish_gather(1, mlp_d2d, mlp_d2d.token)
                return ()
            if i + 1 == depth:
                return ()
            nxt["d2d"] = forward_to_sibling(unit_of(i + 1), x_mid)
            return (nxt["d2d"].token,)

        xs, s = _layer_fwd(i, xs, P, tabs, H, hook)
        saved.append(s)
        if i == 0 and depth > 1:
            nxt["d2d"] = forward_to_sibling(unit_of(1), xs)
            finish_gather(unit_of(1), nxt["d2d"], nxt["d2d"].token)
        elif i + 1 < depth:
            finish_gather(unit_of(i + 1), nxt["d2d"], xs)
    dx, dxb, loss = _loss_head(xs, loss_target[0])
    loss = lax.psum(loss[0, 0], ("x", "y", "c"))

    def unit_meta(u):
        names = units[u][2]
        return [_BIG[n] for n, _ in names], [w_in[n].shape[1:] for n, _ in names]

    def to_sibling_start(u, Gl):
        grads = [(_cols_to_stack(Gl[n]) if n == "fox_w_in" else Gl[n]) for n, _ in units[u][2]]
        return _scatter_d2d_start(grads, *unit_meta(u), tag_of(u))

    def pair_and_send(u, flight, after, before=None):
        n = len(units[u][2])
        kinds, shapes = unit_meta(u)
        got = _wait_copies(flight, after)
        sums = [_pair_add(g, r4, k, s, core, name="pair_add_" + tag_of(u))
                for g, r4, k, s in zip(got[:n], got[n:], kinds, shapes)]
        return _scatter_ici_start(sums, shapes, tag_of(u), before)

    G, to_chips, deps, flying = [None] * depth, [None] * len(units), (), {}
    for i in reversed(range(depth)):
        def hook(dxb_mid, Gp, i=i):
            out = ()
            if i + 1 < depth:
                u = unit_of(i + 1)
                to_chips[u] = pair_and_send(u, flying.pop(u), dxb_mid)
                out = (to_chips[u].token,)
            if i == 0:
                flying[1] = to_sibling_start(1, {"mlp_w1": Gp["w1"], "mlp_w2": Gp["w2"]})
                out = out + (flying[1].token,)
            return out

        dx, dxb, Gp = _layer_bwd(i, dx, dxb, saved[i], P, tabs, H, deps, hook)
        G[i] = _layer_grads(i, Gp, H)
        flying[unit_of(i)] = to_sibling_start(unit_of(i), G[i])
        deps = (flying[unit_of(i)].token,)
    grad_x = dx

    small = [n for n in _WEIGHTS if n not in _BIG]
    rows = []
    for n in small:
        for j in range(w_in[n].shape[0]):
            rows.append(G[_layer_of(n, j)][n])
    sg = _pack_rows(rows, D)
    (parts,) = _reduce_scatter([sg], ["all"], [sg.shape], core, "small")
    sgr = _sum4(parts)
    to_chips[1] = pair_and_send(1, flying.pop(1), dx, sgr)
    to_chips[0] = pair_and_send(0, flying.pop(0), to_chips[1].token)

    results = {}
    for u in reversed(range(len(units))):
        names = units[u][2]
        got = _wait_copies(to_chips[u], to_chips[0].token)
        for a, (n, j) in enumerate(names):
            results[n] = _adamw(w_in[n], m_in[n], v_in[n], got[a], got[len(names) + a], chip, j, results.get(n),
                                name="adamw_" + n)

    g_rows, w_rows, m_rows, v_rows, at = [], [], [], [], 0
    for n in small:
        for j in range(w_in[n].shape[0]):
            width = w_in[n].shape[1]
            if n in ("mla_q_norm", "mla_kv_norm"):
                g_rows.append(lax.dynamic_slice(sgr[at], (dev * width,), (width,)))
            else:
                g_rows.append(sgr[at, :width])
            w_rows.append(w_in[n][j])
            m_rows.append(m_in[n][j])
            v_rows.append(v_in[n][j])
            at += 1
    g_pack = _pack_rows(g_rows, D)
    d_pack, nm_pack, nv_pack = _adamw_small(
        _pack_rows(w_rows, D), g_pack, _pack_rows(m_rows, D), _pack_rows(v_rows, D))
    at = 0
    for n in small:
        L, width = w_in[n].shape
        results[n] = tuple(p[at:at + L, :width] for p in (g_pack, d_pack, nm_pack, nv_pack))
        at += L

    out = [loss, grad_x[None]]
    for part in range(4):
        out += [results[n][part] for n in _WEIGHTS]
    return tuple(out)
```

```python
import functools
import math

import jax
import jax.numpy as jnp
from jax import lax
from jax.experimental import pallas as pl
from jax.experimental.pallas import tpu as pltpu

F32 = jnp.float32
BF16 = jnp.bfloat16

HEAD_DIM = 128
MLA_NOPE = 128
MLA_ROPE = 64
MLA_V = 128
MLA_QK = MLA_NOPE + MLA_ROPE
MLA_QK_PAD = 256
LANE = 128
ROPE_THETA = 10000.0
EPS = 1e-6
ADAM_LR = 0.001
ADAM_B1 = 0.9
ADAM_B2 = 0.999
ADAM_EPS = 1e-08
ADAM_WD = 0.01
ADAM_STEP = 10
N_DEV = 8
N_CHIP = 4
NEG = -1e30
VMEM_LIMIT = 56 * 1024 * 1024
ATT_BLOCK = 256
SB_BLOCK = 256
ATT_HEADS = 2
SB_PARTS = 2
MESH = pl.DeviceIdType.MESH
ANY = pl.BlockSpec(memory_space=pl.ANY)


def _params(sem):
    return pltpu.CompilerParams(dimension_semantics=sem, vmem_limit_bytes=VMEM_LIMIT)


def _tile(dim, pref):
    if dim <= pref:
        return dim
    t = pref
    while dim % t:
        t -= LANE
    return t


def _dot(a, b, dims):
    return lax.dot_general(a, b, (dims, ((), ())), preferred_element_type=F32)


NN = ((1,), (0,))
NT = ((1,), (1,))
TN = ((0,), (0,))


def _mm(a, b, *, mode="nn", out_dtype=F32, res=None, act=None, z=None, name,
        tm=1024, tn=1024, tk=2048, deps=()):
    if mode == "nn":
        (M, K), (_, N) = a.shape, b.shape
    elif mode == "nt":
        (M, K), (N, _) = a.shape, b.shape
    else:
        (K, M), (_, N) = a.shape, b.shape
    tm, tn, tk = _tile(M, tm), _tile(N, tn), _tile(K, tk)
    nk = K // tk
    if mode == "tn":
        a_spec = pl.BlockSpec((tk, tm), lambda i, j, k: (k, i))
    else:
        a_spec = pl.BlockSpec((tm, tk), lambda i, j, k: (i, k))
    if mode == "nt":
        b_spec = pl.BlockSpec((tn, tk), lambda i, j, k: (j, k))
    else:
        b_spec = pl.BlockSpec((tk, tn), lambda i, j, k: (k, j))
    dims = {"nn": NN, "nt": NT, "tn": TN}[mode]
    o_spec = pl.BlockSpec((tm, tn), lambda i, j, k: (i, j))
    in_specs, args = [a_spec, b_spec], [a, b]
    if res is not None:
        in_specs.append(o_spec)
        args.append(res)
    if act == "drelu2":
        in_specs.append(o_spec)
        args.append(z)
    if act == "relu2":
        out_shape = (jax.ShapeDtypeStruct((M, N), F32), jax.ShapeDtypeStruct((M, N), BF16))
        out_specs = (o_spec, o_spec)
    else:
        out_shape = jax.ShapeDtypeStruct((M, N), out_dtype)
        out_specs = o_spec
    has_res, has_z = res is not None, act == "drelu2"
    n_out = 2 if act == "relu2" else 1
    in_specs += [ANY] * len(deps)
    args += list(deps)

    def body(*refs):
        a_ref, b_ref = refs[0], refs[1]
        idx = 2
        res_ref = z_ref = None
        if has_res:
            res_ref = refs[idx]
            idx += 1
        if has_z:
            z_ref = refs[idx]
            idx += 1
        idx += len(deps)
        outs = refs[idx:idx + n_out]

        def finish(r):
            if has_res:
                r = r + res_ref[...]
            if act == "relu2":
                outs[0][...] = r
                rr = jnp.maximum(r, 0.0)
                outs[1][...] = (rr * rr).astype(BF16)
            elif act == "drelu2":
                outs[0][...] = (r * (2.0 * jnp.maximum(z_ref[...], 0.0))).astype(out_dtype)
            else:
                outs[0][...] = r.astype(out_dtype)

        if nk == 1:
            finish(_dot(a_ref[...], b_ref[...], dims))
            return
        acc = refs[-1]
        k = pl.program_id(2)

        @pl.when(k == 0)
        def _():
            acc[...] = _dot(a_ref[...], b_ref[...], dims)

        @pl.when((k > 0) & (k < nk - 1))
        def _():
            acc[...] += _dot(a_ref[...], b_ref[...], dims)

        @pl.when(k == nk - 1)
        def _():
            finish(acc[...] + _dot(a_ref[...], b_ref[...], dims))

    return pl.pallas_call(
        body, name=name, grid=(M // tm, N // tn, nk), in_specs=in_specs, out_specs=out_specs,
        out_shape=out_shape, scratch_shapes=[pltpu.VMEM((tm, tn), F32)] if nk > 1 else [],
        compiler_params=_params(("parallel", "parallel", "arbitrary")),
    )(*args)


def _rmsnorm_fwd(x, g, *, name, tm=256):
    T, n = x.shape
    tm = _tile(T, tm)

    def body(x_ref, g_ref, o_ref):
        xf = x_ref[...]
        r = lax.rsqrt(jnp.mean(xf * xf, axis=-1, keepdims=True) + EPS)
        o_ref[...] = (xf * r * g_ref[...]).astype(BF16)

    return pl.pallas_call(
        body, name=name, grid=(T // tm,),
        in_specs=[pl.BlockSpec((tm, n), lambda i: (i, 0)), pl.BlockSpec((1, n), lambda i: (0, 0))],
        out_specs=pl.BlockSpec((tm, n), lambda i: (i, 0)),
        out_shape=jax.ShapeDtypeStruct((T, n), BF16),
        compiler_params=_params(("parallel",)),
    )(x, g)


def _rmsnorm_bwd(x, g, dy, dx_in=None, *, name, want_f32=True, tm=256):
    T, n = x.shape
    tm = _tile(T, tm)
    has_in = dx_in is not None
    row = pl.BlockSpec((tm, n), lambda i: (i, 0))
    vec = pl.BlockSpec((1, n), lambda i: (0, 0))

    def body(*refs):
        x_ref, g_ref, dy_ref = refs[:3]
        idx = 3
        in_ref = None
        if has_in:
            in_ref = refs[idx]
            idx += 1
        outs = refs[idx:]
        xf = x_ref[...]
        r = lax.rsqrt(jnp.mean(xf * xf, axis=-1, keepdims=True) + EPS)
        dyf = dy_ref[...]
        t = dyf * g_ref[...]
        dx = r * t - xf * (r * r * r * jnp.mean(t * xf, axis=-1, keepdims=True))
        if has_in:
            dx = dx + in_ref[...]
        o = 0
        if want_f32:
            outs[0][...] = dx
            o = 1
        outs[o][...] = dx.astype(BF16)
        dg_ref = outs[o + 1]

        @pl.when(pl.program_id(0) == 0)
        def _():
            dg_ref[...] = jnp.zeros_like(dg_ref)

        dg_ref[...] += jnp.sum(dyf * xf * r, axis=0, keepdims=True)

    in_specs = [row, vec, row] + ([row] if has_in else [])
    out_specs, out_shape = [], []
    if want_f32:
        out_specs.append(row)
        out_shape.append(jax.ShapeDtypeStruct((T, n), F32))
    out_specs += [row, vec]
    out_shape += [jax.ShapeDtypeStruct((T, n), BF16), jax.ShapeDtypeStruct((1, n), F32)]
    args = [x, g, dy] + ([dx_in] if has_in else [])
    return pl.pallas_call(
        body, name=name, grid=(T // tm,), in_specs=in_specs, out_specs=tuple(out_specs),
        out_shape=tuple(out_shape), compiler_params=_params(("arbitrary",)),
    )(*args)


def _loss_head(y, target, *, tm=256):
    T, n = y.shape
    tm = _tile(T, tm)
    row = pl.BlockSpec((tm, n), lambda i: (i, 0))

    def body(y_ref, t_ref, dy_ref, dyb_ref, loss_ref):
        err = y_ref[...] - t_ref[...]
        dy = err * (1.0 / n)
        dy_ref[...] = dy
        dyb_ref[...] = dy.astype(BF16)

        @pl.when(pl.program_id(0) == 0)
        def _():
            loss_ref[...] = jnp.zeros_like(loss_ref)

        part = 0.5 * jnp.sum(jnp.mean(err * err, axis=-1, keepdims=True), axis=0, keepdims=True)
        loss_ref[...] += part

    return pl.pallas_call(
        body, name="loss_head", grid=(T // tm,), in_specs=[row, row],
        out_specs=(row, row, pl.BlockSpec((1, 1), lambda i: (0, 0))),
        out_shape=(jax.ShapeDtypeStruct((T, n), F32), jax.ShapeDtypeStruct((T, n), BF16),
                   jax.ShapeDtypeStruct((1, 1), F32)),
        compiler_params=_params(("arbitrary",)),
    )(y, target)


def _split(v, n):
    parts = []
    for _ in range(n - 1):
        p = v.astype(BF16)
        parts.append(p)
        v = v - p.astype(F32)
    parts.append(v.astype(BF16))
    return parts


def _tri_left(tri, v):
    hi, mid, lo = _split(v, 3)
    return _dot(tri, hi, NN) + _dot(tri, mid, NN) + _dot(tri, lo, NN)


def _iota2(shape, dim):
    return lax.broadcasted_iota(jnp.int32, shape, dim)


def _log_sigmoid(z):
    return jnp.minimum(z, 0.0) - jnp.log1p(jnp.exp(-jnp.abs(z)))


def _log_sigmoid_abs(z):
    return jnp.minimum(z, 0.0) - jnp.log(1.0 + jnp.exp(-jnp.abs(z)))


def _head_spec(rows, width, off):
    return pl.BlockSpec((rows, width), lambda h, i: (i, off + h))


def _full_head_spec(T, width, off):
    return pl.BlockSpec((T, width), lambda h, i: (0, off + h))


def _sb_fwd(qkv, H):
    T = qkv.shape[0]
    B = _tile(T, SB_BLOCK)
    nq = T // B
    scale = 1.0 / math.sqrt(HEAD_DIM)

    HG = ATT_HEADS if H % ATT_HEADS == 0 else 1
    HD = HEAD_DIM

    def body(q_ref, k_ref, v_ref, o_ref, c_ref, tri_s):
        i = pl.program_id(1)
        tri = (_iota2((B, B), 0) > _iota2((B, B), 1)).astype(BF16)
        for p in range(SB_PARTS):
            tri_s[pl.ds(p * B, B), :] = tri
        rows = _iota2((B, B), 0)
        cols = _iota2((B, B), 1)

        def step(n, carry):
            j = i - n
            ks = pl.multiple_of(j * B, B)
            mask = (ks + cols) < (i * B + rows)
            out = []
            for g in range(HG):
                acc, run = carry[g]
                q = q_ref[:, pl.ds(g * HD, HD)]
                kb = k_ref[pl.ds(ks, B), pl.ds(g * HD, HD)]
                vb = v_ref[pl.ds(ks, B), pl.ds(g * HD, HD)]
                z = _dot(q, kb, NT) * scale
                ls = _log_sigmoid_abs(z)
                lk = jnp.where(mask, ls - z, 0.0)
                parts = jnp.concatenate(_split(lk, SB_PARTS), axis=1)
                later = _dot(parts, tri_s[...], NN)
                a = jnp.exp(jnp.where(mask, ls, NEG) + later).astype(BF16)
                acc = acc + jnp.exp(run) * _dot(a, vb, NN)
                out.append((acc, run + jnp.sum(lk, axis=1, keepdims=True)))
            return tuple(out)

        init = tuple((jnp.zeros((B, HD), F32), jnp.zeros((B, 1), F32)) for _ in range(HG))
        res = lax.fori_loop(0, i + 1, step, init)
        for g in range(HG):
            o_ref[:, pl.ds(g * HD, HD)] = res[g][0].astype(BF16)
            c_ref[g] = res[g][1]

    return pl.pallas_call(
        body, name="sb_attn_fwd", grid=(H // HG, nq),
        in_specs=[_head_spec(B, HG * HD, 0), _full_head_spec(T, HG * HD, H // HG),
                  _full_head_spec(T, HG * HD, 2 * H // HG)],
        out_specs=(_head_spec(B, HG * HD, 0), pl.BlockSpec((HG, B, 1), lambda h, i: (h, i, 0))),
        out_shape=(jax.ShapeDtypeStruct((T, H * HD), BF16), jax.ShapeDtypeStruct((H, T, 1), F32)),
        scratch_shapes=[pltpu.VMEM((SB_PARTS * B, B), BF16)],
        compiler_params=_params(("parallel", "arbitrary")),
    )(qkv, qkv, qkv)


def _sb_bwd(qkv, do, ctot, H):
    T = qkv.shape[0]
    B = _tile(T, SB_BLOCK)
    nq = T // B
    scale = 1.0 / math.sqrt(HEAD_DIM)

    def body(q_ref, k_ref, v_ref, do_ref, c_ref, dq_ref, dk_ref, dv_ref, dk_acc, dv_acc,
             tri_incl_s, tri_strict_s):
        i = pl.program_id(1)

        @pl.when(i == 0)
        def _():
            dk_acc[...] = jnp.zeros_like(dk_acc)
            dv_acc[...] = jnp.zeros_like(dv_acc)

        q = q_ref[...]
        do_b = do_ref[...]
        ctot_b = c_ref[...]
        rows = _iota2((B, B), 0)
        cols = _iota2((B, B), 1)
        tri_incl = (rows <= cols).astype(BF16)
        tri_strict = (rows < cols).astype(BF16)
        for p in range(SB_PARTS):
            tri_incl_s[pl.ds(p * B, B), :] = tri_incl
            tri_strict_s[pl.ds(p * B, B), :] = tri_strict

        def step(j, carry):
            dq, lpre, gpre = carry
            ks = pl.multiple_of(j * B, B)
            kb = k_ref[pl.ds(ks, B), :]
            vb = v_ref[pl.ds(ks, B), :]
            z = _dot(q, kb, NT) * scale
            da = _dot(do_b, vb, NT)
            mask = (ks + cols) < (i * B + rows)
            ls = _log_sigmoid_abs(z)
            lk = jnp.where(mask, ls - z, 0.0)
            lsm = jnp.where(mask, ls, NEG)
            incl = _dot(jnp.concatenate(_split(lk, SB_PARTS), axis=1), tri_incl_s[...], NN)
            a = jnp.exp(lsm + (ctot_b - lpre) - incl)
            g = a * da
            gex = gpre + _dot(jnp.concatenate(_split(g, SB_PARTS), axis=1), tri_strict_s[...], NN)
            dzb = ((g * jnp.exp(lsm - z) - jnp.exp(lsm) * gex) * scale).astype(BF16)
            dq = dq + _dot(dzb, kb, NN)
            dk_acc[pl.ds(ks, B), :] += _dot(dzb, q, TN)
            dv_acc[pl.ds(ks, B), :] += _dot(a.astype(BF16), do_b, TN)
            return (dq, lpre + jnp.sum(lk, axis=1, keepdims=True), gpre + jnp.sum(g, axis=1, keepdims=True))

        dq, _, _ = lax.fori_loop(
            0, i + 1, step,
            (jnp.zeros((B, HEAD_DIM), F32), jnp.zeros((B, 1), F32), jnp.zeros((B, 1), F32)))
        dq_ref[...] = dq.astype(BF16)

        @pl.when(i == nq - 1)
        def _():
            dk_ref[...] = dk_acc[...].astype(BF16)
            dv_ref[...] = dv_acc[...].astype(BF16)

    W = H * HEAD_DIM
    return pl.pallas_call(
        body, name="sb_attn_bwd", grid=(H, nq),
        in_specs=[_head_spec(B, HEAD_DIM, 0), _full_head_spec(T, HEAD_DIM, H),
                  _full_head_spec(T, HEAD_DIM, 2 * H), _head_spec(B, HEAD_DIM, 0),
                  pl.BlockSpec((None, B, 1), lambda h, i: (h, i, 0))],
        out_specs=(_head_spec(B, HEAD_DIM, 0), _full_head_spec(T, HEAD_DIM, 0),
                   _full_head_spec(T, HEAD_DIM, 0)),
        out_shape=tuple(jax.ShapeDtypeStruct((T, W), BF16) for _ in range(3)),
        scratch_shapes=[pltpu.VMEM((T, HEAD_DIM), F32), pltpu.VMEM((T, HEAD_DIM), F32),
                        pltpu.VMEM((SB_PARTS * B, B), BF16), pltpu.VMEM((SB_PARTS * B, B), BF16)],
        compiler_params=_params(("parallel", "arbitrary")),
    )(qkv, qkv, qkv, do, ctot)


def _softmax_fwd(q, k, v, cf, H, dqk, scale, *, name):
    T = q.shape[0]
    B = _tile(T, ATT_BLOCK)
    nq = T // B
    has_cf = cf is not None
    HG = ATT_HEADS if H % ATT_HEADS == 0 else 1
    HD = HEAD_DIM

    def body(*refs):
        q_ref, k_ref, v_ref = refs[:3]
        idx = 3
        if has_cf:
            cfc_ref, cfr_ref = refs[3], refs[4]
            idx = 5
        o_ref, of_ref, lse_ref = refs[idx:idx + 3]
        i = pl.program_id(1)
        rows = _iota2((B, B), 0)
        cols = _iota2((B, B), 1)

        def step(j, carry):
            ks = pl.multiple_of(j * B, B)
            mask = (ks + cols) <= (i * B + rows)
            out = []
            for g in range(HG):
                m, l, acc = carry[g]
                qb = q_ref[:, pl.ds(g * dqk, dqk)]
                kb = k_ref[pl.ds(ks, B), pl.ds(g * dqk, dqk)]
                vb = v_ref[pl.ds(ks, B), pl.ds(g * HD, HD)]
                s = _dot(qb, kb, NT) * scale
                if has_cf:
                    s = s + (cfc_ref[g] - cfr_ref[g, :, pl.ds(ks, B)])
                s = jnp.where(mask, s, NEG)
                m_new = jnp.maximum(m, jnp.max(s, axis=1, keepdims=True))
                alpha = jnp.exp(m - m_new)
                p = jnp.exp(s - m_new)
                l = alpha * l + jnp.sum(p, axis=1, keepdims=True)
                acc = alpha * acc + _dot(p.astype(BF16), vb, NN)
                out.append((m_new, l, acc))
            return tuple(out)

        init = tuple((jnp.full((B, 1), NEG, F32), jnp.zeros((B, 1), F32), jnp.zeros((B, HD), F32))
                     for _ in range(HG))
        res = lax.fori_loop(0, i + 1, step, init)
        for g in range(HG):
            m, l, acc = res[g]
            o = acc / l
            o_ref[:, pl.ds(g * HD, HD)] = o.astype(BF16)
            of_ref[:, pl.ds(g * HD, HD)] = o
            lse_ref[g] = m + jnp.log(l)

    stat = pl.BlockSpec((HG, B, 1), lambda h, i: (h, i, 0))
    in_specs = [_head_spec(B, HG * dqk, 0), _full_head_spec(T, HG * dqk, 0), _full_head_spec(T, HG * HD, 0)]
    args = [q, k, v]
    if has_cf:
        in_specs += [stat, pl.BlockSpec((HG, 1, T), lambda h, i: (h, 0, 0))]
        args += list(cf)
    W = H * HD
    return pl.pallas_call(
        body, name=name, grid=(H // HG, nq), in_specs=in_specs,
        out_specs=(_head_spec(B, HG * HD, 0), _head_spec(B, HG * HD, 0), stat),
        out_shape=(jax.ShapeDtypeStruct((T, W), BF16), jax.ShapeDtypeStruct((T, W), F32),
                   jax.ShapeDtypeStruct((H, T, 1), F32)),
        compiler_params=_params(("parallel", "arbitrary")),
    )(*args)


def _softmax_bwd(q, k, v, cf, do, o, lse, H, dqk, scale, *, name):
    T = q.shape[0]
    B = _tile(T, ATT_BLOCK)
    nq = T // B
    has_cf = cf is not None
    HG = ATT_HEADS if H % ATT_HEADS == 0 else 1
    HD = HEAD_DIM

    def body(*refs):
        q_ref, k_ref, v_ref, do_ref, o_ref, lse_ref = refs[:6]
        idx = 6
        if has_cf:
            cfc_ref, cfr_ref = refs[6], refs[7]
            idx = 8
        dq_ref, dk_ref, dv_ref = refs[idx:idx + 3]
        idx += 3
        if has_cf:
            dcc_ref, dcr_ref = refs[idx], refs[idx + 1]
        i = pl.program_id(1)

        @pl.when(i == 0)
        def _():
            dk_ref[...] = jnp.zeros_like(dk_ref)
            dv_ref[...] = jnp.zeros_like(dv_ref)
            if has_cf:
                dcr_ref[...] = jnp.zeros_like(dcr_ref)

        rows = _iota2((B, B), 0)
        cols = _iota2((B, B), 1)
        heads = []
        for g in range(HG):
            do_b = do_ref[:, pl.ds(g * HD, HD)]
            delta = jnp.sum(do_b.astype(F32) * o_ref[:, pl.ds(g * HD, HD)], axis=1, keepdims=True)
            heads.append((q_ref[:, pl.ds(g * dqk, dqk)], do_b, lse_ref[g], delta))

        def step(j, carry):
            ks = pl.multiple_of(j * B, B)
            mask = (ks + cols) <= (i * B + rows)
            out = []
            for g in range(HG):
                dq, rs = carry[g]
                qb, do_b, lse_b, delta = heads[g]
                kb = k_ref[pl.ds(ks, B), pl.ds(g * dqk, dqk)]
                vb = v_ref[pl.ds(ks, B), pl.ds(g * HD, HD)]
                s = _dot(qb, kb, NT) * scale
                if has_cf:
                    s = s + (cfc_ref[g] - cfr_ref[g, :, pl.ds(ks, B)])
                p = jnp.where(mask, jnp.exp(s - lse_b), 0.0)
                dp = _dot(do_b, vb, NT)
                ds = p * (dp - delta)
                dsb = (ds * scale).astype(BF16)
                dq = dq + _dot(dsb, kb, NN)
                dk_ref[pl.ds(ks, B), pl.ds(g * dqk, dqk)] += _dot(dsb, qb, TN)
                dv_ref[pl.ds(ks, B), pl.ds(g * HD, HD)] += _dot(p.astype(BF16), do_b, TN)
                if has_cf:
                    rs = rs + jnp.sum(ds, axis=1, keepdims=True)
                    dcr_ref[g, :, pl.ds(ks, B)] -= jnp.sum(ds, axis=0, keepdims=True)
                out.append((dq, rs))
            return tuple(out)

        init = tuple((jnp.zeros((B, dqk), F32), jnp.zeros((B, 1), F32)) for _ in range(HG))
        res = lax.fori_loop(0, i + 1, step, init)
        for g in range(HG):
            dq_ref[:, pl.ds(g * dqk, dqk)] = res[g][0]
            if has_cf:
                dcc_ref[g] = res[g][1]

    stat = pl.BlockSpec((HG, B, 1), lambda h, i: (h, i, 0))
    rowstat = pl.BlockSpec((HG, 1, T), lambda h, i: (h, 0, 0))
    in_specs = [_head_spec(B, HG * dqk, 0), _full_head_spec(T, HG * dqk, 0), _full_head_spec(T, HG * HD, 0),
                _head_spec(B, HG * HD, 0), _head_spec(B, HG * HD, 0), stat]
    args = [q, k, v, do, o, lse]
    out_specs = [_head_spec(B, HG * dqk, 0), _full_head_spec(T, HG * dqk, 0), _full_head_spec(T, HG * HD, 0)]
    out_shape = [jax.ShapeDtypeStruct((T, H * dqk), F32), jax.ShapeDtypeStruct((T, H * dqk), F32),
                 jax.ShapeDtypeStruct((T, H * HD), F32)]
    if has_cf:
        in_specs += [stat, rowstat]
        args += list(cf)
        out_specs += [stat, rowstat]
        out_shape += [jax.ShapeDtypeStruct((H, T, 1), F32), jax.ShapeDtypeStruct((H, 1, T), F32)]
    return pl.pallas_call(
        body, name=name, grid=(H // HG, nq), in_specs=in_specs, out_specs=tuple(out_specs),
        out_shape=tuple(out_shape), compiler_params=_params(("parallel", "arbitrary")),
    )(*args)


def _headnorm(x, g):
    r = lax.rsqrt(jnp.mean(x * x, axis=-1, keepdims=True) + EPS)
    return x * r * g, r


def _headnorm_bwd(x, g, dy, n):
    r = lax.rsqrt(jnp.sum(x * x, axis=-1, keepdims=True) * (1.0 / n) + EPS)
    t = dy * g
    dx = r * t - x * (r * r * r * (jnp.sum(t * x, axis=-1, keepdims=True) * (1.0 / n)))
    dg = jnp.sum(dy * x * r, axis=0, keepdims=True)
    return dx, dg


def _fox_prep_fwd(qkv, gq, gk, H, *, tm=512):
    T = qkv.shape[0]
    tm = _tile(T, tm)

    def body(q_ref, k_ref, v_ref, gq_ref, gk_ref, qn_ref, kn_ref, vb_ref):
        qn_ref[...] = _headnorm(q_ref[...], gq_ref[...])[0].astype(BF16)
        kn_ref[...] = _headnorm(k_ref[...], gk_ref[...])[0].astype(BF16)
        vb_ref[...] = v_ref[...].astype(BF16)

    def blk(off):
        return pl.BlockSpec((tm, HEAD_DIM), lambda i, h: (i, off + h))

    vec = pl.BlockSpec((1, HEAD_DIM), lambda i, h: (0, 0))
    W = H * HEAD_DIM
    return pl.pallas_call(
        body, name="fox_prep_fwd", grid=(T // tm, H),
        in_specs=[blk(0), blk(H), blk(2 * H), vec, vec], out_specs=(blk(0), blk(0), blk(0)),
        out_shape=tuple(jax.ShapeDtypeStruct((T, W), BF16) for _ in range(3)),
        compiler_params=_params(("parallel", "parallel")),
    )(qkv, qkv, qkv, gq, gk)


def _fox_prep_bwd(qkv, gq, gk, dqn, dkn, dv, H, *, tm=512):
    T = qkv.shape[0]
    tm = _tile(T, tm)

    def body(q_ref, k_ref, gq_ref, gk_ref, dqn_ref, dkn_ref, dv_ref,
             dq_ref, dk_ref, dvb_ref, dgq_ref, dgk_ref):
        @pl.when((pl.program_id(0) == 0) & (pl.program_id(1) == 0))
        def _():
            dgq_ref[...] = jnp.zeros_like(dgq_ref)
            dgk_ref[...] = jnp.zeros_like(dgk_ref)

        dq, dgq = _headnorm_bwd(q_ref[...], gq_ref[...], dqn_ref[...], HEAD_DIM)
        dk, dgk = _headnorm_bwd(k_ref[...], gk_ref[...], dkn_ref[...], HEAD_DIM)
        dq_ref[...] = dq.astype(BF16)
        dk_ref[...] = dk.astype(BF16)
        dvb_ref[...] = dv_ref[...].astype(BF16)
        dgq_ref[...] += dgq
        dgk_ref[...] += dgk

    def blk(off):
        return pl.BlockSpec((tm, HEAD_DIM), lambda i, h: (i, off + h))

    vec = pl.BlockSpec((1, HEAD_DIM), lambda i, h: (0, 0))
    W = H * HEAD_DIM
    return pl.pallas_call(
        body, name="fox_prep_bwd", grid=(T // tm, H),
        in_specs=[blk(0), blk(H), vec, vec, blk(0), blk(0), blk(0)],
        out_specs=(blk(0), blk(0), blk(0), vec, vec),
        out_shape=tuple(jax.ShapeDtypeStruct((T, W), BF16) for _ in range(3))
        + (jax.ShapeDtypeStruct((1, HEAD_DIM), F32), jax.ShapeDtypeStruct((1, HEAD_DIM), F32)),
        compiler_params=_params(("arbitrary", "arbitrary")),
    )(qkv, qkv, gq, gk, dqn, dkn, dv)


def _fox_gate_fwd(flog, bf, *, tm=256):
    T = flog.shape[0]
    tm = _tile(T, tm)

    def body(f_ref, b_ref, cf_ref, carry):
        @pl.when(pl.program_id(0) == 0)
        def _():
            carry[...] = jnp.zeros_like(carry)

        lf = _log_sigmoid(f_ref[...] + b_ref[...])
        tri = (_iota2((tm, tm), 1) <= _iota2((tm, tm), 0)).astype(BF16)
        cf_ref[...] = carry[...] + _tri_left(tri, lf)
        carry[...] += jnp.sum(lf, axis=0, keepdims=True)

    return pl.pallas_call(
        body, name="fox_gate_fwd", grid=(T // tm,),
        in_specs=[pl.BlockSpec((tm, LANE), lambda i: (i, 0)), pl.BlockSpec((1, LANE), lambda i: (0, 0))],
        out_specs=pl.BlockSpec((tm, LANE), lambda i: (i, 0)),
        out_shape=jax.ShapeDtypeStruct((T, LANE), F32),
        scratch_shapes=[pltpu.VMEM((1, LANE), F32)],
        compiler_params=_params(("arbitrary",)),
    )(flog, bf)


def _fox_gate_bwd(flog, bf, dcf, *, tm=256):
    T = flog.shape[0]
    tm = _tile(T, tm)
    nt = T // tm

    def body(f_ref, b_ref, dcf_ref, df_ref, db_ref, carry):
        @pl.when(pl.program_id(0) == 0)
        def _():
            carry[...] = jnp.zeros_like(carry)
            db_ref[...] = jnp.zeros_like(db_ref)

        d = dcf_ref[...]
        tri = (_iota2((tm, tm), 1) >= _iota2((tm, tm), 0)).astype(BF16)
        dlf = carry[...] + _tri_left(tri, d)
        carry[...] += jnp.sum(d, axis=0, keepdims=True)
        xg = f_ref[...] + b_ref[...]
        e = jnp.exp(-jnp.abs(xg))
        sig_neg = jnp.where(xg >= 0.0, e, 1.0) / (1.0 + e)
        df = dlf * sig_neg
        df_ref[...] = df.astype(BF16)
        db_ref[...] += jnp.sum(df, axis=0, keepdims=True)

    rev = pl.BlockSpec((tm, LANE), lambda i: (nt - 1 - i, 0))
    vec = pl.BlockSpec((1, LANE), lambda i: (0, 0))
    return pl.pallas_call(
        body, name="fox_gate_bwd", grid=(nt,), in_specs=[rev, vec, rev], out_specs=(rev, vec),
        out_shape=(jax.ShapeDtypeStruct((T, LANE), BF16), jax.ShapeDtypeStruct((1, LANE), F32)),
        scratch_shapes=[pltpu.VMEM((1, LANE), F32)],
        compiler_params=_params(("arbitrary",)),
    )(flog, bf, dcf)


def _rope_tables(positions):
    half = MLA_ROPE // 2
    inv_freq = ROPE_THETA ** (-jnp.arange(0, half, dtype=F32) * 2.0 / MLA_ROPE)
    ang = positions.astype(F32)[:, None] * inv_freq
    cos, sin = jnp.cos(ang), jnp.sin(ang)
    zero = jnp.zeros_like(cos)
    pad = jnp.zeros((positions.shape[0], LANE - MLA_ROPE), F32)
    cos_t = jnp.concatenate([cos, cos, pad], axis=1)
    sin_up = jnp.concatenate([zero, sin, pad], axis=1)
    sin_dn = jnp.concatenate([-sin, zero, pad], axis=1)
    return cos_t, sin_up, sin_dn


def _rope(x, cos_t, sin_up, sin_dn):
    half = MLA_ROPE // 2
    return x * cos_t + pltpu.roll(x, half, 1) * sin_up + pltpu.roll(x, LANE - half, 1) * sin_dn


def _rope_t(d, cos_t, sin_up, sin_dn):
    half = MLA_ROPE // 2
    return d * cos_t + pltpu.roll(d * sin_up, LANE - half, 1) + pltpu.roll(d * sin_dn, half, 1)


def _norm192(xcat, g):
    r = lax.rsqrt(jnp.sum(xcat * xcat, axis=-1, keepdims=True) * (1.0 / MLA_QK) + EPS)
    return xcat * r * g


def _mla_prep_fwd(qfull, kv, kr, tabs, gq, gk, H, *, tm=512):
    T = qfull.shape[0]
    tm = _tile(T, tm)

    def body(q_ref, kv_ref, kr_ref, c_ref, su_ref, sd_ref, gq_ref, gk_ref, qf_ref, kf_ref, v_ref):
        tabs_b = (c_ref[...], su_ref[...], sd_ref[...])
        qb = q_ref[...]
        qcat = jnp.concatenate([qb[:, :MLA_NOPE], _rope(qb[:, MLA_NOPE:], *tabs_b)], axis=1)
        qf_ref[...] = _norm192(qcat, gq_ref[...]).astype(BF16)
        kvb = kv_ref[...]
        kcat = jnp.concatenate([kvb[:, :MLA_NOPE], _rope(kr_ref[...], *tabs_b)], axis=1)
        kf_ref[...] = _norm192(kcat, gk_ref[...]).astype(BF16)
        v_ref[...] = kvb[:, MLA_NOPE:].astype(BF16)

    head = pl.BlockSpec((tm, MLA_QK_PAD), lambda i, h: (i, h))
    tok = pl.BlockSpec((tm, LANE), lambda i, h: (i, 0))
    vec = pl.BlockSpec((1, MLA_QK_PAD), lambda i, h: (0, 0))
    return pl.pallas_call(
        body, name="mla_prep_fwd", grid=(T // tm, H),
        in_specs=[head, head, tok, tok, tok, tok, vec, vec],
        out_specs=(head, head, pl.BlockSpec((tm, MLA_V), lambda i, h: (i, h))),
        out_shape=(jax.ShapeDtypeStruct((T, H * MLA_QK_PAD), BF16),
                   jax.ShapeDtypeStruct((T, H * MLA_QK_PAD), BF16),
                   jax.ShapeDtypeStruct((T, H * MLA_V), BF16)),
        compiler_params=_params(("parallel", "parallel")),
    )(qfull, kv, kr, *tabs, gq, gk)


def _mla_prep_bwd(qfull, kv, kr, tabs, gq, gk, dqf, dkf, dv, H, *, tm=512):
    T = qfull.shape[0]
    tm = _tile(T, tm)

    def body(q_ref, kv_ref, kr_ref, c_ref, su_ref, sd_ref, gq_ref, gk_ref, dqf_ref, dkf_ref, dv_ref,
             dq_ref, dkv_ref, dkr_ref, dgq_ref, dgk_ref, kr_acc):
        h = pl.program_id(1)

        @pl.when((pl.program_id(0) == 0) & (h == 0))
        def _():
            dgq_ref[...] = jnp.zeros_like(dgq_ref)
            dgk_ref[...] = jnp.zeros_like(dgk_ref)

        @pl.when(h == 0)
        def _():
            kr_acc[...] = jnp.zeros_like(kr_acc)

        tabs_b = (c_ref[...], su_ref[...], sd_ref[...])
        qb = q_ref[...]
        qcat = jnp.concatenate([qb[:, :MLA_NOPE], _rope(qb[:, MLA_NOPE:], *tabs_b)], axis=1)
        dqcat, dgq = _headnorm_bwd(qcat, gq_ref[...], dqf_ref[...], MLA_QK)
        dq_ref[...] = jnp.concatenate(
            [dqcat[:, :MLA_NOPE], _rope_t(dqcat[:, MLA_NOPE:], *tabs_b)], axis=1).astype(BF16)
        dgq_ref[...] += dgq
        kvb = kv_ref[...]
        kcat = jnp.concatenate([kvb[:, :MLA_NOPE], _rope(kr_ref[...], *tabs_b)], axis=1)
        dkcat, dgk = _headnorm_bwd(kcat, gk_ref[...], dkf_ref[...], MLA_QK)
        dkv_ref[...] = jnp.concatenate([dkcat[:, :MLA_NOPE], dv_ref[...]], axis=1).astype(BF16)
        dgk_ref[...] += dgk
        kr_acc[...] += dkcat[:, MLA_NOPE:]

        @pl.when(h == H - 1)
        def _():
            dkr_ref[...] = _rope_t(kr_acc[...], *tabs_b).astype(BF16)

    head = pl.BlockSpec((tm, MLA_QK_PAD), lambda i, h: (i, h))
    tok = pl.BlockSpec((tm, LANE), lambda i, h: (i, 0))
    vec = pl.BlockSpec((1, MLA_QK_PAD), lambda i, h: (0, 0))
    return pl.pallas_call(
        body, name="mla_prep_bwd", grid=(T // tm, H),
        in_specs=[head, head, tok, tok, tok, tok, vec, vec, head, head,
                  pl.BlockSpec((tm, MLA_V), lambda i, h: (i, h))],
        out_specs=(head, head, tok, vec, vec),
        out_shape=(jax.ShapeDtypeStruct((T, H * MLA_QK_PAD), BF16),
                   jax.ShapeDtypeStruct((T, H * MLA_QK_PAD), BF16),
                   jax.ShapeDtypeStruct((T, LANE), BF16),
                   jax.ShapeDtypeStruct((1, MLA_QK_PAD), F32), jax.ShapeDtypeStruct((1, MLA_QK_PAD), F32)),
        scratch_shapes=[pltpu.VMEM((tm, LANE), F32)],
        compiler_params=_params(("arbitrary", "arbitrary")),
    )(qfull, kv, kr, *tabs, gq, gk, dqf, dkf, dv)


def _cf_layouts(cf, H):
    cfh = cf[:, :H].T
    return cfh[:, :, None], cfh[:, None, :]


def _layer_fwd(i, x, P, tabs, H, hook=None):
    kind, j = i % 3, i // 3
    s = {"x_in": x}
    h = _rmsnorm_fwd(x, P["mix_norm"][i:i + 1], name="mix_norm_fwd")
    s["h"] = h
    if kind == 0:
        w = P["sb"][j]
        qkv = _mm(h, w["w_in"], out_dtype=BF16, name="sb_qkv")
        o, ctot = _sb_fwd(qkv, H)
        s.update(qkv=qkv, o=o, ctot=ctot)
    elif kind == 1:
        w = P["fox"][j]
        qkv = _mm(h, w["w_qkv"], name="fox_qkv")
        flog = _mm(h, w["w_f"], name="fox_gate_proj")
        qn, kn, vb = _fox_prep_fwd(qkv, w["gq"], w["gk"], H)
        cf = _cf_layouts(_fox_gate_fwd(flog, w["bf"]), H)
        o, of, lse = _softmax_fwd(qn, kn, vb, cf, H, HEAD_DIM, 1.0 / math.sqrt(HEAD_DIM), name="fox_attn_fwd")
        s.update(qkv=qkv, flog=flog, qn=qn, kn=kn, vb=vb, cf=cf, o=o, of=of, lse=lse)
    else:
        w = P["mla"][j]
        dq = _mm(h, w["w_dq"], name="mla_down_q")
        dkv = _mm(h, w["w_dkv"], name="mla_down_kv")
        kr = _mm(h, w["w_dr"], name="mla_down_rope")
        cq = _rmsnorm_fwd(dq, w["q_norm"], name="mla_q_norm_fwd")
        ckv = _rmsnorm_fwd(dkv, w["kv_norm"], name="mla_kv_norm_fwd")
        qfull = _mm(cq, w["w_uq"], name="mla_up_q")
        kv = _mm(ckv, w["w_ukv"], name="mla_up_kv")
        qf, kf, v = _mla_prep_fwd(qfull, kv, kr, tabs, w["gq"], w["gk"], H)
        o, of, lse = _softmax_fwd(qf, kf, v, None, H, MLA_QK_PAD, 1.0 / math.sqrt(MLA_QK), name="mla_attn_fwd")
        s.update(dq=dq, dkv=dkv, kr=kr, cq=cq, ckv=ckv, qfull=qfull, kv=kv, qf=qf, kf=kf, v=v,
                 o=o, of=of, lse=lse)
    x = _mm(s["o"], w["w_out"], res=x, name="mixer_out")
    s["x_mid"] = x
    deps = hook(x) if hook is not None else ()
    h2 = _rmsnorm_fwd(x, P["mlp_norm"][i:i + 1], name="mlp_norm_fwd")
    z, u = _mm(h2, P["mlp"][i]["w1"], act="relu2", name="mlp_up", deps=deps)
    x = _mm(u, P["mlp"][i]["w2"], res=x, name="mlp_down")
    s.update(h2=h2, z=z, u=u)
    return x, s


def _layer_bwd(i, dx, dxb, s, P, tabs, H, deps=(), hook=None):
    kind, j = i % 3, i // 3
    G = {}
    wm = P["mlp"][i]
    dz = _mm(dxb, wm["w2"], mode="nt", act="drelu2", z=s["z"], out_dtype=BF16, name="mlp_down_bwd",
             deps=deps)
    G["w2"] = _mm(s["u"], dxb, mode="tn", out_dtype=BF16, name="mlp_w2_grad")
    G["w1"] = _mm(s["h2"], dz, mode="tn", out_dtype=BF16, name="mlp_w1_grad")
    dh2 = _mm(dz, wm["w1"], mode="nt", name="mlp_up_bwd")
    dx, dxb, G["mlp_norm"] = _rmsnorm_bwd(s["x_mid"], P["mlp_norm"][i:i + 1], dh2, dx, name="mlp_norm_bwd")
    h = s["h"]
    w = P[("sb", "fox", "mla")[kind]][j]
    do = _mm(dxb, w["w_out"], mode="nt", out_dtype=BF16, name="mixer_out_bwd",
             deps=hook(dxb, G) if hook is not None else ())
    if kind == 0:
        G["w_out"] = _mm(s["o"], dxb, mode="tn", out_dtype=BF16, name="mixer_out_grad")
        dq, dk, dv = _sb_bwd(s["qkv"], do, s["ctot"], H)
        dproj = jnp.concatenate([dq, dk, dv], axis=1)
        G["w_in"] = _mm(h, dproj, mode="tn", out_dtype=BF16, name="sb_qkv_grad")
        dh = _mm(dproj, w["w_in"], mode="nt", name="sb_qkv_bwd")
    elif kind == 1:
        G["w_out"] = _mm(s["o"], dxb, mode="tn", out_dtype=BF16, name="mixer_out_grad")
        dqn, dkn, dv, dcc, dcr = _softmax_bwd(
            s["qn"], s["kn"], s["vb"], s["cf"], do, s["of"], s["lse"], H, HEAD_DIM,
            1.0 / math.sqrt(HEAD_DIM), name="fox_attn_bwd")
        dq, dk, dvb, G["gq"], G["gk"] = _fox_prep_bwd(s["qkv"], w["gq"], w["gk"], dqn, dkn, dv, H)
        dcf = (dcc[:, :, 0] + dcr[:, 0, :]).T
        dcf = jnp.pad(dcf, ((0, 0), (0, LANE - H)))
        dflog, G["bf"] = _fox_gate_bwd(s["flog"], w["bf"], dcf)
        dproj = jnp.concatenate([dq, dk, dvb], axis=1)
        G["w_qkv"] = _mm(h, dproj, mode="tn", out_dtype=BF16, name="fox_qkv_grad")
        G["w_f"] = _mm(h, dflog, mode="tn", out_dtype=BF16, name="fox_gate_grad")
        dh = _mm(dproj, w["w_qkv"], mode="nt", name="fox_qkv_bwd")
        dh = _mm(dflog, w["w_f"], mode="nt", res=dh, name="fox_gate_bwd_proj")
    else:
        G["w_out"] = _mm(s["o"], dxb, mode="tn", out_dtype=BF16, name="mixer_out_grad")
        dqf, dkf, dv = _softmax_bwd(
            s["qf"], s["kf"], s["v"], None, do, s["of"], s["lse"], H, MLA_QK_PAD,
            1.0 / math.sqrt(MLA_QK), name="mla_attn_bwd")
        dqfull, dkv, dkr, G["gq"], G["gk"] = _mla_prep_bwd(
            s["qfull"], s["kv"], s["kr"], tabs, w["gq"], w["gk"], dqf, dkf, dv, H)
        G["w_uq"] = _mm(s["cq"], dqfull, mode="tn", out_dtype=BF16, name="mla_up_q_grad")
        G["w_ukv"] = _mm(s["ckv"], dkv, mode="tn", out_dtype=BF16, name="mla_up_kv_grad")
        dcq = _mm(dqfull, w["w_uq"], mode="nt", name="mla_up_q_bwd")
        dckv = _mm(dkv, w["w_ukv"], mode="nt", name="mla_up_kv_bwd")
        ddq, G["q_norm"] = _rmsnorm_bwd(s["dq"], w["q_norm"], dcq, want_f32=False, name="mla_q_norm_bwd")
        ddkv, G["kv_norm"] = _rmsnorm_bwd(s["dkv"], w["kv_norm"], dckv, want_f32=False, name="mla_kv_norm_bwd")
        G["w_dq"] = _mm(h, ddq, mode="tn", out_dtype=BF16, name="mla_down_q_grad")
        G["w_dkv"] = _mm(h, ddkv, mode="tn", out_dtype=BF16, name="mla_down_kv_grad")
        G["w_dr"] = _mm(h, dkr, mode="tn", out_dtype=BF16, name="mla_down_rope_grad")
        dh = _mm(ddq, w["w_dq"], mode="nt", name="mla_down_q_bwd")
        dh = _mm(ddkv, w["w_dkv"], mode="nt", res=dh, name="mla_down_kv_bwd")
        dh = _mm(dkr, w["w_dr"], mode="nt", res=dh, name="mla_down_rope_bwd")
    dx, dxb, G["mix_norm"] = _rmsnorm_bwd(s["x_in"], P["mix_norm"][i:i + 1], dh, dx, name="mix_norm_bwd")
    return dx, dxb, G


def _local_step(x, positions, loss_target, P, depth, H):
    tabs = _rope_tables(positions)
    saved = []
    for i in range(depth):
        x, s = _layer_fwd(i, x, P, tabs, H)
        saved.append(s)
    dx, dxb, loss = _loss_head(x, loss_target)
    grads = [None] * depth
    for i in reversed(range(depth)):
        dx, dxb, grads[i] = _layer_bwd(i, dx, dxb, saved[i], P, tabs, H)
    return loss, dx, grads


def _pad_cols(a, n):
    return jnp.pad(a, ((0, 0), (0, n - a.shape[1])))


def _prepare_layer(P, full, i, H, parts=("mix", "mlp")):
    W = H * HEAD_DIM
    kind, j = i % 3, i // 3
    if "mlp" in parts:
        P["mlp"][i] = {"w1": full["mlp_w1"][i], "w2": full["mlp_w2"][i]}
    if "mix" not in parts:
        return
    if kind == 0:
        P["sb"][j] = {"w_in": full["sb_w_in"][j], "w_out": full["sb_w_out"][j]}
    elif kind == 1:
        w = full["fox_w_in"][j]
        P["fox"][j] = {
            "w_qkv": w[:, :3 * W], "w_f": _pad_cols(w[:, 3 * W:], LANE),
            "bf": _pad_cols(full["fox_b_f"][j:j + 1], LANE),
            "gq": full["fox_q_gain"][j:j + 1], "gk": full["fox_k_gain"][j:j + 1],
            "w_out": full["fox_w_out"][j]}
    else:
        w = full["mla_w_in"][j]
        q_norm, kv_norm = full["mla_q_norm"][j], full["mla_kv_norm"][j]
        rq, rkv = q_norm.shape[0], kv_norm.shape[0]
        w_uq = full["mla_w_uq"][j].reshape(rq, H, MLA_QK)
        w_uq = jnp.pad(w_uq, ((0, 0), (0, 0), (0, MLA_QK_PAD - MLA_QK))).reshape(rq, H * MLA_QK_PAD)
        P["mla"][j] = {
            "w_dq": w[:, :rq], "w_dkv": w[:, rq:rq + rkv], "w_dr": _pad_cols(w[:, rq + rkv:], LANE),
            "q_norm": q_norm[None], "kv_norm": kv_norm[None],
            "w_uq": w_uq, "w_ukv": full["mla_w_ukv"][j],
            "gq": _pad_cols(full["mla_q_gain"][j:j + 1], MLA_QK_PAD),
            "gk": _pad_cols(full["mla_k_gain"][j:j + 1], MLA_QK_PAD),
            "w_out": full["mla_w_out"][j]}


def _prepare(full, H):
    depth = len(full["mlp_w1"])
    P = {"mix_norm": full["mix_norm"], "mlp_norm": full["mlp_norm"], "mlp": [None] * depth,
         "sb": [None] * len(full["sb_w_in"]), "fox": [None] * len(full["fox_w_in"]),
         "mla": [None] * len(full["mla_w_in"])}
    for i in range(depth):
        _prepare_layer(P, full, i, H)
    return P


def _layer_grads(i, G, H):
    kind = i % 3
    out = {"mix_norm": G["mix_norm"], "mlp_norm": G["mlp_norm"], "mlp_w1": G["w1"], "mlp_w2": G["w2"]}
    if kind == 0:
        out.update(sb_w_in=G["w_in"], sb_w_out=G["w_out"])
    elif kind == 1:
        out.update(fox_w_in=jnp.concatenate([G["w_qkv"], G["w_f"][:, :H]], axis=1), fox_b_f=G["bf"][:, :H],
                   fox_q_gain=G["gq"], fox_k_gain=G["gk"], fox_w_out=G["w_out"])
    else:
        rq = G["w_uq"].shape[0]
        out.update(
            mla_w_in=jnp.concatenate([G["w_dq"], G["w_dkv"], G["w_dr"][:, :MLA_ROPE]], axis=1),
            mla_q_norm=G["q_norm"], mla_kv_norm=G["kv_norm"],
            mla_w_uq=G["w_uq"].reshape(rq, H, MLA_QK_PAD)[:, :, :MLA_QK].reshape(rq, H * MLA_QK),
            mla_w_ukv=G["w_ukv"], mla_q_gain=G["gq"][:, :MLA_QK], mla_k_gain=G["gk"][:, :MLA_QK],
            mla_w_out=G["w_out"])
    return out


def _layer_of(name, j):
    if name.startswith("sb_"):
        return 3 * j
    if name.startswith("fox_"):
        return 3 * j + 1
    if name.startswith("mla_"):
        return 3 * j + 2
    return j


def _unprepare(grads, H):
    out = {}
    for i, G in enumerate(grads):
        for n, g in _layer_grads(i, G, H).items():
            out.setdefault(n, []).append(g)
    return out


class _Place:
    def __init__(self, x, y, c):
        self.x, self.y, self.c = x, y, c
        self.dev = 4 * x + 2 * y + c
        self.chip = 2 * x + y
        self.id = (x, y, c)


def _peers(me, kind):
    if kind == "ici":
        return [_Place(1 - me.x, me.y, me.c), _Place(me.x, 1 - me.y, me.c), _Place(1 - me.x, 1 - me.y, me.c)]
    return [_Place(me.x, me.y, 1 - me.c)]


def _exchange(name, kind, operands, out_shapes, aliases, n_remote, n_local, plan):
    n_in, n_out = len(operands), len(out_shapes)

    def body(*refs):
        in_refs, out_refs = refs[:n_in], refs[n_in:n_in + n_out]
        send_sems, recv_sems, local_sems = refs[n_in + n_out:]
        me = _Place(lax.axis_index("x"), lax.axis_index("y"), lax.axis_index("c"))
        peers = _peers(me, kind)
        remote, local = plan(me, peers, in_refs, out_refs)
        assert len(remote) == n_remote and len(local) == n_local
        sends = []
        for n, (src, dst, k, _) in enumerate(remote):
            cp = pltpu.make_async_remote_copy(
                src_ref=src, dst_ref=dst, send_sem=send_sems.at[n], recv_sem=recv_sems.at[n],
                device_id=peers[k].id, device_id_type=MESH)
            cp.start()
            sends.append(cp)
        own = []
        for n, (src, dst) in enumerate(local):
            cp = pltpu.make_async_copy(src, dst, local_sems.at[n])
            cp.start()
            own.append(cp)
        for n, (src, _, k, landing) in enumerate(remote):
            pltpu.make_async_remote_copy(
                src_ref=src, dst_ref=landing, send_sem=send_sems.at[n], recv_sem=recv_sems.at[n],
                device_id=peers[k].id, device_id_type=MESH).wait_recv()
        for cp in sends:
            cp.wait_send()
        for cp in own:
            cp.wait()

    outs = pl.pallas_call(
        body, name=name, in_specs=[ANY] * n_in, out_specs=tuple([ANY] * n_out),
        out_shape=tuple(out_shapes), input_output_aliases=aliases,
        scratch_shapes=[pltpu.SemaphoreType.DMA((n_remote,)), pltpu.SemaphoreType.DMA((n_remote,)),
                        pltpu.SemaphoreType.DMA((max(n_local, 1),))],
    )(*operands)
    return list(outs)


def _window(ref, kind, d, shard_shape):
    r, c = shard_shape
    if kind == "col":
        return ref.at[:, pl.ds(pl.multiple_of(d * c, c), c)]
    return ref.at[pl.ds(pl.multiple_of(d * r, r), r), :]


def _full_shape(kind, shard_shape):
    r, c = shard_shape
    return (r, N_DEV * c) if kind == "col" else (N_DEV * r, c)


HBM = pl.BlockSpec(memory_space=pltpu.HBM)
SEM = pl.BlockSpec(memory_space=pltpu.SEMAPHORE)
EFFECT = pltpu.SideEffectType.DATAFLOW_SIDE_EFFECTING


class _Flight:
    def __init__(self, name, kind, n_remote, plan, send_sems, recv_sems, bufs, token):
        self.name, self.kind, self.n_remote, self.plan = name, kind, n_remote, plan
        self.send_sems, self.recv_sems, self.bufs, self.token = send_sems, recv_sems, bufs, token


def _start_copies(name, kind, bufs, n_remote, plan, after=None):
    nb = len(bufs)
    n_after = 0 if after is None else 1

    def body(*refs):
        in_refs = refs[:nb]
        send_sems, recv_sems = refs[nb + n_after], refs[nb + n_after + 1]
        token = refs[2 * nb + n_after + 2]
        me = _Place(lax.axis_index("x"), lax.axis_index("y"), lax.axis_index("c"))
        peers = _peers(me, kind)
        for n, (src, dst, k, _) in enumerate(plan(me, peers, in_refs)):
            pltpu.make_async_remote_copy(
                src_ref=src, dst_ref=dst, send_sem=send_sems.at[n], recv_sem=recv_sems.at[n],
                device_id=peers[k].id, device_id_type=MESH).start()
        token[...] = jnp.zeros_like(token)

    outs = pl.pallas_call(
        body, name=name, in_specs=[HBM] * nb + [ANY] * n_after,
        out_specs=(SEM, SEM) + (HBM,) * nb + (pl.BlockSpec(memory_space=pltpu.VMEM),),
        out_shape=(pltpu.SemaphoreType.DMA((n_remote,)), pltpu.SemaphoreType.DMA((n_remote,)))
        + tuple(pltpu.HBM(b.shape, b.dtype) for b in bufs) + (jax.ShapeDtypeStruct((8, LANE), F32),),
        input_output_aliases={a: 2 + a for a in range(nb)},
        compiler_params=pltpu.CompilerParams(has_side_effects=EFFECT),
    )(*[pltpu.with_memory_space_constraint(b, pltpu.HBM) for b in bufs], *([after] if n_after else []))
    return _Flight(name, kind, n_remote, plan, outs[0], outs[1], list(outs[2:2 + nb]), outs[2 + nb])


def _wait_copies(flight, after):
    nb = len(flight.bufs)
    plan, kind = flight.plan, flight.kind

    def body(*refs):
        in_refs = refs[:nb]
        send_sems, recv_sems = refs[nb], refs[nb + 1]
        me = _Place(lax.axis_index("x"), lax.axis_index("y"), lax.axis_index("c"))
        peers = _peers(me, kind)
        for n, (src, _, k, landing) in enumerate(plan(me, peers, in_refs)):
            cp = pltpu.make_async_remote_copy(
                src_ref=src, dst_ref=landing, send_sem=send_sems.at[n], recv_sem=recv_sems.at[n],
                device_id=peers[k].id, device_id_type=MESH)
            cp.wait_send()
            cp.wait_recv()

    outs = pl.pallas_call(
        body, name=flight.name.replace("_start", "_wait"),
        in_specs=[HBM] * nb + [SEM, SEM, ANY], out_specs=(HBM,) * nb,
        out_shape=tuple(pltpu.HBM(b.shape, b.dtype) for b in flight.bufs),
        input_output_aliases={a: a for a in range(nb)},
        compiler_params=pltpu.CompilerParams(has_side_effects=EFFECT),
    )(*flight.bufs, flight.send_sems, flight.recv_sems, after)
    return list(outs)


def _own_in_place(shard, kind, dev, dtype, deps=()):
    r, c = shard.shape
    tr = _row_tile(r, c)
    nb = r // tr

    def body(dev_ref, s_ref, *rest):
        o_ref = rest[-1]
        o_ref[...] = s_ref[...].astype(dtype)

    if kind == "col":
        o_spec = pl.BlockSpec((tr, c), lambda i, dev_ref: (i, dev_ref[0]))
    else:
        o_spec = pl.BlockSpec((tr, c), lambda i, dev_ref: (dev_ref[0] * nb + i, 0))
    return pl.pallas_call(
        body, name="own_in_place",
        grid_spec=pltpu.PrefetchScalarGridSpec(
            num_scalar_prefetch=1, grid=(nb,),
            in_specs=[pl.BlockSpec((tr, c), lambda i, dev_ref: (i, 0))] + [ANY] * len(deps), out_specs=o_spec),
        out_shape=jax.ShapeDtypeStruct(_full_shape(kind, shard.shape), dtype),
        compiler_params=_params(("parallel",)),
    )(dev, shard, *deps)


def _gather_ici_start(fulls, kinds, shapes, tag, after=None):
    n = len(fulls)

    def plan(me, peers, refs):
        remote = []
        for a in range(n):
            mine = _window(refs[a], kinds[a], me.dev, shapes[a])
            for k, p in enumerate(peers):
                remote.append((mine, mine, k, _window(refs[a], kinds[a], p.dev, shapes[a])))
        return remote

    return _start_copies("gather_ici_start_" + tag, "ici", list(fulls), 3 * n, plan, after)


def _gather_d2d_start(fulls, kinds, shapes, tag):
    n = len(fulls)

    def plan(me, peers, refs):
        remote = []
        for a in range(n):
            for ch in range(N_CHIP):
                held = _window(refs[a], kinds[a], 2 * ch + me.c, shapes[a])
                remote.append((held, held, 0, _window(refs[a], kinds[a], 2 * ch + 1 - me.c, shapes[a])))
        return remote

    return _start_copies("gather_d2d_start_" + tag, "d2d", fulls, N_CHIP * n, plan)


def _scatter_d2d_start(grads, kinds, shapes, tag):
    n = len(grads)
    lands = [lax.empty((N_CHIP,) + tuple(s), g.dtype) for g, s in zip(grads, shapes)]

    def plan(me, peers, refs):
        remote = []
        for a in range(n):
            for ch in range(N_CHIP):
                remote.append((_window(refs[a], kinds[a], 2 * ch + 1 - me.c, shapes[a]),
                               refs[n + a].at[ch], 0, refs[n + a].at[ch]))
        return remote

    return _start_copies("scatter_d2d_start_" + tag, "d2d", list(grads) + lands, N_CHIP * n, plan)


def _scatter_ici_start(sums, shapes, tag, after=None):
    n = len(sums)
    lands = [lax.empty((N_CHIP - 1,) + tuple(s), BF16) for s in shapes]

    def plan(me, peers, refs):
        remote = []
        for a in range(n):
            for k, p in enumerate(peers):
                remote.append((refs[a].at[p.chip], refs[n + a].at[k], k, refs[n + a].at[k]))
        return remote

    return _start_copies("scatter_ici_start_" + tag, "ici", list(sums) + lands, 3 * n, plan, after)


def _pair_add(g, r4, kind, shard_shape, core, *, name):
    r, c = shard_shape
    tr = _row_tile(r, c, 524288)
    nb = r // tr
    if kind == "all":
        def body_all(core_ref, g_ref, r_ref, o_ref):
            o_ref[...] = g_ref[...] + r_ref[...]

        spec = pl.BlockSpec((tr, c), lambda i, core_ref: (i, 0))
        return pl.pallas_call(
            body_all, name=name,
            grid_spec=pltpu.PrefetchScalarGridSpec(
                num_scalar_prefetch=1, grid=(nb,), in_specs=[spec, spec], out_specs=spec),
            out_shape=jax.ShapeDtypeStruct((r, c), F32),
            compiler_params=_params(("parallel",)),
        )(core, g, r4)

    def body(core_ref, g_ref, r_ref, o_ref):
        o_ref[...] = (g_ref[...].astype(F32) + r_ref[...].astype(F32)).astype(BF16)

    if kind == "col":
        g_spec = pl.BlockSpec((tr, c), lambda ch, i, core_ref: (i, 2 * ch + core_ref[0]))
    else:
        g_spec = pl.BlockSpec((tr, c), lambda ch, i, core_ref: ((2 * ch + core_ref[0]) * nb + i, 0))
    slot = pl.BlockSpec((None, tr, c), lambda ch, i, core_ref: (ch, i, 0))
    return pl.pallas_call(
        body, name=name,
        grid_spec=pltpu.PrefetchScalarGridSpec(
            num_scalar_prefetch=1, grid=(N_CHIP, nb), in_specs=[g_spec, slot], out_specs=slot),
        out_shape=jax.ShapeDtypeStruct((N_CHIP, r, c), BF16),
        compiler_params=_params(("parallel", "parallel")),
    )(core, g, r4)


def _reduce_scatter(grads, kinds, shard_shapes, core, tag):
    n = len(grads)
    land = []
    for g, k, s in zip(grads, kinds, shard_shapes):
        land.append(jax.ShapeDtypeStruct(s if k == "all" else (N_CHIP,) + tuple(s), g.dtype))
    n_rem = sum(1 if k == "all" else N_CHIP for k in kinds)

    def plan_d2d(me, peers, ins, outs):
        remote = []
        for a in range(n):
            if kinds[a] == "all":
                remote.append((ins[a], outs[a], 0, outs[a]))
                continue
            for ch in range(N_CHIP):
                remote.append((_window(ins[a], kinds[a], 2 * ch + 1 - me.c, shard_shapes[a]),
                               outs[a].at[ch], 0, outs[a].at[ch]))
        return remote, []

    got = _exchange("scatter_d2d_" + tag, "d2d", list(grads), land, {}, n_rem, 0, plan_d2d)
    sums = [_pair_add(g, r4, k, s, core, name="pair_add_" + tag)
            for g, r4, k, s in zip(grads, got, kinds, shard_shapes)]
    land2 = [jax.ShapeDtypeStruct((N_CHIP,) + tuple(s), F32 if k == "all" else BF16)
             for k, s in zip(kinds, shard_shapes)]

    def plan_ici(me, peers, ins, outs):
        remote, local = [], []
        for a in range(n):
            for k, p in enumerate(peers):
                src = ins[a] if kinds[a] == "all" else ins[a].at[p.chip]
                remote.append((src, outs[a].at[me.chip], k, outs[a].at[p.chip]))
            src = ins[a] if kinds[a] == "all" else ins[a].at[me.chip]
            local.append((src, outs[a].at[me.chip]))
        return remote, local

    return _exchange("scatter_ici_" + tag, "ici", sums, land2, {}, 3 * n, n, plan_ici)


def _row_tile(r, c, limit=262144):
    best = None
    for t in range(8, r + 1, 8):
        if r % t == 0 and t * c <= limit:
            best = t
    return best if best is not None else r


def _adam_math(w, g, m, v):
    m = ADAM_B1 * m + (1.0 - ADAM_B1) * g
    v = ADAM_B2 * v + (1.0 - ADAM_B2) * (g * g)
    m_hat = m / (1.0 - ADAM_B1 ** ADAM_STEP)
    v_hat = v / (1.0 - ADAM_B2 ** ADAM_STEP)
    delta = -ADAM_LR * (m_hat / (jnp.sqrt(v_hat) + ADAM_EPS) + ADAM_WD * w)
    return delta, m, v


def _adamw(w, m, v, sums, lands, chip, layer, prev, *, name):
    L, r, c = w.shape
    tr = _row_tile(r, c)
    has_prev = prev is not None

    def body(chip_ref, *refs):
        w_ref, m_ref, v_ref, s_ref, l_ref = refs[:5]
        g_ref, d_ref, nm_ref, nv_ref = refs[-4:]
        g = (s_ref[...].astype(F32) + l_ref[0].astype(F32)) + l_ref[1].astype(F32) + l_ref[2].astype(F32)
        delta, nm, nv = _adam_math(w_ref[...], g, m_ref[...], v_ref[...])
        g_ref[...] = g
        d_ref[...] = delta
        nm_ref[...] = nm
        nv_ref[...] = nv

    slab = pl.BlockSpec((None, tr, c), lambda i, chip_ref: (layer, i, 0))
    in_specs = [slab, slab, slab, pl.BlockSpec((None, tr, c), lambda i, chip_ref: (chip_ref[0], i, 0)),
                pl.BlockSpec((N_CHIP - 1, tr, c), lambda i, chip_ref: (0, i, 0))]
    args = [chip, w, m, v, sums, lands]
    aliases = {}
    if has_prev:
        in_specs += [ANY] * 4
        args += list(prev)
        aliases = {6 + n: n for n in range(4)}
    return pl.pallas_call(
        body, name=name,
        grid_spec=pltpu.PrefetchScalarGridSpec(
            num_scalar_prefetch=1, grid=(r // tr,), in_specs=in_specs, out_specs=(slab,) * 4),
        out_shape=tuple(jax.ShapeDtypeStruct((L, r, c), F32) for _ in range(4)),
        input_output_aliases=aliases, compiler_params=_params(("parallel",)),
    )(*args)


def _adamw_small(w, g, m, v):
    def body(w_ref, g_ref, m_ref, v_ref, d_ref, nm_ref, nv_ref):
        d_ref[...], nm_ref[...], nv_ref[...] = _adam_math(w_ref[...], g_ref[...], m_ref[...], v_ref[...])

    return pl.pallas_call(
        body, name="adamw_small", out_shape=tuple(jax.ShapeDtypeStruct(w.shape, F32) for _ in range(3)),
    )(w, g, m, v)


def _sum4(parts):
    def body(p_ref, o_ref):
        o_ref[...] = (p_ref[0] + p_ref[1]) + p_ref[2] + p_ref[3]

    return pl.pallas_call(
        body, name="sum_chips", out_shape=jax.ShapeDtypeStruct(parts.shape[1:], F32))(parts)


_WEIGHTS = ["mix_norm", "mlp_norm", "sb_w_in", "sb_w_out", "fox_w_in", "fox_b_f", "fox_q_gain", "fox_k_gain",
            "fox_w_out", "mla_w_in", "mla_q_norm", "mla_kv_norm", "mla_w_uq", "mla_w_ukv", "mla_q_gain",
            "mla_k_gain", "mla_w_out", "mlp_w1", "mlp_w2"]
_BIG = {"sb_w_in": "col", "sb_w_out": "row", "fox_w_in": "row", "fox_w_out": "row", "mla_w_in": "row",
        "mla_w_uq": "col", "mla_w_ukv": "col", "mla_w_out": "row", "mlp_w1": "col", "mlp_w2": "row"}


def _layer_big(i):
    kind, j = i % 3, i // 3
    mixer = {0: ["sb_w_in", "sb_w_out"], 1: ["fox_w_in", "fox_w_out"],
             2: ["mla_w_in", "mla_w_uq", "mla_w_ukv", "mla_w_out"]}[kind]
    return [(n, j) for n in mixer] + [("mlp_w1", i), ("mlp_w2", i)]


def _stack_to_cols(a):
    r = a.shape[0] // N_DEV
    return a.reshape(N_DEV, r, a.shape[1]).transpose(1, 0, 2).reshape(r, N_DEV * a.shape[1])


def _cols_to_stack(a):
    c = a.shape[1] // N_DEV
    return a.reshape(a.shape[0], N_DEV, c).transpose(1, 0, 2).reshape(N_DEV * a.shape[0], c)


def _pack_rows(rows, width):
    rows = [jnp.pad(r.reshape(-1).astype(F32), (0, width - r.size)) for r in rows]
    pad = (-len(rows)) % 8
    rows += [jnp.zeros((width,), F32)] * pad
    return jnp.stack(rows)


def kernel(x, positions, mix_norm, mlp_norm, sb_w_in, sb_w_out, fox_w_in, fox_b_f, fox_q_gain, fox_k_gain, fox_w_out, mla_w_in, mla_q_norm, mla_kv_norm, mla_w_uq, mla_w_ukv, mla_q_gain, mla_k_gain, mla_w_out, mlp_w1, mlp_w2, loss_target, m_mix_norm, m_mlp_norm, m_sb_w_in, m_sb_w_out, m_fox_w_in, m_fox_b_f, m_fox_q_gain, m_fox_k_gain, m_fox_w_out, m_mla_w_in, m_mla_q_norm, m_mla_kv_norm, m_mla_w_uq, m_mla_w_ukv, m_mla_q_gain, m_mla_k_gain, m_mla_w_out, m_mlp_w1, m_mlp_w2, v_mix_norm, v_mlp_norm, v_sb_w_in, v_sb_w_out, v_fox_w_in, v_fox_b_f, v_fox_q_gain, v_fox_k_gain, v_fox_w_out, v_mla_w_in, v_mla_q_norm, v_mla_kv_norm, v_mla_w_uq, v_mla_w_ukv, v_mla_q_gain, v_mla_k_gain, v_mla_w_out, v_mlp_w1, v_mlp_w2):
    w_in = dict(zip(_WEIGHTS, (mix_norm, mlp_norm, sb_w_in, sb_w_out, fox_w_in, fox_b_f, fox_q_gain, fox_k_gain, fox_w_out, mla_w_in, mla_q_norm, mla_kv_norm, mla_w_uq, mla_w_ukv, mla_q_gain, mla_k_gain, mla_w_out, mlp_w1, mlp_w2)))
    m_in = dict(zip(_WEIGHTS, (m_mix_norm, m_mlp_norm, m_sb_w_in, m_sb_w_out, m_fox_w_in, m_fox_b_f, m_fox_q_gain, m_fox_k_gain, m_fox_w_out, m_mla_w_in, m_mla_q_norm, m_mla_kv_norm, m_mla_w_uq, m_mla_w_ukv, m_mla_q_gain, m_mla_k_gain, m_mla_w_out, m_mlp_w1, m_mlp_w2)))
    v_in = dict(zip(_WEIGHTS, (v_mix_norm, v_mlp_norm, v_sb_w_in, v_sb_w_out, v_fox_w_in, v_fox_b_f, v_fox_q_gain, v_fox_k_gain, v_fox_w_out, v_mla_w_in, v_mla_q_norm, v_mla_kv_norm, v_mla_w_uq, v_mla_w_ukv, v_mla_q_gain, v_mla_k_gain, v_mla_w_out, v_mlp_w1, v_mlp_w2)))
    depth, D = mix_norm.shape
    H = D // HEAD_DIM
    n_mla = mla_w_in.shape[0]
    dev = 4 * lax.axis_index("x") + 2 * lax.axis_index("y") + lax.axis_index("c")
    core = lax.axis_index("c").astype(jnp.int32).reshape(1)
    dev_arr = dev.astype(jnp.int32).reshape(1)

    chip = (2 * lax.axis_index("x") + lax.axis_index("y")).astype(jnp.int32).reshape(1)
    nq, nkv = mla_q_norm.shape[1], mla_kv_norm.shape[1]

    units = []
    for i in range(depth):
        big = _layer_big(i)
        units += [(i, ("mix",), big[:-2]), (i, ("mlp",), big[-2:])] if i == 0 else [(i, ("mix", "mlp"), big)]

    def unit_of(i):
        return i + 1 if i else 0

    def tag_of(u):
        i, parts, _ = units[u]
        return f"l{i}" if len(parts) == 2 else f"l{i}_{parts[0]}"

    kinds_of, shapes_of, ici = [], [], []
    for i, parts, names in units:
        shards = [w_in[n][j] for n, j in names]
        kinds = [_BIG[n] for n, _ in names]
        behind = (ici[-1].token,) if ici else ()
        fulls = [_own_in_place(s, k, dev_arr, BF16, behind) for s, k in zip(shards, kinds)]
        if i % 3 == 2:
            shards.append(_pack_rows([mla_q_norm[i // 3], mla_kv_norm[i // 3]], LANE))
            kinds.append("row")
            fulls.append(_own_in_place(shards[-1], "row", dev_arr, F32, behind))
        kinds_of.append(kinds)
        shapes_of.append([s.shape for s in shards])
        ici.append(_gather_ici_start(fulls, kinds, shapes_of[-1], tag_of(len(ici)),
                                     ici[-1].token if ici else None))

    full = {n: [None] * w_in[n].shape[0] for n in _WEIGHTS}
    for n in ("mix_norm", "mlp_norm", "fox_b_f", "fox_q_gain", "fox_k_gain", "mla_q_gain", "mla_k_gain"):
        full[n] = w_in[n]
    P = {"mix_norm": mix_norm, "mlp_norm": mlp_norm, "mlp": [None] * depth,
         "sb": [None] * sb_w_in.shape[0], "fox": [None] * fox_w_in.shape[0], "mla": [None] * n_mla}

    def forward_to_sibling(u, after):
        arrived = _wait_copies(ici[u], after)
        return _gather_d2d_start(arrived, kinds_of[u], shapes_of[u], tag_of(u))

    def finish_gather(u, flight, after):
        i, parts, names = units[u]
        got = _wait_copies(flight, after)
        for (n, j), a in zip(names, got):
            full[n][j] = _stack_to_cols(a) if n == "fox_w_in" else a
        if i % 3 == 2:
            tiles = got[-1].reshape(N_DEV, 8, LANE)
            full["mla_q_norm"][i // 3] = tiles[:, 0, :nq].reshape(-1)
            full["mla_kv_norm"][i // 3] = tiles[:, 1, :nkv].reshape(-1)
        _prepare_layer(P, full, i, H, parts)

    tabs = _rope_tables(positions[0])
    d2d = forward_to_sibling(0, ici[-1].token)
    finish_gather(0, d2d, d2d.token)
    xs, saved = x[0], []
    for i in range(depth):
        nxt = {}

        def hook(x_mid, i=i, nxt=nxt):
            if i == 0:
                mlp_d2d = forward_to_sibling(1, x_mid)
                finish_gather(1, mlp_d2d, mlp_d2d.token)
                return ()
            if i + 1 == depth:
                return ()
            nxt["d2d"] = forward_to_sibling(unit_of(i + 1), x_mid)
            return (nxt["d2d"].token,)

        xs, s = _layer_fwd(i, xs, P, tabs, H, hook)
        saved.append(s)
        if i == 0 and depth > 1:
            nxt["d2d"] = forward_to_sibling(unit_of(1), xs)
            finish_gather(unit_of(1), nxt["d2d"], nxt["d2d"].token)
        elif i + 1 < depth:
            finish_gather(unit_of(i + 1), nxt["d2d"], xs)
    dx, dxb, loss = _loss_head(xs, loss_target[0])
    loss = lax.psum(loss[0, 0], ("x", "y", "c"))

    def unit_meta(u):
        names = units[u][2]
        return [_BIG[n] for n, _ in names], [w_in[n].shape[1:] for n, _ in names]

    def to_sibling_start(u, Gl):
        grads = [(_cols_to_stack(Gl[n]) if n == "fox_w_in" else Gl[n]) for n, _ in units[u][2]]
        return _scatter_d2d_start(grads, *unit_meta(u), tag_of(u))

    def pair_and_send(u, flight, after, before=None):
        n = len(units[u][2])
        kinds, shapes = unit_meta(u)
        got = _wait_copies(flight, after)
        sums = [_pair_add(g, r4, k, s, core, name="pair_add_" + tag_of(u))
                for g, r4, k, s in zip(got[:n], got[n:], kinds, shapes)]
        return _scatter_ici_start(sums, shapes, tag_of(u), before)

    G, to_chips, deps, flying = [None] * depth, [None] * len(units), (), {}
    for i in reversed(range(depth)):
        def hook(dxb_mid, Gp, i=i):
            out = ()
            if i + 1 < depth:
                u = unit_of(i + 1)
                to_chips[u] = pair_and_send(u, flying.pop(u), dxb_mid)
                out = (to_chips[u].token,)
            if i == 0:
                flying[1] = to_sibling_start(1, {"mlp_w1": Gp["w1"], "mlp_w2": Gp["w2"]})
                out = out + (flying[1].token,)
            return out

        dx, dxb, Gp = _layer_bwd(i, dx, dxb, saved[i], P, tabs, H, deps, hook)
        G[i] = _layer_grads(i, Gp, H)
        flying[unit_of(i)] = to_sibling_start(unit_of(i), G[i])
        deps = (flying[unit_of(i)].token,)
    grad_x = dx

    small = [n for n in _WEIGHTS if n not in _BIG]
    rows = []
    for n in small:
        for j in range(w_in[n].shape[0]):
            rows.append(G[_layer_of(n, j)][n])
    sg = _pack_rows(rows, D)
    (parts,) = _reduce_scatter([sg], ["all"], [sg.shape], core, "small")
    sgr = _sum4(parts)
    to_chips[1] = pair_and_send(1, flying.pop(1), dx, sgr)
    to_chips[0] = pair_and_send(0, flying.pop(0), to_chips[1].token)

    results = {}
    for u in reversed(range(len(units))):
        names = units[u][2]
        got = _wait_copies(to_chips[u], to_chips[0].token)
        for a, (n, j) in enumerate(names):
            results[n] = _adamw(w_in[n], m_in[n], v_in[n], got[a], got[len(names) + a], chip, j, results.get(n),
                                name="adamw_" + n)

    g_rows, w_rows, m_rows, v_rows, at = [], [], [], [], 0
    for n in small:
        for j in range(w_in[n].shape[0]):
            width = w_in[n].shape[1]
            if n in ("mla_q_norm", "mla_kv_norm"):
                g_rows.append(lax.dynamic_slice(sgr[at], (dev * width,), (width,)))
            else:
                g_rows.append(sgr[at, :width])
            w_rows.append(w_in[n][j])
            m_rows.append(m_in[n][j])
            v_rows.append(v_in[n][j])
            at += 1
    g_pack = _pack_rows(g_rows, D)
    d_pack, nm_pack, nv_pack = _adamw_small(
        _pack_rows(w_rows, D), g_pack, _pack_rows(m_rows, D), _pack_rows(v_rows, D))
    at = 0
    for n in small:
        L, width = w_in[n].shape
        results[n] = tuple(p[at:at + L, :width] for p in (g_pack, d_pack, nm_pack, nv_pack))
        at += L

    out = [loss, grad_x[None]]
    for part in range(4):
        out += [results[n][part] for n in _WEIGHTS]
    return tuple(out)
```

```python
import functools
import math

import jax
import jax.numpy as jnp
from jax import lax
from jax.experimental import pallas as pl
from jax.experimental.pallas import tpu as pltpu

F32 = jnp.float32
BF16 = jnp.bfloat16

HEAD_DIM = 128
MLA_NOPE = 128
MLA_ROPE = 64
MLA_V = 128
MLA_QK = MLA_NOPE + MLA_ROPE
MLA_QK_PAD = 256
LANE = 128
ROPE_THETA = 10000.0
EPS = 1e-6
ADAM_LR = 0.001
ADAM_B1 = 0.9
ADAM_B2 = 0.999
ADAM_EPS = 1e-08
ADAM_WD = 0.01
ADAM_STEP = 10
N_DEV = 8
N_CHIP = 4
NEG = -1e30
VMEM_LIMIT = 56 * 1024 * 1024
ATT_BLOCK = 256
SB_BLOCK = 256
ATT_HEADS = 2
SPLIT_LAYERS = 2
SB_PARTS = 2
MESH = pl.DeviceIdType.MESH
ANY = pl.BlockSpec(memory_space=pl.ANY)


def _params(sem):
    return pltpu.CompilerParams(dimension_semantics=sem, vmem_limit_bytes=VMEM_LIMIT)


def _tile(dim, pref):
    if dim <= pref:
        return dim
    t = pref
    while dim % t:
        t -= LANE
    return t


def _dot(a, b, dims):
    return lax.dot_general(a, b, (dims, ((), ())), preferred_element_type=F32)


NN = ((1,), (0,))
NT = ((1,), (1,))
TN = ((0,), (0,))


def _mm(a, b, *, mode="nn", out_dtype=F32, res=None, act=None, z=None, name,
        tm=1024, tn=1024, tk=2048, deps=()):
    if mode == "nn":
        (M, K), (_, N) = a.shape, b.shape
    elif mode == "nt":
        (M, K), (N, _) = a.shape, b.shape
    else:
        (K, M), (_, N) = a.shape, b.shape
    tm, tn, tk = _tile(M, tm), _tile(N, tn), _tile(K, tk)
    nk = K // tk
    if mode == "tn":
        a_spec = pl.BlockSpec((tk, tm), lambda i, j, k: (k, i))
    else:
        a_spec = pl.BlockSpec((tm, tk), lambda i, j, k: (i, k))
    if mode == "nt":
        b_spec = pl.BlockSpec((tn, tk), lambda i, j, k: (j, k))
    else:
        b_spec = pl.BlockSpec((tk, tn), lambda i, j, k: (k, j))
    dims = {"nn": NN, "nt": NT, "tn": TN}[mode]
    o_spec = pl.BlockSpec((tm, tn), lambda i, j, k: (i, j))
    in_specs, args = [a_spec, b_spec], [a, b]
    if res is not None:
        in_specs.append(o_spec)
        args.append(res)
    if act == "drelu2":
        in_specs.append(o_spec)
        args.append(z)
    if act == "relu2":
        out_shape = (jax.ShapeDtypeStruct((M, N), F32), jax.ShapeDtypeStruct((M, N), BF16))
        out_specs = (o_spec, o_spec)
    else:
        out_shape = jax.ShapeDtypeStruct((M, N), out_dtype)
        out_specs = o_spec
    has_res, has_z = res is not None, act == "drelu2"
    n_out = 2 if act == "relu2" else 1
    in_specs += [ANY] * len(deps)
    args += list(deps)

    def body(*refs):
        a_ref, b_ref = refs[0], refs[1]
        idx = 2
        res_ref = z_ref = None
        if has_res:
            res_ref = refs[idx]
            idx += 1
        if has_z:
            z_ref = refs[idx]
            idx += 1
        idx += len(deps)
        outs = refs[idx:idx + n_out]

        def finish(r):
            if has_res:
                r = r + res_ref[...]
            if act == "relu2":
                outs[0][...] = r
                rr = jnp.maximum(r, 0.0)
                outs[1][...] = (rr * rr).astype(BF16)
            elif act == "drelu2":
                outs[0][...] = (r * (2.0 * jnp.maximum(z_ref[...], 0.0))).astype(out_dtype)
            else:
                outs[0][...] = r.astype(out_dtype)

        if nk == 1:
            finish(_dot(a_ref[...], b_ref[...], dims))
            return
        acc = refs[-1]
        k = pl.program_id(2)

        @pl.when(k == 0)
        def _():
            acc[...] = _dot(a_ref[...], b_ref[...], dims)

        @pl.when((k > 0) & (k < nk - 1))
        def _():
            acc[...] += _dot(a_ref[...], b_ref[...], dims)

        @pl.when(k == nk - 1)
        def _():
            finish(acc[...] + _dot(a_ref[...], b_ref[...], dims))

    return pl.pallas_call(
        body, name=name, grid=(M // tm, N // tn, nk), in_specs=in_specs, out_specs=out_specs,
        out_shape=out_shape, scratch_shapes=[pltpu.VMEM((tm, tn), F32)] if nk > 1 else [],
        compiler_params=_params(("parallel", "parallel", "arbitrary")),
    )(*args)


def _rmsnorm_fwd(x, g, *, name, tm=256):
    T, n = x.shape
    tm = _tile(T, tm)

    def body(x_ref, g_ref, o_ref):
        xf = x_ref[...]
        r = lax.rsqrt(jnp.mean(xf * xf, axis=-1, keepdims=True) + EPS)
        o_ref[...] = (xf * r * g_ref[...]).astype(BF16)

    return pl.pallas_call(
        body, name=name, grid=(T // tm,),
        in_specs=[pl.BlockSpec((tm, n), lambda i: (i, 0)), pl.BlockSpec((1, n), lambda i: (0, 0))],
        out_specs=pl.BlockSpec((tm, n), lambda i: (i, 0)),
        out_shape=jax.ShapeDtypeStruct((T, n), BF16),
        compiler_params=_params(("parallel",)),
    )(x, g)


def _rmsnorm_bwd(x, g, dy, dx_in=None, *, name, want_f32=True, tm=256):
    T, n = x.shape
    tm = _tile(T, tm)
    has_in = dx_in is not None
    row = pl.BlockSpec((tm, n), lambda i: (i, 0))
    vec = pl.BlockSpec((1, n), lambda i: (0, 0))

    def body(*refs):
        x_ref, g_ref, dy_ref = refs[:3]
        idx = 3
        in_ref = None
        if has_in:
            in_ref = refs[idx]
            idx += 1
        outs = refs[idx:]
        xf = x_ref[...]
        r = lax.rsqrt(jnp.mean(xf * xf, axis=-1, keepdims=True) + EPS)
        dyf = dy_ref[...]
        t = dyf * g_ref[...]
        dx = r * t - xf * (r * r * r * jnp.mean(t * xf, axis=-1, keepdims=True))
        if has_in:
            dx = dx + in_ref[...]
        o = 0
        if want_f32:
            outs[0][...] = dx
            o = 1
        outs[o][...] = dx.astype(BF16)
        dg_ref = outs[o + 1]

        @pl.when(pl.program_id(0) == 0)
        def _():
            dg_ref[...] = jnp.zeros_like(dg_ref)

        dg_ref[...] += jnp.sum(dyf * xf * r, axis=0, keepdims=True)

    in_specs = [row, vec, row] + ([row] if has_in else [])
    out_specs, out_shape = [], []
    if want_f32:
        out_specs.append(row)
        out_shape.append(jax.ShapeDtypeStruct((T, n), F32))
    out_specs += [row, vec]
    out_shape += [jax.ShapeDtypeStruct((T, n), BF16), jax.ShapeDtypeStruct((1, n), F32)]
    args = [x, g, dy] + ([dx_in] if has_in else [])
    return pl.pallas_call(
        body, name=name, grid=(T // tm,), in_specs=in_specs, out_specs=tuple(out_specs),
        out_shape=tuple(out_shape), compiler_params=_params(("arbitrary",)),
    )(*args)


def _loss_head(y, target, *, tm=256):
    T, n = y.shape
    tm = _tile(T, tm)
    row = pl.BlockSpec((tm, n), lambda i: (i, 0))

    def body(y_ref, t_ref, dy_ref, dyb_ref, loss_ref):
        err = y_ref[...] - t_ref[...]
        dy = err * (1.0 / n)
        dy_ref[...] = dy
        dyb_ref[...] = dy.astype(BF16)

        @pl.when(pl.program_id(0) == 0)
        def _():
            loss_ref[...] = jnp.zeros_like(loss_ref)

        part = 0.5 * jnp.sum(jnp.mean(err * err, axis=-1, keepdims=True), axis=0, keepdims=True)
        loss_ref[...] += part

    return pl.pallas_call(
        body, name="loss_head", grid=(T // tm,), in_specs=[row, row],
        out_specs=(row, row, pl.BlockSpec((1, 1), lambda i: (0, 0))),
        out_shape=(jax.ShapeDtypeStruct((T, n), F32), jax.ShapeDtypeStruct((T, n), BF16),
                   jax.ShapeDtypeStruct((1, 1), F32)),
        compiler_params=_params(("arbitrary",)),
    )(y, target)


def _split(v, n):
    parts = []
    for _ in range(n - 1):
        p = v.astype(BF16)
        parts.append(p)
        v = v - p.astype(F32)
    parts.append(v.astype(BF16))
    return parts


def _tri_left(tri, v):
    hi, mid, lo = _split(v, 3)
    return _dot(tri, hi, NN) + _dot(tri, mid, NN) + _dot(tri, lo, NN)


def _iota2(shape, dim):
    return lax.broadcasted_iota(jnp.int32, shape, dim)


def _log_sigmoid(z):
    return jnp.minimum(z, 0.0) - jnp.log1p(jnp.exp(-jnp.abs(z)))


def _log_sigmoid_abs(z):
    return jnp.minimum(z, 0.0) - jnp.log(1.0 + jnp.exp(-jnp.abs(z)))


def _head_spec(rows, width, off):
    return pl.BlockSpec((rows, width), lambda h, i: (i, off + h))


def _full_head_spec(T, width, off):
    return pl.BlockSpec((T, width), lambda h, i: (0, off + h))


def _sb_fwd(qkv, H):
    T = qkv.shape[0]
    B = _tile(T, SB_BLOCK)
    nq = T // B
    scale = 1.0 / math.sqrt(HEAD_DIM)

    HG = ATT_HEADS if H % ATT_HEADS == 0 else 1
    HD = HEAD_DIM

    def body(q_ref, k_ref, v_ref, o_ref, c_ref, tri_s):
        i = pl.program_id(1)
        tri = (_iota2((B, B), 0) > _iota2((B, B), 1)).astype(BF16)
        for p in range(SB_PARTS):
            tri_s[pl.ds(p * B, B), :] = tri
        rows = _iota2((B, B), 0)
        cols = _iota2((B, B), 1)

        def step(n, carry):
            j = i - n
            ks = pl.multiple_of(j * B, B)
            mask = (ks + cols) < (i * B + rows)
            out = []
            for g in range(HG):
                acc, run = carry[g]
                q = q_ref[:, pl.ds(g * HD, HD)]
                kb = k_ref[pl.ds(ks, B), pl.ds(g * HD, HD)]
                vb = v_ref[pl.ds(ks, B), pl.ds(g * HD, HD)]
                z = _dot(q, kb, NT) * scale
                ls = _log_sigmoid_abs(z)
                lk = jnp.where(mask, ls - z, 0.0)
                parts = jnp.concatenate(_split(lk, SB_PARTS), axis=1)
                later = _dot(parts, tri_s[...], NN)
                a = jnp.exp(jnp.where(mask, ls, NEG) + later).astype(BF16)
                acc = acc + jnp.exp(run) * _dot(a, vb, NN)
                out.append((acc, run + jnp.sum(lk, axis=1, keepdims=True)))
            return tuple(out)

        init = tuple((jnp.zeros((B, HD), F32), jnp.zeros((B, 1), F32)) for _ in range(HG))
        res = lax.fori_loop(0, i + 1, step, init)
        for g in range(HG):
            o_ref[:, pl.ds(g * HD, HD)] = res[g][0].astype(BF16)
            c_ref[g] = res[g][1]

    return pl.pallas_call(
        body, name="sb_attn_fwd", grid=(H // HG, nq),
        in_specs=[_head_spec(B, HG * HD, 0), _full_head_spec(T, HG * HD, H // HG),
                  _full_head_spec(T, HG * HD, 2 * H // HG)],
        out_specs=(_head_spec(B, HG * HD, 0), pl.BlockSpec((HG, B, 1), lambda h, i: (h, i, 0))),
        out_shape=(jax.ShapeDtypeStruct((T, H * HD), BF16), jax.ShapeDtypeStruct((H, T, 1), F32)),
        scratch_shapes=[pltpu.VMEM((SB_PARTS * B, B), BF16)],
        compiler_params=_params(("parallel", "arbitrary")),
    )(qkv, qkv, qkv)


def _sb_bwd(qkv, do, ctot, H):
    T = qkv.shape[0]
    B = _tile(T, SB_BLOCK)
    nq = T // B
    scale = 1.0 / math.sqrt(HEAD_DIM)
    HG = ATT_HEADS if H % ATT_HEADS == 0 else 1
    HD = HEAD_DIM

    def body(q_ref, k_ref, v_ref, do_ref, c_ref, dq_ref, dk_ref, dv_ref, dk_acc, dv_acc,
             tri_incl_s, tri_strict_s):
        i = pl.program_id(1)

        @pl.when(i == 0)
        def _():
            dk_acc[...] = jnp.zeros_like(dk_acc)
            dv_acc[...] = jnp.zeros_like(dv_acc)

        rows = _iota2((B, B), 0)
        cols = _iota2((B, B), 1)
        tri_incl = (rows <= cols).astype(BF16)
        tri_strict = (rows < cols).astype(BF16)
        for p in range(SB_PARTS):
            tri_incl_s[pl.ds(p * B, B), :] = tri_incl
            tri_strict_s[pl.ds(p * B, B), :] = tri_strict
        heads = [(q_ref[:, pl.ds(g * HD, HD)], do_ref[:, pl.ds(g * HD, HD)], c_ref[g]) for g in range(HG)]

        def step(j, carry):
            ks = pl.multiple_of(j * B, B)
            mask = (ks + cols) < (i * B + rows)
            out = []
            for g in range(HG):
                dq, lpre, gpre = carry[g]
                q, do_b, ctot_b = heads[g]
                kb = k_ref[pl.ds(ks, B), pl.ds(g * HD, HD)]
                vb = v_ref[pl.ds(ks, B), pl.ds(g * HD, HD)]
                z = _dot(q, kb, NT) * scale
                da = _dot(do_b, vb, NT)
                ls = _log_sigmoid_abs(z)
                lk = jnp.where(mask, ls - z, 0.0)
                lsm = jnp.where(mask, ls, NEG)
                incl = _dot(jnp.concatenate(_split(lk, SB_PARTS), axis=1), tri_incl_s[...], NN)
                a = jnp.exp(lsm + (ctot_b - lpre) - incl)
                gw = a * da
                gex = gpre + _dot(jnp.concatenate(_split(gw, SB_PARTS), axis=1), tri_strict_s[...], NN)
                dzb = ((gw * jnp.exp(lsm - z) - jnp.exp(lsm) * gex) * scale).astype(BF16)
                dq = dq + _dot(dzb, kb, NN)
                dk_acc[pl.ds(ks, B), pl.ds(g * HD, HD)] += _dot(dzb, q, TN)
                dv_acc[pl.ds(ks, B), pl.ds(g * HD, HD)] += _dot(a.astype(BF16), do_b, TN)
                out.append((dq, lpre + jnp.sum(lk, axis=1, keepdims=True),
                            gpre + jnp.sum(gw, axis=1, keepdims=True)))
            return tuple(out)

        init = tuple((jnp.zeros((B, HD), F32), jnp.zeros((B, 1), F32), jnp.zeros((B, 1), F32))
                     for _ in range(HG))
        res = lax.fori_loop(0, i + 1, step, init)
        for g in range(HG):
            dq_ref[:, pl.ds(g * HD, HD)] = res[g][0].astype(BF16)

        @pl.when(i == nq - 1)
        def _():
            dk_ref[...] = dk_acc[...].astype(BF16)
            dv_ref[...] = dv_acc[...].astype(BF16)

    W = H * HD
    return pl.pallas_call(
        body, name="sb_attn_bwd", grid=(H // HG, nq),
        in_specs=[_head_spec(B, HG * HD, 0), _full_head_spec(T, HG * HD, H // HG),
                  _full_head_spec(T, HG * HD, 2 * H // HG), _head_spec(B, HG * HD, 0),
                  pl.BlockSpec((HG, B, 1), lambda h, i: (h, i, 0))],
        out_specs=(_head_spec(B, HG * HD, 0), _full_head_spec(T, HG * HD, 0),
                   _full_head_spec(T, HG * HD, 0)),
        out_shape=tuple(jax.ShapeDtypeStruct((T, W), BF16) for _ in range(3)),
        scratch_shapes=[pltpu.VMEM((T, HG * HD), F32), pltpu.VMEM((T, HG * HD), F32),
                        pltpu.VMEM((SB_PARTS * B, B), BF16), pltpu.VMEM((SB_PARTS * B, B), BF16)],
        compiler_params=_params(("parallel", "arbitrary")),
    )(qkv, qkv, qkv, do, ctot)


def _softmax_fwd(q, k, v, cf, H, dqk, scale, *, name):
    T = q.shape[0]
    B = _tile(T, ATT_BLOCK)
    nq = T // B
    has_cf = cf is not None
    HG = ATT_HEADS if H % ATT_HEADS == 0 else 1
    HD = HEAD_DIM

    def body(*refs):
        q_ref, k_ref, v_ref = refs[:3]
        idx = 3
        if has_cf:
            cfc_ref, cfr_ref = refs[3], refs[4]
            idx = 5
        o_ref, of_ref, lse_ref = refs[idx:idx + 3]
        i = pl.program_id(1)
        rows = _iota2((B, B), 0)
        cols = _iota2((B, B), 1)

        def step(j, carry):
            ks = pl.multiple_of(j * B, B)
            mask = (ks + cols) <= (i * B + rows)
            out = []
            for g in range(HG):
                m, l, acc = carry[g]
                qb = q_ref[:, pl.ds(g * dqk, dqk)]
                kb = k_ref[pl.ds(ks, B), pl.ds(g * dqk, dqk)]
                vb = v_ref[pl.ds(ks, B), pl.ds(g * HD, HD)]
                s = _dot(qb, kb, NT) * scale
                if has_cf:
                    s = s + (cfc_ref[g] - cfr_ref[g, :, pl.ds(ks, B)])
                s = jnp.where(mask, s, NEG)
                m_new = jnp.maximum(m, jnp.max(s, axis=1, keepdims=True))
                alpha = jnp.exp(m - m_new)
                p = jnp.exp(s - m_new)
                l = alpha * l + jnp.sum(p, axis=1, keepdims=True)
                acc = alpha * acc + _dot(p.astype(BF16), vb, NN)
                out.append((m_new, l, acc))
            return tuple(out)

        init = tuple((jnp.full((B, 1), NEG, F32), jnp.zeros((B, 1), F32), jnp.zeros((B, HD), F32))
                     for _ in range(HG))
        res = lax.fori_loop(0, i + 1, step, init)
        for g in range(HG):
            m, l, acc = res[g]
            o = acc / l
            o_ref[:, pl.ds(g * HD, HD)] = o.astype(BF16)
            of_ref[:, pl.ds(g * HD, HD)] = o
            lse_ref[g] = m + jnp.log(l)

    stat = pl.BlockSpec((HG, B, 1), lambda h, i: (h, i, 0))
    in_specs = [_head_spec(B, HG * dqk, 0), _full_head_spec(T, HG * dqk, 0), _full_head_spec(T, HG * HD, 0)]
    args = [q, k, v]
    if has_cf:
        in_specs += [stat, pl.BlockSpec((HG, 1, T), lambda h, i: (h, 0, 0))]
        args += list(cf)
    W = H * HD
    return pl.pallas_call(
        body, name=name, grid=(H // HG, nq), in_specs=in_specs,
        out_specs=(_head_spec(B, HG * HD, 0), _head_spec(B, HG * HD, 0), stat),
        out_shape=(jax.ShapeDtypeStruct((T, W), BF16), jax.ShapeDtypeStruct((T, W), F32),
                   jax.ShapeDtypeStruct((H, T, 1), F32)),
        compiler_params=_params(("parallel", "arbitrary")),
    )(*args)


def _softmax_bwd(q, k, v, cf, do, o, lse, H, dqk, scale, *, name):
    T = q.shape[0]
    B = _tile(T, ATT_BLOCK)
    nq = T // B
    has_cf = cf is not None
    HG = ATT_HEADS if H % ATT_HEADS == 0 else 1
    HD = HEAD_DIM

    def body(*refs):
        q_ref, k_ref, v_ref, do_ref, o_ref, lse_ref = refs[:6]
        idx = 6
        if has_cf:
            cfc_ref, cfr_ref = refs[6], refs[7]
            idx = 8
        dq_ref, dk_ref, dv_ref = refs[idx:idx + 3]
        idx += 3
        if has_cf:
            dcc_ref, dcr_ref = refs[idx], refs[idx + 1]
        i = pl.program_id(1)

        @pl.when(i == 0)
        def _():
            dk_ref[...] = jnp.zeros_like(dk_ref)
            dv_ref[...] = jnp.zeros_like(dv_ref)
            if has_cf:
                dcr_ref[...] = jnp.zeros_like(dcr_ref)

        rows = _iota2((B, B), 0)
        cols = _iota2((B, B), 1)
        heads = []
        for g in range(HG):
            do_b = do_ref[:, pl.ds(g * HD, HD)]
            delta = jnp.sum(do_b.astype(F32) * o_ref[:, pl.ds(g * HD, HD)], axis=1, keepdims=True)
            heads.append((q_ref[:, pl.ds(g * dqk, dqk)], do_b, lse_ref[g], delta))

        def step(j, carry):
            ks = pl.multiple_of(j * B, B)
            mask = (ks + cols) <= (i * B + rows)
            out = []
            for g in range(HG):
                dq, rs = carry[g]
                qb, do_b, lse_b, delta = heads[g]
                kb = k_ref[pl.ds(ks, B), pl.ds(g * dqk, dqk)]
                vb = v_ref[pl.ds(ks, B), pl.ds(g * HD, HD)]
                s = _dot(qb, kb, NT) * scale
                if has_cf:
                    s = s + (cfc_ref[g] - cfr_ref[g, :, pl.ds(ks, B)])
                p = jnp.where(mask, jnp.exp(s - lse_b), 0.0)
                dp = _dot(do_b, vb, NT)
                ds = p * (dp - delta)
                dsb = (ds * scale).astype(BF16)
                dq = dq + _dot(dsb, kb, NN)
                dk_ref[pl.ds(ks, B), pl.ds(g * dqk, dqk)] += _dot(dsb, qb, TN)
                dv_ref[pl.ds(ks, B), pl.ds(g * HD, HD)] += _dot(p.astype(BF16), do_b, TN)
                if has_cf:
                    rs = rs + jnp.sum(ds, axis=1, keepdims=True)
                    dcr_ref[g, :, pl.ds(ks, B)] -= jnp.sum(ds, axis=0, keepdims=True)
                out.append((dq, rs))
            return tuple(out)

        init = tuple((jnp.zeros((B, dqk), F32), jnp.zeros((B, 1), F32)) for _ in range(HG))
        res = lax.fori_loop(0, i + 1, step, init)
        for g in range(HG):
            dq_ref[:, pl.ds(g * dqk, dqk)] = res[g][0]
            if has_cf:
                dcc_ref[g] = res[g][1]

    stat = pl.BlockSpec((HG, B, 1), lambda h, i: (h, i, 0))
    rowstat = pl.BlockSpec((HG, 1, T), lambda h, i: (h, 0, 0))
    in_specs = [_head_spec(B, HG * dqk, 0), _full_head_spec(T, HG * dqk, 0), _full_head_spec(T, HG * HD, 0),
                _head_spec(B, HG * HD, 0), _head_spec(B, HG * HD, 0), stat]
    args = [q, k, v, do, o, lse]
    out_specs = [_head_spec(B, HG * dqk, 0), _full_head_spec(T, HG * dqk, 0), _full_head_spec(T, HG * HD, 0)]
    out_shape = [jax.ShapeDtypeStruct((T, H * dqk), F32), jax.ShapeDtypeStruct((T, H * dqk), F32),
                 jax.ShapeDtypeStruct((T, H * HD), F32)]
    if has_cf:
        in_specs += [stat, rowstat]
        args += list(cf)
        out_specs += [stat, rowstat]
        out_shape += [jax.ShapeDtypeStruct((H, T, 1), F32), jax.ShapeDtypeStruct((H, 1, T), F32)]
    return pl.pallas_call(
        body, name=name, grid=(H // HG, nq), in_specs=in_specs, out_specs=tuple(out_specs),
        out_shape=tuple(out_shape), compiler_params=_params(("parallel", "arbitrary")),
    )(*args)


def _headnorm(x, g):
    r = lax.rsqrt(jnp.mean(x * x, axis=-1, keepdims=True) + EPS)
    return x * r * g, r


def _headnorm_bwd(x, g, dy, n):
    r = lax.rsqrt(jnp.sum(x * x, axis=-1, keepdims=True) * (1.0 / n) + EPS)
    t = dy * g
    dx = r * t - x * (r * r * r * (jnp.sum(t * x, axis=-1, keepdims=True) * (1.0 / n)))
    dg = jnp.sum(dy * x * r, axis=0, keepdims=True)
    return dx, dg


def _fox_prep_fwd(qkv, gq, gk, H, *, tm=512):
    T = qkv.shape[0]
    tm = _tile(T, tm)

    def body(q_ref, k_ref, v_ref, gq_ref, gk_ref, qn_ref, kn_ref, vb_ref):
        qn_ref[...] = _headnorm(q_ref[...], gq_ref[...])[0].astype(BF16)
        kn_ref[...] = _headnorm(k_ref[...], gk_ref[...])[0].astype(BF16)
        vb_ref[...] = v_ref[...].astype(BF16)

    def blk(off):
        return pl.BlockSpec((tm, HEAD_DIM), lambda i, h: (i, off + h))

    vec = pl.BlockSpec((1, HEAD_DIM), lambda i, h: (0, 0))
    W = H * HEAD_DIM
    return pl.pallas_call(
        body, name="fox_prep_fwd", grid=(T // tm, H),
        in_specs=[blk(0), blk(H), blk(2 * H), vec, vec], out_specs=(blk(0), blk(0), blk(0)),
        out_shape=tuple(jax.ShapeDtypeStruct((T, W), BF16) for _ in range(3)),
        compiler_params=_params(("parallel", "parallel")),
    )(qkv, qkv, qkv, gq, gk)


def _fox_prep_bwd(qkv, gq, gk, dqn, dkn, dv, H, *, tm=512):
    T = qkv.shape[0]
    tm = _tile(T, tm)

    def body(q_ref, k_ref, gq_ref, gk_ref, dqn_ref, dkn_ref, dv_ref,
             dq_ref, dk_ref, dvb_ref, dgq_ref, dgk_ref):
        @pl.when((pl.program_id(0) == 0) & (pl.program_id(1) == 0))
        def _():
            dgq_ref[...] = jnp.zeros_like(dgq_ref)
            dgk_ref[...] = jnp.zeros_like(dgk_ref)

        dq, dgq = _headnorm_bwd(q_ref[...], gq_ref[...], dqn_ref[...], HEAD_DIM)
        dk, dgk = _headnorm_bwd(k_ref[...], gk_ref[...], dkn_ref[...], HEAD_DIM)
        dq_ref[...] = dq.astype(BF16)
        dk_ref[...] = dk.astype(BF16)
        dvb_ref[...] = dv_ref[...].astype(BF16)
        dgq_ref[...] += dgq
        dgk_ref[...] += dgk

    def blk(off):
        return pl.BlockSpec((tm, HEAD_DIM), lambda i, h: (i, off + h))

    vec = pl.BlockSpec((1, HEAD_DIM), lambda i, h: (0, 0))
    W = H * HEAD_DIM
    return pl.pallas_call(
        body, name="fox_prep_bwd", grid=(T // tm, H),
        in_specs=[blk(0), blk(H), vec, vec, blk(0), blk(0), blk(0)],
        out_specs=(blk(0), blk(0), blk(0), vec, vec),
        out_shape=tuple(jax.ShapeDtypeStruct((T, W), BF16) for _ in range(3))
        + (jax.ShapeDtypeStruct((1, HEAD_DIM), F32), jax.ShapeDtypeStruct((1, HEAD_DIM), F32)),
        compiler_params=_params(("arbitrary", "arbitrary")),
    )(qkv, qkv, gq, gk, dqn, dkn, dv)


def _fox_gate_fwd(flog, bf, *, tm=256):
    T = flog.shape[0]
    tm = _tile(T, tm)

    def body(f_ref, b_ref, cf_ref, carry):
        @pl.when(pl.program_id(0) == 0)
        def _():
            carry[...] = jnp.zeros_like(carry)

        lf = _log_sigmoid(f_ref[...] + b_ref[...])
        tri = (_iota2((tm, tm), 1) <= _iota2((tm, tm), 0)).astype(BF16)
        cf_ref[...] = carry[...] + _tri_left(tri, lf)
        carry[...] += jnp.sum(lf, axis=0, keepdims=True)

    return pl.pallas_call(
        body, name="fox_gate_fwd", grid=(T // tm,),
        in_specs=[pl.BlockSpec((tm, LANE), lambda i: (i, 0)), pl.BlockSpec((1, LANE), lambda i: (0, 0))],
        out_specs=pl.BlockSpec((tm, LANE), lambda i: (i, 0)),
        out_shape=jax.ShapeDtypeStruct((T, LANE), F32),
        scratch_shapes=[pltpu.VMEM((1, LANE), F32)],
        compiler_params=_params(("arbitrary",)),
    )(flog, bf)


def _fox_gate_bwd(flog, bf, dcf, *, tm=256):
    T = flog.shape[0]
    tm = _tile(T, tm)
    nt = T // tm

    def body(f_ref, b_ref, dcf_ref, df_ref, db_ref, carry):
        @pl.when(pl.program_id(0) == 0)
        def _():
            carry[...] = jnp.zeros_like(carry)
            db_ref[...] = jnp.zeros_like(db_ref)

        d = dcf_ref[...]
        tri = (_iota2((tm, tm), 1) >= _iota2((tm, tm), 0)).astype(BF16)
        dlf = carry[...] + _tri_left(tri, d)
        carry[...] += jnp.sum(d, axis=0, keepdims=True)
        xg = f_ref[...] + b_ref[...]
        e = jnp.exp(-jnp.abs(xg))
        sig_neg = jnp.where(xg >= 0.0, e, 1.0) / (1.0 + e)
        df = dlf * sig_neg
        df_ref[...] = df.astype(BF16)
        db_ref[...] += jnp.sum(df, axis=0, keepdims=True)

    rev = pl.BlockSpec((tm, LANE), lambda i: (nt - 1 - i, 0))
    vec = pl.BlockSpec((1, LANE), lambda i: (0, 0))
    return pl.pallas_call(
        body, name="fox_gate_bwd", grid=(nt,), in_specs=[rev, vec, rev], out_specs=(rev, vec),
        out_shape=(jax.ShapeDtypeStruct((T, LANE), BF16), jax.ShapeDtypeStruct((1, LANE), F32)),
        scratch_shapes=[pltpu.VMEM((1, LANE), F32)],
        compiler_params=_params(("arbitrary",)),
    )(flog, bf, dcf)


def _rope_tables(positions):
    half = MLA_ROPE // 2
    inv_freq = ROPE_THETA ** (-jnp.arange(0, half, dtype=F32) * 2.0 / MLA_ROPE)
    ang = positions.astype(F32)[:, None] * inv_freq
    cos, sin = jnp.cos(ang), jnp.sin(ang)
    zero = jnp.zeros_like(cos)
    pad = jnp.zeros((positions.shape[0], LANE - MLA_ROPE), F32)
    cos_t = jnp.concatenate([cos, cos, pad], axis=1)
    sin_up = jnp.concatenate([zero, sin, pad], axis=1)
    sin_dn = jnp.concatenate([-sin, zero, pad], axis=1)
    return cos_t, sin_up, sin_dn


def _rope(x, cos_t, sin_up, sin_dn):
    half = MLA_ROPE // 2
    return x * cos_t + pltpu.roll(x, half, 1) * sin_up + pltpu.roll(x, LANE - half, 1) * sin_dn


def _rope_t(d, cos_t, sin_up, sin_dn):
    half = MLA_ROPE // 2
    return d * cos_t + pltpu.roll(d * sin_up, LANE - half, 1) + pltpu.roll(d * sin_dn, half, 1)


def _norm192(xcat, g):
    r = lax.rsqrt(jnp.sum(xcat * xcat, axis=-1, keepdims=True) * (1.0 / MLA_QK) + EPS)
    return xcat * r * g


def _mla_prep_fwd(qfull, kv, kr, tabs, gq, gk, H, *, tm=512):
    T = qfull.shape[0]
    tm = _tile(T, tm)

    def body(q_ref, kv_ref, kr_ref, c_ref, su_ref, sd_ref, gq_ref, gk_ref, qf_ref, kf_ref, v_ref):
        tabs_b = (c_ref[...], su_ref[...], sd_ref[...])
        qb = q_ref[...]
        qcat = jnp.concatenate([qb[:, :MLA_NOPE], _rope(qb[:, MLA_NOPE:], *tabs_b)], axis=1)
        qf_ref[...] = _norm192(qcat, gq_ref[...]).astype(BF16)
        kvb = kv_ref[...]
        kcat = jnp.concatenate([kvb[:, :MLA_NOPE], _rope(kr_ref[...], *tabs_b)], axis=1)
        kf_ref[...] = _norm192(kcat, gk_ref[...]).astype(BF16)
        v_ref[...] = kvb[:, MLA_NOPE:].astype(BF16)

    head = pl.BlockSpec((tm, MLA_QK_PAD), lambda i, h: (i, h))
    tok = pl.BlockSpec((tm, LANE), lambda i, h: (i, 0))
    vec = pl.BlockSpec((1, MLA_QK_PAD), lambda i, h: (0, 0))
    return pl.pallas_call(
        body, name="mla_prep_fwd", grid=(T // tm, H),
        in_specs=[head, head, tok, tok, tok, tok, vec, vec],
        out_specs=(head, head, pl.BlockSpec((tm, MLA_V), lambda i, h: (i, h))),
        out_shape=(jax.ShapeDtypeStruct((T, H * MLA_QK_PAD), BF16),
                   jax.ShapeDtypeStruct((T, H * MLA_QK_PAD), BF16),
                   jax.ShapeDtypeStruct((T, H * MLA_V), BF16)),
        compiler_params=_params(("parallel", "parallel")),
    )(qfull, kv, kr, *tabs, gq, gk)


def _mla_prep_bwd(qfull, kv, kr, tabs, gq, gk, dqf, dkf, dv, H, *, tm=512):
    T = qfull.shape[0]
    tm = _tile(T, tm)

    def body(q_ref, kv_ref, kr_ref, c_ref, su_ref, sd_ref, gq_ref, gk_ref, dqf_ref, dkf_ref, dv_ref,
             dq_ref, dkv_ref, dkr_ref, dgq_ref, dgk_ref, kr_acc):
        h = pl.program_id(1)

        @pl.when((pl.program_id(0) == 0) & (h == 0))
        def _():
            dgq_ref[...] = jnp.zeros_like(dgq_ref)
            dgk_ref[...] = jnp.zeros_like(dgk_ref)

        @pl.when(h == 0)
        def _():
            kr_acc[...] = jnp.zeros_like(kr_acc)

        tabs_b = (c_ref[...], su_ref[...], sd_ref[...])
        qb = q_ref[...]
        qcat = jnp.concatenate([qb[:, :MLA_NOPE], _rope(qb[:, MLA_NOPE:], *tabs_b)], axis=1)
        dqcat, dgq = _headnorm_bwd(qcat, gq_ref[...], dqf_ref[...], MLA_QK)
        dq_ref[...] = jnp.concatenate(
            [dqcat[:, :MLA_NOPE], _rope_t(dqcat[:, MLA_NOPE:], *tabs_b)], axis=1).astype(BF16)
        dgq_ref[...] += dgq
        kvb = kv_ref[...]
        kcat = jnp.concatenate([kvb[:, :MLA_NOPE], _rope(kr_ref[...], *tabs_b)], axis=1)
        dkcat, dgk = _headnorm_bwd(kcat, gk_ref[...], dkf_ref[...], MLA_QK)
        dkv_ref[...] = jnp.concatenate([dkcat[:, :MLA_NOPE], dv_ref[...]], axis=1).astype(BF16)
        dgk_ref[...] += dgk
        kr_acc[...] += dkcat[:, MLA_NOPE:]

        @pl.when(h == H - 1)
        def _():
            dkr_ref[...] = _rope_t(kr_acc[...], *tabs_b).astype(BF16)

    head = pl.BlockSpec((tm, MLA_QK_PAD), lambda i, h: (i, h))
    tok = pl.BlockSpec((tm, LANE), lambda i, h: (i, 0))
    vec = pl.BlockSpec((1, MLA_QK_PAD), lambda i, h: (0, 0))
    return pl.pallas_call(
        body, name="mla_prep_bwd", grid=(T // tm, H),
        in_specs=[head, head, tok, tok, tok, tok, vec, vec, head, head,
                  pl.BlockSpec((tm, MLA_V), lambda i, h: (i, h))],
        out_specs=(head, head, tok, vec, vec),
        out_shape=(jax.ShapeDtypeStruct((T, H * MLA_QK_PAD), BF16),
                   jax.ShapeDtypeStruct((T, H * MLA_QK_PAD), BF16),
                   jax.ShapeDtypeStruct((T, LANE), BF16),
                   jax.ShapeDtypeStruct((1, MLA_QK_PAD), F32), jax.ShapeDtypeStruct((1, MLA_QK_PAD), F32)),
        scratch_shapes=[pltpu.VMEM((tm, LANE), F32)],
        compiler_params=_params(("arbitrary", "arbitrary")),
    )(qfull, kv, kr, *tabs, gq, gk, dqf, dkf, dv)


def _cf_layouts(cf, H):
    cfh = cf[:, :H].T
    return cfh[:, :, None], cfh[:, None, :]


def _layer_fwd(i, x, P, tabs, H, hook=None):
    kind, j = i % 3, i // 3
    s = {"x_in": x}
    h = _rmsnorm_fwd(x, P["mix_norm"][i:i + 1], name="mix_norm_fwd")
    s["h"] = h
    if kind == 0:
        w = P["sb"][j]
        qkv = _mm(h, w["w_in"], out_dtype=BF16, name="sb_qkv")
        o, ctot = _sb_fwd(qkv, H)
        s.update(qkv=qkv, o=o, ctot=ctot)
    elif kind == 1:
        w = P["fox"][j]
        qkv = _mm(h, w["w_qkv"], name="fox_qkv")
        flog = _mm(h, w["w_f"], name="fox_gate_proj")
        qn, kn, vb = _fox_prep_fwd(qkv, w["gq"], w["gk"], H)
        cf = _cf_layouts(_fox_gate_fwd(flog, w["bf"]), H)
        o, of, lse = _softmax_fwd(qn, kn, vb, cf, H, HEAD_DIM, 1.0 / math.sqrt(HEAD_DIM), name="fox_attn_fwd")
        s.update(qkv=qkv, flog=flog, qn=qn, kn=kn, vb=vb, cf=cf, o=o, of=of, lse=lse)
    else:
        w = P["mla"][j]
        dq = _mm(h, w["w_dq"], name="mla_down_q")
        dkv = _mm(h, w["w_dkv"], name="mla_down_kv")
        kr = _mm(h, w["w_dr"], name="mla_down_rope")
        cq = _rmsnorm_fwd(dq, w["q_norm"], name="mla_q_norm_fwd")
        ckv = _rmsnorm_fwd(dkv, w["kv_norm"], name="mla_kv_norm_fwd")
        qfull = _mm(cq, w["w_uq"], name="mla_up_q")
        kv = _mm(ckv, w["w_ukv"], name="mla_up_kv")
        qf, kf, v = _mla_prep_fwd(qfull, kv, kr, tabs, w["gq"], w["gk"], H)
        o, of, lse = _softmax_fwd(qf, kf, v, None, H, MLA_QK_PAD, 1.0 / math.sqrt(MLA_QK), name="mla_attn_fwd")
        s.update(dq=dq, dkv=dkv, kr=kr, cq=cq, ckv=ckv, qfull=qfull, kv=kv, qf=qf, kf=kf, v=v,
                 o=o, of=of, lse=lse)
    x = _mm(s["o"], w["w_out"], res=x, name="mixer_out")
    s["x_mid"] = x
    deps = hook(x) if hook is not None else ()
    h2 = _rmsnorm_fwd(x, P["mlp_norm"][i:i + 1], name="mlp_norm_fwd")
    z, u = _mm(h2, P["mlp"][i]["w1"], act="relu2", name="mlp_up", deps=deps)
    x = _mm(u, P["mlp"][i]["w2"], res=x, name="mlp_down")
    s.update(h2=h2, z=z, u=u)
    return x, s


def _layer_bwd(i, dx, dxb, s, P, tabs, H, deps=(), hook=None):
    kind, j = i % 3, i // 3
    G = {}
    wm = P["mlp"][i]
    dz = _mm(dxb, wm["w2"], mode="nt", act="drelu2", z=s["z"], out_dtype=BF16, name="mlp_down_bwd",
             deps=deps)
    G["w2"] = _mm(s["u"], dxb, mode="tn", out_dtype=BF16, name="mlp_w2_grad")
    G["w1"] = _mm(s["h2"], dz, mode="tn", out_dtype=BF16, name="mlp_w1_grad")
    dh2 = _mm(dz, wm["w1"], mode="nt", name="mlp_up_bwd")
    dx, dxb, G["mlp_norm"] = _rmsnorm_bwd(s["x_mid"], P["mlp_norm"][i:i + 1], dh2, dx, name="mlp_norm_bwd")
    h = s["h"]
    w = P[("sb", "fox", "mla")[kind]][j]
    do = _mm(dxb, w["w_out"], mode="nt", out_dtype=BF16, name="mixer_out_bwd",
             deps=hook(dxb, G) if hook is not None else ())
    if kind == 0:
        G["w_out"] = _mm(s["o"], dxb, mode="tn", out_dtype=BF16, name="mixer_out_grad")
        dq, dk, dv = _sb_bwd(s["qkv"], do, s["ctot"], H)
        dproj = jnp.concatenate([dq, dk, dv], axis=1)
        G["w_in"] = _mm(h, dproj, mode="tn", out_dtype=BF16, name="sb_qkv_grad")
        dh = _mm(dproj, w["w_in"], mode="nt", name="sb_qkv_bwd")
    elif kind == 1:
        G["w_out"] = _mm(s["o"], dxb, mode="tn", out_dtype=BF16, name="mixer_out_grad")
        dqn, dkn, dv, dcc, dcr = _softmax_bwd(
            s["qn"], s["kn"], s["vb"], s["cf"], do, s["of"], s["lse"], H, HEAD_DIM,
            1.0 / math.sqrt(HEAD_DIM), name="fox_attn_bwd")
        dq, dk, dvb, G["gq"], G["gk"] = _fox_prep_bwd(s["qkv"], w["gq"], w["gk"], dqn, dkn, dv, H)
        dcf = (dcc[:, :, 0] + dcr[:, 0, :]).T
        dcf = jnp.pad(dcf, ((0, 0), (0, LANE - H)))
        dflog, G["bf"] = _fox_gate_bwd(s["flog"], w["bf"], dcf)
        dproj = jnp.concatenate([dq, dk, dvb], axis=1)
        G["w_qkv"] = _mm(h, dproj, mode="tn", out_dtype=BF16, name="fox_qkv_grad")
        G["w_f"] = _mm(h, dflog, mode="tn", out_dtype=BF16, name="fox_gate_grad")
        dh = _mm(dproj, w["w_qkv"], mode="nt", name="fox_qkv_bwd")
        dh = _mm(dflog, w["w_f"], mode="nt", res=dh, name="fox_gate_bwd_proj")
    else:
        G["w_out"] = _mm(s["o"], dxb, mode="tn", out_dtype=BF16, name="mixer_out_grad")
        dqf, dkf, dv = _softmax_bwd(
            s["qf"], s["kf"], s["v"], None, do, s["of"], s["lse"], H, MLA_QK_PAD,
            1.0 / math.sqrt(MLA_QK), name="mla_attn_bwd")
        dqfull, dkv, dkr, G["gq"], G["gk"] = _mla_prep_bwd(
            s["qfull"], s["kv"], s["kr"], tabs, w["gq"], w["gk"], dqf, dkf, dv, H)
        G["w_uq"] = _mm(s["cq"], dqfull, mode="tn", out_dtype=BF16, name="mla_up_q_grad")
        G["w_ukv"] = _mm(s["ckv"], dkv, mode="tn", out_dtype=BF16, name="mla_up_kv_grad")
        dcq = _mm(dqfull, w["w_uq"], mode="nt", name="mla_up_q_bwd")
        dckv = _mm(dkv, w["w_ukv"], mode="nt", name="mla_up_kv_bwd")
        ddq, G["q_norm"] = _rmsnorm_bwd(s["dq"], w["q_norm"], dcq, want_f32=False, name="mla_q_norm_bwd")
        ddkv, G["kv_norm"] = _rmsnorm_bwd(s["dkv"], w["kv_norm"], dckv, want_f32=False, name="mla_kv_norm_bwd")
        G["w_dq"] = _mm(h, ddq, mode="tn", out_dtype=BF16, name="mla_down_q_grad")
        G["w_dkv"] = _mm(h, ddkv, mode="tn", out_dtype=BF16, name="mla_down_kv_grad")
        G["w_dr"] = _mm(h, dkr, mode="tn", out_dtype=BF16, name="mla_down_rope_grad")
        dh = _mm(ddq, w["w_dq"], mode="nt", name="mla_down_q_bwd")
        dh = _mm(ddkv, w["w_dkv"], mode="nt", res=dh, name="mla_down_kv_bwd")
        dh = _mm(dkr, w["w_dr"], mode="nt", res=dh, name="mla_down_rope_bwd")
    dx, dxb, G["mix_norm"] = _rmsnorm_bwd(s["x_in"], P["mix_norm"][i:i + 1], dh, dx, name="mix_norm_bwd")
    return dx, dxb, G


def _local_step(x, positions, loss_target, P, depth, H):
    tabs = _rope_tables(positions)
    saved = []
    for i in range(depth):
        x, s = _layer_fwd(i, x, P, tabs, H)
        saved.append(s)
    dx, dxb, loss = _loss_head(x, loss_target)
    grads = [None] * depth
    for i in reversed(range(depth)):
        dx, dxb, grads[i] = _layer_bwd(i, dx, dxb, saved[i], P, tabs, H)
    return loss, dx, grads


def _pad_cols(a, n):
    return jnp.pad(a, ((0, 0), (0, n - a.shape[1])))


def _prepare_layer(P, full, i, H, parts=("mix", "mlp")):
    W = H * HEAD_DIM
    kind, j = i % 3, i // 3
    if "mlp" in parts:
        P["mlp"][i] = {"w1": full["mlp_w1"][i], "w2": full["mlp_w2"][i]}
    if "mix" not in parts:
        return
    if kind == 0:
        P["sb"][j] = {"w_in": full["sb_w_in"][j], "w_out": full["sb_w_out"][j]}
    elif kind == 1:
        w = full["fox_w_in"][j]
        P["fox"][j] = {
            "w_qkv": w[:, :3 * W], "w_f": _pad_cols(w[:, 3 * W:], LANE),
            "bf": _pad_cols(full["fox_b_f"][j:j + 1], LANE),
            "gq": full["fox_q_gain"][j:j + 1], "gk": full["fox_k_gain"][j:j + 1],
            "w_out": full["fox_w_out"][j]}
    else:
        w = full["mla_w_in"][j]
        q_norm, kv_norm = full["mla_q_norm"][j], full["mla_kv_norm"][j]
        rq, rkv = q_norm.shape[0], kv_norm.shape[0]
        w_uq = full["mla_w_uq"][j].reshape(rq, H, MLA_QK)
        w_uq = jnp.pad(w_uq, ((0, 0), (0, 0), (0, MLA_QK_PAD - MLA_QK))).reshape(rq, H * MLA_QK_PAD)
        P["mla"][j] = {
            "w_dq": w[:, :rq], "w_dkv": w[:, rq:rq + rkv], "w_dr": _pad_cols(w[:, rq + rkv:], LANE),
            "q_norm": q_norm[None], "kv_norm": kv_norm[None],
            "w_uq": w_uq, "w_ukv": full["mla_w_ukv"][j],
            "gq": _pad_cols(full["mla_q_gain"][j:j + 1], MLA_QK_PAD),
            "gk": _pad_cols(full["mla_k_gain"][j:j + 1], MLA_QK_PAD),
            "w_out": full["mla_w_out"][j]}


def _prepare(full, H):
    depth = len(full["mlp_w1"])
    P = {"mix_norm": full["mix_norm"], "mlp_norm": full["mlp_norm"], "mlp": [None] * depth,
         "sb": [None] * len(full["sb_w_in"]), "fox": [None] * len(full["fox_w_in"]),
         "mla": [None] * len(full["mla_w_in"])}
    for i in range(depth):
        _prepare_layer(P, full, i, H)
    return P


def _layer_grads(i, G, H):
    kind = i % 3
    out = {"mix_norm": G["mix_norm"], "mlp_norm": G["mlp_norm"], "mlp_w1": G["w1"], "mlp_w2": G["w2"]}
    if kind == 0:
        out.update(sb_w_in=G["w_in"], sb_w_out=G["w_out"])
    elif kind == 1:
        out.update(fox_w_in=jnp.concatenate([G["w_qkv"], G["w_f"][:, :H]], axis=1), fox_b_f=G["bf"][:, :H],
                   fox_q_gain=G["gq"], fox_k_gain=G["gk"], fox_w_out=G["w_out"])
    else:
        rq = G["w_uq"].shape[0]
        out.update(
            mla_w_in=jnp.concatenate([G["w_dq"], G["w_dkv"], G["w_dr"][:, :MLA_ROPE]], axis=1),
            mla_q_norm=G["q_norm"], mla_kv_norm=G["kv_norm"],
            mla_w_uq=G["w_uq"].reshape(rq, H, MLA_QK_PAD)[:, :, :MLA_QK].reshape(rq, H * MLA_QK),
            mla_w_ukv=G["w_ukv"], mla_q_gain=G["gq"][:, :MLA_QK], mla_k_gain=G["gk"][:, :MLA_QK],
            mla_w_out=G["w_out"])
    return out


def _layer_of(name, j):
    if name.startswith("sb_"):
        return 3 * j
    if name.startswith("fox_"):
        return 3 * j + 1
    if name.startswith("mla_"):
        return 3 * j + 2
    return j


def _unprepare(grads, H):
    out = {}
    for i, G in enumerate(grads):
        for n, g in _layer_grads(i, G, H).items():
            out.setdefault(n, []).append(g)
    return out


class _Place:
    def __init__(self, x, y, c):
        self.x, self.y, self.c = x, y, c
        self.dev = 4 * x + 2 * y + c
        self.chip = 2 * x + y
        self.id = (x, y, c)


def _peers(me, kind):
    if kind == "ici":
        return [_Place(1 - me.x, me.y, me.c), _Place(me.x, 1 - me.y, me.c), _Place(1 - me.x, 1 - me.y, me.c)]
    return [_Place(me.x, me.y, 1 - me.c)]


def _exchange(name, kind, operands, out_shapes, aliases, n_remote, n_local, plan):
    n_in, n_out = len(operands), len(out_shapes)

    def body(*refs):
        in_refs, out_refs = refs[:n_in], refs[n_in:n_in + n_out]
        send_sems, recv_sems, local_sems = refs[n_in + n_out:]
        me = _Place(lax.axis_index("x"), lax.axis_index("y"), lax.axis_index("c"))
        peers = _peers(me, kind)
        remote, local = plan(me, peers, in_refs, out_refs)
        assert len(remote) == n_remote and len(local) == n_local
        sends = []
        for n, (src, dst, k, _) in enumerate(remote):
            cp = pltpu.make_async_remote_copy(
                src_ref=src, dst_ref=dst, send_sem=send_sems.at[n], recv_sem=recv_sems.at[n],
                device_id=peers[k].id, device_id_type=MESH)
            cp.start()
            sends.append(cp)
        own = []
        for n, (src, dst) in enumerate(local):
            cp = pltpu.make_async_copy(src, dst, local_sems.at[n])
            cp.start()
            own.append(cp)
        for n, (src, _, k, landing) in enumerate(remote):
            pltpu.make_async_remote_copy(
                src_ref=src, dst_ref=landing, send_sem=send_sems.at[n], recv_sem=recv_sems.at[n],
                device_id=peers[k].id, device_id_type=MESH).wait_recv()
        for cp in sends:
            cp.wait_send()
        for cp in own:
            cp.wait()

    outs = pl.pallas_call(
        body, name=name, in_specs=[ANY] * n_in, out_specs=tuple([ANY] * n_out),
        out_shape=tuple(out_shapes), input_output_aliases=aliases,
        scratch_shapes=[pltpu.SemaphoreType.DMA((n_remote,)), pltpu.SemaphoreType.DMA((n_remote,)),
                        pltpu.SemaphoreType.DMA((max(n_local, 1),))],
    )(*operands)
    return list(outs)


def _window(ref, kind, d, shard_shape):
    r, c = shard_shape
    if kind == "col":
        return ref.at[:, pl.ds(pl.multiple_of(d * c, c), c)]
    return ref.at[pl.ds(pl.multiple_of(d * r, r), r), :]


def _full_shape(kind, shard_shape):
    r, c = shard_shape
    return (r, N_DEV * c) if kind == "col" else (N_DEV * r, c)


HBM = pl.BlockSpec(memory_space=pltpu.HBM)
SEM = pl.BlockSpec(memory_space=pltpu.SEMAPHORE)
EFFECT = pltpu.SideEffectType.DATAFLOW_SIDE_EFFECTING


class _Flight:
    def __init__(self, name, kind, n_remote, plan, send_sems, recv_sems, bufs, token):
        self.name, self.kind, self.n_remote, self.plan = name, kind, n_remote, plan
        self.send_sems, self.recv_sems, self.bufs, self.token = send_sems, recv_sems, bufs, token


def _start_copies(name, kind, bufs, n_remote, plan, after=None):
    nb = len(bufs)
    n_after = 0 if after is None else 1

    def body(*refs):
        in_refs = refs[:nb]
        send_sems, recv_sems = refs[nb + n_after], refs[nb + n_after + 1]
        token = refs[2 * nb + n_after + 2]
        me = _Place(lax.axis_index("x"), lax.axis_index("y"), lax.axis_index("c"))
        peers = _peers(me, kind)
        for n, (src, dst, k, _) in enumerate(plan(me, peers, in_refs)):
            pltpu.make_async_remote_copy(
                src_ref=src, dst_ref=dst, send_sem=send_sems.at[n], recv_sem=recv_sems.at[n],
                device_id=peers[k].id, device_id_type=MESH).start()
        token[...] = jnp.zeros_like(token)

    outs = pl.pallas_call(
        body, name=name, in_specs=[HBM] * nb + [ANY] * n_after,
        out_specs=(SEM, SEM) + (HBM,) * nb + (pl.BlockSpec(memory_space=pltpu.VMEM),),
        out_shape=(pltpu.SemaphoreType.DMA((n_remote,)), pltpu.SemaphoreType.DMA((n_remote,)))
        + tuple(pltpu.HBM(b.shape, b.dtype) for b in bufs) + (jax.ShapeDtypeStruct((8, LANE), F32),),
        input_output_aliases={a: 2 + a for a in range(nb)},
        compiler_params=pltpu.CompilerParams(has_side_effects=EFFECT),
    )(*[pltpu.with_memory_space_constraint(b, pltpu.HBM) for b in bufs], *([after] if n_after else []))
    return _Flight(name, kind, n_remote, plan, outs[0], outs[1], list(outs[2:2 + nb]), outs[2 + nb])


def _wait_copies(flight, after):
    nb = len(flight.bufs)
    plan, kind = flight.plan, flight.kind

    def body(*refs):
        in_refs = refs[:nb]
        send_sems, recv_sems = refs[nb], refs[nb + 1]
        me = _Place(lax.axis_index("x"), lax.axis_index("y"), lax.axis_index("c"))
        peers = _peers(me, kind)
        for n, (src, _, k, landing) in enumerate(plan(me, peers, in_refs)):
            cp = pltpu.make_async_remote_copy(
                src_ref=src, dst_ref=landing, send_sem=send_sems.at[n], recv_sem=recv_sems.at[n],
                device_id=peers[k].id, device_id_type=MESH)
            cp.wait_send()
            cp.wait_recv()

    outs = pl.pallas_call(
        body, name=flight.name.replace("_start", "_wait"),
        in_specs=[HBM] * nb + [SEM, SEM, ANY], out_specs=(HBM,) * nb,
        out_shape=tuple(pltpu.HBM(b.shape, b.dtype) for b in flight.bufs),
        input_output_aliases={a: a for a in range(nb)},
        compiler_params=pltpu.CompilerParams(has_side_effects=EFFECT),
    )(*flight.bufs, flight.send_sems, flight.recv_sems, after)
    return list(outs)


def _own_in_place(shard, kind, dev, dtype, deps=()):
    r, c = shard.shape
    tr = _row_tile(r, c)
    nb = r // tr

    def body(dev_ref, s_ref, *rest):
        o_ref = rest[-1]
        o_ref[...] = s_ref[...].astype(dtype)

    if kind == "col":
        o_spec = pl.BlockSpec((tr, c), lambda i, dev_ref: (i, dev_ref[0]))
    else:
        o_spec = pl.BlockSpec((tr, c), lambda i, dev_ref: (dev_ref[0] * nb + i, 0))
    return pl.pallas_call(
        body, name="own_in_place",
        grid_spec=pltpu.PrefetchScalarGridSpec(
            num_scalar_prefetch=1, grid=(nb,),
            in_specs=[pl.BlockSpec((tr, c), lambda i, dev_ref: (i, 0))] + [ANY] * len(deps), out_specs=o_spec),
        out_shape=jax.ShapeDtypeStruct(_full_shape(kind, shard.shape), dtype),
        compiler_params=_params(("parallel",)),
    )(dev, shard, *deps)


def _gather_ici_start(fulls, kinds, shapes, tag, after=None):
    n = len(fulls)

    def plan(me, peers, refs):
        remote = []
        for a in range(n):
            mine = _window(refs[a], kinds[a], me.dev, shapes[a])
            for k, p in enumerate(peers):
                remote.append((mine, mine, k, _window(refs[a], kinds[a], p.dev, shapes[a])))
        return remote

    return _start_copies("gather_ici_start_" + tag, "ici", list(fulls), 3 * n, plan, after)


def _gather_d2d_start(fulls, kinds, shapes, tag):
    n = len(fulls)

    def plan(me, peers, refs):
        remote = []
        for a in range(n):
            for ch in range(N_CHIP):
                held = _window(refs[a], kinds[a], 2 * ch + me.c, shapes[a])
                remote.append((held, held, 0, _window(refs[a], kinds[a], 2 * ch + 1 - me.c, shapes[a])))
        return remote

    return _start_copies("gather_d2d_start_" + tag, "d2d", fulls, N_CHIP * n, plan)


def _scatter_d2d_start(grads, kinds, shapes, tag):
    n = len(grads)
    lands = [lax.empty((N_CHIP,) + tuple(s), g.dtype) for g, s in zip(grads, shapes)]

    def plan(me, peers, refs):
        remote = []
        for a in range(n):
            for ch in range(N_CHIP):
                remote.append((_window(refs[a], kinds[a], 2 * ch + 1 - me.c, shapes[a]),
                               refs[n + a].at[ch], 0, refs[n + a].at[ch]))
        return remote

    return _start_copies("scatter_d2d_start_" + tag, "d2d", list(grads) + lands, N_CHIP * n, plan)


def _scatter_ici_start(sums, shapes, tag, after=None):
    n = len(sums)
    lands = [lax.empty((N_CHIP - 1,) + tuple(s), BF16) for s in shapes]

    def plan(me, peers, refs):
        remote = []
        for a in range(n):
            for k, p in enumerate(peers):
                remote.append((refs[a].at[p.chip], refs[n + a].at[k], k, refs[n + a].at[k]))
        return remote

    return _start_copies("scatter_ici_start_" + tag, "ici", list(sums) + lands, 3 * n, plan, after)


def _pair_add(g, r4, kind, shard_shape, core, *, name):
    r, c = shard_shape
    tr = _row_tile(r, c, 524288)
    nb = r // tr
    if kind == "all":
        def body_all(core_ref, g_ref, r_ref, o_ref):
            o_ref[...] = g_ref[...] + r_ref[...]

        spec = pl.BlockSpec((tr, c), lambda i, core_ref: (i, 0))
        return pl.pallas_call(
            body_all, name=name,
            grid_spec=pltpu.PrefetchScalarGridSpec(
                num_scalar_prefetch=1, grid=(nb,), in_specs=[spec, spec], out_specs=spec),
            out_shape=jax.ShapeDtypeStruct((r, c), F32),
            compiler_params=_params(("parallel",)),
        )(core, g, r4)

    def body(core_ref, g_ref, r_ref, o_ref):
        o_ref[...] = (g_ref[...].astype(F32) + r_ref[...].astype(F32)).astype(BF16)

    if kind == "col":
        g_spec = pl.BlockSpec((tr, c), lambda ch, i, core_ref: (i, 2 * ch + core_ref[0]))
    else:
        g_spec = pl.BlockSpec((tr, c), lambda ch, i, core_ref: ((2 * ch + core_ref[0]) * nb + i, 0))
    slot = pl.BlockSpec((None, tr, c), lambda ch, i, core_ref: (ch, i, 0))
    return pl.pallas_call(
        body, name=name,
        grid_spec=pltpu.PrefetchScalarGridSpec(
            num_scalar_prefetch=1, grid=(N_CHIP, nb), in_specs=[g_spec, slot], out_specs=slot),
        out_shape=jax.ShapeDtypeStruct((N_CHIP, r, c), BF16),
        compiler_params=_params(("parallel", "parallel")),
    )(core, g, r4)


def _reduce_scatter(grads, kinds, shard_shapes, core, tag):
    n = len(grads)
    land = []
    for g, k, s in zip(grads, kinds, shard_shapes):
        land.append(jax.ShapeDtypeStruct(s if k == "all" else (N_CHIP,) + tuple(s), g.dtype))
    n_rem = sum(1 if k == "all" else N_CHIP for k in kinds)

    def plan_d2d(me, peers, ins, outs):
        remote = []
        for a in range(n):
            if kinds[a] == "all":
                remote.append((ins[a], outs[a], 0, outs[a]))
                continue
            for ch in range(N_CHIP):
                remote.append((_window(ins[a], kinds[a], 2 * ch + 1 - me.c, shard_shapes[a]),
                               outs[a].at[ch], 0, outs[a].at[ch]))
        return remote, []

    got = _exchange("scatter_d2d_" + tag, "d2d", list(grads), land, {}, n_rem, 0, plan_d2d)
    sums = [_pair_add(g, r4, k, s, core, name="pair_add_" + tag)
            for g, r4, k, s in zip(grads, got, kinds, shard_shapes)]
    land2 = [jax.ShapeDtypeStruct((N_CHIP,) + tuple(s), F32 if k == "all" else BF16)
             for k, s in zip(kinds, shard_shapes)]

    def plan_ici(me, peers, ins, outs):
        remote, local = [], []
        for a in range(n):
            for k, p in enumerate(peers):
                src = ins[a] if kinds[a] == "all" else ins[a].at[p.chip]
                remote.append((src, outs[a].at[me.chip], k, outs[a].at[p.chip]))
            src = ins[a] if kinds[a] == "all" else ins[a].at[me.chip]
            local.append((src, outs[a].at[me.chip]))
        return remote, local

    return _exchange("scatter_ici_" + tag, "ici", sums, land2, {}, 3 * n, n, plan_ici)


def _row_tile(r, c, limit=262144):
    best = None
    for t in range(8, r + 1, 8):
        if r % t == 0 and t * c <= limit:
            best = t
    return best if best is not None else r


def _adam_math(w, g, m, v):
    m = ADAM_B1 * m + (1.0 - ADAM_B1) * g
    v = ADAM_B2 * v + (1.0 - ADAM_B2) * (g * g)
    m_hat = m / (1.0 - ADAM_B1 ** ADAM_STEP)
    v_hat = v / (1.0 - ADAM_B2 ** ADAM_STEP)
    delta = -ADAM_LR * (m_hat / (jnp.sqrt(v_hat) + ADAM_EPS) + ADAM_WD * w)
    return delta, m, v


def _adamw(w, m, v, sums, lands, chip, layer, prev, *, name):
    L, r, c = w.shape
    tr = _row_tile(r, c)
    has_prev = prev is not None

    def body(chip_ref, *refs):
        w_ref, m_ref, v_ref, s_ref, l_ref = refs[:5]
        g_ref, d_ref, nm_ref, nv_ref = refs[-4:]
        g = (s_ref[...].astype(F32) + l_ref[0].astype(F32)) + l_ref[1].astype(F32) + l_ref[2].astype(F32)
        delta, nm, nv = _adam_math(w_ref[...], g, m_ref[...], v_ref[...])
        g_ref[...] = g
        d_ref[...] = delta
        nm_ref[...] = nm
        nv_ref[...] = nv

    slab = pl.BlockSpec((None, tr, c), lambda i, chip_ref: (layer, i, 0))
    in_specs = [slab, slab, slab, pl.BlockSpec((None, tr, c), lambda i, chip_ref: (chip_ref[0], i, 0)),
                pl.BlockSpec((N_CHIP - 1, tr, c), lambda i, chip_ref: (0, i, 0))]
    args = [chip, w, m, v, sums, lands]
    aliases = {}
    if has_prev:
        in_specs += [ANY] * 4
        args += list(prev)
        aliases = {6 + n: n for n in range(4)}
    return pl.pallas_call(
        body, name=name,
        grid_spec=pltpu.PrefetchScalarGridSpec(
            num_scalar_prefetch=1, grid=(r // tr,), in_specs=in_specs, out_specs=(slab,) * 4),
        out_shape=tuple(jax.ShapeDtypeStruct((L, r, c), F32) for _ in range(4)),
        input_output_aliases=aliases, compiler_params=_params(("parallel",)),
    )(*args)


def _adamw_small(w, g, m, v):
    def body(w_ref, g_ref, m_ref, v_ref, d_ref, nm_ref, nv_ref):
        d_ref[...], nm_ref[...], nv_ref[...] = _adam_math(w_ref[...], g_ref[...], m_ref[...], v_ref[...])

    return pl.pallas_call(
        body, name="adamw_small", out_shape=tuple(jax.ShapeDtypeStruct(w.shape, F32) for _ in range(3)),
    )(w, g, m, v)


def _sum4(parts):
    def body(p_ref, o_ref):
        o_ref[...] = (p_ref[0] + p_ref[1]) + p_ref[2] + p_ref[3]

    return pl.pallas_call(
        body, name="sum_chips", out_shape=jax.ShapeDtypeStruct(parts.shape[1:], F32))(parts)


_WEIGHTS = ["mix_norm", "mlp_norm", "sb_w_in", "sb_w_out", "fox_w_in", "fox_b_f", "fox_q_gain", "fox_k_gain",
            "fox_w_out", "mla_w_in", "mla_q_norm", "mla_kv_norm", "mla_w_uq", "mla_w_ukv", "mla_q_gain",
            "mla_k_gain", "mla_w_out", "mlp_w1", "mlp_w2"]
_BIG = {"sb_w_in": "col", "sb_w_out": "row", "fox_w_in": "row", "fox_w_out": "row", "mla_w_in": "row",
        "mla_w_uq": "col", "mla_w_ukv": "col", "mla_w_out": "row", "mlp_w1": "col", "mlp_w2": "row"}


def _layer_big(i):
    kind, j = i % 3, i // 3
    mixer = {0: ["sb_w_in", "sb_w_out"], 1: ["fox_w_in", "fox_w_out"],
             2: ["mla_w_in", "mla_w_uq", "mla_w_ukv", "mla_w_out"]}[kind]
    return [(n, j) for n in mixer] + [("mlp_w1", i), ("mlp_w2", i)]


def _stack_to_cols(a):
    r = a.shape[0] // N_DEV
    return a.reshape(N_DEV, r, a.shape[1]).transpose(1, 0, 2).reshape(r, N_DEV * a.shape[1])


def _cols_to_stack(a):
    c = a.shape[1] // N_DEV
    return a.reshape(a.shape[0], N_DEV, c).transpose(1, 0, 2).reshape(N_DEV * a.shape[0], c)


def _pack_rows(rows, width):
    rows = [jnp.pad(r.reshape(-1).astype(F32), (0, width - r.size)) for r in rows]
    pad = (-len(rows)) % 8
    rows += [jnp.zeros((width,), F32)] * pad
    return jnp.stack(rows)


def kernel(x, positions, mix_norm, mlp_norm, sb_w_in, sb_w_out, fox_w_in, fox_b_f, fox_q_gain, fox_k_gain, fox_w_out, mla_w_in, mla_q_norm, mla_kv_norm, mla_w_uq, mla_w_ukv, mla_q_gain, mla_k_gain, mla_w_out, mlp_w1, mlp_w2, loss_target, m_mix_norm, m_mlp_norm, m_sb_w_in, m_sb_w_out, m_fox_w_in, m_fox_b_f, m_fox_q_gain, m_fox_k_gain, m_fox_w_out, m_mla_w_in, m_mla_q_norm, m_mla_kv_norm, m_mla_w_uq, m_mla_w_ukv, m_mla_q_gain, m_mla_k_gain, m_mla_w_out, m_mlp_w1, m_mlp_w2, v_mix_norm, v_mlp_norm, v_sb_w_in, v_sb_w_out, v_fox_w_in, v_fox_b_f, v_fox_q_gain, v_fox_k_gain, v_fox_w_out, v_mla_w_in, v_mla_q_norm, v_mla_kv_norm, v_mla_w_uq, v_mla_w_ukv, v_mla_q_gain, v_mla_k_gain, v_mla_w_out, v_mlp_w1, v_mlp_w2):
    w_in = dict(zip(_WEIGHTS, (mix_norm, mlp_norm, sb_w_in, sb_w_out, fox_w_in, fox_b_f, fox_q_gain, fox_k_gain, fox_w_out, mla_w_in, mla_q_norm, mla_kv_norm, mla_w_uq, mla_w_ukv, mla_q_gain, mla_k_gain, mla_w_out, mlp_w1, mlp_w2)))
    m_in = dict(zip(_WEIGHTS, (m_mix_norm, m_mlp_norm, m_sb_w_in, m_sb_w_out, m_fox_w_in, m_fox_b_f, m_fox_q_gain, m_fox_k_gain, m_fox_w_out, m_mla_w_in, m_mla_q_norm, m_mla_kv_norm, m_mla_w_uq, m_mla_w_ukv, m_mla_q_gain, m_mla_k_gain, m_mla_w_out, m_mlp_w1, m_mlp_w2)))
    v_in = dict(zip(_WEIGHTS, (v_mix_norm, v_mlp_norm, v_sb_w_in, v_sb_w_out, v_fox_w_in, v_fox_b_f, v_fox_q_gain, v_fox_k_gain, v_fox_w_out, v_mla_w_in, v_mla_q_norm, v_mla_kv_norm, v_mla_w_uq, v_mla_w_ukv, v_mla_q_gain, v_mla_k_gain, v_mla_w_out, v_mlp_w1, v_mlp_w2)))
    depth, D = mix_norm.shape
    H = D // HEAD_DIM
    n_mla = mla_w_in.shape[0]
    dev = 4 * lax.axis_index("x") + 2 * lax.axis_index("y") + lax.axis_index("c")
    core = lax.axis_index("c").astype(jnp.int32).reshape(1)
    dev_arr = dev.astype(jnp.int32).reshape(1)

    chip = (2 * lax.axis_index("x") + lax.axis_index("y")).astype(jnp.int32).reshape(1)
    nq, nkv = mla_q_norm.shape[1], mla_kv_norm.shape[1]

    units, u_mix, u_mlp = [], {}, {}
    for i in range(depth):
        big = _layer_big(i)
        u_mix[i] = len(units)
        if i < SPLIT_LAYERS:
            units += [(i, ("mix",), big[:-2]), (i, ("mlp",), big[-2:])]
        else:
            units.append((i, ("mix", "mlp"), big))
        u_mlp[i] = len(units) - 1

    def split(i):
        return u_mix[i] != u_mlp[i]

    def tag_of(u):
        i, parts, _ = units[u]
        return f"l{i}" if len(parts) == 2 else f"l{i}_{parts[0]}"

    kinds_of, shapes_of, ici = [], [], []
    for i, parts, names in units:
        shards = [w_in[n][j] for n, j in names]
        kinds = [_BIG[n] for n, _ in names]
        behind = (ici[-1].token,) if ici else ()
        fulls = [_own_in_place(s, k, dev_arr, BF16, behind) for s, k in zip(shards, kinds)]
        if i % 3 == 2:
            shards.append(_pack_rows([mla_q_norm[i // 3], mla_kv_norm[i // 3]], LANE))
            kinds.append("row")
            fulls.append(_own_in_place(shards[-1], "row", dev_arr, F32, behind))
        kinds_of.append(kinds)
        shapes_of.append([s.shape for s in shards])
        ici.append(_gather_ici_start(fulls, kinds, shapes_of[-1], tag_of(len(ici)),
                                     ici[-1].token if ici else None))

    full = {n: [None] * w_in[n].shape[0] for n in _WEIGHTS}
    for n in ("mix_norm", "mlp_norm", "fox_b_f", "fox_q_gain", "fox_k_gain", "mla_q_gain", "mla_k_gain"):
        full[n] = w_in[n]
    P = {"mix_norm": mix_norm, "mlp_norm": mlp_norm, "mlp": [None] * depth,
         "sb": [None] * sb_w_in.shape[0], "fox": [None] * fox_w_in.shape[0], "mla": [None] * n_mla}

    def forward_to_sibling(u, after):
        arrived = _wait_copies(ici[u], after)
        return _gather_d2d_start(arrived, kinds_of[u], shapes_of[u], tag_of(u))

    def finish_gather(u, flight, after):
        i, parts, names = units[u]
        got = _wait_copies(flight, after)
        for (n, j), a in zip(names, got):
            full[n][j] = _stack_to_cols(a) if n == "fox_w_in" else a
        if i % 3 == 2:
            tiles = got[-1].reshape(N_DEV, 8, LANE)
            full["mla_q_norm"][i // 3] = tiles[:, 0, :nq].reshape(-1)
            full["mla_kv_norm"][i // 3] = tiles[:, 1, :nkv].reshape(-1)
        _prepare_layer(P, full, i, H, parts)

    tabs = _rope_tables(positions[0])
    d2d = forward_to_sibling(0, ici[-1].token)
    finish_gather(0, d2d, d2d.token)
    xs, saved = x[0], []
    for i in range(depth):
        nxt = {}

        def hook(x_mid, i=i, nxt=nxt):
            if split(i):
                mlp_d2d = forward_to_sibling(u_mlp[i], x_mid)
                finish_gather(u_mlp[i], mlp_d2d, mlp_d2d.token)
            if 1 <= i < depth - 1 and not split(i + 1):
                nxt["d2d"] = forward_to_sibling(u_mix[i + 1], x_mid)
                return (nxt["d2d"].token,)
            return ()

        xs, s = _layer_fwd(i, xs, P, tabs, H, hook)
        saved.append(s)
        if i + 1 < depth:
            if "d2d" in nxt:
                finish_gather(u_mix[i + 1], nxt["d2d"], xs)
            else:
                mix_d2d = forward_to_sibling(u_mix[i + 1], xs)
                finish_gather(u_mix[i + 1], mix_d2d, mix_d2d.token)
    dx, dxb, loss = _loss_head(xs, loss_target[0])
    loss = lax.psum(loss[0, 0], ("x", "y", "c"))

    def unit_meta(u):
        names = units[u][2]
        return [_BIG[n] for n, _ in names], [w_in[n].shape[1:] for n, _ in names]

    def to_sibling_start(u, Gl):
        grads = [(_cols_to_stack(Gl[n]) if n == "fox_w_in" else Gl[n]) for n, _ in units[u][2]]
        return _scatter_d2d_start(grads, *unit_meta(u), tag_of(u))

    def pair_and_send(u, flight, after, before=None):
        n = len(units[u][2])
        kinds, shapes = unit_meta(u)
        got = _wait_copies(flight, after)
        sums = [_pair_add(g, r4, k, s, core, name="pair_add_" + tag_of(u))
                for g, r4, k, s in zip(got[:n], got[n:], kinds, shapes)]
        return _scatter_ici_start(sums, shapes, tag_of(u), before)

    G, to_chips, deps, flying = [None] * depth, [None] * len(units), (), {}
    for i in reversed(range(depth)):
        def hook(dxb_mid, Gp, i=i):
            out = ()
            if i + 1 < depth:
                for u in sorted({u_mlp[i + 1], u_mix[i + 1]}, reverse=True):
                    to_chips[u] = pair_and_send(u, flying.pop(u), dxb_mid)
                    out = out + (to_chips[u].token,)
            if split(i):
                flying[u_mlp[i]] = to_sibling_start(u_mlp[i], {"mlp_w1": Gp["w1"], "mlp_w2": Gp["w2"]})
                out = out + (flying[u_mlp[i]].token,)
            return out

        dx, dxb, Gp = _layer_bwd(i, dx, dxb, saved[i], P, tabs, H, deps, hook)
        G[i] = _layer_grads(i, Gp, H)
        flying[u_mix[i]] = to_sibling_start(u_mix[i], G[i])
        deps = (flying[u_mix[i]].token,)
    grad_x = dx

    small = [n for n in _WEIGHTS if n not in _BIG]
    rows = []
    for n in small:
        for j in range(w_in[n].shape[0]):
            rows.append(G[_layer_of(n, j)][n])
    sg = _pack_rows(rows, D)
    (parts,) = _reduce_scatter([sg], ["all"], [sg.shape], core, "small")
    sgr = _sum4(parts)
    after, before = dx, sgr
    for u in sorted({u_mlp[0], u_mix[0]}, reverse=True):
        to_chips[u] = pair_and_send(u, flying.pop(u), after, before)
        after, before = to_chips[u].token, None

    results = {}
    for u in reversed(range(len(units))):
        names = units[u][2]
        got = _wait_copies(to_chips[u], to_chips[0].token)
        for a, (n, j) in enumerate(names):
            results[n] = _adamw(w_in[n], m_in[n], v_in[n], got[a], got[len(names) + a], chip, j, results.get(n),
                                name="adamw_" + n)

    g_rows, w_rows, m_rows, v_rows, at = [], [], [], [], 0
    for n in small:
        for j in range(w_in[n].shape[0]):
            width = w_in[n].shape[1]
            if n in ("mla_q_norm", "mla_kv_norm"):
                g_rows.append(lax.dynamic_slice(sgr[at], (dev * width,), (width,)))
            else:
                g_rows.append(sgr[at, :width])
            w_rows.append(w_in[n][j])
            m_rows.append(m_in[n][j])
            v_rows.append(v_in[n][j])
            at += 1
    g_pack = _pack_rows(g_rows, D)
    d_pack, nm_pack, nv_pack = _adamw_small(
        _pack_rows(w_rows, D), g_pack, _pack_rows(m_rows, D), _pack_rows(v_rows, D))
    at = 0
    for n in small:
        L, width = w_in[n].shape
        results[n] = tuple(p[at:at + L, :width] for p in (g_pack, d_pack, nm_pack, nv_pack))
        at += L

    out = [loss, grad_x[None]]
    for part in range(4):
        out += [results[n][part] for n in _WEIGHTS]
    return tuple(out)
```

```python
import functools
import math

import jax
import jax.numpy as jnp
from jax import lax
from jax.experimental import pallas as pl
from jax.experimental.pallas import tpu as pltpu

F32 = jnp.float32
BF16 = jnp.bfloat16

HEAD_DIM = 128
MLA_NOPE = 128
MLA_ROPE = 64
MLA_V = 128
MLA_QK = MLA_NOPE + MLA_ROPE
MLA_QK_PAD = 256
LANE = 128
ROPE_THETA = 10000.0
EPS = 1e-6
ADAM_LR = 0.001
ADAM_B1 = 0.9
ADAM_B2 = 0.999
ADAM_EPS = 1e-08
ADAM_WD = 0.01
ADAM_STEP = 10
N_DEV = 8
N_CHIP = 4
NEG = -1e30
VMEM_LIMIT = 56 * 1024 * 1024
ATT_BLOCK = 256
SB_BLOCK = 256
ATT_HEADS = 2
SPLIT_LAYERS = 2
SB_PARTS = 2
MESH = pl.DeviceIdType.MESH
ANY = pl.BlockSpec(memory_space=pl.ANY)


def _params(sem):
    return pltpu.CompilerParams(dimension_semantics=sem, vmem_limit_bytes=VMEM_LIMIT)


def _tile(dim, pref):
    if dim <= pref:
        return dim
    t = pref
    while dim % t:
        t -= LANE
    return t


def _dot(a, b, dims):
    return lax.dot_general(a, b, (dims, ((), ())), preferred_element_type=F32)


NN = ((1,), (0,))
NT = ((1,), (1,))
TN = ((0,), (0,))


def _mm(a, b, *, mode="nn", out_dtype=F32, res=None, act=None, z=None, name,
        tm=1024, tn=1024, tk=2048, deps=()):
    if mode == "nn":
        (M, K), (_, N) = a.shape, b.shape
    elif mode == "nt":
        (M, K), (N, _) = a.shape, b.shape
    else:
        (K, M), (_, N) = a.shape, b.shape
    tm, tn, tk = _tile(M, tm), _tile(N, tn), _tile(K, tk)
    nk = K // tk
    if mode == "tn":
        a_spec = pl.BlockSpec((tk, tm), lambda i, j, k: (k, i))
    else:
        a_spec = pl.BlockSpec((tm, tk), lambda i, j, k: (i, k))
    if mode == "nt":
        b_spec = pl.BlockSpec((tn, tk), lambda i, j, k: (j, k))
    else:
        b_spec = pl.BlockSpec((tk, tn), lambda i, j, k: (k, j))
    dims = {"nn": NN, "nt": NT, "tn": TN}[mode]
    o_spec = pl.BlockSpec((tm, tn), lambda i, j, k: (i, j))
    in_specs, args = [a_spec, b_spec], [a, b]
    if res is not None:
        in_specs.append(o_spec)
        args.append(res)
    if act == "drelu2":
        in_specs.append(o_spec)
        args.append(z)
    if act == "relu2":
        out_shape = (jax.ShapeDtypeStruct((M, N), F32), jax.ShapeDtypeStruct((M, N), BF16))
        out_specs = (o_spec, o_spec)
    else:
        out_shape = jax.ShapeDtypeStruct((M, N), out_dtype)
        out_specs = o_spec
    has_res, has_z = res is not None, act == "drelu2"
    n_out = 2 if act == "relu2" else 1
    in_specs += [ANY] * len(deps)
    args += list(deps)

    def body(*refs):
        a_ref, b_ref = refs[0], refs[1]
        idx = 2
        res_ref = z_ref = None
        if has_res:
            res_ref = refs[idx]
            idx += 1
        if has_z:
            z_ref = refs[idx]
            idx += 1
        idx += len(deps)
        outs = refs[idx:idx + n_out]

        def finish(r):
            if has_res:
                r = r + res_ref[...]
            if act == "relu2":
                outs[0][...] = r
                rr = jnp.maximum(r, 0.0)
                outs[1][...] = (rr * rr).astype(BF16)
            elif act == "drelu2":
                outs[0][...] = (r * (2.0 * jnp.maximum(z_ref[...], 0.0))).astype(out_dtype)
            else:
                outs[0][...] = r.astype(out_dtype)

        if nk == 1:
            finish(_dot(a_ref[...], b_ref[...], dims))
            return
        acc = refs[-1]
        k = pl.program_id(2)

        @pl.when(k == 0)
        def _():
            acc[...] = _dot(a_ref[...], b_ref[...], dims)

        @pl.when((k > 0) & (k < nk - 1))
        def _():
            acc[...] += _dot(a_ref[...], b_ref[...], dims)

        @pl.when(k == nk - 1)
        def _():
            finish(acc[...] + _dot(a_ref[...], b_ref[...], dims))

    return pl.pallas_call(
        body, name=name, grid=(M // tm, N // tn, nk), in_specs=in_specs, out_specs=out_specs,
        out_shape=out_shape, scratch_shapes=[pltpu.VMEM((tm, tn), F32)] if nk > 1 else [],
        compiler_params=_params(("parallel", "parallel", "arbitrary")),
    )(*args)


def _rmsnorm_fwd(x, g, *, name, tm=256):
    T, n = x.shape
    tm = _tile(T, tm)

    def body(x_ref, g_ref, o_ref):
        xf = x_ref[...]
        r = lax.rsqrt(jnp.mean(xf * xf, axis=-1, keepdims=True) + EPS)
        o_ref[...] = (xf * r * g_ref[...]).astype(BF16)

    return pl.pallas_call(
        body, name=name, grid=(T // tm,),
        in_specs=[pl.BlockSpec((tm, n), lambda i: (i, 0)), pl.BlockSpec((1, n), lambda i: (0, 0))],
        out_specs=pl.BlockSpec((tm, n), lambda i: (i, 0)),
        out_shape=jax.ShapeDtypeStruct((T, n), BF16),
        compiler_params=_params(("parallel",)),
    )(x, g)


def _rmsnorm_bwd(x, g, dy, dx_in=None, *, name, want_f32=True, tm=256):
    T, n = x.shape
    tm = _tile(T, tm)
    has_in = dx_in is not None
    row = pl.BlockSpec((tm, n), lambda i: (i, 0))
    vec = pl.BlockSpec((1, n), lambda i: (0, 0))

    def body(*refs):
        x_ref, g_ref, dy_ref = refs[:3]
        idx = 3
        in_ref = None
        if has_in:
            in_ref = refs[idx]
            idx += 1
        outs = refs[idx:]
        xf = x_ref[...]
        r = lax.rsqrt(jnp.mean(xf * xf, axis=-1, keepdims=True) + EPS)
        dyf = dy_ref[...]
        t = dyf * g_ref[...]
        dx = r * t - xf * (r * r * r * jnp.mean(t * xf, axis=-1, keepdims=True))
        if has_in:
            dx = dx + in_ref[...]
        o = 0
        if want_f32:
            outs[0][...] = dx
            o = 1
        outs[o][...] = dx.astype(BF16)
        dg_ref = outs[o + 1]

        @pl.when(pl.program_id(0) == 0)
        def _():
            dg_ref[...] = jnp.zeros_like(dg_ref)

        dg_ref[...] += jnp.sum(dyf * xf * r, axis=0, keepdims=True)

    in_specs = [row, vec, row] + ([row] if has_in else [])
    out_specs, out_shape = [], []
    if want_f32:
        out_specs.append(row)
        out_shape.append(jax.ShapeDtypeStruct((T, n), F32))
    out_specs += [row, vec]
    out_shape += [jax.ShapeDtypeStruct((T, n), BF16), jax.ShapeDtypeStruct((1, n), F32)]
    args = [x, g, dy] + ([dx_in] if has_in else [])
    return pl.pallas_call(
        body, name=name, grid=(T // tm,), in_specs=in_specs, out_specs=tuple(out_specs),
        out_shape=tuple(out_shape), compiler_params=_params(("arbitrary",)),
    )(*args)


def _loss_head(y, target, *, tm=256):
    T, n = y.shape
    tm = _tile(T, tm)
    row = pl.BlockSpec((tm, n), lambda i: (i, 0))

    def body(y_ref, t_ref, dy_ref, dyb_ref, loss_ref):
        err = y_ref[...] - t_ref[...]
        dy = err * (1.0 / n)
        dy_ref[...] = dy
        dyb_ref[...] = dy.astype(BF16)

        @pl.when(pl.program_id(0) == 0)
        def _():
            loss_ref[...] = jnp.zeros_like(loss_ref)

        part = 0.5 * jnp.sum(jnp.mean(err * err, axis=-1, keepdims=True), axis=0, keepdims=True)
        loss_ref[...] += part

    return pl.pallas_call(
        body, name="loss_head", grid=(T // tm,), in_specs=[row, row],
        out_specs=(row, row, pl.BlockSpec((1, 1), lambda i: (0, 0))),
        out_shape=(jax.ShapeDtypeStruct((T, n), F32), jax.ShapeDtypeStruct((T, n), BF16),
                   jax.ShapeDtypeStruct((1, 1), F32)),
        compiler_params=_params(("arbitrary",)),
    )(y, target)


def _split(v, n):
    parts = []
    for _ in range(n - 1):
        p = v.astype(BF16)
        parts.append(p)
        v = v - p.astype(F32)
    parts.append(v.astype(BF16))
    return parts


def _tri_left(tri, v):
    hi, mid, lo = _split(v, 3)
    return _dot(tri, hi, NN) + _dot(tri, mid, NN) + _dot(tri, lo, NN)


def _iota2(shape, dim):
    return lax.broadcasted_iota(jnp.int32, shape, dim)


def _log_sigmoid(z):
    return jnp.minimum(z, 0.0) - jnp.log1p(jnp.exp(-jnp.abs(z)))


def _log_sigmoid_abs(z):
    return jnp.minimum(z, 0.0) - jnp.log(1.0 + jnp.exp(-jnp.abs(z)))


def _head_spec(rows, width, off):
    return pl.BlockSpec((rows, width), lambda h, i: (i, off + h))


def _full_head_spec(T, width, off):
    return pl.BlockSpec((T, width), lambda h, i: (0, off + h))


def _sb_fwd(qkv, H):
    T = qkv.shape[0]
    B = _tile(T, SB_BLOCK)
    nq = T // B
    scale = 1.0 / math.sqrt(HEAD_DIM)

    HG = ATT_HEADS if H % ATT_HEADS == 0 else 1
    HD = HEAD_DIM

    def body(q_ref, k_ref, v_ref, o_ref, c_ref, tri_s):
        i = pl.program_id(1)
        tri = (_iota2((B, B), 0) > _iota2((B, B), 1)).astype(BF16)
        for p in range(SB_PARTS):
            tri_s[pl.ds(p * B, B), :] = tri
        rows = _iota2((B, B), 0)
        cols = _iota2((B, B), 1)

        def step(n, carry):
            j = i - n
            ks = pl.multiple_of(j * B, B)
            mask = (ks + cols) < (i * B + rows)
            out = []
            for g in range(HG):
                acc, run = carry[g]
                q = q_ref[:, pl.ds(g * HD, HD)]
                kb = k_ref[pl.ds(ks, B), pl.ds(g * HD, HD)]
                vb = v_ref[pl.ds(ks, B), pl.ds(g * HD, HD)]
                z = _dot(q, kb, NT) * scale
                ls = _log_sigmoid_abs(z)
                lk = jnp.where(mask, ls - z, 0.0)
                parts = jnp.concatenate(_split(lk, SB_PARTS), axis=1)
                later = _dot(parts, tri_s[...], NN)
                a = jnp.exp(jnp.where(mask, ls, NEG) + later).astype(BF16)
                acc = acc + jnp.exp(run) * _dot(a, vb, NN)
                out.append((acc, run + jnp.sum(lk, axis=1, keepdims=True)))
            return tuple(out)

        init = tuple((jnp.zeros((B, HD), F32), jnp.zeros((B, 1), F32)) for _ in range(HG))
        res = lax.fori_loop(0, i + 1, step, init)
        for g in range(HG):
            o_ref[:, pl.ds(g * HD, HD)] = res[g][0].astype(BF16)
            c_ref[g] = res[g][1]

    return pl.pallas_call(
        body, name="sb_attn_fwd", grid=(H // HG, nq),
        in_specs=[_head_spec(B, HG * HD, 0), _full_head_spec(T, HG * HD, H // HG),
                  _full_head_spec(T, HG * HD, 2 * H // HG)],
        out_specs=(_head_spec(B, HG * HD, 0), pl.BlockSpec((HG, B, 1), lambda h, i: (h, i, 0))),
        out_shape=(jax.ShapeDtypeStruct((T, H * HD), BF16), jax.ShapeDtypeStruct((H, T, 1), F32)),
        scratch_shapes=[pltpu.VMEM((SB_PARTS * B, B), BF16)],
        compiler_params=_params(("parallel", "arbitrary")),
    )(qkv, qkv, qkv)


def _sb_bwd(qkv, do, ctot, H):
    T = qkv.shape[0]
    B = _tile(T, SB_BLOCK)
    nq = T // B
    scale = 1.0 / math.sqrt(HEAD_DIM)
    HG = ATT_HEADS if H % ATT_HEADS == 0 else 1
    HD = HEAD_DIM

    def body(q_ref, k_ref, v_ref, do_ref, c_ref, dq_ref, dk_ref, dv_ref, dk_acc, dv_acc,
             tri_incl_s, tri_strict_s):
        i = pl.program_id(1)

        @pl.when(i == 0)
        def _():
            dk_acc[...] = jnp.zeros_like(dk_acc)
            dv_acc[...] = jnp.zeros_like(dv_acc)

        rows = _iota2((B, B), 0)
        cols = _iota2((B, B), 1)
        tri_incl = (rows <= cols).astype(BF16)
        tri_strict = (rows < cols).astype(BF16)
        for p in range(SB_PARTS):
            tri_incl_s[pl.ds(p * B, B), :] = tri_incl
            tri_strict_s[pl.ds(p * B, B), :] = tri_strict
        heads = [(q_ref[:, pl.ds(g * HD, HD)], do_ref[:, pl.ds(g * HD, HD)], c_ref[g]) for g in range(HG)]

        def step(j, carry):
            ks = pl.multiple_of(j * B, B)
            mask = (ks + cols) < (i * B + rows)
            out = []
            for g in range(HG):
                dq, lpre, gpre = carry[g]
                q, do_b, ctot_b = heads[g]
                kb = k_ref[pl.ds(ks, B), pl.ds(g * HD, HD)]
                vb = v_ref[pl.ds(ks, B), pl.ds(g * HD, HD)]
                z = _dot(q, kb, NT) * scale
                da = _dot(do_b, vb, NT)
                ls = _log_sigmoid_abs(z)
                lk = jnp.where(mask, ls - z, 0.0)
                lsm = jnp.where(mask, ls, NEG)
                incl = _dot(jnp.concatenate(_split(lk, SB_PARTS), axis=1), tri_incl_s[...], NN)
                a = jnp.exp(lsm + (ctot_b - lpre) - incl)
                gw = a * da
                gex = gpre + _dot(jnp.concatenate(_split(gw, SB_PARTS), axis=1), tri_strict_s[...], NN)
                dzb = ((gw * jnp.exp(lsm - z) - jnp.exp(lsm) * gex) * scale).astype(BF16)
                dq = dq + _dot(dzb, kb, NN)
                dk_acc[pl.ds(ks, B), pl.ds(g * HD, HD)] += _dot(dzb, q, TN)
                dv_acc[pl.ds(ks, B), pl.ds(g * HD, HD)] += _dot(a.astype(BF16), do_b, TN)
                out.append((dq, lpre + jnp.sum(lk, axis=1, keepdims=True),
                            gpre + jnp.sum(gw, axis=1, keepdims=True)))
            return tuple(out)

        init = tuple((jnp.zeros((B, HD), F32), jnp.zeros((B, 1), F32), jnp.zeros((B, 1), F32))
                     for _ in range(HG))
        res = lax.fori_loop(0, i + 1, step, init)
        for g in range(HG):
            dq_ref[:, pl.ds(g * HD, HD)] = res[g][0].astype(BF16)

        @pl.when(i == nq - 1)
        def _():
            dk_ref[...] = dk_acc[...].astype(BF16)
            dv_ref[...] = dv_acc[...].astype(BF16)

    W = H * HD
    return pl.pallas_call(
        body, name="sb_attn_bwd", grid=(H // HG, nq),
        in_specs=[_head_spec(B, HG * HD, 0), _full_head_spec(T, HG * HD, H // HG),
                  _full_head_spec(T, HG * HD, 2 * H // HG), _head_spec(B, HG * HD, 0),
                  pl.BlockSpec((HG, B, 1), lambda h, i: (h, i, 0))],
        out_specs=(_head_spec(B, HG * HD, 0), _full_head_spec(T, HG * HD, 0),
                   _full_head_spec(T, HG * HD, 0)),
        out_shape=tuple(jax.ShapeDtypeStruct((T, W), BF16) for _ in range(3)),
        scratch_shapes=[pltpu.VMEM((T, HG * HD), F32), pltpu.VMEM((T, HG * HD), F32),
                        pltpu.VMEM((SB_PARTS * B, B), BF16), pltpu.VMEM((SB_PARTS * B, B), BF16)],
        compiler_params=_params(("parallel", "arbitrary")),
    )(qkv, qkv, qkv, do, ctot)


def _softmax_fwd(q, k, v, cf, H, dqk, scale, *, name):
    T = q.shape[0]
    B = _tile(T, ATT_BLOCK)
    nq = T // B
    has_cf = cf is not None
    HG = ATT_HEADS if H % ATT_HEADS == 0 else 1
    HD = HEAD_DIM

    def body(*refs):
        q_ref, k_ref, v_ref = refs[:3]
        idx = 3
        if has_cf:
            cfc_ref, cfr_ref = refs[3], refs[4]
            idx = 5
        o_ref, of_ref, lse_ref = refs[idx:idx + 3]
        i = pl.program_id(1)
        rows = _iota2((B, B), 0)
        cols = _iota2((B, B), 1)

        def step(j, carry):
            ks = pl.multiple_of(j * B, B)
            mask = (ks + cols) <= (i * B + rows)
            out = []
            for g in range(HG):
                m, l, acc = carry[g]
                qb = q_ref[:, pl.ds(g * dqk, dqk)]
                kb = k_ref[pl.ds(ks, B), pl.ds(g * dqk, dqk)]
                vb = v_ref[pl.ds(ks, B), pl.ds(g * HD, HD)]
                s = _dot(qb, kb, NT) * scale
                if has_cf:
                    s = s + (cfc_ref[g] - cfr_ref[g, :, pl.ds(ks, B)])
                s = jnp.where(mask, s, NEG)
                m_new = jnp.maximum(m, jnp.max(s, axis=1, keepdims=True))
                alpha = jnp.exp(m - m_new)
                p = jnp.exp(s - m_new)
                l = alpha * l + jnp.sum(p, axis=1, keepdims=True)
                acc = alpha * acc + _dot(p.astype(BF16), vb, NN)
                out.append((m_new, l, acc))
            return tuple(out)

        init = tuple((jnp.full((B, 1), NEG, F32), jnp.zeros((B, 1), F32), jnp.zeros((B, HD), F32))
                     for _ in range(HG))
        res = lax.fori_loop(0, i + 1, step, init)
        for g in range(HG):
            m, l, acc = res[g]
            o = acc / l
            o_ref[:, pl.ds(g * HD, HD)] = o.astype(BF16)
            of_ref[:, pl.ds(g * HD, HD)] = o
            lse_ref[g] = m + jnp.log(l)

    stat = pl.BlockSpec((HG, B, 1), lambda h, i: (h, i, 0))
    in_specs = [_head_spec(B, HG * dqk, 0), _full_head_spec(T, HG * dqk, 0), _full_head_spec(T, HG * HD, 0)]
    args = [q, k, v]
    if has_cf:
        in_specs += [stat, pl.BlockSpec((HG, 1, T), lambda h, i: (h, 0, 0))]
        args += list(cf)
    W = H * HD
    return pl.pallas_call(
        body, name=name, grid=(H // HG, nq), in_specs=in_specs,
        out_specs=(_head_spec(B, HG * HD, 0), _head_spec(B, HG * HD, 0), stat),
        out_shape=(jax.ShapeDtypeStruct((T, W), BF16), jax.ShapeDtypeStruct((T, W), F32),
                   jax.ShapeDtypeStruct((H, T, 1), F32)),
        compiler_params=_params(("parallel", "arbitrary")),
    )(*args)


def _softmax_bwd(q, k, v, cf, do, o, lse, H, dqk, scale, *, name):
    T = q.shape[0]
    B = _tile(T, ATT_BLOCK)
    nq = T // B
    has_cf = cf is not None
    HG = ATT_HEADS if H % ATT_HEADS == 0 else 1
    HD = HEAD_DIM

    def body(*refs):
        q_ref, k_ref, v_ref, do_ref, o_ref, lse_ref = refs[:6]
        idx = 6
        if has_cf:
            cfc_ref, cfr_ref = refs[6], refs[7]
            idx = 8
        dq_ref, dk_ref, dv_ref = refs[idx:idx + 3]
        idx += 3
        if has_cf:
            dcc_ref, dcr_ref = refs[idx], refs[idx + 1]
        i = pl.program_id(1)

        @pl.when(i == 0)
        def _():
            dk_ref[...] = jnp.zeros_like(dk_ref)
            dv_ref[...] = jnp.zeros_like(dv_ref)
            if has_cf:
                dcr_ref[...] = jnp.zeros_like(dcr_ref)

        rows = _iota2((B, B), 0)
        cols = _iota2((B, B), 1)
        heads = []
        for g in range(HG):
            do_b = do_ref[:, pl.ds(g * HD, HD)]
            delta = jnp.sum(do_b.astype(F32) * o_ref[:, pl.ds(g * HD, HD)], axis=1, keepdims=True)
            heads.append((q_ref[:, pl.ds(g * dqk, dqk)], do_b, lse_ref[g], delta))

        def step(j, carry):
            ks = pl.multiple_of(j * B, B)
            mask = (ks + cols) <= (i * B + rows)
            out = []
            for g in range(HG):
                dq, rs = carry[g]
                qb, do_b, lse_b, delta = heads[g]
                kb = k_ref[pl.ds(ks, B), pl.ds(g * dqk, dqk)]
                vb = v_ref[pl.ds(ks, B), pl.ds(g * HD, HD)]
                s = _dot(qb, kb, NT) * scale
                if has_cf:
                    s = s + (cfc_ref[g] - cfr_ref[g, :, pl.ds(ks, B)])
                p = jnp.where(mask, jnp.exp(s - lse_b), 0.0)
                dp = _dot(do_b, vb, NT)
                ds = p * (dp - delta)
                dsb = (ds * scale).astype(BF16)
                dq = dq + _dot(dsb, kb, NN)
                dk_ref[pl.ds(ks, B), pl.ds(g * dqk, dqk)] += _dot(dsb, qb, TN)
                dv_ref[pl.ds(ks, B), pl.ds(g * HD, HD)] += _dot(p.astype(BF16), do_b, TN)
                if has_cf:
                    rs = rs + jnp.sum(ds, axis=1, keepdims=True)
                    dcr_ref[g, :, pl.ds(ks, B)] -= jnp.sum(ds, axis=0, keepdims=True)
                out.append((dq, rs))
            return tuple(out)

        init = tuple((jnp.zeros((B, dqk), F32), jnp.zeros((B, 1), F32)) for _ in range(HG))
        res = lax.fori_loop(0, i + 1, step, init)
        for g in range(HG):
            dq_ref[:, pl.ds(g * dqk, dqk)] = res[g][0]
            if has_cf:
                dcc_ref[g] = res[g][1]

    stat = pl.BlockSpec((HG, B, 1), lambda h, i: (h, i, 0))
    rowstat = pl.BlockSpec((HG, 1, T), lambda h, i: (h, 0, 0))
    in_specs = [_head_spec(B, HG * dqk, 0), _full_head_spec(T, HG * dqk, 0), _full_head_spec(T, HG * HD, 0),
                _head_spec(B, HG * HD, 0), _head_spec(B, HG * HD, 0), stat]
    args = [q, k, v, do, o, lse]
    out_specs = [_head_spec(B, HG * dqk, 0), _full_head_spec(T, HG * dqk, 0), _full_head_spec(T, HG * HD, 0)]
    out_shape = [jax.ShapeDtypeStruct((T, H * dqk), F32), jax.ShapeDtypeStruct((T, H * dqk), F32),
                 jax.ShapeDtypeStruct((T, H * HD), F32)]
    if has_cf:
        in_specs += [stat, rowstat]
        args += list(cf)
        out_specs += [stat, rowstat]
        out_shape += [jax.ShapeDtypeStruct((H, T, 1), F32), jax.ShapeDtypeStruct((H, 1, T), F32)]
    return pl.pallas_call(
        body, name=name, grid=(H // HG, nq), in_specs=in_specs, out_specs=tuple(out_specs),
        out_shape=tuple(out_shape), compiler_params=_params(("parallel", "arbitrary")),
    )(*args)


def _headnorm(x, g):
    r = lax.rsqrt(jnp.mean(x * x, axis=-1, keepdims=True) + EPS)
    return x * r * g, r


def _headnorm_bwd(x, g, dy, n):
    r = lax.rsqrt(jnp.sum(x * x, axis=-1, keepdims=True) * (1.0 / n) + EPS)
    t = dy * g
    dx = r * t - x * (r * r * r * (jnp.sum(t * x, axis=-1, keepdims=True) * (1.0 / n)))
    dg = jnp.sum(dy * x * r, axis=0, keepdims=True)
    return dx, dg


def _fox_prep_fwd(qkv, gq, gk, H, *, tm=512):
    T = qkv.shape[0]
    tm = _tile(T, tm)

    def body(q_ref, k_ref, v_ref, gq_ref, gk_ref, qn_ref, kn_ref, vb_ref):
        qn_ref[...] = _headnorm(q_ref[...], gq_ref[...])[0].astype(BF16)
        kn_ref[...] = _headnorm(k_ref[...], gk_ref[...])[0].astype(BF16)
        vb_ref[...] = v_ref[...].astype(BF16)

    def blk(off):
        return pl.BlockSpec((tm, HEAD_DIM), lambda i, h: (i, off + h))

    vec = pl.BlockSpec((1, HEAD_DIM), lambda i, h: (0, 0))
    W = H * HEAD_DIM
    return pl.pallas_call(
        body, name="fox_prep_fwd", grid=(T // tm, H),
        in_specs=[blk(0), blk(H), blk(2 * H), vec, vec], out_specs=(blk(0), blk(0), blk(0)),
        out_shape=tuple(jax.ShapeDtypeStruct((T, W), BF16) for _ in range(3)),
        compiler_params=_params(("parallel", "parallel")),
    )(qkv, qkv, qkv, gq, gk)


def _fox_prep_bwd(qkv, gq, gk, dqn, dkn, dv, H, *, tm=512):
    T = qkv.shape[0]
    tm = _tile(T, tm)

    def body(q_ref, k_ref, gq_ref, gk_ref, dqn_ref, dkn_ref, dv_ref,
             dq_ref, dk_ref, dvb_ref, dgq_ref, dgk_ref):
        @pl.when((pl.program_id(0) == 0) & (pl.program_id(1) == 0))
        def _():
            dgq_ref[...] = jnp.zeros_like(dgq_ref)
            dgk_ref[...] = jnp.zeros_like(dgk_ref)

        dq, dgq = _headnorm_bwd(q_ref[...], gq_ref[...], dqn_ref[...], HEAD_DIM)
        dk, dgk = _headnorm_bwd(k_ref[...], gk_ref[...], dkn_ref[...], HEAD_DIM)
        dq_ref[...] = dq.astype(BF16)
        dk_ref[...] = dk.astype(BF16)
        dvb_ref[...] = dv_ref[...].astype(BF16)
        dgq_ref[...] += dgq
        dgk_ref[...] += dgk

    def blk(off):
        return pl.BlockSpec((tm, HEAD_DIM), lambda i, h: (i, off + h))

    vec = pl.BlockSpec((1, HEAD_DIM), lambda i, h: (0, 0))
    W = H * HEAD_DIM
    return pl.pallas_call(
        body, name="fox_prep_bwd", grid=(T // tm, H),
        in_specs=[blk(0), blk(H), vec, vec, blk(0), blk(0), blk(0)],
        out_specs=(blk(0), blk(0), blk(0), vec, vec),
        out_shape=tuple(jax.ShapeDtypeStruct((T, W), BF16) for _ in range(3))
        + (jax.ShapeDtypeStruct((1, HEAD_DIM), F32), jax.ShapeDtypeStruct((1, HEAD_DIM), F32)),
        compiler_params=_params(("arbitrary", "arbitrary")),
    )(qkv, qkv, gq, gk, dqn, dkn, dv)


def _fox_gate_fwd(flog, bf, *, tm=256):
    T = flog.shape[0]
    tm = _tile(T, tm)

    def body(f_ref, b_ref, cf_ref, carry):
        @pl.when(pl.program_id(0) == 0)
        def _():
            carry[...] = jnp.zeros_like(carry)

        lf = _log_sigmoid(f_ref[...] + b_ref[...])
        tri = (_iota2((tm, tm), 1) <= _iota2((tm, tm), 0)).astype(BF16)
        cf_ref[...] = carry[...] + _tri_left(tri, lf)
        carry[...] += jnp.sum(lf, axis=0, keepdims=True)

    return pl.pallas_call(
        body, name="fox_gate_fwd", grid=(T // tm,),
        in_specs=[pl.BlockSpec((tm, LANE), lambda i: (i, 0)), pl.BlockSpec((1, LANE), lambda i: (0, 0))],
        out_specs=pl.BlockSpec((tm, LANE), lambda i: (i, 0)),
        out_shape=jax.ShapeDtypeStruct((T, LANE), F32),
        scratch_shapes=[pltpu.VMEM((1, LANE), F32)],
        compiler_params=_params(("arbitrary",)),
    )(flog, bf)


def _fox_gate_bwd(flog, bf, dcf, *, tm=256):
    T = flog.shape[0]
    tm = _tile(T, tm)
    nt = T // tm

    def body(f_ref, b_ref, dcf_ref, df_ref, db_ref, carry):
        @pl.when(pl.program_id(0) == 0)
        def _():
            carry[...] = jnp.zeros_like(carry)
            db_ref[...] = jnp.zeros_like(db_ref)

        d = dcf_ref[...]
        tri = (_iota2((tm, tm), 1) >= _iota2((tm, tm), 0)).astype(BF16)
        dlf = carry[...] + _tri_left(tri, d)
        carry[...] += jnp.sum(d, axis=0, keepdims=True)
        xg = f_ref[...] + b_ref[...]
        e = jnp.exp(-jnp.abs(xg))
        sig_neg = jnp.where(xg >= 0.0, e, 1.0) / (1.0 + e)
        df = dlf * sig_neg
        df_ref[...] = df.astype(BF16)
        db_ref[...] += jnp.sum(df, axis=0, keepdims=True)

    rev = pl.BlockSpec((tm, LANE), lambda i: (nt - 1 - i, 0))
    vec = pl.BlockSpec((1, LANE), lambda i: (0, 0))
    return pl.pallas_call(
        body, name="fox_gate_bwd", grid=(nt,), in_specs=[rev, vec, rev], out_specs=(rev, vec),
        out_shape=(jax.ShapeDtypeStruct((T, LANE), BF16), jax.ShapeDtypeStruct((1, LANE), F32)),
        scratch_shapes=[pltpu.VMEM((1, LANE), F32)],
        compiler_params=_params(("arbitrary",)),
    )(flog, bf, dcf)


def _rope_tables(positions):
    half = MLA_ROPE // 2
    inv_freq = ROPE_THETA ** (-jnp.arange(0, half, dtype=F32) * 2.0 / MLA_ROPE)
    ang = positions.astype(F32)[:, None] * inv_freq
    cos, sin = jnp.cos(ang), jnp.sin(ang)
    zero = jnp.zeros_like(cos)
    pad = jnp.zeros((positions.shape[0], LANE - MLA_ROPE), F32)
    cos_t = jnp.concatenate([cos, cos, pad], axis=1)
    sin_up = jnp.concatenate([zero, sin, pad], axis=1)
    sin_dn = jnp.concatenate([-sin, zero, pad], axis=1)
    return cos_t, sin_up, sin_dn


def _rope(x, cos_t, sin_up, sin_dn):
    half = MLA_ROPE // 2
    return x * cos_t + pltpu.roll(x, half, 1) * sin_up + pltpu.roll(x, LANE - half, 1) * sin_dn


def _rope_t(d, cos_t, sin_up, sin_dn):
    half = MLA_ROPE // 2
    return d * cos_t + pltpu.roll(d * sin_up, LANE - half, 1) + pltpu.roll(d * sin_dn, half, 1)


def _norm192(xcat, g):
    r = lax.rsqrt(jnp.sum(xcat * xcat, axis=-1, keepdims=True) * (1.0 / MLA_QK) + EPS)
    return xcat * r * g


def _mla_prep_fwd(qfull, kv, kr, tabs, gq, gk, H, *, tm=512):
    T = qfull.shape[0]
    tm = _tile(T, tm)

    def body(q_ref, kv_ref, kr_ref, c_ref, su_ref, sd_ref, gq_ref, gk_ref, qf_ref, kf_ref, v_ref):
        tabs_b = (c_ref[...], su_ref[...], sd_ref[...])
        qb = q_ref[...]
        qcat = jnp.concatenate([qb[:, :MLA_NOPE], _rope(qb[:, MLA_NOPE:], *tabs_b)], axis=1)
        qf_ref[...] = _norm192(qcat, gq_ref[...]).astype(BF16)
        kvb = kv_ref[...]
        kcat = jnp.concatenate([kvb[:, :MLA_NOPE], _rope(kr_ref[...], *tabs_b)], axis=1)
        kf_ref[...] = _norm192(kcat, gk_ref[...]).astype(BF16)
        v_ref[...] = kvb[:, MLA_NOPE:].astype(BF16)

    head = pl.BlockSpec((tm, MLA_QK_PAD), lambda i, h: (i, h))
    tok = pl.BlockSpec((tm, LANE), lambda i, h: (i, 0))
    vec = pl.BlockSpec((1, MLA_QK_PAD), lambda i, h: (0, 0))
    return pl.pallas_call(
        body, name="mla_prep_fwd", grid=(T // tm, H),
        in_specs=[head, head, tok, tok, tok, tok, vec, vec],
        out_specs=(head, head, pl.BlockSpec((tm, MLA_V), lambda i, h: (i, h))),
        out_shape=(jax.ShapeDtypeStruct((T, H * MLA_QK_PAD), BF16),
                   jax.ShapeDtypeStruct((T, H * MLA_QK_PAD), BF16),
                   jax.ShapeDtypeStruct((T, H * MLA_V), BF16)),
        compiler_params=_params(("parallel", "parallel")),
    )(qfull, kv, kr, *tabs, gq, gk)


def _mla_prep_bwd(qfull, kv, kr, tabs, gq, gk, dqf, dkf, dv, H, *, tm=512):
    T = qfull.shape[0]
    tm = _tile(T, tm)

    def body(q_ref, kv_ref, kr_ref, c_ref, su_ref, sd_ref, gq_ref, gk_ref, dqf_ref, dkf_ref, dv_ref,
             dq_ref, dkv_ref, dkr_ref, dgq_ref, dgk_ref, kr_acc):
        h = pl.program_id(1)

        @pl.when((pl.program_id(0) == 0) & (h == 0))
        def _():
            dgq_ref[...] = jnp.zeros_like(dgq_ref)
            dgk_ref[...] = jnp.zeros_like(dgk_ref)

        @pl.when(h == 0)
        def _():
            kr_acc[...] = jnp.zeros_like(kr_acc)

        tabs_b = (c_ref[...], su_ref[...], sd_ref[...])
        qb = q_ref[...]
        qcat = jnp.concatenate([qb[:, :MLA_NOPE], _rope(qb[:, MLA_NOPE:], *tabs_b)], axis=1)
        dqcat, dgq = _headnorm_bwd(qcat, gq_ref[...], dqf_ref[...], MLA_QK)
        dq_ref[...] = jnp.concatenate(
            [dqcat[:, :MLA_NOPE], _rope_t(dqcat[:, MLA_NOPE:], *tabs_b)], axis=1).astype(BF16)
        dgq_ref[...] += dgq
        kvb = kv_ref[...]
        kcat = jnp.concatenate([kvb[:, :MLA_NOPE], _rope(kr_ref[...], *tabs_b)], axis=1)
        dkcat, dgk = _headnorm_bwd(kcat, gk_ref[...], dkf_ref[...], MLA_QK)
        dkv_ref[...] = jnp.concatenate([dkcat[:, :MLA_NOPE], dv_ref[...]], axis=1).astype(BF16)
        dgk_ref[...] += dgk
        kr_acc[...] += dkcat[:, MLA_NOPE:]

        @pl.when(h == H - 1)
        def _():
            dkr_ref[...] = _rope_t(kr_acc[...], *tabs_b).astype(BF16)

    head = pl.BlockSpec((tm, MLA_QK_PAD), lambda i, h: (i, h))
    tok = pl.BlockSpec((tm, LANE), lambda i, h: (i, 0))
    vec = pl.BlockSpec((1, MLA_QK_PAD), lambda i, h: (0, 0))
    return pl.pallas_call(
        body, name="mla_prep_bwd", grid=(T // tm, H),
        in_specs=[head, head, tok, tok, tok, tok, vec, vec, head, head,
                  pl.BlockSpec((tm, MLA_V), lambda i, h: (i, h))],
        out_specs=(head, head, tok, vec, vec),
        out_shape=(jax.ShapeDtypeStruct((T, H * MLA_QK_PAD), BF16),
                   jax.ShapeDtypeStruct((T, H * MLA_QK_PAD), BF16),
                   jax.ShapeDtypeStruct((T, LANE), BF16),
                   jax.ShapeDtypeStruct((1, MLA_QK_PAD), F32), jax.ShapeDtypeStruct((1, MLA_QK_PAD), F32)),
        scratch_shapes=[pltpu.VMEM((tm, LANE), F32)],
        compiler_params=_params(("arbitrary", "arbitrary")),
    )(qfull, kv, kr, *tabs, gq, gk, dqf, dkf, dv)


def _cf_layouts(cf, H):
    cfh = cf[:, :H].T
    return cfh[:, :, None], cfh[:, None, :]


def _layer_fwd(i, x, P, tabs, H, hook=None):
    kind, j = i % 3, i // 3
    s = {"x_in": x}
    h = _rmsnorm_fwd(x, P["mix_norm"][i:i + 1], name="mix_norm_fwd")
    s["h"] = h
    if kind == 0:
        w = P["sb"][j]
        qkv = _mm(h, w["w_in"], out_dtype=BF16, name="sb_qkv")
        o, ctot = _sb_fwd(qkv, H)
        s.update(qkv=qkv, o=o, ctot=ctot)
    elif kind == 1:
        w = P["fox"][j]
        qkv = _mm(h, w["w_qkv"], name="fox_qkv")
        flog = _mm(h, w["w_f"], name="fox_gate_proj")
        qn, kn, vb = _fox_prep_fwd(qkv, w["gq"], w["gk"], H)
        cf = _cf_layouts(_fox_gate_fwd(flog, w["bf"]), H)
        o, of, lse = _softmax_fwd(qn, kn, vb, cf, H, HEAD_DIM, 1.0 / math.sqrt(HEAD_DIM), name="fox_attn_fwd")
        s.update(qkv=qkv, flog=flog, qn=qn, kn=kn, vb=vb, cf=cf, o=o, of=of, lse=lse)
    else:
        w = P["mla"][j]
        dq = _mm(h, w["w_dq"], name="mla_down_q")
        dkv = _mm(h, w["w_dkv"], name="mla_down_kv")
        kr = _mm(h, w["w_dr"], name="mla_down_rope")
        cq = _rmsnorm_fwd(dq, w["q_norm"], name="mla_q_norm_fwd")
        ckv = _rmsnorm_fwd(dkv, w["kv_norm"], name="mla_kv_norm_fwd")
        qfull = _mm(cq, w["w_uq"], name="mla_up_q")
        kv = _mm(ckv, w["w_ukv"], name="mla_up_kv")
        qf, kf, v = _mla_prep_fwd(qfull, kv, kr, tabs, w["gq"], w["gk"], H)
        o, of, lse = _softmax_fwd(qf, kf, v, None, H, MLA_QK_PAD, 1.0 / math.sqrt(MLA_QK), name="mla_attn_fwd")
        s.update(dq=dq, dkv=dkv, kr=kr, cq=cq, ckv=ckv, qfull=qfull, kv=kv, qf=qf, kf=kf, v=v,
                 o=o, of=of, lse=lse)
    x = _mm(s["o"], w["w_out"], res=x, name="mixer_out")
    s["x_mid"] = x
    deps = hook(x) if hook is not None else ()
    h2 = _rmsnorm_fwd(x, P["mlp_norm"][i:i + 1], name="mlp_norm_fwd")
    z, u = _mm(h2, P["mlp"][i]["w1"], act="relu2", name="mlp_up", deps=deps)
    x = _mm(u, P["mlp"][i]["w2"], res=x, name="mlp_down")
    s.update(h2=h2, z=z, u=u)
    return x, s


def _layer_bwd(i, dx, dxb, s, P, tabs, H, deps=(), hook=None, hook2=None):
    kind, j = i % 3, i // 3
    G = {}
    wm = P["mlp"][i]
    dz = _mm(dxb, wm["w2"], mode="nt", act="drelu2", z=s["z"], out_dtype=BF16, name="mlp_down_bwd",
             deps=deps)
    G["w2"] = _mm(s["u"], dxb, mode="tn", out_dtype=BF16, name="mlp_w2_grad")
    G["w1"] = _mm(s["h2"], dz, mode="tn", out_dtype=BF16, name="mlp_w1_grad")
    dh2 = _mm(dz, wm["w1"], mode="nt", name="mlp_up_bwd")
    dx, dxb, G["mlp_norm"] = _rmsnorm_bwd(s["x_mid"], P["mlp_norm"][i:i + 1], dh2, dx, name="mlp_norm_bwd")
    h = s["h"]
    w = P[("sb", "fox", "mla")[kind]][j]
    do = _mm(dxb, w["w_out"], mode="nt", out_dtype=BF16, name="mixer_out_bwd",
             deps=hook(dxb, G) if hook is not None else ())
    G["w_out"] = _mm(s["o"], dxb, mode="tn", out_dtype=BF16, name="mixer_out_grad")
    if kind == 0:
        dq, dk, dv = _sb_bwd(s["qkv"], do, s["ctot"], H)
        dproj = jnp.concatenate([dq, dk, dv], axis=1)
        G["w_in"] = _mm(h, dproj, mode="tn", out_dtype=BF16, name="sb_qkv_grad",
                        deps=hook2(dv) if hook2 is not None else ())
        dh = _mm(dproj, w["w_in"], mode="nt", name="sb_qkv_bwd")
    elif kind == 1:
        dqn, dkn, dv, dcc, dcr = _softmax_bwd(
            s["qn"], s["kn"], s["vb"], s["cf"], do, s["of"], s["lse"], H, HEAD_DIM,
            1.0 / math.sqrt(HEAD_DIM), name="fox_attn_bwd")
        dq, dk, dvb, G["gq"], G["gk"] = _fox_prep_bwd(s["qkv"], w["gq"], w["gk"], dqn, dkn, dv, H)
        dcf = (dcc[:, :, 0] + dcr[:, 0, :]).T
        dcf = jnp.pad(dcf, ((0, 0), (0, LANE - H)))
        dflog, G["bf"] = _fox_gate_bwd(s["flog"], w["bf"], dcf)
        dproj = jnp.concatenate([dq, dk, dvb], axis=1)
        G["w_qkv"] = _mm(h, dproj, mode="tn", out_dtype=BF16, name="fox_qkv_grad",
                         deps=hook2(dv) if hook2 is not None else ())
        G["w_f"] = _mm(h, dflog, mode="tn", out_dtype=BF16, name="fox_gate_grad")
        dh = _mm(dproj, w["w_qkv"], mode="nt", name="fox_qkv_bwd")
        dh = _mm(dflog, w["w_f"], mode="nt", res=dh, name="fox_gate_bwd_proj")
    else:
        dqf, dkf, dv = _softmax_bwd(
            s["qf"], s["kf"], s["v"], None, do, s["of"], s["lse"], H, MLA_QK_PAD,
            1.0 / math.sqrt(MLA_QK), name="mla_attn_bwd")
        dqfull, dkv, dkr, G["gq"], G["gk"] = _mla_prep_bwd(
            s["qfull"], s["kv"], s["kr"], tabs, w["gq"], w["gk"], dqf, dkf, dv, H)
        G["w_uq"] = _mm(s["cq"], dqfull, mode="tn", out_dtype=BF16, name="mla_up_q_grad",
                        deps=hook2(dv) if hook2 is not None else ())
        G["w_ukv"] = _mm(s["ckv"], dkv, mode="tn", out_dtype=BF16, name="mla_up_kv_grad")
        dcq = _mm(dqfull, w["w_uq"], mode="nt", name="mla_up_q_bwd")
        dckv = _mm(dkv, w["w_ukv"], mode="nt", name="mla_up_kv_bwd")
        ddq, G["q_norm"] = _rmsnorm_bwd(s["dq"], w["q_norm"], dcq, want_f32=False, name="mla_q_norm_bwd")
        ddkv, G["kv_norm"] = _rmsnorm_bwd(s["dkv"], w["kv_norm"], dckv, want_f32=False, name="mla_kv_norm_bwd")
        G["w_dq"] = _mm(h, ddq, mode="tn", out_dtype=BF16, name="mla_down_q_grad")
        G["w_dkv"] = _mm(h, ddkv, mode="tn", out_dtype=BF16, name="mla_down_kv_grad")
        G["w_dr"] = _mm(h, dkr, mode="tn", out_dtype=BF16, name="mla_down_rope_grad")
        dh = _mm(ddq, w["w_dq"], mode="nt", name="mla_down_q_bwd")
        dh = _mm(ddkv, w["w_dkv"], mode="nt", res=dh, name="mla_down_kv_bwd")
        dh = _mm(dkr, w["w_dr"], mode="nt", res=dh, name="mla_down_rope_bwd")
    dx, dxb, G["mix_norm"] = _rmsnorm_bwd(s["x_in"], P["mix_norm"][i:i + 1], dh, dx, name="mix_norm_bwd")
    return dx, dxb, G


def _local_step(x, positions, loss_target, P, depth, H):
    tabs = _rope_tables(positions)
    saved = []
    for i in range(depth):
        x, s = _layer_fwd(i, x, P, tabs, H)
        saved.append(s)
    dx, dxb, loss = _loss_head(x, loss_target)
    grads = [None] * depth
    for i in reversed(range(depth)):
        dx, dxb, grads[i] = _layer_bwd(i, dx, dxb, saved[i], P, tabs, H)
    return loss, dx, grads


def _pad_cols(a, n):
    return jnp.pad(a, ((0, 0), (0, n - a.shape[1])))


def _prepare_layer(P, full, i, H, parts=("mix", "mlp")):
    W = H * HEAD_DIM
    kind, j = i % 3, i // 3
    if "mlp" in parts:
        P["mlp"][i] = {"w1": full["mlp_w1"][i], "w2": full["mlp_w2"][i]}
    if "mix" not in parts:
        return
    if kind == 0:
        P["sb"][j] = {"w_in": full["sb_w_in"][j], "w_out": full["sb_w_out"][j]}
    elif kind == 1:
        w = full["fox_w_in"][j]
        P["fox"][j] = {
            "w_qkv": w[:, :3 * W], "w_f": _pad_cols(w[:, 3 * W:], LANE),
            "bf": _pad_cols(full["fox_b_f"][j:j + 1], LANE),
            "gq": full["fox_q_gain"][j:j + 1], "gk": full["fox_k_gain"][j:j + 1],
            "w_out": full["fox_w_out"][j]}
    else:
        w = full["mla_w_in"][j]
        q_norm, kv_norm = full["mla_q_norm"][j], full["mla_kv_norm"][j]
        rq, rkv = q_norm.shape[0], kv_norm.shape[0]
        w_uq = full["mla_w_uq"][j].reshape(rq, H, MLA_QK)
        w_uq = jnp.pad(w_uq, ((0, 0), (0, 0), (0, MLA_QK_PAD - MLA_QK))).reshape(rq, H * MLA_QK_PAD)
        P["mla"][j] = {
            "w_dq": w[:, :rq], "w_dkv": w[:, rq:rq + rkv], "w_dr": _pad_cols(w[:, rq + rkv:], LANE),
            "q_norm": q_norm[None], "kv_norm": kv_norm[None],
            "w_uq": w_uq, "w_ukv": full["mla_w_ukv"][j],
            "gq": _pad_cols(full["mla_q_gain"][j:j + 1], MLA_QK_PAD),
            "gk": _pad_cols(full["mla_k_gain"][j:j + 1], MLA_QK_PAD),
            "w_out": full["mla_w_out"][j]}


def _prepare(full, H):
    depth = len(full["mlp_w1"])
    P = {"mix_norm": full["mix_norm"], "mlp_norm": full["mlp_norm"], "mlp": [None] * depth,
         "sb": [None] * len(full["sb_w_in"]), "fox": [None] * len(full["fox_w_in"]),
         "mla": [None] * len(full["mla_w_in"])}
    for i in range(depth):
        _prepare_layer(P, full, i, H)
    return P


def _layer_grads(i, G, H):
    kind = i % 3
    out = {"mix_norm": G["mix_norm"], "mlp_norm": G["mlp_norm"], "mlp_w1": G["w1"], "mlp_w2": G["w2"]}
    if kind == 0:
        out.update(sb_w_in=G["w_in"], sb_w_out=G["w_out"])
    elif kind == 1:
        out.update(fox_w_in=jnp.concatenate([G["w_qkv"], G["w_f"][:, :H]], axis=1), fox_b_f=G["bf"][:, :H],
                   fox_q_gain=G["gq"], fox_k_gain=G["gk"], fox_w_out=G["w_out"])
    else:
        rq = G["w_uq"].shape[0]
        out.update(
            mla_w_in=jnp.concatenate([G["w_dq"], G["w_dkv"], G["w_dr"][:, :MLA_ROPE]], axis=1),
            mla_q_norm=G["q_norm"], mla_kv_norm=G["kv_norm"],
            mla_w_uq=G["w_uq"].reshape(rq, H, MLA_QK_PAD)[:, :, :MLA_QK].reshape(rq, H * MLA_QK),
            mla_w_ukv=G["w_ukv"], mla_q_gain=G["gq"][:, :MLA_QK], mla_k_gain=G["gk"][:, :MLA_QK],
            mla_w_out=G["w_out"])
    return out


def _layer_of(name, j):
    if name.startswith("sb_"):
        return 3 * j
    if name.startswith("fox_"):
        return 3 * j + 1
    if name.startswith("mla_"):
        return 3 * j + 2
    return j


def _unprepare(grads, H):
    out = {}
    for i, G in enumerate(grads):
        for n, g in _layer_grads(i, G, H).items():
            out.setdefault(n, []).append(g)
    return out


class _Place:
    def __init__(self, x, y, c):
        self.x, self.y, self.c = x, y, c
        self.dev = 4 * x + 2 * y + c
        self.chip = 2 * x + y
        self.id = (x, y, c)


def _peers(me, kind):
    if kind == "ici":
        return [_Place(1 - me.x, me.y, me.c), _Place(me.x, 1 - me.y, me.c), _Place(1 - me.x, 1 - me.y, me.c)]
    return [_Place(me.x, me.y, 1 - me.c)]


def _exchange(name, kind, operands, out_shapes, aliases, n_remote, n_local, plan):
    n_in, n_out = len(operands), len(out_shapes)

    def body(*refs):
        in_refs, out_refs = refs[:n_in], refs[n_in:n_in + n_out]
        send_sems, recv_sems, local_sems = refs[n_in + n_out:]
        me = _Place(lax.axis_index("x"), lax.axis_index("y"), lax.axis_index("c"))
        peers = _peers(me, kind)
        remote, local = plan(me, peers, in_refs, out_refs)
        assert len(remote) == n_remote and len(local) == n_local
        sends = []
        for n, (src, dst, k, _) in enumerate(remote):
            cp = pltpu.make_async_remote_copy(
                src_ref=src, dst_ref=dst, send_sem=send_sems.at[n], recv_sem=recv_sems.at[n],
                device_id=peers[k].id, device_id_type=MESH)
            cp.start()
            sends.append(cp)
        own = []
        for n, (src, dst) in enumerate(local):
            cp = pltpu.make_async_copy(src, dst, local_sems.at[n])
            cp.start()
            own.append(cp)
        for n, (src, _, k, landing) in enumerate(remote):
            pltpu.make_async_remote_copy(
                src_ref=src, dst_ref=landing, send_sem=send_sems.at[n], recv_sem=recv_sems.at[n],
                device_id=peers[k].id, device_id_type=MESH).wait_recv()
        for cp in sends:
            cp.wait_send()
        for cp in own:
            cp.wait()

    outs = pl.pallas_call(
        body, name=name, in_specs=[ANY] * n_in, out_specs=tuple([ANY] * n_out),
        out_shape=tuple(out_shapes), input_output_aliases=aliases,
        scratch_shapes=[pltpu.SemaphoreType.DMA((n_remote,)), pltpu.SemaphoreType.DMA((n_remote,)),
                        pltpu.SemaphoreType.DMA((max(n_local, 1),))],
    )(*operands)
    return list(outs)


def _window(ref, kind, d, shard_shape):
    r, c = shard_shape
    if kind == "col":
        return ref.at[:, pl.ds(pl.multiple_of(d * c, c), c)]
    return ref.at[pl.ds(pl.multiple_of(d * r, r), r), :]


def _full_shape(kind, shard_shape):
    r, c = shard_shape
    return (r, N_DEV * c) if kind == "col" else (N_DEV * r, c)


HBM = pl.BlockSpec(memory_space=pltpu.HBM)
SEM = pl.BlockSpec(memory_space=pltpu.SEMAPHORE)
EFFECT = pltpu.SideEffectType.DATAFLOW_SIDE_EFFECTING


class _Flight:
    def __init__(self, name, kind, n_remote, plan, send_sems, recv_sems, bufs, token):
        self.name, self.kind, self.n_remote, self.plan = name, kind, n_remote, plan
        self.send_sems, self.recv_sems, self.bufs, self.token = send_sems, recv_sems, bufs, token


def _start_copies(name, kind, bufs, n_remote, plan, after=None):
    nb = len(bufs)
    n_after = 0 if after is None else 1

    def body(*refs):
        in_refs = refs[:nb]
        send_sems, recv_sems = refs[nb + n_after], refs[nb + n_after + 1]
        token = refs[2 * nb + n_after + 2]
        me = _Place(lax.axis_index("x"), lax.axis_index("y"), lax.axis_index("c"))
        peers = _peers(me, kind)
        for n, (src, dst, k, _) in enumerate(plan(me, peers, in_refs)):
            pltpu.make_async_remote_copy(
                src_ref=src, dst_ref=dst, send_sem=send_sems.at[n], recv_sem=recv_sems.at[n],
                device_id=peers[k].id, device_id_type=MESH).start()
        token[...] = jnp.zeros_like(token)

    outs = pl.pallas_call(
        body, name=name, in_specs=[HBM] * nb + [ANY] * n_after,
        out_specs=(SEM, SEM) + (HBM,) * nb + (pl.BlockSpec(memory_space=pltpu.VMEM),),
        out_shape=(pltpu.SemaphoreType.DMA((n_remote,)), pltpu.SemaphoreType.DMA((n_remote,)))
        + tuple(pltpu.HBM(b.shape, b.dtype) for b in bufs) + (jax.ShapeDtypeStruct((8, LANE), F32),),
        input_output_aliases={a: 2 + a for a in range(nb)},
        compiler_params=pltpu.CompilerParams(has_side_effects=EFFECT),
    )(*[pltpu.with_memory_space_constraint(b, pltpu.HBM) for b in bufs], *([after] if n_after else []))
    return _Flight(name, kind, n_remote, plan, outs[0], outs[1], list(outs[2:2 + nb]), outs[2 + nb])


def _wait_copies(flight, after):
    nb = len(flight.bufs)
    plan, kind = flight.plan, flight.kind

    def body(*refs):
        in_refs = refs[:nb]
        send_sems, recv_sems = refs[nb], refs[nb + 1]
        me = _Place(lax.axis_index("x"), lax.axis_index("y"), lax.axis_index("c"))
        peers = _peers(me, kind)
        for n, (src, _, k, landing) in enumerate(plan(me, peers, in_refs)):
            cp = pltpu.make_async_remote_copy(
                src_ref=src, dst_ref=landing, send_sem=send_sems.at[n], recv_sem=recv_sems.at[n],
                device_id=peers[k].id, device_id_type=MESH)
            cp.wait_send()
            cp.wait_recv()

    outs = pl.pallas_call(
        body, name=flight.name.replace("_start", "_wait"),
        in_specs=[HBM] * nb + [SEM, SEM, ANY], out_specs=(HBM,) * nb,
        out_shape=tuple(pltpu.HBM(b.shape, b.dtype) for b in flight.bufs),
        input_output_aliases={a: a for a in range(nb)},
        compiler_params=pltpu.CompilerParams(has_side_effects=EFFECT),
    )(*flight.bufs, flight.send_sems, flight.recv_sems, after)
    return list(outs)


def _own_in_place(shard, kind, dev, dtype, deps=()):
    r, c = shard.shape
    tr = _row_tile(r, c)
    nb = r // tr

    def body(dev_ref, s_ref, *rest):
        o_ref = rest[-1]
        o_ref[...] = s_ref[...].astype(dtype)

    if kind == "col":
        o_spec = pl.BlockSpec((tr, c), lambda i, dev_ref: (i, dev_ref[0]))
    else:
        o_spec = pl.BlockSpec((tr, c), lambda i, dev_ref: (dev_ref[0] * nb + i, 0))
    return pl.pallas_call(
        body, name="own_in_place",
        grid_spec=pltpu.PrefetchScalarGridSpec(
            num_scalar_prefetch=1, grid=(nb,),
            in_specs=[pl.BlockSpec((tr, c), lambda i, dev_ref: (i, 0))] + [ANY] * len(deps), out_specs=o_spec),
        out_shape=jax.ShapeDtypeStruct(_full_shape(kind, shard.shape), dtype),
        compiler_params=_params(("parallel",)),
    )(dev, shard, *deps)


def _gather_ici_start(fulls, kinds, shapes, tag, after=None):
    n = len(fulls)

    def plan(me, peers, refs):
        remote = []
        for a in range(n):
            mine = _window(refs[a], kinds[a], me.dev, shapes[a])
            for k, p in enumerate(peers):
                remote.append((mine, mine, k, _window(refs[a], kinds[a], p.dev, shapes[a])))
        return remote

    return _start_copies("gather_ici_start_" + tag, "ici", list(fulls), 3 * n, plan, after)


def _gather_d2d_start(fulls, kinds, shapes, tag):
    n = len(fulls)

    def plan(me, peers, refs):
        remote = []
        for a in range(n):
            for ch in range(N_CHIP):
                held = _window(refs[a], kinds[a], 2 * ch + me.c, shapes[a])
                remote.append((held, held, 0, _window(refs[a], kinds[a], 2 * ch + 1 - me.c, shapes[a])))
        return remote

    return _start_copies("gather_d2d_start_" + tag, "d2d", fulls, N_CHIP * n, plan)


def _scatter_d2d_start(grads, kinds, shapes, tag):
    n = len(grads)
    lands = [lax.empty((N_CHIP,) + tuple(s), g.dtype) for g, s in zip(grads, shapes)]

    def plan(me, peers, refs):
        remote = []
        for a in range(n):
            for ch in range(N_CHIP):
                remote.append((_window(refs[a], kinds[a], 2 * ch + 1 - me.c, shapes[a]),
                               refs[n + a].at[ch], 0, refs[n + a].at[ch]))
        return remote

    return _start_copies("scatter_d2d_start_" + tag, "d2d", list(grads) + lands, N_CHIP * n, plan)


def _scatter_ici_start(sums, shapes, tag, after=None):
    n = len(sums)
    lands = [lax.empty((N_CHIP - 1,) + tuple(s), BF16) for s in shapes]

    def plan(me, peers, refs):
        remote = []
        for a in range(n):
            for k, p in enumerate(peers):
                remote.append((refs[a].at[p.chip], refs[n + a].at[k], k, refs[n + a].at[k]))
        return remote

    return _start_copies("scatter_ici_start_" + tag, "ici", list(sums) + lands, 3 * n, plan, after)


def _pair_add(g, r4, kind, shard_shape, core, *, name):
    r, c = shard_shape
    tr = _row_tile(r, c, 524288)
    nb = r // tr

    def body(core_ref, g_ref, r_ref, o_ref):
        o_ref[...] = (g_ref[...].astype(F32) + r_ref[...].astype(F32)).astype(BF16)

    if kind == "col":
        g_spec = pl.BlockSpec((tr, c), lambda ch, i, core_ref: (i, 2 * ch + core_ref[0]))
    else:
        g_spec = pl.BlockSpec((tr, c), lambda ch, i, core_ref: ((2 * ch + core_ref[0]) * nb + i, 0))
    slot = pl.BlockSpec((None, tr, c), lambda ch, i, core_ref: (ch, i, 0))
    return pl.pallas_call(
        body, name=name,
        grid_spec=pltpu.PrefetchScalarGridSpec(
            num_scalar_prefetch=1, grid=(N_CHIP, nb), in_specs=[g_spec, slot], out_specs=slot),
        out_shape=jax.ShapeDtypeStruct((N_CHIP, r, c), BF16),
        compiler_params=_params(("parallel", "parallel")),
    )(core, g, r4)


def _allreduce_small_start(part, chip):
    def plan_swap(me, peers, ins, outs):
        return [(ins[0], outs[0], 0, outs[0])], []

    (got,) = _exchange("allreduce_small_swap", "d2d", [part], [jax.ShapeDtypeStruct(part.shape, F32)],
                       {}, 1, 0, plan_swap)

    def body(chip_ref, a_ref, b_ref, o_ref):
        o_ref[...] = a_ref[...] + b_ref[...]

    whole = pl.BlockSpec(part.shape, lambda i, chip_ref: (0, 0))
    slots = pl.pallas_call(
        body, name="allreduce_small_pair",
        grid_spec=pltpu.PrefetchScalarGridSpec(
            num_scalar_prefetch=1, grid=(1,), in_specs=[whole, whole],
            out_specs=pl.BlockSpec((None,) + part.shape, lambda i, chip_ref: (chip_ref[0], 0, 0))),
        out_shape=jax.ShapeDtypeStruct((N_CHIP,) + part.shape, F32),
    )(chip, part, got)

    def plan(me, peers, refs):
        mine = refs[0].at[me.chip]
        return [(mine, mine, k, refs[0].at[p.chip]) for k, p in enumerate(peers)]

    return _start_copies("allreduce_small_start", "ici", [slots], 3, plan)


def _row_tile(r, c, limit=262144):
    best = None
    for t in range(8, r + 1, 8):
        if r % t == 0 and t * c <= limit:
            best = t
    return best if best is not None else r


def _adam_math(w, g, m, v):
    m = ADAM_B1 * m + (1.0 - ADAM_B1) * g
    v = ADAM_B2 * v + (1.0 - ADAM_B2) * (g * g)
    m_hat = m / (1.0 - ADAM_B1 ** ADAM_STEP)
    v_hat = v / (1.0 - ADAM_B2 ** ADAM_STEP)
    delta = -ADAM_LR * (m_hat / (jnp.sqrt(v_hat) + ADAM_EPS) + ADAM_WD * w)
    return delta, m, v


def _adamw(w, m, v, sums, lands, chip, layer, prev, *, name):
    L, r, c = w.shape
    tr = _row_tile(r, c)
    has_prev = prev is not None

    def body(chip_ref, *refs):
        w_ref, m_ref, v_ref, s_ref, l_ref = refs[:5]
        g_ref, d_ref, nm_ref, nv_ref = refs[-4:]
        g = (s_ref[...].astype(F32) + l_ref[0].astype(F32)) + l_ref[1].astype(F32) + l_ref[2].astype(F32)
        delta, nm, nv = _adam_math(w_ref[...], g, m_ref[...], v_ref[...])
        g_ref[...] = g
        d_ref[...] = delta
        nm_ref[...] = nm
        nv_ref[...] = nv

    slab = pl.BlockSpec((None, tr, c), lambda i, chip_ref: (layer, i, 0))
    in_specs = [slab, slab, slab, pl.BlockSpec((None, tr, c), lambda i, chip_ref: (chip_ref[0], i, 0)),
                pl.BlockSpec((N_CHIP - 1, tr, c), lambda i, chip_ref: (0, i, 0))]
    args = [chip, w, m, v, sums, lands]
    aliases = {}
    if has_prev:
        in_specs += [ANY] * 4
        args += list(prev)
        aliases = {6 + n: n for n in range(4)}
    return pl.pallas_call(
        body, name=name,
        grid_spec=pltpu.PrefetchScalarGridSpec(
            num_scalar_prefetch=1, grid=(r // tr,), in_specs=in_specs, out_specs=(slab,) * 4),
        out_shape=tuple(jax.ShapeDtypeStruct((L, r, c), F32) for _ in range(4)),
        input_output_aliases=aliases, compiler_params=_params(("parallel",)),
    )(*args)


def _adamw_small(w, g, m, v):
    def body(w_ref, g_ref, m_ref, v_ref, d_ref, nm_ref, nv_ref):
        d_ref[...], nm_ref[...], nv_ref[...] = _adam_math(w_ref[...], g_ref[...], m_ref[...], v_ref[...])

    return pl.pallas_call(
        body, name="adamw_small", out_shape=tuple(jax.ShapeDtypeStruct(w.shape, F32) for _ in range(3)),
    )(w, g, m, v)


def _sum4(parts):
    def body(p_ref, o_ref):
        o_ref[...] = (p_ref[0] + p_ref[1]) + p_ref[2] + p_ref[3]

    return pl.pallas_call(
        body, name="sum_chips", out_shape=jax.ShapeDtypeStruct(parts.shape[1:], F32))(parts)


_WEIGHTS = ["mix_norm", "mlp_norm", "sb_w_in", "sb_w_out", "fox_w_in", "fox_b_f", "fox_q_gain", "fox_k_gain",
            "fox_w_out", "mla_w_in", "mla_q_norm", "mla_kv_norm", "mla_w_uq", "mla_w_ukv", "mla_q_gain",
            "mla_k_gain", "mla_w_out", "mlp_w1", "mlp_w2"]
_BIG = {"sb_w_in": "col", "sb_w_out": "row", "fox_w_in": "row", "fox_w_out": "row", "mla_w_in": "row",
        "mla_w_uq": "col", "mla_w_ukv": "col", "mla_w_out": "row", "mlp_w1": "col", "mlp_w2": "row"}


def _layer_big(i):
    kind, j = i % 3, i // 3
    mixer = {0: ["sb_w_in", "sb_w_out"], 1: ["fox_w_in", "fox_w_out"],
             2: ["mla_w_in", "mla_w_uq", "mla_w_ukv", "mla_w_out"]}[kind]
    return [(n, j) for n in mixer] + [("mlp_w1", i), ("mlp_w2", i)]


def _stack_to_cols(a):
    r = a.shape[0] // N_DEV
    return a.reshape(N_DEV, r, a.shape[1]).transpose(1, 0, 2).reshape(r, N_DEV * a.shape[1])


def _cols_to_stack(a):
    c = a.shape[1] // N_DEV
    return a.reshape(a.shape[0], N_DEV, c).transpose(1, 0, 2).reshape(N_DEV * a.shape[0], c)


def _pack_rows(rows, width):
    rows = [jnp.pad(r.reshape(-1).astype(F32), (0, width - r.size)) for r in rows]
    pad = (-len(rows)) % 8
    rows += [jnp.zeros((width,), F32)] * pad
    return jnp.stack(rows)


def kernel(x, positions, mix_norm, mlp_norm, sb_w_in, sb_w_out, fox_w_in, fox_b_f, fox_q_gain, fox_k_gain, fox_w_out, mla_w_in, mla_q_norm, mla_kv_norm, mla_w_uq, mla_w_ukv, mla_q_gain, mla_k_gain, mla_w_out, mlp_w1, mlp_w2, loss_target, m_mix_norm, m_mlp_norm, m_sb_w_in, m_sb_w_out, m_fox_w_in, m_fox_b_f, m_fox_q_gain, m_fox_k_gain, m_fox_w_out, m_mla_w_in, m_mla_q_norm, m_mla_kv_norm, m_mla_w_uq, m_mla_w_ukv, m_mla_q_gain, m_mla_k_gain, m_mla_w_out, m_mlp_w1, m_mlp_w2, v_mix_norm, v_mlp_norm, v_sb_w_in, v_sb_w_out, v_fox_w_in, v_fox_b_f, v_fox_q_gain, v_fox_k_gain, v_fox_w_out, v_mla_w_in, v_mla_q_norm, v_mla_kv_norm, v_mla_w_uq, v_mla_w_ukv, v_mla_q_gain, v_mla_k_gain, v_mla_w_out, v_mlp_w1, v_mlp_w2):
    w_in = dict(zip(_WEIGHTS, (mix_norm, mlp_norm, sb_w_in, sb_w_out, fox_w_in, fox_b_f, fox_q_gain, fox_k_gain, fox_w_out, mla_w_in, mla_q_norm, mla_kv_norm, mla_w_uq, mla_w_ukv, mla_q_gain, mla_k_gain, mla_w_out, mlp_w1, mlp_w2)))
    m_in = dict(zip(_WEIGHTS, (m_mix_norm, m_mlp_norm, m_sb_w_in, m_sb_w_out, m_fox_w_in, m_fox_b_f, m_fox_q_gain, m_fox_k_gain, m_fox_w_out, m_mla_w_in, m_mla_q_norm, m_mla_kv_norm, m_mla_w_uq, m_mla_w_ukv, m_mla_q_gain, m_mla_k_gain, m_mla_w_out, m_mlp_w1, m_mlp_w2)))
    v_in = dict(zip(_WEIGHTS, (v_mix_norm, v_mlp_norm, v_sb_w_in, v_sb_w_out, v_fox_w_in, v_fox_b_f, v_fox_q_gain, v_fox_k_gain, v_fox_w_out, v_mla_w_in, v_mla_q_norm, v_mla_kv_norm, v_mla_w_uq, v_mla_w_ukv, v_mla_q_gain, v_mla_k_gain, v_mla_w_out, v_mlp_w1, v_mlp_w2)))
    depth, D = mix_norm.shape
    H = D // HEAD_DIM
    n_mla = mla_w_in.shape[0]
    dev = 4 * lax.axis_index("x") + 2 * lax.axis_index("y") + lax.axis_index("c")
    core = lax.axis_index("c").astype(jnp.int32).reshape(1)
    dev_arr = dev.astype(jnp.int32).reshape(1)

    chip = (2 * lax.axis_index("x") + lax.axis_index("y")).astype(jnp.int32).reshape(1)
    nq, nkv = mla_q_norm.shape[1], mla_kv_norm.shape[1]

    units, u_mix, u_mlp = [], {}, {}
    for i in range(depth):
        big = _layer_big(i)
        u_mix[i] = len(units)
        if i < SPLIT_LAYERS:
            units += [(i, ("mix",), big[:-2]), (i, ("mlp",), big[-2:])]
        else:
            units.append((i, ("mix", "mlp"), big))
        u_mlp[i] = len(units) - 1

    def split(i):
        return u_mix[i] != u_mlp[i]

    def tag_of(u):
        i, parts, _ = units[u]
        return f"l{i}" if len(parts) == 2 else f"l{i}_{parts[0]}"

    kinds_of, shapes_of, ici = [], [], []
    for i, parts, names in units:
        shards = [w_in[n][j] for n, j in names]
        kinds = [_BIG[n] for n, _ in names]
        behind = (ici[-1].token,) if ici else ()
        fulls = [_own_in_place(s, k, dev_arr, BF16, behind) for s, k in zip(shards, kinds)]
        if i % 3 == 2:
            shards.append(_pack_rows([mla_q_norm[i // 3], mla_kv_norm[i // 3]], LANE))
            kinds.append("row")
            fulls.append(_own_in_place(shards[-1], "row", dev_arr, F32, behind))
        kinds_of.append(kinds)
        shapes_of.append([s.shape for s in shards])
        ici.append(_gather_ici_start(fulls, kinds, shapes_of[-1], tag_of(len(ici)),
                                     ici[-1].token if ici else None))

    full = {n: [None] * w_in[n].shape[0] for n in _WEIGHTS}
    for n in ("mix_norm", "mlp_norm", "fox_b_f", "fox_q_gain", "fox_k_gain", "mla_q_gain", "mla_k_gain"):
        full[n] = w_in[n]
    P = {"mix_norm": mix_norm, "mlp_norm": mlp_norm, "mlp": [None] * depth,
         "sb": [None] * sb_w_in.shape[0], "fox": [None] * fox_w_in.shape[0], "mla": [None] * n_mla}

    def forward_to_sibling(u, after):
        arrived = _wait_copies(ici[u], after)
        return _gather_d2d_start(arrived, kinds_of[u], shapes_of[u], tag_of(u))

    def finish_gather(u, flight, after):
        i, parts, names = units[u]
        got = _wait_copies(flight, after)
        for (n, j), a in zip(names, got):
            full[n][j] = _stack_to_cols(a) if n == "fox_w_in" else a
        if i % 3 == 2:
            tiles = got[-1].reshape(N_DEV, 8, LANE)
            full["mla_q_norm"][i // 3] = tiles[:, 0, :nq].reshape(-1)
            full["mla_kv_norm"][i // 3] = tiles[:, 1, :nkv].reshape(-1)
        _prepare_layer(P, full, i, H, parts)

    tabs = _rope_tables(positions[0])
    d2d = forward_to_sibling(0, ici[-1].token)
    finish_gather(0, d2d, d2d.token)
    xs, saved = x[0], []
    for i in range(depth):
        nxt = {}

        def hook(x_mid, i=i, nxt=nxt):
            if split(i):
                mlp_d2d = forward_to_sibling(u_mlp[i], x_mid)
                finish_gather(u_mlp[i], mlp_d2d, mlp_d2d.token)
            if 1 <= i < depth - 1 and not split(i + 1):
                nxt["d2d"] = forward_to_sibling(u_mix[i + 1], x_mid)
                return (nxt["d2d"].token,)
            return ()

        xs, s = _layer_fwd(i, xs, P, tabs, H, hook)
        saved.append(s)
        if i + 1 < depth:
            if "d2d" in nxt:
                finish_gather(u_mix[i + 1], nxt["d2d"], xs)
            else:
                mix_d2d = forward_to_sibling(u_mix[i + 1], xs)
                finish_gather(u_mix[i + 1], mix_d2d, mix_d2d.token)
    dx, dxb, loss = _loss_head(xs, loss_target[0])
    loss = lax.psum(loss[0, 0], ("x", "y", "c"))

    def unit_meta(u):
        names = units[u][2]
        return [_BIG[n] for n, _ in names], [w_in[n].shape[1:] for n, _ in names]

    def to_sibling_start(u, Gl):
        grads = [(_cols_to_stack(Gl[n]) if n == "fox_w_in" else Gl[n]) for n, _ in units[u][2]]
        return _scatter_d2d_start(grads, *unit_meta(u), tag_of(u))

    def pair_and_send(u, flight, after, before=None):
        n = len(units[u][2])
        kinds, shapes = unit_meta(u)
        got = _wait_copies(flight, after)
        sums = [_pair_add(g, r4, k, s, core, name="pair_add_" + tag_of(u))
                for g, r4, k, s in zip(got[:n], got[n:], kinds, shapes)]
        return _scatter_ici_start(sums, shapes, tag_of(u), before)

    G, to_chips, deps, flying = [None] * depth, [None] * len(units), (), {}
    for i in reversed(range(depth)):
        def hook(dxb_mid, Gp, i=i):
            out = ()
            if i + 1 < depth:
                u = u_mix[i + 1]
                to_chips[u] = pair_and_send(u, flying.pop(u), dxb_mid)
                out = (to_chips[u].token,)
            if split(i):
                flying[u_mlp[i]] = to_sibling_start(u_mlp[i], {"mlp_w1": Gp["w1"], "mlp_w2": Gp["w2"]})
                out = out + (flying[u_mlp[i]].token,)
            return out

        def hook2(dv, i=i):
            if not split(i):
                return ()
            u = u_mlp[i]
            to_chips[u] = pair_and_send(u, flying.pop(u), dv)
            return (to_chips[u].token,)

        dx, dxb, Gp = _layer_bwd(i, dx, dxb, saved[i], P, tabs, H, deps, hook, hook2)
        G[i] = _layer_grads(i, Gp, H)
        flying[u_mix[i]] = to_sibling_start(u_mix[i], G[i])
        deps = (flying[u_mix[i]].token,)
    grad_x = dx
    to_chips[0] = pair_and_send(0, flying.pop(0), dx)

    small = [n for n in _WEIGHTS if n not in _BIG]
    rows = []
    for n in small:
        for j in range(w_in[n].shape[0]):
            rows.append(G[_layer_of(n, j)][n])
    small_flight = _allreduce_small_start(_pack_rows(rows, D) + to_chips[0].token[0, 0], chip)

    results = {}
    for u in reversed(range(len(units))):
        names = units[u][2]
        got = _wait_copies(to_chips[u], small_flight.token)
        for a, (n, j) in enumerate(names):
            results[n] = _adamw(w_in[n], m_in[n], v_in[n], got[a], got[len(names) + a], chip, j, results.get(n),
                                name="adamw_" + n)

    (parts,) = _wait_copies(small_flight, results[units[0][2][0][0]][0])
    sgr = _sum4(parts)
    g_rows, w_rows, m_rows, v_rows, at = [], [], [], [], 0
    for n in small:
        for j in range(w_in[n].shape[0]):
            width = w_in[n].shape[1]
            if n in ("mla_q_norm", "mla_kv_norm"):
                g_rows.append(lax.dynamic_slice(sgr[at], (dev * width,), (width,)))
            else:
                g_rows.append(sgr[at, :width])
            w_rows.append(w_in[n][j])
            m_rows.append(m_in[n][j])
            v_rows.append(v_in[n][j])
            at += 1
    g_pack = _pack_rows(g_rows, D)
    d_pack, nm_pack, nv_pack = _adamw_small(
        _pack_rows(w_rows, D), g_pack, _pack_rows(m_rows, D), _pack_rows(v_rows, D))
    at = 0
    for n in small:
        L, width = w_in[n].shape
        results[n] = tuple(p[at:at + L, :width] for p in (g_pack, d_pack, nm_pack, nv_pack))
        at += L

    out = [loss, grad_x[None]]
    for part in range(4):
        out += [results[n][part] for n in _WEIGHTS]
    return tuple(out)
```

```python
import functools
import math

import jax
import jax.numpy as jnp
from jax import lax
from jax.experimental import pallas as pl
from jax.experimental.pallas import tpu as pltpu

F32 = jnp.float32
BF16 = jnp.bfloat16

HEAD_DIM = 128
MLA_NOPE = 128
MLA_ROPE = 64
MLA_V = 128
MLA_QK = MLA_NOPE + MLA_ROPE
MLA_QK_PAD = 256
LANE = 128
ROPE_THETA = 10000.0
EPS = 1e-6
ADAM_LR = 0.001
ADAM_B1 = 0.9
ADAM_B2 = 0.999
ADAM_EPS = 1e-08
ADAM_WD = 0.01
ADAM_STEP = 10
N_DEV = 8
N_CHIP = 4
NEG = -1e30
VMEM_LIMIT = 56 * 1024 * 1024
ATT_BLOCK = 256
SB_BLOCK = 256
ATT_HEADS = 2
SPLIT_LAYERS = 2
SB_PARTS = 2
MESH = pl.DeviceIdType.MESH
ANY = pl.BlockSpec(memory_space=pl.ANY)


def _params(sem):
    return pltpu.CompilerParams(dimension_semantics=sem, vmem_limit_bytes=VMEM_LIMIT)


def _tile(dim, pref):
    if dim <= pref:
        return dim
    t = pref
    while dim % t:
        t -= LANE
    return t


def _dot(a, b, dims):
    return lax.dot_general(a, b, (dims, ((), ())), preferred_element_type=F32)


NN = ((1,), (0,))
NT = ((1,), (1,))
TN = ((0,), (0,))


def _mm(a, b, *, mode="nn", out_dtype=F32, res=None, act=None, z=None, name,
        tm=1024, tn=1024, tk=2048, deps=()):
    if mode == "nn":
        (M, K), (_, N) = a.shape, b.shape
    elif mode == "nt":
        (M, K), (N, _) = a.shape, b.shape
    else:
        (K, M), (_, N) = a.shape, b.shape
    tm, tn, tk = _tile(M, tm), _tile(N, tn), _tile(K, tk)
    nk = K // tk
    if mode == "tn":
        a_spec = pl.BlockSpec((tk, tm), lambda i, j, k: (k, i))
    else:
        a_spec = pl.BlockSpec((tm, tk), lambda i, j, k: (i, k))
    if mode == "nt":
        b_spec = pl.BlockSpec((tn, tk), lambda i, j, k: (j, k))
    else:
        b_spec = pl.BlockSpec((tk, tn), lambda i, j, k: (k, j))
    dims = {"nn": NN, "nt": NT, "tn": TN}[mode]
    o_spec = pl.BlockSpec((tm, tn), lambda i, j, k: (i, j))
    in_specs, args = [a_spec, b_spec], [a, b]
    if res is not None:
        in_specs.append(o_spec)
        args.append(res)
    if act == "drelu2":
        in_specs.append(o_spec)
        args.append(z)
    if act == "relu2":
        out_shape = (jax.ShapeDtypeStruct((M, N), F32), jax.ShapeDtypeStruct((M, N), BF16))
        out_specs = (o_spec, o_spec)
    else:
        out_shape = jax.ShapeDtypeStruct((M, N), out_dtype)
        out_specs = o_spec
    has_res, has_z = res is not None, act == "drelu2"
    n_out = 2 if act == "relu2" else 1
    in_specs += [ANY] * len(deps)
    args += list(deps)

    def body(*refs):
        a_ref, b_ref = refs[0], refs[1]
        idx = 2
        res_ref = z_ref = None
        if has_res:
            res_ref = refs[idx]
            idx += 1
        if has_z:
            z_ref = refs[idx]
            idx += 1
        idx += len(deps)
        outs = refs[idx:idx + n_out]

        def finish(r):
            if has_res:
                r = r + res_ref[...]
            if act == "relu2":
                outs[0][...] = r
                rr = jnp.maximum(r, 0.0)
                outs[1][...] = (rr * rr).astype(BF16)
            elif act == "drelu2":
                outs[0][...] = (r * (2.0 * jnp.maximum(z_ref[...], 0.0))).astype(out_dtype)
            else:
                outs[0][...] = r.astype(out_dtype)

        if nk == 1:
            finish(_dot(a_ref[...], b_ref[...], dims))
            return
        acc = refs[-1]
        k = pl.program_id(2)

        @pl.when(k == 0)
        def _():
            acc[...] = _dot(a_ref[...], b_ref[...], dims)

        @pl.when((k > 0) & (k < nk - 1))
        def _():
            acc[...] += _dot(a_ref[...], b_ref[...], dims)

        @pl.when(k == nk - 1)
        def _():
            finish(acc[...] + _dot(a_ref[...], b_ref[...], dims))

    return pl.pallas_call(
        body, name=name, grid=(M // tm, N // tn, nk), in_specs=in_specs, out_specs=out_specs,
        out_shape=out_shape, scratch_shapes=[pltpu.VMEM((tm, tn), F32)] if nk > 1 else [],
        compiler_params=_params(("parallel", "parallel", "arbitrary")),
    )(*args)


def _rmsnorm_fwd(x, g, *, name, tm=256):
    T, n = x.shape
    tm = _tile(T, tm)

    def body(x_ref, g_ref, o_ref):
        xf = x_ref[...]
        r = lax.rsqrt(jnp.mean(xf * xf, axis=-1, keepdims=True) + EPS)
        o_ref[...] = (xf * r * g_ref[...]).astype(BF16)

    return pl.pallas_call(
        body, name=name, grid=(T // tm,),
        in_specs=[pl.BlockSpec((tm, n), lambda i: (i, 0)), pl.BlockSpec((1, n), lambda i: (0, 0))],
        out_specs=pl.BlockSpec((tm, n), lambda i: (i, 0)),
        out_shape=jax.ShapeDtypeStruct((T, n), BF16),
        compiler_params=_params(("parallel",)),
    )(x, g)


def _rmsnorm_bwd(x, g, dy, dx_in=None, *, name, want_f32=True, tm=256):
    T, n = x.shape
    tm = _tile(T, tm)
    has_in = dx_in is not None
    row = pl.BlockSpec((tm, n), lambda i: (i, 0))
    vec = pl.BlockSpec((1, n), lambda i: (0, 0))

    def body(*refs):
        x_ref, g_ref, dy_ref = refs[:3]
        idx = 3
        in_ref = None
        if has_in:
            in_ref = refs[idx]
            idx += 1
        outs = refs[idx:]
        xf = x_ref[...]
        r = lax.rsqrt(jnp.mean(xf * xf, axis=-1, keepdims=True) + EPS)
        dyf = dy_ref[...]
        t = dyf * g_ref[...]
        dx = r * t - xf * (r * r * r * jnp.mean(t * xf, axis=-1, keepdims=True))
        if has_in:
            dx = dx + in_ref[...]
        o = 0
        if want_f32:
            outs[0][...] = dx
            o = 1
        outs[o][...] = dx.astype(BF16)
        dg_ref = outs[o + 1]

        @pl.when(pl.program_id(0) == 0)
        def _():
            dg_ref[...] = jnp.zeros_like(dg_ref)

        dg_ref[...] += jnp.sum(dyf * xf * r, axis=0, keepdims=True)

    in_specs = [row, vec, row] + ([row] if has_in else [])
    out_specs, out_shape = [], []
    if want_f32:
        out_specs.append(row)
        out_shape.append(jax.ShapeDtypeStruct((T, n), F32))
    out_specs += [row, vec]
    out_shape += [jax.ShapeDtypeStruct((T, n), BF16), jax.ShapeDtypeStruct((1, n), F32)]
    args = [x, g, dy] + ([dx_in] if has_in else [])
    return pl.pallas_call(
        body, name=name, grid=(T // tm,), in_specs=in_specs, out_specs=tuple(out_specs),
        out_shape=tuple(out_shape), compiler_params=_params(("arbitrary",)),
    )(*args)


def _loss_head(y, target, *, tm=256):
    T, n = y.shape
    tm = _tile(T, tm)
    row = pl.BlockSpec((tm, n), lambda i: (i, 0))

    def body(y_ref, t_ref, dy_ref, dyb_ref, loss_ref):
        err = y_ref[...] - t_ref[...]
        dy = err * (1.0 / n)
        dy_ref[...] = dy
        dyb_ref[...] = dy.astype(BF16)

        @pl.when(pl.program_id(0) == 0)
        def _():
            loss_ref[...] = jnp.zeros_like(loss_ref)

        part = 0.5 * jnp.sum(jnp.mean(err * err, axis=-1, keepdims=True), axis=0, keepdims=True)
        loss_ref[...] += part

    return pl.pallas_call(
        body, name="loss_head", grid=(T // tm,), in_specs=[row, row],
        out_specs=(row, row, pl.BlockSpec((1, 1), lambda i: (0, 0))),
        out_shape=(jax.ShapeDtypeStruct((T, n), F32), jax.ShapeDtypeStruct((T, n), BF16),
                   jax.ShapeDtypeStruct((1, 1), F32)),
        compiler_params=_params(("arbitrary",)),
    )(y, target)


def _split(v, n):
    parts = []
    for _ in range(n - 1):
        p = v.astype(BF16)
        parts.append(p)
        v = v - p.astype(F32)
    parts.append(v.astype(BF16))
    return parts


def _tri_left(tri, v):
    hi, mid, lo = _split(v, 3)
    return _dot(tri, hi, NN) + _dot(tri, mid, NN) + _dot(tri, lo, NN)


def _iota2(shape, dim):
    return lax.broadcasted_iota(jnp.int32, shape, dim)


def _log_sigmoid(z):
    return jnp.minimum(z, 0.0) - jnp.log1p(jnp.exp(-jnp.abs(z)))


def _log_sigmoid_abs(z):
    return jnp.minimum(z, 0.0) - jnp.log(1.0 + jnp.exp(-jnp.abs(z)))


def _head_spec(rows, width, off):
    return pl.BlockSpec((rows, width), lambda h, i: (i, off + h))


def _full_head_spec(T, width, off):
    return pl.BlockSpec((T, width), lambda h, i: (0, off + h))


def _sb_fwd(qkv, H):
    T = qkv.shape[0]
    B = _tile(T, SB_BLOCK)
    nq = T // B
    scale = 1.0 / math.sqrt(HEAD_DIM)

    HG = ATT_HEADS if H % ATT_HEADS == 0 else 1
    HD = HEAD_DIM

    def body(q_ref, k_ref, v_ref, o_ref, c_ref, tri_s):
        i = pl.program_id(1)
        tri = (_iota2((B, B), 0) > _iota2((B, B), 1)).astype(BF16)
        for p in range(SB_PARTS):
            tri_s[pl.ds(p * B, B), :] = tri
        rows = _iota2((B, B), 0)
        cols = _iota2((B, B), 1)

        def step(n, carry):
            j = i - n
            ks = pl.multiple_of(j * B, B)
            mask = (ks + cols) < (i * B + rows)
            out = []
            for g in range(HG):
                acc, run = carry[g]
                q = q_ref[:, pl.ds(g * HD, HD)]
                kb = k_ref[pl.ds(ks, B), pl.ds(g * HD, HD)]
                vb = v_ref[pl.ds(ks, B), pl.ds(g * HD, HD)]
                z = _dot(q, kb, NT) * scale
                ls = _log_sigmoid_abs(z)
                lk = jnp.where(mask, ls - z, 0.0)
                parts = jnp.concatenate(_split(lk, SB_PARTS), axis=1)
                later = _dot(parts, tri_s[...], NN)
                a = jnp.exp(jnp.where(mask, ls, NEG) + later).astype(BF16)
                acc = acc + jnp.exp(run) * _dot(a, vb, NN)
                out.append((acc, run + jnp.sum(lk, axis=1, keepdims=True)))
            return tuple(out)

        init = tuple((jnp.zeros((B, HD), F32), jnp.zeros((B, 1), F32)) for _ in range(HG))
        res = lax.fori_loop(0, i + 1, step, init)
        for g in range(HG):
            o_ref[:, pl.ds(g * HD, HD)] = res[g][0].astype(BF16)
            c_ref[g] = res[g][1]

    return pl.pallas_call(
        body, name="sb_attn_fwd", grid=(H // HG, nq),
        in_specs=[_head_spec(B, HG * HD, 0), _full_head_spec(T, HG * HD, H // HG),
                  _full_head_spec(T, HG * HD, 2 * H // HG)],
        out_specs=(_head_spec(B, HG * HD, 0), pl.BlockSpec((HG, B, 1), lambda h, i: (h, i, 0))),
        out_shape=(jax.ShapeDtypeStruct((T, H * HD), BF16), jax.ShapeDtypeStruct((H, T, 1), F32)),
        scratch_shapes=[pltpu.VMEM((SB_PARTS * B, B), BF16)],
        compiler_params=_params(("parallel", "arbitrary")),
    )(qkv, qkv, qkv)


def _sb_bwd(qkv, do, ctot, H):
    T = qkv.shape[0]
    B = _tile(T, SB_BLOCK)
    nq = T // B
    scale = 1.0 / math.sqrt(HEAD_DIM)
    HG = ATT_HEADS if H % ATT_HEADS == 0 else 1
    HD = HEAD_DIM

    def body(q_ref, k_ref, v_ref, do_ref, c_ref, dq_ref, dk_ref, dv_ref, dk_acc, dv_acc,
             tri_incl_s, tri_strict_s):
        i = pl.program_id(1)

        @pl.when(i == 0)
        def _():
            dk_acc[...] = jnp.zeros_like(dk_acc)
            dv_acc[...] = jnp.zeros_like(dv_acc)

        rows = _iota2((B, B), 0)
        cols = _iota2((B, B), 1)
        tri_incl = (rows <= cols).astype(BF16)
        tri_strict = (rows < cols).astype(BF16)
        for p in range(SB_PARTS):
            tri_incl_s[pl.ds(p * B, B), :] = tri_incl
            tri_strict_s[pl.ds(p * B, B), :] = tri_strict
        heads = [(q_ref[:, pl.ds(g * HD, HD)], do_ref[:, pl.ds(g * HD, HD)], c_ref[g]) for g in range(HG)]

        def step(j, carry):
            ks = pl.multiple_of(j * B, B)
            mask = (ks + cols) < (i * B + rows)
            out = []
            for g in range(HG):
                dq, lpre, gpre = carry[g]
                q, do_b, ctot_b = heads[g]
                kb = k_ref[pl.ds(ks, B), pl.ds(g * HD, HD)]
                vb = v_ref[pl.ds(ks, B), pl.ds(g * HD, HD)]
                z = _dot(q, kb, NT) * scale
                da = _dot(do_b, vb, NT)
                ls = _log_sigmoid_abs(z)
                lk = jnp.where(mask, ls - z, 0.0)
                lsm = jnp.where(mask, ls, NEG)
                incl = _dot(jnp.concatenate(_split(lk, SB_PARTS), axis=1), tri_incl_s[...], NN)
                a = jnp.exp(lsm + (ctot_b - lpre) - incl)
                gw = a * da
                gex = gpre + _dot(jnp.concatenate(_split(gw, SB_PARTS), axis=1), tri_strict_s[...], NN)
                dzb = ((gw * jnp.exp(lsm - z) - jnp.exp(lsm) * gex) * scale).astype(BF16)
                dq = dq + _dot(dzb, kb, NN)
                dk_acc[pl.ds(ks, B), pl.ds(g * HD, HD)] += _dot(dzb, q, TN)
                dv_acc[pl.ds(ks, B), pl.ds(g * HD, HD)] += _dot(a.astype(BF16), do_b, TN)
                out.append((dq, lpre + jnp.sum(lk, axis=1, keepdims=True),
                            gpre + jnp.sum(gw, axis=1, keepdims=True)))
            return tuple(out)

        init = tuple((jnp.zeros((B, HD), F32), jnp.zeros((B, 1), F32), jnp.zeros((B, 1), F32))
                     for _ in range(HG))
        res = lax.fori_loop(0, i + 1, step, init)
        for g in range(HG):
            dq_ref[:, pl.ds(g * HD, HD)] = res[g][0].astype(BF16)

        @pl.when(i == nq - 1)
        def _():
            dk_ref[...] = dk_acc[...].astype(BF16)
            dv_ref[...] = dv_acc[...].astype(BF16)

    W = H * HD
    return pl.pallas_call(
        body, name="sb_attn_bwd", grid=(H // HG, nq),
        in_specs=[_head_spec(B, HG * HD, 0), _full_head_spec(T, HG * HD, H // HG),
                  _full_head_spec(T, HG * HD, 2 * H // HG), _head_spec(B, HG * HD, 0),
                  pl.BlockSpec((HG, B, 1), lambda h, i: (h, i, 0))],
        out_specs=(_head_spec(B, HG * HD, 0), _full_head_spec(T, HG * HD, 0),
                   _full_head_spec(T, HG * HD, 0)),
        out_shape=tuple(jax.ShapeDtypeStruct((T, W), BF16) for _ in range(3)),
        scratch_shapes=[pltpu.VMEM((T, HG * HD), F32), pltpu.VMEM((T, HG * HD), F32),
                        pltpu.VMEM((SB_PARTS * B, B), BF16), pltpu.VMEM((SB_PARTS * B, B), BF16)],
        compiler_params=_params(("parallel", "arbitrary")),
    )(qkv, qkv, qkv, do, ctot)


def _softmax_fwd(q, k, v, cf, H, dqk, scale, *, name):
    T = q.shape[0]
    B = _tile(T, ATT_BLOCK)
    nq = T // B
    has_cf = cf is not None
    HG = ATT_HEADS if H % ATT_HEADS == 0 else 1
    HD = HEAD_DIM

    def body(*refs):
        q_ref, k_ref, v_ref = refs[:3]
        idx = 3
        if has_cf:
            cfc_ref, cfr_ref = refs[3], refs[4]
            idx = 5
        o_ref, of_ref, lse_ref = refs[idx:idx + 3]
        i = pl.program_id(1)
        rows = _iota2((B, B), 0)
        cols = _iota2((B, B), 1)

        def step(j, carry):
            ks = pl.multiple_of(j * B, B)
            mask = (ks + cols) <= (i * B + rows)
            out = []
            for g in range(HG):
                m, l, acc = carry[g]
                qb = q_ref[:, pl.ds(g * dqk, dqk)]
                kb = k_ref[pl.ds(ks, B), pl.ds(g * dqk, dqk)]
                vb = v_ref[pl.ds(ks, B), pl.ds(g * HD, HD)]
                s = _dot(qb, kb, NT) * scale
                if has_cf:
                    s = s + (cfc_ref[g] - cfr_ref[g, :, pl.ds(ks, B)])
                s = jnp.where(mask, s, NEG)
                m_new = jnp.maximum(m, jnp.max(s, axis=1, keepdims=True))
                alpha = jnp.exp(m - m_new)
                p = jnp.exp(s - m_new)
                l = alpha * l + jnp.sum(p, axis=1, keepdims=True)
                acc = alpha * acc + _dot(p.astype(BF16), vb, NN)
                out.append((m_new, l, acc))
            return tuple(out)

        init = tuple((jnp.full((B, 1), NEG, F32), jnp.zeros((B, 1), F32), jnp.zeros((B, HD), F32))
                     for _ in range(HG))
        res = lax.fori_loop(0, i + 1, step, init)
        for g in range(HG):
            m, l, acc = res[g]
            o = acc / l
            o_ref[:, pl.ds(g * HD, HD)] = o.astype(BF16)
            of_ref[:, pl.ds(g * HD, HD)] = o
            lse_ref[g] = m + jnp.log(l)

    stat = pl.BlockSpec((HG, B, 1), lambda h, i: (h, i, 0))
    in_specs = [_head_spec(B, HG * dqk, 0), _full_head_spec(T, HG * dqk, 0), _full_head_spec(T, HG * HD, 0)]
    args = [q, k, v]
    if has_cf:
        in_specs += [stat, pl.BlockSpec((HG, 1, T), lambda h, i: (h, 0, 0))]
        args += list(cf)
    W = H * HD
    return pl.pallas_call(
        body, name=name, grid=(H // HG, nq), in_specs=in_specs,
        out_specs=(_head_spec(B, HG * HD, 0), _head_spec(B, HG * HD, 0), stat),
        out_shape=(jax.ShapeDtypeStruct((T, W), BF16), jax.ShapeDtypeStruct((T, W), F32),
                   jax.ShapeDtypeStruct((H, T, 1), F32)),
        compiler_params=_params(("parallel", "arbitrary")),
    )(*args)


def _softmax_bwd(q, k, v, cf, do, o, lse, H, dqk, scale, *, name):
    T = q.shape[0]
    B = _tile(T, ATT_BLOCK)
    nq = T // B
    has_cf = cf is not None
    HG = ATT_HEADS if H % ATT_HEADS == 0 else 1
    HD = HEAD_DIM

    def body(*refs):
        q_ref, k_ref, v_ref, do_ref, o_ref, lse_ref = refs[:6]
        idx = 6
        if has_cf:
            cfc_ref, cfr_ref = refs[6], refs[7]
            idx = 8
        dq_ref, dk_ref, dv_ref = refs[idx:idx + 3]
        idx += 3
        if has_cf:
            dcc_ref, dcr_ref = refs[idx], refs[idx + 1]
        i = pl.program_id(1)

        @pl.when(i == 0)
        def _():
            dk_ref[...] = jnp.zeros_like(dk_ref)
            dv_ref[...] = jnp.zeros_like(dv_ref)
            if has_cf:
                dcr_ref[...] = jnp.zeros_like(dcr_ref)

        rows = _iota2((B, B), 0)
        cols = _iota2((B, B), 1)
        heads = []
        for g in range(HG):
            do_b = do_ref[:, pl.ds(g * HD, HD)]
            delta = jnp.sum(do_b.astype(F32) * o_ref[:, pl.ds(g * HD, HD)], axis=1, keepdims=True)
            heads.append((q_ref[:, pl.ds(g * dqk, dqk)], do_b, lse_ref[g], delta))

        def step(j, carry):
            ks = pl.multiple_of(j * B, B)
            mask = (ks + cols) <= (i * B + rows)
            out = []
            for g in range(HG):
                dq, rs = carry[g]
                qb, do_b, lse_b, delta = heads[g]
                kb = k_ref[pl.ds(ks, B), pl.ds(g * dqk, dqk)]
                vb = v_ref[pl.ds(ks, B), pl.ds(g * HD, HD)]
                s = _dot(qb, kb, NT) * scale
                if has_cf:
                    s = s + (cfc_ref[g] - cfr_ref[g, :, pl.ds(ks, B)])
                p = jnp.where(mask, jnp.exp(s - lse_b), 0.0)
                dp = _dot(do_b, vb, NT)
                ds = p * (dp - delta)
                dsb = (ds * scale).astype(BF16)
                dq = dq + _dot(dsb, kb, NN)
                dk_ref[pl.ds(ks, B), pl.ds(g * dqk, dqk)] += _dot(dsb, qb, TN)
                dv_ref[pl.ds(ks, B), pl.ds(g * HD, HD)] += _dot(p.astype(BF16), do_b, TN)
                if has_cf:
                    rs = rs + jnp.sum(ds, axis=1, keepdims=True)
                    dcr_ref[g, :, pl.ds(ks, B)] -= jnp.sum(ds, axis=0, keepdims=True)
                out.append((dq, rs))
            return tuple(out)

        init = tuple((jnp.zeros((B, dqk), F32), jnp.zeros((B, 1), F32)) for _ in range(HG))
        res = lax.fori_loop(0, i + 1, step, init)
        for g in range(HG):
            dq_ref[:, pl.ds(g * dqk, dqk)] = res[g][0]
            if has_cf:
                dcc_ref[g] = res[g][1]

    stat = pl.BlockSpec((HG, B, 1), lambda h, i: (h, i, 0))
    rowstat = pl.BlockSpec((HG, 1, T), lambda h, i: (h, 0, 0))
    in_specs = [_head_spec(B, HG * dqk, 0), _full_head_spec(T, HG * dqk, 0), _full_head_spec(T, HG * HD, 0),
                _head_spec(B, HG * HD, 0), _head_spec(B, HG * HD, 0), stat]
    args = [q, k, v, do, o, lse]
    out_specs = [_head_spec(B, HG * dqk, 0), _full_head_spec(T, HG * dqk, 0), _full_head_spec(T, HG * HD, 0)]
    out_shape = [jax.ShapeDtypeStruct((T, H * dqk), F32), jax.ShapeDtypeStruct((T, H * dqk), F32),
                 jax.ShapeDtypeStruct((T, H * HD), F32)]
    if has_cf:
        in_specs += [stat, rowstat]
        args += list(cf)
        out_specs += [stat, rowstat]
        out_shape += [jax.ShapeDtypeStruct((H, T, 1), F32), jax.ShapeDtypeStruct((H, 1, T), F32)]
    return pl.pallas_call(
        body, name=name, grid=(H // HG, nq), in_specs=in_specs, out_specs=tuple(out_specs),
        out_shape=tuple(out_shape), compiler_params=_params(("parallel", "arbitrary")),
    )(*args)


def _headnorm(x, g):
    r = lax.rsqrt(jnp.mean(x * x, axis=-1, keepdims=True) + EPS)
    return x * r * g, r


def _headnorm_bwd(x, g, dy, n):
    r = lax.rsqrt(jnp.sum(x * x, axis=-1, keepdims=True) * (1.0 / n) + EPS)
    t = dy * g
    dx = r * t - x * (r * r * r * (jnp.sum(t * x, axis=-1, keepdims=True) * (1.0 / n)))
    dg = jnp.sum(dy * x * r, axis=0, keepdims=True)
    return dx, dg


def _fox_prep_fwd(qkv, gq, gk, H, *, tm=512):
    T = qkv.shape[0]
    tm = _tile(T, tm)

    def body(q_ref, k_ref, v_ref, gq_ref, gk_ref, qn_ref, kn_ref, vb_ref):
        qn_ref[...] = _headnorm(q_ref[...], gq_ref[...])[0].astype(BF16)
        kn_ref[...] = _headnorm(k_ref[...], gk_ref[...])[0].astype(BF16)
        vb_ref[...] = v_ref[...].astype(BF16)

    def blk(off):
        return pl.BlockSpec((tm, HEAD_DIM), lambda i, h: (i, off + h))

    vec = pl.BlockSpec((1, HEAD_DIM), lambda i, h: (0, 0))
    W = H * HEAD_DIM
    return pl.pallas_call(
        body, name="fox_prep_fwd", grid=(T // tm, H),
        in_specs=[blk(0), blk(H), blk(2 * H), vec, vec], out_specs=(blk(0), blk(0), blk(0)),
        out_shape=tuple(jax.ShapeDtypeStruct((T, W), BF16) for _ in range(3)),
        compiler_params=_params(("parallel", "parallel")),
    )(qkv, qkv, qkv, gq, gk)


def _fox_prep_bwd(qkv, gq, gk, dqn, dkn, dv, H, *, tm=512):
    T = qkv.shape[0]
    tm = _tile(T, tm)

    def body(q_ref, k_ref, gq_ref, gk_ref, dqn_ref, dkn_ref, dv_ref,
             dq_ref, dk_ref, dvb_ref, dgq_ref, dgk_ref):
        @pl.when((pl.program_id(0) == 0) & (pl.program_id(1) == 0))
        def _():
            dgq_ref[...] = jnp.zeros_like(dgq_ref)
            dgk_ref[...] = jnp.zeros_like(dgk_ref)

        dq, dgq = _headnorm_bwd(q_ref[...], gq_ref[...], dqn_ref[...], HEAD_DIM)
        dk, dgk = _headnorm_bwd(k_ref[...], gk_ref[...], dkn_ref[...], HEAD_DIM)
        dq_ref[...] = dq.astype(BF16)
        dk_ref[...] = dk.astype(BF16)
        dvb_ref[...] = dv_ref[...].astype(BF16)
        dgq_ref[...] += dgq
        dgk_ref[...] += dgk

    def blk(off):
        return pl.BlockSpec((tm, HEAD_DIM), lambda i, h: (i, off + h))

    vec = pl.BlockSpec((1, HEAD_DIM), lambda i, h: (0, 0))
    W = H * HEAD_DIM
    return pl.pallas_call(
        body, name="fox_prep_bwd", grid=(T // tm, H),
        in_specs=[blk(0), blk(H), vec, vec, blk(0), blk(0), blk(0)],
        out_specs=(blk(0), blk(0), blk(0), vec, vec),
        out_shape=tuple(jax.ShapeDtypeStruct((T, W), BF16) for _ in range(3))
        + (jax.ShapeDtypeStruct((1, HEAD_DIM), F32), jax.ShapeDtypeStruct((1, HEAD_DIM), F32)),
        compiler_params=_params(("arbitrary", "arbitrary")),
    )(qkv, qkv, gq, gk, dqn, dkn, dv)


def _fox_gate_fwd(flog, bf, *, tm=256):
    T = flog.shape[0]
    tm = _tile(T, tm)

    def body(f_ref, b_ref, cf_ref, carry):
        @pl.when(pl.program_id(0) == 0)
        def _():
            carry[...] = jnp.zeros_like(carry)

        lf = _log_sigmoid(f_ref[...] + b_ref[...])
        tri = (_iota2((tm, tm), 1) <= _iota2((tm, tm), 0)).astype(BF16)
        cf_ref[...] = carry[...] + _tri_left(tri, lf)
        carry[...] += jnp.sum(lf, axis=0, keepdims=True)

    return pl.pallas_call(
        body, name="fox_gate_fwd", grid=(T // tm,),
        in_specs=[pl.BlockSpec((tm, LANE), lambda i: (i, 0)), pl.BlockSpec((1, LANE), lambda i: (0, 0))],
        out_specs=pl.BlockSpec((tm, LANE), lambda i: (i, 0)),
        out_shape=jax.ShapeDtypeStruct((T, LANE), F32),
        scratch_shapes=[pltpu.VMEM((1, LANE), F32)],
        compiler_params=_params(("arbitrary",)),
    )(flog, bf)


def _fox_gate_bwd(flog, bf, dcf, *, tm=256):
    T = flog.shape[0]
    tm = _tile(T, tm)
    nt = T // tm

    def body(f_ref, b_ref, dcf_ref, df_ref, db_ref, carry):
        @pl.when(pl.program_id(0) == 0)
        def _():
            carry[...] = jnp.zeros_like(carry)
            db_ref[...] = jnp.zeros_like(db_ref)

        d = dcf_ref[...]
        tri = (_iota2((tm, tm), 1) >= _iota2((tm, tm), 0)).astype(BF16)
        dlf = carry[...] + _tri_left(tri, d)
        carry[...] += jnp.sum(d, axis=0, keepdims=True)
        xg = f_ref[...] + b_ref[...]
        e = jnp.exp(-jnp.abs(xg))
        sig_neg = jnp.where(xg >= 0.0, e, 1.0) / (1.0 + e)
        df = dlf * sig_neg
        df_ref[...] = df.astype(BF16)
        db_ref[...] += jnp.sum(df, axis=0, keepdims=True)

    rev = pl.BlockSpec((tm, LANE), lambda i: (nt - 1 - i, 0))
    vec = pl.BlockSpec((1, LANE), lambda i: (0, 0))
    return pl.pallas_call(
        body, name="fox_gate_bwd", grid=(nt,), in_specs=[rev, vec, rev], out_specs=(rev, vec),
        out_shape=(jax.ShapeDtypeStruct((T, LANE), BF16), jax.ShapeDtypeStruct((1, LANE), F32)),
        scratch_shapes=[pltpu.VMEM((1, LANE), F32)],
        compiler_params=_params(("arbitrary",)),
    )(flog, bf, dcf)


def _rope_tables(positions):
    half = MLA_ROPE // 2
    inv_freq = ROPE_THETA ** (-jnp.arange(0, half, dtype=F32) * 2.0 / MLA_ROPE)
    ang = positions.astype(F32)[:, None] * inv_freq
    cos, sin = jnp.cos(ang), jnp.sin(ang)
    zero = jnp.zeros_like(cos)
    pad = jnp.zeros((positions.shape[0], LANE - MLA_ROPE), F32)
    cos_t = jnp.concatenate([cos, cos, pad], axis=1)
    sin_up = jnp.concatenate([zero, sin, pad], axis=1)
    sin_dn = jnp.concatenate([-sin, zero, pad], axis=1)
    return cos_t, sin_up, sin_dn


def _rope(x, cos_t, sin_up, sin_dn):
    half = MLA_ROPE // 2
    return x * cos_t + pltpu.roll(x, half, 1) * sin_up + pltpu.roll(x, LANE - half, 1) * sin_dn


def _rope_t(d, cos_t, sin_up, sin_dn):
    half = MLA_ROPE // 2
    return d * cos_t + pltpu.roll(d * sin_up, LANE - half, 1) + pltpu.roll(d * sin_dn, half, 1)


def _norm192(xcat, g):
    r = lax.rsqrt(jnp.sum(xcat * xcat, axis=-1, keepdims=True) * (1.0 / MLA_QK) + EPS)
    return xcat * r * g


def _mla_prep_fwd(qfull, kv, kr, tabs, gq, gk, H, *, tm=512):
    T = qfull.shape[0]
    tm = _tile(T, tm)

    def body(q_ref, kv_ref, kr_ref, c_ref, su_ref, sd_ref, gq_ref, gk_ref, qf_ref, kf_ref, v_ref):
        tabs_b = (c_ref[...], su_ref[...], sd_ref[...])
        qb = q_ref[...]
        qcat = jnp.concatenate([qb[:, :MLA_NOPE], _rope(qb[:, MLA_NOPE:], *tabs_b)], axis=1)
        qf_ref[...] = _norm192(qcat, gq_ref[...]).astype(BF16)
        kvb = kv_ref[...]
        kcat = jnp.concatenate([kvb[:, :MLA_NOPE], _rope(kr_ref[...], *tabs_b)], axis=1)
        kf_ref[...] = _norm192(kcat, gk_ref[...]).astype(BF16)
        v_ref[...] = kvb[:, MLA_NOPE:].astype(BF16)

    head = pl.BlockSpec((tm, MLA_QK_PAD), lambda i, h: (i, h))
    tok = pl.BlockSpec((tm, LANE), lambda i, h: (i, 0))
    vec = pl.BlockSpec((1, MLA_QK_PAD), lambda i, h: (0, 0))
    return pl.pallas_call(
        body, name="mla_prep_fwd", grid=(T // tm, H),
        in_specs=[head, head, tok, tok, tok, tok, vec, vec],
        out_specs=(head, head, pl.BlockSpec((tm, MLA_V), lambda i, h: (i, h))),
        out_shape=(jax.ShapeDtypeStruct((T, H * MLA_QK_PAD), BF16),
                   jax.ShapeDtypeStruct((T, H * MLA_QK_PAD), BF16),
                   jax.ShapeDtypeStruct((T, H * MLA_V), BF16)),
        compiler_params=_params(("parallel", "parallel")),
    )(qfull, kv, kr, *tabs, gq, gk)


def _mla_prep_bwd(qfull, kv, kr, tabs, gq, gk, dqf, dkf, dv, H, *, tm=512):
    T = qfull.shape[0]
    tm = _tile(T, tm)

    def body(q_ref, kv_ref, kr_ref, c_ref, su_ref, sd_ref, gq_ref, gk_ref, dqf_ref, dkf_ref, dv_ref,
             dq_ref, dkv_ref, dkr_ref, dgq_ref, dgk_ref, kr_acc):
        h = pl.program_id(1)

        @pl.when((pl.program_id(0) == 0) & (h == 0))
        def _():
            dgq_ref[...] = jnp.zeros_like(dgq_ref)
            dgk_ref[...] = jnp.zeros_like(dgk_ref)

        @pl.when(h == 0)
        def _():
            kr_acc[...] = jnp.zeros_like(kr_acc)

        tabs_b = (c_ref[...], su_ref[...], sd_ref[...])
        qb = q_ref[...]
        qcat = jnp.concatenate([qb[:, :MLA_NOPE], _rope(qb[:, MLA_NOPE:], *tabs_b)], axis=1)
        dqcat, dgq = _headnorm_bwd(qcat, gq_ref[...], dqf_ref[...], MLA_QK)
        dq_ref[...] = jnp.concatenate(
            [dqcat[:, :MLA_NOPE], _rope_t(dqcat[:, MLA_NOPE:], *tabs_b)], axis=1).astype(BF16)
        dgq_ref[...] += dgq
        kvb = kv_ref[...]
        kcat = jnp.concatenate([kvb[:, :MLA_NOPE], _rope(kr_ref[...], *tabs_b)], axis=1)
        dkcat, dgk = _headnorm_bwd(kcat, gk_ref[...], dkf_ref[...], MLA_QK)
        dkv_ref[...] = jnp.concatenate([dkcat[:, :MLA_NOPE], dv_ref[...]], axis=1).astype(BF16)
        dgk_ref[...] += dgk
        kr_acc[...] += dkcat[:, MLA_NOPE:]

        @pl.when(h == H - 1)
        def _():
            dkr_ref[...] = _rope_t(kr_acc[...], *tabs_b).astype(BF16)

    head = pl.BlockSpec((tm, MLA_QK_PAD), lambda i, h: (i, h))
    tok = pl.BlockSpec((tm, LANE), lambda i, h: (i, 0))
    vec = pl.BlockSpec((1, MLA_QK_PAD), lambda i, h: (0, 0))
    return pl.pallas_call(
        body, name="mla_prep_bwd", grid=(T // tm, H),
        in_specs=[head, head, tok, tok, tok, tok, vec, vec, head, head,
                  pl.BlockSpec((tm, MLA_V), lambda i, h: (i, h))],
        out_specs=(head, head, tok, vec, vec),
        out_shape=(jax.ShapeDtypeStruct((T, H * MLA_QK_PAD), BF16),
                   jax.ShapeDtypeStruct((T, H * MLA_QK_PAD), BF16),
                   jax.ShapeDtypeStruct((T, LANE), BF16),
                   jax.ShapeDtypeStruct((1, MLA_QK_PAD), F32), jax.ShapeDtypeStruct((1, MLA_QK_PAD), F32)),
        scratch_shapes=[pltpu.VMEM((tm, LANE), F32)],
        compiler_params=_params(("arbitrary", "arbitrary")),
    )(qfull, kv, kr, *tabs, gq, gk, dqf, dkf, dv)


def _cf_layouts(cf, H):
    cfh = cf[:, :H].T
    return cfh[:, :, None], cfh[:, None, :]


def _layer_fwd(i, x, P, tabs, H, hook=None):
    kind, j = i % 3, i // 3
    s = {"x_in": x}
    h = _rmsnorm_fwd(x, P["mix_norm"][i:i + 1], name="mix_norm_fwd")
    s["h"] = h
    if kind == 0:
        w = P["sb"][j]
        qkv = _mm(h, w["w_in"], out_dtype=BF16, name="sb_qkv")
        o, ctot = _sb_fwd(qkv, H)
        s.update(qkv=qkv, o=o, ctot=ctot)
    elif kind == 1:
        w = P["fox"][j]
        qkv = _mm(h, w["w_qkv"], name="fox_qkv")
        flog = _mm(h, w["w_f"], name="fox_gate_proj")
        qn, kn, vb = _fox_prep_fwd(qkv, w["gq"], w["gk"], H)
        cf = _cf_layouts(_fox_gate_fwd(flog, w["bf"]), H)
        o, of, lse = _softmax_fwd(qn, kn, vb, cf, H, HEAD_DIM, 1.0 / math.sqrt(HEAD_DIM), name="fox_attn_fwd")
        s.update(qkv=qkv, flog=flog, qn=qn, kn=kn, vb=vb, cf=cf, o=o, of=of, lse=lse)
    else:
        w = P["mla"][j]
        dq = _mm(h, w["w_dq"], name="mla_down_q")
        dkv = _mm(h, w["w_dkv"], name="mla_down_kv")
        kr = _mm(h, w["w_dr"], name="mla_down_rope")
        cq = _rmsnorm_fwd(dq, w["q_norm"], name="mla_q_norm_fwd")
        ckv = _rmsnorm_fwd(dkv, w["kv_norm"], name="mla_kv_norm_fwd")
        qfull = _mm(cq, w["w_uq"], name="mla_up_q")
        kv = _mm(ckv, w["w_ukv"], name="mla_up_kv")
        qf, kf, v = _mla_prep_fwd(qfull, kv, kr, tabs, w["gq"], w["gk"], H)
        o, of, lse = _softmax_fwd(qf, kf, v, None, H, MLA_QK_PAD, 1.0 / math.sqrt(MLA_QK), name="mla_attn_fwd")
        s.update(dq=dq, dkv=dkv, kr=kr, cq=cq, ckv=ckv, qfull=qfull, kv=kv, qf=qf, kf=kf, v=v,
                 o=o, of=of, lse=lse)
    x = _mm(s["o"], w["w_out"], res=x, name="mixer_out")
    s["x_mid"] = x
    deps = hook(x) if hook is not None else ()
    h2 = _rmsnorm_fwd(x, P["mlp_norm"][i:i + 1], name="mlp_norm_fwd")
    z, u = _mm(h2, P["mlp"][i]["w1"], act="relu2", name="mlp_up", deps=deps)
    x = _mm(u, P["mlp"][i]["w2"], res=x, name="mlp_down")
    s.update(h2=h2, z=z, u=u)
    return x, s


def _layer_bwd(i, dx, dxb, s, P, tabs, H, deps=(), hook=None, hook2=None):
    kind, j = i % 3, i // 3
    G = {}
    wm = P["mlp"][i]
    dz = _mm(dxb, wm["w2"], mode="nt", act="drelu2", z=s["z"], out_dtype=BF16, name="mlp_down_bwd",
             deps=deps)
    G["w2"] = _mm(s["u"], dxb, mode="tn", out_dtype=BF16, name="mlp_w2_grad")
    G["w1"] = _mm(s["h2"], dz, mode="tn", out_dtype=BF16, name="mlp_w1_grad")
    dh2 = _mm(dz, wm["w1"], mode="nt", name="mlp_up_bwd")
    dx, dxb, G["mlp_norm"] = _rmsnorm_bwd(s["x_mid"], P["mlp_norm"][i:i + 1], dh2, dx, name="mlp_norm_bwd")
    h = s["h"]
    w = P[("sb", "fox", "mla")[kind]][j]
    do = _mm(dxb, w["w_out"], mode="nt", out_dtype=BF16, name="mixer_out_bwd",
             deps=hook(dxb, G) if hook is not None else ())
    G["w_out"] = _mm(s["o"], dxb, mode="tn", out_dtype=BF16, name="mixer_out_grad")
    if kind == 0:
        dq, dk, dv = _sb_bwd(s["qkv"], do, s["ctot"], H)
        dproj = jnp.concatenate([dq, dk, dv], axis=1)
        G["w_in"] = _mm(h, dproj, mode="tn", out_dtype=BF16, name="sb_qkv_grad",
                        deps=hook2(dv) if hook2 is not None else ())
        dh = _mm(dproj, w["w_in"], mode="nt", name="sb_qkv_bwd")
    elif kind == 1:
        dqn, dkn, dv, dcc, dcr = _softmax_bwd(
            s["qn"], s["kn"], s["vb"], s["cf"], do, s["of"], s["lse"], H, HEAD_DIM,
            1.0 / math.sqrt(HEAD_DIM), name="fox_attn_bwd")
        dq, dk, dvb, G["gq"], G["gk"] = _fox_prep_bwd(s["qkv"], w["gq"], w["gk"], dqn, dkn, dv, H)
        dcf = (dcc[:, :, 0] + dcr[:, 0, :]).T
        dcf = jnp.pad(dcf, ((0, 0), (0, LANE - H)))
        dflog, G["bf"] = _fox_gate_bwd(s["flog"], w["bf"], dcf)
        dproj = jnp.concatenate([dq, dk, dvb], axis=1)
        G["w_qkv"] = _mm(h, dproj, mode="tn", out_dtype=BF16, name="fox_qkv_grad",
                         deps=hook2(dv) if hook2 is not None else ())
        G["w_f"] = _mm(h, dflog, mode="tn", out_dtype=BF16, name="fox_gate_grad")
        dh = _mm(dproj, w["w_qkv"], mode="nt", name="fox_qkv_bwd")
        dh = _mm(dflog, w["w_f"], mode="nt", res=dh, name="fox_gate_bwd_proj")
    else:
        dqf, dkf, dv = _softmax_bwd(
            s["qf"], s["kf"], s["v"], None, do, s["of"], s["lse"], H, MLA_QK_PAD,
            1.0 / math.sqrt(MLA_QK), name="mla_attn_bwd")
        dqfull, dkv, dkr, G["gq"], G["gk"] = _mla_prep_bwd(
            s["qfull"], s["kv"], s["kr"], tabs, w["gq"], w["gk"], dqf, dkf, dv, H)
        G["w_uq"] = _mm(s["cq"], dqfull, mode="tn", out_dtype=BF16, name="mla_up_q_grad",
                        deps=hook2(dv) if hook2 is not None else ())
        G["w_ukv"] = _mm(s["ckv"], dkv, mode="tn", out_dtype=BF16, name="mla_up_kv_grad")
        dcq = _mm(dqfull, w["w_uq"], mode="nt", name="mla_up_q_bwd")
        dckv = _mm(dkv, w["w_ukv"], mode="nt", name="mla_up_kv_bwd")
        ddq, G["q_norm"] = _rmsnorm_bwd(s["dq"], w["q_norm"], dcq, want_f32=False, name="mla_q_norm_bwd")
        ddkv, G["kv_norm"] = _rmsnorm_bwd(s["dkv"], w["kv_norm"], dckv, want_f32=False, name="mla_kv_norm_bwd")
        G["w_dq"] = _mm(h, ddq, mode="tn", out_dtype=BF16, name="mla_down_q_grad")
        G["w_dkv"] = _mm(h, ddkv, mode="tn", out_dtype=BF16, name="mla_down_kv_grad")
        G["w_dr"] = _mm(h, dkr, mode="tn", out_dtype=BF16, name="mla_down_rope_grad")
        dh = _mm(ddq, w["w_dq"], mode="nt", name="mla_down_q_bwd")
        dh = _mm(ddkv, w["w_dkv"], mode="nt", res=dh, name="mla_down_kv_bwd")
        dh = _mm(dkr, w["w_dr"], mode="nt", res=dh, name="mla_down_rope_bwd")
    dx, dxb, G["mix_norm"] = _rmsnorm_bwd(s["x_in"], P["mix_norm"][i:i + 1], dh, dx, name="mix_norm_bwd")
    return dx, dxb, G


def _local_step(x, positions, loss_target, P, depth, H):
    tabs = _rope_tables(positions)
    saved = []
    for i in range(depth):
        x, s = _layer_fwd(i, x, P, tabs, H)
        saved.append(s)
    dx, dxb, loss = _loss_head(x, loss_target)
    grads = [None] * depth
    for i in reversed(range(depth)):
        dx, dxb, grads[i] = _layer_bwd(i, dx, dxb, saved[i], P, tabs, H)
    return loss, dx, grads


def _pad_cols(a, n):
    return jnp.pad(a, ((0, 0), (0, n - a.shape[1])))


def _prepare_layer(P, full, i, H, parts=("mix", "mlp")):
    W = H * HEAD_DIM
    kind, j = i % 3, i // 3
    if "mlp" in parts:
        P["mlp"][i] = {"w1": full["mlp_w1"][i], "w2": full["mlp_w2"][i]}
    if "mix" not in parts:
        return
    if kind == 0:
        P["sb"][j] = {"w_in": full["sb_w_in"][j], "w_out": full["sb_w_out"][j]}
    elif kind == 1:
        w = full["fox_w_in"][j]
        P["fox"][j] = {
            "w_qkv": w[:, :3 * W], "w_f": _pad_cols(w[:, 3 * W:], LANE),
            "bf": _pad_cols(full["fox_b_f"][j:j + 1], LANE),
            "gq": full["fox_q_gain"][j:j + 1], "gk": full["fox_k_gain"][j:j + 1],
            "w_out": full["fox_w_out"][j]}
    else:
        w = full["mla_w_in"][j]
        q_norm, kv_norm = full["mla_q_norm"][j], full["mla_kv_norm"][j]
        rq, rkv = q_norm.shape[0], kv_norm.shape[0]
        w_uq = full["mla_w_uq"][j].reshape(rq, H, MLA_QK)
        w_uq = jnp.pad(w_uq, ((0, 0), (0, 0), (0, MLA_QK_PAD - MLA_QK))).reshape(rq, H * MLA_QK_PAD)
        P["mla"][j] = {
            "w_dq": w[:, :rq], "w_dkv": w[:, rq:rq + rkv], "w_dr": _pad_cols(w[:, rq + rkv:], LANE),
            "q_norm": q_norm[None], "kv_norm": kv_norm[None],
            "w_uq": w_uq, "w_ukv": full["mla_w_ukv"][j],
            "gq": _pad_cols(full["mla_q_gain"][j:j + 1], MLA_QK_PAD),
            "gk": _pad_cols(full["mla_k_gain"][j:j + 1], MLA_QK_PAD),
            "w_out": full["mla_w_out"][j]}


def _prepare(full, H):
    depth = len(full["mlp_w1"])
    P = {"mix_norm": full["mix_norm"], "mlp_norm": full["mlp_norm"], "mlp": [None] * depth,
         "sb": [None] * len(full["sb_w_in"]), "fox": [None] * len(full["fox_w_in"]),
         "mla": [None] * len(full["mla_w_in"])}
    for i in range(depth):
        _prepare_layer(P, full, i, H)
    return P


def _layer_grads(i, G, H):
    kind = i % 3
    out = {"mix_norm": G["mix_norm"], "mlp_norm": G["mlp_norm"], "mlp_w1": G["w1"], "mlp_w2": G["w2"]}
    if kind == 0:
        out.update(sb_w_in=G["w_in"], sb_w_out=G["w_out"])
    elif kind == 1:
        out.update(fox_w_in=jnp.concatenate([G["w_qkv"], G["w_f"][:, :H]], axis=1), fox_b_f=G["bf"][:, :H],
                   fox_q_gain=G["gq"], fox_k_gain=G["gk"], fox_w_out=G["w_out"])
    else:
        rq = G["w_uq"].shape[0]
        out.update(
            mla_w_in=jnp.concatenate([G["w_dq"], G["w_dkv"], G["w_dr"][:, :MLA_ROPE]], axis=1),
            mla_q_norm=G["q_norm"], mla_kv_norm=G["kv_norm"],
            mla_w_uq=G["w_uq"].reshape(rq, H, MLA_QK_PAD)[:, :, :MLA_QK].reshape(rq, H * MLA_QK),
            mla_w_ukv=G["w_ukv"], mla_q_gain=G["gq"][:, :MLA_QK], mla_k_gain=G["gk"][:, :MLA_QK],
            mla_w_out=G["w_out"])
    return out


def _layer_of(name, j):
    if name.startswith("sb_"):
        return 3 * j
    if name.startswith("fox_"):
        return 3 * j + 1
    if name.startswith("mla_"):
        return 3 * j + 2
    return j


def _unprepare(grads, H):
    out = {}
    for i, G in enumerate(grads):
        for n, g in _layer_grads(i, G, H).items():
            out.setdefault(n, []).append(g)
    return out


class _Place:
    def __init__(self, x, y, c):
        self.x, self.y, self.c = x, y, c
        self.dev = 4 * x + 2 * y + c
        self.chip = 2 * x + y
        self.id = (x, y, c)


def _peers(me, kind):
    if kind == "ici":
        return [_Place(1 - me.x, me.y, me.c), _Place(me.x, 1 - me.y, me.c), _Place(1 - me.x, 1 - me.y, me.c)]
    return [_Place(me.x, me.y, 1 - me.c)]


def _exchange(name, kind, operands, out_shapes, aliases, n_remote, n_local, plan):
    n_in, n_out = len(operands), len(out_shapes)

    def body(*refs):
        in_refs, out_refs = refs[:n_in], refs[n_in:n_in + n_out]
        send_sems, recv_sems, local_sems = refs[n_in + n_out:]
        me = _Place(lax.axis_index("x"), lax.axis_index("y"), lax.axis_index("c"))
        peers = _peers(me, kind)
        remote, local = plan(me, peers, in_refs, out_refs)
        assert len(remote) == n_remote and len(local) == n_local
        sends = []
        for n, (src, dst, k, _) in enumerate(remote):
            cp = pltpu.make_async_remote_copy(
                src_ref=src, dst_ref=dst, send_sem=send_sems.at[n], recv_sem=recv_sems.at[n],
                device_id=peers[k].id, device_id_type=MESH)
            cp.start()
            sends.append(cp)
        own = []
        for n, (src, dst) in enumerate(local):
            cp = pltpu.make_async_copy(src, dst, local_sems.at[n])
            cp.start()
            own.append(cp)
        for n, (src, _, k, landing) in enumerate(remote):
            pltpu.make_async_remote_copy(
                src_ref=src, dst_ref=landing, send_sem=send_sems.at[n], recv_sem=recv_sems.at[n],
                device_id=peers[k].id, device_id_type=MESH).wait_recv()
        for cp in sends:
            cp.wait_send()
        for cp in own:
            cp.wait()

    outs = pl.pallas_call(
        body, name=name, in_specs=[ANY] * n_in, out_specs=tuple([ANY] * n_out),
        out_shape=tuple(out_shapes), input_output_aliases=aliases,
        scratch_shapes=[pltpu.SemaphoreType.DMA((n_remote,)), pltpu.SemaphoreType.DMA((n_remote,)),
                        pltpu.SemaphoreType.DMA((max(n_local, 1),))],
    )(*operands)
    return list(outs)


def _window(ref, kind, d, shard_shape):
    r, c = shard_shape
    if kind == "col":
        return ref.at[:, pl.ds(pl.multiple_of(d * c, c), c)]
    return ref.at[pl.ds(pl.multiple_of(d * r, r), r), :]


def _full_shape(kind, shard_shape):
    r, c = shard_shape
    return (r, N_DEV * c) if kind == "col" else (N_DEV * r, c)


HBM = pl.BlockSpec(memory_space=pltpu.HBM)
SEM = pl.BlockSpec(memory_space=pltpu.SEMAPHORE)
EFFECT = pltpu.SideEffectType.DATAFLOW_SIDE_EFFECTING


class _Flight:
    def __init__(self, name, kind, n_remote, plan, send_sems, recv_sems, bufs, token):
        self.name, self.kind, self.n_remote, self.plan = name, kind, n_remote, plan
        self.send_sems, self.recv_sems, self.bufs, self.token = send_sems, recv_sems, bufs, token


def _start_copies(name, kind, bufs, n_remote, plan, after=None):
    nb = len(bufs)
    n_after = 0 if after is None else 1

    def body(*refs):
        in_refs = refs[:nb]
        send_sems, recv_sems = refs[nb + n_after], refs[nb + n_after + 1]
        token = refs[2 * nb + n_after + 2]
        me = _Place(lax.axis_index("x"), lax.axis_index("y"), lax.axis_index("c"))
        peers = _peers(me, kind)
        for n, (src, dst, k, _) in enumerate(plan(me, peers, in_refs)):
            pltpu.make_async_remote_copy(
                src_ref=src, dst_ref=dst, send_sem=send_sems.at[n], recv_sem=recv_sems.at[n],
                device_id=peers[k].id, device_id_type=MESH).start()
        token[...] = jnp.zeros_like(token)

    outs = pl.pallas_call(
        body, name=name, in_specs=[HBM] * nb + [ANY] * n_after,
        out_specs=(SEM, SEM) + (HBM,) * nb + (pl.BlockSpec(memory_space=pltpu.VMEM),),
        out_shape=(pltpu.SemaphoreType.DMA((n_remote,)), pltpu.SemaphoreType.DMA((n_remote,)))
        + tuple(pltpu.HBM(b.shape, b.dtype) for b in bufs) + (jax.ShapeDtypeStruct((8, LANE), F32),),
        input_output_aliases={a: 2 + a for a in range(nb)},
        compiler_params=pltpu.CompilerParams(has_side_effects=EFFECT),
    )(*[pltpu.with_memory_space_constraint(b, pltpu.HBM) for b in bufs], *([after] if n_after else []))
    return _Flight(name, kind, n_remote, plan, outs[0], outs[1], list(outs[2:2 + nb]), outs[2 + nb])


def _wait_copies(flight, after):
    nb = len(flight.bufs)
    plan, kind = flight.plan, flight.kind
    after = list(after) if isinstance(after, (tuple, list)) else [after]

    def body(*refs):
        in_refs = refs[:nb]
        send_sems, recv_sems = refs[nb], refs[nb + 1]
        me = _Place(lax.axis_index("x"), lax.axis_index("y"), lax.axis_index("c"))
        peers = _peers(me, kind)
        for n, (src, _, k, landing) in enumerate(plan(me, peers, in_refs)):
            cp = pltpu.make_async_remote_copy(
                src_ref=src, dst_ref=landing, send_sem=send_sems.at[n], recv_sem=recv_sems.at[n],
                device_id=peers[k].id, device_id_type=MESH)
            cp.wait_send()
            cp.wait_recv()

    outs = pl.pallas_call(
        body, name=flight.name.replace("_start", "_wait"),
        in_specs=[HBM] * nb + [SEM, SEM] + [ANY] * len(after), out_specs=(HBM,) * nb,
        out_shape=tuple(pltpu.HBM(b.shape, b.dtype) for b in flight.bufs),
        input_output_aliases={a: a for a in range(nb)},
        compiler_params=pltpu.CompilerParams(has_side_effects=EFFECT),
    )(*flight.bufs, flight.send_sems, flight.recv_sems, *after)
    return list(outs)


def _own_in_place(shard, kind, dev, dtype, deps=()):
    r, c = shard.shape
    tr = _row_tile(r, c)
    nb = r // tr

    def body(dev_ref, s_ref, *rest):
        o_ref = rest[-1]
        o_ref[...] = s_ref[...].astype(dtype)

    if kind == "col":
        o_spec = pl.BlockSpec((tr, c), lambda i, dev_ref: (i, dev_ref[0]))
    else:
        o_spec = pl.BlockSpec((tr, c), lambda i, dev_ref: (dev_ref[0] * nb + i, 0))
    return pl.pallas_call(
        body, name="own_in_place",
        grid_spec=pltpu.PrefetchScalarGridSpec(
            num_scalar_prefetch=1, grid=(nb,),
            in_specs=[pl.BlockSpec((tr, c), lambda i, dev_ref: (i, 0))] + [ANY] * len(deps), out_specs=o_spec),
        out_shape=jax.ShapeDtypeStruct(_full_shape(kind, shard.shape), dtype),
        compiler_params=_params(("parallel",)),
    )(dev, shard, *deps)


def _gather_ici_start(fulls, kinds, shapes, tag, after=None):
    n = len(fulls)

    def plan(me, peers, refs):
        remote = []
        for a in range(n):
            mine = _window(refs[a], kinds[a], me.dev, shapes[a])
            for k, p in enumerate(peers):
                remote.append((mine, mine, k, _window(refs[a], kinds[a], p.dev, shapes[a])))
        return remote

    return _start_copies("gather_ici_start_" + tag, "ici", list(fulls), 3 * n, plan, after)


def _gather_d2d_start(fulls, kinds, shapes, tag):
    n = len(fulls)

    def plan(me, peers, refs):
        remote = []
        for a in range(n):
            for ch in range(N_CHIP):
                held = _window(refs[a], kinds[a], 2 * ch + me.c, shapes[a])
                remote.append((held, held, 0, _window(refs[a], kinds[a], 2 * ch + 1 - me.c, shapes[a])))
        return remote

    return _start_copies("gather_d2d_start_" + tag, "d2d", fulls, N_CHIP * n, plan)


def _scatter_d2d_start(grads, kinds, shapes, tag):
    n = len(grads)
    lands = [lax.empty((N_CHIP,) + tuple(s), g.dtype) for g, s in zip(grads, shapes)]

    def plan(me, peers, refs):
        remote = []
        for a in range(n):
            for ch in range(N_CHIP):
                remote.append((_window(refs[a], kinds[a], 2 * ch + 1 - me.c, shapes[a]),
                               refs[n + a].at[ch], 0, refs[n + a].at[ch]))
        return remote

    return _start_copies("scatter_d2d_start_" + tag, "d2d", list(grads) + lands, N_CHIP * n, plan)


def _scatter_ici_start(sums, shapes, tag, after=None):
    n = len(sums)
    lands = [lax.empty((N_CHIP - 1,) + tuple(s), BF16) for s in shapes]

    def plan(me, peers, refs):
        remote = []
        for a in range(n):
            for k, p in enumerate(peers):
                remote.append((refs[a].at[p.chip], refs[n + a].at[k], k, refs[n + a].at[k]))
        return remote

    return _start_copies("scatter_ici_start_" + tag, "ici", list(sums) + lands, 3 * n, plan, after)


def _pair_add(g, r4, kind, shard_shape, core, *, name):
    r, c = shard_shape
    tr = _row_tile(r, c, 524288)
    nb = r // tr

    def body(core_ref, g_ref, r_ref, o_ref):
        o_ref[...] = (g_ref[...].astype(F32) + r_ref[...].astype(F32)).astype(BF16)

    if kind == "col":
        g_spec = pl.BlockSpec((tr, c), lambda ch, i, core_ref: (i, 2 * ch + core_ref[0]))
    else:
        g_spec = pl.BlockSpec((tr, c), lambda ch, i, core_ref: ((2 * ch + core_ref[0]) * nb + i, 0))
    slot = pl.BlockSpec((None, tr, c), lambda ch, i, core_ref: (ch, i, 0))
    return pl.pallas_call(
        body, name=name,
        grid_spec=pltpu.PrefetchScalarGridSpec(
            num_scalar_prefetch=1, grid=(N_CHIP, nb), in_specs=[g_spec, slot], out_specs=slot),
        out_shape=jax.ShapeDtypeStruct((N_CHIP, r, c), BF16),
        compiler_params=_params(("parallel", "parallel")),
    )(core, g, r4)


def _allreduce_small_start(part, chip):
    def plan_swap(me, peers, ins, outs):
        return [(ins[0], outs[0], 0, outs[0])], []

    (got,) = _exchange("allreduce_small_swap", "d2d", [part], [jax.ShapeDtypeStruct(part.shape, F32)],
                       {}, 1, 0, plan_swap)

    def body(chip_ref, a_ref, b_ref, o_ref):
        o_ref[...] = a_ref[...] + b_ref[...]

    whole = pl.BlockSpec(part.shape, lambda i, chip_ref: (0, 0))
    slots = pl.pallas_call(
        body, name="allreduce_small_pair",
        grid_spec=pltpu.PrefetchScalarGridSpec(
            num_scalar_prefetch=1, grid=(1,), in_specs=[whole, whole],
            out_specs=pl.BlockSpec((None,) + part.shape, lambda i, chip_ref: (chip_ref[0], 0, 0))),
        out_shape=jax.ShapeDtypeStruct((N_CHIP,) + part.shape, F32),
    )(chip, part, got)

    def plan(me, peers, refs):
        mine = refs[0].at[me.chip]
        return [(mine, mine, k, refs[0].at[p.chip]) for k, p in enumerate(peers)]

    return _start_copies("allreduce_small_start", "ici", [slots], 3, plan)


def _row_tile(r, c, limit=262144):
    best = None
    for t in range(8, r + 1, 8):
        if r % t == 0 and t * c <= limit:
            best = t
    return best if best is not None else r


def _adam_math(w, g, m, v):
    m = ADAM_B1 * m + (1.0 - ADAM_B1) * g
    v = ADAM_B2 * v + (1.0 - ADAM_B2) * (g * g)
    m_hat = m / (1.0 - ADAM_B1 ** ADAM_STEP)
    v_hat = v / (1.0 - ADAM_B2 ** ADAM_STEP)
    delta = -ADAM_LR * (m_hat / (jnp.sqrt(v_hat) + ADAM_EPS) + ADAM_WD * w)
    return delta, m, v


def _adamw(w, m, v, sums, lands, chip, layer, prev, *, name):
    L, r, c = w.shape
    tr = _row_tile(r, c)
    has_prev = prev is not None

    def body(chip_ref, *refs):
        w_ref, m_ref, v_ref, s_ref, l_ref = refs[:5]
        g_ref, d_ref, nm_ref, nv_ref = refs[-4:]
        g = (s_ref[...].astype(F32) + l_ref[0].astype(F32)) + l_ref[1].astype(F32) + l_ref[2].astype(F32)
        delta, nm, nv = _adam_math(w_ref[...], g, m_ref[...], v_ref[...])
        g_ref[...] = g
        d_ref[...] = delta
        nm_ref[...] = nm
        nv_ref[...] = nv

    slab = pl.BlockSpec((None, tr, c), lambda i, chip_ref: (layer, i, 0))
    in_specs = [slab, slab, slab, pl.BlockSpec((None, tr, c), lambda i, chip_ref: (chip_ref[0], i, 0)),
                pl.BlockSpec((N_CHIP - 1, tr, c), lambda i, chip_ref: (0, i, 0))]
    args = [chip, w, m, v, sums, lands]
    aliases = {}
    if has_prev:
        in_specs += [ANY] * 4
        args += list(prev)
        aliases = {6 + n: n for n in range(4)}
    return pl.pallas_call(
        body, name=name,
        grid_spec=pltpu.PrefetchScalarGridSpec(
            num_scalar_prefetch=1, grid=(r // tr,), in_specs=in_specs, out_specs=(slab,) * 4),
        out_shape=tuple(jax.ShapeDtypeStruct((L, r, c), F32) for _ in range(4)),
        input_output_aliases=aliases, compiler_params=_params(("parallel",)),
    )(*args)


def _adamw_small(w, g, m, v):
    def body(w_ref, g_ref, m_ref, v_ref, d_ref, nm_ref, nv_ref):
        d_ref[...], nm_ref[...], nv_ref[...] = _adam_math(w_ref[...], g_ref[...], m_ref[...], v_ref[...])

    return pl.pallas_call(
        body, name="adamw_small", out_shape=tuple(jax.ShapeDtypeStruct(w.shape, F32) for _ in range(3)),
    )(w, g, m, v)


def _sum4(parts):
    def body(p_ref, o_ref):
        o_ref[...] = (p_ref[0] + p_ref[1]) + p_ref[2] + p_ref[3]

    return pl.pallas_call(
        body, name="sum_chips", out_shape=jax.ShapeDtypeStruct(parts.shape[1:], F32))(parts)


_WEIGHTS = ["mix_norm", "mlp_norm", "sb_w_in", "sb_w_out", "fox_w_in", "fox_b_f", "fox_q_gain", "fox_k_gain",
            "fox_w_out", "mla_w_in", "mla_q_norm", "mla_kv_norm", "mla_w_uq", "mla_w_ukv", "mla_q_gain",
            "mla_k_gain", "mla_w_out", "mlp_w1", "mlp_w2"]
_BIG = {"sb_w_in": "col", "sb_w_out": "row", "fox_w_in": "row", "fox_w_out": "row", "mla_w_in": "row",
        "mla_w_uq": "col", "mla_w_ukv": "col", "mla_w_out": "row", "mlp_w1": "col", "mlp_w2": "row"}


def _layer_big(i):
    kind, j = i % 3, i // 3
    mixer = {0: ["sb_w_in", "sb_w_out"], 1: ["fox_w_in", "fox_w_out"],
             2: ["mla_w_in", "mla_w_uq", "mla_w_ukv", "mla_w_out"]}[kind]
    return [(n, j) for n in mixer] + [("mlp_w1", i), ("mlp_w2", i)]


def _stack_to_cols(a):
    r = a.shape[0] // N_DEV
    return a.reshape(N_DEV, r, a.shape[1]).transpose(1, 0, 2).reshape(r, N_DEV * a.shape[1])


def _cols_to_stack(a):
    c = a.shape[1] // N_DEV
    return a.reshape(a.shape[0], N_DEV, c).transpose(1, 0, 2).reshape(N_DEV * a.shape[0], c)


def _pack_rows(rows, width):
    rows = [jnp.pad(r.reshape(-1).astype(F32), (0, width - r.size)) for r in rows]
    pad = (-len(rows)) % 8
    rows += [jnp.zeros((width,), F32)] * pad
    return jnp.stack(rows)


def kernel(x, positions, mix_norm, mlp_norm, sb_w_in, sb_w_out, fox_w_in, fox_b_f, fox_q_gain, fox_k_gain, fox_w_out, mla_w_in, mla_q_norm, mla_kv_norm, mla_w_uq, mla_w_ukv, mla_q_gain, mla_k_gain, mla_w_out, mlp_w1, mlp_w2, loss_target, m_mix_norm, m_mlp_norm, m_sb_w_in, m_sb_w_out, m_fox_w_in, m_fox_b_f, m_fox_q_gain, m_fox_k_gain, m_fox_w_out, m_mla_w_in, m_mla_q_norm, m_mla_kv_norm, m_mla_w_uq, m_mla_w_ukv, m_mla_q_gain, m_mla_k_gain, m_mla_w_out, m_mlp_w1, m_mlp_w2, v_mix_norm, v_mlp_norm, v_sb_w_in, v_sb_w_out, v_fox_w_in, v_fox_b_f, v_fox_q_gain, v_fox_k_gain, v_fox_w_out, v_mla_w_in, v_mla_q_norm, v_mla_kv_norm, v_mla_w_uq, v_mla_w_ukv, v_mla_q_gain, v_mla_k_gain, v_mla_w_out, v_mlp_w1, v_mlp_w2):
    w_in = dict(zip(_WEIGHTS, (mix_norm, mlp_norm, sb_w_in, sb_w_out, fox_w_in, fox_b_f, fox_q_gain, fox_k_gain, fox_w_out, mla_w_in, mla_q_norm, mla_kv_norm, mla_w_uq, mla_w_ukv, mla_q_gain, mla_k_gain, mla_w_out, mlp_w1, mlp_w2)))
    m_in = dict(zip(_WEIGHTS, (m_mix_norm, m_mlp_norm, m_sb_w_in, m_sb_w_out, m_fox_w_in, m_fox_b_f, m_fox_q_gain, m_fox_k_gain, m_fox_w_out, m_mla_w_in, m_mla_q_norm, m_mla_kv_norm, m_mla_w_uq, m_mla_w_ukv, m_mla_q_gain, m_mla_k_gain, m_mla_w_out, m_mlp_w1, m_mlp_w2)))
    v_in = dict(zip(_WEIGHTS, (v_mix_norm, v_mlp_norm, v_sb_w_in, v_sb_w_out, v_fox_w_in, v_fox_b_f, v_fox_q_gain, v_fox_k_gain, v_fox_w_out, v_mla_w_in, v_mla_q_norm, v_mla_kv_norm, v_mla_w_uq, v_mla_w_ukv, v_mla_q_gain, v_mla_k_gain, v_mla_w_out, v_mlp_w1, v_mlp_w2)))
    depth, D = mix_norm.shape
    H = D // HEAD_DIM
    n_mla = mla_w_in.shape[0]
    dev = 4 * lax.axis_index("x") + 2 * lax.axis_index("y") + lax.axis_index("c")
    core = lax.axis_index("c").astype(jnp.int32).reshape(1)
    dev_arr = dev.astype(jnp.int32).reshape(1)

    chip = (2 * lax.axis_index("x") + lax.axis_index("y")).astype(jnp.int32).reshape(1)
    nq, nkv = mla_q_norm.shape[1], mla_kv_norm.shape[1]

    units, u_mix, u_mlp = [], {}, {}
    for i in range(depth):
        big = _layer_big(i)
        u_mix[i] = len(units)
        if i < SPLIT_LAYERS:
            units += [(i, ("mix",), big[:-2]), (i, ("mlp",), big[-2:])]
        else:
            units.append((i, ("mix", "mlp"), big))
        u_mlp[i] = len(units) - 1

    def split(i):
        return u_mix[i] != u_mlp[i]

    def tag_of(u):
        i, parts, _ = units[u]
        return f"l{i}" if len(parts) == 2 else f"l{i}_{parts[0]}"

    kinds_of, shapes_of, ici = [], [], []
    for i, parts, names in units:
        shards = [w_in[n][j] for n, j in names]
        kinds = [_BIG[n] for n, _ in names]
        behind = (ici[-1].token,) if ici else ()
        fulls = [_own_in_place(s, k, dev_arr, BF16, behind) for s, k in zip(shards, kinds)]
        if i % 3 == 2:
            shards.append(_pack_rows([mla_q_norm[i // 3], mla_kv_norm[i // 3]], LANE))
            kinds.append("row")
            fulls.append(_own_in_place(shards[-1], "row", dev_arr, F32, behind))
        kinds_of.append(kinds)
        shapes_of.append([s.shape for s in shards])
        ici.append(_gather_ici_start(fulls, kinds, shapes_of[-1], tag_of(len(ici)),
                                     ici[-1].token if ici else None))

    full = {n: [None] * w_in[n].shape[0] for n in _WEIGHTS}
    for n in ("mix_norm", "mlp_norm", "fox_b_f", "fox_q_gain", "fox_k_gain", "mla_q_gain", "mla_k_gain"):
        full[n] = w_in[n]
    P = {"mix_norm": mix_norm, "mlp_norm": mlp_norm, "mlp": [None] * depth,
         "sb": [None] * sb_w_in.shape[0], "fox": [None] * fox_w_in.shape[0], "mla": [None] * n_mla}

    def forward_to_sibling(u, after):
        arrived = _wait_copies(ici[u], after)
        return _gather_d2d_start(arrived, kinds_of[u], shapes_of[u], tag_of(u))

    def finish_gather(u, flight, after):
        i, parts, names = units[u]
        got = _wait_copies(flight, after)
        for (n, j), a in zip(names, got):
            full[n][j] = _stack_to_cols(a) if n == "fox_w_in" else a
        if i % 3 == 2:
            tiles = got[-1].reshape(N_DEV, 8, LANE)
            full["mla_q_norm"][i // 3] = tiles[:, 0, :nq].reshape(-1)
            full["mla_kv_norm"][i // 3] = tiles[:, 1, :nkv].reshape(-1)
        _prepare_layer(P, full, i, H, parts)

    tabs = _rope_tables(positions[0])
    d2d = forward_to_sibling(0, ici[-1].token)
    finish_gather(0, d2d, d2d.token)
    xs, saved = x[0], []
    for i in range(depth):
        nxt = {}

        def hook(x_mid, i=i, nxt=nxt):
            if split(i):
                mlp_d2d = forward_to_sibling(u_mlp[i], x_mid)
                finish_gather(u_mlp[i], mlp_d2d, mlp_d2d.token)
            if 1 <= i < depth - 1 and not split(i + 1):
                nxt["d2d"] = forward_to_sibling(u_mix[i + 1], x_mid)
                return (nxt["d2d"].token,)
            return ()

        xs, s = _layer_fwd(i, xs, P, tabs, H, hook)
        saved.append(s)
        if i + 1 < depth:
            if "d2d" in nxt:
                finish_gather(u_mix[i + 1], nxt["d2d"], xs)
            else:
                mix_d2d = forward_to_sibling(u_mix[i + 1], xs)
                finish_gather(u_mix[i + 1], mix_d2d, mix_d2d.token)
    dx, dxb, loss = _loss_head(xs, loss_target[0])
    loss = lax.psum(loss[0, 0], ("x", "y", "c"))

    def unit_meta(u):
        names = units[u][2]
        return [_BIG[n] for n, _ in names], [w_in[n].shape[1:] for n, _ in names]

    def to_sibling_start(u, Gl):
        grads = [(_cols_to_stack(Gl[n]) if n == "fox_w_in" else Gl[n]) for n, _ in units[u][2]]
        return _scatter_d2d_start(grads, *unit_meta(u), tag_of(u))

    def pair_and_send(u, flight, after, before=None):
        n = len(units[u][2])
        kinds, shapes = unit_meta(u)
        got = _wait_copies(flight, after)
        sums = [_pair_add(g, r4, k, s, core, name="pair_add_" + tag_of(u))
                for g, r4, k, s in zip(got[:n], got[n:], kinds, shapes)]
        return _scatter_ici_start(sums, shapes, tag_of(u), before)

    G, to_chips, deps, flying = [None] * depth, [None] * len(units), (), {}
    for i in reversed(range(depth)):
        def hook(dxb_mid, Gp, i=i):
            out = ()
            if i + 1 < depth:
                u = u_mix[i + 1]
                to_chips[u] = pair_and_send(u, flying.pop(u), dxb_mid)
                out = (to_chips[u].token,)
            if split(i):
                flying[u_mlp[i]] = to_sibling_start(u_mlp[i], {"mlp_w1": Gp["w1"], "mlp_w2": Gp["w2"]})
                out = out + (flying[u_mlp[i]].token,)
            return out

        def hook2(dv, i=i):
            if not split(i):
                return ()
            u = u_mlp[i]
            to_chips[u] = pair_and_send(u, flying.pop(u), dv)
            return (to_chips[u].token,)

        dx, dxb, Gp = _layer_bwd(i, dx, dxb, saved[i], P, tabs, H, deps, hook, hook2)
        G[i] = _layer_grads(i, Gp, H)
        flying[u_mix[i]] = to_sibling_start(u_mix[i], G[i])
        deps = (flying[u_mix[i]].token,)
    grad_x = dx
    to_chips[0] = pair_and_send(0, flying.pop(0), dx)

    small = [n for n in _WEIGHTS if n not in _BIG]
    rows = []
    for n in small:
        for j in range(w_in[n].shape[0]):
            rows.append(G[_layer_of(n, j)][n])
    small_flight = _allreduce_small_start(_pack_rows(rows, D) + to_chips[0].token[0, 0], chip)

    results, after = {}, [small_flight.token]
    for u in reversed(range(len(units))):
        names = units[u][2]
        got = _wait_copies(to_chips[u], after)
        for a, (n, j) in enumerate(names):
            results[n] = _adamw(w_in[n], m_in[n], v_in[n], got[a], got[len(names) + a], chip, j, results.get(n),
                                name="adamw_" + n)
        after = [results[n][0] for n, _ in names]

    (parts,) = _wait_copies(small_flight, after)
    sgr = _sum4(parts)
    g_rows, w_rows, m_rows, v_rows, at = [], [], [], [], 0
    for n in small:
        for j in range(w_in[n].shape[0]):
            width = w_in[n].shape[1]
            if n in ("mla_q_norm", "mla_kv_norm"):
                g_rows.append(lax.dynamic_slice(sgr[at], (dev * width,), (width,)))
            else:
                g_rows.append(sgr[at, :width])
            w_rows.append(w_in[n][j])
            m_rows.append(m_in[n][j])
            v_rows.append(v_in[n][j])
            at += 1
    g_pack = _pack_rows(g_rows, D)
    d_pack, nm_pack, nv_pack = _adamw_small(
        _pack_rows(w_rows, D), g_pack, _pack_rows(m_rows, D), _pack_rows(v_rows, D))
    at = 0
    for n in small:
        L, width = w_in[n].shape
        results[n] = tuple(p[at:at + L, :width] for p in (g_pack, d_pack, nm_pack, nv_pack))
        at += L

    out = [loss, grad_x[None]]
    for part in range(4):
        out += [results[n][part] for n in _WEIGHTS]
    return tuple(out)
```

```python
import functools
import math

import jax
import jax.numpy as jnp
from jax import lax
from jax.experimental import pallas as pl
from jax.experimental.pallas import tpu as pltpu

F32 = jnp.float32
BF16 = jnp.bfloat16

HEAD_DIM = 128
MLA_NOPE = 128
MLA_ROPE = 64
MLA_V = 128
MLA_QK = MLA_NOPE + MLA_ROPE
MLA_QK_PAD = 256
LANE = 128
ROPE_THETA = 10000.0
EPS = 1e-6
ADAM_LR = 0.001
ADAM_B1 = 0.9
ADAM_B2 = 0.999
ADAM_EPS = 1e-08
ADAM_WD = 0.01
ADAM_STEP = 10
N_DEV = 8
N_CHIP = 4
NEG = -1e30
VMEM_LIMIT = 56 * 1024 * 1024
ATT_BLOCK = 256
SB_BLOCK = 256
ATT_HEADS = 4
SPLIT_LAYERS = 2
SB_PARTS = 2
MESH = pl.DeviceIdType.MESH
ANY = pl.BlockSpec(memory_space=pl.ANY)


def _params(sem):
    return pltpu.CompilerParams(dimension_semantics=sem, vmem_limit_bytes=VMEM_LIMIT)


def _tile(dim, pref):
    if dim <= pref:
        return dim
    t = pref
    while dim % t:
        t -= LANE
    return t


def _dot(a, b, dims):
    return lax.dot_general(a, b, (dims, ((), ())), preferred_element_type=F32)


NN = ((1,), (0,))
NT = ((1,), (1,))
TN = ((0,), (0,))


def _mm(a, b, *, mode="nn", out_dtype=F32, res=None, act=None, z=None, name,
        tm=1024, tn=1024, tk=2048, deps=()):
    if mode == "nn":
        (M, K), (_, N) = a.shape, b.shape
    elif mode == "nt":
        (M, K), (N, _) = a.shape, b.shape
    else:
        (K, M), (_, N) = a.shape, b.shape
    tm, tn, tk = _tile(M, tm), _tile(N, tn), _tile(K, tk)
    nk = K // tk
    if mode == "tn":
        a_spec = pl.BlockSpec((tk, tm), lambda i, j, k: (k, i))
    else:
        a_spec = pl.BlockSpec((tm, tk), lambda i, j, k: (i, k))
    if mode == "nt":
        b_spec = pl.BlockSpec((tn, tk), lambda i, j, k: (j, k))
    else:
        b_spec = pl.BlockSpec((tk, tn), lambda i, j, k: (k, j))
    dims = {"nn": NN, "nt": NT, "tn": TN}[mode]
    o_spec = pl.BlockSpec((tm, tn), lambda i, j, k: (i, j))
    in_specs, args = [a_spec, b_spec], [a, b]
    if res is not None:
        in_specs.append(o_spec)
        args.append(res)
    if act == "drelu2":
        in_specs.append(o_spec)
        args.append(z)
    if act == "relu2":
        out_shape = (jax.ShapeDtypeStruct((M, N), F32), jax.ShapeDtypeStruct((M, N), BF16))
        out_specs = (o_spec, o_spec)
    else:
        out_shape = jax.ShapeDtypeStruct((M, N), out_dtype)
        out_specs = o_spec
    has_res, has_z = res is not None, act == "drelu2"
    n_out = 2 if act == "relu2" else 1
    in_specs += [ANY] * len(deps)
    args += list(deps)

    def body(*refs):
        a_ref, b_ref = refs[0], refs[1]
        idx = 2
        res_ref = z_ref = None
        if has_res:
            res_ref = refs[idx]
            idx += 1
        if has_z:
            z_ref = refs[idx]
            idx += 1
        idx += len(deps)
        outs = refs[idx:idx + n_out]

        def finish(r):
            if has_res:
                r = r + res_ref[...]
            if act == "relu2":
                outs[0][...] = r
                rr = jnp.maximum(r, 0.0)
                outs[1][...] = (rr * rr).astype(BF16)
            elif act == "drelu2":
                outs[0][...] = (r * (2.0 * jnp.maximum(z_ref[...], 0.0))).astype(out_dtype)
            else:
                outs[0][...] = r.astype(out_dtype)

        if nk == 1:
            finish(_dot(a_ref[...], b_ref[...], dims))
            return
        acc = refs[-1]
        k = pl.program_id(2)

        @pl.when(k == 0)
        def _():
            acc[...] = _dot(a_ref[...], b_ref[...], dims)

        @pl.when((k > 0) & (k < nk - 1))
        def _():
            acc[...] += _dot(a_ref[...], b_ref[...], dims)

        @pl.when(k == nk - 1)
        def _():
            finish(acc[...] + _dot(a_ref[...], b_ref[...], dims))

    return pl.pallas_call(
        body, name=name, grid=(M // tm, N // tn, nk), in_specs=in_specs, out_specs=out_specs,
        out_shape=out_shape, scratch_shapes=[pltpu.VMEM((tm, tn), F32)] if nk > 1 else [],
        compiler_params=_params(("parallel", "parallel", "arbitrary")),
    )(*args)


def _rmsnorm_fwd(x, g, *, name, tm=256):
    T, n = x.shape
    tm = _tile(T, tm)

    def body(x_ref, g_ref, o_ref):
        xf = x_ref[...]
        r = lax.rsqrt(jnp.mean(xf * xf, axis=-1, keepdims=True) + EPS)
        o_ref[...] = (xf * r * g_ref[...]).astype(BF16)

    return pl.pallas_call(
        body, name=name, grid=(T // tm,),
        in_specs=[pl.BlockSpec((tm, n), lambda i: (i, 0)), pl.BlockSpec((1, n), lambda i: (0, 0))],
        out_specs=pl.BlockSpec((tm, n), lambda i: (i, 0)),
        out_shape=jax.ShapeDtypeStruct((T, n), BF16),
        compiler_params=_params(("parallel",)),
    )(x, g)


def _rmsnorm_bwd(x, g, dy, dx_in=None, *, name, want_f32=True, tm=256):
    T, n = x.shape
    tm = _tile(T, tm)
    has_in = dx_in is not None
    row = pl.BlockSpec((tm, n), lambda i: (i, 0))
    vec = pl.BlockSpec((1, n), lambda i: (0, 0))

    def body(*refs):
        x_ref, g_ref, dy_ref = refs[:3]
        idx = 3
        in_ref = None
        if has_in:
            in_ref = refs[idx]
            idx += 1
        outs = refs[idx:]
        xf = x_ref[...]
        r = lax.rsqrt(jnp.mean(xf * xf, axis=-1, keepdims=True) + EPS)
        dyf = dy_ref[...]
        t = dyf * g_ref[...]
        dx = r * t - xf * (r * r * r * jnp.mean(t * xf, axis=-1, keepdims=True))
        if has_in:
            dx = dx + in_ref[...]
        o = 0
        if want_f32:
            outs[0][...] = dx
            o = 1
        outs[o][...] = dx.astype(BF16)
        dg_ref = outs[o + 1]

        @pl.when(pl.program_id(0) == 0)
        def _():
            dg_ref[...] = jnp.zeros_like(dg_ref)

        dg_ref[...] += jnp.sum(dyf * xf * r, axis=0, keepdims=True)

    in_specs = [row, vec, row] + ([row] if has_in else [])
    out_specs, out_shape = [], []
    if want_f32:
        out_specs.append(row)
        out_shape.append(jax.ShapeDtypeStruct((T, n), F32))
    out_specs += [row, vec]
    out_shape += [jax.ShapeDtypeStruct((T, n), BF16), jax.ShapeDtypeStruct((1, n), F32)]
    args = [x, g, dy] + ([dx_in] if has_in else [])
    return pl.pallas_call(
        body, name=name, grid=(T // tm,), in_specs=in_specs, out_specs=tuple(out_specs),
        out_shape=tuple(out_shape), compiler_params=_params(("arbitrary",)),
    )(*args)


def _loss_head(y, target, *, tm=256):
    T, n = y.shape
    tm = _tile(T, tm)
    row = pl.BlockSpec((tm, n), lambda i: (i, 0))

    def body(y_ref, t_ref, dy_ref, dyb_ref, loss_ref):
        err = y_ref[...] - t_ref[...]
        dy = err * (1.0 / n)
        dy_ref[...] = dy
        dyb_ref[...] = dy.astype(BF16)

        @pl.when(pl.program_id(0) == 0)
        def _():
            loss_ref[...] = jnp.zeros_like(loss_ref)

        part = 0.5 * jnp.sum(jnp.mean(err * err, axis=-1, keepdims=True), axis=0, keepdims=True)
        loss_ref[...] += part

    return pl.pallas_call(
        body, name="loss_head", grid=(T // tm,), in_specs=[row, row],
        out_specs=(row, row, pl.BlockSpec((1, 1), lambda i: (0, 0))),
        out_shape=(jax.ShapeDtypeStruct((T, n), F32), jax.ShapeDtypeStruct((T, n), BF16),
                   jax.ShapeDtypeStruct((1, 1), F32)),
        compiler_params=_params(("arbitrary",)),
    )(y, target)


def _split(v, n):
    parts = []
    for _ in range(n - 1):
        p = v.astype(BF16)
        parts.append(p)
        v = v - p.astype(F32)
    parts.append(v.astype(BF16))
    return parts


def _tri_left(tri, v):
    hi, mid, lo = _split(v, 3)
    return _dot(tri, hi, NN) + _dot(tri, mid, NN) + _dot(tri, lo, NN)


def _iota2(shape, dim):
    return lax.broadcasted_iota(jnp.int32, shape, dim)


def _log_sigmoid(z):
    return jnp.minimum(z, 0.0) - jnp.log1p(jnp.exp(-jnp.abs(z)))


def _log_sigmoid_abs(z):
    return jnp.minimum(z, 0.0) - jnp.log(1.0 + jnp.exp(-jnp.abs(z)))


def _head_spec(rows, width, off):
    return pl.BlockSpec((rows, width), lambda h, i: (i, off + h))


def _full_head_spec(T, width, off):
    return pl.BlockSpec((T, width), lambda h, i: (0, off + h))


def _sb_fwd(qkv, H):
    T = qkv.shape[0]
    B = _tile(T, SB_BLOCK)
    nq = T // B
    scale = 1.0 / math.sqrt(HEAD_DIM)

    HG = ATT_HEADS if H % ATT_HEADS == 0 else 1
    HD = HEAD_DIM

    def body(q_ref, k_ref, v_ref, o_ref, c_ref, tri_s):
        i = pl.program_id(1)
        tri = (_iota2((B, B), 0) > _iota2((B, B), 1)).astype(BF16)
        for p in range(SB_PARTS):
            tri_s[pl.ds(p * B, B), :] = tri
        rows = _iota2((B, B), 0)
        cols = _iota2((B, B), 1)

        def step(n, carry):
            j = i - n
            ks = pl.multiple_of(j * B, B)
            mask = (ks + cols) < (i * B + rows)
            out = []
            for g in range(HG):
                acc, run = carry[g]
                q = q_ref[:, pl.ds(g * HD, HD)]
                kb = k_ref[pl.ds(ks, B), pl.ds(g * HD, HD)]
                vb = v_ref[pl.ds(ks, B), pl.ds(g * HD, HD)]
                z = _dot(q, kb, NT) * scale
                ls = _log_sigmoid_abs(z)
                lk = jnp.where(mask, ls - z, 0.0)
                parts = jnp.concatenate(_split(lk, SB_PARTS), axis=1)
                later = _dot(parts, tri_s[...], NN)
                a = jnp.exp(jnp.where(mask, ls, NEG) + later).astype(BF16)
                acc = acc + jnp.exp(run) * _dot(a, vb, NN)
                out.append((acc, run + jnp.sum(lk, axis=1, keepdims=True)))
            return tuple(out)

        init = tuple((jnp.zeros((B, HD), F32), jnp.zeros((B, 1), F32)) for _ in range(HG))
        res = lax.fori_loop(0, i + 1, step, init)
        for g in range(HG):
            o_ref[:, pl.ds(g * HD, HD)] = res[g][0].astype(BF16)
            c_ref[g] = res[g][1]

    return pl.pallas_call(
        body, name="sb_attn_fwd", grid=(H // HG, nq),
        in_specs=[_head_spec(B, HG * HD, 0), _full_head_spec(T, HG * HD, H // HG),
                  _full_head_spec(T, HG * HD, 2 * H // HG)],
        out_specs=(_head_spec(B, HG * HD, 0), pl.BlockSpec((HG, B, 1), lambda h, i: (h, i, 0))),
        out_shape=(jax.ShapeDtypeStruct((T, H * HD), BF16), jax.ShapeDtypeStruct((H, T, 1), F32)),
        scratch_shapes=[pltpu.VMEM((SB_PARTS * B, B), BF16)],
        compiler_params=_params(("parallel", "arbitrary")),
    )(qkv, qkv, qkv)


def _sb_bwd(qkv, do, ctot, H):
    T = qkv.shape[0]
    B = _tile(T, SB_BLOCK)
    nq = T // B
    scale = 1.0 / math.sqrt(HEAD_DIM)
    HG = ATT_HEADS if H % ATT_HEADS == 0 else 1
    HD = HEAD_DIM

    def body(q_ref, k_ref, v_ref, do_ref, c_ref, dq_ref, dk_ref, dv_ref, dk_acc, dv_acc,
             tri_incl_s, tri_strict_s):
        i = pl.program_id(1)

        @pl.when(i == 0)
        def _():
            dk_acc[...] = jnp.zeros_like(dk_acc)
            dv_acc[...] = jnp.zeros_like(dv_acc)

        rows = _iota2((B, B), 0)
        cols = _iota2((B, B), 1)
        tri_incl = (rows <= cols).astype(BF16)
        tri_strict = (rows < cols).astype(BF16)
        for p in range(SB_PARTS):
            tri_incl_s[pl.ds(p * B, B), :] = tri_incl
            tri_strict_s[pl.ds(p * B, B), :] = tri_strict
        heads = [(q_ref[:, pl.ds(g * HD, HD)], do_ref[:, pl.ds(g * HD, HD)], c_ref[g]) for g in range(HG)]

        def step(j, carry):
            ks = pl.multiple_of(j * B, B)
            mask = (ks + cols) < (i * B + rows)
            out = []
            for g in range(HG):
                dq, lpre, gpre = carry[g]
                q, do_b, ctot_b = heads[g]
                kb = k_ref[pl.ds(ks, B), pl.ds(g * HD, HD)]
                vb = v_ref[pl.ds(ks, B), pl.ds(g * HD, HD)]
                z = _dot(q, kb, NT) * scale
                da = _dot(do_b, vb, NT)
                ls = _log_sigmoid_abs(z)
                lk = jnp.where(mask, ls - z, 0.0)
                lsm = jnp.where(mask, ls, NEG)
                incl = _dot(jnp.concatenate(_split(lk, SB_PARTS), axis=1), tri_incl_s[...], NN)
                a = jnp.exp(lsm + (ctot_b - lpre) - incl)
                gw = a * da
                gex = gpre + _dot(jnp.concatenate(_split(gw, SB_PARTS), axis=1), tri_strict_s[...], NN)
                dzb = ((gw * jnp.exp(lsm - z) - jnp.exp(lsm) * gex) * scale).astype(BF16)
                dq = dq + _dot(dzb, kb, NN)
                dk_acc[pl.ds(ks, B), pl.ds(g * HD, HD)] += _dot(dzb, q, TN)
                dv_acc[pl.ds(ks, B), pl.ds(g * HD, HD)] += _dot(a.astype(BF16), do_b, TN)
                out.append((dq, lpre + jnp.sum(lk, axis=1, keepdims=True),
                            gpre + jnp.sum(gw, axis=1, keepdims=True)))
            return tuple(out)

        init = tuple((jnp.zeros((B, HD), F32), jnp.zeros((B, 1), F32), jnp.zeros((B, 1), F32))
                     for _ in range(HG))
        res = lax.fori_loop(0, i + 1, step, init)
        for g in range(HG):
            dq_ref[:, pl.ds(g * HD, HD)] = res[g][0].astype(BF16)

        @pl.when(i == nq - 1)
        def _():
            dk_ref[...] = dk_acc[...].astype(BF16)
            dv_ref[...] = dv_acc[...].astype(BF16)

    W = H * HD
    return pl.pallas_call(
        body, name="sb_attn_bwd", grid=(H // HG, nq),
        in_specs=[_head_spec(B, HG * HD, 0), _full_head_spec(T, HG * HD, H // HG),
                  _full_head_spec(T, HG * HD, 2 * H // HG), _head_spec(B, HG * HD, 0),
                  pl.BlockSpec((HG, B, 1), lambda h, i: (h, i, 0))],
        out_specs=(_head_spec(B, HG * HD, 0), _full_head_spec(T, HG * HD, 0),
                   _full_head_spec(T, HG * HD, 0)),
        out_shape=tuple(jax.ShapeDtypeStruct((T, W), BF16) for _ in range(3)),
        scratch_shapes=[pltpu.VMEM((T, HG * HD), F32), pltpu.VMEM((T, HG * HD), F32),
                        pltpu.VMEM((SB_PARTS * B, B), BF16), pltpu.VMEM((SB_PARTS * B, B), BF16)],
        compiler_params=_params(("parallel", "arbitrary")),
    )(qkv, qkv, qkv, do, ctot)


def _softmax_fwd(q, k, v, cf, H, dqk, scale, *, name):
    T = q.shape[0]
    B = _tile(T, ATT_BLOCK)
    nq = T // B
    has_cf = cf is not None
    HG = ATT_HEADS if H % ATT_HEADS == 0 else 1
    HD = HEAD_DIM

    def body(*refs):
        q_ref, k_ref, v_ref = refs[:3]
        idx = 3
        if has_cf:
            cfc_ref, cfr_ref = refs[3], refs[4]
            idx = 5
        o_ref, of_ref, lse_ref = refs[idx:idx + 3]
        i = pl.program_id(1)
        rows = _iota2((B, B), 0)
        cols = _iota2((B, B), 1)

        def step(j, carry):
            ks = pl.multiple_of(j * B, B)
            mask = (ks + cols) <= (i * B + rows)
            out = []
            for g in range(HG):
                m, l, acc = carry[g]
                qb = q_ref[:, pl.ds(g * dqk, dqk)]
                kb = k_ref[pl.ds(ks, B), pl.ds(g * dqk, dqk)]
                vb = v_ref[pl.ds(ks, B), pl.ds(g * HD, HD)]
                s = _dot(qb, kb, NT) * scale
                if has_cf:
                    s = s + (cfc_ref[g] - cfr_ref[g, :, pl.ds(ks, B)])
                s = jnp.where(mask, s, NEG)
                m_new = jnp.maximum(m, jnp.max(s, axis=1, keepdims=True))
                alpha = jnp.exp(m - m_new)
                p = jnp.exp(s - m_new)
                l = alpha * l + jnp.sum(p, axis=1, keepdims=True)
                acc = alpha * acc + _dot(p.astype(BF16), vb, NN)
                out.append((m_new, l, acc))
            return tuple(out)

        init = tuple((jnp.full((B, 1), NEG, F32), jnp.zeros((B, 1), F32), jnp.zeros((B, HD), F32))
                     for _ in range(HG))
        res = lax.fori_loop(0, i + 1, step, init)
        for g in range(HG):
            m, l, acc = res[g]
            o = acc / l
            o_ref[:, pl.ds(g * HD, HD)] = o.astype(BF16)
            of_ref[:, pl.ds(g * HD, HD)] = o
            lse_ref[g] = m + jnp.log(l)

    stat = pl.BlockSpec((HG, B, 1), lambda h, i: (h, i, 0))
    in_specs = [_head_spec(B, HG * dqk, 0), _full_head_spec(T, HG * dqk, 0), _full_head_spec(T, HG * HD, 0)]
    args = [q, k, v]
    if has_cf:
        in_specs += [stat, pl.BlockSpec((HG, 1, T), lambda h, i: (h, 0, 0))]
        args += list(cf)
    W = H * HD
    return pl.pallas_call(
        body, name=name, grid=(H // HG, nq), in_specs=in_specs,
        out_specs=(_head_spec(B, HG * HD, 0), _head_spec(B, HG * HD, 0), stat),
        out_shape=(jax.ShapeDtypeStruct((T, W), BF16), jax.ShapeDtypeStruct((T, W), F32),
                   jax.ShapeDtypeStruct((H, T, 1), F32)),
        compiler_params=_params(("parallel", "arbitrary")),
    )(*args)


def _softmax_bwd(q, k, v, cf, do, o, lse, H, dqk, scale, *, name):
    T = q.shape[0]
    B = _tile(T, ATT_BLOCK)
    nq = T // B
    has_cf = cf is not None
    HG = ATT_HEADS if H % ATT_HEADS == 0 else 1
    HD = HEAD_DIM

    def body(*refs):
        q_ref, k_ref, v_ref, do_ref, o_ref, lse_ref = refs[:6]
        idx = 6
        if has_cf:
            cfc_ref, cfr_ref = refs[6], refs[7]
            idx = 8
        dq_ref, dk_ref, dv_ref = refs[idx:idx + 3]
        idx += 3
        if has_cf:
            dcc_ref, dcr_ref = refs[idx], refs[idx + 1]
        i = pl.program_id(1)

        @pl.when(i == 0)
        def _():
            dk_ref[...] = jnp.zeros_like(dk_ref)
            dv_ref[...] = jnp.zeros_like(dv_ref)
            if has_cf:
                dcr_ref[...] = jnp.zeros_like(dcr_ref)

        rows = _iota2((B, B), 0)
        cols = _iota2((B, B), 1)
        heads = []
        for g in range(HG):
            do_b = do_ref[:, pl.ds(g * HD, HD)]
            delta = jnp.sum(do_b.astype(F32) * o_ref[:, pl.ds(g * HD, HD)], axis=1, keepdims=True)
            heads.append((q_ref[:, pl.ds(g * dqk, dqk)], do_b, lse_ref[g], delta))

        def step(j, carry):
            ks = pl.multiple_of(j * B, B)
            mask = (ks + cols) <= (i * B + rows)
            out = []
            for g in range(HG):
                dq, rs = carry[g]
                qb, do_b, lse_b, delta = heads[g]
                kb = k_ref[pl.ds(ks, B), pl.ds(g * dqk, dqk)]
                vb = v_ref[pl.ds(ks, B), pl.ds(g * HD, HD)]
                s = _dot(qb, kb, NT) * scale
                if has_cf:
                    s = s + (cfc_ref[g] - cfr_ref[g, :, pl.ds(ks, B)])
                p = jnp.where(mask, jnp.exp(s - lse_b), 0.0)
                dp = _dot(do_b, vb, NT)
                ds = p * (dp - delta)
                dsb = (ds * scale).astype(BF16)
                dq = dq + _dot(dsb, kb, NN)
                dk_ref[pl.ds(ks, B), pl.ds(g * dqk, dqk)] += _dot(dsb, qb, TN)
                dv_ref[pl.ds(ks, B), pl.ds(g * HD, HD)] += _dot(p.astype(BF16), do_b, TN)
                if has_cf:
                    rs = rs + jnp.sum(ds, axis=1, keepdims=True)
                    dcr_ref[g, :, pl.ds(ks, B)] -= jnp.sum(ds, axis=0, keepdims=True)
                out.append((dq, rs))
            return tuple(out)

        init = tuple((jnp.zeros((B, dqk), F32), jnp.zeros((B, 1), F32)) for _ in range(HG))
        res = lax.fori_loop(0, i + 1, step, init)
        for g in range(HG):
            dq_ref[:, pl.ds(g * dqk, dqk)] = res[g][0]
            if has_cf:
                dcc_ref[g] = res[g][1]

    stat = pl.BlockSpec((HG, B, 1), lambda h, i: (h, i, 0))
    rowstat = pl.BlockSpec((HG, 1, T), lambda h, i: (h, 0, 0))
    in_specs = [_head_spec(B, HG * dqk, 0), _full_head_spec(T, HG * dqk, 0), _full_head_spec(T, HG * HD, 0),
                _head_spec(B, HG * HD, 0), _head_spec(B, HG * HD, 0), stat]
    args = [q, k, v, do, o, lse]
    out_specs = [_head_spec(B, HG * dqk, 0), _full_head_spec(T, HG * dqk, 0), _full_head_spec(T, HG * HD, 0)]
    out_shape = [jax.ShapeDtypeStruct((T, H * dqk), F32), jax.ShapeDtypeStruct((T, H * dqk), F32),
                 jax.ShapeDtypeStruct((T, H * HD), F32)]
    if has_cf:
        in_specs += [stat, rowstat]
        args += list(cf)
        out_specs += [stat, rowstat]
        out_shape += [jax.ShapeDtypeStruct((H, T, 1), F32), jax.ShapeDtypeStruct((H, 1, T), F32)]
    return pl.pallas_call(
        body, name=name, grid=(H // HG, nq), in_specs=in_specs, out_specs=tuple(out_specs),
        out_shape=tuple(out_shape), compiler_params=_params(("parallel", "arbitrary")),
    )(*args)


def _headnorm(x, g):
    r = lax.rsqrt(jnp.mean(x * x, axis=-1, keepdims=True) + EPS)
    return x * r * g, r


def _headnorm_bwd(x, g, dy, n):
    r = lax.rsqrt(jnp.sum(x * x, axis=-1, keepdims=True) * (1.0 / n) + EPS)
    t = dy * g
    dx = r * t - x * (r * r * r * (jnp.sum(t * x, axis=-1, keepdims=True) * (1.0 / n)))
    dg = jnp.sum(dy * x * r, axis=0, keepdims=True)
    return dx, dg


def _fox_prep_fwd(qkv, gq, gk, H, *, tm=512):
    T = qkv.shape[0]
    tm = _tile(T, tm)

    def body(q_ref, k_ref, v_ref, gq_ref, gk_ref, qn_ref, kn_ref, vb_ref):
        qn_ref[...] = _headnorm(q_ref[...], gq_ref[...])[0].astype(BF16)
        kn_ref[...] = _headnorm(k_ref[...], gk_ref[...])[0].astype(BF16)
        vb_ref[...] = v_ref[...].astype(BF16)

    def blk(off):
        return pl.BlockSpec((tm, HEAD_DIM), lambda i, h: (i, off + h))

    vec = pl.BlockSpec((1, HEAD_DIM), lambda i, h: (0, 0))
    W = H * HEAD_DIM
    return pl.pallas_call(
        body, name="fox_prep_fwd", grid=(T // tm, H),
        in_specs=[blk(0), blk(H), blk(2 * H), vec, vec], out_specs=(blk(0), blk(0), blk(0)),
        out_shape=tuple(jax.ShapeDtypeStruct((T, W), BF16) for _ in range(3)),
        compiler_params=_params(("parallel", "parallel")),
    )(qkv, qkv, qkv, gq, gk)


def _fox_prep_bwd(qkv, gq, gk, dqn, dkn, dv, H, *, tm=512):
    T = qkv.shape[0]
    tm = _tile(T, tm)

    def body(q_ref, k_ref, gq_ref, gk_ref, dqn_ref, dkn_ref, dv_ref,
             dq_ref, dk_ref, dvb_ref, dgq_ref, dgk_ref):
        @pl.when((pl.program_id(0) == 0) & (pl.program_id(1) == 0))
        def _():
            dgq_ref[...] = jnp.zeros_like(dgq_ref)
            dgk_ref[...] = jnp.zeros_like(dgk_ref)

        dq, dgq = _headnorm_bwd(q_ref[...], gq_ref[...], dqn_ref[...], HEAD_DIM)
        dk, dgk = _headnorm_bwd(k_ref[...], gk_ref[...], dkn_ref[...], HEAD_DIM)
        dq_ref[...] = dq.astype(BF16)
        dk_ref[...] = dk.astype(BF16)
        dvb_ref[...] = dv_ref[...].astype(BF16)
        dgq_ref[...] += dgq
        dgk_ref[...] += dgk

    def blk(off):
        return pl.BlockSpec((tm, HEAD_DIM), lambda i, h: (i, off + h))

    vec = pl.BlockSpec((1, HEAD_DIM), lambda i, h: (0, 0))
    W = H * HEAD_DIM
    return pl.pallas_call(
        body, name="fox_prep_bwd", grid=(T // tm, H),
        in_specs=[blk(0), blk(H), vec, vec, blk(0), blk(0), blk(0)],
        out_specs=(blk(0), blk(0), blk(0), vec, vec),
        out_shape=tuple(jax.ShapeDtypeStruct((T, W), BF16) for _ in range(3))
        + (jax.ShapeDtypeStruct((1, HEAD_DIM), F32), jax.ShapeDtypeStruct((1, HEAD_DIM), F32)),
        compiler_params=_params(("arbitrary", "arbitrary")),
    )(qkv, qkv, gq, gk, dqn, dkn, dv)


def _fox_gate_fwd(flog, bf, *, tm=256):
    T = flog.shape[0]
    tm = _tile(T, tm)

    def body(f_ref, b_ref, cf_ref, carry):
        @pl.when(pl.program_id(0) == 0)
        def _():
            carry[...] = jnp.zeros_like(carry)

        lf = _log_sigmoid(f_ref[...] + b_ref[...])
        tri = (_iota2((tm, tm), 1) <= _iota2((tm, tm), 0)).astype(BF16)
        cf_ref[...] = carry[...] + _tri_left(tri, lf)
        carry[...] += jnp.sum(lf, axis=0, keepdims=True)

    return pl.pallas_call(
        body, name="fox_gate_fwd", grid=(T // tm,),
        in_specs=[pl.BlockSpec((tm, LANE), lambda i: (i, 0)), pl.BlockSpec((1, LANE), lambda i: (0, 0))],
        out_specs=pl.BlockSpec((tm, LANE), lambda i: (i, 0)),
        out_shape=jax.ShapeDtypeStruct((T, LANE), F32),
        scratch_shapes=[pltpu.VMEM((1, LANE), F32)],
        compiler_params=_params(("arbitrary",)),
    )(flog, bf)


def _fox_gate_bwd(flog, bf, dcf, *, tm=256):
    T = flog.shape[0]
    tm = _tile(T, tm)
    nt = T // tm

    def body(f_ref, b_ref, dcf_ref, df_ref, db_ref, carry):
        @pl.when(pl.program_id(0) == 0)
        def _():
            carry[...] = jnp.zeros_like(carry)
            db_ref[...] = jnp.zeros_like(db_ref)

        d = dcf_ref[...]
        tri = (_iota2((tm, tm), 1) >= _iota2((tm, tm), 0)).astype(BF16)
        dlf = carry[...] + _tri_left(tri, d)
        carry[...] += jnp.sum(d, axis=0, keepdims=True)
        xg = f_ref[...] + b_ref[...]
        e = jnp.exp(-jnp.abs(xg))
        sig_neg = jnp.where(xg >= 0.0, e, 1.0) / (1.0 + e)
        df = dlf * sig_neg
        df_ref[...] = df.astype(BF16)
        db_ref[...] += jnp.sum(df, axis=0, keepdims=True)

    rev = pl.BlockSpec((tm, LANE), lambda i: (nt - 1 - i, 0))
    vec = pl.BlockSpec((1, LANE), lambda i: (0, 0))
    return pl.pallas_call(
        body, name="fox_gate_bwd", grid=(nt,), in_specs=[rev, vec, rev], out_specs=(rev, vec),
        out_shape=(jax.ShapeDtypeStruct((T, LANE), BF16), jax.ShapeDtypeStruct((1, LANE), F32)),
        scratch_shapes=[pltpu.VMEM((1, LANE), F32)],
        compiler_params=_params(("arbitrary",)),
    )(flog, bf, dcf)


def _rope_tables(positions):
    half = MLA_ROPE // 2
    inv_freq = ROPE_THETA ** (-jnp.arange(0, half, dtype=F32) * 2.0 / MLA_ROPE)
    ang = positions.astype(F32)[:, None] * inv_freq
    cos, sin = jnp.cos(ang), jnp.sin(ang)
    zero = jnp.zeros_like(cos)
    pad = jnp.zeros((positions.shape[0], LANE - MLA_ROPE), F32)
    cos_t = jnp.concatenate([cos, cos, pad], axis=1)
    sin_up = jnp.concatenate([zero, sin, pad], axis=1)
    sin_dn = jnp.concatenate([-sin, zero, pad], axis=1)
    return cos_t, sin_up, sin_dn


def _rope(x, cos_t, sin_up, sin_dn):
    half = MLA_ROPE // 2
    return x * cos_t + pltpu.roll(x, half, 1) * sin_up + pltpu.roll(x, LANE - half, 1) * sin_dn


def _rope_t(d, cos_t, sin_up, sin_dn):
    half = MLA_ROPE // 2
    return d * cos_t + pltpu.roll(d * sin_up, LANE - half, 1) + pltpu.roll(d * sin_dn, half, 1)


def _norm192(xcat, g):
    r = lax.rsqrt(jnp.sum(xcat * xcat, axis=-1, keepdims=True) * (1.0 / MLA_QK) + EPS)
    return xcat * r * g


def _mla_prep_fwd(qfull, kv, kr, tabs, gq, gk, H, *, tm=512):
    T = qfull.shape[0]
    tm = _tile(T, tm)

    def body(q_ref, kv_ref, kr_ref, c_ref, su_ref, sd_ref, gq_ref, gk_ref, qf_ref, kf_ref, v_ref):
        tabs_b = (c_ref[...], su_ref[...], sd_ref[...])
        qb = q_ref[...]
        qcat = jnp.concatenate([qb[:, :MLA_NOPE], _rope(qb[:, MLA_NOPE:], *tabs_b)], axis=1)
        qf_ref[...] = _norm192(qcat, gq_ref[...]).astype(BF16)
        kvb = kv_ref[...]
        kcat = jnp.concatenate([kvb[:, :MLA_NOPE], _rope(kr_ref[...], *tabs_b)], axis=1)
        kf_ref[...] = _norm192(kcat, gk_ref[...]).astype(BF16)
        v_ref[...] = kvb[:, MLA_NOPE:].astype(BF16)

    head = pl.BlockSpec((tm, MLA_QK_PAD), lambda i, h: (i, h))
    tok = pl.BlockSpec((tm, LANE), lambda i, h: (i, 0))
    vec = pl.BlockSpec((1, MLA_QK_PAD), lambda i, h: (0, 0))
    return pl.pallas_call(
        body, name="mla_prep_fwd", grid=(T // tm, H),
        in_specs=[head, head, tok, tok, tok, tok, vec, vec],
        out_specs=(head, head, pl.BlockSpec((tm, MLA_V), lambda i, h: (i, h))),
        out_shape=(jax.ShapeDtypeStruct((T, H * MLA_QK_PAD), BF16),
                   jax.ShapeDtypeStruct((T, H * MLA_QK_PAD), BF16),
                   jax.ShapeDtypeStruct((T, H * MLA_V), BF16)),
        compiler_params=_params(("parallel", "parallel")),
    )(qfull, kv, kr, *tabs, gq, gk)


def _mla_prep_bwd(qfull, kv, kr, tabs, gq, gk, dqf, dkf, dv, H, *, tm=512):
    T = qfull.shape[0]
    tm = _tile(T, tm)

    def body(q_ref, kv_ref, kr_ref, c_ref, su_ref, sd_ref, gq_ref, gk_ref, dqf_ref, dkf_ref, dv_ref,
             dq_ref, dkv_ref, dkr_ref, dgq_ref, dgk_ref, kr_acc):
        h = pl.program_id(1)

        @pl.when((pl.program_id(0) == 0) & (h == 0))
        def _():
            dgq_ref[...] = jnp.zeros_like(dgq_ref)
            dgk_ref[...] = jnp.zeros_like(dgk_ref)

        @pl.when(h == 0)
        def _():
            kr_acc[...] = jnp.zeros_like(kr_acc)

        tabs_b = (c_ref[...], su_ref[...], sd_ref[...])
        qb = q_ref[...]
        qcat = jnp.concatenate([qb[:, :MLA_NOPE], _rope(qb[:, MLA_NOPE:], *tabs_b)], axis=1)
        dqcat, dgq = _headnorm_bwd(qcat, gq_ref[...], dqf_ref[...], MLA_QK)
        dq_ref[...] = jnp.concatenate(
            [dqcat[:, :MLA_NOPE], _rope_t(dqcat[:, MLA_NOPE:], *tabs_b)], axis=1).astype(BF16)
        dgq_ref[...] += dgq
        kvb = kv_ref[...]
        kcat = jnp.concatenate([kvb[:, :MLA_NOPE], _rope(kr_ref[...], *tabs_b)], axis=1)
        dkcat, dgk = _headnorm_bwd(kcat, gk_ref[...], dkf_ref[...], MLA_QK)
        dkv_ref[...] = jnp.concatenate([dkcat[:, :MLA_NOPE], dv_ref[...]], axis=1).astype(BF16)
        dgk_ref[...] += dgk
        kr_acc[...] += dkcat[:, MLA_NOPE:]

        @pl.when(h == H - 1)
        def _():
            dkr_ref[...] = _rope_t(kr_acc[...], *tabs_b).astype(BF16)

    head = pl.BlockSpec((tm, MLA_QK_PAD), lambda i, h: (i, h))
    tok = pl.BlockSpec((tm, LANE), lambda i, h: (i, 0))
    vec = pl.BlockSpec((1, MLA_QK_PAD), lambda i, h: (0, 0))
    return pl.pallas_call(
        body, name="mla_prep_bwd", grid=(T // tm, H),
        in_specs=[head, head, tok, tok, tok, tok, vec, vec, head, head,
                  pl.BlockSpec((tm, MLA_V), lambda i, h: (i, h))],
        out_specs=(head, head, tok, vec, vec),
        out_shape=(jax.ShapeDtypeStruct((T, H * MLA_QK_PAD), BF16),
                   jax.ShapeDtypeStruct((T, H * MLA_QK_PAD), BF16),
                   jax.ShapeDtypeStruct((T, LANE), BF16),
                   jax.ShapeDtypeStruct((1, MLA_QK_PAD), F32), jax.ShapeDtypeStruct((1, MLA_QK_PAD), F32)),
        scratch_shapes=[pltpu.VMEM((tm, LANE), F32)],
        compiler_params=_params(("arbitrary", "arbitrary")),
    )(qfull, kv, kr, *tabs, gq, gk, dqf, dkf, dv)


def _cf_layouts(cf, H):
    cfh = cf[:, :H].T
    return cfh[:, :, None], cfh[:, None, :]


def _layer_fwd(i, x, P, tabs, H, hook=None):
    kind, j = i % 3, i // 3
    s = {"x_in": x}
    h = _rmsnorm_fwd(x, P["mix_norm"][i:i + 1], name="mix_norm_fwd")
    s["h"] = h
    if kind == 0:
        w = P["sb"][j]
        qkv = _mm(h, w["w_in"], out_dtype=BF16, name="sb_qkv")
        o, ctot = _sb_fwd(qkv, H)
        s.update(qkv=qkv, o=o, ctot=ctot)
    elif kind == 1:
        w = P["fox"][j]
        qkv = _mm(h, w["w_qkv"], name="fox_qkv")
        flog = _mm(h, w["w_f"], name="fox_gate_proj")
        qn, kn, vb = _fox_prep_fwd(qkv, w["gq"], w["gk"], H)
        cf = _cf_layouts(_fox_gate_fwd(flog, w["bf"]), H)
        o, of, lse = _softmax_fwd(qn, kn, vb, cf, H, HEAD_DIM, 1.0 / math.sqrt(HEAD_DIM), name="fox_attn_fwd")
        s.update(qkv=qkv, flog=flog, qn=qn, kn=kn, vb=vb, cf=cf, o=o, of=of, lse=lse)
    else:
        w = P["mla"][j]
        dq = _mm(h, w["w_dq"], name="mla_down_q")
        dkv = _mm(h, w["w_dkv"], name="mla_down_kv")
        kr = _mm(h, w["w_dr"], name="mla_down_rope")
        cq = _rmsnorm_fwd(dq, w["q_norm"], name="mla_q_norm_fwd")
        ckv = _rmsnorm_fwd(dkv, w["kv_norm"], name="mla_kv_norm_fwd")
        qfull = _mm(cq, w["w_uq"], name="mla_up_q")
        kv = _mm(ckv, w["w_ukv"], name="mla_up_kv")
        qf, kf, v = _mla_prep_fwd(qfull, kv, kr, tabs, w["gq"], w["gk"], H)
        o, of, lse = _softmax_fwd(qf, kf, v, None, H, MLA_QK_PAD, 1.0 / math.sqrt(MLA_QK), name="mla_attn_fwd")
        s.update(dq=dq, dkv=dkv, kr=kr, cq=cq, ckv=ckv, qfull=qfull, kv=kv, qf=qf, kf=kf, v=v,
                 o=o, of=of, lse=lse)
    x = _mm(s["o"], w["w_out"], res=x, name="mixer_out")
    s["x_mid"] = x
    deps = hook(x) if hook is not None else ()
    h2 = _rmsnorm_fwd(x, P["mlp_norm"][i:i + 1], name="mlp_norm_fwd")
    z, u = _mm(h2, P["mlp"][i]["w1"], act="relu2", name="mlp_up", deps=deps)
    x = _mm(u, P["mlp"][i]["w2"], res=x, name="mlp_down")
    s.update(h2=h2, z=z, u=u)
    return x, s


def _layer_bwd(i, dx, dxb, s, P, tabs, H, deps=(), hook=None, hook2=None):
    kind, j = i % 3, i // 3
    G = {}
    wm = P["mlp"][i]
    dz = _mm(dxb, wm["w2"], mode="nt", act="drelu2", z=s["z"], out_dtype=BF16, name="mlp_down_bwd",
             deps=deps)
    G["w2"] = _mm(s["u"], dxb, mode="tn", out_dtype=BF16, name="mlp_w2_grad")
    G["w1"] = _mm(s["h2"], dz, mode="tn", out_dtype=BF16, name="mlp_w1_grad")
    dh2 = _mm(dz, wm["w1"], mode="nt", name="mlp_up_bwd")
    dx, dxb, G["mlp_norm"] = _rmsnorm_bwd(s["x_mid"], P["mlp_norm"][i:i + 1], dh2, dx, name="mlp_norm_bwd")
    h = s["h"]
    w = P[("sb", "fox", "mla")[kind]][j]
    do = _mm(dxb, w["w_out"], mode="nt", out_dtype=BF16, name="mixer_out_bwd",
             deps=hook(dxb, G) if hook is not None else ())
    G["w_out"] = _mm(s["o"], dxb, mode="tn", out_dtype=BF16, name="mixer_out_grad")
    if kind == 0:
        dq, dk, dv = _sb_bwd(s["qkv"], do, s["ctot"], H)
        dproj = jnp.concatenate([dq, dk, dv], axis=1)
        G["w_in"] = _mm(h, dproj, mode="tn", out_dtype=BF16, name="sb_qkv_grad",
                        deps=hook2(dv) if hook2 is not None else ())
        dh = _mm(dproj, w["w_in"], mode="nt", name="sb_qkv_bwd")
    elif kind == 1:
        dqn, dkn, dv, dcc, dcr = _softmax_bwd(
            s["qn"], s["kn"], s["vb"], s["cf"], do, s["of"], s["lse"], H, HEAD_DIM,
            1.0 / math.sqrt(HEAD_DIM), name="fox_attn_bwd")
        dq, dk, dvb, G["gq"], G["gk"] = _fox_prep_bwd(s["qkv"], w["gq"], w["gk"], dqn, dkn, dv, H)
        dcf = (dcc[:, :, 0] + dcr[:, 0, :]).T
        dcf = jnp.pad(dcf, ((0, 0), (0, LANE - H)))
        dflog, G["bf"] = _fox_gate_bwd(s["flog"], w["bf"], dcf)
        dproj = jnp.concatenate([dq, dk, dvb], axis=1)
        G["w_qkv"] = _mm(h, dproj, mode="tn", out_dtype=BF16, name="fox_qkv_grad",
                         deps=hook2(dv) if hook2 is not None else ())
        G["w_f"] = _mm(h, dflog, mode="tn", out_dtype=BF16, name="fox_gate_grad")
        dh = _mm(dproj, w["w_qkv"], mode="nt", name="fox_qkv_bwd")
        dh = _mm(dflog, w["w_f"], mode="nt", res=dh, name="fox_gate_bwd_proj")
    else:
        dqf, dkf, dv = _softmax_bwd(
            s["qf"], s["kf"], s["v"], None, do, s["of"], s["lse"], H, MLA_QK_PAD,
            1.0 / math.sqrt(MLA_QK), name="mla_attn_bwd")
        dqfull, dkv, dkr, G["gq"], G["gk"] = _mla_prep_bwd(
            s["qfull"], s["kv"], s["kr"], tabs, w["gq"], w["gk"], dqf, dkf, dv, H)
        G["w_uq"] = _mm(s["cq"], dqfull, mode="tn", out_dtype=BF16, name="mla_up_q_grad",
                        deps=hook2(dv) if hook2 is not None else ())
        G["w_ukv"] = _mm(s["ckv"], dkv, mode="tn", out_dtype=BF16, name="mla_up_kv_grad")
        dcq = _mm(dqfull, w["w_uq"], mode="nt", name="mla_up_q_bwd")
        dckv = _mm(dkv, w["w_ukv"], mode="nt", name="mla_up_kv_bwd")
        ddq, G["q_norm"] = _rmsnorm_bwd(s["dq"], w["q_norm"], dcq, want_f32=False, name="mla_q_norm_bwd")
        ddkv, G["kv_norm"] = _rmsnorm_bwd(s["dkv"], w["kv_norm"], dckv, want_f32=False, name="mla_kv_norm_bwd")
        G["w_dq"] = _mm(h, ddq, mode="tn", out_dtype=BF16, name="mla_down_q_grad")
        G["w_dkv"] = _mm(h, ddkv, mode="tn", out_dtype=BF16, name="mla_down_kv_grad")
        G["w_dr"] = _mm(h, dkr, mode="tn", out_dtype=BF16, name="mla_down_rope_grad")
        dh = _mm(ddq, w["w_dq"], mode="nt", name="mla_down_q_bwd")
        dh = _mm(ddkv, w["w_dkv"], mode="nt", res=dh, name="mla_down_kv_bwd")
        dh = _mm(dkr, w["w_dr"], mode="nt", res=dh, name="mla_down_rope_bwd")
    dx, dxb, G["mix_norm"] = _rmsnorm_bwd(s["x_in"], P["mix_norm"][i:i + 1], dh, dx, name="mix_norm_bwd")
    return dx, dxb, G


def _local_step(x, positions, loss_target, P, depth, H):
    tabs = _rope_tables(positions)
    saved = []
    for i in range(depth):
        x, s = _layer_fwd(i, x, P, tabs, H)
        saved.append(s)
    dx, dxb, loss = _loss_head(x, loss_target)
    grads = [None] * depth
    for i in reversed(range(depth)):
        dx, dxb, grads[i] = _layer_bwd(i, dx, dxb, saved[i], P, tabs, H)
    return loss, dx, grads


def _pad_cols(a, n):
    return jnp.pad(a, ((0, 0), (0, n - a.shape[1])))


def _prepare_layer(P, full, i, H, parts=("mix", "mlp")):
    W = H * HEAD_DIM
    kind, j = i % 3, i // 3
    if "mlp" in parts:
        P["mlp"][i] = {"w1": full["mlp_w1"][i], "w2": full["mlp_w2"][i]}
    if "mix" not in parts:
        return
    if kind == 0:
        P["sb"][j] = {"w_in": full["sb_w_in"][j], "w_out": full["sb_w_out"][j]}
    elif kind == 1:
        w = full["fox_w_in"][j]
        P["fox"][j] = {
            "w_qkv": w[:, :3 * W], "w_f": _pad_cols(w[:, 3 * W:], LANE),
            "bf": _pad_cols(full["fox_b_f"][j:j + 1], LANE),
            "gq": full["fox_q_gain"][j:j + 1], "gk": full["fox_k_gain"][j:j + 1],
            "w_out": full["fox_w_out"][j]}
    else:
        w = full["mla_w_in"][j]
        q_norm, kv_norm = full["mla_q_norm"][j], full["mla_kv_norm"][j]
        rq, rkv = q_norm.shape[0], kv_norm.shape[0]
        w_uq = full["mla_w_uq"][j].reshape(rq, H, MLA_QK)
        w_uq = jnp.pad(w_uq, ((0, 0), (0, 0), (0, MLA_QK_PAD - MLA_QK))).reshape(rq, H * MLA_QK_PAD)
        P["mla"][j] = {
            "w_dq": w[:, :rq], "w_dkv": w[:, rq:rq + rkv], "w_dr": _pad_cols(w[:, rq + rkv:], LANE),
            "q_norm": q_norm[None], "kv_norm": kv_norm[None],
            "w_uq": w_uq, "w_ukv": full["mla_w_ukv"][j],
            "gq": _pad_cols(full["mla_q_gain"][j:j + 1], MLA_QK_PAD),
            "gk": _pad_cols(full["mla_k_gain"][j:j + 1], MLA_QK_PAD),
            "w_out": full["mla_w_out"][j]}


def _prepare(full, H):
    depth = len(full["mlp_w1"])
    P = {"mix_norm": full["mix_norm"], "mlp_norm": full["mlp_norm"], "mlp": [None] * depth,
         "sb": [None] * len(full["sb_w_in"]), "fox": [None] * len(full["fox_w_in"]),
         "mla": [None] * len(full["mla_w_in"])}
    for i in range(depth):
        _prepare_layer(P, full, i, H)
    return P


def _layer_grads(i, G, H):
    kind = i % 3
    out = {"mix_norm": G["mix_norm"], "mlp_norm": G["mlp_norm"], "mlp_w1": G["w1"], "mlp_w2": G["w2"]}
    if kind == 0:
        out.update(sb_w_in=G["w_in"], sb_w_out=G["w_out"])
    elif kind == 1:
        out.update(fox_w_in=jnp.concatenate([G["w_qkv"], G["w_f"][:, :H]], axis=1), fox_b_f=G["bf"][:, :H],
                   fox_q_gain=G["gq"], fox_k_gain=G["gk"], fox_w_out=G["w_out"])
    else:
        rq = G["w_uq"].shape[0]
        out.update(
            mla_w_in=jnp.concatenate([G["w_dq"], G["w_dkv"], G["w_dr"][:, :MLA_ROPE]], axis=1),
            mla_q_norm=G["q_norm"], mla_kv_norm=G["kv_norm"],
            mla_w_uq=G["w_uq"].reshape(rq, H, MLA_QK_PAD)[:, :, :MLA_QK].reshape(rq, H * MLA_QK),
            mla_w_ukv=G["w_ukv"], mla_q_gain=G["gq"][:, :MLA_QK], mla_k_gain=G["gk"][:, :MLA_QK],
            mla_w_out=G["w_out"])
    return out


def _layer_of(name, j):
    if name.startswith("sb_"):
        return 3 * j
    if name.startswith("fox_"):
        return 3 * j + 1
    if name.startswith("mla_"):
        return 3 * j + 2
    return j


def _unprepare(grads, H):
    out = {}
    for i, G in enumerate(grads):
        for n, g in _layer_grads(i, G, H).items():
            out.setdefault(n, []).append(g)
    return out


class _Place:
    def __init__(self, x, y, c):
        self.x, self.y, self.c = x, y, c
        self.dev = 4 * x + 2 * y + c
        self.chip = 2 * x + y
        self.id = (x, y, c)


def _peers(me, kind):
    if kind == "ici":
        return [_Place(1 - me.x, me.y, me.c), _Place(me.x, 1 - me.y, me.c), _Place(1 - me.x, 1 - me.y, me.c)]
    return [_Place(me.x, me.y, 1 - me.c)]


def _exchange(name, kind, operands, out_shapes, aliases, n_remote, n_local, plan):
    n_in, n_out = len(operands), len(out_shapes)

    def body(*refs):
        in_refs, out_refs = refs[:n_in], refs[n_in:n_in + n_out]
        send_sems, recv_sems, local_sems = refs[n_in + n_out:]
        me = _Place(lax.axis_index("x"), lax.axis_index("y"), lax.axis_index("c"))
        peers = _peers(me, kind)
        remote, local = plan(me, peers, in_refs, out_refs)
        assert len(remote) == n_remote and len(local) == n_local
        sends = []
        for n, (src, dst, k, _) in enumerate(remote):
            cp = pltpu.make_async_remote_copy(
                src_ref=src, dst_ref=dst, send_sem=send_sems.at[n], recv_sem=recv_sems.at[n],
                device_id=peers[k].id, device_id_type=MESH)
            cp.start()
            sends.append(cp)
        own = []
        for n, (src, dst) in enumerate(local):
            cp = pltpu.make_async_copy(src, dst, local_sems.at[n])
            cp.start()
            own.append(cp)
        for n, (src, _, k, landing) in enumerate(remote):
            pltpu.make_async_remote_copy(
                src_ref=src, dst_ref=landing, send_sem=send_sems.at[n], recv_sem=recv_sems.at[n],
                device_id=peers[k].id, device_id_type=MESH).wait_recv()
        for cp in sends:
            cp.wait_send()
        for cp in own:
            cp.wait()

    outs = pl.pallas_call(
        body, name=name, in_specs=[ANY] * n_in, out_specs=tuple([ANY] * n_out),
        out_shape=tuple(out_shapes), input_output_aliases=aliases,
        scratch_shapes=[pltpu.SemaphoreType.DMA((n_remote,)), pltpu.SemaphoreType.DMA((n_remote,)),
                        pltpu.SemaphoreType.DMA((max(n_local, 1),))],
    )(*operands)
    return list(outs)


def _window(ref, kind, d, shard_shape):
    r, c = shard_shape
    if kind == "col":
        return ref.at[:, pl.ds(pl.multiple_of(d * c, c), c)]
    return ref.at[pl.ds(pl.multiple_of(d * r, r), r), :]


def _full_shape(kind, shard_shape):
    r, c = shard_shape
    return (r, N_DEV * c) if kind == "col" else (N_DEV * r, c)


HBM = pl.BlockSpec(memory_space=pltpu.HBM)
SEM = pl.BlockSpec(memory_space=pltpu.SEMAPHORE)
EFFECT = pltpu.SideEffectType.DATAFLOW_SIDE_EFFECTING


class _Flight:
    def __init__(self, name, kind, n_remote, plan, send_sems, recv_sems, bufs, token):
        self.name, self.kind, self.n_remote, self.plan = name, kind, n_remote, plan
        self.send_sems, self.recv_sems, self.bufs, self.token = send_sems, recv_sems, bufs, token


def _start_copies(name, kind, bufs, n_remote, plan, after=None):
    nb = len(bufs)
    n_after = 0 if after is None else 1

    def body(*refs):
        in_refs = refs[:nb]
        send_sems, recv_sems = refs[nb + n_after], refs[nb + n_after + 1]
        token = refs[2 * nb + n_after + 2]
        me = _Place(lax.axis_index("x"), lax.axis_index("y"), lax.axis_index("c"))
        peers = _peers(me, kind)
        for n, (src, dst, k, _) in enumerate(plan(me, peers, in_refs)):
            pltpu.make_async_remote_copy(
                src_ref=src, dst_ref=dst, send_sem=send_sems.at[n], recv_sem=recv_sems.at[n],
                device_id=peers[k].id, device_id_type=MESH).start()
        token[...] = jnp.zeros_like(token)

    outs = pl.pallas_call(
        body, name=name, in_specs=[HBM] * nb + [ANY] * n_after,
        out_specs=(SEM, SEM) + (HBM,) * nb + (pl.BlockSpec(memory_space=pltpu.VMEM),),
        out_shape=(pltpu.SemaphoreType.DMA((n_remote,)), pltpu.SemaphoreType.DMA((n_remote,)))
        + tuple(pltpu.HBM(b.shape, b.dtype) for b in bufs) + (jax.ShapeDtypeStruct((8, LANE), F32),),
        input_output_aliases={a: 2 + a for a in range(nb)},
        compiler_params=pltpu.CompilerParams(has_side_effects=EFFECT),
    )(*[pltpu.with_memory_space_constraint(b, pltpu.HBM) for b in bufs], *([after] if n_after else []))
    return _Flight(name, kind, n_remote, plan, outs[0], outs[1], list(outs[2:2 + nb]), outs[2 + nb])


def _wait_copies(flight, after):
    nb = len(flight.bufs)
    plan, kind = flight.plan, flight.kind
    after = list(after) if isinstance(after, (tuple, list)) else [after]

    def body(*refs):
        in_refs = refs[:nb]
        send_sems, recv_sems = refs[nb], refs[nb + 1]
        me = _Place(lax.axis_index("x"), lax.axis_index("y"), lax.axis_index("c"))
        peers = _peers(me, kind)
        for n, (src, _, k, landing) in enumerate(plan(me, peers, in_refs)):
            cp = pltpu.make_async_remote_copy(
                src_ref=src, dst_ref=landing, send_sem=send_sems.at[n], recv_sem=recv_sems.at[n],
                device_id=peers[k].id, device_id_type=MESH)
            cp.wait_send()
            cp.wait_recv()

    outs = pl.pallas_call(
        body, name=flight.name.replace("_start", "_wait"),
        in_specs=[HBM] * nb + [SEM, SEM] + [ANY] * len(after), out_specs=(HBM,) * nb,
        out_shape=tuple(pltpu.HBM(b.shape, b.dtype) for b in flight.bufs),
        input_output_aliases={a: a for a in range(nb)},
        compiler_params=pltpu.CompilerParams(has_side_effects=EFFECT),
    )(*flight.bufs, flight.send_sems, flight.recv_sems, *after)
    return list(outs)


def _own_in_place(shard, kind, dev, dtype, deps=()):
    r, c = shard.shape
    tr = _row_tile(r, c)
    nb = r // tr

    def body(dev_ref, s_ref, *rest):
        o_ref = rest[-1]
        o_ref[...] = s_ref[...].astype(dtype)

    if kind == "col":
        o_spec = pl.BlockSpec((tr, c), lambda i, dev_ref: (i, dev_ref[0]))
    else:
        o_spec = pl.BlockSpec((tr, c), lambda i, dev_ref: (dev_ref[0] * nb + i, 0))
    return pl.pallas_call(
        body, name="own_in_place",
        grid_spec=pltpu.PrefetchScalarGridSpec(
            num_scalar_prefetch=1, grid=(nb,),
            in_specs=[pl.BlockSpec((tr, c), lambda i, dev_ref: (i, 0))] + [ANY] * len(deps), out_specs=o_spec),
        out_shape=jax.ShapeDtypeStruct(_full_shape(kind, shard.shape), dtype),
        compiler_params=_params(("parallel",)),
    )(dev, shard, *deps)


def _gather_ici_start(fulls, kinds, shapes, tag, after=None):
    n = len(fulls)

    def plan(me, peers, refs):
        remote = []
        for a in range(n):
            mine = _window(refs[a], kinds[a], me.dev, shapes[a])
            for k, p in enumerate(peers):
                remote.append((mine, mine, k, _window(refs[a], kinds[a], p.dev, shapes[a])))
        return remote

    return _start_copies("gather_ici_start_" + tag, "ici", list(fulls), 3 * n, plan, after)


def _gather_d2d_start(fulls, kinds, shapes, tag):
    n = len(fulls)

    def plan(me, peers, refs):
        remote = []
        for a in range(n):
            for ch in range(N_CHIP):
                held = _window(refs[a], kinds[a], 2 * ch + me.c, shapes[a])
                remote.append((held, held, 0, _window(refs[a], kinds[a], 2 * ch + 1 - me.c, shapes[a])))
        return remote

    return _start_copies("gather_d2d_start_" + tag, "d2d", fulls, N_CHIP * n, plan)


def _scatter_d2d_start(grads, kinds, shapes, tag):
    n = len(grads)
    lands = [lax.empty((N_CHIP,) + tuple(s), g.dtype) for g, s in zip(grads, shapes)]

    def plan(me, peers, refs):
        remote = []
        for a in range(n):
            for ch in range(N_CHIP):
                remote.append((_window(refs[a], kinds[a], 2 * ch + 1 - me.c, shapes[a]),
                               refs[n + a].at[ch], 0, refs[n + a].at[ch]))
        return remote

    return _start_copies("scatter_d2d_start_" + tag, "d2d", list(grads) + lands, N_CHIP * n, plan)


def _scatter_ici_start(sums, shapes, tag, after=None):
    n = len(sums)
    lands = [lax.empty((N_CHIP - 1,) + tuple(s), BF16) for s in shapes]

    def plan(me, peers, refs):
        remote = []
        for a in range(n):
            for k, p in enumerate(peers):
                remote.append((refs[a].at[p.chip], refs[n + a].at[k], k, refs[n + a].at[k]))
        return remote

    return _start_copies("scatter_ici_start_" + tag, "ici", list(sums) + lands, 3 * n, plan, after)


def _pair_add(g, r4, kind, shard_shape, core, *, name):
    r, c = shard_shape
    tr = _row_tile(r, c, 524288)
    nb = r // tr

    def body(core_ref, g_ref, r_ref, o_ref):
        o_ref[...] = (g_ref[...].astype(F32) + r_ref[...].astype(F32)).astype(BF16)

    if kind == "col":
        g_spec = pl.BlockSpec((tr, c), lambda ch, i, core_ref: (i, 2 * ch + core_ref[0]))
    else:
        g_spec = pl.BlockSpec((tr, c), lambda ch, i, core_ref: ((2 * ch + core_ref[0]) * nb + i, 0))
    slot = pl.BlockSpec((None, tr, c), lambda ch, i, core_ref: (ch, i, 0))
    return pl.pallas_call(
        body, name=name,
        grid_spec=pltpu.PrefetchScalarGridSpec(
            num_scalar_prefetch=1, grid=(N_CHIP, nb), in_specs=[g_spec, slot], out_specs=slot),
        out_shape=jax.ShapeDtypeStruct((N_CHIP, r, c), BF16),
        compiler_params=_params(("parallel", "parallel")),
    )(core, g, r4)


def _allreduce_small_start(part, chip):
    def plan_swap(me, peers, ins, outs):
        return [(ins[0], outs[0], 0, outs[0])], []

    (got,) = _exchange("allreduce_small_swap", "d2d", [part], [jax.ShapeDtypeStruct(part.shape, F32)],
                       {}, 1, 0, plan_swap)

    def body(chip_ref, a_ref, b_ref, o_ref):
        o_ref[...] = a_ref[...] + b_ref[...]

    whole = pl.BlockSpec(part.shape, lambda i, chip_ref: (0, 0))
    slots = pl.pallas_call(
        body, name="allreduce_small_pair",
        grid_spec=pltpu.PrefetchScalarGridSpec(
            num_scalar_prefetch=1, grid=(1,), in_specs=[whole, whole],
            out_specs=pl.BlockSpec((None,) + part.shape, lambda i, chip_ref: (chip_ref[0], 0, 0))),
        out_shape=jax.ShapeDtypeStruct((N_CHIP,) + part.shape, F32),
    )(chip, part, got)

    def plan(me, peers, refs):
        mine = refs[0].at[me.chip]
        return [(mine, mine, k, refs[0].at[p.chip]) for k, p in enumerate(peers)]

    return _start_copies("allreduce_small_start", "ici", [slots], 3, plan)


def _row_tile(r, c, limit=262144):
    best = None
    for t in range(8, r + 1, 8):
        if r % t == 0 and t * c <= limit:
            best = t
    return best if best is not None else r


def _adam_math(w, g, m, v):
    m = ADAM_B1 * m + (1.0 - ADAM_B1) * g
    v = ADAM_B2 * v + (1.0 - ADAM_B2) * (g * g)
    m_hat = m / (1.0 - ADAM_B1 ** ADAM_STEP)
    v_hat = v / (1.0 - ADAM_B2 ** ADAM_STEP)
    delta = -ADAM_LR * (m_hat / (jnp.sqrt(v_hat) + ADAM_EPS) + ADAM_WD * w)
    return delta, m, v


def _adamw(w, m, v, sums, lands, chip, layer, prev, *, name):
    L, r, c = w.shape
    tr = _row_tile(r, c)
    has_prev = prev is not None

    def body(chip_ref, *refs):
        w_ref, m_ref, v_ref, s_ref, l_ref = refs[:5]
        g_ref, d_ref, nm_ref, nv_ref = refs[-4:]
        g = (s_ref[...].astype(F32) + l_ref[0].astype(F32)) + l_ref[1].astype(F32) + l_ref[2].astype(F32)
        delta, nm, nv = _adam_math(w_ref[...], g, m_ref[...], v_ref[...])
        g_ref[...] = g
        d_ref[...] = delta
        nm_ref[...] = nm
        nv_ref[...] = nv

    slab = pl.BlockSpec((None, tr, c), lambda i, chip_ref: (layer, i, 0))
    in_specs = [slab, slab, slab, pl.BlockSpec((None, tr, c), lambda i, chip_ref: (chip_ref[0], i, 0)),
                pl.BlockSpec((N_CHIP - 1, tr, c), lambda i, chip_ref: (0, i, 0))]
    args = [chip, w, m, v, sums, lands]
    aliases = {}
    if has_prev:
        in_specs += [ANY] * 4
        args += list(prev)
        aliases = {6 + n: n for n in range(4)}
    return pl.pallas_call(
        body, name=name,
        grid_spec=pltpu.PrefetchScalarGridSpec(
            num_scalar_prefetch=1, grid=(r // tr,), in_specs=in_specs, out_specs=(slab,) * 4),
        out_shape=tuple(jax.ShapeDtypeStruct((L, r, c), F32) for _ in range(4)),
        input_output_aliases=aliases, compiler_params=_params(("parallel",)),
    )(*args)


def _adamw_small(w, g, m, v):
    def body(w_ref, g_ref, m_ref, v_ref, d_ref, nm_ref, nv_ref):
        d_ref[...], nm_ref[...], nv_ref[...] = _adam_math(w_ref[...], g_ref[...], m_ref[...], v_ref[...])

    return pl.pallas_call(
        body, name="adamw_small", out_shape=tuple(jax.ShapeDtypeStruct(w.shape, F32) for _ in range(3)),
    )(w, g, m, v)


def _sum4(parts):
    def body(p_ref, o_ref):
        o_ref[...] = (p_ref[0] + p_ref[1]) + p_ref[2] + p_ref[3]

    return pl.pallas_call(
        body, name="sum_chips", out_shape=jax.ShapeDtypeStruct(parts.shape[1:], F32))(parts)


_WEIGHTS = ["mix_norm", "mlp_norm", "sb_w_in", "sb_w_out", "fox_w_in", "fox_b_f", "fox_q_gain", "fox_k_gain",
            "fox_w_out", "mla_w_in", "mla_q_norm", "mla_kv_norm", "mla_w_uq", "mla_w_ukv", "mla_q_gain",
            "mla_k_gain", "mla_w_out", "mlp_w1", "mlp_w2"]
_BIG = {"sb_w_in": "col", "sb_w_out": "row", "fox_w_in": "row", "fox_w_out": "row", "mla_w_in": "row",
        "mla_w_uq": "col", "mla_w_ukv": "col", "mla_w_out": "row", "mlp_w1": "col", "mlp_w2": "row"}


def _layer_big(i):
    kind, j = i % 3, i // 3
    mixer = {0: ["sb_w_in", "sb_w_out"], 1: ["fox_w_in", "fox_w_out"],
             2: ["mla_w_in", "mla_w_uq", "mla_w_ukv", "mla_w_out"]}[kind]
    return [(n, j) for n in mixer] + [("mlp_w1", i), ("mlp_w2", i)]


def _stack_to_cols(a):
    r = a.shape[0] // N_DEV
    return a.reshape(N_DEV, r, a.shape[1]).transpose(1, 0, 2).reshape(r, N_DEV * a.shape[1])


def _cols_to_stack(a):
    c = a.shape[1] // N_DEV
    return a.reshape(a.shape[0], N_DEV, c).transpose(1, 0, 2).reshape(N_DEV * a.shape[0], c)


def _pack_rows(rows, width):
    rows = [jnp.pad(r.reshape(-1).astype(F32), (0, width - r.size)) for r in rows]
    pad = (-len(rows)) % 8
    rows += [jnp.zeros((width,), F32)] * pad
    return jnp.stack(rows)


def kernel(x, positions, mix_norm, mlp_norm, sb_w_in, sb_w_out, fox_w_in, fox_b_f, fox_q_gain, fox_k_gain, fox_w_out, mla_w_in, mla_q_norm, mla_kv_norm, mla_w_uq, mla_w_ukv, mla_q_gain, mla_k_gain, mla_w_out, mlp_w1, mlp_w2, loss_target, m_mix_norm, m_mlp_norm, m_sb_w_in, m_sb_w_out, m_fox_w_in, m_fox_b_f, m_fox_q_gain, m_fox_k_gain, m_fox_w_out, m_mla_w_in, m_mla_q_norm, m_mla_kv_norm, m_mla_w_uq, m_mla_w_ukv, m_mla_q_gain, m_mla_k_gain, m_mla_w_out, m_mlp_w1, m_mlp_w2, v_mix_norm, v_mlp_norm, v_sb_w_in, v_sb_w_out, v_fox_w_in, v_fox_b_f, v_fox_q_gain, v_fox_k_gain, v_fox_w_out, v_mla_w_in, v_mla_q_norm, v_mla_kv_norm, v_mla_w_uq, v_mla_w_ukv, v_mla_q_gain, v_mla_k_gain, v_mla_w_out, v_mlp_w1, v_mlp_w2):
    w_in = dict(zip(_WEIGHTS, (mix_norm, mlp_norm, sb_w_in, sb_w_out, fox_w_in, fox_b_f, fox_q_gain, fox_k_gain, fox_w_out, mla_w_in, mla_q_norm, mla_kv_norm, mla_w_uq, mla_w_ukv, mla_q_gain, mla_k_gain, mla_w_out, mlp_w1, mlp_w2)))
    m_in = dict(zip(_WEIGHTS, (m_mix_norm, m_mlp_norm, m_sb_w_in, m_sb_w_out, m_fox_w_in, m_fox_b_f, m_fox_q_gain, m_fox_k_gain, m_fox_w_out, m_mla_w_in, m_mla_q_norm, m_mla_kv_norm, m_mla_w_uq, m_mla_w_ukv, m_mla_q_gain, m_mla_k_gain, m_mla_w_out, m_mlp_w1, m_mlp_w2)))
    v_in = dict(zip(_WEIGHTS, (v_mix_norm, v_mlp_norm, v_sb_w_in, v_sb_w_out, v_fox_w_in, v_fox_b_f, v_fox_q_gain, v_fox_k_gain, v_fox_w_out, v_mla_w_in, v_mla_q_norm, v_mla_kv_norm, v_mla_w_uq, v_mla_w_ukv, v_mla_q_gain, v_mla_k_gain, v_mla_w_out, v_mlp_w1, v_mlp_w2)))
    depth, D = mix_norm.shape
    H = D // HEAD_DIM
    n_mla = mla_w_in.shape[0]
    dev = 4 * lax.axis_index("x") + 2 * lax.axis_index("y") + lax.axis_index("c")
    core = lax.axis_index("c").astype(jnp.int32).reshape(1)
    dev_arr = dev.astype(jnp.int32).reshape(1)

    chip = (2 * lax.axis_index("x") + lax.axis_index("y")).astype(jnp.int32).reshape(1)
    nq, nkv = mla_q_norm.shape[1], mla_kv_norm.shape[1]

    units, u_mix, u_mlp = [], {}, {}
    for i in range(depth):
        big = _layer_big(i)
        u_mix[i] = len(units)
        if i < SPLIT_LAYERS:
            units += [(i, ("mix",), big[:-2]), (i, ("mlp",), big[-2:])]
        else:
            units.append((i, ("mix", "mlp"), big))
        u_mlp[i] = len(units) - 1

    def split(i):
        return u_mix[i] != u_mlp[i]

    def tag_of(u):
        i, parts, _ = units[u]
        return f"l{i}" if len(parts) == 2 else f"l{i}_{parts[0]}"

    kinds_of, shapes_of, ici = [], [], []
    for i, parts, names in units:
        shards = [w_in[n][j] for n, j in names]
        kinds = [_BIG[n] for n, _ in names]
        behind = (ici[-1].token,) if ici else ()
        fulls = [_own_in_place(s, k, dev_arr, BF16, behind) for s, k in zip(shards, kinds)]
        if i % 3 == 2:
            shards.append(_pack_rows([mla_q_norm[i // 3], mla_kv_norm[i // 3]], LANE))
            kinds.append("row")
            fulls.append(_own_in_place(shards[-1], "row", dev_arr, F32, behind))
        kinds_of.append(kinds)
        shapes_of.append([s.shape for s in shards])
        ici.append(_gather_ici_start(fulls, kinds, shapes_of[-1], tag_of(len(ici)),
                                     ici[-1].token if ici else None))

    full = {n: [None] * w_in[n].shape[0] for n in _WEIGHTS}
    for n in ("mix_norm", "mlp_norm", "fox_b_f", "fox_q_gain", "fox_k_gain", "mla_q_gain", "mla_k_gain"):
        full[n] = w_in[n]
    P = {"mix_norm": mix_norm, "mlp_norm": mlp_norm, "mlp": [None] * depth,
         "sb": [None] * sb_w_in.shape[0], "fox": [None] * fox_w_in.shape[0], "mla": [None] * n_mla}

    def forward_to_sibling(u, after):
        arrived = _wait_copies(ici[u], after)
        return _gather_d2d_start(arrived, kinds_of[u], shapes_of[u], tag_of(u))

    def finish_gather(u, flight, after):
        i, parts, names = units[u]
        got = _wait_copies(flight, after)
        for (n, j), a in zip(names, got):
            full[n][j] = _stack_to_cols(a) if n == "fox_w_in" else a
        if i % 3 == 2:
            tiles = got[-1].reshape(N_DEV, 8, LANE)
            full["mla_q_norm"][i // 3] = tiles[:, 0, :nq].reshape(-1)
            full["mla_kv_norm"][i // 3] = tiles[:, 1, :nkv].reshape(-1)
        _prepare_layer(P, full, i, H, parts)

    tabs = _rope_tables(positions[0])
    d2d = forward_to_sibling(0, ici[-1].token)
    finish_gather(0, d2d, d2d.token)
    xs, saved = x[0], []
    for i in range(depth):
        nxt = {}

        def hook(x_mid, i=i, nxt=nxt):
            if split(i):
                mlp_d2d = forward_to_sibling(u_mlp[i], x_mid)
                finish_gather(u_mlp[i], mlp_d2d, mlp_d2d.token)
            if 1 <= i < depth - 1 and not split(i + 1):
                nxt["d2d"] = forward_to_sibling(u_mix[i + 1], x_mid)
                return (nxt["d2d"].token,)
            return ()

        xs, s = _layer_fwd(i, xs, P, tabs, H, hook)
        saved.append(s)
        if i + 1 < depth:
            if "d2d" in nxt:
                finish_gather(u_mix[i + 1], nxt["d2d"], xs)
            else:
                mix_d2d = forward_to_sibling(u_mix[i + 1], xs)
                finish_gather(u_mix[i + 1], mix_d2d, mix_d2d.token)
    dx, dxb, loss = _loss_head(xs, loss_target[0])
    loss = lax.psum(loss[0, 0], ("x", "y", "c"))

    def unit_meta(u):
        names = units[u][2]
        return [_BIG[n] for n, _ in names], [w_in[n].shape[1:] for n, _ in names]

    def to_sibling_start(u, Gl):
        grads = [(_cols_to_stack(Gl[n]) if n == "fox_w_in" else Gl[n]) for n, _ in units[u][2]]
        return _scatter_d2d_start(grads, *unit_meta(u), tag_of(u))

    def pair_and_send(u, flight, after, before=None):
        n = len(units[u][2])
        kinds, shapes = unit_meta(u)
        got = _wait_copies(flight, after)
        sums = [_pair_add(g, r4, k, s, core, name="pair_add_" + tag_of(u))
                for g, r4, k, s in zip(got[:n], got[n:], kinds, shapes)]
        return _scatter_ici_start(sums, shapes, tag_of(u), before)

    G, to_chips, deps, flying = [None] * depth, [None] * len(units), (), {}
    for i in reversed(range(depth)):
        def hook(dxb_mid, Gp, i=i):
            out = ()
            if i + 1 < depth:
                u = u_mix[i + 1]
                to_chips[u] = pair_and_send(u, flying.pop(u), dxb_mid)
                out = (to_chips[u].token,)
            if split(i):
                flying[u_mlp[i]] = to_sibling_start(u_mlp[i], {"mlp_w1": Gp["w1"], "mlp_w2": Gp["w2"]})
                out = out + (flying[u_mlp[i]].token,)
            return out

        def hook2(dv, i=i):
            if not split(i):
                return ()
            u = u_mlp[i]
            to_chips[u] = pair_and_send(u, flying.pop(u), dv)
            return (to_chips[u].token,)

        dx, dxb, Gp = _layer_bwd(i, dx, dxb, saved[i], P, tabs, H, deps, hook, hook2)
        G[i] = _layer_grads(i, Gp, H)
        flying[u_mix[i]] = to_sibling_start(u_mix[i], G[i])
        deps = (flying[u_mix[i]].token,)
    grad_x = dx
    to_chips[0] = pair_and_send(0, flying.pop(0), dx)

    small = [n for n in _WEIGHTS if n not in _BIG]
    rows = []
    for n in small:
        for j in range(w_in[n].shape[0]):
            rows.append(G[_layer_of(n, j)][n])
    small_flight = _allreduce_small_start(_pack_rows(rows, D) + to_chips[0].token[0, 0], chip)

    results, after = {}, [small_flight.token]
    for u in reversed(range(len(units))):
        names = units[u][2]
        got = _wait_copies(to_chips[u], after)
        for a, (n, j) in enumerate(names):
            results[n] = _adamw(w_in[n], m_in[n], v_in[n], got[a], got[len(names) + a], chip, j, results.get(n),
                                name="adamw_" + n)
        after = [results[n][0] for n, _ in names]

    (parts,) = _wait_copies(small_flight, after)
    sgr = _sum4(parts)
    g_rows, w_rows, m_rows, v_rows, at = [], [], [], [], 0
    for n in small:
        for j in range(w_in[n].shape[0]):
            width = w_in[n].shape[1]
            if n in ("mla_q_norm", "mla_kv_norm"):
                g_rows.append(lax.dynamic_slice(sgr[at], (dev * width,), (width,)))
            else:
                g_rows.append(sgr[at, :width])
            w_rows.append(w_in[n][j])
            m_rows.append(m_in[n][j])
            v_rows.append(v_in[n][j])
            at += 1
    g_pack = _pack_rows(g_rows, D)
    d_pack, nm_pack, nv_pack = _adamw_small(
        _pack_rows(w_rows, D), g_pack, _pack_rows(m_rows, D), _pack_rows(v_rows, D))
    at = 0
    for n in small:
        L, width = w_in[n].shape
        results[n] = tuple(p[at:at + L, :width] for p in (g_pack, d_pack, nm_pack, nv_pack))
        at += L

    out = [loss, grad_x[None]]
    for part in range(4):
        out += [results[n][part] for n in _WEIGHTS]
    return tuple(out)
```

```python
import functools
import math

import jax
import jax.numpy as jnp
from jax import lax
from jax.experimental import pallas as pl
from jax.experimental.pallas import tpu as pltpu

F32 = jnp.float32
BF16 = jnp.bfloat16

HEAD_DIM = 128
MLA_NOPE = 128
MLA_ROPE = 64
MLA_V = 128
MLA_QK = MLA_NOPE + MLA_ROPE
MLA_QK_PAD = 256
LANE = 128
ROPE_THETA = 10000.0
EPS = 1e-6
ADAM_LR = 0.001
ADAM_B1 = 0.9
ADAM_B2 = 0.999
ADAM_EPS = 1e-08
ADAM_WD = 0.01
ADAM_STEP = 10
N_DEV = 8
N_CHIP = 4
NEG = -1e30
VMEM_LIMIT = 56 * 1024 * 1024
ATT_BLOCK = 256
SB_BLOCK = 256
ATT_HEADS = 4
SPLIT_LAYERS = 2
SB_PARTS = 2
MESH = pl.DeviceIdType.MESH
ANY = pl.BlockSpec(memory_space=pl.ANY)


def _params(sem):
    return pltpu.CompilerParams(dimension_semantics=sem, vmem_limit_bytes=VMEM_LIMIT)


def _tile(dim, pref):
    if dim <= pref:
        return dim
    t = pref
    while dim % t:
        t -= LANE
    return t


def _dot(a, b, dims):
    return lax.dot_general(a, b, (dims, ((), ())), preferred_element_type=F32)


NN = ((1,), (0,))
NT = ((1,), (1,))
TN = ((0,), (0,))


def _mm(a, b, *, mode="nn", out_dtype=F32, res=None, act=None, z=None, name,
        tm=1024, tn=1024, tk=2048, deps=()):
    if mode == "nn":
        (M, K), (_, N) = a.shape, b.shape
    elif mode == "nt":
        (M, K), (N, _) = a.shape, b.shape
    else:
        (K, M), (_, N) = a.shape, b.shape
    tm, tn, tk = _tile(M, tm), _tile(N, tn), _tile(K, tk)
    nk = K // tk
    if mode == "tn":
        a_spec = pl.BlockSpec((tk, tm), lambda i, j, k: (k, i))
    else:
        a_spec = pl.BlockSpec((tm, tk), lambda i, j, k: (i, k))
    if mode == "nt":
        b_spec = pl.BlockSpec((tn, tk), lambda i, j, k: (j, k))
    else:
        b_spec = pl.BlockSpec((tk, tn), lambda i, j, k: (k, j))
    dims = {"nn": NN, "nt": NT, "tn": TN}[mode]
    o_spec = pl.BlockSpec((tm, tn), lambda i, j, k: (i, j))
    in_specs, args = [a_spec, b_spec], [a, b]
    if res is not None:
        in_specs.append(o_spec)
        args.append(res)
    if act == "drelu2":
        in_specs.append(o_spec)
        args.append(z)
    if act == "relu2":
        out_shape = (jax.ShapeDtypeStruct((M, N), F32), jax.ShapeDtypeStruct((M, N), BF16))
        out_specs = (o_spec, o_spec)
    else:
        out_shape = jax.ShapeDtypeStruct((M, N), out_dtype)
        out_specs = o_spec
    has_res, has_z = res is not None, act == "drelu2"
    n_out = 2 if act == "relu2" else 1
    in_specs += [ANY] * len(deps)
    args += list(deps)

    def body(*refs):
        a_ref, b_ref = refs[0], refs[1]
        idx = 2
        res_ref = z_ref = None
        if has_res:
            res_ref = refs[idx]
            idx += 1
        if has_z:
            z_ref = refs[idx]
            idx += 1
        idx += len(deps)
        outs = refs[idx:idx + n_out]

        def finish(r):
            if has_res:
                r = r + res_ref[...]
            if act == "relu2":
                outs[0][...] = r
                rr = jnp.maximum(r, 0.0)
                outs[1][...] = (rr * rr).astype(BF16)
            elif act == "drelu2":
                outs[0][...] = (r * (2.0 * jnp.maximum(z_ref[...], 0.0))).astype(out_dtype)
            else:
                outs[0][...] = r.astype(out_dtype)

        if nk == 1:
            finish(_dot(a_ref[...], b_ref[...], dims))
            return
        acc = refs[-1]
        k = pl.program_id(2)

        @pl.when(k == 0)
        def _():
            acc[...] = _dot(a_ref[...], b_ref[...], dims)

        @pl.when((k > 0) & (k < nk - 1))
        def _():
            acc[...] += _dot(a_ref[...], b_ref[...], dims)

        @pl.when(k == nk - 1)
        def _():
            finish(acc[...] + _dot(a_ref[...], b_ref[...], dims))

    return pl.pallas_call(
        body, name=name, grid=(M // tm, N // tn, nk), in_specs=in_specs, out_specs=out_specs,
        out_shape=out_shape, scratch_shapes=[pltpu.VMEM((tm, tn), F32)] if nk > 1 else [],
        compiler_params=_params(("parallel", "parallel", "arbitrary")),
    )(*args)


def _rmsnorm_fwd(x, g, *, name, tm=256):
    T, n = x.shape
    tm = _tile(T, tm)

    def body(x_ref, g_ref, o_ref):
        xf = x_ref[...]
        r = lax.rsqrt(jnp.mean(xf * xf, axis=-1, keepdims=True) + EPS)
        o_ref[...] = (xf * r * g_ref[...]).astype(BF16)

    return pl.pallas_call(
        body, name=name, grid=(T // tm,),
        in_specs=[pl.BlockSpec((tm, n), lambda i: (i, 0)), pl.BlockSpec((1, n), lambda i: (0, 0))],
        out_specs=pl.BlockSpec((tm, n), lambda i: (i, 0)),
        out_shape=jax.ShapeDtypeStruct((T, n), BF16),
        compiler_params=_params(("parallel",)),
    )(x, g)


def _rmsnorm_bwd(x, g, dy, dx_in=None, *, name, want_f32=True, tm=256):
    T, n = x.shape
    tm = _tile(T, tm)
    has_in = dx_in is not None
    row = pl.BlockSpec((tm, n), lambda i: (i, 0))
    vec = pl.BlockSpec((1, n), lambda i: (0, 0))

    def body(*refs):
        x_ref, g_ref, dy_ref = refs[:3]
        idx = 3
        in_ref = None
        if has_in:
            in_ref = refs[idx]
            idx += 1
        outs = refs[idx:]
        xf = x_ref[...]
        r = lax.rsqrt(jnp.mean(xf * xf, axis=-1, keepdims=True) + EPS)
        dyf = dy_ref[...]
        t = dyf * g_ref[...]
        dx = r * t - xf * (r * r * r * jnp.mean(t * xf, axis=-1, keepdims=True))
        if has_in:
            dx = dx + in_ref[...]
        o = 0
        if want_f32:
            outs[0][...] = dx
            o = 1
        outs[o][...] = dx.astype(BF16)
        dg_ref = outs[o + 1]

        @pl.when(pl.program_id(0) == 0)
        def _():
            dg_ref[...] = jnp.zeros_like(dg_ref)

        dg_ref[...] += jnp.sum(dyf * xf * r, axis=0, keepdims=True)

    in_specs = [row, vec, row] + ([row] if has_in else [])
    out_specs, out_shape = [], []
    if want_f32:
        out_specs.append(row)
        out_shape.append(jax.ShapeDtypeStruct((T, n), F32))
    out_specs += [row, vec]
    out_shape += [jax.ShapeDtypeStruct((T, n), BF16), jax.ShapeDtypeStruct((1, n), F32)]
    args = [x, g, dy] + ([dx_in] if has_in else [])
    return pl.pallas_call(
        body, name=name, grid=(T // tm,), in_specs=in_specs, out_specs=tuple(out_specs),
        out_shape=tuple(out_shape), compiler_params=_params(("arbitrary",)),
    )(*args)


def _loss_head(y, target, *, tm=256):
    T, n = y.shape
    tm = _tile(T, tm)
    row = pl.BlockSpec((tm, n), lambda i: (i, 0))

    def body(y_ref, t_ref, dy_ref, dyb_ref, loss_ref):
        err = y_ref[...] - t_ref[...]
        dy = err * (1.0 / n)
        dy_ref[...] = dy
        dyb_ref[...] = dy.astype(BF16)

        @pl.when(pl.program_id(0) == 0)
        def _():
            loss_ref[...] = jnp.zeros_like(loss_ref)

        part = 0.5 * jnp.sum(jnp.mean(err * err, axis=-1, keepdims=True), axis=0, keepdims=True)
        loss_ref[...] += part

    return pl.pallas_call(
        body, name="loss_head", grid=(T // tm,), in_specs=[row, row],
        out_specs=(row, row, pl.BlockSpec((1, 1), lambda i: (0, 0))),
        out_shape=(jax.ShapeDtypeStruct((T, n), F32), jax.ShapeDtypeStruct((T, n), BF16),
                   jax.ShapeDtypeStruct((1, 1), F32)),
        compiler_params=_params(("arbitrary",)),
    )(y, target)


def _split(v, n):
    parts = []
    for _ in range(n - 1):
        p = v.astype(BF16)
        parts.append(p)
        v = v - p.astype(F32)
    parts.append(v.astype(BF16))
    return parts


def _tri_left(tri, v):
    hi, mid, lo = _split(v, 3)
    return _dot(tri, hi, NN) + _dot(tri, mid, NN) + _dot(tri, lo, NN)


def _iota2(shape, dim):
    return lax.broadcasted_iota(jnp.int32, shape, dim)


def _log_sigmoid(z):
    return jnp.minimum(z, 0.0) - jnp.log1p(jnp.exp(-jnp.abs(z)))


def _log_sigmoid_abs(z):
    return jnp.minimum(z, 0.0) - jnp.log(1.0 + jnp.exp(-jnp.abs(z)))


def _head_spec(rows, width, off):
    return pl.BlockSpec((rows, width), lambda h, i: (i, off + h))


def _full_head_spec(T, width, off):
    return pl.BlockSpec((T, width), lambda h, i: (0, off + h))


def _sb_fwd(qkv, H):
    T = qkv.shape[0]
    B = _tile(T, SB_BLOCK)
    nq = T // B
    scale = 1.0 / math.sqrt(HEAD_DIM)

    HG = ATT_HEADS if H % ATT_HEADS == 0 else 1
    HD = HEAD_DIM

    def body(q_ref, k_ref, v_ref, o_ref, c_ref, tri_s):
        i = pl.program_id(1)
        tri = (_iota2((B, B), 0) > _iota2((B, B), 1)).astype(BF16)
        for p in range(SB_PARTS):
            tri_s[pl.ds(p * B, B), :] = tri
        rows = _iota2((B, B), 0)
        cols = _iota2((B, B), 1)

        def step(n, carry):
            j = i - n
            ks = pl.multiple_of(j * B, B)
            mask = (ks + cols) < (i * B + rows)
            out = []
            for g in range(HG):
                acc, run = carry[g]
                q = q_ref[:, pl.ds(g * HD, HD)]
                kb = k_ref[pl.ds(ks, B), pl.ds(g * HD, HD)]
                vb = v_ref[pl.ds(ks, B), pl.ds(g * HD, HD)]
                z = _dot(q, kb, NT) * scale
                ls = _log_sigmoid_abs(z)
                lk = jnp.where(mask, ls - z, 0.0)
                parts = jnp.concatenate(_split(lk, SB_PARTS), axis=1)
                later = _dot(parts, tri_s[...], NN)
                a = jnp.exp(jnp.where(mask, ls, NEG) + later).astype(BF16)
                acc = acc + jnp.exp(run) * _dot(a, vb, NN)
                out.append((acc, run + jnp.sum(lk, axis=1, keepdims=True)))
            return tuple(out)

        init = tuple((jnp.zeros((B, HD), F32), jnp.zeros((B, 1), F32)) for _ in range(HG))
        res = lax.fori_loop(0, i + 1, step, init)
        for g in range(HG):
            o_ref[:, pl.ds(g * HD, HD)] = res[g][0].astype(BF16)
            c_ref[g] = res[g][1]

    return pl.pallas_call(
        body, name="sb_attn_fwd", grid=(H // HG, nq),
        in_specs=[_head_spec(B, HG * HD, 0), _full_head_spec(T, HG * HD, H // HG),
                  _full_head_spec(T, HG * HD, 2 * H // HG)],
        out_specs=(_head_spec(B, HG * HD, 0), pl.BlockSpec((HG, B, 1), lambda h, i: (h, i, 0))),
        out_shape=(jax.ShapeDtypeStruct((T, H * HD), BF16), jax.ShapeDtypeStruct((H, T, 1), F32)),
        scratch_shapes=[pltpu.VMEM((SB_PARTS * B, B), BF16)],
        compiler_params=_params(("parallel", "arbitrary")),
    )(qkv, qkv, qkv)


def _sb_bwd(qkv, do, ctot, H):
    T = qkv.shape[0]
    B = _tile(T, SB_BLOCK)
    nq = T // B
    scale = 1.0 / math.sqrt(HEAD_DIM)
    HG = ATT_HEADS if H % ATT_HEADS == 0 else 1
    HD = HEAD_DIM

    def body(q_ref, k_ref, v_ref, do_ref, c_ref, dq_ref, dk_ref, dv_ref, dk_acc, dv_acc,
             tri_incl_s, tri_strict_s):
        i = pl.program_id(1)

        @pl.when(i == 0)
        def _():
            dk_acc[...] = jnp.zeros_like(dk_acc)
            dv_acc[...] = jnp.zeros_like(dv_acc)

        rows = _iota2((B, B), 0)
        cols = _iota2((B, B), 1)
        tri_incl = (rows <= cols).astype(BF16)
        tri_strict = (rows < cols).astype(BF16)
        for p in range(SB_PARTS):
            tri_incl_s[pl.ds(p * B, B), :] = tri_incl
            tri_strict_s[pl.ds(p * B, B), :] = tri_strict
        heads = [(q_ref[:, pl.ds(g * HD, HD)], do_ref[:, pl.ds(g * HD, HD)], c_ref[g]) for g in range(HG)]

        def step(j, carry):
            ks = pl.multiple_of(j * B, B)
            mask = (ks + cols) < (i * B + rows)
            out = []
            for g in range(HG):
                dq, lpre, gpre = carry[g]
                q, do_b, ctot_b = heads[g]
                kb = k_ref[pl.ds(ks, B), pl.ds(g * HD, HD)]
                vb = v_ref[pl.ds(ks, B), pl.ds(g * HD, HD)]
                z = _dot(q, kb, NT) * scale
                da = _dot(do_b, vb, NT)
                ls = _log_sigmoid_abs(z)
                lk = jnp.where(mask, ls - z, 0.0)
                lsm = jnp.where(mask, ls, NEG)
                incl = _dot(jnp.concatenate(_split(lk, SB_PARTS), axis=1), tri_incl_s[...], NN)
                a = jnp.exp(lsm + (ctot_b - lpre) - incl)
                gw = a * da
                gex = gpre + _dot(jnp.concatenate(_split(gw, SB_PARTS), axis=1), tri_strict_s[...], NN)
                dzb = ((gw * jnp.exp(lsm - z) - jnp.exp(lsm) * gex) * scale).astype(BF16)
                dq = dq + _dot(dzb, kb, NN)
                dk_acc[pl.ds(ks, B), pl.ds(g * HD, HD)] += _dot(dzb, q, TN)
                dv_acc[pl.ds(ks, B), pl.ds(g * HD, HD)] += _dot(a.astype(BF16), do_b, TN)
                out.append((dq, lpre + jnp.sum(lk, axis=1, keepdims=True),
                            gpre + jnp.sum(gw, axis=1, keepdims=True)))
            return tuple(out)

        init = tuple((jnp.zeros((B, HD), F32), jnp.zeros((B, 1), F32), jnp.zeros((B, 1), F32))
                     for _ in range(HG))
        res = lax.fori_loop(0, i + 1, step, init)
        for g in range(HG):
            dq_ref[:, pl.ds(g * HD, HD)] = res[g][0].astype(BF16)

        @pl.when(i == nq - 1)
        def _():
            dk_ref[...] = dk_acc[...].astype(BF16)
            dv_ref[...] = dv_acc[...].astype(BF16)

    W = H * HD
    return pl.pallas_call(
        body, name="sb_attn_bwd", grid=(H // HG, nq),
        in_specs=[_head_spec(B, HG * HD, 0), _full_head_spec(T, HG * HD, H // HG),
                  _full_head_spec(T, HG * HD, 2 * H // HG), _head_spec(B, HG * HD, 0),
                  pl.BlockSpec((HG, B, 1), lambda h, i: (h, i, 0))],
        out_specs=(_head_spec(B, HG * HD, 0), _full_head_spec(T, HG * HD, 0),
                   _full_head_spec(T, HG * HD, 0)),
        out_shape=tuple(jax.ShapeDtypeStruct((T, W), BF16) for _ in range(3)),
        scratch_shapes=[pltpu.VMEM((T, HG * HD), F32), pltpu.VMEM((T, HG * HD), F32),
                        pltpu.VMEM((SB_PARTS * B, B), BF16), pltpu.VMEM((SB_PARTS * B, B), BF16)],
        compiler_params=_params(("parallel", "arbitrary")),
    )(qkv, qkv, qkv, do, ctot)


def _softmax_fwd(q, k, v, cf, H, dqk, scale, *, name):
    T = q.shape[0]
    B = _tile(T, ATT_BLOCK)
    nq = T // B
    has_cf = cf is not None
    HG = ATT_HEADS if H % ATT_HEADS == 0 else 1
    HD = HEAD_DIM

    def body(*refs):
        q_ref, k_ref, v_ref = refs[:3]
        idx = 3
        if has_cf:
            cfc_ref, cfr_ref = refs[3], refs[4]
            idx = 5
        o_ref, of_ref, lse_ref = refs[idx:idx + 3]
        i = pl.program_id(1)
        rows = _iota2((B, B), 0)
        cols = _iota2((B, B), 1)

        def step(j, carry):
            ks = pl.multiple_of(j * B, B)
            mask = (ks + cols) <= (i * B + rows)
            out = []
            for g in range(HG):
                m, l, acc = carry[g]
                qb = q_ref[:, pl.ds(g * dqk, dqk)]
                kb = k_ref[pl.ds(ks, B), pl.ds(g * dqk, dqk)]
                vb = v_ref[pl.ds(ks, B), pl.ds(g * HD, HD)]
                s = _dot(qb, kb, NT) * scale
                if has_cf:
                    s = s + (cfc_ref[g] - cfr_ref[g, :, pl.ds(ks, B)])
                s = jnp.where(mask, s, NEG)
                m_new = jnp.maximum(m, jnp.max(s, axis=1, keepdims=True))
                alpha = jnp.exp(m - m_new)
                p = jnp.exp(s - m_new)
                l = alpha * l + jnp.sum(p, axis=1, keepdims=True)
                acc = alpha * acc + _dot(p.astype(BF16), vb, NN)
                out.append((m_new, l, acc))
            return tuple(out)

        init = tuple((jnp.full((B, 1), NEG, F32), jnp.zeros((B, 1), F32), jnp.zeros((B, HD), F32))
                     for _ in range(HG))
        res = lax.fori_loop(0, i + 1, step, init)
        for g in range(HG):
            m, l, acc = res[g]
            o = acc / l
            o_ref[:, pl.ds(g * HD, HD)] = o.astype(BF16)
            of_ref[:, pl.ds(g * HD, HD)] = o
            lse_ref[g] = m + jnp.log(l)

    stat = pl.BlockSpec((HG, B, 1), lambda h, i: (h, i, 0))
    in_specs = [_head_spec(B, HG * dqk, 0), _full_head_spec(T, HG * dqk, 0), _full_head_spec(T, HG * HD, 0)]
    args = [q, k, v]
    if has_cf:
        in_specs += [stat, pl.BlockSpec((HG, 1, T), lambda h, i: (h, 0, 0))]
        args += list(cf)
    W = H * HD
    return pl.pallas_call(
        body, name=name, grid=(H // HG, nq), in_specs=in_specs,
        out_specs=(_head_spec(B, HG * HD, 0), _head_spec(B, HG * HD, 0), stat),
        out_shape=(jax.ShapeDtypeStruct((T, W), BF16), jax.ShapeDtypeStruct((T, W), F32),
                   jax.ShapeDtypeStruct((H, T, 1), F32)),
        compiler_params=_params(("parallel", "arbitrary")),
    )(*args)


def _softmax_bwd(q, k, v, cf, do, o, lse, H, dqk, scale, *, name):
    T = q.shape[0]
    B = _tile(T, ATT_BLOCK)
    nq = T // B
    has_cf = cf is not None
    HG = ATT_HEADS if H % ATT_HEADS == 0 else 1
    HD = HEAD_DIM

    def body(*refs):
        q_ref, k_ref, v_ref, do_ref, o_ref, lse_ref = refs[:6]
        idx = 6
        if has_cf:
            cfc_ref, cfr_ref = refs[6], refs[7]
            idx = 8
        dq_ref, dk_ref, dv_ref = refs[idx:idx + 3]
        idx += 3
        if has_cf:
            dcc_ref, dcr_ref = refs[idx], refs[idx + 1]
        i = pl.program_id(1)

        @pl.when(i == 0)
        def _():
            dk_ref[...] = jnp.zeros_like(dk_ref)
            dv_ref[...] = jnp.zeros_like(dv_ref)
            if has_cf:
                dcr_ref[...] = jnp.zeros_like(dcr_ref)

        rows = _iota2((B, B), 0)
        cols = _iota2((B, B), 1)
        heads = []
        for g in range(HG):
            do_b = do_ref[:, pl.ds(g * HD, HD)]
            delta = jnp.sum(do_b.astype(F32) * o_ref[:, pl.ds(g * HD, HD)], axis=1, keepdims=True)
            heads.append((q_ref[:, pl.ds(g * dqk, dqk)], do_b, lse_ref[g], delta))

        def step(j, carry):
            ks = pl.multiple_of(j * B, B)
            mask = (ks + cols) <= (i * B + rows)
            out = []
            for g in range(HG):
                dq, rs = carry[g]
                qb, do_b, lse_b, delta = heads[g]
                kb = k_ref[pl.ds(ks, B), pl.ds(g * dqk, dqk)]
                vb = v_ref[pl.ds(ks, B), pl.ds(g * HD, HD)]
                s = _dot(qb, kb, NT) * scale
                if has_cf:
                    s = s + (cfc_ref[g] - cfr_ref[g, :, pl.ds(ks, B)])
                p = jnp.where(mask, jnp.exp(s - lse_b), 0.0)
                dp = _dot(do_b, vb, NT)
                ds = p * (dp - delta)
                dsb = (ds * scale).astype(BF16)
                dq = dq + _dot(dsb, kb, NN)
                dk_ref[pl.ds(ks, B), pl.ds(g * dqk, dqk)] += _dot(dsb, qb, TN)
                dv_ref[pl.ds(ks, B), pl.ds(g * HD, HD)] += _dot(p.astype(BF16), do_b, TN)
                if has_cf:
                    rs = rs + jnp.sum(ds, axis=1, keepdims=True)
                    dcr_ref[g, :, pl.ds(ks, B)] -= jnp.sum(ds, axis=0, keepdims=True)
                out.append((dq, rs))
            return tuple(out)

        init = tuple((jnp.zeros((B, dqk), F32), jnp.zeros((B, 1), F32)) for _ in range(HG))
        res = lax.fori_loop(0, i + 1, step, init)
        for g in range(HG):
            dq_ref[:, pl.ds(g * dqk, dqk)] = res[g][0]
            if has_cf:
                dcc_ref[g] = res[g][1]

    stat = pl.BlockSpec((HG, B, 1), lambda h, i: (h, i, 0))
    rowstat = pl.BlockSpec((HG, 1, T), lambda h, i: (h, 0, 0))
    in_specs = [_head_spec(B, HG * dqk, 0), _full_head_spec(T, HG * dqk, 0), _full_head_spec(T, HG * HD, 0),
                _head_spec(B, HG * HD, 0), _head_spec(B, HG * HD, 0), stat]
    args = [q, k, v, do, o, lse]
    out_specs = [_head_spec(B, HG * dqk, 0), _full_head_spec(T, HG * dqk, 0), _full_head_spec(T, HG * HD, 0)]
    out_shape = [jax.ShapeDtypeStruct((T, H * dqk), F32), jax.ShapeDtypeStruct((T, H * dqk), F32),
                 jax.ShapeDtypeStruct((T, H * HD), F32)]
    if has_cf:
        in_specs += [stat, rowstat]
        args += list(cf)
        out_specs += [stat, rowstat]
        out_shape += [jax.ShapeDtypeStruct((H, T, 1), F32), jax.ShapeDtypeStruct((H, 1, T), F32)]
    return pl.pallas_call(
        body, name=name, grid=(H // HG, nq), in_specs=in_specs, out_specs=tuple(out_specs),
        out_shape=tuple(out_shape), compiler_params=_params(("parallel", "arbitrary")),
    )(*args)


def _headnorm(x, g):
    r = lax.rsqrt(jnp.mean(x * x, axis=-1, keepdims=True) + EPS)
    return x * r * g, r


def _headnorm_bwd(x, g, dy, n):
    r = lax.rsqrt(jnp.sum(x * x, axis=-1, keepdims=True) * (1.0 / n) + EPS)
    t = dy * g
    dx = r * t - x * (r * r * r * (jnp.sum(t * x, axis=-1, keepdims=True) * (1.0 / n)))
    dg = jnp.sum(dy * x * r, axis=0, keepdims=True)
    return dx, dg


def _fox_prep_fwd(qkv, gq, gk, H, *, tm=512):
    T = qkv.shape[0]
    tm = _tile(T, tm)

    def body(q_ref, k_ref, v_ref, gq_ref, gk_ref, qn_ref, kn_ref, vb_ref):
        qn_ref[...] = _headnorm(q_ref[...], gq_ref[...])[0].astype(BF16)
        kn_ref[...] = _headnorm(k_ref[...], gk_ref[...])[0].astype(BF16)
        vb_ref[...] = v_ref[...].astype(BF16)

    def blk(off):
        return pl.BlockSpec((tm, HEAD_DIM), lambda i, h: (i, off + h))

    vec = pl.BlockSpec((1, HEAD_DIM), lambda i, h: (0, 0))
    W = H * HEAD_DIM
    return pl.pallas_call(
        body, name="fox_prep_fwd", grid=(T // tm, H),
        in_specs=[blk(0), blk(H), blk(2 * H), vec, vec], out_specs=(blk(0), blk(0), blk(0)),
        out_shape=tuple(jax.ShapeDtypeStruct((T, W), BF16) for _ in range(3)),
        compiler_params=_params(("parallel", "parallel")),
    )(qkv, qkv, qkv, gq, gk)


def _fox_prep_bwd(qkv, gq, gk, dqn, dkn, dv, H, *, tm=512):
    T = qkv.shape[0]
    tm = _tile(T, tm)

    def body(q_ref, k_ref, gq_ref, gk_ref, dqn_ref, dkn_ref, dv_ref,
             dq_ref, dk_ref, dvb_ref, dgq_ref, dgk_ref):
        @pl.when((pl.program_id(0) == 0) & (pl.program_id(1) == 0))
        def _():
            dgq_ref[...] = jnp.zeros_like(dgq_ref)
            dgk_ref[...] = jnp.zeros_like(dgk_ref)

        dq, dgq = _headnorm_bwd(q_ref[...], gq_ref[...], dqn_ref[...], HEAD_DIM)
        dk, dgk = _headnorm_bwd(k_ref[...], gk_ref[...], dkn_ref[...], HEAD_DIM)
        dq_ref[...] = dq.astype(BF16)
        dk_ref[...] = dk.astype(BF16)
        dvb_ref[...] = dv_ref[...].astype(BF16)
        dgq_ref[...] += dgq
        dgk_ref[...] += dgk

    def blk(off):
        return pl.BlockSpec((tm, HEAD_DIM), lambda i, h: (i, off + h))

    vec = pl.BlockSpec((1, HEAD_DIM), lambda i, h: (0, 0))
    W = H * HEAD_DIM
    return pl.pallas_call(
        body, name="fox_prep_bwd", grid=(T // tm, H),
        in_specs=[blk(0), blk(H), vec, vec, blk(0), blk(0), blk(0)],
        out_specs=(blk(0), blk(0), blk(0), vec, vec),
        out_shape=tuple(jax.ShapeDtypeStruct((T, W), BF16) for _ in range(3))
        + (jax.ShapeDtypeStruct((1, HEAD_DIM), F32), jax.ShapeDtypeStruct((1, HEAD_DIM), F32)),
        compiler_params=_params(("arbitrary", "arbitrary")),
    )(qkv, qkv, gq, gk, dqn, dkn, dv)


def _fox_gate_fwd(flog, bf, *, tm=256):
    T = flog.shape[0]
    tm = _tile(T, tm)

    def body(f_ref, b_ref, cf_ref, carry):
        @pl.when(pl.program_id(0) == 0)
        def _():
            carry[...] = jnp.zeros_like(carry)

        lf = _log_sigmoid(f_ref[...] + b_ref[...])
        tri = (_iota2((tm, tm), 1) <= _iota2((tm, tm), 0)).astype(BF16)
        cf_ref[...] = carry[...] + _tri_left(tri, lf)
        carry[...] += jnp.sum(lf, axis=0, keepdims=True)

    return pl.pallas_call(
        body, name="fox_gate_fwd", grid=(T // tm,),
        in_specs=[pl.BlockSpec((tm, LANE), lambda i: (i, 0)), pl.BlockSpec((1, LANE), lambda i: (0, 0))],
        out_specs=pl.BlockSpec((tm, LANE), lambda i: (i, 0)),
        out_shape=jax.ShapeDtypeStruct((T, LANE), F32),
        scratch_shapes=[pltpu.VMEM((1, LANE), F32)],
        compiler_params=_params(("arbitrary",)),
    )(flog, bf)


def _fox_gate_bwd(flog, bf, dcf, *, tm=256):
    T = flog.shape[0]
    tm = _tile(T, tm)
    nt = T // tm

    def body(f_ref, b_ref, dcf_ref, df_ref, db_ref, carry):
        @pl.when(pl.program_id(0) == 0)
        def _():
            carry[...] = jnp.zeros_like(carry)
            db_ref[...] = jnp.zeros_like(db_ref)

        d = dcf_ref[...]
        tri = (_iota2((tm, tm), 1) >= _iota2((tm, tm), 0)).astype(BF16)
        dlf = carry[...] + _tri_left(tri, d)
        carry[...] += jnp.sum(d, axis=0, keepdims=True)
        xg = f_ref[...] + b_ref[...]
        e = jnp.exp(-jnp.abs(xg))
        sig_neg = jnp.where(xg >= 0.0, e, 1.0) / (1.0 + e)
        df = dlf * sig_neg
        df_ref[...] = df.astype(BF16)
        db_ref[...] += jnp.sum(df, axis=0, keepdims=True)

    rev = pl.BlockSpec((tm, LANE), lambda i: (nt - 1 - i, 0))
    vec = pl.BlockSpec((1, LANE), lambda i: (0, 0))
    return pl.pallas_call(
        body, name="fox_gate_bwd", grid=(nt,), in_specs=[rev, vec, rev], out_specs=(rev, vec),
        out_shape=(jax.ShapeDtypeStruct((T, LANE), BF16), jax.ShapeDtypeStruct((1, LANE), F32)),
        scratch_shapes=[pltpu.VMEM((1, LANE), F32)],
        compiler_params=_params(("arbitrary",)),
    )(flog, bf, dcf)


def _rope_tables(positions):
    half = MLA_ROPE // 2
    inv_freq = ROPE_THETA ** (-jnp.arange(0, half, dtype=F32) * 2.0 / MLA_ROPE)
    ang = positions.astype(F32)[:, None] * inv_freq
    cos, sin = jnp.cos(ang), jnp.sin(ang)
    zero = jnp.zeros_like(cos)
    pad = jnp.zeros((positions.shape[0], LANE - MLA_ROPE), F32)
    cos_t = jnp.concatenate([cos, cos, pad], axis=1)
    sin_up = jnp.concatenate([zero, sin, pad], axis=1)
    sin_dn = jnp.concatenate([-sin, zero, pad], axis=1)
    return cos_t, sin_up, sin_dn


def _rope(x, cos_t, sin_up, sin_dn):
    half = MLA_ROPE // 2
    return x * cos_t + pltpu.roll(x, half, 1) * sin_up + pltpu.roll(x, LANE - half, 1) * sin_dn


def _rope_t(d, cos_t, sin_up, sin_dn):
    half = MLA_ROPE // 2
    return d * cos_t + pltpu.roll(d * sin_up, LANE - half, 1) + pltpu.roll(d * sin_dn, half, 1)


def _norm192(xcat, g):
    r = lax.rsqrt(jnp.sum(xcat * xcat, axis=-1, keepdims=True) * (1.0 / MLA_QK) + EPS)
    return xcat * r * g


def _mla_prep_fwd(qfull, kv, kr, tabs, gq, gk, H, *, tm=512):
    T = qfull.shape[0]
    tm = _tile(T, tm)

    def body(q_ref, kv_ref, kr_ref, c_ref, su_ref, sd_ref, gq_ref, gk_ref, qf_ref, kf_ref, v_ref):
        tabs_b = (c_ref[...], su_ref[...], sd_ref[...])
        qb = q_ref[...]
        qcat = jnp.concatenate([qb[:, :MLA_NOPE], _rope(qb[:, MLA_NOPE:], *tabs_b)], axis=1)
        qf_ref[...] = _norm192(qcat, gq_ref[...]).astype(BF16)
        kvb = kv_ref[...]
        kcat = jnp.concatenate([kvb[:, :MLA_NOPE], _rope(kr_ref[...], *tabs_b)], axis=1)
        kf_ref[...] = _norm192(kcat, gk_ref[...]).astype(BF16)
        v_ref[...] = kvb[:, MLA_NOPE:].astype(BF16)

    head = pl.BlockSpec((tm, MLA_QK_PAD), lambda i, h: (i, h))
    tok = pl.BlockSpec((tm, LANE), lambda i, h: (i, 0))
    vec = pl.BlockSpec((1, MLA_QK_PAD), lambda i, h: (0, 0))
    return pl.pallas_call(
        body, name="mla_prep_fwd", grid=(T // tm, H),
        in_specs=[head, head, tok, tok, tok, tok, vec, vec],
        out_specs=(head, head, pl.BlockSpec((tm, MLA_V), lambda i, h: (i, h))),
        out_shape=(jax.ShapeDtypeStruct((T, H * MLA_QK_PAD), BF16),
                   jax.ShapeDtypeStruct((T, H * MLA_QK_PAD), BF16),
                   jax.ShapeDtypeStruct((T, H * MLA_V), BF16)),
        compiler_params=_params(("parallel", "parallel")),
    )(qfull, kv, kr, *tabs, gq, gk)


def _mla_prep_bwd(qfull, kv, kr, tabs, gq, gk, dqf, dkf, dv, H, *, tm=512):
    T = qfull.shape[0]
    tm = _tile(T, tm)

    def body(q_ref, kv_ref, kr_ref, c_ref, su_ref, sd_ref, gq_ref, gk_ref, dqf_ref, dkf_ref, dv_ref,
             dq_ref, dkv_ref, dkr_ref, dgq_ref, dgk_ref, kr_acc):
        h = pl.program_id(1)

        @pl.when((pl.program_id(0) == 0) & (h == 0))
        def _():
            dgq_ref[...] = jnp.zeros_like(dgq_ref)
            dgk_ref[...] = jnp.zeros_like(dgk_ref)

        @pl.when(h == 0)
        def _():
            kr_acc[...] = jnp.zeros_like(kr_acc)

        tabs_b = (c_ref[...], su_ref[...], sd_ref[...])
        qb = q_ref[...]
        qcat = jnp.concatenate([qb[:, :MLA_NOPE], _rope(qb[:, MLA_NOPE:], *tabs_b)], axis=1)
        dqcat, dgq = _headnorm_bwd(qcat, gq_ref[...], dqf_ref[...], MLA_QK)
        dq_ref[...] = jnp.concatenate(
            [dqcat[:, :MLA_NOPE], _rope_t(dqcat[:, MLA_NOPE:], *tabs_b)], axis=1).astype(BF16)
        dgq_ref[...] += dgq
        kvb = kv_ref[...]
        kcat = jnp.concatenate([kvb[:, :MLA_NOPE], _rope(kr_ref[...], *tabs_b)], axis=1)
        dkcat, dgk = _headnorm_bwd(kcat, gk_ref[...], dkf_ref[...], MLA_QK)
        dkv_ref[...] = jnp.concatenate([dkcat[:, :MLA_NOPE], dv_ref[...]], axis=1).astype(BF16)
        dgk_ref[...] += dgk
        kr_acc[...] += dkcat[:, MLA_NOPE:]

        @pl.when(h == H - 1)
        def _():
            dkr_ref[...] = _rope_t(kr_acc[...], *tabs_b).astype(BF16)

    head = pl.BlockSpec((tm, MLA_QK_PAD), lambda i, h: (i, h))
    tok = pl.BlockSpec((tm, LANE), lambda i, h: (i, 0))
    vec = pl.BlockSpec((1, MLA_QK_PAD), lambda i, h: (0, 0))
    return pl.pallas_call(
        body, name="mla_prep_bwd", grid=(T // tm, H),
        in_specs=[head, head, tok, tok, tok, tok, vec, vec, head, head,
                  pl.BlockSpec((tm, MLA_V), lambda i, h: (i, h))],
        out_specs=(head, head, tok, vec, vec),
        out_shape=(jax.ShapeDtypeStruct((T, H * MLA_QK_PAD), BF16),
                   jax.ShapeDtypeStruct((T, H * MLA_QK_PAD), BF16),
                   jax.ShapeDtypeStruct((T, LANE), BF16),
                   jax.ShapeDtypeStruct((1, MLA_QK_PAD), F32), jax.ShapeDtypeStruct((1, MLA_QK_PAD), F32)),
        scratch_shapes=[pltpu.VMEM((tm, LANE), F32)],
        compiler_params=_params(("arbitrary", "arbitrary")),
    )(qfull, kv, kr, *tabs, gq, gk, dqf, dkf, dv)


def _cf_layouts(cf, H):
    cfh = cf[:, :H].T
    return cfh[:, :, None], cfh[:, None, :]


def _layer_fwd(i, x, P, tabs, H, hook=None):
    kind, j = i % 3, i // 3
    s = {"x_in": x}
    h = _rmsnorm_fwd(x, P["mix_norm"][i:i + 1], name="mix_norm_fwd")
    s["h"] = h
    if kind == 0:
        w = P["sb"][j]
        qkv = _mm(h, w["w_in"], out_dtype=BF16, name="sb_qkv")
        o, ctot = _sb_fwd(qkv, H)
        s.update(qkv=qkv, o=o, ctot=ctot)
    elif kind == 1:
        w = P["fox"][j]
        qkv = _mm(h, w["w_qkv"], name="fox_qkv")
        flog = _mm(h, w["w_f"], name="fox_gate_proj")
        qn, kn, vb = _fox_prep_fwd(qkv, w["gq"], w["gk"], H)
        cf = _cf_layouts(_fox_gate_fwd(flog, w["bf"]), H)
        o, of, lse = _softmax_fwd(qn, kn, vb, cf, H, HEAD_DIM, 1.0 / math.sqrt(HEAD_DIM), name="fox_attn_fwd")
        s.update(qkv=qkv, flog=flog, qn=qn, kn=kn, vb=vb, cf=cf, o=o, of=of, lse=lse)
    else:
        w = P["mla"][j]
        dq = _mm(h, w["w_dq"], name="mla_down_q")
        dkv = _mm(h, w["w_dkv"], name="mla_down_kv")
        kr = _mm(h, w["w_dr"], name="mla_down_rope")
        cq = _rmsnorm_fwd(dq, w["q_norm"], name="mla_q_norm_fwd")
        ckv = _rmsnorm_fwd(dkv, w["kv_norm"], name="mla_kv_norm_fwd")
        qfull = _mm(cq, w["w_uq"], name="mla_up_q")
        kv = _mm(ckv, w["w_ukv"], name="mla_up_kv")
        qf, kf, v = _mla_prep_fwd(qfull, kv, kr, tabs, w["gq"], w["gk"], H)
        o, of, lse = _softmax_fwd(qf, kf, v, None, H, MLA_QK_PAD, 1.0 / math.sqrt(MLA_QK), name="mla_attn_fwd")
        s.update(dq=dq, dkv=dkv, kr=kr, cq=cq, ckv=ckv, qfull=qfull, kv=kv, qf=qf, kf=kf, v=v,
                 o=o, of=of, lse=lse)
    x = _mm(s["o"], w["w_out"], res=x, name="mixer_out")
    s["x_mid"] = x
    deps = hook(x) if hook is not None else ()
    h2 = _rmsnorm_fwd(x, P["mlp_norm"][i:i + 1], name="mlp_norm_fwd")
    z, u = _mm(h2, P["mlp"][i]["w1"], act="relu2", name="mlp_up", deps=deps)
    x = _mm(u, P["mlp"][i]["w2"], res=x, name="mlp_down")
    s.update(h2=h2, z=z, u=u)
    return x, s


def _layer_bwd(i, dx, dxb, s, P, tabs, H, deps=(), hook=None, hook2=None):
    kind, j = i % 3, i // 3
    G = {}
    wm = P["mlp"][i]
    dz = _mm(dxb, wm["w2"], mode="nt", act="drelu2", z=s["z"], out_dtype=BF16, name="mlp_down_bwd",
             deps=deps)
    G["w2"] = _mm(s["u"], dxb, mode="tn", out_dtype=BF16, name="mlp_w2_grad")
    G["w1"] = _mm(s["h2"], dz, mode="tn", out_dtype=BF16, name="mlp_w1_grad")
    dh2 = _mm(dz, wm["w1"], mode="nt", name="mlp_up_bwd")
    dx, dxb, G["mlp_norm"] = _rmsnorm_bwd(s["x_mid"], P["mlp_norm"][i:i + 1], dh2, dx, name="mlp_norm_bwd")
    h = s["h"]
    w = P[("sb", "fox", "mla")[kind]][j]
    do = _mm(dxb, w["w_out"], mode="nt", out_dtype=BF16, name="mixer_out_bwd",
             deps=hook(dxb, G) if hook is not None else ())
    G["w_out"] = _mm(s["o"], dxb, mode="tn", out_dtype=BF16, name="mixer_out_grad")
    if kind == 0:
        dq, dk, dv = _sb_bwd(s["qkv"], do, s["ctot"], H)
        dproj = jnp.concatenate([dq, dk, dv], axis=1)
        G["w_in"] = _mm(h, dproj, mode="tn", out_dtype=BF16, name="sb_qkv_grad",
                        deps=hook2(dv) if hook2 is not None else ())
        dh = _mm(dproj, w["w_in"], mode="nt", name="sb_qkv_bwd")
    elif kind == 1:
        dqn, dkn, dv, dcc, dcr = _softmax_bwd(
            s["qn"], s["kn"], s["vb"], s["cf"], do, s["of"], s["lse"], H, HEAD_DIM,
            1.0 / math.sqrt(HEAD_DIM), name="fox_attn_bwd")
        dq, dk, dvb, G["gq"], G["gk"] = _fox_prep_bwd(s["qkv"], w["gq"], w["gk"], dqn, dkn, dv, H)
        dcf = (dcc[:, :, 0] + dcr[:, 0, :]).T
        dcf = jnp.pad(dcf, ((0, 0), (0, LANE - H)))
        dflog, G["bf"] = _fox_gate_bwd(s["flog"], w["bf"], dcf)
        dproj = jnp.concatenate([dq, dk, dvb], axis=1)
        G["w_qkv"] = _mm(h, dproj, mode="tn", out_dtype=BF16, name="fox_qkv_grad",
                         deps=hook2(dv) if hook2 is not None else ())
        G["w_f"] = _mm(h, dflog, mode="tn", out_dtype=BF16, name="fox_gate_grad")
        dh = _mm(dproj, w["w_qkv"], mode="nt", name="fox_qkv_bwd")
        dh = _mm(dflog, w["w_f"], mode="nt", res=dh, name="fox_gate_bwd_proj")
    else:
        dqf, dkf, dv = _softmax_bwd(
            s["qf"], s["kf"], s["v"], None, do, s["of"], s["lse"], H, MLA_QK_PAD,
            1.0 / math.sqrt(MLA_QK), name="mla_attn_bwd")
        dqfull, dkv, dkr, G["gq"], G["gk"] = _mla_prep_bwd(
            s["qfull"], s["kv"], s["kr"], tabs, w["gq"], w["gk"], dqf, dkf, dv, H)
        G["w_uq"] = _mm(s["cq"], dqfull, mode="tn", out_dtype=BF16, name="mla_up_q_grad",
                        deps=hook2(dv) if hook2 is not None else ())
        G["w_ukv"] = _mm(s["ckv"], dkv, mode="tn", out_dtype=BF16, name="mla_up_kv_grad")
        dcq = _mm(dqfull, w["w_uq"], mode="nt", name="mla_up_q_bwd")
        dckv = _mm(dkv, w["w_ukv"], mode="nt", name="mla_up_kv_bwd")
        ddq, G["q_norm"] = _rmsnorm_bwd(s["dq"], w["q_norm"], dcq, want_f32=False, name="mla_q_norm_bwd")
        ddkv, G["kv_norm"] = _rmsnorm_bwd(s["dkv"], w["kv_norm"], dckv, want_f32=False, name="mla_kv_norm_bwd")
        G["w_dq"] = _mm(h, ddq, mode="tn", out_dtype=BF16, name="mla_down_q_grad")
        G["w_dkv"] = _mm(h, ddkv, mode="tn", out_dtype=BF16, name="mla_down_kv_grad")
        G["w_dr"] = _mm(h, dkr, mode="tn", out_dtype=BF16, name="mla_down_rope_grad")
        dh = _mm(ddq, w["w_dq"], mode="nt", name="mla_down_q_bwd")
        dh = _mm(ddkv, w["w_dkv"], mode="nt", res=dh, name="mla_down_kv_bwd")
        dh = _mm(dkr, w["w_dr"], mode="nt", res=dh, name="mla_down_rope_bwd")
    dx, dxb, G["mix_norm"] = _rmsnorm_bwd(s["x_in"], P["mix_norm"][i:i + 1], dh, dx, name="mix_norm_bwd")
    return dx, dxb, G


def _local_step(x, positions, loss_target, P, depth, H):
    tabs = _rope_tables(positions)
    saved = []
    for i in range(depth):
        x, s = _layer_fwd(i, x, P, tabs, H)
        saved.append(s)
    dx, dxb, loss = _loss_head(x, loss_target)
    grads = [None] * depth
    for i in reversed(range(depth)):
        dx, dxb, grads[i] = _layer_bwd(i, dx, dxb, saved[i], P, tabs, H)
    return loss, dx, grads


def _pad_cols(a, n):
    return jnp.pad(a, ((0, 0), (0, n - a.shape[1])))


def _prepare_layer(P, full, i, H, parts=("mix", "mlp")):
    W = H * HEAD_DIM
    kind, j = i % 3, i // 3
    if "mlp" in parts:
        P["mlp"][i] = {"w1": full["mlp_w1"][i], "w2": full["mlp_w2"][i]}
    if "mix" not in parts:
        return
    if kind == 0:
        P["sb"][j] = {"w_in": full["sb_w_in"][j], "w_out": full["sb_w_out"][j]}
    elif kind == 1:
        w = full["fox_w_in"][j]
        P["fox"][j] = {
            "w_qkv": w[:, :3 * W], "w_f": _pad_cols(w[:, 3 * W:], LANE),
            "bf": _pad_cols(full["fox_b_f"][j:j + 1], LANE),
            "gq": full["fox_q_gain"][j:j + 1], "gk": full["fox_k_gain"][j:j + 1],
            "w_out": full["fox_w_out"][j]}
    else:
        w = full["mla_w_in"][j]
        q_norm, kv_norm = full["mla_q_norm"][j], full["mla_kv_norm"][j]
        rq, rkv = q_norm.shape[0], kv_norm.shape[0]
        w_uq = full["mla_w_uq"][j].reshape(rq, H, MLA_QK)
        w_uq = jnp.pad(w_uq, ((0, 0), (0, 0), (0, MLA_QK_PAD - MLA_QK))).reshape(rq, H * MLA_QK_PAD)
        P["mla"][j] = {
            "w_dq": w[:, :rq], "w_dkv": w[:, rq:rq + rkv], "w_dr": _pad_cols(w[:, rq + rkv:], LANE),
            "q_norm": q_norm[None], "kv_norm": kv_norm[None],
            "w_uq": w_uq, "w_ukv": full["mla_w_ukv"][j],
            "gq": _pad_cols(full["mla_q_gain"][j:j + 1], MLA_QK_PAD),
            "gk": _pad_cols(full["mla_k_gain"][j:j + 1], MLA_QK_PAD),
            "w_out": full["mla_w_out"][j]}


def _prepare(full, H):
    depth = len(full["mlp_w1"])
    P = {"mix_norm": full["mix_norm"], "mlp_norm": full["mlp_norm"], "mlp": [None] * depth,
         "sb": [None] * len(full["sb_w_in"]), "fox": [None] * len(full["fox_w_in"]),
         "mla": [None] * len(full["mla_w_in"])}
    for i in range(depth):
        _prepare_layer(P, full, i, H)
    return P


def _layer_grads(i, G, H):
    kind = i % 3
    out = {"mix_norm": G["mix_norm"], "mlp_norm": G["mlp_norm"], "mlp_w1": G["w1"], "mlp_w2": G["w2"]}
    if kind == 0:
        out.update(sb_w_in=G["w_in"], sb_w_out=G["w_out"])
    elif kind == 1:
        out.update(fox_w_in=jnp.concatenate([G["w_qkv"], G["w_f"][:, :H]], axis=1), fox_b_f=G["bf"][:, :H],
                   fox_q_gain=G["gq"], fox_k_gain=G["gk"], fox_w_out=G["w_out"])
    else:
        rq = G["w_uq"].shape[0]
        out.update(
            mla_w_in=jnp.concatenate([G["w_dq"], G["w_dkv"], G["w_dr"][:, :MLA_ROPE]], axis=1),
            mla_q_norm=G["q_norm"], mla_kv_norm=G["kv_norm"],
            mla_w_uq=G["w_uq"].reshape(rq, H, MLA_QK_PAD)[:, :, :MLA_QK].reshape(rq, H * MLA_QK),
            mla_w_ukv=G["w_ukv"], mla_q_gain=G["gq"][:, :MLA_QK], mla_k_gain=G["gk"][:, :MLA_QK],
            mla_w_out=G["w_out"])
    return out


def _layer_of(name, j):
    if name.startswith("sb_"):
        return 3 * j
    if name.startswith("fox_"):
        return 3 * j + 1
    if name.startswith("mla_"):
        return 3 * j + 2
    return j


def _unprepare(grads, H):
    out = {}
    for i, G in enumerate(grads):
        for n, g in _layer_grads(i, G, H).items():
            out.setdefault(n, []).append(g)
    return out


class _Place:
    def __init__(self, x, y, c):
        self.x, self.y, self.c = x, y, c
        self.dev = 4 * x + 2 * y + c
        self.chip = 2 * x + y
        self.id = (x, y, c)


def _peers(me, kind):
    if kind == "ici":
        return [_Place(1 - me.x, me.y, me.c), _Place(me.x, 1 - me.y, me.c), _Place(1 - me.x, 1 - me.y, me.c)]
    return [_Place(me.x, me.y, 1 - me.c)]


def _exchange(name, kind, operands, out_shapes, aliases, n_remote, n_local, plan):
    n_in, n_out = len(operands), len(out_shapes)

    def body(*refs):
        in_refs, out_refs = refs[:n_in], refs[n_in:n_in + n_out]
        send_sems, recv_sems, local_sems = refs[n_in + n_out:]
        me = _Place(lax.axis_index("x"), lax.axis_index("y"), lax.axis_index("c"))
        peers = _peers(me, kind)
        remote, local = plan(me, peers, in_refs, out_refs)
        assert len(remote) == n_remote and len(local) == n_local
        sends = []
        for n, (src, dst, k, _) in enumerate(remote):
            cp = pltpu.make_async_remote_copy(
                src_ref=src, dst_ref=dst, send_sem=send_sems.at[n], recv_sem=recv_sems.at[n],
                device_id=peers[k].id, device_id_type=MESH)
            cp.start()
            sends.append(cp)
        own = []
        for n, (src, dst) in enumerate(local):
            cp = pltpu.make_async_copy(src, dst, local_sems.at[n])
            cp.start()
            own.append(cp)
        for n, (src, _, k, landing) in enumerate(remote):
            pltpu.make_async_remote_copy(
                src_ref=src, dst_ref=landing, send_sem=send_sems.at[n], recv_sem=recv_sems.at[n],
                device_id=peers[k].id, device_id_type=MESH).wait_recv()
        for cp in sends:
            cp.wait_send()
        for cp in own:
            cp.wait()

    outs = pl.pallas_call(
        body, name=name, in_specs=[ANY] * n_in, out_specs=tuple([ANY] * n_out),
        out_shape=tuple(out_shapes), input_output_aliases=aliases,
        scratch_shapes=[pltpu.SemaphoreType.DMA((n_remote,)), pltpu.SemaphoreType.DMA((n_remote,)),
                        pltpu.SemaphoreType.DMA((max(n_local, 1),))],
    )(*operands)
    return list(outs)


def _window(ref, kind, d, shard_shape):
    r, c = shard_shape
    if kind == "col":
        return ref.at[:, pl.ds(pl.multiple_of(d * c, c), c)]
    return ref.at[pl.ds(pl.multiple_of(d * r, r), r), :]


def _full_shape(kind, shard_shape):
    r, c = shard_shape
    return (r, N_DEV * c) if kind == "col" else (N_DEV * r, c)


HBM = pl.BlockSpec(memory_space=pltpu.HBM)
SEM = pl.BlockSpec(memory_space=pltpu.SEMAPHORE)
EFFECT = pltpu.SideEffectType.DATAFLOW_SIDE_EFFECTING


class _Flight:
    def __init__(self, name, kind, n_remote, plan, send_sems, recv_sems, bufs, token):
        self.name, self.kind, self.n_remote, self.plan = name, kind, n_remote, plan
        self.send_sems, self.recv_sems, self.bufs, self.token = send_sems, recv_sems, bufs, token


def _start_copies(name, kind, bufs, n_remote, plan, after=None):
    nb = len(bufs)
    n_after = 0 if after is None else 1

    def body(*refs):
        in_refs = refs[:nb]
        send_sems, recv_sems = refs[nb + n_after], refs[nb + n_after + 1]
        token = refs[2 * nb + n_after + 2]
        me = _Place(lax.axis_index("x"), lax.axis_index("y"), lax.axis_index("c"))
        peers = _peers(me, kind)
        for n, (src, dst, k, _) in enumerate(plan(me, peers, in_refs)):
            pltpu.make_async_remote_copy(
                src_ref=src, dst_ref=dst, send_sem=send_sems.at[n], recv_sem=recv_sems.at[n],
                device_id=peers[k].id, device_id_type=MESH).start()
        token[...] = jnp.zeros_like(token)

    outs = pl.pallas_call(
        body, name=name, in_specs=[HBM] * nb + [ANY] * n_after,
        out_specs=(SEM, SEM) + (HBM,) * nb + (pl.BlockSpec(memory_space=pltpu.VMEM),),
        out_shape=(pltpu.SemaphoreType.DMA((n_remote,)), pltpu.SemaphoreType.DMA((n_remote,)))
        + tuple(pltpu.HBM(b.shape, b.dtype) for b in bufs) + (jax.ShapeDtypeStruct((8, LANE), F32),),
        input_output_aliases={a: 2 + a for a in range(nb)},
        compiler_params=pltpu.CompilerParams(has_side_effects=EFFECT),
    )(*[pltpu.with_memory_space_constraint(b, pltpu.HBM) for b in bufs], *([after] if n_after else []))
    return _Flight(name, kind, n_remote, plan, outs[0], outs[1], list(outs[2:2 + nb]), outs[2 + nb])


def _wait_copies(flight, after):
    nb = len(flight.bufs)
    plan, kind = flight.plan, flight.kind
    after = list(after) if isinstance(after, (tuple, list)) else [after]

    def body(*refs):
        in_refs = refs[:nb]
        send_sems, recv_sems = refs[nb], refs[nb + 1]
        me = _Place(lax.axis_index("x"), lax.axis_index("y"), lax.axis_index("c"))
        peers = _peers(me, kind)
        for n, (src, _, k, landing) in enumerate(plan(me, peers, in_refs)):
            cp = pltpu.make_async_remote_copy(
                src_ref=src, dst_ref=landing, send_sem=send_sems.at[n], recv_sem=recv_sems.at[n],
                device_id=peers[k].id, device_id_type=MESH)
            cp.wait_send()
            cp.wait_recv()

    outs = pl.pallas_call(
        body, name=flight.name.replace("_start", "_wait"),
        in_specs=[HBM] * nb + [SEM, SEM] + [ANY] * len(after), out_specs=(HBM,) * nb,
        out_shape=tuple(pltpu.HBM(b.shape, b.dtype) for b in flight.bufs),
        input_output_aliases={a: a for a in range(nb)},
        compiler_params=pltpu.CompilerParams(has_side_effects=EFFECT),
    )(*flight.bufs, flight.send_sems, flight.recv_sems, *after)
    return list(outs)


def _own_in_place(shard, kind, dev, dtype, deps=()):
    r, c = shard.shape
    tr = _row_tile(r, c)
    nb = r // tr

    def body(dev_ref, s_ref, *rest):
        o_ref = rest[-1]
        o_ref[...] = s_ref[...].astype(dtype)

    if kind == "col":
        o_spec = pl.BlockSpec((tr, c), lambda i, dev_ref: (i, dev_ref[0]))
    else:
        o_spec = pl.BlockSpec((tr, c), lambda i, dev_ref: (dev_ref[0] * nb + i, 0))
    return pl.pallas_call(
        body, name="own_in_place",
        grid_spec=pltpu.PrefetchScalarGridSpec(
            num_scalar_prefetch=1, grid=(nb,),
            in_specs=[pl.BlockSpec((tr, c), lambda i, dev_ref: (i, 0))] + [ANY] * len(deps), out_specs=o_spec),
        out_shape=jax.ShapeDtypeStruct(_full_shape(kind, shard.shape), dtype),
        compiler_params=_params(("parallel",)),
    )(dev, shard, *deps)


def _gather_ici_start(fulls, kinds, shapes, tag, after=None):
    n = len(fulls)

    def plan(me, peers, refs):
        remote = []
        for a in range(n):
            mine = _window(refs[a], kinds[a], me.dev, shapes[a])
            for k, p in enumerate(peers):
                remote.append((mine, mine, k, _window(refs[a], kinds[a], p.dev, shapes[a])))
        return remote

    return _start_copies("gather_ici_start_" + tag, "ici", list(fulls), 3 * n, plan, after)


def _gather_d2d_start(fulls, kinds, shapes, tag):
    n = len(fulls)

    def plan(me, peers, refs):
        remote = []
        for a in range(n):
            for ch in range(N_CHIP):
                held = _window(refs[a], kinds[a], 2 * ch + me.c, shapes[a])
                remote.append((held, held, 0, _window(refs[a], kinds[a], 2 * ch + 1 - me.c, shapes[a])))
        return remote

    return _start_copies("gather_d2d_start_" + tag, "d2d", fulls, N_CHIP * n, plan)


def _scatter_d2d_start(grads, kinds, shapes, tag):
    n = len(grads)
    lands = [lax.empty((N_CHIP,) + tuple(s), g.dtype) for g, s in zip(grads, shapes)]

    def plan(me, peers, refs):
        remote = []
        for a in range(n):
            for ch in range(N_CHIP):
                remote.append((_window(refs[a], kinds[a], 2 * ch + 1 - me.c, shapes[a]),
                               refs[n + a].at[ch], 0, refs[n + a].at[ch]))
        return remote

    return _start_copies("scatter_d2d_start_" + tag, "d2d", list(grads) + lands, N_CHIP * n, plan)


def _scatter_ici_start(sums, shapes, tag, after=None):
    n = len(sums)
    lands = [lax.empty((N_CHIP - 1,) + tuple(s), BF16) for s in shapes]

    def plan(me, peers, refs):
        remote = []
        for a in range(n):
            for k, p in enumerate(peers):
                remote.append((refs[a].at[p.chip], refs[n + a].at[k], k, refs[n + a].at[k]))
        return remote

    return _start_copies("scatter_ici_start_" + tag, "ici", list(sums) + lands, 3 * n, plan, after)


def _pair_add(g, r4, kind, shard_shape, core, *, name):
    r, c = shard_shape
    tr = _row_tile(r, c, 524288)
    nb = r // tr

    def body(core_ref, g_ref, r_ref, o_ref):
        o_ref[...] = (g_ref[...].astype(F32) + r_ref[...].astype(F32)).astype(BF16)

    if kind == "col":
        g_spec = pl.BlockSpec((tr, c), lambda ch, i, core_ref: (i, 2 * ch + core_ref[0]))
    else:
        g_spec = pl.BlockSpec((tr, c), lambda ch, i, core_ref: ((2 * ch + core_ref[0]) * nb + i, 0))
    slot = pl.BlockSpec((None, tr, c), lambda ch, i, core_ref: (ch, i, 0))
    return pl.pallas_call(
        body, name=name,
        grid_spec=pltpu.PrefetchScalarGridSpec(
            num_scalar_prefetch=1, grid=(N_CHIP, nb), in_specs=[g_spec, slot], out_specs=slot),
        out_shape=jax.ShapeDtypeStruct((N_CHIP, r, c), BF16),
        compiler_params=_params(("parallel", "parallel")),
    )(core, g, r4)


def _allreduce_small_start(part, chip):
    def plan_swap(me, peers, ins, outs):
        return [(ins[0], outs[0], 0, outs[0])], []

    (got,) = _exchange("allreduce_small_swap", "d2d", [part], [jax.ShapeDtypeStruct(part.shape, F32)],
                       {}, 1, 0, plan_swap)

    def body(chip_ref, a_ref, b_ref, o_ref):
        o_ref[...] = a_ref[...] + b_ref[...]

    whole = pl.BlockSpec(part.shape, lambda i, chip_ref: (0, 0))
    slots = pl.pallas_call(
        body, name="allreduce_small_pair",
        grid_spec=pltpu.PrefetchScalarGridSpec(
            num_scalar_prefetch=1, grid=(1,), in_specs=[whole, whole],
            out_specs=pl.BlockSpec((None,) + part.shape, lambda i, chip_ref: (chip_ref[0], 0, 0))),
        out_shape=jax.ShapeDtypeStruct((N_CHIP,) + part.shape, F32),
    )(chip, part, got)

    def plan(me, peers, refs):
        mine = refs[0].at[me.chip]
        return [(mine, mine, k, refs[0].at[p.chip]) for k, p in enumerate(peers)]

    return _start_copies("allreduce_small_start", "ici", [slots], 3, plan)


def _row_tile(r, c, limit=262144):
    best = None
    for t in range(8, r + 1, 8):
        if r % t == 0 and t * c <= limit:
            best = t
    return best if best is not None else r


def _adam_math(w, g, m, v):
    m = ADAM_B1 * m + (1.0 - ADAM_B1) * g
    v = ADAM_B2 * v + (1.0 - ADAM_B2) * (g * g)
    m_hat = m / (1.0 - ADAM_B1 ** ADAM_STEP)
    v_hat = v / (1.0 - ADAM_B2 ** ADAM_STEP)
    delta = -ADAM_LR * (m_hat / (jnp.sqrt(v_hat) + ADAM_EPS) + ADAM_WD * w)
    return delta, m, v


def _adamw(w, m, v, sums, lands, chip, layer, prev, *, name):
    L, r, c = w.shape
    tr = _row_tile(r, c)
    has_prev = prev is not None

    def body(chip_ref, *refs):
        w_ref, m_ref, v_ref, s_ref, l_ref = refs[:5]
        g_ref, d_ref, nm_ref, nv_ref = refs[-4:]
        g = (s_ref[...].astype(F32) + l_ref[0].astype(F32)) + l_ref[1].astype(F32) + l_ref[2].astype(F32)
        delta, nm, nv = _adam_math(w_ref[...], g, m_ref[...], v_ref[...])
        g_ref[...] = g
        d_ref[...] = delta
        nm_ref[...] = nm
        nv_ref[...] = nv

    slab = pl.BlockSpec((None, tr, c), lambda i, chip_ref: (layer, i, 0))
    in_specs = [slab, slab, slab, pl.BlockSpec((None, tr, c), lambda i, chip_ref: (chip_ref[0], i, 0)),
                pl.BlockSpec((N_CHIP - 1, tr, c), lambda i, chip_ref: (0, i, 0))]
    args = [chip, w, m, v, sums, lands]
    aliases = {}
    if has_prev:
        in_specs += [ANY] * 4
        args += list(prev)
        aliases = {6 + n: n for n in range(4)}
    return pl.pallas_call(
        body, name=name,
        grid_spec=pltpu.PrefetchScalarGridSpec(
            num_scalar_prefetch=1, grid=(r // tr,), in_specs=in_specs, out_specs=(slab,) * 4),
        out_shape=tuple(jax.ShapeDtypeStruct((L, r, c), F32) for _ in range(4)),
        input_output_aliases=aliases, compiler_params=_params(("parallel",)),
    )(*args)


def _adamw_small(w, g, m, v):
    def body(w_ref, g_ref, m_ref, v_ref, d_ref, nm_ref, nv_ref):
        d_ref[...], nm_ref[...], nv_ref[...] = _adam_math(w_ref[...], g_ref[...], m_ref[...], v_ref[...])

    return pl.pallas_call(
        body, name="adamw_small", out_shape=tuple(jax.ShapeDtypeStruct(w.shape, F32) for _ in range(3)),
    )(w, g, m, v)


def _sum4(parts):
    def body(p_ref, o_ref):
        o_ref[...] = (p_ref[0] + p_ref[1]) + p_ref[2] + p_ref[3]

    return pl.pallas_call(
        body, name="sum_chips", out_shape=jax.ShapeDtypeStruct(parts.shape[1:], F32))(parts)


_WEIGHTS = ["mix_norm", "mlp_norm", "sb_w_in", "sb_w_out", "fox_w_in", "fox_b_f", "fox_q_gain", "fox_k_gain",
            "fox_w_out", "mla_w_in", "mla_q_norm", "mla_kv_norm", "mla_w_uq", "mla_w_ukv", "mla_q_gain",
            "mla_k_gain", "mla_w_out", "mlp_w1", "mlp_w2"]
_BIG = {"sb_w_in": "col", "sb_w_out": "row", "fox_w_in": "row", "fox_w_out": "row", "mla_w_in": "row",
        "mla_w_uq": "col", "mla_w_ukv": "col", "mla_w_out": "row", "mlp_w1": "col", "mlp_w2": "row"}


def _layer_big(i):
    kind, j = i % 3, i // 3
    mixer = {0: ["sb_w_in", "sb_w_out"], 1: ["fox_w_in", "fox_w_out"],
             2: ["mla_w_in", "mla_w_uq", "mla_w_ukv", "mla_w_out"]}[kind]
    return [(n, j) for n in mixer] + [("mlp_w1", i), ("mlp_w2", i)]


def _stack_to_cols(a):
    r = a.shape[0] // N_DEV
    return a.reshape(N_DEV, r, a.shape[1]).transpose(1, 0, 2).reshape(r, N_DEV * a.shape[1])


def _cols_to_stack(a):
    c = a.shape[1] // N_DEV
    return a.reshape(a.shape[0], N_DEV, c).transpose(1, 0, 2).reshape(N_DEV * a.shape[0], c)


def _pack_rows(rows, width):
    rows = [jnp.pad(r.reshape(-1).astype(F32), (0, width - r.size)) for r in rows]
    pad = (-len(rows)) % 8
    rows += [jnp.zeros((width,), F32)] * pad
    return jnp.stack(rows)


def kernel(x, positions, mix_norm, mlp_norm, sb_w_in, sb_w_out, fox_w_in, fox_b_f, fox_q_gain, fox_k_gain, fox_w_out, mla_w_in, mla_q_norm, mla_kv_norm, mla_w_uq, mla_w_ukv, mla_q_gain, mla_k_gain, mla_w_out, mlp_w1, mlp_w2, loss_target, m_mix_norm, m_mlp_norm, m_sb_w_in, m_sb_w_out, m_fox_w_in, m_fox_b_f, m_fox_q_gain, m_fox_k_gain, m_fox_w_out, m_mla_w_in, m_mla_q_norm, m_mla_kv_norm, m_mla_w_uq, m_mla_w_ukv, m_mla_q_gain, m_mla_k_gain, m_mla_w_out, m_mlp_w1, m_mlp_w2, v_mix_norm, v_mlp_norm, v_sb_w_in, v_sb_w_out, v_fox_w_in, v_fox_b_f, v_fox_q_gain, v_fox_k_gain, v_fox_w_out, v_mla_w_in, v_mla_q_norm, v_mla_kv_norm, v_mla_w_uq, v_mla_w_ukv, v_mla_q_gain, v_mla_k_gain, v_mla_w_out, v_mlp_w1, v_mlp_w2):
    w_in = dict(zip(_WEIGHTS, (mix_norm, mlp_norm, sb_w_in, sb_w_out, fox_w_in, fox_b_f, fox_q_gain, fox_k_gain, fox_w_out, mla_w_in, mla_q_norm, mla_kv_norm, mla_w_uq, mla_w_ukv, mla_q_gain, mla_k_gain, mla_w_out, mlp_w1, mlp_w2)))
    m_in = dict(zip(_WEIGHTS, (m_mix_norm, m_mlp_norm, m_sb_w_in, m_sb_w_out, m_fox_w_in, m_fox_b_f, m_fox_q_gain, m_fox_k_gain, m_fox_w_out, m_mla_w_in, m_mla_q_norm, m_mla_kv_norm, m_mla_w_uq, m_mla_w_ukv, m_mla_q_gain, m_mla_k_gain, m_mla_w_out, m_mlp_w1, m_mlp_w2)))
    v_in = dict(zip(_WEIGHTS, (v_mix_norm, v_mlp_norm, v_sb_w_in, v_sb_w_out, v_fox_w_in, v_fox_b_f, v_fox_q_gain, v_fox_k_gain, v_fox_w_out, v_mla_w_in, v_mla_q_norm, v_mla_kv_norm, v_mla_w_uq, v_mla_w_ukv, v_mla_q_gain, v_mla_k_gain, v_mla_w_out, v_mlp_w1, v_mlp_w2)))
    depth, D = mix_norm.shape
    H = D // HEAD_DIM
    n_mla = mla_w_in.shape[0]
    dev = 4 * lax.axis_index("x") + 2 * lax.axis_index("y") + lax.axis_index("c")
    core = lax.axis_index("c").astype(jnp.int32).reshape(1)
    dev_arr = dev.astype(jnp.int32).reshape(1)

    chip = (2 * lax.axis_index("x") + lax.axis_index("y")).astype(jnp.int32).reshape(1)
    nq, nkv = mla_q_norm.shape[1], mla_kv_norm.shape[1]

    units, u_mix, u_mlp = [], {}, {}
    for i in range(depth):
        big = _layer_big(i)
        u_mix[i] = len(units)
        if i < SPLIT_LAYERS:
            units += [(i, ("mix",), big[:-2]), (i, ("mlp",), big[-2:])]
        else:
            units.append((i, ("mix", "mlp"), big))
        u_mlp[i] = len(units) - 1

    def split(i):
        return u_mix[i] != u_mlp[i]

    def tag_of(u):
        i, parts, _ = units[u]
        return f"l{i}" if len(parts) == 2 else f"l{i}_{parts[0]}"

    kinds_of, shapes_of, ici = [], [], []
    for i, parts, names in units:
        shards = [w_in[n][j] for n, j in names]
        kinds = [_BIG[n] for n, _ in names]
        behind = (ici[-1].token,) if ici else ()
        fulls = [_own_in_place(s, k, dev_arr, BF16, behind) for s, k in zip(shards, kinds)]
        if i % 3 == 2:
            shards.append(_pack_rows([mla_q_norm[i // 3], mla_kv_norm[i // 3]], LANE))
            kinds.append("row")
            fulls.append(_own_in_place(shards[-1], "row", dev_arr, F32, behind))
        kinds_of.append(kinds)
        shapes_of.append([s.shape for s in shards])
        ici.append(_gather_ici_start(fulls, kinds, shapes_of[-1], tag_of(len(ici)),
                                     ici[-1].token if ici else None))

    full = {n: [None] * w_in[n].shape[0] for n in _WEIGHTS}
    for n in ("mix_norm", "mlp_norm", "fox_b_f", "fox_q_gain", "fox_k_gain", "mla_q_gain", "mla_k_gain"):
        full[n] = w_in[n]
    P = {"mix_norm": mix_norm, "mlp_norm": mlp_norm, "mlp": [None] * depth,
         "sb": [None] * sb_w_in.shape[0], "fox": [None] * fox_w_in.shape[0], "mla": [None] * n_mla}

    def forward_to_sibling(u, after):
        arrived = _wait_copies(ici[u], after)
        return _gather_d2d_start(arrived, kinds_of[u], shapes_of[u], tag_of(u))

    def finish_gather(u, flight, after):
        i, parts, names = units[u]
        got = _wait_copies(flight, after)
        for (n, j), a in zip(names, got):
            full[n][j] = _stack_to_cols(a) if n == "fox_w_in" else a
        if i % 3 == 2:
            tiles = got[-1].reshape(N_DEV, 8, LANE)
            full["mla_q_norm"][i // 3] = tiles[:, 0, :nq].reshape(-1)
            full["mla_kv_norm"][i // 3] = tiles[:, 1, :nkv].reshape(-1)
        _prepare_layer(P, full, i, H, parts)

    tabs = _rope_tables(positions[0])
    d2d = forward_to_sibling(0, ici[-1].token)
    finish_gather(0, d2d, d2d.token)
    xs, saved = x[0], []
    for i in range(depth):
        nxt = {}

        def hook(x_mid, i=i, nxt=nxt):
            if split(i):
                mlp_d2d = forward_to_sibling(u_mlp[i], x_mid)
                finish_gather(u_mlp[i], mlp_d2d, mlp_d2d.token)
            if 1 <= i < depth - 1 and not split(i + 1):
                nxt["d2d"] = forward_to_sibling(u_mix[i + 1], x_mid)
                return (nxt["d2d"].token,)
            return ()

        xs, s = _layer_fwd(i, xs, P, tabs, H, hook)
        saved.append(s)
        if i + 1 < depth:
            if "d2d" in nxt:
                finish_gather(u_mix[i + 1], nxt["d2d"], xs)
            else:
                mix_d2d = forward_to_sibling(u_mix[i + 1], xs)
                finish_gather(u_mix[i + 1], mix_d2d, mix_d2d.token)
    dx, dxb, loss_part = _loss_head(xs, loss_target[0])

    def unit_meta(u):
        names = units[u][2]
        return [_BIG[n] for n, _ in names], [w_in[n].shape[1:] for n, _ in names]

    def to_sibling_start(u, Gl):
        grads = [(_cols_to_stack(Gl[n]) if n == "fox_w_in" else Gl[n]) for n, _ in units[u][2]]
        return _scatter_d2d_start(grads, *unit_meta(u), tag_of(u))

    def pair_and_send(u, flight, after, before=None):
        n = len(units[u][2])
        kinds, shapes = unit_meta(u)
        got = _wait_copies(flight, after)
        sums = [_pair_add(g, r4, k, s, core, name="pair_add_" + tag_of(u))
                for g, r4, k, s in zip(got[:n], got[n:], kinds, shapes)]
        return _scatter_ici_start(sums, shapes, tag_of(u), before)

    G, to_chips, deps, flying = [None] * depth, [None] * len(units), (), {}
    for i in reversed(range(depth)):
        def hook(dxb_mid, Gp, i=i):
            out = ()
            if i + 1 < depth:
                u = u_mix[i + 1]
                to_chips[u] = pair_and_send(u, flying.pop(u), dxb_mid)
                out = (to_chips[u].token,)
            if split(i):
                flying[u_mlp[i]] = to_sibling_start(u_mlp[i], {"mlp_w1": Gp["w1"], "mlp_w2": Gp["w2"]})
                out = out + (flying[u_mlp[i]].token,)
            return out

        def hook2(dv, i=i):
            if not split(i):
                return ()
            u = u_mlp[i]
            to_chips[u] = pair_and_send(u, flying.pop(u), dv)
            return (to_chips[u].token,)

        dx, dxb, Gp = _layer_bwd(i, dx, dxb, saved[i], P, tabs, H, deps, hook, hook2)
        G[i] = _layer_grads(i, Gp, H)
        flying[u_mix[i]] = to_sibling_start(u_mix[i], G[i])
        deps = (flying[u_mix[i]].token,)
    grad_x = dx
    to_chips[0] = pair_and_send(0, flying.pop(0), dx)

    small = [n for n in _WEIGHTS if n not in _BIG]
    rows = []
    for n in small:
        for j in range(w_in[n].shape[0]):
            rows.append(G[_layer_of(n, j)][n])
    loss_row = len(rows)
    rows.append(loss_part)
    small_flight = _allreduce_small_start(_pack_rows(rows, D) + to_chips[0].token[0, 0], chip)

    results, after = {}, [small_flight.token]
    for u in reversed(range(len(units))):
        names = units[u][2]
        got = _wait_copies(to_chips[u], after)
        for a, (n, j) in enumerate(names):
            results[n] = _adamw(w_in[n], m_in[n], v_in[n], got[a], got[len(names) + a], chip, j, results.get(n),
                                name="adamw_" + n)
        after = [results[n][0] for n, _ in names]

    (parts,) = _wait_copies(small_flight, after)
    sgr = _sum4(parts)
    g_rows, w_rows, m_rows, v_rows, at = [], [], [], [], 0
    for n in small:
        for j in range(w_in[n].shape[0]):
            width = w_in[n].shape[1]
            if n in ("mla_q_norm", "mla_kv_norm"):
                g_rows.append(lax.dynamic_slice(sgr[at], (dev * width,), (width,)))
            else:
                g_rows.append(sgr[at, :width])
            w_rows.append(w_in[n][j])
            m_rows.append(m_in[n][j])
            v_rows.append(v_in[n][j])
            at += 1
    g_pack = _pack_rows(g_rows, D)
    d_pack, nm_pack, nv_pack = _adamw_small(
        _pack_rows(w_rows, D), g_pack, _pack_rows(m_rows, D), _pack_rows(v_rows, D))
    at = 0
    for n in small:
        L, width = w_in[n].shape
        results[n] = tuple(p[at:at + L, :width] for p in (g_pack, d_pack, nm_pack, nv_pack))
        at += L

    out = [sgr[loss_row, 0], grad_x[None]]
    for part in range(4):
        out += [results[n][part] for n in _WEIGHTS]
    return tuple(out)
```

```python
import functools
import math

import jax
import jax.numpy as jnp
from jax import lax
from jax.experimental import pallas as pl
from jax.experimental.pallas import tpu as pltpu

F32 = jnp.float32
BF16 = jnp.bfloat16

HEAD_DIM = 128
MLA_NOPE = 128
MLA_ROPE = 64
MLA_V = 128
MLA_QK = MLA_NOPE + MLA_ROPE
MLA_QK_PAD = 256
LANE = 128
ROPE_THETA = 10000.0
EPS = 1e-6
ADAM_LR = 0.001
ADAM_B1 = 0.9
ADAM_B2 = 0.999
ADAM_EPS = 1e-08
ADAM_WD = 0.01
ADAM_STEP = 10
N_DEV = 8
N_CHIP = 4
NEG = -1e30
VMEM_LIMIT = 56 * 1024 * 1024
ATT_BLOCK = 256
SB_BLOCK = 256
ATT_HEADS = 4
ATT_HEADS_FWD = 8
SPLIT_LAYERS = 2
SB_PARTS = 2
MESH = pl.DeviceIdType.MESH
ANY = pl.BlockSpec(memory_space=pl.ANY)


def _params(sem):
    return pltpu.CompilerParams(dimension_semantics=sem, vmem_limit_bytes=VMEM_LIMIT)


def _tile(dim, pref):
    if dim <= pref:
        return dim
    t = pref
    while dim % t:
        t -= LANE
    return t


def _dot(a, b, dims):
    return lax.dot_general(a, b, (dims, ((), ())), preferred_element_type=F32)


NN = ((1,), (0,))
NT = ((1,), (1,))
TN = ((0,), (0,))


def _mm(a, b, *, mode="nn", out_dtype=F32, res=None, act=None, z=None, name,
        tm=1024, tn=1024, tk=2048, deps=()):
    if mode == "nn":
        (M, K), (_, N) = a.shape, b.shape
    elif mode == "nt":
        (M, K), (N, _) = a.shape, b.shape
    else:
        (K, M), (_, N) = a.shape, b.shape
    tm, tn, tk = _tile(M, tm), _tile(N, tn), _tile(K, tk)
    nk = K // tk
    if mode == "tn":
        a_spec = pl.BlockSpec((tk, tm), lambda i, j, k: (k, i))
    else:
        a_spec = pl.BlockSpec((tm, tk), lambda i, j, k: (i, k))
    if mode == "nt":
        b_spec = pl.BlockSpec((tn, tk), lambda i, j, k: (j, k))
    else:
        b_spec = pl.BlockSpec((tk, tn), lambda i, j, k: (k, j))
    dims = {"nn": NN, "nt": NT, "tn": TN}[mode]
    o_spec = pl.BlockSpec((tm, tn), lambda i, j, k: (i, j))
    in_specs, args = [a_spec, b_spec], [a, b]
    if res is not None:
        in_specs.append(o_spec)
        args.append(res)
    if act == "drelu2":
        in_specs.append(o_spec)
        args.append(z)
    if act == "relu2":
        out_shape = (jax.ShapeDtypeStruct((M, N), F32), jax.ShapeDtypeStruct((M, N), BF16))
        out_specs = (o_spec, o_spec)
    else:
        out_shape = jax.ShapeDtypeStruct((M, N), out_dtype)
        out_specs = o_spec
    has_res, has_z = res is not None, act == "drelu2"
    n_out = 2 if act == "relu2" else 1
    in_specs += [ANY] * len(deps)
    args += list(deps)

    def body(*refs):
        a_ref, b_ref = refs[0], refs[1]
        idx = 2
        res_ref = z_ref = None
        if has_res:
            res_ref = refs[idx]
            idx += 1
        if has_z:
            z_ref = refs[idx]
            idx += 1
        idx += len(deps)
        outs = refs[idx:idx + n_out]

        def finish(r):
            if has_res:
                r = r + res_ref[...]
            if act == "relu2":
                outs[0][...] = r
                rr = jnp.maximum(r, 0.0)
                outs[1][...] = (rr * rr).astype(BF16)
            elif act == "drelu2":
                outs[0][...] = (r * (2.0 * jnp.maximum(z_ref[...], 0.0))).astype(out_dtype)
            else:
                outs[0][...] = r.astype(out_dtype)

        if nk == 1:
            finish(_dot(a_ref[...], b_ref[...], dims))
            return
        acc = refs[-1]
        k = pl.program_id(2)

        @pl.when(k == 0)
        def _():
            acc[...] = _dot(a_ref[...], b_ref[...], dims)

        @pl.when((k > 0) & (k < nk - 1))
        def _():
            acc[...] += _dot(a_ref[...], b_ref[...], dims)

        @pl.when(k == nk - 1)
        def _():
            finish(acc[...] + _dot(a_ref[...], b_ref[...], dims))

    return pl.pallas_call(
        body, name=name, grid=(M // tm, N // tn, nk), in_specs=in_specs, out_specs=out_specs,
        out_shape=out_shape, scratch_shapes=[pltpu.VMEM((tm, tn), F32)] if nk > 1 else [],
        compiler_params=_params(("parallel", "parallel", "arbitrary")),
    )(*args)


def _rmsnorm_fwd(x, g, *, name, tm=256):
    T, n = x.shape
    tm = _tile(T, tm)

    def body(x_ref, g_ref, o_ref):
        xf = x_ref[...]
        r = lax.rsqrt(jnp.mean(xf * xf, axis=-1, keepdims=True) + EPS)
        o_ref[...] = (xf * r * g_ref[...]).astype(BF16)

    return pl.pallas_call(
        body, name=name, grid=(T // tm,),
        in_specs=[pl.BlockSpec((tm, n), lambda i: (i, 0)), pl.BlockSpec((1, n), lambda i: (0, 0))],
        out_specs=pl.BlockSpec((tm, n), lambda i: (i, 0)),
        out_shape=jax.ShapeDtypeStruct((T, n), BF16),
        compiler_params=_params(("parallel",)),
    )(x, g)


def _rmsnorm_bwd(x, g, dy, dx_in=None, *, name, want_f32=True, tm=256):
    T, n = x.shape
    tm = _tile(T, tm)
    has_in = dx_in is not None
    row = pl.BlockSpec((tm, n), lambda i: (i, 0))
    vec = pl.BlockSpec((1, n), lambda i: (0, 0))

    def body(*refs):
        x_ref, g_ref, dy_ref = refs[:3]
        idx = 3
        in_ref = None
        if has_in:
            in_ref = refs[idx]
            idx += 1
        outs = refs[idx:]
        xf = x_ref[...]
        r = lax.rsqrt(jnp.mean(xf * xf, axis=-1, keepdims=True) + EPS)
        dyf = dy_ref[...]
        t = dyf * g_ref[...]
        dx = r * t - xf * (r * r * r * jnp.mean(t * xf, axis=-1, keepdims=True))
        if has_in:
            dx = dx + in_ref[...]
        o = 0
        if want_f32:
            outs[0][...] = dx
            o = 1
        outs[o][...] = dx.astype(BF16)
        dg_ref = outs[o + 1]

        @pl.when(pl.program_id(0) == 0)
        def _():
            dg_ref[...] = jnp.zeros_like(dg_ref)

        dg_ref[...] += jnp.sum(dyf * xf * r, axis=0, keepdims=True)

    in_specs = [row, vec, row] + ([row] if has_in else [])
    out_specs, out_shape = [], []
    if want_f32:
        out_specs.append(row)
        out_shape.append(jax.ShapeDtypeStruct((T, n), F32))
    out_specs += [row, vec]
    out_shape += [jax.ShapeDtypeStruct((T, n), BF16), jax.ShapeDtypeStruct((1, n), F32)]
    args = [x, g, dy] + ([dx_in] if has_in else [])
    return pl.pallas_call(
        body, name=name, grid=(T // tm,), in_specs=in_specs, out_specs=tuple(out_specs),
        out_shape=tuple(out_shape), compiler_params=_params(("arbitrary",)),
    )(*args)


def _loss_head(y, target, *, tm=256):
    T, n = y.shape
    tm = _tile(T, tm)
    row = pl.BlockSpec((tm, n), lambda i: (i, 0))

    def body(y_ref, t_ref, dy_ref, dyb_ref, loss_ref):
        err = y_ref[...] - t_ref[...]
        dy = err * (1.0 / n)
        dy_ref[...] = dy
        dyb_ref[...] = dy.astype(BF16)

        @pl.when(pl.program_id(0) == 0)
        def _():
            loss_ref[...] = jnp.zeros_like(loss_ref)

        part = 0.5 * jnp.sum(jnp.mean(err * err, axis=-1, keepdims=True), axis=0, keepdims=True)
        loss_ref[...] += part

    return pl.pallas_call(
        body, name="loss_head", grid=(T // tm,), in_specs=[row, row],
        out_specs=(row, row, pl.BlockSpec((1, 1), lambda i: (0, 0))),
        out_shape=(jax.ShapeDtypeStruct((T, n), F32), jax.ShapeDtypeStruct((T, n), BF16),
                   jax.ShapeDtypeStruct((1, 1), F32)),
        compiler_params=_params(("arbitrary",)),
    )(y, target)


def _split(v, n):
    parts = []
    for _ in range(n - 1):
        p = v.astype(BF16)
        parts.append(p)
        v = v - p.astype(F32)
    parts.append(v.astype(BF16))
    return parts


def _tri_left(tri, v):
    hi, mid, lo = _split(v, 3)
    return _dot(tri, hi, NN) + _dot(tri, mid, NN) + _dot(tri, lo, NN)


def _iota2(shape, dim):
    return lax.broadcasted_iota(jnp.int32, shape, dim)


def _log_sigmoid(z):
    return jnp.minimum(z, 0.0) - jnp.log1p(jnp.exp(-jnp.abs(z)))


def _log_sigmoid_abs(z):
    return jnp.minimum(z, 0.0) - jnp.log(1.0 + jnp.exp(-jnp.abs(z)))


def _head_spec(rows, width, off):
    return pl.BlockSpec((rows, width), lambda h, i: (i, off + h))


def _full_head_spec(T, width, off):
    return pl.BlockSpec((T, width), lambda h, i: (0, off + h))


def _sb_fwd(qkv, H):
    T = qkv.shape[0]
    B = _tile(T, SB_BLOCK)
    nq = T // B
    scale = 1.0 / math.sqrt(HEAD_DIM)

    HG = ATT_HEADS_FWD if H % ATT_HEADS_FWD == 0 else 1
    HD = HEAD_DIM

    def body(q_ref, k_ref, v_ref, o_ref, c_ref, tri_s):
        i = pl.program_id(1)
        tri = (_iota2((B, B), 0) > _iota2((B, B), 1)).astype(BF16)
        for p in range(SB_PARTS):
            tri_s[pl.ds(p * B, B), :] = tri
        rows = _iota2((B, B), 0)
        cols = _iota2((B, B), 1)

        def step(n, carry):
            j = i - n
            ks = pl.multiple_of(j * B, B)
            mask = (ks + cols) < (i * B + rows)
            out = []
            for g in range(HG):
                acc, run = carry[g]
                q = q_ref[:, pl.ds(g * HD, HD)]
                kb = k_ref[pl.ds(ks, B), pl.ds(g * HD, HD)]
                vb = v_ref[pl.ds(ks, B), pl.ds(g * HD, HD)]
                z = _dot(q, kb, NT) * scale
                ls = _log_sigmoid_abs(z)
                lk = jnp.where(mask, ls - z, 0.0)
                parts = jnp.concatenate(_split(lk, SB_PARTS), axis=1)
                later = _dot(parts, tri_s[...], NN)
                a = jnp.exp(jnp.where(mask, ls, NEG) + later).astype(BF16)
                acc = acc + jnp.exp(run) * _dot(a, vb, NN)
                out.append((acc, run + jnp.sum(lk, axis=1, keepdims=True)))
            return tuple(out)

        init = tuple((jnp.zeros((B, HD), F32), jnp.zeros((B, 1), F32)) for _ in range(HG))
        res = lax.fori_loop(0, i + 1, step, init)
        for g in range(HG):
            o_ref[:, pl.ds(g * HD, HD)] = res[g][0].astype(BF16)
            c_ref[g] = res[g][1]

    return pl.pallas_call(
        body, name="sb_attn_fwd", grid=(H // HG, nq),
        in_specs=[_head_spec(B, HG * HD, 0), _full_head_spec(T, HG * HD, H // HG),
                  _full_head_spec(T, HG * HD, 2 * H // HG)],
        out_specs=(_head_spec(B, HG * HD, 0), pl.BlockSpec((HG, B, 1), lambda h, i: (h, i, 0))),
        out_shape=(jax.ShapeDtypeStruct((T, H * HD), BF16), jax.ShapeDtypeStruct((H, T, 1), F32)),
        scratch_shapes=[pltpu.VMEM((SB_PARTS * B, B), BF16)],
        compiler_params=_params(("parallel", "arbitrary")),
    )(qkv, qkv, qkv)


def _sb_bwd(qkv, do, ctot, H):
    T = qkv.shape[0]
    B = _tile(T, SB_BLOCK)
    nq = T // B
    scale = 1.0 / math.sqrt(HEAD_DIM)
    HG = ATT_HEADS if H % ATT_HEADS == 0 else 1
    HD = HEAD_DIM

    def body(q_ref, k_ref, v_ref, do_ref, c_ref, dq_ref, dk_ref, dv_ref, dk_acc, dv_acc,
             tri_incl_s, tri_strict_s):
        i = pl.program_id(1)

        @pl.when(i == 0)
        def _():
            dk_acc[...] = jnp.zeros_like(dk_acc)
            dv_acc[...] = jnp.zeros_like(dv_acc)

        rows = _iota2((B, B), 0)
        cols = _iota2((B, B), 1)
        tri_incl = (rows <= cols).astype(BF16)
        tri_strict = (rows < cols).astype(BF16)
        for p in range(SB_PARTS):
            tri_incl_s[pl.ds(p * B, B), :] = tri_incl
            tri_strict_s[pl.ds(p * B, B), :] = tri_strict
        heads = [(q_ref[:, pl.ds(g * HD, HD)], do_ref[:, pl.ds(g * HD, HD)], c_ref[g]) for g in range(HG)]

        def step(j, carry):
            ks = pl.multiple_of(j * B, B)
            mask = (ks + cols) < (i * B + rows)
            out = []
            for g in range(HG):
                dq, lpre, gpre = carry[g]
                q, do_b, ctot_b = heads[g]
                kb = k_ref[pl.ds(ks, B), pl.ds(g * HD, HD)]
                vb = v_ref[pl.ds(ks, B), pl.ds(g * HD, HD)]
                z = _dot(q, kb, NT) * scale
                da = _dot(do_b, vb, NT)
                ls = _log_sigmoid_abs(z)
                lk = jnp.where(mask, ls - z, 0.0)
                lsm = jnp.where(mask, ls, NEG)
                incl = _dot(jnp.concatenate(_split(lk, SB_PARTS), axis=1), tri_incl_s[...], NN)
                a = jnp.exp(lsm + (ctot_b - lpre) - incl)
                gw = a * da
                gex = gpre + _dot(jnp.concatenate(_split(gw, SB_PARTS), axis=1), tri_strict_s[...], NN)
                dzb = ((gw * jnp.exp(lsm - z) - jnp.exp(lsm) * gex) * scale).astype(BF16)
                dq = dq + _dot(dzb, kb, NN)
                dk_acc[pl.ds(ks, B), pl.ds(g * HD, HD)] += _dot(dzb, q, TN)
                dv_acc[pl.ds(ks, B), pl.ds(g * HD, HD)] += _dot(a.astype(BF16), do_b, TN)
                out.append((dq, lpre + jnp.sum(lk, axis=1, keepdims=True),
                            gpre + jnp.sum(gw, axis=1, keepdims=True)))
            return tuple(out)

        init = tuple((jnp.zeros((B, HD), F32), jnp.zeros((B, 1), F32), jnp.zeros((B, 1), F32))
                     for _ in range(HG))
        res = lax.fori_loop(0, i + 1, step, init)
        for g in range(HG):
            dq_ref[:, pl.ds(g * HD, HD)] = res[g][0].astype(BF16)

        @pl.when(i == nq - 1)
        def _():
            dk_ref[...] = dk_acc[...].astype(BF16)
            dv_ref[...] = dv_acc[...].astype(BF16)

    W = H * HD
    return pl.pallas_call(
        body, name="sb_attn_bwd", grid=(H // HG, nq),
        in_specs=[_head_spec(B, HG * HD, 0), _full_head_spec(T, HG * HD, H // HG),
                  _full_head_spec(T, HG * HD, 2 * H // HG), _head_spec(B, HG * HD, 0),
                  pl.BlockSpec((HG, B, 1), lambda h, i: (h, i, 0))],
        out_specs=(_head_spec(B, HG * HD, 0), _full_head_spec(T, HG * HD, 0),
                   _full_head_spec(T, HG * HD, 0)),
        out_shape=tuple(jax.ShapeDtypeStruct((T, W), BF16) for _ in range(3)),
        scratch_shapes=[pltpu.VMEM((T, HG * HD), F32), pltpu.VMEM((T, HG * HD), F32),
                        pltpu.VMEM((SB_PARTS * B, B), BF16), pltpu.VMEM((SB_PARTS * B, B), BF16)],
        compiler_params=_params(("parallel", "arbitrary")),
    )(qkv, qkv, qkv, do, ctot)


def _softmax_fwd(q, k, v, cf, H, dqk, scale, *, name):
    T = q.shape[0]
    B = _tile(T, ATT_BLOCK)
    nq = T // B
    has_cf = cf is not None
    HG = ATT_HEADS_FWD if H % ATT_HEADS_FWD == 0 else 1
    HD = HEAD_DIM

    def body(*refs):
        q_ref, k_ref, v_ref = refs[:3]
        idx = 3
        if has_cf:
            cfc_ref, cfr_ref = refs[3], refs[4]
            idx = 5
        o_ref, of_ref, lse_ref = refs[idx:idx + 3]
        i = pl.program_id(1)
        rows = _iota2((B, B), 0)
        cols = _iota2((B, B), 1)

        def step(j, carry):
            ks = pl.multiple_of(j * B, B)
            mask = (ks + cols) <= (i * B + rows)
            out = []
            for g in range(HG):
                m, l, acc = carry[g]
                qb = q_ref[:, pl.ds(g * dqk, dqk)]
                kb = k_ref[pl.ds(ks, B), pl.ds(g * dqk, dqk)]
                vb = v_ref[pl.ds(ks, B), pl.ds(g * HD, HD)]
                s = _dot(qb, kb, NT) * scale
                if has_cf:
                    s = s + (cfc_ref[g] - cfr_ref[g, :, pl.ds(ks, B)])
                s = jnp.where(mask, s, NEG)
                m_new = jnp.maximum(m, jnp.max(s, axis=1, keepdims=True))
                alpha = jnp.exp(m - m_new)
                p = jnp.exp(s - m_new)
                l = alpha * l + jnp.sum(p, axis=1, keepdims=True)
                acc = alpha * acc + _dot(p.astype(BF16), vb, NN)
                out.append((m_new, l, acc))
            return tuple(out)

        init = tuple((jnp.full((B, 1), NEG, F32), jnp.zeros((B, 1), F32), jnp.zeros((B, HD), F32))
                     for _ in range(HG))
        res = lax.fori_loop(0, i + 1, step, init)
        for g in range(HG):
            m, l, acc = res[g]
            o = acc / l
            o_ref[:, pl.ds(g * HD, HD)] = o.astype(BF16)
            of_ref[:, pl.ds(g * HD, HD)] = o
            lse_ref[g] = m + jnp.log(l)

    stat = pl.BlockSpec((HG, B, 1), lambda h, i: (h, i, 0))
    in_specs = [_head_spec(B, HG * dqk, 0), _full_head_spec(T, HG * dqk, 0), _full_head_spec(T, HG * HD, 0)]
    args = [q, k, v]
    if has_cf:
        in_specs += [stat, pl.BlockSpec((HG, 1, T), lambda h, i: (h, 0, 0))]
        args += list(cf)
    W = H * HD
    return pl.pallas_call(
        body, name=name, grid=(H // HG, nq), in_specs=in_specs,
        out_specs=(_head_spec(B, HG * HD, 0), _head_spec(B, HG * HD, 0), stat),
        out_shape=(jax.ShapeDtypeStruct((T, W), BF16), jax.ShapeDtypeStruct((T, W), F32),
                   jax.ShapeDtypeStruct((H, T, 1), F32)),
        compiler_params=_params(("parallel", "arbitrary")),
    )(*args)


def _softmax_bwd(q, k, v, cf, do, o, lse, H, dqk, scale, *, name):
    T = q.shape[0]
    B = _tile(T, ATT_BLOCK)
    nq = T // B
    has_cf = cf is not None
    HG = ATT_HEADS if H % ATT_HEADS == 0 else 1
    HD = HEAD_DIM

    def body(*refs):
        q_ref, k_ref, v_ref, do_ref, o_ref, lse_ref = refs[:6]
        idx = 6
        if has_cf:
            cfc_ref, cfr_ref = refs[6], refs[7]
            idx = 8
        dq_ref, dk_ref, dv_ref = refs[idx:idx + 3]
        idx += 3
        if has_cf:
            dcc_ref, dcr_ref = refs[idx], refs[idx + 1]
        i = pl.program_id(1)

        @pl.when(i == 0)
        def _():
            dk_ref[...] = jnp.zeros_like(dk_ref)
            dv_ref[...] = jnp.zeros_like(dv_ref)
            if has_cf:
                dcr_ref[...] = jnp.zeros_like(dcr_ref)

        rows = _iota2((B, B), 0)
        cols = _iota2((B, B), 1)
        heads = []
        for g in range(HG):
            do_b = do_ref[:, pl.ds(g * HD, HD)]
            delta = jnp.sum(do_b.astype(F32) * o_ref[:, pl.ds(g * HD, HD)], axis=1, keepdims=True)
            heads.append((q_ref[:, pl.ds(g * dqk, dqk)], do_b, lse_ref[g], delta))

        def step(j, carry):
            ks = pl.multiple_of(j * B, B)
            mask = (ks + cols) <= (i * B + rows)
            out = []
            for g in range(HG):
                dq, rs = carry[g]
                qb, do_b, lse_b, delta = heads[g]
                kb = k_ref[pl.ds(ks, B), pl.ds(g * dqk, dqk)]
                vb = v_ref[pl.ds(ks, B), pl.ds(g * HD, HD)]
                s = _dot(qb, kb, NT) * scale
                if has_cf:
                    s = s + (cfc_ref[g] - cfr_ref[g, :, pl.ds(ks, B)])
                p = jnp.where(mask, jnp.exp(s - lse_b), 0.0)
                dp = _dot(do_b, vb, NT)
                ds = p * (dp - delta)
                dsb = (ds * scale).astype(BF16)
                dq = dq + _dot(dsb, kb, NN)
                dk_ref[pl.ds(ks, B), pl.ds(g * dqk, dqk)] += _dot(dsb, qb, TN)
                dv_ref[pl.ds(ks, B), pl.ds(g * HD, HD)] += _dot(p.astype(BF16), do_b, TN)
                if has_cf:
                    rs = rs + jnp.sum(ds, axis=1, keepdims=True)
                    dcr_ref[g, :, pl.ds(ks, B)] -= jnp.sum(ds, axis=0, keepdims=True)
                out.append((dq, rs))
            return tuple(out)

        init = tuple((jnp.zeros((B, dqk), F32), jnp.zeros((B, 1), F32)) for _ in range(HG))
        res = lax.fori_loop(0, i + 1, step, init)
        for g in range(HG):
            dq_ref[:, pl.ds(g * dqk, dqk)] = res[g][0]
            if has_cf:
                dcc_ref[g] = res[g][1]

    stat = pl.BlockSpec((HG, B, 1), lambda h, i: (h, i, 0))
    rowstat = pl.BlockSpec((HG, 1, T), lambda h, i: (h, 0, 0))
    in_specs = [_head_spec(B, HG * dqk, 0), _full_head_spec(T, HG * dqk, 0), _full_head_spec(T, HG * HD, 0),
                _head_spec(B, HG * HD, 0), _head_spec(B, HG * HD, 0), stat]
    args = [q, k, v, do, o, lse]
    out_specs = [_head_spec(B, HG * dqk, 0), _full_head_spec(T, HG * dqk, 0), _full_head_spec(T, HG * HD, 0)]
    out_shape = [jax.ShapeDtypeStruct((T, H * dqk), F32), jax.ShapeDtypeStruct((T, H * dqk), F32),
                 jax.ShapeDtypeStruct((T, H * HD), F32)]
    if has_cf:
        in_specs += [stat, rowstat]
        args += list(cf)
        out_specs += [stat, rowstat]
        out_shape += [jax.ShapeDtypeStruct((H, T, 1), F32), jax.ShapeDtypeStruct((H, 1, T), F32)]
    return pl.pallas_call(
        body, name=name, grid=(H // HG, nq), in_specs=in_specs, out_specs=tuple(out_specs),
        out_shape=tuple(out_shape), compiler_params=_params(("parallel", "arbitrary")),
    )(*args)


def _headnorm(x, g):
    r = lax.rsqrt(jnp.mean(x * x, axis=-1, keepdims=True) + EPS)
    return x * r * g, r


def _headnorm_bwd(x, g, dy, n):
    r = lax.rsqrt(jnp.sum(x * x, axis=-1, keepdims=True) * (1.0 / n) + EPS)
    t = dy * g
    dx = r * t - x * (r * r * r * (jnp.sum(t * x, axis=-1, keepdims=True) * (1.0 / n)))
    dg = jnp.sum(dy * x * r, axis=0, keepdims=True)
    return dx, dg


def _fox_prep_fwd(qkv, gq, gk, H, *, tm=512):
    T = qkv.shape[0]
    tm = _tile(T, tm)

    def body(q_ref, k_ref, v_ref, gq_ref, gk_ref, qn_ref, kn_ref, vb_ref):
        qn_ref[...] = _headnorm(q_ref[...], gq_ref[...])[0].astype(BF16)
        kn_ref[...] = _headnorm(k_ref[...], gk_ref[...])[0].astype(BF16)
        vb_ref[...] = v_ref[...].astype(BF16)

    def blk(off):
        return pl.BlockSpec((tm, HEAD_DIM), lambda i, h: (i, off + h))

    vec = pl.BlockSpec((1, HEAD_DIM), lambda i, h: (0, 0))
    W = H * HEAD_DIM
    return pl.pallas_call(
        body, name="fox_prep_fwd", grid=(T // tm, H),
        in_specs=[blk(0), blk(H), blk(2 * H), vec, vec], out_specs=(blk(0), blk(0), blk(0)),
        out_shape=tuple(jax.ShapeDtypeStruct((T, W), BF16) for _ in range(3)),
        compiler_params=_params(("parallel", "parallel")),
    )(qkv, qkv, qkv, gq, gk)


def _fox_prep_bwd(qkv, gq, gk, dqn, dkn, dv, H, *, tm=512):
    T = qkv.shape[0]
    tm = _tile(T, tm)

    def body(q_ref, k_ref, gq_ref, gk_ref, dqn_ref, dkn_ref, dv_ref,
             dq_ref, dk_ref, dvb_ref, dgq_ref, dgk_ref):
        @pl.when((pl.program_id(0) == 0) & (pl.program_id(1) == 0))
        def _():
            dgq_ref[...] = jnp.zeros_like(dgq_ref)
            dgk_ref[...] = jnp.zeros_like(dgk_ref)

        dq, dgq = _headnorm_bwd(q_ref[...], gq_ref[...], dqn_ref[...], HEAD_DIM)
        dk, dgk = _headnorm_bwd(k_ref[...], gk_ref[...], dkn_ref[...], HEAD_DIM)
        dq_ref[...] = dq.astype(BF16)
        dk_ref[...] = dk.astype(BF16)
        dvb_ref[...] = dv_ref[...].astype(BF16)
        dgq_ref[...] += dgq
        dgk_ref[...] += dgk

    def blk(off):
        return pl.BlockSpec((tm, HEAD_DIM), lambda i, h: (i, off + h))

    vec = pl.BlockSpec((1, HEAD_DIM), lambda i, h: (0, 0))
    W = H * HEAD_DIM
    return pl.pallas_call(
        body, name="fox_prep_bwd", grid=(T // tm, H),
        in_specs=[blk(0), blk(H), vec, vec, blk(0), blk(0), blk(0)],
        out_specs=(blk(0), blk(0), blk(0), vec, vec),
        out_shape=tuple(jax.ShapeDtypeStruct((T, W), BF16) for _ in range(3))
        + (jax.ShapeDtypeStruct((1, HEAD_DIM), F32), jax.ShapeDtypeStruct((1, HEAD_DIM), F32)),
        compiler_params=_params(("arbitrary", "arbitrary")),
    )(qkv, qkv, gq, gk, dqn, dkn, dv)


def _fox_gate_fwd(flog, bf, *, tm=256):
    T = flog.shape[0]
    tm = _tile(T, tm)

    def body(f_ref, b_ref, cf_ref, carry):
        @pl.when(pl.program_id(0) == 0)
        def _():
            carry[...] = jnp.zeros_like(carry)

        lf = _log_sigmoid(f_ref[...] + b_ref[...])
        tri = (_iota2((tm, tm), 1) <= _iota2((tm, tm), 0)).astype(BF16)
        cf_ref[...] = carry[...] + _tri_left(tri, lf)
        carry[...] += jnp.sum(lf, axis=0, keepdims=True)

    return pl.pallas_call(
        body, name="fox_gate_fwd", grid=(T // tm,),
        in_specs=[pl.BlockSpec((tm, LANE), lambda i: (i, 0)), pl.BlockSpec((1, LANE), lambda i: (0, 0))],
        out_specs=pl.BlockSpec((tm, LANE), lambda i: (i, 0)),
        out_shape=jax.ShapeDtypeStruct((T, LANE), F32),
        scratch_shapes=[pltpu.VMEM((1, LANE), F32)],
        compiler_params=_params(("arbitrary",)),
    )(flog, bf)


def _fox_gate_bwd(flog, bf, dcf, *, tm=256):
    T = flog.shape[0]
    tm = _tile(T, tm)
    nt = T // tm

    def body(f_ref, b_ref, dcf_ref, df_ref, db_ref, carry):
        @pl.when(pl.program_id(0) == 0)
        def _():
            carry[...] = jnp.zeros_like(carry)
            db_ref[...] = jnp.zeros_like(db_ref)

        d = dcf_ref[...]
        tri = (_iota2((tm, tm), 1) >= _iota2((tm, tm), 0)).astype(BF16)
        dlf = carry[...] + _tri_left(tri, d)
        carry[...] += jnp.sum(d, axis=0, keepdims=True)
        xg = f_ref[...] + b_ref[...]
        e = jnp.exp(-jnp.abs(xg))
        sig_neg = jnp.where(xg >= 0.0, e, 1.0) / (1.0 + e)
        df = dlf * sig_neg
        df_ref[...] = df.astype(BF16)
        db_ref[...] += jnp.sum(df, axis=0, keepdims=True)

    rev = pl.BlockSpec((tm, LANE), lambda i: (nt - 1 - i, 0))
    vec = pl.BlockSpec((1, LANE), lambda i: (0, 0))
    return pl.pallas_call(
        body, name="fox_gate_bwd", grid=(nt,), in_specs=[rev, vec, rev], out_specs=(rev, vec),
        out_shape=(jax.ShapeDtypeStruct((T, LANE), BF16), jax.ShapeDtypeStruct((1, LANE), F32)),
        scratch_shapes=[pltpu.VMEM((1, LANE), F32)],
        compiler_params=_params(("arbitrary",)),
    )(flog, bf, dcf)


def _rope_tables(positions):
    half = MLA_ROPE // 2
    inv_freq = ROPE_THETA ** (-jnp.arange(0, half, dtype=F32) * 2.0 / MLA_ROPE)
    ang = positions.astype(F32)[:, None] * inv_freq
    cos, sin = jnp.cos(ang), jnp.sin(ang)
    zero = jnp.zeros_like(cos)
    pad = jnp.zeros((positions.shape[0], LANE - MLA_ROPE), F32)
    cos_t = jnp.concatenate([cos, cos, pad], axis=1)
    sin_up = jnp.concatenate([zero, sin, pad], axis=1)
    sin_dn = jnp.concatenate([-sin, zero, pad], axis=1)
    return cos_t, sin_up, sin_dn


def _rope(x, cos_t, sin_up, sin_dn):
    half = MLA_ROPE // 2
    return x * cos_t + pltpu.roll(x, half, 1) * sin_up + pltpu.roll(x, LANE - half, 1) * sin_dn


def _rope_t(d, cos_t, sin_up, sin_dn):
    half = MLA_ROPE // 2
    return d * cos_t + pltpu.roll(d * sin_up, LANE - half, 1) + pltpu.roll(d * sin_dn, half, 1)


def _norm192(xcat, g):
    r = lax.rsqrt(jnp.sum(xcat * xcat, axis=-1, keepdims=True) * (1.0 / MLA_QK) + EPS)
    return xcat * r * g


def _mla_prep_fwd(qfull, kv, kr, tabs, gq, gk, H, *, tm=512):
    T = qfull.shape[0]
    tm = _tile(T, tm)

    def body(q_ref, kv_ref, kr_ref, c_ref, su_ref, sd_ref, gq_ref, gk_ref, qf_ref, kf_ref, v_ref):
        tabs_b = (c_ref[...], su_ref[...], sd_ref[...])
        qb = q_ref[...]
        qcat = jnp.concatenate([qb[:, :MLA_NOPE], _rope(qb[:, MLA_NOPE:], *tabs_b)], axis=1)
        qf_ref[...] = _norm192(qcat, gq_ref[...]).astype(BF16)
        kvb = kv_ref[...]
        kcat = jnp.concatenate([kvb[:, :MLA_NOPE], _rope(kr_ref[...], *tabs_b)], axis=1)
        kf_ref[...] = _norm192(kcat, gk_ref[...]).astype(BF16)
        v_ref[...] = kvb[:, MLA_NOPE:].astype(BF16)

    head = pl.BlockSpec((tm, MLA_QK_PAD), lambda i, h: (i, h))
    tok = pl.BlockSpec((tm, LANE), lambda i, h: (i, 0))
    vec = pl.BlockSpec((1, MLA_QK_PAD), lambda i, h: (0, 0))
    return pl.pallas_call(
        body, name="mla_prep_fwd", grid=(T // tm, H),
        in_specs=[head, head, tok, tok, tok, tok, vec, vec],
        out_specs=(head, head, pl.BlockSpec((tm, MLA_V), lambda i, h: (i, h))),
        out_shape=(jax.ShapeDtypeStruct((T, H * MLA_QK_PAD), BF16),
                   jax.ShapeDtypeStruct((T, H * MLA_QK_PAD), BF16),
                   jax.ShapeDtypeStruct((T, H * MLA_V), BF16)),
        compiler_params=_params(("parallel", "parallel")),
    )(qfull, kv, kr, *tabs, gq, gk)


def _mla_prep_bwd(qfull, kv, kr, tabs, gq, gk, dqf, dkf, dv, H, *, tm=512):
    T = qfull.shape[0]
    tm = _tile(T, tm)

    def body(q_ref, kv_ref, kr_ref, c_ref, su_ref, sd_ref, gq_ref, gk_ref, dqf_ref, dkf_ref, dv_ref,
             dq_ref, dkv_ref, dkr_ref, dgq_ref, dgk_ref, kr_acc):
        h = pl.program_id(1)

        @pl.when((pl.program_id(0) == 0) & (h == 0))
        def _():
            dgq_ref[...] = jnp.zeros_like(dgq_ref)
            dgk_ref[...] = jnp.zeros_like(dgk_ref)

        @pl.when(h == 0)
        def _():
            kr_acc[...] = jnp.zeros_like(kr_acc)

        tabs_b = (c_ref[...], su_ref[...], sd_ref[...])
        qb = q_ref[...]
        qcat = jnp.concatenate([qb[:, :MLA_NOPE], _rope(qb[:, MLA_NOPE:], *tabs_b)], axis=1)
        dqcat, dgq = _headnorm_bwd(qcat, gq_ref[...], dqf_ref[...], MLA_QK)
        dq_ref[...] = jnp.concatenate(
            [dqcat[:, :MLA_NOPE], _rope_t(dqcat[:, MLA_NOPE:], *tabs_b)], axis=1).astype(BF16)
        dgq_ref[...] += dgq
        kvb = kv_ref[...]
        kcat = jnp.concatenate([kvb[:, :MLA_NOPE], _rope(kr_ref[...], *tabs_b)], axis=1)
        dkcat, dgk = _headnorm_bwd(kcat, gk_ref[...], dkf_ref[...], MLA_QK)
        dkv_ref[...] = jnp.concatenate([dkcat[:, :MLA_NOPE], dv_ref[...]], axis=1).astype(BF16)
        dgk_ref[...] += dgk
        kr_acc[...] += dkcat[:, MLA_NOPE:]

        @pl.when(h == H - 1)
        def _():
            dkr_ref[...] = _rope_t(kr_acc[...], *tabs_b).astype(BF16)

    head = pl.BlockSpec((tm, MLA_QK_PAD), lambda i, h: (i, h))
    tok = pl.BlockSpec((tm, LANE), lambda i, h: (i, 0))
    vec = pl.BlockSpec((1, MLA_QK_PAD), lambda i, h: (0, 0))
    return pl.pallas_call(
        body, name="mla_prep_bwd", grid=(T // tm, H),
        in_specs=[head, head, tok, tok, tok, tok, vec, vec, head, head,
                  pl.BlockSpec((tm, MLA_V), lambda i, h: (i, h))],
        out_specs=(head, head, tok, vec, vec),
        out_shape=(jax.ShapeDtypeStruct((T, H * MLA_QK_PAD), BF16),
                   jax.ShapeDtypeStruct((T, H * MLA_QK_PAD), BF16),
                   jax.ShapeDtypeStruct((T, LANE), BF16),
                   jax.ShapeDtypeStruct((1, MLA_QK_PAD), F32), jax.ShapeDtypeStruct((1, MLA_QK_PAD), F32)),
        scratch_shapes=[pltpu.VMEM((tm, LANE), F32)],
        compiler_params=_params(("arbitrary", "arbitrary")),
    )(qfull, kv, kr, *tabs, gq, gk, dqf, dkf, dv)


def _cf_layouts(cf, H):
    cfh = cf[:, :H].T
    return cfh[:, :, None], cfh[:, None, :]


def _layer_fwd(i, x, P, tabs, H, hook=None):
    kind, j = i % 3, i // 3
    s = {"x_in": x}
    h = _rmsnorm_fwd(x, P["mix_norm"][i:i + 1], name="mix_norm_fwd")
    s["h"] = h
    if kind == 0:
        w = P["sb"][j]
        qkv = _mm(h, w["w_in"], out_dtype=BF16, name="sb_qkv")
        o, ctot = _sb_fwd(qkv, H)
        s.update(qkv=qkv, o=o, ctot=ctot)
    elif kind == 1:
        w = P["fox"][j]
        qkv = _mm(h, w["w_qkv"], name="fox_qkv")
        flog = _mm(h, w["w_f"], name="fox_gate_proj")
        qn, kn, vb = _fox_prep_fwd(qkv, w["gq"], w["gk"], H)
        cf = _cf_layouts(_fox_gate_fwd(flog, w["bf"]), H)
        o, of, lse = _softmax_fwd(qn, kn, vb, cf, H, HEAD_DIM, 1.0 / math.sqrt(HEAD_DIM), name="fox_attn_fwd")
        s.update(qkv=qkv, flog=flog, qn=qn, kn=kn, vb=vb, cf=cf, o=o, of=of, lse=lse)
    else:
        w = P["mla"][j]
        dq = _mm(h, w["w_dq"], name="mla_down_q")
        dkv = _mm(h, w["w_dkv"], name="mla_down_kv")
        kr = _mm(h, w["w_dr"], name="mla_down_rope")
        cq = _rmsnorm_fwd(dq, w["q_norm"], name="mla_q_norm_fwd")
        ckv = _rmsnorm_fwd(dkv, w["kv_norm"], name="mla_kv_norm_fwd")
        qfull = _mm(cq, w["w_uq"], name="mla_up_q")
        kv = _mm(ckv, w["w_ukv"], name="mla_up_kv")
        qf, kf, v = _mla_prep_fwd(qfull, kv, kr, tabs, w["gq"], w["gk"], H)
        o, of, lse = _softmax_fwd(qf, kf, v, None, H, MLA_QK_PAD, 1.0 / math.sqrt(MLA_QK), name="mla_attn_fwd")
        s.update(dq=dq, dkv=dkv, kr=kr, cq=cq, ckv=ckv, qfull=qfull, kv=kv, qf=qf, kf=kf, v=v,
                 o=o, of=of, lse=lse)
    x = _mm(s["o"], w["w_out"], res=x, name="mixer_out")
    s["x_mid"] = x
    deps = hook(x) if hook is not None else ()
    h2 = _rmsnorm_fwd(x, P["mlp_norm"][i:i + 1], name="mlp_norm_fwd")
    z, u = _mm(h2, P["mlp"][i]["w1"], act="relu2", name="mlp_up", deps=deps)
    x = _mm(u, P["mlp"][i]["w2"], res=x, name="mlp_down")
    s.update(h2=h2, z=z, u=u)
    return x, s


def _layer_bwd(i, dx, dxb, s, P, tabs, H, deps=(), hook=None, hook2=None):
    kind, j = i % 3, i // 3
    G = {}
    wm = P["mlp"][i]
    dz = _mm(dxb, wm["w2"], mode="nt", act="drelu2", z=s["z"], out_dtype=BF16, name="mlp_down_bwd",
             deps=deps)
    G["w2"] = _mm(s["u"], dxb, mode="tn", out_dtype=BF16, name="mlp_w2_grad")
    G["w1"] = _mm(s["h2"], dz, mode="tn", out_dtype=BF16, name="mlp_w1_grad")
    dh2 = _mm(dz, wm["w1"], mode="nt", name="mlp_up_bwd")
    dx, dxb, G["mlp_norm"] = _rmsnorm_bwd(s["x_mid"], P["mlp_norm"][i:i + 1], dh2, dx, name="mlp_norm_bwd")
    h = s["h"]
    w = P[("sb", "fox", "mla")[kind]][j]
    do = _mm(dxb, w["w_out"], mode="nt", out_dtype=BF16, name="mixer_out_bwd",
             deps=hook(dxb, G) if hook is not None else ())
    G["w_out"] = _mm(s["o"], dxb, mode="tn", out_dtype=BF16, name="mixer_out_grad")
    if kind == 0:
        dq, dk, dv = _sb_bwd(s["qkv"], do, s["ctot"], H)
        dproj = jnp.concatenate([dq, dk, dv], axis=1)
        G["w_in"] = _mm(h, dproj, mode="tn", out_dtype=BF16, name="sb_qkv_grad",
                        deps=hook2(dv) if hook2 is not None else ())
        dh = _mm(dproj, w["w_in"], mode="nt", name="sb_qkv_bwd")
    elif kind == 1:
        dqn, dkn, dv, dcc, dcr = _softmax_bwd(
            s["qn"], s["kn"], s["vb"], s["cf"], do, s["of"], s["lse"], H, HEAD_DIM,
            1.0 / math.sqrt(HEAD_DIM), name="fox_attn_bwd")
        dq, dk, dvb, G["gq"], G["gk"] = _fox_prep_bwd(s["qkv"], w["gq"], w["gk"], dqn, dkn, dv, H)
        dcf = (dcc[:, :, 0] + dcr[:, 0, :]).T
        dcf = jnp.pad(dcf, ((0, 0), (0, LANE - H)))
        dflog, G["bf"] = _fox_gate_bwd(s["flog"], w["bf"], dcf)
        dproj = jnp.concatenate([dq, dk, dvb], axis=1)
        G["w_qkv"] = _mm(h, dproj, mode="tn", out_dtype=BF16, name="fox_qkv_grad",
                         deps=hook2(dv) if hook2 is not None else ())
        G["w_f"] = _mm(h, dflog, mode="tn", out_dtype=BF16, name="fox_gate_grad")
        dh = _mm(dproj, w["w_qkv"], mode="nt", name="fox_qkv_bwd")
        dh = _mm(dflog, w["w_f"], mode="nt", res=dh, name="fox_gate_bwd_proj")
    else:
        dqf, dkf, dv = _softmax_bwd(
            s["qf"], s["kf"], s["v"], None, do, s["of"], s["lse"], H, MLA_QK_PAD,
            1.0 / math.sqrt(MLA_QK), name="mla_attn_bwd")
        dqfull, dkv, dkr, G["gq"], G["gk"] = _mla_prep_bwd(
            s["qfull"], s["kv"], s["kr"], tabs, w["gq"], w["gk"], dqf, dkf, dv, H)
        G["w_uq"] = _mm(s["cq"], dqfull, mode="tn", out_dtype=BF16, name="mla_up_q_grad",
                        deps=hook2(dv) if hook2 is not None else ())
        G["w_ukv"] = _mm(s["ckv"], dkv, mode="tn", out_dtype=BF16, name="mla_up_kv_grad")
        dcq = _mm(dqfull, w["w_uq"], mode="nt", name="mla_up_q_bwd")
        dckv = _mm(dkv, w["w_ukv"], mode="nt", name="mla_up_kv_bwd")
        ddq, G["q_norm"] = _rmsnorm_bwd(s["dq"], w["q_norm"], dcq, want_f32=False, name="mla_q_norm_bwd")
        ddkv, G["kv_norm"] = _rmsnorm_bwd(s["dkv"], w["kv_norm"], dckv, want_f32=False, name="mla_kv_norm_bwd")
        G["w_dq"] = _mm(h, ddq, mode="tn", out_dtype=BF16, name="mla_down_q_grad")
        G["w_dkv"] = _mm(h, ddkv, mode="tn", out_dtype=BF16, name="mla_down_kv_grad")
        G["w_dr"] = _mm(h, dkr, mode="tn", out_dtype=BF16, name="mla_down_rope_grad")
        dh = _mm(ddq, w["w_dq"], mode="nt", name="mla_down_q_bwd")
        dh = _mm(ddkv, w["w_dkv"], mode="nt", res=dh, name="mla_down_kv_bwd")
        dh = _mm(dkr, w["w_dr"], mode="nt", res=dh, name="mla_down_rope_bwd")
    dx, dxb, G["mix_norm"] = _rmsnorm_bwd(s["x_in"], P["mix_norm"][i:i + 1], dh, dx, name="mix_norm_bwd")
    return dx, dxb, G


def _local_step(x, positions, loss_target, P, depth, H):
    tabs = _rope_tables(positions)
    saved = []
    for i in range(depth):
        x, s = _layer_fwd(i, x, P, tabs, H)
        saved.append(s)
    dx, dxb, loss = _loss_head(x, loss_target)
    grads = [None] * depth
    for i in reversed(range(depth)):
        dx, dxb, grads[i] = _layer_bwd(i, dx, dxb, saved[i], P, tabs, H)
    return loss, dx, grads


def _pad_cols(a, n):
    return jnp.pad(a, ((0, 0), (0, n - a.shape[1])))


def _prepare_layer(P, full, i, H, parts=("mix", "mlp")):
    W = H * HEAD_DIM
    kind, j = i % 3, i // 3
    if "mlp" in parts:
        P["mlp"][i] = {"w1": full["mlp_w1"][i], "w2": full["mlp_w2"][i]}
    if "mix" not in parts:
        return
    if kind == 0:
        P["sb"][j] = {"w_in": full["sb_w_in"][j], "w_out": full["sb_w_out"][j]}
    elif kind == 1:
        w = full["fox_w_in"][j]
        P["fox"][j] = {
            "w_qkv": w[:, :3 * W], "w_f": _pad_cols(w[:, 3 * W:], LANE),
            "bf": _pad_cols(full["fox_b_f"][j:j + 1], LANE),
            "gq": full["fox_q_gain"][j:j + 1], "gk": full["fox_k_gain"][j:j + 1],
            "w_out": full["fox_w_out"][j]}
    else:
        w = full["mla_w_in"][j]
        q_norm, kv_norm = full["mla_q_norm"][j], full["mla_kv_norm"][j]
        rq, rkv = q_norm.shape[0], kv_norm.shape[0]
        w_uq = full["mla_w_uq"][j].reshape(rq, H, MLA_QK)
        w_uq = jnp.pad(w_uq, ((0, 0), (0, 0), (0, MLA_QK_PAD - MLA_QK))).reshape(rq, H * MLA_QK_PAD)
        P["mla"][j] = {
            "w_dq": w[:, :rq], "w_dkv": w[:, rq:rq + rkv], "w_dr": _pad_cols(w[:, rq + rkv:], LANE),
            "q_norm": q_norm[None], "kv_norm": kv_norm[None],
            "w_uq": w_uq, "w_ukv": full["mla_w_ukv"][j],
            "gq": _pad_cols(full["mla_q_gain"][j:j + 1], MLA_QK_PAD),
            "gk": _pad_cols(full["mla_k_gain"][j:j + 1], MLA_QK_PAD),
            "w_out": full["mla_w_out"][j]}


def _prepare(full, H):
    depth = len(full["mlp_w1"])
    P = {"mix_norm": full["mix_norm"], "mlp_norm": full["mlp_norm"], "mlp": [None] * depth,
         "sb": [None] * len(full["sb_w_in"]), "fox": [None] * len(full["fox_w_in"]),
         "mla": [None] * len(full["mla_w_in"])}
    for i in range(depth):
        _prepare_layer(P, full, i, H)
    return P


def _layer_grads(i, G, H):
    kind = i % 3
    out = {"mix_norm": G["mix_norm"], "mlp_norm": G["mlp_norm"], "mlp_w1": G["w1"], "mlp_w2": G["w2"]}
    if kind == 0:
        out.update(sb_w_in=G["w_in"], sb_w_out=G["w_out"])
    elif kind == 1:
        out.update(fox_w_in=jnp.concatenate([G["w_qkv"], G["w_f"][:, :H]], axis=1), fox_b_f=G["bf"][:, :H],
                   fox_q_gain=G["gq"], fox_k_gain=G["gk"], fox_w_out=G["w_out"])
    else:
        rq = G["w_uq"].shape[0]
        out.update(
            mla_w_in=jnp.concatenate([G["w_dq"], G["w_dkv"], G["w_dr"][:, :MLA_ROPE]], axis=1),
            mla_q_norm=G["q_norm"], mla_kv_norm=G["kv_norm"],
            mla_w_uq=G["w_uq"].reshape(rq, H, MLA_QK_PAD)[:, :, :MLA_QK].reshape(rq, H * MLA_QK),
            mla_w_ukv=G["w_ukv"], mla_q_gain=G["gq"][:, :MLA_QK], mla_k_gain=G["gk"][:, :MLA_QK],
            mla_w_out=G["w_out"])
    return out


def _layer_of(name, j):
    if name.startswith("sb_"):
        return 3 * j
    if name.startswith("fox_"):
        return 3 * j + 1
    if name.startswith("mla_"):
        return 3 * j + 2
    return j


def _unprepare(grads, H):
    out = {}
    for i, G in enumerate(grads):
        for n, g in _layer_grads(i, G, H).items():
            out.setdefault(n, []).append(g)
    return out


class _Place:
    def __init__(self, x, y, c):
        self.x, self.y, self.c = x, y, c
        self.dev = 4 * x + 2 * y + c
        self.chip = 2 * x + y
        self.id = (x, y, c)


def _peers(me, kind):
    if kind == "ici":
        return [_Place(1 - me.x, me.y, me.c), _Place(me.x, 1 - me.y, me.c), _Place(1 - me.x, 1 - me.y, me.c)]
    return [_Place(me.x, me.y, 1 - me.c)]


def _exchange(name, kind, operands, out_shapes, aliases, n_remote, n_local, plan):
    n_in, n_out = len(operands), len(out_shapes)

    def body(*refs):
        in_refs, out_refs = refs[:n_in], refs[n_in:n_in + n_out]
        send_sems, recv_sems, local_sems = refs[n_in + n_out:]
        me = _Place(lax.axis_index("x"), lax.axis_index("y"), lax.axis_index("c"))
        peers = _peers(me, kind)
        remote, local = plan(me, peers, in_refs, out_refs)
        assert len(remote) == n_remote and len(local) == n_local
        sends = []
        for n, (src, dst, k, _) in enumerate(remote):
            cp = pltpu.make_async_remote_copy(
                src_ref=src, dst_ref=dst, send_sem=send_sems.at[n], recv_sem=recv_sems.at[n],
                device_id=peers[k].id, device_id_type=MESH)
            cp.start()
            sends.append(cp)
        own = []
        for n, (src, dst) in enumerate(local):
            cp = pltpu.make_async_copy(src, dst, local_sems.at[n])
            cp.start()
            own.append(cp)
        for n, (src, _, k, landing) in enumerate(remote):
            pltpu.make_async_remote_copy(
                src_ref=src, dst_ref=landing, send_sem=send_sems.at[n], recv_sem=recv_sems.at[n],
                device_id=peers[k].id, device_id_type=MESH).wait_recv()
        for cp in sends:
            cp.wait_send()
        for cp in own:
            cp.wait()

    outs = pl.pallas_call(
        body, name=name, in_specs=[ANY] * n_in, out_specs=tuple([ANY] * n_out),
        out_shape=tuple(out_shapes), input_output_aliases=aliases,
        scratch_shapes=[pltpu.SemaphoreType.DMA((n_remote,)), pltpu.SemaphoreType.DMA((n_remote,)),
                        pltpu.SemaphoreType.DMA((max(n_local, 1),))],
    )(*operands)
    return list(outs)


def _window(ref, kind, d, shard_shape):
    r, c = shard_shape
    if kind == "col":
        return ref.at[:, pl.ds(pl.multiple_of(d * c, c), c)]
    return ref.at[pl.ds(pl.multiple_of(d * r, r), r), :]


def _full_shape(kind, shard_shape):
    r, c = shard_shape
    return (r, N_DEV * c) if kind == "col" else (N_DEV * r, c)


HBM = pl.BlockSpec(memory_space=pltpu.HBM)
SEM = pl.BlockSpec(memory_space=pltpu.SEMAPHORE)
EFFECT = pltpu.SideEffectType.DATAFLOW_SIDE_EFFECTING


class _Flight:
    def __init__(self, name, kind, n_remote, plan, send_sems, recv_sems, bufs, token):
        self.name, self.kind, self.n_remote, self.plan = name, kind, n_remote, plan
        self.send_sems, self.recv_sems, self.bufs, self.token = send_sems, recv_sems, bufs, token


def _start_copies(name, kind, bufs, n_remote, plan, after=None):
    nb = len(bufs)
    n_after = 0 if after is None else 1

    def body(*refs):
        in_refs = refs[:nb]
        send_sems, recv_sems = refs[nb + n_after], refs[nb + n_after + 1]
        token = refs[2 * nb + n_after + 2]
        me = _Place(lax.axis_index("x"), lax.axis_index("y"), lax.axis_index("c"))
        peers = _peers(me, kind)
        for n, (src, dst, k, _) in enumerate(plan(me, peers, in_refs)):
            pltpu.make_async_remote_copy(
                src_ref=src, dst_ref=dst, send_sem=send_sems.at[n], recv_sem=recv_sems.at[n],
                device_id=peers[k].id, device_id_type=MESH).start()
        token[...] = jnp.zeros_like(token)

    outs = pl.pallas_call(
        body, name=name, in_specs=[HBM] * nb + [ANY] * n_after,
        out_specs=(SEM, SEM) + (HBM,) * nb + (pl.BlockSpec(memory_space=pltpu.VMEM),),
        out_shape=(pltpu.SemaphoreType.DMA((n_remote,)), pltpu.SemaphoreType.DMA((n_remote,)))
        + tuple(pltpu.HBM(b.shape, b.dtype) for b in bufs) + (jax.ShapeDtypeStruct((8, LANE), F32),),
        input_output_aliases={a: 2 + a for a in range(nb)},
        compiler_params=pltpu.CompilerParams(has_side_effects=EFFECT),
    )(*[pltpu.with_memory_space_constraint(b, pltpu.HBM) for b in bufs], *([after] if n_after else []))
    return _Flight(name, kind, n_remote, plan, outs[0], outs[1], list(outs[2:2 + nb]), outs[2 + nb])


def _wait_copies(flight, after):
    nb = len(flight.bufs)
    plan, kind = flight.plan, flight.kind
    after = list(after) if isinstance(after, (tuple, list)) else [after]

    def body(*refs):
        in_refs = refs[:nb]
        send_sems, recv_sems = refs[nb], refs[nb + 1]
        me = _Place(lax.axis_index("x"), lax.axis_index("y"), lax.axis_index("c"))
        peers = _peers(me, kind)
        for n, (src, _, k, landing) in enumerate(plan(me, peers, in_refs)):
            cp = pltpu.make_async_remote_copy(
                src_ref=src, dst_ref=landing, send_sem=send_sems.at[n], recv_sem=recv_sems.at[n],
                device_id=peers[k].id, device_id_type=MESH)
            cp.wait_send()
            cp.wait_recv()

    outs = pl.pallas_call(
        body, name=flight.name.replace("_start", "_wait"),
        in_specs=[HBM] * nb + [SEM, SEM] + [ANY] * len(after), out_specs=(HBM,) * nb,
        out_shape=tuple(pltpu.HBM(b.shape, b.dtype) for b in flight.bufs),
        input_output_aliases={a: a for a in range(nb)},
        compiler_params=pltpu.CompilerParams(has_side_effects=EFFECT),
    )(*flight.bufs, flight.send_sems, flight.recv_sems, *after)
    return list(outs)


def _own_in_place(shard, kind, dev, dtype, deps=()):
    r, c = shard.shape
    tr = _row_tile(r, c)
    nb = r // tr

    def body(dev_ref, s_ref, *rest):
        o_ref = rest[-1]
        o_ref[...] = s_ref[...].astype(dtype)

    if kind == "col":
        o_spec = pl.BlockSpec((tr, c), lambda i, dev_ref: (i, dev_ref[0]))
    else:
        o_spec = pl.BlockSpec((tr, c), lambda i, dev_ref: (dev_ref[0] * nb + i, 0))
    return pl.pallas_call(
        body, name="own_in_place",
        grid_spec=pltpu.PrefetchScalarGridSpec(
            num_scalar_prefetch=1, grid=(nb,),
            in_specs=[pl.BlockSpec((tr, c), lambda i, dev_ref: (i, 0))] + [ANY] * len(deps), out_specs=o_spec),
        out_shape=jax.ShapeDtypeStruct(_full_shape(kind, shard.shape), dtype),
        compiler_params=_params(("parallel",)),
    )(dev, shard, *deps)


def _gather_ici_start(fulls, kinds, shapes, tag, after=None):
    n = len(fulls)

    def plan(me, peers, refs):
        remote = []
        for a in range(n):
            mine = _window(refs[a], kinds[a], me.dev, shapes[a])
            for k, p in enumerate(peers):
                remote.append((mine, mine, k, _window(refs[a], kinds[a], p.dev, shapes[a])))
        return remote

    return _start_copies("gather_ici_start_" + tag, "ici", list(fulls), 3 * n, plan, after)


def _gather_d2d_start(fulls, kinds, shapes, tag):
    n = len(fulls)

    def plan(me, peers, refs):
        remote = []
        for a in range(n):
            for ch in range(N_CHIP):
                held = _window(refs[a], kinds[a], 2 * ch + me.c, shapes[a])
                remote.append((held, held, 0, _window(refs[a], kinds[a], 2 * ch + 1 - me.c, shapes[a])))
        return remote

    return _start_copies("gather_d2d_start_" + tag, "d2d", fulls, N_CHIP * n, plan)


def _scatter_d2d_start(grads, kinds, shapes, tag):
    n = len(grads)
    lands = [lax.empty((N_CHIP,) + tuple(s), g.dtype) for g, s in zip(grads, shapes)]

    def plan(me, peers, refs):
        remote = []
        for a in range(n):
            for ch in range(N_CHIP):
                remote.append((_window(refs[a], kinds[a], 2 * ch + 1 - me.c, shapes[a]),
                               refs[n + a].at[ch], 0, refs[n + a].at[ch]))
        return remote

    return _start_copies("scatter_d2d_start_" + tag, "d2d", list(grads) + lands, N_CHIP * n, plan)


def _scatter_ici_start(sums, shapes, tag, after=None):
    n = len(sums)
    lands = [lax.empty((N_CHIP - 1,) + tuple(s), BF16) for s in shapes]

    def plan(me, peers, refs):
        remote = []
        for a in range(n):
            for k, p in enumerate(peers):
                remote.append((refs[a].at[p.chip], refs[n + a].at[k], k, refs[n + a].at[k]))
        return remote

    return _start_copies("scatter_ici_start_" + tag, "ici", list(sums) + lands, 3 * n, plan, after)


def _pair_add(g, r4, kind, shard_shape, core, *, name):
    r, c = shard_shape
    tr = _row_tile(r, c, 524288)
    nb = r // tr

    def body(core_ref, g_ref, r_ref, o_ref):
        o_ref[...] = (g_ref[...].astype(F32) + r_ref[...].astype(F32)).astype(BF16)

    if kind == "col":
        g_spec = pl.BlockSpec((tr, c), lambda ch, i, core_ref: (i, 2 * ch + core_ref[0]))
    else:
        g_spec = pl.BlockSpec((tr, c), lambda ch, i, core_ref: ((2 * ch + core_ref[0]) * nb + i, 0))
    slot = pl.BlockSpec((None, tr, c), lambda ch, i, core_ref: (ch, i, 0))
    return pl.pallas_call(
        body, name=name,
        grid_spec=pltpu.PrefetchScalarGridSpec(
            num_scalar_prefetch=1, grid=(N_CHIP, nb), in_specs=[g_spec, slot], out_specs=slot),
        out_shape=jax.ShapeDtypeStruct((N_CHIP, r, c), BF16),
        compiler_params=_params(("parallel", "parallel")),
    )(core, g, r4)


def _allreduce_small_start(part, chip):
    def plan_swap(me, peers, ins, outs):
        return [(ins[0], outs[0], 0, outs[0])], []

    (got,) = _exchange("allreduce_small_swap", "d2d", [part], [jax.ShapeDtypeStruct(part.shape, F32)],
                       {}, 1, 0, plan_swap)

    def body(chip_ref, a_ref, b_ref, o_ref):
        o_ref[...] = a_ref[...] + b_ref[...]

    whole = pl.BlockSpec(part.shape, lambda i, chip_ref: (0, 0))
    slots = pl.pallas_call(
        body, name="allreduce_small_pair",
        grid_spec=pltpu.PrefetchScalarGridSpec(
            num_scalar_prefetch=1, grid=(1,), in_specs=[whole, whole],
            out_specs=pl.BlockSpec((None,) + part.shape, lambda i, chip_ref: (chip_ref[0], 0, 0))),
        out_shape=jax.ShapeDtypeStruct((N_CHIP,) + part.shape, F32),
    )(chip, part, got)

    def plan(me, peers, refs):
        mine = refs[0].at[me.chip]
        return [(mine, mine, k, refs[0].at[p.chip]) for k, p in enumerate(peers)]

    return _start_copies("allreduce_small_start", "ici", [slots], 3, plan)


def _row_tile(r, c, limit=262144):
    best = None
    for t in range(8, r + 1, 8):
        if r % t == 0 and t * c <= limit:
            best = t
    return best if best is not None else r


def _adam_math(w, g, m, v):
    m = ADAM_B1 * m + (1.0 - ADAM_B1) * g
    v = ADAM_B2 * v + (1.0 - ADAM_B2) * (g * g)
    m_hat = m / (1.0 - ADAM_B1 ** ADAM_STEP)
    v_hat = v / (1.0 - ADAM_B2 ** ADAM_STEP)
    delta = -ADAM_LR * (m_hat / (jnp.sqrt(v_hat) + ADAM_EPS) + ADAM_WD * w)
    return delta, m, v


def _adamw(w, m, v, sums, lands, chip, layer, prev, *, name):
    L, r, c = w.shape
    tr = _row_tile(r, c)
    has_prev = prev is not None

    def body(chip_ref, *refs):
        w_ref, m_ref, v_ref, s_ref, l_ref = refs[:5]
        g_ref, d_ref, nm_ref, nv_ref = refs[-4:]
        g = (s_ref[...].astype(F32) + l_ref[0].astype(F32)) + l_ref[1].astype(F32) + l_ref[2].astype(F32)
        delta, nm, nv = _adam_math(w_ref[...], g, m_ref[...], v_ref[...])
        g_ref[...] = g
        d_ref[...] = delta
        nm_ref[...] = nm
        nv_ref[...] = nv

    slab = pl.BlockSpec((None, tr, c), lambda i, chip_ref: (layer, i, 0))
    in_specs = [slab, slab, slab, pl.BlockSpec((None, tr, c), lambda i, chip_ref: (chip_ref[0], i, 0)),
                pl.BlockSpec((N_CHIP - 1, tr, c), lambda i, chip_ref: (0, i, 0))]
    args = [chip, w, m, v, sums, lands]
    aliases = {}
    if has_prev:
        in_specs += [ANY] * 4
        args += list(prev)
        aliases = {6 + n: n for n in range(4)}
    return pl.pallas_call(
        body, name=name,
        grid_spec=pltpu.PrefetchScalarGridSpec(
            num_scalar_prefetch=1, grid=(r // tr,), in_specs=in_specs, out_specs=(slab,) * 4),
        out_shape=tuple(jax.ShapeDtypeStruct((L, r, c), F32) for _ in range(4)),
        input_output_aliases=aliases, compiler_params=_params(("parallel",)),
    )(*args)


def _adamw_small(w, g, m, v):
    def body(w_ref, g_ref, m_ref, v_ref, d_ref, nm_ref, nv_ref):
        d_ref[...], nm_ref[...], nv_ref[...] = _adam_math(w_ref[...], g_ref[...], m_ref[...], v_ref[...])

    return pl.pallas_call(
        body, name="adamw_small", out_shape=tuple(jax.ShapeDtypeStruct(w.shape, F32) for _ in range(3)),
    )(w, g, m, v)


def _sum4(parts):
    def body(p_ref, o_ref):
        o_ref[...] = (p_ref[0] + p_ref[1]) + p_ref[2] + p_ref[3]

    return pl.pallas_call(
        body, name="sum_chips", out_shape=jax.ShapeDtypeStruct(parts.shape[1:], F32))(parts)


_WEIGHTS = ["mix_norm", "mlp_norm", "sb_w_in", "sb_w_out", "fox_w_in", "fox_b_f", "fox_q_gain", "fox_k_gain",
            "fox_w_out", "mla_w_in", "mla_q_norm", "mla_kv_norm", "mla_w_uq", "mla_w_ukv", "mla_q_gain",
            "mla_k_gain", "mla_w_out", "mlp_w1", "mlp_w2"]
_BIG = {"sb_w_in": "col", "sb_w_out": "row", "fox_w_in": "row", "fox_w_out": "row", "mla_w_in": "row",
        "mla_w_uq": "col", "mla_w_ukv": "col", "mla_w_out": "row", "mlp_w1": "col", "mlp_w2": "row"}


def _layer_big(i):
    kind, j = i % 3, i // 3
    mixer = {0: ["sb_w_in", "sb_w_out"], 1: ["fox_w_in", "fox_w_out"],
             2: ["mla_w_in", "mla_w_uq", "mla_w_ukv", "mla_w_out"]}[kind]
    return [(n, j) for n in mixer] + [("mlp_w1", i), ("mlp_w2", i)]


def _stack_to_cols(a):
    r = a.shape[0] // N_DEV
    return a.reshape(N_DEV, r, a.shape[1]).transpose(1, 0, 2).reshape(r, N_DEV * a.shape[1])


def _cols_to_stack(a):
    c = a.shape[1] // N_DEV
    return a.reshape(a.shape[0], N_DEV, c).transpose(1, 0, 2).reshape(N_DEV * a.shape[0], c)


def _pack_rows(rows, width):
    rows = [jnp.pad(r.reshape(-1).astype(F32), (0, width - r.size)) for r in rows]
    pad = (-len(rows)) % 8
    rows += [jnp.zeros((width,), F32)] * pad
    return jnp.stack(rows)


def kernel(x, positions, mix_norm, mlp_norm, sb_w_in, sb_w_out, fox_w_in, fox_b_f, fox_q_gain, fox_k_gain, fox_w_out, mla_w_in, mla_q_norm, mla_kv_norm, mla_w_uq, mla_w_ukv, mla_q_gain, mla_k_gain, mla_w_out, mlp_w1, mlp_w2, loss_target, m_mix_norm, m_mlp_norm, m_sb_w_in, m_sb_w_out, m_fox_w_in, m_fox_b_f, m_fox_q_gain, m_fox_k_gain, m_fox_w_out, m_mla_w_in, m_mla_q_norm, m_mla_kv_norm, m_mla_w_uq, m_mla_w_ukv, m_mla_q_gain, m_mla_k_gain, m_mla_w_out, m_mlp_w1, m_mlp_w2, v_mix_norm, v_mlp_norm, v_sb_w_in, v_sb_w_out, v_fox_w_in, v_fox_b_f, v_fox_q_gain, v_fox_k_gain, v_fox_w_out, v_mla_w_in, v_mla_q_norm, v_mla_kv_norm, v_mla_w_uq, v_mla_w_ukv, v_mla_q_gain, v_mla_k_gain, v_mla_w_out, v_mlp_w1, v_mlp_w2):
    w_in = dict(zip(_WEIGHTS, (mix_norm, mlp_norm, sb_w_in, sb_w_out, fox_w_in, fox_b_f, fox_q_gain, fox_k_gain, fox_w_out, mla_w_in, mla_q_norm, mla_kv_norm, mla_w_uq, mla_w_ukv, mla_q_gain, mla_k_gain, mla_w_out, mlp_w1, mlp_w2)))
    m_in = dict(zip(_WEIGHTS, (m_mix_norm, m_mlp_norm, m_sb_w_in, m_sb_w_out, m_fox_w_in, m_fox_b_f, m_fox_q_gain, m_fox_k_gain, m_fox_w_out, m_mla_w_in, m_mla_q_norm, m_mla_kv_norm, m_mla_w_uq, m_mla_w_ukv, m_mla_q_gain, m_mla_k_gain, m_mla_w_out, m_mlp_w1, m_mlp_w2)))
    v_in = dict(zip(_WEIGHTS, (v_mix_norm, v_mlp_norm, v_sb_w_in, v_sb_w_out, v_fox_w_in, v_fox_b_f, v_fox_q_gain, v_fox_k_gain, v_fox_w_out, v_mla_w_in, v_mla_q_norm, v_mla_kv_norm, v_mla_w_uq, v_mla_w_ukv, v_mla_q_gain, v_mla_k_gain, v_mla_w_out, v_mlp_w1, v_mlp_w2)))
    depth, D = mix_norm.shape
    H = D // HEAD_DIM
    n_mla = mla_w_in.shape[0]
    dev = 4 * lax.axis_index("x") + 2 * lax.axis_index("y") + lax.axis_index("c")
    core = lax.axis_index("c").astype(jnp.int32).reshape(1)
    dev_arr = dev.astype(jnp.int32).reshape(1)

    chip = (2 * lax.axis_index("x") + lax.axis_index("y")).astype(jnp.int32).reshape(1)
    nq, nkv = mla_q_norm.shape[1], mla_kv_norm.shape[1]

    units, u_mix, u_mlp = [], {}, {}
    for i in range(depth):
        big = _layer_big(i)
        u_mix[i] = len(units)
        if i < SPLIT_LAYERS:
            units += [(i, ("mix",), big[:-2]), (i, ("mlp",), big[-2:])]
        else:
            units.append((i, ("mix", "mlp"), big))
        u_mlp[i] = len(units) - 1

    def split(i):
        return u_mix[i] != u_mlp[i]

    def tag_of(u):
        i, parts, _ = units[u]
        return f"l{i}" if len(parts) == 2 else f"l{i}_{parts[0]}"

    kinds_of, shapes_of, ici = [], [], []
    for i, parts, names in units:
        shards = [w_in[n][j] for n, j in names]
        kinds = [_BIG[n] for n, _ in names]
        behind = (ici[-1].token,) if ici else ()
        fulls = [_own_in_place(s, k, dev_arr, BF16, behind) for s, k in zip(shards, kinds)]
        if i % 3 == 2:
            shards.append(_pack_rows([mla_q_norm[i // 3], mla_kv_norm[i // 3]], LANE))
            kinds.append("row")
            fulls.append(_own_in_place(shards[-1], "row", dev_arr, F32, behind))
        kinds_of.append(kinds)
        shapes_of.append([s.shape for s in shards])
        ici.append(_gather_ici_start(fulls, kinds, shapes_of[-1], tag_of(len(ici)),
                                     ici[-1].token if ici else None))

    full = {n: [None] * w_in[n].shape[0] for n in _WEIGHTS}
    for n in ("mix_norm", "mlp_norm", "fox_b_f", "fox_q_gain", "fox_k_gain", "mla_q_gain", "mla_k_gain"):
        full[n] = w_in[n]
    P = {"mix_norm": mix_norm, "mlp_norm": mlp_norm, "mlp": [None] * depth,
         "sb": [None] * sb_w_in.shape[0], "fox": [None] * fox_w_in.shape[0], "mla": [None] * n_mla}

    def forward_to_sibling(u, after):
        arrived = _wait_copies(ici[u], after)
        return _gather_d2d_start(arrived, kinds_of[u], shapes_of[u], tag_of(u))

    def finish_gather(u, flight, after):
        i, parts, names = units[u]
        got = _wait_copies(flight, after)
        for (n, j), a in zip(names, got):
            full[n][j] = _stack_to_cols(a) if n == "fox_w_in" else a
        if i % 3 == 2:
            tiles = got[-1].reshape(N_DEV, 8, LANE)
            full["mla_q_norm"][i // 3] = tiles[:, 0, :nq].reshape(-1)
            full["mla_kv_norm"][i // 3] = tiles[:, 1, :nkv].reshape(-1)
        _prepare_layer(P, full, i, H, parts)

    tabs = _rope_tables(positions[0])
    d2d = forward_to_sibling(0, ici[-1].token)
    finish_gather(0, d2d, d2d.token)
    xs, saved = x[0], []
    for i in range(depth):
        nxt = {}

        def hook(x_mid, i=i, nxt=nxt):
            if split(i):
                mlp_d2d = forward_to_sibling(u_mlp[i], x_mid)
                finish_gather(u_mlp[i], mlp_d2d, mlp_d2d.token)
            if 1 <= i < depth - 1 and not split(i + 1):
                nxt["d2d"] = forward_to_sibling(u_mix[i + 1], x_mid)
                return (nxt["d2d"].token,)
            return ()

        xs, s = _layer_fwd(i, xs, P, tabs, H, hook)
        saved.append(s)
        if i + 1 < depth:
            if "d2d" in nxt:
                finish_gather(u_mix[i + 1], nxt["d2d"], xs)
            else:
                mix_d2d = forward_to_sibling(u_mix[i + 1], xs)
                finish_gather(u_mix[i + 1], mix_d2d, mix_d2d.token)
    dx, dxb, loss_part = _loss_head(xs, loss_target[0])

    def unit_meta(u):
        names = units[u][2]
        return [_BIG[n] for n, _ in names], [w_in[n].shape[1:] for n, _ in names]

    def to_sibling_start(u, Gl):
        grads = [(_cols_to_stack(Gl[n]) if n == "fox_w_in" else Gl[n]) for n, _ in units[u][2]]
        return _scatter_d2d_start(grads, *unit_meta(u), tag_of(u))

    def pair_and_send(u, flight, after, before=None):
        n = len(units[u][2])
        kinds, shapes = unit_meta(u)
        got = _wait_copies(flight, after)
        sums = [_pair_add(g, r4, k, s, core, name="pair_add_" + tag_of(u))
                for g, r4, k, s in zip(got[:n], got[n:], kinds, shapes)]
        return _scatter_ici_start(sums, shapes, tag_of(u), before)

    G, to_chips, deps, flying = [None] * depth, [None] * len(units), (), {}
    for i in reversed(range(depth)):
        def hook(dxb_mid, Gp, i=i):
            out = ()
            if i + 1 < depth:
                u = u_mix[i + 1]
                to_chips[u] = pair_and_send(u, flying.pop(u), dxb_mid)
                out = (to_chips[u].token,)
            if split(i):
                flying[u_mlp[i]] = to_sibling_start(u_mlp[i], {"mlp_w1": Gp["w1"], "mlp_w2": Gp["w2"]})
                out = out + (flying[u_mlp[i]].token,)
            return out

        def hook2(dv, i=i):
            if not split(i):
                return ()
            u = u_mlp[i]
            to_chips[u] = pair_and_send(u, flying.pop(u), dv)
            return (to_chips[u].token,)

        dx, dxb, Gp = _layer_bwd(i, dx, dxb, saved[i], P, tabs, H, deps, hook, hook2)
        G[i] = _layer_grads(i, Gp, H)
        flying[u_mix[i]] = to_sibling_start(u_mix[i], G[i])
        deps = (flying[u_mix[i]].token,)
    grad_x = dx
    to_chips[0] = pair_and_send(0, flying.pop(0), dx)

    small = [n for n in _WEIGHTS if n not in _BIG]
    rows = []
    for n in small:
        for j in range(w_in[n].shape[0]):
            rows.append(G[_layer_of(n, j)][n])
    loss_row = len(rows)
    rows.append(loss_part)
    small_flight = _allreduce_small_start(_pack_rows(rows, D) + to_chips[0].token[0, 0], chip)

    results, after = {}, [small_flight.token]
    for u in reversed(range(len(units))):
        names = units[u][2]
        got = _wait_copies(to_chips[u], after)
        for a, (n, j) in enumerate(names):
            results[n] = _adamw(w_in[n], m_in[n], v_in[n], got[a], got[len(names) + a], chip, j, results.get(n),
                                name="adamw_" + n)
        after = [results[n][0] for n, _ in names]

    (parts,) = _wait_copies(small_flight, after)
    sgr = _sum4(parts)
    g_rows, w_rows, m_rows, v_rows, at = [], [], [], [], 0
    for n in small:
        for j in range(w_in[n].shape[0]):
            width = w_in[n].shape[1]
            if n in ("mla_q_norm", "mla_kv_norm"):
                g_rows.append(lax.dynamic_slice(sgr[at], (dev * width,), (width,)))
            else:
                g_rows.append(sgr[at, :width])
            w_rows.append(w_in[n][j])
            m_rows.append(m_in[n][j])
            v_rows.append(v_in[n][j])
            at += 1
    g_pack = _pack_rows(g_rows, D)
    d_pack, nm_pack, nv_pack = _adamw_small(
        _pack_rows(w_rows, D), g_pack, _pack_rows(m_rows, D), _pack_rows(v_rows, D))
    at = 0
    for n in small:
        L, width = w_in[n].shape
        results[n] = tuple(p[at:at + L, :width] for p in (g_pack, d_pack, nm_pack, nv_pack))
        at += L

    out = [sgr[loss_row, 0], grad_x[None]]
    for part in range(4):
        out += [results[n][part] for n in _WEIGHTS]
    return tuple(out)
```

```python
import functools
import math

import jax
import jax.numpy as jnp
from jax import lax
from jax.experimental import pallas as pl
from jax.experimental.pallas import tpu as pltpu

F32 = jnp.float32
BF16 = jnp.bfloat16

HEAD_DIM = 128
MLA_NOPE = 128
MLA_ROPE = 64
MLA_V = 128
MLA_QK = MLA_NOPE + MLA_ROPE
MLA_QK_PAD = 256
LANE = 128
ROPE_THETA = 10000.0
EPS = 1e-6
ADAM_LR = 0.001
ADAM_B1 = 0.9
ADAM_B2 = 0.999
ADAM_EPS = 1e-08
ADAM_WD = 0.01
ADAM_STEP = 10
N_DEV = 8
N_CHIP = 4
NEG = -1e30
VMEM_LIMIT = 56 * 1024 * 1024
ATT_BLOCK = 256
SB_BLOCK = 256
ATT_HEADS = 4
ATT_HEADS_FWD = 8
SPLIT_LAYERS = 2
SB_PARTS = 2
MESH = pl.DeviceIdType.MESH
ANY = pl.BlockSpec(memory_space=pl.ANY)


def _params(sem):
    return pltpu.CompilerParams(dimension_semantics=sem, vmem_limit_bytes=VMEM_LIMIT)


def _tile(dim, pref):
    if dim <= pref:
        return dim
    t = pref
    while dim % t:
        t -= LANE
    return t


def _dot(a, b, dims):
    return lax.dot_general(a, b, (dims, ((), ())), preferred_element_type=F32)


NN = ((1,), (0,))
NT = ((1,), (1,))
TN = ((0,), (0,))


def _mm(a, b, *, mode="nn", out_dtype=F32, res=None, act=None, z=None, name,
        tm=1024, tn=1024, tk=2048, deps=()):
    if mode == "nn":
        (M, K), (_, N) = a.shape, b.shape
    elif mode == "nt":
        (M, K), (N, _) = a.shape, b.shape
    else:
        (K, M), (_, N) = a.shape, b.shape
    tm, tn, tk = _tile(M, tm), _tile(N, tn), _tile(K, tk)
    nk = K // tk
    if mode == "tn":
        a_spec = pl.BlockSpec((tk, tm), lambda i, j, k: (k, i))
    else:
        a_spec = pl.BlockSpec((tm, tk), lambda i, j, k: (i, k))
    if mode == "nt":
        b_spec = pl.BlockSpec((tn, tk), lambda i, j, k: (j, k))
    else:
        b_spec = pl.BlockSpec((tk, tn), lambda i, j, k: (k, j))
    dims = {"nn": NN, "nt": NT, "tn": TN}[mode]
    o_spec = pl.BlockSpec((tm, tn), lambda i, j, k: (i, j))
    in_specs, args = [a_spec, b_spec], [a, b]
    if res is not None:
        in_specs.append(o_spec)
        args.append(res)
    if act == "drelu2":
        in_specs.append(o_spec)
        args.append(z)
    if act == "relu2":
        out_shape = (jax.ShapeDtypeStruct((M, N), F32), jax.ShapeDtypeStruct((M, N), BF16))
        out_specs = (o_spec, o_spec)
    else:
        out_shape = jax.ShapeDtypeStruct((M, N), out_dtype)
        out_specs = o_spec
    has_res, has_z = res is not None, act == "drelu2"
    n_out = 2 if act == "relu2" else 1
    in_specs += [ANY] * len(deps)
    args += list(deps)

    def body(*refs):
        a_ref, b_ref = refs[0], refs[1]
        idx = 2
        res_ref = z_ref = None
        if has_res:
            res_ref = refs[idx]
            idx += 1
        if has_z:
            z_ref = refs[idx]
            idx += 1
        idx += len(deps)
        outs = refs[idx:idx + n_out]

        def finish(r):
            if has_res:
                r = r + res_ref[...]
            if act == "relu2":
                outs[0][...] = r
                rr = jnp.maximum(r, 0.0)
                outs[1][...] = (rr * rr).astype(BF16)
            elif act == "drelu2":
                outs[0][...] = (r * (2.0 * jnp.maximum(z_ref[...], 0.0))).astype(out_dtype)
            else:
                outs[0][...] = r.astype(out_dtype)

        if nk == 1:
            finish(_dot(a_ref[...], b_ref[...], dims))
            return
        acc = refs[-1]
        k = pl.program_id(2)

        @pl.when(k == 0)
        def _():
            acc[...] = _dot(a_ref[...], b_ref[...], dims)

        @pl.when((k > 0) & (k < nk - 1))
        def _():
            acc[...] += _dot(a_ref[...], b_ref[...], dims)

        @pl.when(k == nk - 1)
        def _():
            finish(acc[...] + _dot(a_ref[...], b_ref[...], dims))

    return pl.pallas_call(
        body, name=name, grid=(M // tm, N // tn, nk), in_specs=in_specs, out_specs=out_specs,
        out_shape=out_shape, scratch_shapes=[pltpu.VMEM((tm, tn), F32)] if nk > 1 else [],
        compiler_params=_params(("parallel", "parallel", "arbitrary")),
    )(*args)


def _rmsnorm_fwd(x, g, *, name, tm=256):
    T, n = x.shape
    tm = _tile(T, tm)

    def body(x_ref, g_ref, o_ref):
        xf = x_ref[...]
        r = lax.rsqrt(jnp.mean(xf * xf, axis=-1, keepdims=True) + EPS)
        o_ref[...] = (xf * r * g_ref[...]).astype(BF16)

    return pl.pallas_call(
        body, name=name, grid=(T // tm,),
        in_specs=[pl.BlockSpec((tm, n), lambda i: (i, 0)), pl.BlockSpec((1, n), lambda i: (0, 0))],
        out_specs=pl.BlockSpec((tm, n), lambda i: (i, 0)),
        out_shape=jax.ShapeDtypeStruct((T, n), BF16),
        compiler_params=_params(("parallel",)),
    )(x, g)


def _rmsnorm_bwd(x, g, dy, dx_in=None, *, name, want_f32=True, tm=256):
    T, n = x.shape
    tm = _tile(T, tm)
    has_in = dx_in is not None
    row = pl.BlockSpec((tm, n), lambda i: (i, 0))
    vec = pl.BlockSpec((1, n), lambda i: (0, 0))

    def body(*refs):
        x_ref, g_ref, dy_ref = refs[:3]
        idx = 3
        in_ref = None
        if has_in:
            in_ref = refs[idx]
            idx += 1
        outs = refs[idx:]
        xf = x_ref[...]
        r = lax.rsqrt(jnp.mean(xf * xf, axis=-1, keepdims=True) + EPS)
        dyf = dy_ref[...]
        t = dyf * g_ref[...]
        dx = r * t - xf * (r * r * r * jnp.mean(t * xf, axis=-1, keepdims=True))
        if has_in:
            dx = dx + in_ref[...]
        o = 0
        if want_f32:
            outs[0][...] = dx
            o = 1
        outs[o][...] = dx.astype(BF16)
        dg_ref = outs[o + 1]

        @pl.when(pl.program_id(0) == 0)
        def _():
            dg_ref[...] = jnp.zeros_like(dg_ref)

        dg_ref[...] += jnp.sum(dyf * xf * r, axis=0, keepdims=True)

    in_specs = [row, vec, row] + ([row] if has_in else [])
    out_specs, out_shape = [], []
    if want_f32:
        out_specs.append(row)
        out_shape.append(jax.ShapeDtypeStruct((T, n), F32))
    out_specs += [row, vec]
    out_shape += [jax.ShapeDtypeStruct((T, n), BF16), jax.ShapeDtypeStruct((1, n), F32)]
    args = [x, g, dy] + ([dx_in] if has_in else [])
    return pl.pallas_call(
        body, name=name, grid=(T // tm,), in_specs=in_specs, out_specs=tuple(out_specs),
        out_shape=tuple(out_shape), compiler_params=_params(("arbitrary",)),
    )(*args)


def _loss_head(y, target, *, tm=256):
    T, n = y.shape
    tm = _tile(T, tm)
    row = pl.BlockSpec((tm, n), lambda i: (i, 0))

    def body(y_ref, t_ref, dy_ref, dyb_ref, loss_ref):
        err = y_ref[...] - t_ref[...]
        dy = err * (1.0 / n)
        dy_ref[...] = dy
        dyb_ref[...] = dy.astype(BF16)

        @pl.when(pl.program_id(0) == 0)
        def _():
            loss_ref[...] = jnp.zeros_like(loss_ref)

        part = 0.5 * jnp.sum(jnp.mean(err * err, axis=-1, keepdims=True), axis=0, keepdims=True)
        loss_ref[...] += part

    return pl.pallas_call(
        body, name="loss_head", grid=(T // tm,), in_specs=[row, row],
        out_specs=(row, row, pl.BlockSpec((1, 1), lambda i: (0, 0))),
        out_shape=(jax.ShapeDtypeStruct((T, n), F32), jax.ShapeDtypeStruct((T, n), BF16),
                   jax.ShapeDtypeStruct((1, 1), F32)),
        compiler_params=_params(("arbitrary",)),
    )(y, target)


def _split(v, n):
    parts = []
    for _ in range(n - 1):
        p = v.astype(BF16)
        parts.append(p)
        v = v - p.astype(F32)
    parts.append(v.astype(BF16))
    return parts


def _tri_left(tri, v):
    hi, mid, lo = _split(v, 3)
    return _dot(tri, hi, NN) + _dot(tri, mid, NN) + _dot(tri, lo, NN)


def _iota2(shape, dim):
    return lax.broadcasted_iota(jnp.int32, shape, dim)


def _log_sigmoid(z):
    return jnp.minimum(z, 0.0) - jnp.log1p(jnp.exp(-jnp.abs(z)))


def _log_sigmoid_abs(z):
    return jnp.minimum(z, 0.0) - jnp.log(1.0 + jnp.exp(-jnp.abs(z)))


def _head_spec(rows, width, off):
    return pl.BlockSpec((rows, width), lambda h, i: (i, off + h))


def _full_head_spec(T, width, off):
    return pl.BlockSpec((T, width), lambda h, i: (0, off + h))


def _sb_fwd(qkv, H):
    T = qkv.shape[0]
    B = _tile(T, SB_BLOCK)
    nq = T // B
    scale = 1.0 / math.sqrt(HEAD_DIM)

    HG = ATT_HEADS_FWD if H % ATT_HEADS_FWD == 0 else 1
    HD = HEAD_DIM

    def body(q_ref, k_ref, v_ref, o_ref, c_ref, tri_s):
        i = pl.program_id(1)
        tri = (_iota2((B, B), 0) > _iota2((B, B), 1)).astype(BF16)
        for p in range(SB_PARTS):
            tri_s[pl.ds(p * B, B), :] = tri
        rows = _iota2((B, B), 0)
        cols = _iota2((B, B), 1)

        def step(n, carry):
            j = i - n
            ks = pl.multiple_of(j * B, B)
            mask = (ks + cols) < (i * B + rows)
            out = []
            for g in range(HG):
                acc, run = carry[g]
                q = q_ref[:, pl.ds(g * HD, HD)]
                kb = k_ref[pl.ds(ks, B), pl.ds(g * HD, HD)]
                vb = v_ref[pl.ds(ks, B), pl.ds(g * HD, HD)]
                z = _dot(q, kb, NT) * scale
                ls = _log_sigmoid_abs(z)
                lk = jnp.where(mask, ls - z, 0.0)
                parts = jnp.concatenate(_split(lk, SB_PARTS), axis=1)
                later = _dot(parts, tri_s[...], NN)
                a = jnp.exp(jnp.where(mask, ls, NEG) + later).astype(BF16)
                acc = acc + jnp.exp(run) * _dot(a, vb, NN)
                out.append((acc, run + jnp.sum(lk, axis=1, keepdims=True)))
            return tuple(out)

        init = tuple((jnp.zeros((B, HD), F32), jnp.zeros((B, 1), F32)) for _ in range(HG))
        res = lax.fori_loop(0, i + 1, step, init)
        for g in range(HG):
            o_ref[:, pl.ds(g * HD, HD)] = res[g][0].astype(BF16)
            c_ref[g] = res[g][1]

    return pl.pallas_call(
        body, name="sb_attn_fwd", grid=(H // HG, nq),
        in_specs=[_head_spec(B, HG * HD, 0), _full_head_spec(T, HG * HD, H // HG),
                  _full_head_spec(T, HG * HD, 2 * H // HG)],
        out_specs=(_head_spec(B, HG * HD, 0), pl.BlockSpec((HG, B, 1), lambda h, i: (h, i, 0))),
        out_shape=(jax.ShapeDtypeStruct((T, H * HD), BF16), jax.ShapeDtypeStruct((H, T, 1), F32)),
        scratch_shapes=[pltpu.VMEM((SB_PARTS * B, B), BF16)],
        compiler_params=_params(("parallel", "arbitrary")),
    )(qkv, qkv, qkv)


def _sb_bwd(qkv, do, ctot, H):
    T = qkv.shape[0]
    B = _tile(T, SB_BLOCK)
    nq = T // B
    scale = 1.0 / math.sqrt(HEAD_DIM)
    HG = ATT_HEADS if H % ATT_HEADS == 0 else 1
    HD = HEAD_DIM

    def body(q_ref, k_ref, v_ref, do_ref, c_ref, dq_ref, dk_ref, dv_ref, dk_acc, dv_acc,
             tri_incl_s, tri_strict_s):
        i = pl.program_id(1)

        @pl.when(i == 0)
        def _():
            dk_acc[...] = jnp.zeros_like(dk_acc)
            dv_acc[...] = jnp.zeros_like(dv_acc)

        rows = _iota2((B, B), 0)
        cols = _iota2((B, B), 1)
        tri_incl = (rows <= cols).astype(BF16)
        tri_strict = (rows < cols).astype(BF16)
        for p in range(SB_PARTS):
            tri_incl_s[pl.ds(p * B, B), :] = tri_incl
            tri_strict_s[pl.ds(p * B, B), :] = tri_strict
        heads = [(q_ref[:, pl.ds(g * HD, HD)], do_ref[:, pl.ds(g * HD, HD)], c_ref[g]) for g in range(HG)]

        def step(j, carry):
            ks = pl.multiple_of(j * B, B)
            mask = (ks + cols) < (i * B + rows)
            out = []
            for g in range(HG):
                dq, lpre, gpre = carry[g]
                q, do_b, ctot_b = heads[g]
                kb = k_ref[pl.ds(ks, B), pl.ds(g * HD, HD)]
                vb = v_ref[pl.ds(ks, B), pl.ds(g * HD, HD)]
                z = _dot(q, kb, NT) * scale
                da = _dot(do_b, vb, NT)
                ls = _log_sigmoid_abs(z)
                lk = jnp.where(mask, ls - z, 0.0)
                lsm = jnp.where(mask, ls, NEG)
                incl = _dot(jnp.concatenate(_split(lk, SB_PARTS), axis=1), tri_incl_s[...], NN)
                a = jnp.exp(lsm + (ctot_b - lpre) - incl)
                gw = a * da
                gex = gpre + _dot(jnp.concatenate(_split(gw, SB_PARTS), axis=1), tri_strict_s[...], NN)
                dzb = ((gw * jnp.exp(lsm - z) - jnp.exp(lsm) * gex) * scale).astype(BF16)
                dq = dq + _dot(dzb, kb, NN)
                dk_acc[pl.ds(ks, B), pl.ds(g * HD, HD)] += _dot(dzb, q, TN)
                dv_acc[pl.ds(ks, B), pl.ds(g * HD, HD)] += _dot(a.astype(BF16), do_b, TN)
                out.append((dq, lpre + jnp.sum(lk, axis=1, keepdims=True),
                            gpre + jnp.sum(gw, axis=1, keepdims=True)))
            return tuple(out)

        init = tuple((jnp.zeros((B, HD), F32), jnp.zeros((B, 1), F32), jnp.zeros((B, 1), F32))
                     for _ in range(HG))
        res = lax.fori_loop(0, i + 1, step, init)
        for g in range(HG):
            dq_ref[:, pl.ds(g * HD, HD)] = res[g][0].astype(BF16)

        @pl.when(i == nq - 1)
        def _():
            dk_ref[...] = dk_acc[...].astype(BF16)
            dv_ref[...] = dv_acc[...].astype(BF16)

    W = H * HD
    return pl.pallas_call(
        body, name="sb_attn_bwd", grid=(H // HG, nq),
        in_specs=[_head_spec(B, HG * HD, 0), _full_head_spec(T, HG * HD, H // HG),
                  _full_head_spec(T, HG * HD, 2 * H // HG), _head_spec(B, HG * HD, 0),
                  pl.BlockSpec((HG, B, 1), lambda h, i: (h, i, 0))],
        out_specs=(_head_spec(B, HG * HD, 0), _full_head_spec(T, HG * HD, 0),
                   _full_head_spec(T, HG * HD, 0)),
        out_shape=tuple(jax.ShapeDtypeStruct((T, W), BF16) for _ in range(3)),
        scratch_shapes=[pltpu.VMEM((T, HG * HD), F32), pltpu.VMEM((T, HG * HD), F32),
                        pltpu.VMEM((SB_PARTS * B, B), BF16), pltpu.VMEM((SB_PARTS * B, B), BF16)],
        compiler_params=_params(("parallel", "arbitrary")),
    )(qkv, qkv, qkv, do, ctot)


def _softmax_fwd(q, k, v, cf, H, dqk, scale, *, name):
    T = q.shape[0]
    B = _tile(T, ATT_BLOCK)
    nq = T // B
    has_cf = cf is not None
    HG = ATT_HEADS_FWD if H % ATT_HEADS_FWD == 0 else 1
    HD = HEAD_DIM

    def body(*refs):
        q_ref, k_ref, v_ref = refs[:3]
        idx = 3
        if has_cf:
            cfc_ref, cfr_ref = refs[3], refs[4]
            idx = 5
        o_ref, of_ref, lse_ref = refs[idx:idx + 3]
        i = pl.program_id(1)
        rows = _iota2((B, B), 0)
        cols = _iota2((B, B), 1)

        def step(j, carry):
            ks = pl.multiple_of(j * B, B)
            mask = (ks + cols) <= (i * B + rows)
            out = []
            for g in range(HG):
                m, l, acc = carry[g]
                qb = q_ref[:, pl.ds(g * dqk, dqk)]
                kb = k_ref[pl.ds(ks, B), pl.ds(g * dqk, dqk)]
                vb = v_ref[pl.ds(ks, B), pl.ds(g * HD, HD)]
                s = _dot(qb, kb, NT) * scale
                if has_cf:
                    s = s + (cfc_ref[g] - cfr_ref[g, :, pl.ds(ks, B)])
                s = jnp.where(mask, s, NEG)
                m_new = jnp.maximum(m, jnp.max(s, axis=1, keepdims=True))
                alpha = jnp.exp(m - m_new)
                p = jnp.exp(s - m_new)
                l = alpha * l + jnp.sum(p, axis=1, keepdims=True)
                acc = alpha * acc + _dot(p.astype(BF16), vb, NN)
                out.append((m_new, l, acc))
            return tuple(out)

        init = tuple((jnp.full((B, 1), NEG, F32), jnp.zeros((B, 1), F32), jnp.zeros((B, HD), F32))
                     for _ in range(HG))
        res = lax.fori_loop(0, i + 1, step, init)
        for g in range(HG):
            m, l, acc = res[g]
            o = acc / l
            o_ref[:, pl.ds(g * HD, HD)] = o.astype(BF16)
            of_ref[:, pl.ds(g * HD, HD)] = o
            lse_ref[g] = m + jnp.log(l)

    stat = pl.BlockSpec((HG, B, 1), lambda h, i: (h, i, 0))
    in_specs = [_head_spec(B, HG * dqk, 0), _full_head_spec(T, HG * dqk, 0), _full_head_spec(T, HG * HD, 0)]
    args = [q, k, v]
    if has_cf:
        in_specs += [stat, pl.BlockSpec((HG, 1, T), lambda h, i: (h, 0, 0))]
        args += list(cf)
    W = H * HD
    return pl.pallas_call(
        body, name=name, grid=(H // HG, nq), in_specs=in_specs,
        out_specs=(_head_spec(B, HG * HD, 0), _head_spec(B, HG * HD, 0), stat),
        out_shape=(jax.ShapeDtypeStruct((T, W), BF16), jax.ShapeDtypeStruct((T, W), F32),
                   jax.ShapeDtypeStruct((H, T, 1), F32)),
        compiler_params=_params(("parallel", "arbitrary")),
    )(*args)


def _softmax_bwd(q, k, v, cf, do, o, lse, H, dqk, scale, *, name):
    T = q.shape[0]
    B = _tile(T, ATT_BLOCK)
    nq = T // B
    has_cf = cf is not None
    HG = ATT_HEADS if H % ATT_HEADS == 0 else 1
    HD = HEAD_DIM

    def body(*refs):
        q_ref, k_ref, v_ref, do_ref, o_ref, lse_ref = refs[:6]
        idx = 6
        if has_cf:
            cfc_ref, cfr_ref = refs[6], refs[7]
            idx = 8
        dq_ref, dk_ref, dv_ref = refs[idx:idx + 3]
        idx += 3
        if has_cf:
            dcc_ref, dcr_ref = refs[idx], refs[idx + 1]
        i = pl.program_id(1)

        @pl.when(i == 0)
        def _():
            dk_ref[...] = jnp.zeros_like(dk_ref)
            dv_ref[...] = jnp.zeros_like(dv_ref)
            if has_cf:
                dcr_ref[...] = jnp.zeros_like(dcr_ref)

        rows = _iota2((B, B), 0)
        cols = _iota2((B, B), 1)
        heads = []
        for g in range(HG):
            do_b = do_ref[:, pl.ds(g * HD, HD)]
            delta = jnp.sum(do_b.astype(F32) * o_ref[:, pl.ds(g * HD, HD)], axis=1, keepdims=True)
            heads.append((q_ref[:, pl.ds(g * dqk, dqk)], do_b, lse_ref[g], delta))

        def step(j, carry):
            ks = pl.multiple_of(j * B, B)
            mask = (ks + cols) <= (i * B + rows)
            out = []
            for g in range(HG):
                dq, rs = carry[g]
                qb, do_b, lse_b, delta = heads[g]
                kb = k_ref[pl.ds(ks, B), pl.ds(g * dqk, dqk)]
                vb = v_ref[pl.ds(ks, B), pl.ds(g * HD, HD)]
                s = _dot(qb, kb, NT) * scale
                if has_cf:
                    s = s + (cfc_ref[g] - cfr_ref[g, :, pl.ds(ks, B)])
                p = jnp.where(mask, jnp.exp(s - lse_b), 0.0)
                dp = _dot(do_b, vb, NT)
                ds = p * (dp - delta)
                dsb = (ds * scale).astype(BF16)
                dq = dq + _dot(dsb, kb, NN)
                dk_ref[pl.ds(ks, B), pl.ds(g * dqk, dqk)] += _dot(dsb, qb, TN)
                dv_ref[pl.ds(ks, B), pl.ds(g * HD, HD)] += _dot(p.astype(BF16), do_b, TN)
                if has_cf:
                    rs = rs + jnp.sum(ds, axis=1, keepdims=True)
                    dcr_ref[g, :, pl.ds(ks, B)] -= jnp.sum(ds, axis=0, keepdims=True)
                out.append((dq, rs))
            return tuple(out)

        init = tuple((jnp.zeros((B, dqk), F32), jnp.zeros((B, 1), F32)) for _ in range(HG))
        res = lax.fori_loop(0, i + 1, step, init)
        for g in range(HG):
            dq_ref[:, pl.ds(g * dqk, dqk)] = res[g][0]
            if has_cf:
                dcc_ref[g] = res[g][1]

    stat = pl.BlockSpec((HG, B, 1), lambda h, i: (h, i, 0))
    rowstat = pl.BlockSpec((HG, 1, T), lambda h, i: (h, 0, 0))
    in_specs = [_head_spec(B, HG * dqk, 0), _full_head_spec(T, HG * dqk, 0), _full_head_spec(T, HG * HD, 0),
                _head_spec(B, HG * HD, 0), _head_spec(B, HG * HD, 0), stat]
    args = [q, k, v, do, o, lse]
    out_specs = [_head_spec(B, HG * dqk, 0), _full_head_spec(T, HG * dqk, 0), _full_head_spec(T, HG * HD, 0)]
    out_shape = [jax.ShapeDtypeStruct((T, H * dqk), F32), jax.ShapeDtypeStruct((T, H * dqk), F32),
                 jax.ShapeDtypeStruct((T, H * HD), F32)]
    if has_cf:
        in_specs += [stat, rowstat]
        args += list(cf)
        out_specs += [stat, rowstat]
        out_shape += [jax.ShapeDtypeStruct((H, T, 1), F32), jax.ShapeDtypeStruct((H, 1, T), F32)]
    return pl.pallas_call(
        body, name=name, grid=(H // HG, nq), in_specs=in_specs, out_specs=tuple(out_specs),
        out_shape=tuple(out_shape), compiler_params=_params(("parallel", "arbitrary")),
    )(*args)


def _headnorm(x, g):
    r = lax.rsqrt(jnp.mean(x * x, axis=-1, keepdims=True) + EPS)
    return x * r * g, r


def _headnorm_bwd(x, g, dy, n):
    r = lax.rsqrt(jnp.sum(x * x, axis=-1, keepdims=True) * (1.0 / n) + EPS)
    t = dy * g
    dx = r * t - x * (r * r * r * (jnp.sum(t * x, axis=-1, keepdims=True) * (1.0 / n)))
    dg = jnp.sum(dy * x * r, axis=0, keepdims=True)
    return dx, dg


def _fox_prep_fwd(qkv, gq, gk, H, *, tm=512):
    T = qkv.shape[0]
    tm = _tile(T, tm)

    def body(q_ref, k_ref, v_ref, gq_ref, gk_ref, qn_ref, kn_ref, vb_ref):
        qn_ref[...] = _headnorm(q_ref[...], gq_ref[...])[0].astype(BF16)
        kn_ref[...] = _headnorm(k_ref[...], gk_ref[...])[0].astype(BF16)
        vb_ref[...] = v_ref[...].astype(BF16)

    def blk(off):
        return pl.BlockSpec((tm, HEAD_DIM), lambda i, h: (i, off + h))

    vec = pl.BlockSpec((1, HEAD_DIM), lambda i, h: (0, 0))
    W = H * HEAD_DIM
    return pl.pallas_call(
        body, name="fox_prep_fwd", grid=(T // tm, H),
        in_specs=[blk(0), blk(H), blk(2 * H), vec, vec], out_specs=(blk(0), blk(0), blk(0)),
        out_shape=tuple(jax.ShapeDtypeStruct((T, W), BF16) for _ in range(3)),
        compiler_params=_params(("parallel", "parallel")),
    )(qkv, qkv, qkv, gq, gk)


def _fox_prep_bwd(qkv, gq, gk, dqn, dkn, dv, H, *, tm=512):
    T = qkv.shape[0]
    tm = _tile(T, tm)

    def body(q_ref, k_ref, gq_ref, gk_ref, dqn_ref, dkn_ref, dv_ref,
             dq_ref, dk_ref, dvb_ref, dgq_ref, dgk_ref):
        @pl.when((pl.program_id(0) == 0) & (pl.program_id(1) == 0))
        def _():
            dgq_ref[...] = jnp.zeros_like(dgq_ref)
            dgk_ref[...] = jnp.zeros_like(dgk_ref)

        dq, dgq = _headnorm_bwd(q_ref[...], gq_ref[...], dqn_ref[...], HEAD_DIM)
        dk, dgk = _headnorm_bwd(k_ref[...], gk_ref[...], dkn_ref[...], HEAD_DIM)
        dq_ref[...] = dq.astype(BF16)
        dk_ref[...] = dk.astype(BF16)
        dvb_ref[...] = dv_ref[...].astype(BF16)
        dgq_ref[...] += dgq
        dgk_ref[...] += dgk

    def blk(off):
        return pl.BlockSpec((tm, HEAD_DIM), lambda i, h: (i, off + h))

    vec = pl.BlockSpec((1, HEAD_DIM), lambda i, h: (0, 0))
    W = H * HEAD_DIM
    return pl.pallas_call(
        body, name="fox_prep_bwd", grid=(T // tm, H),
        in_specs=[blk(0), blk(H), vec, vec, blk(0), blk(0), blk(0)],
        out_specs=(blk(0), blk(0), blk(0), vec, vec),
        out_shape=tuple(jax.ShapeDtypeStruct((T, W), BF16) for _ in range(3))
        + (jax.ShapeDtypeStruct((1, HEAD_DIM), F32), jax.ShapeDtypeStruct((1, HEAD_DIM), F32)),
        compiler_params=_params(("arbitrary", "arbitrary")),
    )(qkv, qkv, gq, gk, dqn, dkn, dv)


def _fox_gate_fwd(flog, bf, *, tm=256):
    T = flog.shape[0]
    tm = _tile(T, tm)

    def body(f_ref, b_ref, cf_ref, carry):
        @pl.when(pl.program_id(0) == 0)
        def _():
            carry[...] = jnp.zeros_like(carry)

        lf = _log_sigmoid(f_ref[...] + b_ref[...])
        tri = (_iota2((tm, tm), 1) <= _iota2((tm, tm), 0)).astype(BF16)
        cf_ref[...] = carry[...] + _tri_left(tri, lf)
        carry[...] += jnp.sum(lf, axis=0, keepdims=True)

    return pl.pallas_call(
        body, name="fox_gate_fwd", grid=(T // tm,),
        in_specs=[pl.BlockSpec((tm, LANE), lambda i: (i, 0)), pl.BlockSpec((1, LANE), lambda i: (0, 0))],
        out_specs=pl.BlockSpec((tm, LANE), lambda i: (i, 0)),
        out_shape=jax.ShapeDtypeStruct((T, LANE), F32),
        scratch_shapes=[pltpu.VMEM((1, LANE), F32)],
        compiler_params=_params(("arbitrary",)),
    )(flog, bf)


def _fox_gate_bwd(flog, bf, dcf, *, tm=256):
    T = flog.shape[0]
    tm = _tile(T, tm)
    nt = T // tm

    def body(f_ref, b_ref, dcf_ref, df_ref, db_ref, carry):
        @pl.when(pl.program_id(0) == 0)
        def _():
            carry[...] = jnp.zeros_like(carry)
            db_ref[...] = jnp.zeros_like(db_ref)

        d = dcf_ref[...]
        tri = (_iota2((tm, tm), 1) >= _iota2((tm, tm), 0)).astype(BF16)
        dlf = carry[...] + _tri_left(tri, d)
        carry[...] += jnp.sum(d, axis=0, keepdims=True)
        xg = f_ref[...] + b_ref[...]
        e = jnp.exp(-jnp.abs(xg))
        sig_neg = jnp.where(xg >= 0.0, e, 1.0) / (1.0 + e)
        df = dlf * sig_neg
        df_ref[...] = df.astype(BF16)
        db_ref[...] += jnp.sum(df, axis=0, keepdims=True)

    rev = pl.BlockSpec((tm, LANE), lambda i: (nt - 1 - i, 0))
    vec = pl.BlockSpec((1, LANE), lambda i: (0, 0))
    return pl.pallas_call(
        body, name="fox_gate_bwd", grid=(nt,), in_specs=[rev, vec, rev], out_specs=(rev, vec),
        out_shape=(jax.ShapeDtypeStruct((T, LANE), BF16), jax.ShapeDtypeStruct((1, LANE), F32)),
        scratch_shapes=[pltpu.VMEM((1, LANE), F32)],
        compiler_params=_params(("arbitrary",)),
    )(flog, bf, dcf)


def _rope_tables(positions):
    half = MLA_ROPE // 2
    inv_freq = ROPE_THETA ** (-jnp.arange(0, half, dtype=F32) * 2.0 / MLA_ROPE)
    ang = positions.astype(F32)[:, None] * inv_freq
    cos, sin = jnp.cos(ang), jnp.sin(ang)
    zero = jnp.zeros_like(cos)
    pad = jnp.zeros((positions.shape[0], LANE - MLA_ROPE), F32)
    cos_t = jnp.concatenate([cos, cos, pad], axis=1)
    sin_up = jnp.concatenate([zero, sin, pad], axis=1)
    sin_dn = jnp.concatenate([-sin, zero, pad], axis=1)
    return cos_t, sin_up, sin_dn


def _rope(x, cos_t, sin_up, sin_dn):
    half = MLA_ROPE // 2
    return x * cos_t + pltpu.roll(x, half, 1) * sin_up + pltpu.roll(x, LANE - half, 1) * sin_dn


def _rope_t(d, cos_t, sin_up, sin_dn):
    half = MLA_ROPE // 2
    return d * cos_t + pltpu.roll(d * sin_up, LANE - half, 1) + pltpu.roll(d * sin_dn, half, 1)


def _norm192(xcat, g):
    r = lax.rsqrt(jnp.sum(xcat * xcat, axis=-1, keepdims=True) * (1.0 / MLA_QK) + EPS)
    return xcat * r * g


def _mla_prep_fwd(qfull, kv, kr, tabs, gq, gk, H, *, tm=512):
    T = qfull.shape[0]
    tm = _tile(T, tm)

    def body(q_ref, kv_ref, kr_ref, c_ref, su_ref, sd_ref, gq_ref, gk_ref, qf_ref, kf_ref, v_ref):
        tabs_b = (c_ref[...], su_ref[...], sd_ref[...])
        qb = q_ref[...]
        qcat = jnp.concatenate([qb[:, :MLA_NOPE], _rope(qb[:, MLA_NOPE:], *tabs_b)], axis=1)
        qf_ref[...] = _norm192(qcat, gq_ref[...]).astype(BF16)
        kvb = kv_ref[...]
        kcat = jnp.concatenate([kvb[:, :MLA_NOPE], _rope(kr_ref[...], *tabs_b)], axis=1)
        kf_ref[...] = _norm192(kcat, gk_ref[...]).astype(BF16)
        v_ref[...] = kvb[:, MLA_NOPE:].astype(BF16)

    head = pl.BlockSpec((tm, MLA_QK_PAD), lambda i, h: (i, h))
    tok = pl.BlockSpec((tm, LANE), lambda i, h: (i, 0))
    vec = pl.BlockSpec((1, MLA_QK_PAD), lambda i, h: (0, 0))
    return pl.pallas_call(
        body, name="mla_prep_fwd", grid=(T // tm, H),
        in_specs=[head, head, tok, tok, tok, tok, vec, vec],
        out_specs=(head, head, pl.BlockSpec((tm, MLA_V), lambda i, h: (i, h))),
        out_shape=(jax.ShapeDtypeStruct((T, H * MLA_QK_PAD), BF16),
                   jax.ShapeDtypeStruct((T, H * MLA_QK_PAD), BF16),
                   jax.ShapeDtypeStruct((T, H * MLA_V), BF16)),
        compiler_params=_params(("parallel", "parallel")),
    )(qfull, kv, kr, *tabs, gq, gk)


def _mla_prep_bwd(qfull, kv, kr, tabs, gq, gk, dqf, dkf, dv, H, *, tm=512):
    T = qfull.shape[0]
    tm = _tile(T, tm)

    def body(q_ref, kv_ref, kr_ref, c_ref, su_ref, sd_ref, gq_ref, gk_ref, dqf_ref, dkf_ref, dv_ref,
             dq_ref, dkv_ref, dkr_ref, dgq_ref, dgk_ref, kr_acc):
        h = pl.program_id(1)

        @pl.when((pl.program_id(0) == 0) & (h == 0))
        def _():
            dgq_ref[...] = jnp.zeros_like(dgq_ref)
            dgk_ref[...] = jnp.zeros_like(dgk_ref)

        @pl.when(h == 0)
        def _():
            kr_acc[...] = jnp.zeros_like(kr_acc)

        tabs_b = (c_ref[...], su_ref[...], sd_ref[...])
        qb = q_ref[...]
        qcat = jnp.concatenate([qb[:, :MLA_NOPE], _rope(qb[:, MLA_NOPE:], *tabs_b)], axis=1)
        dqcat, dgq = _headnorm_bwd(qcat, gq_ref[...], dqf_ref[...], MLA_QK)
        dq_ref[...] = jnp.concatenate(
            [dqcat[:, :MLA_NOPE], _rope_t(dqcat[:, MLA_NOPE:], *tabs_b)], axis=1).astype(BF16)
        dgq_ref[...] += dgq
        kvb = kv_ref[...]
        kcat = jnp.concatenate([kvb[:, :MLA_NOPE], _rope(kr_ref[...], *tabs_b)], axis=1)
        dkcat, dgk = _headnorm_bwd(kcat, gk_ref[...], dkf_ref[...], MLA_QK)
        dkv_ref[...] = jnp.concatenate([dkcat[:, :MLA_NOPE], dv_ref[...]], axis=1).astype(BF16)
        dgk_ref[...] += dgk
        kr_acc[...] += dkcat[:, MLA_NOPE:]

        @pl.when(h == H - 1)
        def _():
            dkr_ref[...] = _rope_t(kr_acc[...], *tabs_b).astype(BF16)

    head = pl.BlockSpec((tm, MLA_QK_PAD), lambda i, h: (i, h))
    tok = pl.BlockSpec((tm, LANE), lambda i, h: (i, 0))
    vec = pl.BlockSpec((1, MLA_QK_PAD), lambda i, h: (0, 0))
    return pl.pallas_call(
        body, name="mla_prep_bwd", grid=(T // tm, H),
        in_specs=[head, head, tok, tok, tok, tok, vec, vec, head, head,
                  pl.BlockSpec((tm, MLA_V), lambda i, h: (i, h))],
        out_specs=(head, head, tok, vec, vec),
        out_shape=(jax.ShapeDtypeStruct((T, H * MLA_QK_PAD), BF16),
                   jax.ShapeDtypeStruct((T, H * MLA_QK_PAD), BF16),
                   jax.ShapeDtypeStruct((T, LANE), BF16),
                   jax.ShapeDtypeStruct((1, MLA_QK_PAD), F32), jax.ShapeDtypeStruct((1, MLA_QK_PAD), F32)),
        scratch_shapes=[pltpu.VMEM((tm, LANE), F32)],
        compiler_params=_params(("arbitrary", "arbitrary")),
    )(qfull, kv, kr, *tabs, gq, gk, dqf, dkf, dv)


def _cf_layouts(cf, H):
    cfh = cf[:, :H].T
    return cfh[:, :, None], cfh[:, None, :]


def _layer_fwd(i, x, P, tabs, H, hook=None):
    kind, j = i % 3, i // 3
    s = {"x_in": x}
    h = _rmsnorm_fwd(x, P["mix_norm"][i:i + 1], name="mix_norm_fwd")
    s["h"] = h
    if kind == 0:
        w = P["sb"][j]
        qkv = _mm(h, w["w_in"], out_dtype=BF16, name="sb_qkv")
        o, ctot = _sb_fwd(qkv, H)
        s.update(qkv=qkv, o=o, ctot=ctot)
    elif kind == 1:
        w = P["fox"][j]
        qkv = _mm(h, w["w_qkv"], name="fox_qkv")
        flog = _mm(h, w["w_f"], name="fox_gate_proj")
        qn, kn, vb = _fox_prep_fwd(qkv, w["gq"], w["gk"], H)
        cf = _cf_layouts(_fox_gate_fwd(flog, w["bf"]), H)
        o, of, lse = _softmax_fwd(qn, kn, vb, cf, H, HEAD_DIM, 1.0 / math.sqrt(HEAD_DIM), name="fox_attn_fwd")
        s.update(qkv=qkv, flog=flog, qn=qn, kn=kn, vb=vb, cf=cf, o=o, of=of, lse=lse)
    else:
        w = P["mla"][j]
        dq = _mm(h, w["w_dq"], name="mla_down_q")
        dkv = _mm(h, w["w_dkv"], name="mla_down_kv")
        kr = _mm(h, w["w_dr"], name="mla_down_rope")
        cq = _rmsnorm_fwd(dq, w["q_norm"], name="mla_q_norm_fwd")
        ckv = _rmsnorm_fwd(dkv, w["kv_norm"], name="mla_kv_norm_fwd")
        qfull = _mm(cq, w["w_uq"], name="mla_up_q")
        kv = _mm(ckv, w["w_ukv"], name="mla_up_kv")
        qf, kf, v = _mla_prep_fwd(qfull, kv, kr, tabs, w["gq"], w["gk"], H)
        o, of, lse = _softmax_fwd(qf, kf, v, None, H, MLA_QK_PAD, 1.0 / math.sqrt(MLA_QK), name="mla_attn_fwd")
        s.update(dq=dq, dkv=dkv, kr=kr, cq=cq, ckv=ckv, qfull=qfull, kv=kv, qf=qf, kf=kf, v=v,
                 o=o, of=of, lse=lse)
    x = _mm(s["o"], w["w_out"], res=x, name="mixer_out")
    s["x_mid"] = x
    deps = hook(x) if hook is not None else ()
    h2 = _rmsnorm_fwd(x, P["mlp_norm"][i:i + 1], name="mlp_norm_fwd")
    z, u = _mm(h2, P["mlp"][i]["w1"], act="relu2", name="mlp_up", deps=deps)
    x = _mm(u, P["mlp"][i]["w2"], res=x, name="mlp_down")
    s.update(h2=h2, z=z, u=u)
    return x, s


def _layer_bwd(i, dx, dxb, s, P, tabs, H, deps=(), hook=None, hook2=None):
    kind, j = i % 3, i // 3
    G = {}
    wm = P["mlp"][i]
    dz = _mm(dxb, wm["w2"], mode="nt", act="drelu2", z=s["z"], out_dtype=BF16, name="mlp_down_bwd",
             deps=deps)
    G["w2"] = _mm(s["u"], dxb, mode="tn", out_dtype=BF16, name="mlp_w2_grad")
    G["w1"] = _mm(s["h2"], dz, mode="tn", out_dtype=BF16, name="mlp_w1_grad")
    dh2 = _mm(dz, wm["w1"], mode="nt", name="mlp_up_bwd")
    dx, dxb, G["mlp_norm"] = _rmsnorm_bwd(s["x_mid"], P["mlp_norm"][i:i + 1], dh2, dx, name="mlp_norm_bwd")
    h = s["h"]
    w = P[("sb", "fox", "mla")[kind]][j]
    do = _mm(dxb, w["w_out"], mode="nt", out_dtype=BF16, name="mixer_out_bwd",
             deps=hook(dxb, G) if hook is not None else ())
    G["w_out"] = _mm(s["o"], dxb, mode="tn", out_dtype=BF16, name="mixer_out_grad")
    if kind == 0:
        dq, dk, dv = _sb_bwd(s["qkv"], do, s["ctot"], H)
        dproj = jnp.concatenate([dq, dk, dv], axis=1)
        G["w_in"] = _mm(h, dproj, mode="tn", out_dtype=BF16, name="sb_qkv_grad",
                        deps=hook2(dv) if hook2 is not None else ())
        dh = _mm(dproj, w["w_in"], mode="nt", name="sb_qkv_bwd")
    elif kind == 1:
        dqn, dkn, dv, dcc, dcr = _softmax_bwd(
            s["qn"], s["kn"], s["vb"], s["cf"], do, s["of"], s["lse"], H, HEAD_DIM,
            1.0 / math.sqrt(HEAD_DIM), name="fox_attn_bwd")
        dq, dk, dvb, G["gq"], G["gk"] = _fox_prep_bwd(s["qkv"], w["gq"], w["gk"], dqn, dkn, dv, H)
        dcf = (dcc[:, :, 0] + dcr[:, 0, :]).T
        dcf = jnp.pad(dcf, ((0, 0), (0, LANE - H)))
        dflog, G["bf"] = _fox_gate_bwd(s["flog"], w["bf"], dcf)
        dproj = jnp.concatenate([dq, dk, dvb], axis=1)
        G["w_qkv"] = _mm(h, dproj, mode="tn", out_dtype=BF16, name="fox_qkv_grad",
                         deps=hook2(dv) if hook2 is not None else ())
        G["w_f"] = _mm(h, dflog, mode="tn", out_dtype=BF16, name="fox_gate_grad")
        dh = _mm(dproj, w["w_qkv"], mode="nt", name="fox_qkv_bwd")
        dh = _mm(dflog, w["w_f"], mode="nt", res=dh, name="fox_gate_bwd_proj")
    else:
        dqf, dkf, dv = _softmax_bwd(
            s["qf"], s["kf"], s["v"], None, do, s["of"], s["lse"], H, MLA_QK_PAD,
            1.0 / math.sqrt(MLA_QK), name="mla_attn_bwd")
        dqfull, dkv, dkr, G["gq"], G["gk"] = _mla_prep_bwd(
            s["qfull"], s["kv"], s["kr"], tabs, w["gq"], w["gk"], dqf, dkf, dv, H)
        G["w_uq"] = _mm(s["cq"], dqfull, mode="tn", out_dtype=BF16, name="mla_up_q_grad",
                        deps=hook2(dv) if hook2 is not None else ())
        G["w_ukv"] = _mm(s["ckv"], dkv, mode="tn", out_dtype=BF16, name="mla_up_kv_grad")
        dcq = _mm(dqfull, w["w_uq"], mode="nt", name="mla_up_q_bwd")
        dckv = _mm(dkv, w["w_ukv"], mode="nt", name="mla_up_kv_bwd")
        ddq, G["q_norm"] = _rmsnorm_bwd(s["dq"], w["q_norm"], dcq, want_f32=False, name="mla_q_norm_bwd")
        ddkv, G["kv_norm"] = _rmsnorm_bwd(s["dkv"], w["kv_norm"], dckv, want_f32=False, name="mla_kv_norm_bwd")
        G["w_dq"] = _mm(h, ddq, mode="tn", out_dtype=BF16, name="mla_down_q_grad")
        G["w_dkv"] = _mm(h, ddkv, mode="tn", out_dtype=BF16, name="mla_down_kv_grad")
        G["w_dr"] = _mm(h, dkr, mode="tn", out_dtype=BF16, name="mla_down_rope_grad")
        dh = _mm(ddq, w["w_dq"], mode="nt", name="mla_down_q_bwd")
        dh = _mm(ddkv, w["w_dkv"], mode="nt", res=dh, name="mla_down_kv_bwd")
        dh = _mm(dkr, w["w_dr"], mode="nt", res=dh, name="mla_down_rope_bwd")
    dx, dxb, G["mix_norm"] = _rmsnorm_bwd(s["x_in"], P["mix_norm"][i:i + 1], dh, dx, name="mix_norm_bwd")
    return dx, dxb, G


def _local_step(x, positions, loss_target, P, depth, H):
    tabs = _rope_tables(positions)
    saved = []
    for i in range(depth):
        x, s = _layer_fwd(i, x, P, tabs, H)
        saved.append(s)
    dx, dxb, loss = _loss_head(x, loss_target)
    grads = [None] * depth
    for i in reversed(range(depth)):
        dx, dxb, grads[i] = _layer_bwd(i, dx, dxb, saved[i], P, tabs, H)
    return loss, dx, grads


def _pad_cols(a, n):
    return jnp.pad(a, ((0, 0), (0, n - a.shape[1])))


def _prepare_layer(P, full, i, H, parts=("mix", "mlp")):
    W = H * HEAD_DIM
    kind, j = i % 3, i // 3
    if "mlp" in parts:
        P["mlp"][i] = {"w1": full["mlp_w1"][i], "w2": full["mlp_w2"][i]}
    if "mix" not in parts:
        return
    if kind == 0:
        P["sb"][j] = {"w_in": full["sb_w_in"][j], "w_out": full["sb_w_out"][j]}
    elif kind == 1:
        w = full["fox_w_in"][j]
        P["fox"][j] = {
            "w_qkv": w[:, :3 * W], "w_f": _pad_cols(w[:, 3 * W:], LANE),
            "bf": _pad_cols(full["fox_b_f"][j:j + 1], LANE),
            "gq": full["fox_q_gain"][j:j + 1], "gk": full["fox_k_gain"][j:j + 1],
            "w_out": full["fox_w_out"][j]}
    else:
        w = full["mla_w_in"][j]
        q_norm, kv_norm = full["mla_q_norm"][j], full["mla_kv_norm"][j]
        rq, rkv = q_norm.shape[0], kv_norm.shape[0]
        w_uq = full["mla_w_uq"][j].reshape(rq, H, MLA_QK)
        w_uq = jnp.pad(w_uq, ((0, 0), (0, 0), (0, MLA_QK_PAD - MLA_QK))).reshape(rq, H * MLA_QK_PAD)
        P["mla"][j] = {
            "w_dq": w[:, :rq], "w_dkv": w[:, rq:rq + rkv], "w_dr": _pad_cols(w[:, rq + rkv:], LANE),
            "q_norm": q_norm[None], "kv_norm": kv_norm[None],
            "w_uq": w_uq, "w_ukv": full["mla_w_ukv"][j],
            "gq": _pad_cols(full["mla_q_gain"][j:j + 1], MLA_QK_PAD),
            "gk": _pad_cols(full["mla_k_gain"][j:j + 1], MLA_QK_PAD),
            "w_out": full["mla_w_out"][j]}


def _prepare(full, H):
    depth = len(full["mlp_w1"])
    P = {"mix_norm": full["mix_norm"], "mlp_norm": full["mlp_norm"], "mlp": [None] * depth,
         "sb": [None] * len(full["sb_w_in"]), "fox": [None] * len(full["fox_w_in"]),
         "mla": [None] * len(full["mla_w_in"])}
    for i in range(depth):
        _prepare_layer(P, full, i, H)
    return P


def _layer_grads(i, G, H):
    kind = i % 3
    out = {"mix_norm": G["mix_norm"], "mlp_norm": G["mlp_norm"], "mlp_w1": G["w1"], "mlp_w2": G["w2"]}
    if kind == 0:
        out.update(sb_w_in=G["w_in"], sb_w_out=G["w_out"])
    elif kind == 1:
        out.update(fox_w_in=jnp.concatenate([G["w_qkv"], G["w_f"][:, :H]], axis=1), fox_b_f=G["bf"][:, :H],
                   fox_q_gain=G["gq"], fox_k_gain=G["gk"], fox_w_out=G["w_out"])
    else:
        rq = G["w_uq"].shape[0]
        out.update(
            mla_w_in=jnp.concatenate([G["w_dq"], G["w_dkv"], G["w_dr"][:, :MLA_ROPE]], axis=1),
            mla_q_norm=G["q_norm"], mla_kv_norm=G["kv_norm"],
            mla_w_uq=G["w_uq"].reshape(rq, H, MLA_QK_PAD)[:, :, :MLA_QK].reshape(rq, H * MLA_QK),
            mla_w_ukv=G["w_ukv"], mla_q_gain=G["gq"][:, :MLA_QK], mla_k_gain=G["gk"][:, :MLA_QK],
            mla_w_out=G["w_out"])
    return out


def _layer_of(name, j):
    if name.startswith("sb_"):
        return 3 * j
    if name.startswith("fox_"):
        return 3 * j + 1
    if name.startswith("mla_"):
        return 3 * j + 2
    return j


def _unprepare(grads, H):
    out = {}
    for i, G in enumerate(grads):
        for n, g in _layer_grads(i, G, H).items():
            out.setdefault(n, []).append(g)
    return out


class _Place:
    def __init__(self, x, y, c):
        self.x, self.y, self.c = x, y, c
        self.dev = 4 * x + 2 * y + c
        self.chip = 2 * x + y
        self.id = (x, y, c)


def _peers(me, kind):
    if kind == "ici":
        return [_Place(1 - me.x, me.y, me.c), _Place(me.x, 1 - me.y, me.c), _Place(1 - me.x, 1 - me.y, me.c)]
    return [_Place(me.x, me.y, 1 - me.c)]


def _exchange(name, kind, operands, out_shapes, aliases, n_remote, n_local, plan):
    n_in, n_out = len(operands), len(out_shapes)

    def body(*refs):
        in_refs, out_refs = refs[:n_in], refs[n_in:n_in + n_out]
        send_sems, recv_sems, local_sems = refs[n_in + n_out:]
        me = _Place(lax.axis_index("x"), lax.axis_index("y"), lax.axis_index("c"))
        peers = _peers(me, kind)
        remote, local = plan(me, peers, in_refs, out_refs)
        assert len(remote) == n_remote and len(local) == n_local
        sends = []
        for n, (src, dst, k, _) in enumerate(remote):
            cp = pltpu.make_async_remote_copy(
                src_ref=src, dst_ref=dst, send_sem=send_sems.at[n], recv_sem=recv_sems.at[n],
                device_id=peers[k].id, device_id_type=MESH)
            cp.start()
            sends.append(cp)
        own = []
        for n, (src, dst) in enumerate(local):
            cp = pltpu.make_async_copy(src, dst, local_sems.at[n])
            cp.start()
            own.append(cp)
        for n, (src, _, k, landing) in enumerate(remote):
            pltpu.make_async_remote_copy(
                src_ref=src, dst_ref=landing, send_sem=send_sems.at[n], recv_sem=recv_sems.at[n],
                device_id=peers[k].id, device_id_type=MESH).wait_recv()
        for cp in sends:
            cp.wait_send()
        for cp in own:
            cp.wait()

    outs = pl.pallas_call(
        body, name=name, in_specs=[ANY] * n_in, out_specs=tuple([ANY] * n_out),
        out_shape=tuple(out_shapes), input_output_aliases=aliases,
        scratch_shapes=[pltpu.SemaphoreType.DMA((n_remote,)), pltpu.SemaphoreType.DMA((n_remote,)),
                        pltpu.SemaphoreType.DMA((max(n_local, 1),))],
    )(*operands)
    return list(outs)


def _window(ref, kind, d, shard_shape):
    r, c = shard_shape
    if kind == "col":
        return ref.at[:, pl.ds(pl.multiple_of(d * c, c), c)]
    return ref.at[pl.ds(pl.multiple_of(d * r, r), r), :]


def _full_shape(kind, shard_shape):
    r, c = shard_shape
    return (r, N_DEV * c) if kind == "col" else (N_DEV * r, c)


HBM = pl.BlockSpec(memory_space=pltpu.HBM)
SEM = pl.BlockSpec(memory_space=pltpu.SEMAPHORE)
EFFECT = pltpu.SideEffectType.DATAFLOW_SIDE_EFFECTING


class _Flight:
    def __init__(self, name, kind, n_remote, plan, send_sems, recv_sems, bufs, token):
        self.name, self.kind, self.n_remote, self.plan = name, kind, n_remote, plan
        self.send_sems, self.recv_sems, self.bufs, self.token = send_sems, recv_sems, bufs, token


def _start_copies(name, kind, bufs, n_remote, plan, after=None):
    nb = len(bufs)
    n_after = 0 if after is None else 1

    def body(*refs):
        in_refs = refs[:nb]
        send_sems, recv_sems = refs[nb + n_after], refs[nb + n_after + 1]
        token = refs[2 * nb + n_after + 2]
        me = _Place(lax.axis_index("x"), lax.axis_index("y"), lax.axis_index("c"))
        peers = _peers(me, kind)
        for n, (src, dst, k, _) in enumerate(plan(me, peers, in_refs)):
            pltpu.make_async_remote_copy(
                src_ref=src, dst_ref=dst, send_sem=send_sems.at[n], recv_sem=recv_sems.at[n],
                device_id=peers[k].id, device_id_type=MESH).start()
        token[...] = jnp.zeros_like(token)

    outs = pl.pallas_call(
        body, name=name, in_specs=[HBM] * nb + [ANY] * n_after,
        out_specs=(SEM, SEM) + (HBM,) * nb + (pl.BlockSpec(memory_space=pltpu.VMEM),),
        out_shape=(pltpu.SemaphoreType.DMA((n_remote,)), pltpu.SemaphoreType.DMA((n_remote,)))
        + tuple(pltpu.HBM(b.shape, b.dtype) for b in bufs) + (jax.ShapeDtypeStruct((8, LANE), F32),),
        input_output_aliases={a: 2 + a for a in range(nb)},
        compiler_params=pltpu.CompilerParams(has_side_effects=EFFECT),
    )(*[pltpu.with_memory_space_constraint(b, pltpu.HBM) for b in bufs], *([after] if n_after else []))
    return _Flight(name, kind, n_remote, plan, outs[0], outs[1], list(outs[2:2 + nb]), outs[2 + nb])


def _wait_copies(flight, after):
    nb = len(flight.bufs)
    plan, kind = flight.plan, flight.kind
    after = list(after) if isinstance(after, (tuple, list)) else [after]

    def body(*refs):
        in_refs = refs[:nb]
        send_sems, recv_sems = refs[nb], refs[nb + 1]
        me = _Place(lax.axis_index("x"), lax.axis_index("y"), lax.axis_index("c"))
        peers = _peers(me, kind)
        for n, (src, _, k, landing) in enumerate(plan(me, peers, in_refs)):
            cp = pltpu.make_async_remote_copy(
                src_ref=src, dst_ref=landing, send_sem=send_sems.at[n], recv_sem=recv_sems.at[n],
                device_id=peers[k].id, device_id_type=MESH)
            cp.wait_send()
            cp.wait_recv()

    outs = pl.pallas_call(
        body, name=flight.name.replace("_start", "_wait"),
        in_specs=[HBM] * nb + [SEM, SEM] + [ANY] * len(after), out_specs=(HBM,) * nb,
        out_shape=tuple(pltpu.HBM(b.shape, b.dtype) for b in flight.bufs),
        input_output_aliases={a: a for a in range(nb)},
        compiler_params=pltpu.CompilerParams(has_side_effects=EFFECT),
    )(*flight.bufs, flight.send_sems, flight.recv_sems, *after)
    return list(outs)


def _own_in_place(shard, kind, dev, dtype, deps=()):
    r, c = shard.shape
    tr = _row_tile(r, c)
    nb = r // tr

    def body(dev_ref, s_ref, *rest):
        o_ref = rest[-1]
        o_ref[...] = s_ref[...].astype(dtype)

    if kind == "col":
        o_spec = pl.BlockSpec((tr, c), lambda i, dev_ref: (i, dev_ref[0]))
    else:
        o_spec = pl.BlockSpec((tr, c), lambda i, dev_ref: (dev_ref[0] * nb + i, 0))
    return pl.pallas_call(
        body, name="own_in_place",
        grid_spec=pltpu.PrefetchScalarGridSpec(
            num_scalar_prefetch=1, grid=(nb,),
            in_specs=[pl.BlockSpec((tr, c), lambda i, dev_ref: (i, 0))] + [ANY] * len(deps), out_specs=o_spec),
        out_shape=jax.ShapeDtypeStruct(_full_shape(kind, shard.shape), dtype),
        compiler_params=_params(("parallel",)),
    )(dev, shard, *deps)


def _gather_ici_start(fulls, kinds, shapes, tag, after=None):
    n = len(fulls)

    def plan(me, peers, refs):
        remote = []
        for a in range(n):
            mine = _window(refs[a], kinds[a], me.dev, shapes[a])
            for k, p in enumerate(peers):
                remote.append((mine, mine, k, _window(refs[a], kinds[a], p.dev, shapes[a])))
        return remote

    return _start_copies("gather_ici_start_" + tag, "ici", list(fulls), 3 * n, plan, after)


def _gather_d2d_start(fulls, kinds, shapes, tag):
    n = len(fulls)

    def plan(me, peers, refs):
        remote = []
        for a in range(n):
            for ch in range(N_CHIP):
                held = _window(refs[a], kinds[a], 2 * ch + me.c, shapes[a])
                remote.append((held, held, 0, _window(refs[a], kinds[a], 2 * ch + 1 - me.c, shapes[a])))
        return remote

    return _start_copies("gather_d2d_start_" + tag, "d2d", fulls, N_CHIP * n, plan)


def _scatter_d2d_start(grads, kinds, shapes, tag):
    n = len(grads)
    lands = [lax.empty((N_CHIP,) + tuple(s), g.dtype) for g, s in zip(grads, shapes)]

    def plan(me, peers, refs):
        remote = []
        for a in range(n):
            for ch in range(N_CHIP):
                remote.append((_window(refs[a], kinds[a], 2 * ch + 1 - me.c, shapes[a]),
                               refs[n + a].at[ch], 0, refs[n + a].at[ch]))
        return remote

    return _start_copies("scatter_d2d_start_" + tag, "d2d", list(grads) + lands, N_CHIP * n, plan)


def _scatter_ici_start(sums, shapes, tag, after=None):
    n = len(sums)
    lands = [lax.empty((N_CHIP - 1,) + tuple(s), BF16) for s in shapes]

    def plan(me, peers, refs):
        remote = []
        for a in range(n):
            for k, p in enumerate(peers):
                remote.append((refs[a].at[p.chip], refs[n + a].at[k], k, refs[n + a].at[k]))
        return remote

    return _start_copies("scatter_ici_start_" + tag, "ici", list(sums) + lands, 3 * n, plan, after)


def _pair_add(g, r4, kind, shard_shape, core, *, name):
    r, c = shard_shape
    tr = _row_tile(r, c, 524288)
    nb = r // tr

    def body(core_ref, g_ref, r_ref, o_ref):
        o_ref[...] = (g_ref[...].astype(F32) + r_ref[...].astype(F32)).astype(BF16)

    if kind == "col":
        g_spec = pl.BlockSpec((tr, c), lambda ch, i, core_ref: (i, 2 * ch + core_ref[0]))
    else:
        g_spec = pl.BlockSpec((tr, c), lambda ch, i, core_ref: ((2 * ch + core_ref[0]) * nb + i, 0))
    slot = pl.BlockSpec((None, tr, c), lambda ch, i, core_ref: (ch, i, 0))
    return pl.pallas_call(
        body, name=name,
        grid_spec=pltpu.PrefetchScalarGridSpec(
            num_scalar_prefetch=1, grid=(N_CHIP, nb), in_specs=[g_spec, slot], out_specs=slot),
        out_shape=jax.ShapeDtypeStruct((N_CHIP, r, c), BF16),
        compiler_params=_params(("parallel", "parallel")),
    )(core, g, r4)


def _allreduce_small_start(part, chip):
    def plan_swap(me, peers, ins, outs):
        return [(ins[0], outs[0], 0, outs[0])], []

    (got,) = _exchange("allreduce_small_swap", "d2d", [part], [jax.ShapeDtypeStruct(part.shape, F32)],
                       {}, 1, 0, plan_swap)

    def body(chip_ref, a_ref, b_ref, o_ref):
        o_ref[...] = a_ref[...] + b_ref[...]

    whole = pl.BlockSpec(part.shape, lambda i, chip_ref: (0, 0))
    slots = pl.pallas_call(
        body, name="allreduce_small_pair",
        grid_spec=pltpu.PrefetchScalarGridSpec(
            num_scalar_prefetch=1, grid=(1,), in_specs=[whole, whole],
            out_specs=pl.BlockSpec((None,) + part.shape, lambda i, chip_ref: (chip_ref[0], 0, 0))),
        out_shape=jax.ShapeDtypeStruct((N_CHIP,) + part.shape, F32),
    )(chip, part, got)

    def plan(me, peers, refs):
        mine = refs[0].at[me.chip]
        return [(mine, mine, k, refs[0].at[p.chip]) for k, p in enumerate(peers)]

    return _start_copies("allreduce_small_start", "ici", [slots], 3, plan)


def _row_tile(r, c, limit=524288):
    best = None
    for t in range(8, r + 1, 8):
        if r % t == 0 and t * c <= limit:
            best = t
    return best if best is not None else r


def _adam_math(w, g, m, v):
    m = ADAM_B1 * m + (1.0 - ADAM_B1) * g
    v = ADAM_B2 * v + (1.0 - ADAM_B2) * (g * g)
    m_hat = m / (1.0 - ADAM_B1 ** ADAM_STEP)
    v_hat = v / (1.0 - ADAM_B2 ** ADAM_STEP)
    delta = -ADAM_LR * (m_hat / (jnp.sqrt(v_hat) + ADAM_EPS) + ADAM_WD * w)
    return delta, m, v


def _adamw(w, m, v, sums, lands, chip, layer, prev, *, name):
    L, r, c = w.shape
    tr = _row_tile(r, c)
    has_prev = prev is not None

    def body(chip_ref, *refs):
        w_ref, m_ref, v_ref, s_ref, l_ref = refs[:5]
        g_ref, d_ref, nm_ref, nv_ref = refs[-4:]
        g = (s_ref[...].astype(F32) + l_ref[0].astype(F32)) + l_ref[1].astype(F32) + l_ref[2].astype(F32)
        delta, nm, nv = _adam_math(w_ref[...], g, m_ref[...], v_ref[...])
        g_ref[...] = g
        d_ref[...] = delta
        nm_ref[...] = nm
        nv_ref[...] = nv

    slab = pl.BlockSpec((None, tr, c), lambda i, chip_ref: (layer, i, 0))
    in_specs = [slab, slab, slab, pl.BlockSpec((None, tr, c), lambda i, chip_ref: (chip_ref[0], i, 0)),
                pl.BlockSpec((N_CHIP - 1, tr, c), lambda i, chip_ref: (0, i, 0))]
    args = [chip, w, m, v, sums, lands]
    aliases = {}
    if has_prev:
        in_specs += [ANY] * 4
        args += list(prev)
        aliases = {6 + n: n for n in range(4)}
    return pl.pallas_call(
        body, name=name,
        grid_spec=pltpu.PrefetchScalarGridSpec(
            num_scalar_prefetch=1, grid=(r // tr,), in_specs=in_specs, out_specs=(slab,) * 4),
        out_shape=tuple(jax.ShapeDtypeStruct((L, r, c), F32) for _ in range(4)),
        input_output_aliases=aliases, compiler_params=_params(("parallel",)),
    )(*args)


def _adamw_small(w, g, m, v):
    def body(w_ref, g_ref, m_ref, v_ref, d_ref, nm_ref, nv_ref):
        d_ref[...], nm_ref[...], nv_ref[...] = _adam_math(w_ref[...], g_ref[...], m_ref[...], v_ref[...])

    return pl.pallas_call(
        body, name="adamw_small", out_shape=tuple(jax.ShapeDtypeStruct(w.shape, F32) for _ in range(3)),
    )(w, g, m, v)


def _sum4(parts):
    def body(p_ref, o_ref):
        o_ref[...] = (p_ref[0] + p_ref[1]) + p_ref[2] + p_ref[3]

    return pl.pallas_call(
        body, name="sum_chips", out_shape=jax.ShapeDtypeStruct(parts.shape[1:], F32))(parts)


_WEIGHTS = ["mix_norm", "mlp_norm", "sb_w_in", "sb_w_out", "fox_w_in", "fox_b_f", "fox_q_gain", "fox_k_gain",
            "fox_w_out", "mla_w_in", "mla_q_norm", "mla_kv_norm", "mla_w_uq", "mla_w_ukv", "mla_q_gain",
            "mla_k_gain", "mla_w_out", "mlp_w1", "mlp_w2"]
_BIG = {"sb_w_in": "col", "sb_w_out": "row", "fox_w_in": "row", "fox_w_out": "row", "mla_w_in": "row",
        "mla_w_uq": "col", "mla_w_ukv": "col", "mla_w_out": "row", "mlp_w1": "col", "mlp_w2": "row"}


def _layer_big(i):
    kind, j = i % 3, i // 3
    mixer = {0: ["sb_w_in", "sb_w_out"], 1: ["fox_w_in", "fox_w_out"],
             2: ["mla_w_in", "mla_w_uq", "mla_w_ukv", "mla_w_out"]}[kind]
    return [(n, j) for n in mixer] + [("mlp_w1", i), ("mlp_w2", i)]


def _stack_to_cols(a):
    r = a.shape[0] // N_DEV
    return a.reshape(N_DEV, r, a.shape[1]).transpose(1, 0, 2).reshape(r, N_DEV * a.shape[1])


def _cols_to_stack(a):
    c = a.shape[1] // N_DEV
    return a.reshape(a.shape[0], N_DEV, c).transpose(1, 0, 2).reshape(N_DEV * a.shape[0], c)


def _pack_rows(rows, width):
    rows = [jnp.pad(r.reshape(-1).astype(F32), (0, width - r.size)) for r in rows]
    pad = (-len(rows)) % 8
    rows += [jnp.zeros((width,), F32)] * pad
    return jnp.stack(rows)


def kernel(x, positions, mix_norm, mlp_norm, sb_w_in, sb_w_out, fox_w_in, fox_b_f, fox_q_gain, fox_k_gain, fox_w_out, mla_w_in, mla_q_norm, mla_kv_norm, mla_w_uq, mla_w_ukv, mla_q_gain, mla_k_gain, mla_w_out, mlp_w1, mlp_w2, loss_target, m_mix_norm, m_mlp_norm, m_sb_w_in, m_sb_w_out, m_fox_w_in, m_fox_b_f, m_fox_q_gain, m_fox_k_gain, m_fox_w_out, m_mla_w_in, m_mla_q_norm, m_mla_kv_norm, m_mla_w_uq, m_mla_w_ukv, m_mla_q_gain, m_mla_k_gain, m_mla_w_out, m_mlp_w1, m_mlp_w2, v_mix_norm, v_mlp_norm, v_sb_w_in, v_sb_w_out, v_fox_w_in, v_fox_b_f, v_fox_q_gain, v_fox_k_gain, v_fox_w_out, v_mla_w_in, v_mla_q_norm, v_mla_kv_norm, v_mla_w_uq, v_mla_w_ukv, v_mla_q_gain, v_mla_k_gain, v_mla_w_out, v_mlp_w1, v_mlp_w2):
    w_in = dict(zip(_WEIGHTS, (mix_norm, mlp_norm, sb_w_in, sb_w_out, fox_w_in, fox_b_f, fox_q_gain, fox_k_gain, fox_w_out, mla_w_in, mla_q_norm, mla_kv_norm, mla_w_uq, mla_w_ukv, mla_q_gain, mla_k_gain, mla_w_out, mlp_w1, mlp_w2)))
    m_in = dict(zip(_WEIGHTS, (m_mix_norm, m_mlp_norm, m_sb_w_in, m_sb_w_out, m_fox_w_in, m_fox_b_f, m_fox_q_gain, m_fox_k_gain, m_fox_w_out, m_mla_w_in, m_mla_q_norm, m_mla_kv_norm, m_mla_w_uq, m_mla_w_ukv, m_mla_q_gain, m_mla_k_gain, m_mla_w_out, m_mlp_w1, m_mlp_w2)))
    v_in = dict(zip(_WEIGHTS, (v_mix_norm, v_mlp_norm, v_sb_w_in, v_sb_w_out, v_fox_w_in, v_fox_b_f, v_fox_q_gain, v_fox_k_gain, v_fox_w_out, v_mla_w_in, v_mla_q_norm, v_mla_kv_norm, v_mla_w_uq, v_mla_w_ukv, v_mla_q_gain, v_mla_k_gain, v_mla_w_out, v_mlp_w1, v_mlp_w2)))
    depth, D = mix_norm.shape
    H = D // HEAD_DIM
    n_mla = mla_w_in.shape[0]
    dev = 4 * lax.axis_index("x") + 2 * lax.axis_index("y") + lax.axis_index("c")
    core = lax.axis_index("c").astype(jnp.int32).reshape(1)
    dev_arr = dev.astype(jnp.int32).reshape(1)

    chip = (2 * lax.axis_index("x") + lax.axis_index("y")).astype(jnp.int32).reshape(1)
    nq, nkv = mla_q_norm.shape[1], mla_kv_norm.shape[1]

    units, u_mix, u_mlp = [], {}, {}
    for i in range(depth):
        big = _layer_big(i)
        u_mix[i] = len(units)
        if i < SPLIT_LAYERS:
            units += [(i, ("mix",), big[:-2]), (i, ("mlp",), big[-2:])]
        else:
            units.append((i, ("mix", "mlp"), big))
        u_mlp[i] = len(units) - 1

    def split(i):
        return u_mix[i] != u_mlp[i]

    def tag_of(u):
        i, parts, _ = units[u]
        return f"l{i}" if len(parts) == 2 else f"l{i}_{parts[0]}"

    kinds_of, shapes_of, ici = [], [], []
    for i, parts, names in units:
        shards = [w_in[n][j] for n, j in names]
        kinds = [_BIG[n] for n, _ in names]
        behind = (ici[-1].token,) if ici else ()
        fulls = [_own_in_place(s, k, dev_arr, BF16, behind) for s, k in zip(shards, kinds)]
        if i % 3 == 2:
            shards.append(_pack_rows([mla_q_norm[i // 3], mla_kv_norm[i // 3]], LANE))
            kinds.append("row")
            fulls.append(_own_in_place(shards[-1], "row", dev_arr, F32, behind))
        kinds_of.append(kinds)
        shapes_of.append([s.shape for s in shards])
        ici.append(_gather_ici_start(fulls, kinds, shapes_of[-1], tag_of(len(ici)),
                                     ici[-1].token if ici else None))

    full = {n: [None] * w_in[n].shape[0] for n in _WEIGHTS}
    for n in ("mix_norm", "mlp_norm", "fox_b_f", "fox_q_gain", "fox_k_gain", "mla_q_gain", "mla_k_gain"):
        full[n] = w_in[n]
    P = {"mix_norm": mix_norm, "mlp_norm": mlp_norm, "mlp": [None] * depth,
         "sb": [None] * sb_w_in.shape[0], "fox": [None] * fox_w_in.shape[0], "mla": [None] * n_mla}

    def forward_to_sibling(u, after):
        arrived = _wait_copies(ici[u], after)
        return _gather_d2d_start(arrived, kinds_of[u], shapes_of[u], tag_of(u))

    def finish_gather(u, flight, after):
        i, parts, names = units[u]
        got = _wait_copies(flight, after)
        for (n, j), a in zip(names, got):
            full[n][j] = _stack_to_cols(a) if n == "fox_w_in" else a
        if i % 3 == 2:
            tiles = got[-1].reshape(N_DEV, 8, LANE)
            full["mla_q_norm"][i // 3] = tiles[:, 0, :nq].reshape(-1)
            full["mla_kv_norm"][i // 3] = tiles[:, 1, :nkv].reshape(-1)
        _prepare_layer(P, full, i, H, parts)

    tabs = _rope_tables(positions[0])
    d2d = forward_to_sibling(0, ici[-1].token)
    finish_gather(0, d2d, d2d.token)
    xs, saved = x[0], []
    for i in range(depth):
        nxt = {}

        def hook(x_mid, i=i, nxt=nxt):
            if split(i):
                mlp_d2d = forward_to_sibling(u_mlp[i], x_mid)
                finish_gather(u_mlp[i], mlp_d2d, mlp_d2d.token)
            if 1 <= i < depth - 1 and not split(i + 1):
                nxt["d2d"] = forward_to_sibling(u_mix[i + 1], x_mid)
                return (nxt["d2d"].token,)
            return ()

        xs, s = _layer_fwd(i, xs, P, tabs, H, hook)
        saved.append(s)
        if i + 1 < depth:
            if "d2d" in nxt:
                finish_gather(u_mix[i + 1], nxt["d2d"], xs)
            else:
                mix_d2d = forward_to_sibling(u_mix[i + 1], xs)
                finish_gather(u_mix[i + 1], mix_d2d, mix_d2d.token)
    dx, dxb, loss_part = _loss_head(xs, loss_target[0])

    def unit_meta(u):
        names = units[u][2]
        return [_BIG[n] for n, _ in names], [w_in[n].shape[1:] for n, _ in names]

    def to_sibling_start(u, Gl):
        grads = [(_cols_to_stack(Gl[n]) if n == "fox_w_in" else Gl[n]) for n, _ in units[u][2]]
        return _scatter_d2d_start(grads, *unit_meta(u), tag_of(u))

    def pair_and_send(u, flight, after, before=None):
        n = len(units[u][2])
        kinds, shapes = unit_meta(u)
        got = _wait_copies(flight, after)
        sums = [_pair_add(g, r4, k, s, core, name="pair_add_" + tag_of(u))
                for g, r4, k, s in zip(got[:n], got[n:], kinds, shapes)]
        return _scatter_ici_start(sums, shapes, tag_of(u), before)

    G, to_chips, deps, flying = [None] * depth, [None] * len(units), (), {}
    for i in reversed(range(depth)):
        def hook(dxb_mid, Gp, i=i):
            out = ()
            if i + 1 < depth:
                u = u_mix[i + 1]
                to_chips[u] = pair_and_send(u, flying.pop(u), dxb_mid)
                out = (to_chips[u].token,)
            if split(i):
                flying[u_mlp[i]] = to_sibling_start(u_mlp[i], {"mlp_w1": Gp["w1"], "mlp_w2": Gp["w2"]})
                out = out + (flying[u_mlp[i]].token,)
            return out

        def hook2(dv, i=i):
            if not split(i):
                return ()
            u = u_mlp[i]
            to_chips[u] = pair_and_send(u, flying.pop(u), dv)
            return (to_chips[u].token,)

        dx, dxb, Gp = _layer_bwd(i, dx, dxb, saved[i], P, tabs, H, deps, hook, hook2)
        G[i] = _layer_grads(i, Gp, H)
        flying[u_mix[i]] = to_sibling_start(u_mix[i], G[i])
        deps = (flying[u_mix[i]].token,)
    grad_x = dx
    to_chips[0] = pair_and_send(0, flying.pop(0), dx)

    small = [n for n in _WEIGHTS if n not in _BIG]
    rows = []
    for n in small:
        for j in range(w_in[n].shape[0]):
            rows.append(G[_layer_of(n, j)][n])
    loss_row = len(rows)
    rows.append(loss_part)
    small_flight = _allreduce_small_start(_pack_rows(rows, D) + to_chips[0].token[0, 0], chip)

    results, after = {}, [small_flight.token]
    for u in reversed(range(len(units))):
        names = units[u][2]
        got = _wait_copies(to_chips[u], after)
        for a, (n, j) in enumerate(names):
            results[n] = _adamw(w_in[n], m_in[n], v_in[n], got[a], got[len(names) + a], chip, j, results.get(n),
                                name="adamw_" + n)
        after = [results[n][0] for n, _ in names]

    (parts,) = _wait_copies(small_flight, after)
    sgr = _sum4(parts)
    g_rows, w_rows, m_rows, v_rows, at = [], [], [], [], 0
    for n in small:
        for j in range(w_in[n].shape[0]):
            width = w_in[n].shape[1]
            if n in ("mla_q_norm", "mla_kv_norm"):
                g_rows.append(lax.dynamic_slice(sgr[at], (dev * width,), (width,)))
            else:
                g_rows.append(sgr[at, :width])
            w_rows.append(w_in[n][j])
            m_rows.append(m_in[n][j])
            v_rows.append(v_in[n][j])
            at += 1
    g_pack = _pack_rows(g_rows, D)
    d_pack, nm_pack, nv_pack = _adamw_small(
        _pack_rows(w_rows, D), g_pack, _pack_rows(m_rows, D), _pack_rows(v_rows, D))
    at = 0
    for n in small:
        L, width = w_in[n].shape
        results[n] = tuple(p[at:at + L, :width] for p in (g_pack, d_pack, nm_pack, nv_pack))
        at += L

    out = [sgr[loss_row, 0], grad_x[None]]
    for part in range(4):
        out += [results[n][part] for n in _WEIGHTS]
    return tuple(out)
```

```python
import functools
import math

import jax
import jax.numpy as jnp
from jax import lax
from jax.experimental import pallas as pl
from jax.experimental.pallas import tpu as pltpu

F32 = jnp.float32
BF16 = jnp.bfloat16

HEAD_DIM = 128
MLA_NOPE = 128
MLA_ROPE = 64
MLA_V = 128
MLA_QK = MLA_NOPE + MLA_ROPE
MLA_QK_PAD = 256
LANE = 128
ROPE_THETA = 10000.0
EPS = 1e-6
ADAM_LR = 0.001
ADAM_B1 = 0.9
ADAM_B2 = 0.999
ADAM_EPS = 1e-08
ADAM_WD = 0.01
ADAM_STEP = 10
N_DEV = 8
N_CHIP = 4
NEG = -1e30
VMEM_LIMIT = 56 * 1024 * 1024
ATT_BLOCK = 256
SB_BLOCK = 256
ATT_HEADS = 4
ATT_HEADS_FWD = 8
SPLIT_LAYERS = 2
SB_PARTS = 2
MESH = pl.DeviceIdType.MESH
ANY = pl.BlockSpec(memory_space=pl.ANY)


def _params(sem):
    return pltpu.CompilerParams(dimension_semantics=sem, vmem_limit_bytes=VMEM_LIMIT)


def _tile(dim, pref):
    if dim <= pref:
        return dim
    t = pref
    while dim % t:
        t -= LANE
    return t


def _dot(a, b, dims):
    return lax.dot_general(a, b, (dims, ((), ())), preferred_element_type=F32)


NN = ((1,), (0,))
NT = ((1,), (1,))
TN = ((0,), (0,))


def _mm(a, b, *, mode="nn", out_dtype=F32, res=None, act=None, z=None, name,
        tm=1024, tn=1024, tk=2048, deps=()):
    if mode == "nn":
        (M, K), (_, N) = a.shape, b.shape
    elif mode == "nt":
        (M, K), (N, _) = a.shape, b.shape
    else:
        (K, M), (_, N) = a.shape, b.shape
    tm, tn, tk = _tile(M, tm), _tile(N, tn), _tile(K, tk)
    nk = K // tk
    if mode == "tn":
        a_spec = pl.BlockSpec((tk, tm), lambda i, j, k: (k, i))
    else:
        a_spec = pl.BlockSpec((tm, tk), lambda i, j, k: (i, k))
    if mode == "nt":
        b_spec = pl.BlockSpec((tn, tk), lambda i, j, k: (j, k))
    else:
        b_spec = pl.BlockSpec((tk, tn), lambda i, j, k: (k, j))
    dims = {"nn": NN, "nt": NT, "tn": TN}[mode]
    o_spec = pl.BlockSpec((tm, tn), lambda i, j, k: (i, j))
    in_specs, args = [a_spec, b_spec], [a, b]
    if res is not None:
        in_specs.append(o_spec)
        args.append(res)
    if act == "drelu2":
        in_specs.append(o_spec)
        args.append(z)
    if act == "relu2":
        out_shape = (jax.ShapeDtypeStruct((M, N), F32), jax.ShapeDtypeStruct((M, N), BF16))
        out_specs = (o_spec, o_spec)
    else:
        out_shape = jax.ShapeDtypeStruct((M, N), out_dtype)
        out_specs = o_spec
    has_res, has_z = res is not None, act == "drelu2"
    n_out = 2 if act == "relu2" else 1
    in_specs += [ANY] * len(deps)
    args += list(deps)

    def body(*refs):
        a_ref, b_ref = refs[0], refs[1]
        idx = 2
        res_ref = z_ref = None
        if has_res:
            res_ref = refs[idx]
            idx += 1
        if has_z:
            z_ref = refs[idx]
            idx += 1
        idx += len(deps)
        outs = refs[idx:idx + n_out]

        def finish(r):
            if has_res:
                r = r + res_ref[...]
            if act == "relu2":
                outs[0][...] = r
                rr = jnp.maximum(r, 0.0)
                outs[1][...] = (rr * rr).astype(BF16)
            elif act == "drelu2":
                outs[0][...] = (r * (2.0 * jnp.maximum(z_ref[...], 0.0))).astype(out_dtype)
            else:
                outs[0][...] = r.astype(out_dtype)

        if nk == 1:
            finish(_dot(a_ref[...], b_ref[...], dims))
            return
        acc = refs[-1]
        k = pl.program_id(2)

        @pl.when(k == 0)
        def _():
            acc[...] = _dot(a_ref[...], b_ref[...], dims)

        @pl.when((k > 0) & (k < nk - 1))
        def _():
            acc[...] += _dot(a_ref[...], b_ref[...], dims)

        @pl.when(k == nk - 1)
        def _():
            finish(acc[...] + _dot(a_ref[...], b_ref[...], dims))

    return pl.pallas_call(
        body, name=name, grid=(M // tm, N // tn, nk), in_specs=in_specs, out_specs=out_specs,
        out_shape=out_shape, scratch_shapes=[pltpu.VMEM((tm, tn), F32)] if nk > 1 else [],
        compiler_params=_params(("parallel", "parallel", "arbitrary")),
    )(*args)


def _rmsnorm_fwd(x, g, *, name, tm=256):
    T, n = x.shape
    tm = _tile(T, tm)

    def body(x_ref, g_ref, o_ref):
        xf = x_ref[...]
        r = lax.rsqrt(jnp.mean(xf * xf, axis=-1, keepdims=True) + EPS)
        o_ref[...] = (xf * r * g_ref[...]).astype(BF16)

    return pl.pallas_call(
        body, name=name, grid=(T // tm,),
        in_specs=[pl.BlockSpec((tm, n), lambda i: (i, 0)), pl.BlockSpec((1, n), lambda i: (0, 0))],
        out_specs=pl.BlockSpec((tm, n), lambda i: (i, 0)),
        out_shape=jax.ShapeDtypeStruct((T, n), BF16),
        compiler_params=_params(("parallel",)),
    )(x, g)


def _rmsnorm_bwd(x, g, dy, dx_in=None, *, name, want_f32=True, tm=256):
    T, n = x.shape
    tm = _tile(T, tm)
    has_in = dx_in is not None
    row = pl.BlockSpec((tm, n), lambda i: (i, 0))
    vec = pl.BlockSpec((1, n), lambda i: (0, 0))

    def body(*refs):
        x_ref, g_ref, dy_ref = refs[:3]
        idx = 3
        in_ref = None
        if has_in:
            in_ref = refs[idx]
            idx += 1
        outs = refs[idx:]
        xf = x_ref[...]
        r = lax.rsqrt(jnp.mean(xf * xf, axis=-1, keepdims=True) + EPS)
        dyf = dy_ref[...]
        t = dyf * g_ref[...]
        dx = r * t - xf * (r * r * r * jnp.mean(t * xf, axis=-1, keepdims=True))
        if has_in:
            dx = dx + in_ref[...]
        o = 0
        if want_f32:
            outs[0][...] = dx
            o = 1
        outs[o][...] = dx.astype(BF16)
        dg_ref = outs[o + 1]

        @pl.when(pl.program_id(0) == 0)
        def _():
            dg_ref[...] = jnp.zeros_like(dg_ref)

        dg_ref[...] += jnp.sum(dyf * xf * r, axis=0, keepdims=True)

    in_specs = [row, vec, row] + ([row] if has_in else [])
    out_specs, out_shape = [], []
    if want_f32:
        out_specs.append(row)
        out_shape.append(jax.ShapeDtypeStruct((T, n), F32))
    out_specs += [row, vec]
    out_shape += [jax.ShapeDtypeStruct((T, n), BF16), jax.ShapeDtypeStruct((1, n), F32)]
    args = [x, g, dy] + ([dx_in] if has_in else [])
    return pl.pallas_call(
        body, name=name, grid=(T // tm,), in_specs=in_specs, out_specs=tuple(out_specs),
        out_shape=tuple(out_shape), compiler_params=_params(("arbitrary",)),
    )(*args)


def _loss_head(y, target, *, tm=256):
    T, n = y.shape
    tm = _tile(T, tm)
    row = pl.BlockSpec((tm, n), lambda i: (i, 0))

    def body(y_ref, t_ref, dy_ref, dyb_ref, loss_ref):
        err = y_ref[...] - t_ref[...]
        dy = err * (1.0 / n)
        dy_ref[...] = dy
        dyb_ref[...] = dy.astype(BF16)

        @pl.when(pl.program_id(0) == 0)
        def _():
            loss_ref[...] = jnp.zeros_like(loss_ref)

        part = 0.5 * jnp.sum(jnp.mean(err * err, axis=-1, keepdims=True), axis=0, keepdims=True)
        loss_ref[...] += part

    return pl.pallas_call(
        body, name="loss_head", grid=(T // tm,), in_specs=[row, row],
        out_specs=(row, row, pl.BlockSpec((1, 1), lambda i: (0, 0))),
        out_shape=(jax.ShapeDtypeStruct((T, n), F32), jax.ShapeDtypeStruct((T, n), BF16),
                   jax.ShapeDtypeStruct((1, 1), F32)),
        compiler_params=_params(("arbitrary",)),
    )(y, target)


def _split(v, n):
    parts = []
    for _ in range(n - 1):
        p = v.astype(BF16)
        parts.append(p)
        v = v - p.astype(F32)
    parts.append(v.astype(BF16))
    return parts


def _tri_left(tri, v):
    hi, mid, lo = _split(v, 3)
    return _dot(tri, hi, NN) + _dot(tri, mid, NN) + _dot(tri, lo, NN)


def _iota2(shape, dim):
    return lax.broadcasted_iota(jnp.int32, shape, dim)


def _log_sigmoid(z):
    return jnp.minimum(z, 0.0) - jnp.log1p(jnp.exp(-jnp.abs(z)))


def _log_sigmoid_abs(z):
    return jnp.minimum(z, 0.0) - jnp.log(1.0 + jnp.exp(-jnp.abs(z)))


def _head_spec(rows, width, off):
    return pl.BlockSpec((rows, width), lambda h, i: (i, off + h))


def _full_head_spec(T, width, off):
    return pl.BlockSpec((T, width), lambda h, i: (0, off + h))


def _sb_fwd(qkv, H):
    T = qkv.shape[0]
    B = _tile(T, SB_BLOCK)
    nq = T // B
    scale = 1.0 / math.sqrt(HEAD_DIM)

    HG = ATT_HEADS_FWD if H % ATT_HEADS_FWD == 0 else 1
    HD = HEAD_DIM

    def body(q_ref, k_ref, v_ref, o_ref, c_ref, tri_s):
        i = pl.program_id(1)
        tri = (_iota2((B, B), 0) > _iota2((B, B), 1)).astype(BF16)
        for p in range(SB_PARTS):
            tri_s[pl.ds(p * B, B), :] = tri
        rows = _iota2((B, B), 0)
        cols = _iota2((B, B), 1)

        def block(j, carry, diagonal):
            ks = pl.multiple_of(j * B, B)
            out = []
            for g in range(HG):
                acc, run = carry[g]
                q = q_ref[:, pl.ds(g * HD, HD)]
                kb = k_ref[pl.ds(ks, B), pl.ds(g * HD, HD)]
                vb = v_ref[pl.ds(ks, B), pl.ds(g * HD, HD)]
                z = _dot(q, kb, NT) * scale
                ls = _log_sigmoid_abs(z)
                lk = ls - z
                if diagonal:
                    lk = jnp.where(rows > cols, lk, 0.0)
                    ls = jnp.where(rows > cols, ls, NEG)
                parts = jnp.concatenate(_split(lk, SB_PARTS), axis=1)
                later = _dot(parts, tri_s[...], NN)
                a = jnp.exp(ls + later).astype(BF16)
                acc = acc + jnp.exp(run) * _dot(a, vb, NN)
                out.append((acc, run + jnp.sum(lk, axis=1, keepdims=True)))
            return tuple(out)

        init = tuple((jnp.zeros((B, HD), F32), jnp.zeros((B, 1), F32)) for _ in range(HG))
        res = block(i, init, True)
        res = lax.fori_loop(1, i + 1, lambda n, c: block(i - n, c, False), res)
        for g in range(HG):
            o_ref[:, pl.ds(g * HD, HD)] = res[g][0].astype(BF16)
            c_ref[g] = res[g][1]

    return pl.pallas_call(
        body, name="sb_attn_fwd", grid=(H // HG, nq),
        in_specs=[_head_spec(B, HG * HD, 0), _full_head_spec(T, HG * HD, H // HG),
                  _full_head_spec(T, HG * HD, 2 * H // HG)],
        out_specs=(_head_spec(B, HG * HD, 0), pl.BlockSpec((HG, B, 1), lambda h, i: (h, i, 0))),
        out_shape=(jax.ShapeDtypeStruct((T, H * HD), BF16), jax.ShapeDtypeStruct((H, T, 1), F32)),
        scratch_shapes=[pltpu.VMEM((SB_PARTS * B, B), BF16)],
        compiler_params=_params(("parallel", "arbitrary")),
    )(qkv, qkv, qkv)


def _sb_bwd(qkv, do, ctot, H):
    T = qkv.shape[0]
    B = _tile(T, SB_BLOCK)
    nq = T // B
    scale = 1.0 / math.sqrt(HEAD_DIM)
    HG = ATT_HEADS if H % ATT_HEADS == 0 else 1
    HD = HEAD_DIM

    def body(q_ref, k_ref, v_ref, do_ref, c_ref, dq_ref, dk_ref, dv_ref, dk_acc, dv_acc,
             tri_incl_s, tri_strict_s):
        i = pl.program_id(1)

        @pl.when(i == 0)
        def _():
            dk_acc[...] = jnp.zeros_like(dk_acc)
            dv_acc[...] = jnp.zeros_like(dv_acc)

        rows = _iota2((B, B), 0)
        cols = _iota2((B, B), 1)
        tri_incl = (rows <= cols).astype(BF16)
        tri_strict = (rows < cols).astype(BF16)
        for p in range(SB_PARTS):
            tri_incl_s[pl.ds(p * B, B), :] = tri_incl
            tri_strict_s[pl.ds(p * B, B), :] = tri_strict
        heads = [(q_ref[:, pl.ds(g * HD, HD)], do_ref[:, pl.ds(g * HD, HD)], c_ref[g]) for g in range(HG)]

        def block(j, carry, diagonal):
            ks = pl.multiple_of(j * B, B)
            out = []
            for g in range(HG):
                dq, lpre, gpre = carry[g]
                q, do_b, ctot_b = heads[g]
                kb = k_ref[pl.ds(ks, B), pl.ds(g * HD, HD)]
                vb = v_ref[pl.ds(ks, B), pl.ds(g * HD, HD)]
                z = _dot(q, kb, NT) * scale
                da = _dot(do_b, vb, NT)
                lsm = _log_sigmoid_abs(z)
                lk = lsm - z
                if diagonal:
                    lk = jnp.where(rows > cols, lk, 0.0)
                    lsm = jnp.where(rows > cols, lsm, NEG)
                incl = _dot(jnp.concatenate(_split(lk, SB_PARTS), axis=1), tri_incl_s[...], NN)
                a = jnp.exp(lsm + (ctot_b - lpre) - incl)
                gw = a * da
                gex = gpre + _dot(jnp.concatenate(_split(gw, SB_PARTS), axis=1), tri_strict_s[...], NN)
                dzb = ((gw * jnp.exp(lsm - z) - jnp.exp(lsm) * gex) * scale).astype(BF16)
                dq = dq + _dot(dzb, kb, NN)
                dk_acc[pl.ds(ks, B), pl.ds(g * HD, HD)] += _dot(dzb, q, TN)
                dv_acc[pl.ds(ks, B), pl.ds(g * HD, HD)] += _dot(a.astype(BF16), do_b, TN)
                out.append((dq, lpre + jnp.sum(lk, axis=1, keepdims=True),
                            gpre + jnp.sum(gw, axis=1, keepdims=True)))
            return tuple(out)

        init = tuple((jnp.zeros((B, HD), F32), jnp.zeros((B, 1), F32), jnp.zeros((B, 1), F32))
                     for _ in range(HG))
        res = lax.fori_loop(0, i, lambda j, c: block(j, c, False), init)
        res = block(i, res, True)
        for g in range(HG):
            dq_ref[:, pl.ds(g * HD, HD)] = res[g][0].astype(BF16)

        @pl.when(i == nq - 1)
        def _():
            dk_ref[...] = dk_acc[...].astype(BF16)
            dv_ref[...] = dv_acc[...].astype(BF16)

    W = H * HD
    return pl.pallas_call(
        body, name="sb_attn_bwd", grid=(H // HG, nq),
        in_specs=[_head_spec(B, HG * HD, 0), _full_head_spec(T, HG * HD, H // HG),
                  _full_head_spec(T, HG * HD, 2 * H // HG), _head_spec(B, HG * HD, 0),
                  pl.BlockSpec((HG, B, 1), lambda h, i: (h, i, 0))],
        out_specs=(_head_spec(B, HG * HD, 0), _full_head_spec(T, HG * HD, 0),
                   _full_head_spec(T, HG * HD, 0)),
        out_shape=tuple(jax.ShapeDtypeStruct((T, W), BF16) for _ in range(3)),
        scratch_shapes=[pltpu.VMEM((T, HG * HD), F32), pltpu.VMEM((T, HG * HD), F32),
                        pltpu.VMEM((SB_PARTS * B, B), BF16), pltpu.VMEM((SB_PARTS * B, B), BF16)],
        compiler_params=_params(("parallel", "arbitrary")),
    )(qkv, qkv, qkv, do, ctot)


def _softmax_fwd(q, k, v, cf, H, dqk, scale, *, name):
    T = q.shape[0]
    B = _tile(T, ATT_BLOCK)
    nq = T // B
    has_cf = cf is not None
    HG = ATT_HEADS_FWD if H % ATT_HEADS_FWD == 0 else 1
    HD = HEAD_DIM

    def body(*refs):
        q_ref, k_ref, v_ref = refs[:3]
        idx = 3
        if has_cf:
            cfc_ref, cfr_ref = refs[3], refs[4]
            idx = 5
        o_ref, of_ref, lse_ref = refs[idx:idx + 3]
        i = pl.program_id(1)
        rows = _iota2((B, B), 0)
        cols = _iota2((B, B), 1)

        def step(j, carry):
            ks = pl.multiple_of(j * B, B)
            mask = (ks + cols) <= (i * B + rows)
            out = []
            for g in range(HG):
                m, l, acc = carry[g]
                qb = q_ref[:, pl.ds(g * dqk, dqk)]
                kb = k_ref[pl.ds(ks, B), pl.ds(g * dqk, dqk)]
                vb = v_ref[pl.ds(ks, B), pl.ds(g * HD, HD)]
                s = _dot(qb, kb, NT) * scale
                if has_cf:
                    s = s + (cfc_ref[g] - cfr_ref[g, :, pl.ds(ks, B)])
                s = jnp.where(mask, s, NEG)
                m_new = jnp.maximum(m, jnp.max(s, axis=1, keepdims=True))
                alpha = jnp.exp(m - m_new)
                p = jnp.exp(s - m_new)
                l = alpha * l + jnp.sum(p, axis=1, keepdims=True)
                acc = alpha * acc + _dot(p.astype(BF16), vb, NN)
                out.append((m_new, l, acc))
            return tuple(out)

        init = tuple((jnp.full((B, 1), NEG, F32), jnp.zeros((B, 1), F32), jnp.zeros((B, HD), F32))
                     for _ in range(HG))
        res = lax.fori_loop(0, i + 1, step, init)
        for g in range(HG):
            m, l, acc = res[g]
            o = acc / l
            o_ref[:, pl.ds(g * HD, HD)] = o.astype(BF16)
            of_ref[:, pl.ds(g * HD, HD)] = o
            lse_ref[g] = m + jnp.log(l)

    stat = pl.BlockSpec((HG, B, 1), lambda h, i: (h, i, 0))
    in_specs = [_head_spec(B, HG * dqk, 0), _full_head_spec(T, HG * dqk, 0), _full_head_spec(T, HG * HD, 0)]
    args = [q, k, v]
    if has_cf:
        in_specs += [stat, pl.BlockSpec((HG, 1, T), lambda h, i: (h, 0, 0))]
        args += list(cf)
    W = H * HD
    return pl.pallas_call(
        body, name=name, grid=(H // HG, nq), in_specs=in_specs,
        out_specs=(_head_spec(B, HG * HD, 0), _head_spec(B, HG * HD, 0), stat),
        out_shape=(jax.ShapeDtypeStruct((T, W), BF16), jax.ShapeDtypeStruct((T, W), F32),
                   jax.ShapeDtypeStruct((H, T, 1), F32)),
        compiler_params=_params(("parallel", "arbitrary")),
    )(*args)


def _softmax_bwd(q, k, v, cf, do, o, lse, H, dqk, scale, *, name):
    T = q.shape[0]
    B = _tile(T, ATT_BLOCK)
    nq = T // B
    has_cf = cf is not None
    HG = ATT_HEADS if H % ATT_HEADS == 0 else 1
    HD = HEAD_DIM

    def body(*refs):
        q_ref, k_ref, v_ref, do_ref, o_ref, lse_ref = refs[:6]
        idx = 6
        if has_cf:
            cfc_ref, cfr_ref = refs[6], refs[7]
            idx = 8
        dq_ref, dk_ref, dv_ref = refs[idx:idx + 3]
        idx += 3
        if has_cf:
            dcc_ref, dcr_ref = refs[idx], refs[idx + 1]
        i = pl.program_id(1)

        @pl.when(i == 0)
        def _():
            dk_ref[...] = jnp.zeros_like(dk_ref)
            dv_ref[...] = jnp.zeros_like(dv_ref)
            if has_cf:
                dcr_ref[...] = jnp.zeros_like(dcr_ref)

        rows = _iota2((B, B), 0)
        cols = _iota2((B, B), 1)
        heads = []
        for g in range(HG):
            do_b = do_ref[:, pl.ds(g * HD, HD)]
            delta = jnp.sum(do_b.astype(F32) * o_ref[:, pl.ds(g * HD, HD)], axis=1, keepdims=True)
            heads.append((q_ref[:, pl.ds(g * dqk, dqk)], do_b, lse_ref[g], delta))

        def step(j, carry):
            ks = pl.multiple_of(j * B, B)
            mask = (ks + cols) <= (i * B + rows)
            out = []
            for g in range(HG):
                dq, rs = carry[g]
                qb, do_b, lse_b, delta = heads[g]
                kb = k_ref[pl.ds(ks, B), pl.ds(g * dqk, dqk)]
                vb = v_ref[pl.ds(ks, B), pl.ds(g * HD, HD)]
                s = _dot(qb, kb, NT) * scale
                if has_cf:
                    s = s + (cfc_ref[g] - cfr_ref[g, :, pl.ds(ks, B)])
                p = jnp.where(mask, jnp.exp(s - lse_b), 0.0)
                dp = _dot(do_b, vb, NT)
                ds = p * (dp - delta)
                dsb = (ds * scale).astype(BF16)
                dq = dq + _dot(dsb, kb, NN)
                dk_ref[pl.ds(ks, B), pl.ds(g * dqk, dqk)] += _dot(dsb, qb, TN)
                dv_ref[pl.ds(ks, B), pl.ds(g * HD, HD)] += _dot(p.astype(BF16), do_b, TN)
                if has_cf:
                    rs = rs + jnp.sum(ds, axis=1, keepdims=True)
                    dcr_ref[g, :, pl.ds(ks, B)] -= jnp.sum(ds, axis=0, keepdims=True)
                out.append((dq, rs))
            return tuple(out)

        init = tuple((jnp.zeros((B, dqk), F32), jnp.zeros((B, 1), F32)) for _ in range(HG))
        res = lax.fori_loop(0, i + 1, step, init)
        for g in range(HG):
            dq_ref[:, pl.ds(g * dqk, dqk)] = res[g][0]
            if has_cf:
                dcc_ref[g] = res[g][1]

    stat = pl.BlockSpec((HG, B, 1), lambda h, i: (h, i, 0))
    rowstat = pl.BlockSpec((HG, 1, T), lambda h, i: (h, 0, 0))
    in_specs = [_head_spec(B, HG * dqk, 0), _full_head_spec(T, HG * dqk, 0), _full_head_spec(T, HG * HD, 0),
                _head_spec(B, HG * HD, 0), _head_spec(B, HG * HD, 0), stat]
    args = [q, k, v, do, o, lse]
    out_specs = [_head_spec(B, HG * dqk, 0), _full_head_spec(T, HG * dqk, 0), _full_head_spec(T, HG * HD, 0)]
    out_shape = [jax.ShapeDtypeStruct((T, H * dqk), F32), jax.ShapeDtypeStruct((T, H * dqk), F32),
                 jax.ShapeDtypeStruct((T, H * HD), F32)]
    if has_cf:
        in_specs += [stat, rowstat]
        args += list(cf)
        out_specs += [stat, rowstat]
        out_shape += [jax.ShapeDtypeStruct((H, T, 1), F32), jax.ShapeDtypeStruct((H, 1, T), F32)]
    return pl.pallas_call(
        body, name=name, grid=(H // HG, nq), in_specs=in_specs, out_specs=tuple(out_specs),
        out_shape=tuple(out_shape), compiler_params=_params(("parallel", "arbitrary")),
    )(*args)


def _headnorm(x, g):
    r = lax.rsqrt(jnp.mean(x * x, axis=-1, keepdims=True) + EPS)
    return x * r * g, r


def _headnorm_bwd(x, g, dy, n):
    r = lax.rsqrt(jnp.sum(x * x, axis=-1, keepdims=True) * (1.0 / n) + EPS)
    t = dy * g
    dx = r * t - x * (r * r * r * (jnp.sum(t * x, axis=-1, keepdims=True) * (1.0 / n)))
    dg = jnp.sum(dy * x * r, axis=0, keepdims=True)
    return dx, dg


def _fox_prep_fwd(qkv, gq, gk, H, *, tm=512):
    T = qkv.shape[0]
    tm = _tile(T, tm)

    def body(q_ref, k_ref, v_ref, gq_ref, gk_ref, qn_ref, kn_ref, vb_ref):
        qn_ref[...] = _headnorm(q_ref[...], gq_ref[...])[0].astype(BF16)
        kn_ref[...] = _headnorm(k_ref[...], gk_ref[...])[0].astype(BF16)
        vb_ref[...] = v_ref[...].astype(BF16)

    def blk(off):
        return pl.BlockSpec((tm, HEAD_DIM), lambda i, h: (i, off + h))

    vec = pl.BlockSpec((1, HEAD_DIM), lambda i, h: (0, 0))
    W = H * HEAD_DIM
    return pl.pallas_call(
        body, name="fox_prep_fwd", grid=(T // tm, H),
        in_specs=[blk(0), blk(H), blk(2 * H), vec, vec], out_specs=(blk(0), blk(0), blk(0)),
        out_shape=tuple(jax.ShapeDtypeStruct((T, W), BF16) for _ in range(3)),
        compiler_params=_params(("parallel", "parallel")),
    )(qkv, qkv, qkv, gq, gk)


def _fox_prep_bwd(qkv, gq, gk, dqn, dkn, dv, H, *, tm=512):
    T = qkv.shape[0]
    tm = _tile(T, tm)

    def body(q_ref, k_ref, gq_ref, gk_ref, dqn_ref, dkn_ref, dv_ref,
             dq_ref, dk_ref, dvb_ref, dgq_ref, dgk_ref):
        @pl.when((pl.program_id(0) == 0) & (pl.program_id(1) == 0))
        def _():
            dgq_ref[...] = jnp.zeros_like(dgq_ref)
            dgk_ref[...] = jnp.zeros_like(dgk_ref)

        dq, dgq = _headnorm_bwd(q_ref[...], gq_ref[...], dqn_ref[...], HEAD_DIM)
        dk, dgk = _headnorm_bwd(k_ref[...], gk_ref[...], dkn_ref[...], HEAD_DIM)
        dq_ref[...] = dq.astype(BF16)
        dk_ref[...] = dk.astype(BF16)
        dvb_ref[...] = dv_ref[...].astype(BF16)
        dgq_ref[...] += dgq
        dgk_ref[...] += dgk

    def blk(off):
        return pl.BlockSpec((tm, HEAD_DIM), lambda i, h: (i, off + h))

    vec = pl.BlockSpec((1, HEAD_DIM), lambda i, h: (0, 0))
    W = H * HEAD_DIM
    return pl.pallas_call(
        body, name="fox_prep_bwd", grid=(T // tm, H),
        in_specs=[blk(0), blk(H), vec, vec, blk(0), blk(0), blk(0)],
        out_specs=(blk(0), blk(0), blk(0), vec, vec),
        out_shape=tuple(jax.ShapeDtypeStruct((T, W), BF16) for _ in range(3))
        + (jax.ShapeDtypeStruct((1, HEAD_DIM), F32), jax.ShapeDtypeStruct((1, HEAD_DIM), F32)),
        compiler_params=_params(("arbitrary", "arbitrary")),
    )(qkv, qkv, gq, gk, dqn, dkn, dv)


def _fox_gate_fwd(flog, bf, *, tm=256):
    T = flog.shape[0]
    tm = _tile(T, tm)

    def body(f_ref, b_ref, cf_ref, carry):
        @pl.when(pl.program_id(0) == 0)
        def _():
            carry[...] = jnp.zeros_like(carry)

        lf = _log_sigmoid(f_ref[...] + b_ref[...])
        tri = (_iota2((tm, tm), 1) <= _iota2((tm, tm), 0)).astype(BF16)
        cf_ref[...] = carry[...] + _tri_left(tri, lf)
        carry[...] += jnp.sum(lf, axis=0, keepdims=True)

    return pl.pallas_call(
        body, name="fox_gate_fwd", grid=(T // tm,),
        in_specs=[pl.BlockSpec((tm, LANE), lambda i: (i, 0)), pl.BlockSpec((1, LANE), lambda i: (0, 0))],
        out_specs=pl.BlockSpec((tm, LANE), lambda i: (i, 0)),
        out_shape=jax.ShapeDtypeStruct((T, LANE), F32),
        scratch_shapes=[pltpu.VMEM((1, LANE), F32)],
        compiler_params=_params(("arbitrary",)),
    )(flog, bf)


def _fox_gate_bwd(flog, bf, dcf, *, tm=256):
    T = flog.shape[0]
    tm = _tile(T, tm)
    nt = T // tm

    def body(f_ref, b_ref, dcf_ref, df_ref, db_ref, carry):
        @pl.when(pl.program_id(0) == 0)
        def _():
            carry[...] = jnp.zeros_like(carry)
            db_ref[...] = jnp.zeros_like(db_ref)

        d = dcf_ref[...]
        tri = (_iota2((tm, tm), 1) >= _iota2((tm, tm), 0)).astype(BF16)
        dlf = carry[...] + _tri_left(tri, d)
        carry[...] += jnp.sum(d, axis=0, keepdims=True)
        xg = f_ref[...] + b_ref[...]
        e = jnp.exp(-jnp.abs(xg))
        sig_neg = jnp.where(xg >= 0.0, e, 1.0) / (1.0 + e)
        df = dlf * sig_neg
        df_ref[...] = df.astype(BF16)
        db_ref[...] += jnp.sum(df, axis=0, keepdims=True)

    rev = pl.BlockSpec((tm, LANE), lambda i: (nt - 1 - i, 0))
    vec = pl.BlockSpec((1, LANE), lambda i: (0, 0))
    return pl.pallas_call(
        body, name="fox_gate_bwd", grid=(nt,), in_specs=[rev, vec, rev], out_specs=(rev, vec),
        out_shape=(jax.ShapeDtypeStruct((T, LANE), BF16), jax.ShapeDtypeStruct((1, LANE), F32)),
        scratch_shapes=[pltpu.VMEM((1, LANE), F32)],
        compiler_params=_params(("arbitrary",)),
    )(flog, bf, dcf)


def _rope_tables(positions):
    half = MLA_ROPE // 2
    inv_freq = ROPE_THETA ** (-jnp.arange(0, half, dtype=F32) * 2.0 / MLA_ROPE)
    ang = positions.astype(F32)[:, None] * inv_freq
    cos, sin = jnp.cos(ang), jnp.sin(ang)
    zero = jnp.zeros_like(cos)
    pad = jnp.zeros((positions.shape[0], LANE - MLA_ROPE), F32)
    cos_t = jnp.concatenate([cos, cos, pad], axis=1)
    sin_up = jnp.concatenate([zero, sin, pad], axis=1)
    sin_dn = jnp.concatenate([-sin, zero, pad], axis=1)
    return cos_t, sin_up, sin_dn


def _rope(x, cos_t, sin_up, sin_dn):
    half = MLA_ROPE // 2
    return x * cos_t + pltpu.roll(x, half, 1) * sin_up + pltpu.roll(x, LANE - half, 1) * sin_dn


def _rope_t(d, cos_t, sin_up, sin_dn):
    half = MLA_ROPE // 2
    return d * cos_t + pltpu.roll(d * sin_up, LANE - half, 1) + pltpu.roll(d * sin_dn, half, 1)


def _norm192(xcat, g):
    r = lax.rsqrt(jnp.sum(xcat * xcat, axis=-1, keepdims=True) * (1.0 / MLA_QK) + EPS)
    return xcat * r * g


def _mla_prep_fwd(qfull, kv, kr, tabs, gq, gk, H, *, tm=512):
    T = qfull.shape[0]
    tm = _tile(T, tm)

    def body(q_ref, kv_ref, kr_ref, c_ref, su_ref, sd_ref, gq_ref, gk_ref, qf_ref, kf_ref, v_ref):
        tabs_b = (c_ref[...], su_ref[...], sd_ref[...])
        qb = q_ref[...]
        qcat = jnp.concatenate([qb[:, :MLA_NOPE], _rope(qb[:, MLA_NOPE:], *tabs_b)], axis=1)
        qf_ref[...] = _norm192(qcat, gq_ref[...]).astype(BF16)
        kvb = kv_ref[...]
        kcat = jnp.concatenate([kvb[:, :MLA_NOPE], _rope(kr_ref[...], *tabs_b)], axis=1)
        kf_ref[...] = _norm192(kcat, gk_ref[...]).astype(BF16)
        v_ref[...] = kvb[:, MLA_NOPE:].astype(BF16)

    head = pl.BlockSpec((tm, MLA_QK_PAD), lambda i, h: (i, h))
    tok = pl.BlockSpec((tm, LANE), lambda i, h: (i, 0))
    vec = pl.BlockSpec((1, MLA_QK_PAD), lambda i, h: (0, 0))
    return pl.pallas_call(
        body, name="mla_prep_fwd", grid=(T // tm, H),
        in_specs=[head, head, tok, tok, tok, tok, vec, vec],
        out_specs=(head, head, pl.BlockSpec((tm, MLA_V), lambda i, h: (i, h))),
        out_shape=(jax.ShapeDtypeStruct((T, H * MLA_QK_PAD), BF16),
                   jax.ShapeDtypeStruct((T, H * MLA_QK_PAD), BF16),
                   jax.ShapeDtypeStruct((T, H * MLA_V), BF16)),
        compiler_params=_params(("parallel", "parallel")),
    )(qfull, kv, kr, *tabs, gq, gk)


def _mla_prep_bwd(qfull, kv, kr, tabs, gq, gk, dqf, dkf, dv, H, *, tm=512):
    T = qfull.shape[0]
    tm = _tile(T, tm)

    def body(q_ref, kv_ref, kr_ref, c_ref, su_ref, sd_ref, gq_ref, gk_ref, dqf_ref, dkf_ref, dv_ref,
             dq_ref, dkv_ref, dkr_ref, dgq_ref, dgk_ref, kr_acc):
        h = pl.program_id(1)

        @pl.when((pl.program_id(0) == 0) & (h == 0))
        def _():
            dgq_ref[...] = jnp.zeros_like(dgq_ref)
            dgk_ref[...] = jnp.zeros_like(dgk_ref)

        @pl.when(h == 0)
        def _():
            kr_acc[...] = jnp.zeros_like(kr_acc)

        tabs_b = (c_ref[...], su_ref[...], sd_ref[...])
        qb = q_ref[...]
        qcat = jnp.concatenate([qb[:, :MLA_NOPE], _rope(qb[:, MLA_NOPE:], *tabs_b)], axis=1)
        dqcat, dgq = _headnorm_bwd(qcat, gq_ref[...], dqf_ref[...], MLA_QK)
        dq_ref[...] = jnp.concatenate(
            [dqcat[:, :MLA_NOPE], _rope_t(dqcat[:, MLA_NOPE:], *tabs_b)], axis=1).astype(BF16)
        dgq_ref[...] += dgq
        kvb = kv_ref[...]
        kcat = jnp.concatenate([kvb[:, :MLA_NOPE], _rope(kr_ref[...], *tabs_b)], axis=1)
        dkcat, dgk = _headnorm_bwd(kcat, gk_ref[...], dkf_ref[...], MLA_QK)
        dkv_ref[...] = jnp.concatenate([dkcat[:, :MLA_NOPE], dv_ref[...]], axis=1).astype(BF16)
        dgk_ref[...] += dgk
        kr_acc[...] += dkcat[:, MLA_NOPE:]

        @pl.when(h == H - 1)
        def _():
            dkr_ref[...] = _rope_t(kr_acc[...], *tabs_b).astype(BF16)

    head = pl.BlockSpec((tm, MLA_QK_PAD), lambda i, h: (i, h))
    tok = pl.BlockSpec((tm, LANE), lambda i, h: (i, 0))
    vec = pl.BlockSpec((1, MLA_QK_PAD), lambda i, h: (0, 0))
    return pl.pallas_call(
        body, name="mla_prep_bwd", grid=(T // tm, H),
        in_specs=[head, head, tok, tok, tok, tok, vec, vec, head, head,
                  pl.BlockSpec((tm, MLA_V), lambda i, h: (i, h))],
        out_specs=(head, head, tok, vec, vec),
        out_shape=(jax.ShapeDtypeStruct((T, H * MLA_QK_PAD), BF16),
                   jax.ShapeDtypeStruct((T, H * MLA_QK_PAD), BF16),
                   jax.ShapeDtypeStruct((T, LANE), BF16),
                   jax.ShapeDtypeStruct((1, MLA_QK_PAD), F32), jax.ShapeDtypeStruct((1, MLA_QK_PAD), F32)),
        scratch_shapes=[pltpu.VMEM((tm, LANE), F32)],
        compiler_params=_params(("arbitrary", "arbitrary")),
    )(qfull, kv, kr, *tabs, gq, gk, dqf, dkf, dv)


def _cf_layouts(cf, H):
    cfh = cf[:, :H].T
    return cfh[:, :, None], cfh[:, None, :]


def _layer_fwd(i, x, P, tabs, H, hook=None):
    kind, j = i % 3, i // 3
    s = {"x_in": x}
    h = _rmsnorm_fwd(x, P["mix_norm"][i:i + 1], name="mix_norm_fwd")
    s["h"] = h
    if kind == 0:
        w = P["sb"][j]
        qkv = _mm(h, w["w_in"], out_dtype=BF16, name="sb_qkv")
        o, ctot = _sb_fwd(qkv, H)
        s.update(qkv=qkv, o=o, ctot=ctot)
    elif kind == 1:
        w = P["fox"][j]
        qkv = _mm(h, w["w_qkv"], name="fox_qkv")
        flog = _mm(h, w["w_f"], name="fox_gate_proj")
        qn, kn, vb = _fox_prep_fwd(qkv, w["gq"], w["gk"], H)
        cf = _cf_layouts(_fox_gate_fwd(flog, w["bf"]), H)
        o, of, lse = _softmax_fwd(qn, kn, vb, cf, H, HEAD_DIM, 1.0 / math.sqrt(HEAD_DIM), name="fox_attn_fwd")
        s.update(qkv=qkv, flog=flog, qn=qn, kn=kn, vb=vb, cf=cf, o=o, of=of, lse=lse)
    else:
        w = P["mla"][j]
        dq = _mm(h, w["w_dq"], name="mla_down_q")
        dkv = _mm(h, w["w_dkv"], name="mla_down_kv")
        kr = _mm(h, w["w_dr"], name="mla_down_rope")
        cq = _rmsnorm_fwd(dq, w["q_norm"], name="mla_q_norm_fwd")
        ckv = _rmsnorm_fwd(dkv, w["kv_norm"], name="mla_kv_norm_fwd")
        qfull = _mm(cq, w["w_uq"], name="mla_up_q")
        kv = _mm(ckv, w["w_ukv"], name="mla_up_kv")
        qf, kf, v = _mla_prep_fwd(qfull, kv, kr, tabs, w["gq"], w["gk"], H)
        o, of, lse = _softmax_fwd(qf, kf, v, None, H, MLA_QK_PAD, 1.0 / math.sqrt(MLA_QK), name="mla_attn_fwd")
        s.update(dq=dq, dkv=dkv, kr=kr, cq=cq, ckv=ckv, qfull=qfull, kv=kv, qf=qf, kf=kf, v=v,
                 o=o, of=of, lse=lse)
    x = _mm(s["o"], w["w_out"], res=x, name="mixer_out")
    s["x_mid"] = x
    deps = hook(x) if hook is not None else ()
    h2 = _rmsnorm_fwd(x, P["mlp_norm"][i:i + 1], name="mlp_norm_fwd")
    z, u = _mm(h2, P["mlp"][i]["w1"], act="relu2", name="mlp_up", deps=deps)
    x = _mm(u, P["mlp"][i]["w2"], res=x, name="mlp_down")
    s.update(h2=h2, z=z, u=u)
    return x, s


def _layer_bwd(i, dx, dxb, s, P, tabs, H, deps=(), hook=None, hook2=None):
    kind, j = i % 3, i // 3
    G = {}
    wm = P["mlp"][i]
    dz = _mm(dxb, wm["w2"], mode="nt", act="drelu2", z=s["z"], out_dtype=BF16, name="mlp_down_bwd",
             deps=deps)
    G["w2"] = _mm(s["u"], dxb, mode="tn", out_dtype=BF16, name="mlp_w2_grad")
    G["w1"] = _mm(s["h2"], dz, mode="tn", out_dtype=BF16, name="mlp_w1_grad")
    dh2 = _mm(dz, wm["w1"], mode="nt", name="mlp_up_bwd")
    dx, dxb, G["mlp_norm"] = _rmsnorm_bwd(s["x_mid"], P["mlp_norm"][i:i + 1], dh2, dx, name="mlp_norm_bwd")
    h = s["h"]
    w = P[("sb", "fox", "mla")[kind]][j]
    do = _mm(dxb, w["w_out"], mode="nt", out_dtype=BF16, name="mixer_out_bwd",
             deps=hook(dxb, G) if hook is not None else ())
    G["w_out"] = _mm(s["o"], dxb, mode="tn", out_dtype=BF16, name="mixer_out_grad")
    if kind == 0:
        dq, dk, dv = _sb_bwd(s["qkv"], do, s["ctot"], H)
        dproj = jnp.concatenate([dq, dk, dv], axis=1)
        G["w_in"] = _mm(h, dproj, mode="tn", out_dtype=BF16, name="sb_qkv_grad",
                        deps=hook2(dv) if hook2 is not None else ())
        dh = _mm(dproj, w["w_in"], mode="nt", name="sb_qkv_bwd")
    elif kind == 1:
        dqn, dkn, dv, dcc, dcr = _softmax_bwd(
            s["qn"], s["kn"], s["vb"], s["cf"], do, s["of"], s["lse"], H, HEAD_DIM,
            1.0 / math.sqrt(HEAD_DIM), name="fox_attn_bwd")
        dq, dk, dvb, G["gq"], G["gk"] = _fox_prep_bwd(s["qkv"], w["gq"], w["gk"], dqn, dkn, dv, H)
        dcf = (dcc[:, :, 0] + dcr[:, 0, :]).T
        dcf = jnp.pad(dcf, ((0, 0), (0, LANE - H)))
        dflog, G["bf"] = _fox_gate_bwd(s["flog"], w["bf"], dcf)
        dproj = jnp.concatenate([dq, dk, dvb], axis=1)
        G["w_qkv"] = _mm(h, dproj, mode="tn", out_dtype=BF16, name="fox_qkv_grad",
                         deps=hook2(dv) if hook2 is not None else ())
        G["w_f"] = _mm(h, dflog, mode="tn", out_dtype=BF16, name="fox_gate_grad")
        dh = _mm(dproj, w["w_qkv"], mode="nt", name="fox_qkv_bwd")
        dh = _mm(dflog, w["w_f"], mode="nt", res=dh, name="fox_gate_bwd_proj")
    else:
        dqf, dkf, dv = _softmax_bwd(
            s["qf"], s["kf"], s["v"], None, do, s["of"], s["lse"], H, MLA_QK_PAD,
            1.0 / math.sqrt(MLA_QK), name="mla_attn_bwd")
        dqfull, dkv, dkr, G["gq"], G["gk"] = _mla_prep_bwd(
            s["qfull"], s["kv"], s["kr"], tabs, w["gq"], w["gk"], dqf, dkf, dv, H)
        G["w_uq"] = _mm(s["cq"], dqfull, mode="tn", out_dtype=BF16, name="mla_up_q_grad",
                        deps=hook2(dv) if hook2 is not None else ())
        G["w_ukv"] = _mm(s["ckv"], dkv, mode="tn", out_dtype=BF16, name="mla_up_kv_grad")
        dcq = _mm(dqfull, w["w_uq"], mode="nt", name="mla_up_q_bwd")
        dckv = _mm(dkv, w["w_ukv"], mode="nt", name="mla_up_kv_bwd")
        ddq, G["q_norm"] = _rmsnorm_bwd(s["dq"], w["q_norm"], dcq, want_f32=False, name="mla_q_norm_bwd")
        ddkv, G["kv_norm"] = _rmsnorm_bwd(s["dkv"], w["kv_norm"], dckv, want_f32=False, name="mla_kv_norm_bwd")
        G["w_dq"] = _mm(h, ddq, mode="tn", out_dtype=BF16, name="mla_down_q_grad")
        G["w_dkv"] = _mm(h, ddkv, mode="tn", out_dtype=BF16, name="mla_down_kv_grad")
        G["w_dr"] = _mm(h, dkr, mode="tn", out_dtype=BF16, name="mla_down_rope_grad")
        dh = _mm(ddq, w["w_dq"], mode="nt", name="mla_down_q_bwd")
        dh = _mm(ddkv, w["w_dkv"], mode="nt", res=dh, name="mla_down_kv_bwd")
        dh = _mm(dkr, w["w_dr"], mode="nt", res=dh, name="mla_down_rope_bwd")
    dx, dxb, G["mix_norm"] = _rmsnorm_bwd(s["x_in"], P["mix_norm"][i:i + 1], dh, dx, name="mix_norm_bwd")
    return dx, dxb, G


def _local_step(x, positions, loss_target, P, depth, H):
    tabs = _rope_tables(positions)
    saved = []
    for i in range(depth):
        x, s = _layer_fwd(i, x, P, tabs, H)
        saved.append(s)
    dx, dxb, loss = _loss_head(x, loss_target)
    grads = [None] * depth
    for i in reversed(range(depth)):
        dx, dxb, grads[i] = _layer_bwd(i, dx, dxb, saved[i], P, tabs, H)
    return loss, dx, grads


def _pad_cols(a, n):
    return jnp.pad(a, ((0, 0), (0, n - a.shape[1])))


def _prepare_layer(P, full, i, H, parts=("mix", "mlp")):
    W = H * HEAD_DIM
    kind, j = i % 3, i // 3
    if "mlp" in parts:
        P["mlp"][i] = {"w1": full["mlp_w1"][i], "w2": full["mlp_w2"][i]}
    if "mix" not in parts:
        return
    if kind == 0:
        P["sb"][j] = {"w_in": full["sb_w_in"][j], "w_out": full["sb_w_out"][j]}
    elif kind == 1:
        w = full["fox_w_in"][j]
        P["fox"][j] = {
            "w_qkv": w[:, :3 * W], "w_f": _pad_cols(w[:, 3 * W:], LANE),
            "bf": _pad_cols(full["fox_b_f"][j:j + 1], LANE),
            "gq": full["fox_q_gain"][j:j + 1], "gk": full["fox_k_gain"][j:j + 1],
            "w_out": full["fox_w_out"][j]}
    else:
        w = full["mla_w_in"][j]
        q_norm, kv_norm = full["mla_q_norm"][j], full["mla_kv_norm"][j]
        rq, rkv = q_norm.shape[0], kv_norm.shape[0]
        w_uq = full["mla_w_uq"][j].reshape(rq, H, MLA_QK)
        w_uq = jnp.pad(w_uq, ((0, 0), (0, 0), (0, MLA_QK_PAD - MLA_QK))).reshape(rq, H * MLA_QK_PAD)
        P["mla"][j] = {
            "w_dq": w[:, :rq], "w_dkv": w[:, rq:rq + rkv], "w_dr": _pad_cols(w[:, rq + rkv:], LANE),
            "q_norm": q_norm[None], "kv_norm": kv_norm[None],
            "w_uq": w_uq, "w_ukv": full["mla_w_ukv"][j],
            "gq": _pad_cols(full["mla_q_gain"][j:j + 1], MLA_QK_PAD),
            "gk": _pad_cols(full["mla_k_gain"][j:j + 1], MLA_QK_PAD),
            "w_out": full["mla_w_out"][j]}


def _prepare(full, H):
    depth = len(full["mlp_w1"])
    P = {"mix_norm": full["mix_norm"], "mlp_norm": full["mlp_norm"], "mlp": [None] * depth,
         "sb": [None] * len(full["sb_w_in"]), "fox": [None] * len(full["fox_w_in"]),
         "mla": [None] * len(full["mla_w_in"])}
    for i in range(depth):
        _prepare_layer(P, full, i, H)
    return P


def _layer_grads(i, G, H):
    kind = i % 3
    out = {"mix_norm": G["mix_norm"], "mlp_norm": G["mlp_norm"], "mlp_w1": G["w1"], "mlp_w2": G["w2"]}
    if kind == 0:
        out.update(sb_w_in=G["w_in"], sb_w_out=G["w_out"])
    elif kind == 1:
        out.update(fox_w_in=jnp.concatenate([G["w_qkv"], G["w_f"][:, :H]], axis=1), fox_b_f=G["bf"][:, :H],
                   fox_q_gain=G["gq"], fox_k_gain=G["gk"], fox_w_out=G["w_out"])
    else:
        rq = G["w_uq"].shape[0]
        out.update(
            mla_w_in=jnp.concatenate([G["w_dq"], G["w_dkv"], G["w_dr"][:, :MLA_ROPE]], axis=1),
            mla_q_norm=G["q_norm"], mla_kv_norm=G["kv_norm"],
            mla_w_uq=G["w_uq"].reshape(rq, H, MLA_QK_PAD)[:, :, :MLA_QK].reshape(rq, H * MLA_QK),
            mla_w_ukv=G["w_ukv"], mla_q_gain=G["gq"][:, :MLA_QK], mla_k_gain=G["gk"][:, :MLA_QK],
            mla_w_out=G["w_out"])
    return out


def _layer_of(name, j):
    if name.startswith("sb_"):
        return 3 * j
    if name.startswith("fox_"):
        return 3 * j + 1
    if name.startswith("mla_"):
        return 3 * j + 2
    return j


def _unprepare(grads, H):
    out = {}
    for i, G in enumerate(grads):
        for n, g in _layer_grads(i, G, H).items():
            out.setdefault(n, []).append(g)
    return out


class _Place:
    def __init__(self, x, y, c):
        self.x, self.y, self.c = x, y, c
        self.dev = 4 * x + 2 * y + c
        self.chip = 2 * x + y
        self.id = (x, y, c)


def _peers(me, kind):
    if kind == "ici":
        return [_Place(1 - me.x, me.y, me.c), _Place(me.x, 1 - me.y, me.c), _Place(1 - me.x, 1 - me.y, me.c)]
    return [_Place(me.x, me.y, 1 - me.c)]


def _exchange(name, kind, operands, out_shapes, aliases, n_remote, n_local, plan):
    n_in, n_out = len(operands), len(out_shapes)

    def body(*refs):
        in_refs, out_refs = refs[:n_in], refs[n_in:n_in + n_out]
        send_sems, recv_sems, local_sems = refs[n_in + n_out:]
        me = _Place(lax.axis_index("x"), lax.axis_index("y"), lax.axis_index("c"))
        peers = _peers(me, kind)
        remote, local = plan(me, peers, in_refs, out_refs)
        assert len(remote) == n_remote and len(local) == n_local
        sends = []
        for n, (src, dst, k, _) in enumerate(remote):
            cp = pltpu.make_async_remote_copy(
                src_ref=src, dst_ref=dst, send_sem=send_sems.at[n], recv_sem=recv_sems.at[n],
                device_id=peers[k].id, device_id_type=MESH)
            cp.start()
            sends.append(cp)
        own = []
        for n, (src, dst) in enumerate(local):
            cp = pltpu.make_async_copy(src, dst, local_sems.at[n])
            cp.start()
            own.append(cp)
        for n, (src, _, k, landing) in enumerate(remote):
            pltpu.make_async_remote_copy(
                src_ref=src, dst_ref=landing, send_sem=send_sems.at[n], recv_sem=recv_sems.at[n],
                device_id=peers[k].id, device_id_type=MESH).wait_recv()
        for cp in sends:
            cp.wait_send()
        for cp in own:
            cp.wait()

    outs = pl.pallas_call(
        body, name=name, in_specs=[ANY] * n_in, out_specs=tuple([ANY] * n_out),
        out_shape=tuple(out_shapes), input_output_aliases=aliases,
        scratch_shapes=[pltpu.SemaphoreType.DMA((n_remote,)), pltpu.SemaphoreType.DMA((n_remote,)),
                        pltpu.SemaphoreType.DMA((max(n_local, 1),))],
    )(*operands)
    return list(outs)


def _window(ref, kind, d, shard_shape):
    r, c = shard_shape
    if kind == "col":
        return ref.at[:, pl.ds(pl.multiple_of(d * c, c), c)]
    return ref.at[pl.ds(pl.multiple_of(d * r, r), r), :]


def _full_shape(kind, shard_shape):
    r, c = shard_shape
    return (r, N_DEV * c) if kind == "col" else (N_DEV * r, c)


HBM = pl.BlockSpec(memory_space=pltpu.HBM)
SEM = pl.BlockSpec(memory_space=pltpu.SEMAPHORE)
EFFECT = pltpu.SideEffectType.DATAFLOW_SIDE_EFFECTING


class _Flight:
    def __init__(self, name, kind, n_remote, plan, send_sems, recv_sems, bufs, token):
        self.name, self.kind, self.n_remote, self.plan = name, kind, n_remote, plan
        self.send_sems, self.recv_sems, self.bufs, self.token = send_sems, recv_sems, bufs, token


def _start_copies(name, kind, bufs, n_remote, plan, after=None):
    nb = len(bufs)
    n_after = 0 if after is None else 1

    def body(*refs):
        in_refs = refs[:nb]
        send_sems, recv_sems = refs[nb + n_after], refs[nb + n_after + 1]
        token = refs[2 * nb + n_after + 2]
        me = _Place(lax.axis_index("x"), lax.axis_index("y"), lax.axis_index("c"))
        peers = _peers(me, kind)
        for n, (src, dst, k, _) in enumerate(plan(me, peers, in_refs)):
            pltpu.make_async_remote_copy(
                src_ref=src, dst_ref=dst, send_sem=send_sems.at[n], recv_sem=recv_sems.at[n],
                device_id=peers[k].id, device_id_type=MESH).start()
        token[...] = jnp.zeros_like(token)

    outs = pl.pallas_call(
        body, name=name, in_specs=[HBM] * nb + [ANY] * n_after,
        out_specs=(SEM, SEM) + (HBM,) * nb + (pl.BlockSpec(memory_space=pltpu.VMEM),),
        out_shape=(pltpu.SemaphoreType.DMA((n_remote,)), pltpu.SemaphoreType.DMA((n_remote,)))
        + tuple(pltpu.HBM(b.shape, b.dtype) for b in bufs) + (jax.ShapeDtypeStruct((8, LANE), F32),),
        input_output_aliases={a: 2 + a for a in range(nb)},
        compiler_params=pltpu.CompilerParams(has_side_effects=EFFECT),
    )(*[pltpu.with_memory_space_constraint(b, pltpu.HBM) for b in bufs], *([after] if n_after else []))
    return _Flight(name, kind, n_remote, plan, outs[0], outs[1], list(outs[2:2 + nb]), outs[2 + nb])


def _wait_copies(flight, after):
    nb = len(flight.bufs)
    plan, kind = flight.plan, flight.kind
    after = list(after) if isinstance(after, (tuple, list)) else [after]

    def body(*refs):
        in_refs = refs[:nb]
        send_sems, recv_sems = refs[nb], refs[nb + 1]
        me = _Place(lax.axis_index("x"), lax.axis_index("y"), lax.axis_index("c"))
        peers = _peers(me, kind)
        for n, (src, _, k, landing) in enumerate(plan(me, peers, in_refs)):
            cp = pltpu.make_async_remote_copy(
                src_ref=src, dst_ref=landing, send_sem=send_sems.at[n], recv_sem=recv_sems.at[n],
                device_id=peers[k].id, device_id_type=MESH)
            cp.wait_send()
            cp.wait_recv()

    outs = pl.pallas_call(
        body, name=flight.name.replace("_start", "_wait"),
        in_specs=[HBM] * nb + [SEM, SEM] + [ANY] * len(after), out_specs=(HBM,) * nb,
        out_shape=tuple(pltpu.HBM(b.shape, b.dtype) for b in flight.bufs),
        input_output_aliases={a: a for a in range(nb)},
        compiler_params=pltpu.CompilerParams(has_side_effects=EFFECT),
    )(*flight.bufs, flight.send_sems, flight.recv_sems, *after)
    return list(outs)


def _own_in_place(shard, kind, dev, dtype, deps=()):
    r, c = shard.shape
    tr = _row_tile(r, c)
    nb = r // tr

    def body(dev_ref, s_ref, *rest):
        o_ref = rest[-1]
        o_ref[...] = s_ref[...].astype(dtype)

    if kind == "col":
        o_spec = pl.BlockSpec((tr, c), lambda i, dev_ref: (i, dev_ref[0]))
    else:
        o_spec = pl.BlockSpec((tr, c), lambda i, dev_ref: (dev_ref[0] * nb + i, 0))
    return pl.pallas_call(
        body, name="own_in_place",
        grid_spec=pltpu.PrefetchScalarGridSpec(
            num_scalar_prefetch=1, grid=(nb,),
            in_specs=[pl.BlockSpec((tr, c), lambda i, dev_ref: (i, 0))] + [ANY] * len(deps), out_specs=o_spec),
        out_shape=jax.ShapeDtypeStruct(_full_shape(kind, shard.shape), dtype),
        compiler_params=_params(("parallel",)),
    )(dev, shard, *deps)


def _gather_ici_start(fulls, kinds, shapes, tag, after=None):
    n = len(fulls)

    def plan(me, peers, refs):
        remote = []
        for a in range(n):
            mine = _window(refs[a], kinds[a], me.dev, shapes[a])
            for k, p in enumerate(peers):
                remote.append((mine, mine, k, _window(refs[a], kinds[a], p.dev, shapes[a])))
        return remote

    return _start_copies("gather_ici_start_" + tag, "ici", list(fulls), 3 * n, plan, after)


def _gather_d2d_start(fulls, kinds, shapes, tag):
    n = len(fulls)

    def plan(me, peers, refs):
        remote = []
        for a in range(n):
            for ch in range(N_CHIP):
                held = _window(refs[a], kinds[a], 2 * ch + me.c, shapes[a])
                remote.append((held, held, 0, _window(refs[a], kinds[a], 2 * ch + 1 - me.c, shapes[a])))
        return remote

    return _start_copies("gather_d2d_start_" + tag, "d2d", fulls, N_CHIP * n, plan)


def _scatter_d2d_start(grads, kinds, shapes, tag):
    n = len(grads)
    lands = [lax.empty((N_CHIP,) + tuple(s), g.dtype) for g, s in zip(grads, shapes)]

    def plan(me, peers, refs):
        remote = []
        for a in range(n):
            for ch in range(N_CHIP):
                remote.append((_window(refs[a], kinds[a], 2 * ch + 1 - me.c, shapes[a]),
                               refs[n + a].at[ch], 0, refs[n + a].at[ch]))
        return remote

    return _start_copies("scatter_d2d_start_" + tag, "d2d", list(grads) + lands, N_CHIP * n, plan)


def _scatter_ici_start(sums, shapes, tag, after=None):
    n = len(sums)
    lands = [lax.empty((N_CHIP - 1,) + tuple(s), BF16) for s in shapes]

    def plan(me, peers, refs):
        remote = []
        for a in range(n):
            for k, p in enumerate(peers):
                remote.append((refs[a].at[p.chip], refs[n + a].at[k], k, refs[n + a].at[k]))
        return remote

    return _start_copies("scatter_ici_start_" + tag, "ici", list(sums) + lands, 3 * n, plan, after)


def _pair_add(g, r4, kind, shard_shape, core, *, name):
    r, c = shard_shape
    tr = _row_tile(r, c, 524288)
    nb = r // tr

    def body(core_ref, g_ref, r_ref, o_ref):
        o_ref[...] = (g_ref[...].astype(F32) + r_ref[...].astype(F32)).astype(BF16)

    if kind == "col":
        g_spec = pl.BlockSpec((tr, c), lambda ch, i, core_ref: (i, 2 * ch + core_ref[0]))
    else:
        g_spec = pl.BlockSpec((tr, c), lambda ch, i, core_ref: ((2 * ch + core_ref[0]) * nb + i, 0))
    slot = pl.BlockSpec((None, tr, c), lambda ch, i, core_ref: (ch, i, 0))
    return pl.pallas_call(
        body, name=name,
        grid_spec=pltpu.PrefetchScalarGridSpec(
            num_scalar_prefetch=1, grid=(N_CHIP, nb), in_specs=[g_spec, slot], out_specs=slot),
        out_shape=jax.ShapeDtypeStruct((N_CHIP, r, c), BF16),
        compiler_params=_params(("parallel", "parallel")),
    )(core, g, r4)


def _allreduce_small_start(part, chip):
    def plan_swap(me, peers, ins, outs):
        return [(ins[0], outs[0], 0, outs[0])], []

    (got,) = _exchange("allreduce_small_swap", "d2d", [part], [jax.ShapeDtypeStruct(part.shape, F32)],
                       {}, 1, 0, plan_swap)

    def body(chip_ref, a_ref, b_ref, o_ref):
        o_ref[...] = a_ref[...] + b_ref[...]

    whole = pl.BlockSpec(part.shape, lambda i, chip_ref: (0, 0))
    slots = pl.pallas_call(
        body, name="allreduce_small_pair",
        grid_spec=pltpu.PrefetchScalarGridSpec(
            num_scalar_prefetch=1, grid=(1,), in_specs=[whole, whole],
            out_specs=pl.BlockSpec((None,) + part.shape, lambda i, chip_ref: (chip_ref[0], 0, 0))),
        out_shape=jax.ShapeDtypeStruct((N_CHIP,) + part.shape, F32),
    )(chip, part, got)

    def plan(me, peers, refs):
        mine = refs[0].at[me.chip]
        return [(mine, mine, k, refs[0].at[p.chip]) for k, p in enumerate(peers)]

    return _start_copies("allreduce_small_start", "ici", [slots], 3, plan)


def _row_tile(r, c, limit=262144):
    best = None
    for t in range(8, r + 1, 8):
        if r % t == 0 and t * c <= limit:
            best = t
    return best if best is not None else r


def _adam_math(w, g, m, v):
    m = ADAM_B1 * m + (1.0 - ADAM_B1) * g
    v = ADAM_B2 * v + (1.0 - ADAM_B2) * (g * g)
    m_hat = m / (1.0 - ADAM_B1 ** ADAM_STEP)
    v_hat = v / (1.0 - ADAM_B2 ** ADAM_STEP)
    delta = -ADAM_LR * (m_hat / (jnp.sqrt(v_hat) + ADAM_EPS) + ADAM_WD * w)
    return delta, m, v


def _adamw(w, m, v, sums, lands, chip, layer, prev, *, name):
    L, r, c = w.shape
    tr = _row_tile(r, c)
    has_prev = prev is not None

    def body(chip_ref, *refs):
        w_ref, m_ref, v_ref, s_ref, l_ref = refs[:5]
        g_ref, d_ref, nm_ref, nv_ref = refs[-4:]
        g = (s_ref[...].astype(F32) + l_ref[0].astype(F32)) + l_ref[1].astype(F32) + l_ref[2].astype(F32)
        delta, nm, nv = _adam_math(w_ref[...], g, m_ref[...], v_ref[...])
        g_ref[...] = g
        d_ref[...] = delta
        nm_ref[...] = nm
        nv_ref[...] = nv

    slab = pl.BlockSpec((None, tr, c), lambda i, chip_ref: (layer, i, 0))
    in_specs = [slab, slab, slab, pl.BlockSpec((None, tr, c), lambda i, chip_ref: (chip_ref[0], i, 0)),
                pl.BlockSpec((N_CHIP - 1, tr, c), lambda i, chip_ref: (0, i, 0))]
    args = [chip, w, m, v, sums, lands]
    aliases = {}
    if has_prev:
        in_specs += [ANY] * 4
        args += list(prev)
        aliases = {6 + n: n for n in range(4)}
    return pl.pallas_call(
        body, name=name,
        grid_spec=pltpu.PrefetchScalarGridSpec(
            num_scalar_prefetch=1, grid=(r // tr,), in_specs=in_specs, out_specs=(slab,) * 4),
        out_shape=tuple(jax.ShapeDtypeStruct((L, r, c), F32) for _ in range(4)),
        input_output_aliases=aliases, compiler_params=_params(("parallel",)),
    )(*args)


def _adamw_small(w, g, m, v):
    def body(w_ref, g_ref, m_ref, v_ref, d_ref, nm_ref, nv_ref):
        d_ref[...], nm_ref[...], nv_ref[...] = _adam_math(w_ref[...], g_ref[...], m_ref[...], v_ref[...])

    return pl.pallas_call(
        body, name="adamw_small", out_shape=tuple(jax.ShapeDtypeStruct(w.shape, F32) for _ in range(3)),
    )(w, g, m, v)


def _sum4(parts):
    def body(p_ref, o_ref):
        o_ref[...] = (p_ref[0] + p_ref[1]) + p_ref[2] + p_ref[3]

    return pl.pallas_call(
        body, name="sum_chips", out_shape=jax.ShapeDtypeStruct(parts.shape[1:], F32))(parts)


_WEIGHTS = ["mix_norm", "mlp_norm", "sb_w_in", "sb_w_out", "fox_w_in", "fox_b_f", "fox_q_gain", "fox_k_gain",
            "fox_w_out", "mla_w_in", "mla_q_norm", "mla_kv_norm", "mla_w_uq", "mla_w_ukv", "mla_q_gain",
            "mla_k_gain", "mla_w_out", "mlp_w1", "mlp_w2"]
_BIG = {"sb_w_in": "col", "sb_w_out": "row", "fox_w_in": "row", "fox_w_out": "row", "mla_w_in": "row",
        "mla_w_uq": "col", "mla_w_ukv": "col", "mla_w_out": "row", "mlp_w1": "col", "mlp_w2": "row"}


def _layer_big(i):
    kind, j = i % 3, i // 3
    mixer = {0: ["sb_w_in", "sb_w_out"], 1: ["fox_w_in", "fox_w_out"],
             2: ["mla_w_in", "mla_w_uq", "mla_w_ukv", "mla_w_out"]}[kind]
    return [(n, j) for n in mixer] + [("mlp_w1", i), ("mlp_w2", i)]


def _stack_to_cols(a):
    r = a.shape[0] // N_DEV
    return a.reshape(N_DEV, r, a.shape[1]).transpose(1, 0, 2).reshape(r, N_DEV * a.shape[1])


def _cols_to_stack(a):
    c = a.shape[1] // N_DEV
    return a.reshape(a.shape[0], N_DEV, c).transpose(1, 0, 2).reshape(N_DEV * a.shape[0], c)


def _pack_rows(rows, width):
    rows = [jnp.pad(r.reshape(-1).astype(F32), (0, width - r.size)) for r in rows]
    pad = (-len(rows)) % 8
    rows += [jnp.zeros((width,), F32)] * pad
    return jnp.stack(rows)


def kernel(x, positions, mix_norm, mlp_norm, sb_w_in, sb_w_out, fox_w_in, fox_b_f, fox_q_gain, fox_k_gain, fox_w_out, mla_w_in, mla_q_norm, mla_kv_norm, mla_w_uq, mla_w_ukv, mla_q_gain, mla_k_gain, mla_w_out, mlp_w1, mlp_w2, loss_target, m_mix_norm, m_mlp_norm, m_sb_w_in, m_sb_w_out, m_fox_w_in, m_fox_b_f, m_fox_q_gain, m_fox_k_gain, m_fox_w_out, m_mla_w_in, m_mla_q_norm, m_mla_kv_norm, m_mla_w_uq, m_mla_w_ukv, m_mla_q_gain, m_mla_k_gain, m_mla_w_out, m_mlp_w1, m_mlp_w2, v_mix_norm, v_mlp_norm, v_sb_w_in, v_sb_w_out, v_fox_w_in, v_fox_b_f, v_fox_q_gain, v_fox_k_gain, v_fox_w_out, v_mla_w_in, v_mla_q_norm, v_mla_kv_norm, v_mla_w_uq, v_mla_w_ukv, v_mla_q_gain, v_mla_k_gain, v_mla_w_out, v_mlp_w1, v_mlp_w2):
    w_in = dict(zip(_WEIGHTS, (mix_norm, mlp_norm, sb_w_in, sb_w_out, fox_w_in, fox_b_f, fox_q_gain, fox_k_gain, fox_w_out, mla_w_in, mla_q_norm, mla_kv_norm, mla_w_uq, mla_w_ukv, mla_q_gain, mla_k_gain, mla_w_out, mlp_w1, mlp_w2)))
    m_in = dict(zip(_WEIGHTS, (m_mix_norm, m_mlp_norm, m_sb_w_in, m_sb_w_out, m_fox_w_in, m_fox_b_f, m_fox_q_gain, m_fox_k_gain, m_fox_w_out, m_mla_w_in, m_mla_q_norm, m_mla_kv_norm, m_mla_w_uq, m_mla_w_ukv, m_mla_q_gain, m_mla_k_gain, m_mla_w_out, m_mlp_w1, m_mlp_w2)))
    v_in = dict(zip(_WEIGHTS, (v_mix_norm, v_mlp_norm, v_sb_w_in, v_sb_w_out, v_fox_w_in, v_fox_b_f, v_fox_q_gain, v_fox_k_gain, v_fox_w_out, v_mla_w_in, v_mla_q_norm, v_mla_kv_norm, v_mla_w_uq, v_mla_w_ukv, v_mla_q_gain, v_mla_k_gain, v_mla_w_out, v_mlp_w1, v_mlp_w2)))
    depth, D = mix_norm.shape
    H = D // HEAD_DIM
    n_mla = mla_w_in.shape[0]
    dev = 4 * lax.axis_index("x") + 2 * lax.axis_index("y") + lax.axis_index("c")
    core = lax.axis_index("c").astype(jnp.int32).reshape(1)
    dev_arr = dev.astype(jnp.int32).reshape(1)

    chip = (2 * lax.axis_index("x") + lax.axis_index("y")).astype(jnp.int32).reshape(1)
    nq, nkv = mla_q_norm.shape[1], mla_kv_norm.shape[1]

    units, u_mix, u_mlp = [], {}, {}
    for i in range(depth):
        big = _layer_big(i)
        u_mix[i] = len(units)
        if i < SPLIT_LAYERS:
            units += [(i, ("mix",), big[:-2]), (i, ("mlp",), big[-2:])]
        else:
            units.append((i, ("mix", "mlp"), big))
        u_mlp[i] = len(units) - 1

    def split(i):
        return u_mix[i] != u_mlp[i]

    def tag_of(u):
        i, parts, _ = units[u]
        return f"l{i}" if len(parts) == 2 else f"l{i}_{parts[0]}"

    kinds_of, shapes_of, ici = [], [], []
    for i, parts, names in units:
        shards = [w_in[n][j] for n, j in names]
        kinds = [_BIG[n] for n, _ in names]
        behind = (ici[-1].token,) if ici else ()
        fulls = [_own_in_place(s, k, dev_arr, BF16, behind) for s, k in zip(shards, kinds)]
        if i % 3 == 2:
            shards.append(_pack_rows([mla_q_norm[i // 3], mla_kv_norm[i // 3]], LANE))
            kinds.append("row")
            fulls.append(_own_in_place(shards[-1], "row", dev_arr, F32, behind))
        kinds_of.append(kinds)
        shapes_of.append([s.shape for s in shards])
        ici.append(_gather_ici_start(fulls, kinds, shapes_of[-1], tag_of(len(ici)),
                                     ici[-1].token if ici else None))

    full = {n: [None] * w_in[n].shape[0] for n in _WEIGHTS}
    for n in ("mix_norm", "mlp_norm", "fox_b_f", "fox_q_gain", "fox_k_gain", "mla_q_gain", "mla_k_gain"):
        full[n] = w_in[n]
    P = {"mix_norm": mix_norm, "mlp_norm": mlp_norm, "mlp": [None] * depth,
         "sb": [None] * sb_w_in.shape[0], "fox": [None] * fox_w_in.shape[0], "mla": [None] * n_mla}

    def forward_to_sibling(u, after):
        arrived = _wait_copies(ici[u], after)
        return _gather_d2d_start(arrived, kinds_of[u], shapes_of[u], tag_of(u))

    def finish_gather(u, flight, after):
        i, parts, names = units[u]
        got = _wait_copies(flight, after)
        for (n, j), a in zip(names, got):
            full[n][j] = _stack_to_cols(a) if n == "fox_w_in" else a
        if i % 3 == 2:
            tiles = got[-1].reshape(N_DEV, 8, LANE)
            full["mla_q_norm"][i // 3] = tiles[:, 0, :nq].reshape(-1)
            full["mla_kv_norm"][i // 3] = tiles[:, 1, :nkv].reshape(-1)
        _prepare_layer(P, full, i, H, parts)

    tabs = _rope_tables(positions[0])
    d2d = forward_to_sibling(0, ici[-1].token)
    finish_gather(0, d2d, d2d.token)
    xs, saved = x[0], []
    for i in range(depth):
        nxt = {}

        def hook(x_mid, i=i, nxt=nxt):
            if split(i):
                mlp_d2d = forward_to_sibling(u_mlp[i], x_mid)
                finish_gather(u_mlp[i], mlp_d2d, mlp_d2d.token)
            if 1 <= i < depth - 1 and not split(i + 1):
                nxt["d2d"] = forward_to_sibling(u_mix[i + 1], x_mid)
                return (nxt["d2d"].token,)
            return ()

        xs, s = _layer_fwd(i, xs, P, tabs, H, hook)
        saved.append(s)
        if i + 1 < depth:
            if "d2d" in nxt:
                finish_gather(u_mix[i + 1], nxt["d2d"], xs)
            else:
                mix_d2d = forward_to_sibling(u_mix[i + 1], xs)
                finish_gather(u_mix[i + 1], mix_d2d, mix_d2d.token)
    dx, dxb, loss_part = _loss_head(xs, loss_target[0])

    def unit_meta(u):
        names = units[u][2]
        return [_BIG[n] for n, _ in names], [w_in[n].shape[1:] for n, _ in names]

    def to_sibling_start(u, Gl):
        grads = [(_cols_to_stack(Gl[n]) if n == "fox_w_in" else Gl[n]) for n, _ in units[u][2]]
        return _scatter_d2d_start(grads, *unit_meta(u), tag_of(u))

    def pair_and_send(u, flight, after, before=None):
        n = len(units[u][2])
        kinds, shapes = unit_meta(u)
        got = _wait_copies(flight, after)
        sums = [_pair_add(g, r4, k, s, core, name="pair_add_" + tag_of(u))
                for g, r4, k, s in zip(got[:n], got[n:], kinds, shapes)]
        return _scatter_ici_start(sums, shapes, tag_of(u), before)

    G, to_chips, deps, flying = [None] * depth, [None] * len(units), (), {}
    for i in reversed(range(depth)):
        def hook(dxb_mid, Gp, i=i):
            out = ()
            if i + 1 < depth:
                u = u_mix[i + 1]
                to_chips[u] = pair_and_send(u, flying.pop(u), dxb_mid)
                out = (to_chips[u].token,)
            if split(i):
                flying[u_mlp[i]] = to_sibling_start(u_mlp[i], {"mlp_w1": Gp["w1"], "mlp_w2": Gp["w2"]})
                out = out + (flying[u_mlp[i]].token,)
            return out

        def hook2(dv, i=i):
            if not split(i):
                return ()
            u = u_mlp[i]
            to_chips[u] = pair_and_send(u, flying.pop(u), dv)
            return (to_chips[u].token,)

        dx, dxb, Gp = _layer_bwd(i, dx, dxb, saved[i], P, tabs, H, deps, hook, hook2)
        G[i] = _layer_grads(i, Gp, H)
        flying[u_mix[i]] = to_sibling_start(u_mix[i], G[i])
        deps = (flying[u_mix[i]].token,)
    grad_x = dx
    to_chips[0] = pair_and_send(0, flying.pop(0), dx)

    small = [n for n in _WEIGHTS if n not in _BIG]
    rows = []
    for n in small:
        for j in range(w_in[n].shape[0]):
            rows.append(G[_layer_of(n, j)][n])
    loss_row = len(rows)
    rows.append(loss_part)
    small_flight = _allreduce_small_start(_pack_rows(rows, D) + to_chips[0].token[0, 0], chip)

    results, after = {}, [small_flight.token]
    for u in reversed(range(len(units))):
        names = units[u][2]
        got = _wait_copies(to_chips[u], after)
        for a, (n, j) in enumerate(names):
            results[n] = _adamw(w_in[n], m_in[n], v_in[n], got[a], got[len(names) + a], chip, j, results.get(n),
                                name="adamw_" + n)
        after = [results[n][0] for n, _ in names]

    (parts,) = _wait_copies(small_flight, after)
    sgr = _sum4(parts)
    g_rows, w_rows, m_rows, v_rows, at = [], [], [], [], 0
    for n in small:
        for j in range(w_in[n].shape[0]):
            width = w_in[n].shape[1]
            if n in ("mla_q_norm", "mla_kv_norm"):
                g_rows.append(lax.dynamic_slice(sgr[at], (dev * width,), (width,)))
            else:
                g_rows.append(sgr[at, :width])
            w_rows.append(w_in[n][j])
            m_rows.append(m_in[n][j])
            v_rows.append(v_in[n][j])
            at += 1
    g_pack = _pack_rows(g_rows, D)
    d_pack, nm_pack, nv_pack = _adamw_small(
        _pack_rows(w_rows, D), g_pack, _pack_rows(m_rows, D), _pack_rows(v_rows, D))
    at = 0
    for n in small:
        L, width = w_in[n].shape
        results[n] = tuple(p[at:at + L, :width] for p in (g_pack, d_pack, nm_pack, nv_pack))
        at += L

    out = [sgr[loss_row, 0], grad_x[None]]
    for part in range(4):
        out += [results[n][part] for n in _WEIGHTS]
    return tuple(out)
```

```python
import functools
import math

import jax
import jax.numpy as jnp
from jax import lax
from jax.experimental import pallas as pl
from jax.experimental.pallas import tpu as pltpu

F32 = jnp.float32
BF16 = jnp.bfloat16

HEAD_DIM = 128
MLA_NOPE = 128
MLA_ROPE = 64
MLA_V = 128
MLA_QK = MLA_NOPE + MLA_ROPE
MLA_QK_PAD = 256
LANE = 128
ROPE_THETA = 10000.0
EPS = 1e-6
ADAM_LR = 0.001
ADAM_B1 = 0.9
ADAM_B2 = 0.999
ADAM_EPS = 1e-08
ADAM_WD = 0.01
ADAM_STEP = 10
N_DEV = 8
N_CHIP = 4
NEG = -1e30
VMEM_LIMIT = 56 * 1024 * 1024
ATT_BLOCK = 256
SB_BLOCK = 256
ATT_HEADS = 4
ATT_HEADS_FWD = 8
SPLIT_LAYERS = 2
SB_PARTS = 2
MESH = pl.DeviceIdType.MESH
ANY = pl.BlockSpec(memory_space=pl.ANY)


def _params(sem):
    return pltpu.CompilerParams(dimension_semantics=sem, vmem_limit_bytes=VMEM_LIMIT)


def _tile(dim, pref):
    if dim <= pref:
        return dim
    t = pref
    while dim % t:
        t -= LANE
    return t


def _dot(a, b, dims):
    return lax.dot_general(a, b, (dims, ((), ())), preferred_element_type=F32)


NN = ((1,), (0,))
NT = ((1,), (1,))
TN = ((0,), (0,))


def _mm(a, b, *, mode="nn", out_dtype=F32, res=None, act=None, z=None, name,
        tm=1024, tn=1024, tk=2048, deps=()):
    if mode == "nn":
        (M, K), (_, N) = a.shape, b.shape
    elif mode == "nt":
        (M, K), (N, _) = a.shape, b.shape
    else:
        (K, M), (_, N) = a.shape, b.shape
    tm, tn, tk = _tile(M, tm), _tile(N, tn), _tile(K, tk)
    nk = K // tk
    if mode == "tn":
        a_spec = pl.BlockSpec((tk, tm), lambda i, j, k: (k, i))
    else:
        a_spec = pl.BlockSpec((tm, tk), lambda i, j, k: (i, k))
    if mode == "nt":
        b_spec = pl.BlockSpec((tn, tk), lambda i, j, k: (j, k))
    else:
        b_spec = pl.BlockSpec((tk, tn), lambda i, j, k: (k, j))
    dims = {"nn": NN, "nt": NT, "tn": TN}[mode]
    o_spec = pl.BlockSpec((tm, tn), lambda i, j, k: (i, j))
    in_specs, args = [a_spec, b_spec], [a, b]
    if res is not None:
        in_specs.append(o_spec)
        args.append(res)
    if act == "drelu2":
        in_specs.append(o_spec)
        args.append(z)
    if act == "relu2":
        out_shape = (jax.ShapeDtypeStruct((M, N), F32), jax.ShapeDtypeStruct((M, N), BF16))
        out_specs = (o_spec, o_spec)
    else:
        out_shape = jax.ShapeDtypeStruct((M, N), out_dtype)
        out_specs = o_spec
    has_res, has_z = res is not None, act == "drelu2"
    n_out = 2 if act == "relu2" else 1
    in_specs += [ANY] * len(deps)
    args += list(deps)

    def body(*refs):
        a_ref, b_ref = refs[0], refs[1]
        idx = 2
        res_ref = z_ref = None
        if has_res:
            res_ref = refs[idx]
            idx += 1
        if has_z:
            z_ref = refs[idx]
            idx += 1
        idx += len(deps)
        outs = refs[idx:idx + n_out]

        def finish(r):
            if has_res:
                r = r + res_ref[...]
            if act == "relu2":
                outs[0][...] = r
                rr = jnp.maximum(r, 0.0)
                outs[1][...] = (rr * rr).astype(BF16)
            elif act == "drelu2":
                outs[0][...] = (r * (2.0 * jnp.maximum(z_ref[...], 0.0))).astype(out_dtype)
            else:
                outs[0][...] = r.astype(out_dtype)

        if nk == 1:
            finish(_dot(a_ref[...], b_ref[...], dims))
            return
        acc = refs[-1]
        k = pl.program_id(2)

        @pl.when(k == 0)
        def _():
            acc[...] = _dot(a_ref[...], b_ref[...], dims)

        @pl.when((k > 0) & (k < nk - 1))
        def _():
            acc[...] += _dot(a_ref[...], b_ref[...], dims)

        @pl.when(k == nk - 1)
        def _():
            finish(acc[...] + _dot(a_ref[...], b_ref[...], dims))

    return pl.pallas_call(
        body, name=name, grid=(M // tm, N // tn, nk), in_specs=in_specs, out_specs=out_specs,
        out_shape=out_shape, scratch_shapes=[pltpu.VMEM((tm, tn), F32)] if nk > 1 else [],
        compiler_params=_params(("parallel", "parallel", "arbitrary")),
    )(*args)


def _rmsnorm_fwd(x, g, *, name, tm=256):
    T, n = x.shape
    tm = _tile(T, tm)

    def body(x_ref, g_ref, o_ref):
        xf = x_ref[...]
        r = lax.rsqrt(jnp.mean(xf * xf, axis=-1, keepdims=True) + EPS)
        o_ref[...] = (xf * r * g_ref[...]).astype(BF16)

    return pl.pallas_call(
        body, name=name, grid=(T // tm,),
        in_specs=[pl.BlockSpec((tm, n), lambda i: (i, 0)), pl.BlockSpec((1, n), lambda i: (0, 0))],
        out_specs=pl.BlockSpec((tm, n), lambda i: (i, 0)),
        out_shape=jax.ShapeDtypeStruct((T, n), BF16),
        compiler_params=_params(("parallel",)),
    )(x, g)


def _rmsnorm_bwd(x, g, dy, dx_in=None, *, name, want_f32=True, tm=256):
    T, n = x.shape
    tm = _tile(T, tm)
    has_in = dx_in is not None
    row = pl.BlockSpec((tm, n), lambda i: (i, 0))
    vec = pl.BlockSpec((1, n), lambda i: (0, 0))

    def body(*refs):
        x_ref, g_ref, dy_ref = refs[:3]
        idx = 3
        in_ref = None
        if has_in:
            in_ref = refs[idx]
            idx += 1
        outs = refs[idx:]
        xf = x_ref[...]
        r = lax.rsqrt(jnp.mean(xf * xf, axis=-1, keepdims=True) + EPS)
        dyf = dy_ref[...]
        t = dyf * g_ref[...]
        dx = r * t - xf * (r * r * r * jnp.mean(t * xf, axis=-1, keepdims=True))
        if has_in:
            dx = dx + in_ref[...]
        o = 0
        if want_f32:
            outs[0][...] = dx
            o = 1
        outs[o][...] = dx.astype(BF16)
        dg_ref = outs[o + 1]

        @pl.when(pl.program_id(0) == 0)
        def _():
            dg_ref[...] = jnp.zeros_like(dg_ref)

        dg_ref[...] += jnp.sum(dyf * xf * r, axis=0, keepdims=True)

    in_specs = [row, vec, row] + ([row] if has_in else [])
    out_specs, out_shape = [], []
    if want_f32:
        out_specs.append(row)
        out_shape.append(jax.ShapeDtypeStruct((T, n), F32))
    out_specs += [row, vec]
    out_shape += [jax.ShapeDtypeStruct((T, n), BF16), jax.ShapeDtypeStruct((1, n), F32)]
    args = [x, g, dy] + ([dx_in] if has_in else [])
    return pl.pallas_call(
        body, name=name, grid=(T // tm,), in_specs=in_specs, out_specs=tuple(out_specs),
        out_shape=tuple(out_shape), compiler_params=_params(("arbitrary",)),
    )(*args)


def _loss_head(y, target, *, tm=256):
    T, n = y.shape
    tm = _tile(T, tm)
    row = pl.BlockSpec((tm, n), lambda i: (i, 0))

    def body(y_ref, t_ref, dy_ref, dyb_ref, loss_ref):
        err = y_ref[...] - t_ref[...]
        dy = err * (1.0 / n)
        dy_ref[...] = dy
        dyb_ref[...] = dy.astype(BF16)

        @pl.when(pl.program_id(0) == 0)
        def _():
            loss_ref[...] = jnp.zeros_like(loss_ref)

        part = 0.5 * jnp.sum(jnp.mean(err * err, axis=-1, keepdims=True), axis=0, keepdims=True)
        loss_ref[...] += part

    return pl.pallas_call(
        body, name="loss_head", grid=(T // tm,), in_specs=[row, row],
        out_specs=(row, row, pl.BlockSpec((1, 1), lambda i: (0, 0))),
        out_shape=(jax.ShapeDtypeStruct((T, n), F32), jax.ShapeDtypeStruct((T, n), BF16),
                   jax.ShapeDtypeStruct((1, 1), F32)),
        compiler_params=_params(("arbitrary",)),
    )(y, target)


def _split(v, n):
    parts = []
    for _ in range(n - 1):
        p = v.astype(BF16)
        parts.append(p)
        v = v - p.astype(F32)
    parts.append(v.astype(BF16))
    return parts


def _tri_left(tri, v):
    hi, mid, lo = _split(v, 3)
    return _dot(tri, hi, NN) + _dot(tri, mid, NN) + _dot(tri, lo, NN)


def _iota2(shape, dim):
    return lax.broadcasted_iota(jnp.int32, shape, dim)


def _log_sigmoid(z):
    return jnp.minimum(z, 0.0) - jnp.log1p(jnp.exp(-jnp.abs(z)))


def _log_sigmoid_abs(z):
    return jnp.minimum(z, 0.0) - jnp.log(1.0 + jnp.exp(-jnp.abs(z)))


def _head_spec(rows, width, off):
    return pl.BlockSpec((rows, width), lambda h, i: (i, off + h))


def _full_head_spec(T, width, off):
    return pl.BlockSpec((T, width), lambda h, i: (0, off + h))


def _sb_fwd(qkv, H):
    T = qkv.shape[0]
    B = _tile(T, SB_BLOCK)
    nq = T // B
    scale = 1.0 / math.sqrt(HEAD_DIM)

    HG = ATT_HEADS_FWD if H % ATT_HEADS_FWD == 0 else 1
    HD = HEAD_DIM

    def body(q_ref, k_ref, v_ref, o_ref, c_ref, tri_s):
        i = pl.program_id(1)
        tri = (_iota2((B, B), 0) > _iota2((B, B), 1)).astype(BF16)
        for p in range(SB_PARTS):
            tri_s[pl.ds(p * B, B), :] = tri
        rows = _iota2((B, B), 0)
        cols = _iota2((B, B), 1)

        def block(j, carry, diagonal):
            ks = pl.multiple_of(j * B, B)
            out = []
            for g in range(HG):
                acc, run = carry[g]
                q = q_ref[:, pl.ds(g * HD, HD)]
                kb = k_ref[pl.ds(ks, B), pl.ds(g * HD, HD)]
                vb = v_ref[pl.ds(ks, B), pl.ds(g * HD, HD)]
                z = _dot(q, kb, NT) * scale
                ls = _log_sigmoid_abs(z)
                lk = ls - z
                if diagonal:
                    lk = jnp.where(rows > cols, lk, 0.0)
                    ls = jnp.where(rows > cols, ls, NEG)
                parts = jnp.concatenate(_split(lk, SB_PARTS), axis=1)
                later = _dot(parts, tri_s[...], NN)
                a = jnp.exp(ls + later).astype(BF16)
                acc = acc + jnp.exp(run) * _dot(a, vb, NN)
                out.append((acc, run + jnp.sum(lk, axis=1, keepdims=True)))
            return tuple(out)

        init = tuple((jnp.zeros((B, HD), F32), jnp.zeros((B, 1), F32)) for _ in range(HG))
        res = block(i, init, True)
        res = lax.fori_loop(1, i + 1, lambda n, c: block(i - n, c, False), res)
        for g in range(HG):
            o_ref[:, pl.ds(g * HD, HD)] = res[g][0].astype(BF16)
            c_ref[g] = res[g][1]

    return pl.pallas_call(
        body, name="sb_attn_fwd", grid=(H // HG, nq),
        in_specs=[_head_spec(B, HG * HD, 0), _full_head_spec(T, HG * HD, H // HG),
                  _full_head_spec(T, HG * HD, 2 * H // HG)],
        out_specs=(_head_spec(B, HG * HD, 0), pl.BlockSpec((HG, B, 1), lambda h, i: (h, i, 0))),
        out_shape=(jax.ShapeDtypeStruct((T, H * HD), BF16), jax.ShapeDtypeStruct((H, T, 1), F32)),
        scratch_shapes=[pltpu.VMEM((SB_PARTS * B, B), BF16)],
        compiler_params=_params(("parallel", "arbitrary")),
    )(qkv, qkv, qkv)


def _sb_bwd(qkv, do, ctot, H):
    T = qkv.shape[0]
    B = _tile(T, SB_BLOCK)
    nq = T // B
    scale = 1.0 / math.sqrt(HEAD_DIM)
    HG = ATT_HEADS if H % ATT_HEADS == 0 else 1
    HD = HEAD_DIM

    def body(q_ref, k_ref, v_ref, do_ref, c_ref, dq_ref, dk_ref, dv_ref, dk_acc, dv_acc,
             tri_incl_s, tri_strict_s):
        i = pl.program_id(1)

        @pl.when(i == 0)
        def _():
            dk_acc[...] = jnp.zeros_like(dk_acc)
            dv_acc[...] = jnp.zeros_like(dv_acc)

        rows = _iota2((B, B), 0)
        cols = _iota2((B, B), 1)
        tri_incl = (rows <= cols).astype(BF16)
        tri_strict = (rows < cols).astype(BF16)
        for p in range(SB_PARTS):
            tri_incl_s[pl.ds(p * B, B), :] = tri_incl
            tri_strict_s[pl.ds(p * B, B), :] = tri_strict
        heads = [(q_ref[:, pl.ds(g * HD, HD)], do_ref[:, pl.ds(g * HD, HD)], c_ref[g]) for g in range(HG)]

        def block(j, carry, diagonal):
            ks = pl.multiple_of(j * B, B)
            out = []
            for g in range(HG):
                dq, lpre, gpre = carry[g]
                q, do_b, ctot_b = heads[g]
                kb = k_ref[pl.ds(ks, B), pl.ds(g * HD, HD)]
                vb = v_ref[pl.ds(ks, B), pl.ds(g * HD, HD)]
                z = _dot(q, kb, NT) * scale
                da = _dot(do_b, vb, NT)
                lsm = _log_sigmoid_abs(z)
                lk = lsm - z
                if diagonal:
                    lk = jnp.where(rows > cols, lk, 0.0)
                    lsm = jnp.where(rows > cols, lsm, NEG)
                incl = _dot(jnp.concatenate(_split(lk, SB_PARTS), axis=1), tri_incl_s[...], NN)
                a = jnp.exp(lsm + (ctot_b - lpre) - incl)
                gw = a * da
                gex = gpre + _dot(jnp.concatenate(_split(gw, SB_PARTS), axis=1), tri_strict_s[...], NN)
                dzb = ((gw * jnp.exp(lsm - z) - jnp.exp(lsm) * gex) * scale).astype(BF16)
                dq = dq + _dot(dzb, kb, NN)
                dk_acc[pl.ds(ks, B), pl.ds(g * HD, HD)] += _dot(dzb, q, TN)
                dv_acc[pl.ds(ks, B), pl.ds(g * HD, HD)] += _dot(a.astype(BF16), do_b, TN)
                out.append((dq, lpre + jnp.sum(lk, axis=1, keepdims=True),
                            gpre + jnp.sum(gw, axis=1, keepdims=True)))
            return tuple(out)

        init = tuple((jnp.zeros((B, HD), F32), jnp.zeros((B, 1), F32), jnp.zeros((B, 1), F32))
                     for _ in range(HG))
        res = lax.fori_loop(0, i, lambda j, c: block(j, c, False), init)
        res = block(i, res, True)
        for g in range(HG):
            dq_ref[:, pl.ds(g * HD, HD)] = res[g][0].astype(BF16)

        @pl.when(i == nq - 1)
        def _():
            dk_ref[...] = dk_acc[...].astype(BF16)
            dv_ref[...] = dv_acc[...].astype(BF16)

    W = H * HD
    return pl.pallas_call(
        body, name="sb_attn_bwd", grid=(H // HG, nq),
        in_specs=[_head_spec(B, HG * HD, 0), _full_head_spec(T, HG * HD, H // HG),
                  _full_head_spec(T, HG * HD, 2 * H // HG), _head_spec(B, HG * HD, 0),
                  pl.BlockSpec((HG, B, 1), lambda h, i: (h, i, 0))],
        out_specs=(_head_spec(B, HG * HD, 0), _full_head_spec(T, HG * HD, 0),
                   _full_head_spec(T, HG * HD, 0)),
        out_shape=tuple(jax.ShapeDtypeStruct((T, W), BF16) for _ in range(3)),
        scratch_shapes=[pltpu.VMEM((T, HG * HD), F32), pltpu.VMEM((T, HG * HD), F32),
                        pltpu.VMEM((SB_PARTS * B, B), BF16), pltpu.VMEM((SB_PARTS * B, B), BF16)],
        compiler_params=_params(("parallel", "arbitrary")),
    )(qkv, qkv, qkv, do, ctot)


def _softmax_fwd(q, k, v, cf, H, dqk, scale, *, name):
    T = q.shape[0]
    B = _tile(T, ATT_BLOCK)
    nq = T // B
    has_cf = cf is not None
    HG = ATT_HEADS_FWD if H % ATT_HEADS_FWD == 0 else 1
    HD = HEAD_DIM

    def body(*refs):
        q_ref, k_ref, v_ref = refs[:3]
        idx = 3
        if has_cf:
            cfc_ref, cfr_ref = refs[3], refs[4]
            idx = 5
        o_ref, of_ref, lse_ref = refs[idx:idx + 3]
        i = pl.program_id(1)
        rows = _iota2((B, B), 0)
        cols = _iota2((B, B), 1)

        def step(j, carry, diagonal):
            ks = pl.multiple_of(j * B, B)
            out = []
            for g in range(HG):
                m, l, acc = carry[g]
                qb = q_ref[:, pl.ds(g * dqk, dqk)]
                kb = k_ref[pl.ds(ks, B), pl.ds(g * dqk, dqk)]
                vb = v_ref[pl.ds(ks, B), pl.ds(g * HD, HD)]
                s = _dot(qb, kb, NT) * scale
                if has_cf:
                    s = s + (cfc_ref[g] - cfr_ref[g, :, pl.ds(ks, B)])
                if diagonal:
                    s = jnp.where(cols <= rows, s, NEG)
                m_new = jnp.maximum(m, jnp.max(s, axis=1, keepdims=True))
                alpha = jnp.exp(m - m_new)
                p = jnp.exp(s - m_new)
                l = alpha * l + jnp.sum(p, axis=1, keepdims=True)
                acc = alpha * acc + _dot(p.astype(BF16), vb, NN)
                out.append((m_new, l, acc))
            return tuple(out)

        init = tuple((jnp.full((B, 1), NEG, F32), jnp.zeros((B, 1), F32), jnp.zeros((B, HD), F32))
                     for _ in range(HG))
        res = lax.fori_loop(0, i, lambda j, c: step(j, c, False), init)
        res = step(i, res, True)
        for g in range(HG):
            m, l, acc = res[g]
            o = acc / l
            o_ref[:, pl.ds(g * HD, HD)] = o.astype(BF16)
            of_ref[:, pl.ds(g * HD, HD)] = o
            lse_ref[g] = m + jnp.log(l)

    stat = pl.BlockSpec((HG, B, 1), lambda h, i: (h, i, 0))
    in_specs = [_head_spec(B, HG * dqk, 0), _full_head_spec(T, HG * dqk, 0), _full_head_spec(T, HG * HD, 0)]
    args = [q, k, v]
    if has_cf:
        in_specs += [stat, pl.BlockSpec((HG, 1, T), lambda h, i: (h, 0, 0))]
        args += list(cf)
    W = H * HD
    return pl.pallas_call(
        body, name=name, grid=(H // HG, nq), in_specs=in_specs,
        out_specs=(_head_spec(B, HG * HD, 0), _head_spec(B, HG * HD, 0), stat),
        out_shape=(jax.ShapeDtypeStruct((T, W), BF16), jax.ShapeDtypeStruct((T, W), F32),
                   jax.ShapeDtypeStruct((H, T, 1), F32)),
        compiler_params=_params(("parallel", "arbitrary")),
    )(*args)


def _softmax_bwd(q, k, v, cf, do, o, lse, H, dqk, scale, *, name):
    T = q.shape[0]
    B = _tile(T, ATT_BLOCK)
    nq = T // B
    has_cf = cf is not None
    HG = ATT_HEADS if H % ATT_HEADS == 0 else 1
    HD = HEAD_DIM

    def body(*refs):
        q_ref, k_ref, v_ref, do_ref, o_ref, lse_ref = refs[:6]
        idx = 6
        if has_cf:
            cfc_ref, cfr_ref = refs[6], refs[7]
            idx = 8
        dq_ref, dk_ref, dv_ref = refs[idx:idx + 3]
        idx += 3
        if has_cf:
            dcc_ref, dcr_ref = refs[idx], refs[idx + 1]
        i = pl.program_id(1)

        @pl.when(i == 0)
        def _():
            dk_ref[...] = jnp.zeros_like(dk_ref)
            dv_ref[...] = jnp.zeros_like(dv_ref)
            if has_cf:
                dcr_ref[...] = jnp.zeros_like(dcr_ref)

        rows = _iota2((B, B), 0)
        cols = _iota2((B, B), 1)
        heads = []
        for g in range(HG):
            do_b = do_ref[:, pl.ds(g * HD, HD)]
            delta = jnp.sum(do_b.astype(F32) * o_ref[:, pl.ds(g * HD, HD)], axis=1, keepdims=True)
            heads.append((q_ref[:, pl.ds(g * dqk, dqk)], do_b, lse_ref[g], delta))

        def step(j, carry, diagonal):
            ks = pl.multiple_of(j * B, B)
            out = []
            for g in range(HG):
                dq, rs = carry[g]
                qb, do_b, lse_b, delta = heads[g]
                kb = k_ref[pl.ds(ks, B), pl.ds(g * dqk, dqk)]
                vb = v_ref[pl.ds(ks, B), pl.ds(g * HD, HD)]
                s = _dot(qb, kb, NT) * scale
                if has_cf:
                    s = s + (cfc_ref[g] - cfr_ref[g, :, pl.ds(ks, B)])
                p = jnp.exp(s - lse_b)
                if diagonal:
                    p = jnp.where(cols <= rows, p, 0.0)
                dp = _dot(do_b, vb, NT)
                ds = p * (dp - delta)
                dsb = (ds * scale).astype(BF16)
                dq = dq + _dot(dsb, kb, NN)
                dk_ref[pl.ds(ks, B), pl.ds(g * dqk, dqk)] += _dot(dsb, qb, TN)
                dv_ref[pl.ds(ks, B), pl.ds(g * HD, HD)] += _dot(p.astype(BF16), do_b, TN)
                if has_cf:
                    rs = rs + jnp.sum(ds, axis=1, keepdims=True)
                    dcr_ref[g, :, pl.ds(ks, B)] -= jnp.sum(ds, axis=0, keepdims=True)
                out.append((dq, rs))
            return tuple(out)

        init = tuple((jnp.zeros((B, dqk), F32), jnp.zeros((B, 1), F32)) for _ in range(HG))
        res = lax.fori_loop(0, i, lambda j, c: step(j, c, False), init)
        res = step(i, res, True)
        for g in range(HG):
            dq_ref[:, pl.ds(g * dqk, dqk)] = res[g][0]
            if has_cf:
                dcc_ref[g] = res[g][1]

    stat = pl.BlockSpec((HG, B, 1), lambda h, i: (h, i, 0))
    rowstat = pl.BlockSpec((HG, 1, T), lambda h, i: (h, 0, 0))
    in_specs = [_head_spec(B, HG * dqk, 0), _full_head_spec(T, HG * dqk, 0), _full_head_spec(T, HG * HD, 0),
                _head_spec(B, HG * HD, 0), _head_spec(B, HG * HD, 0), stat]
    args = [q, k, v, do, o, lse]
    out_specs = [_head_spec(B, HG * dqk, 0), _full_head_spec(T, HG * dqk, 0), _full_head_spec(T, HG * HD, 0)]
    out_shape = [jax.ShapeDtypeStruct((T, H * dqk), F32), jax.ShapeDtypeStruct((T, H * dqk), F32),
                 jax.ShapeDtypeStruct((T, H * HD), F32)]
    if has_cf:
        in_specs += [stat, rowstat]
        args += list(cf)
        out_specs += [stat, rowstat]
        out_shape += [jax.ShapeDtypeStruct((H, T, 1), F32), jax.ShapeDtypeStruct((H, 1, T), F32)]
    return pl.pallas_call(
        body, name=name, grid=(H // HG, nq), in_specs=in_specs, out_specs=tuple(out_specs),
        out_shape=tuple(out_shape), compiler_params=_params(("parallel", "arbitrary")),
    )(*args)


def _headnorm(x, g):
    r = lax.rsqrt(jnp.mean(x * x, axis=-1, keepdims=True) + EPS)
    return x * r * g, r


def _headnorm_bwd(x, g, dy, n):
    r = lax.rsqrt(jnp.sum(x * x, axis=-1, keepdims=True) * (1.0 / n) + EPS)
    t = dy * g
    dx = r * t - x * (r * r * r * (jnp.sum(t * x, axis=-1, keepdims=True) * (1.0 / n)))
    dg = jnp.sum(dy * x * r, axis=0, keepdims=True)
    return dx, dg


def _fox_prep_fwd(qkv, gq, gk, H, *, tm=512):
    T = qkv.shape[0]
    tm = _tile(T, tm)

    def body(q_ref, k_ref, v_ref, gq_ref, gk_ref, qn_ref, kn_ref, vb_ref):
        qn_ref[...] = _headnorm(q_ref[...], gq_ref[...])[0].astype(BF16)
        kn_ref[...] = _headnorm(k_ref[...], gk_ref[...])[0].astype(BF16)
        vb_ref[...] = v_ref[...].astype(BF16)

    def blk(off):
        return pl.BlockSpec((tm, HEAD_DIM), lambda i, h: (i, off + h))

    vec = pl.BlockSpec((1, HEAD_DIM), lambda i, h: (0, 0))
    W = H * HEAD_DIM
    return pl.pallas_call(
        body, name="fox_prep_fwd", grid=(T // tm, H),
        in_specs=[blk(0), blk(H), blk(2 * H), vec, vec], out_specs=(blk(0), blk(0), blk(0)),
        out_shape=tuple(jax.ShapeDtypeStruct((T, W), BF16) for _ in range(3)),
        compiler_params=_params(("parallel", "parallel")),
    )(qkv, qkv, qkv, gq, gk)


def _fox_prep_bwd(qkv, gq, gk, dqn, dkn, dv, H, *, tm=512):
    T = qkv.shape[0]
    tm = _tile(T, tm)

    def body(q_ref, k_ref, gq_ref, gk_ref, dqn_ref, dkn_ref, dv_ref,
             dq_ref, dk_ref, dvb_ref, dgq_ref, dgk_ref):
        @pl.when((pl.program_id(0) == 0) & (pl.program_id(1) == 0))
        def _():
            dgq_ref[...] = jnp.zeros_like(dgq_ref)
            dgk_ref[...] = jnp.zeros_like(dgk_ref)

        dq, dgq = _headnorm_bwd(q_ref[...], gq_ref[...], dqn_ref[...], HEAD_DIM)
        dk, dgk = _headnorm_bwd(k_ref[...], gk_ref[...], dkn_ref[...], HEAD_DIM)
        dq_ref[...] = dq.astype(BF16)
        dk_ref[...] = dk.astype(BF16)
        dvb_ref[...] = dv_ref[...].astype(BF16)
        dgq_ref[...] += dgq
        dgk_ref[...] += dgk

    def blk(off):
        return pl.BlockSpec((tm, HEAD_DIM), lambda i, h: (i, off + h))

    vec = pl.BlockSpec((1, HEAD_DIM), lambda i, h: (0, 0))
    W = H * HEAD_DIM
    return pl.pallas_call(
        body, name="fox_prep_bwd", grid=(T // tm, H),
        in_specs=[blk(0), blk(H), vec, vec, blk(0), blk(0), blk(0)],
        out_specs=(blk(0), blk(0), blk(0), vec, vec),
        out_shape=tuple(jax.ShapeDtypeStruct((T, W), BF16) for _ in range(3))
        + (jax.ShapeDtypeStruct((1, HEAD_DIM), F32), jax.ShapeDtypeStruct((1, HEAD_DIM), F32)),
        compiler_params=_params(("arbitrary", "arbitrary")),
    )(qkv, qkv, gq, gk, dqn, dkn, dv)


def _fox_gate_fwd(flog, bf, *, tm=256):
    T = flog.shape[0]
    tm = _tile(T, tm)

    def body(f_ref, b_ref, cf_ref, carry):
        @pl.when(pl.program_id(0) == 0)
        def _():
            carry[...] = jnp.zeros_like(carry)

        lf = _log_sigmoid(f_ref[...] + b_ref[...])
        tri = (_iota2((tm, tm), 1) <= _iota2((tm, tm), 0)).astype(BF16)
        cf_ref[...] = carry[...] + _tri_left(tri, lf)
        carry[...] += jnp.sum(lf, axis=0, keepdims=True)

    return pl.pallas_call(
        body, name="fox_gate_fwd", grid=(T // tm,),
        in_specs=[pl.BlockSpec((tm, LANE), lambda i: (i, 0)), pl.BlockSpec((1, LANE), lambda i: (0, 0))],
        out_specs=pl.BlockSpec((tm, LANE), lambda i: (i, 0)),
        out_shape=jax.ShapeDtypeStruct((T, LANE), F32),
        scratch_shapes=[pltpu.VMEM((1, LANE), F32)],
        compiler_params=_params(("arbitrary",)),
    )(flog, bf)


def _fox_gate_bwd(flog, bf, dcf, *, tm=256):
    T = flog.shape[0]
    tm = _tile(T, tm)
    nt = T // tm

    def body(f_ref, b_ref, dcf_ref, df_ref, db_ref, carry):
        @pl.when(pl.program_id(0) == 0)
        def _():
            carry[...] = jnp.zeros_like(carry)
            db_ref[...] = jnp.zeros_like(db_ref)

        d = dcf_ref[...]
        tri = (_iota2((tm, tm), 1) >= _iota2((tm, tm), 0)).astype(BF16)
        dlf = carry[...] + _tri_left(tri, d)
        carry[...] += jnp.sum(d, axis=0, keepdims=True)
        xg = f_ref[...] + b_ref[...]
        e = jnp.exp(-jnp.abs(xg))
        sig_neg = jnp.where(xg >= 0.0, e, 1.0) / (1.0 + e)
        df = dlf * sig_neg
        df_ref[...] = df.astype(BF16)
        db_ref[...] += jnp.sum(df, axis=0, keepdims=True)

    rev = pl.BlockSpec((tm, LANE), lambda i: (nt - 1 - i, 0))
    vec = pl.BlockSpec((1, LANE), lambda i: (0, 0))
    return pl.pallas_call(
        body, name="fox_gate_bwd", grid=(nt,), in_specs=[rev, vec, rev], out_specs=(rev, vec),
        out_shape=(jax.ShapeDtypeStruct((T, LANE), BF16), jax.ShapeDtypeStruct((1, LANE), F32)),
        scratch_shapes=[pltpu.VMEM((1, LANE), F32)],
        compiler_params=_params(("arbitrary",)),
    )(flog, bf, dcf)


def _rope_tables(positions):
    half = MLA_ROPE // 2
    inv_freq = ROPE_THETA ** (-jnp.arange(0, half, dtype=F32) * 2.0 / MLA_ROPE)
    ang = positions.astype(F32)[:, None] * inv_freq
    cos, sin = jnp.cos(ang), jnp.sin(ang)
    zero = jnp.zeros_like(cos)
    pad = jnp.zeros((positions.shape[0], LANE - MLA_ROPE), F32)
    cos_t = jnp.concatenate([cos, cos, pad], axis=1)
    sin_up = jnp.concatenate([zero, sin, pad], axis=1)
    sin_dn = jnp.concatenate([-sin, zero, pad], axis=1)
    return cos_t, sin_up, sin_dn


def _rope(x, cos_t, sin_up, sin_dn):
    half = MLA_ROPE // 2
    return x * cos_t + pltpu.roll(x, half, 1) * sin_up + pltpu.roll(x, LANE - half, 1) * sin_dn


def _rope_t(d, cos_t, sin_up, sin_dn):
    half = MLA_ROPE // 2
    return d * cos_t + pltpu.roll(d * sin_up, LANE - half, 1) + pltpu.roll(d * sin_dn, half, 1)


def _norm192(xcat, g):
    r = lax.rsqrt(jnp.sum(xcat * xcat, axis=-1, keepdims=True) * (1.0 / MLA_QK) + EPS)
    return xcat * r * g


def _mla_prep_fwd(qfull, kv, kr, tabs, gq, gk, H, *, tm=512):
    T = qfull.shape[0]
    tm = _tile(T, tm)

    def body(q_ref, kv_ref, kr_ref, c_ref, su_ref, sd_ref, gq_ref, gk_ref, qf_ref, kf_ref, v_ref):
        tabs_b = (c_ref[...], su_ref[...], sd_ref[...])
        qb = q_ref[...]
        qcat = jnp.concatenate([qb[:, :MLA_NOPE], _rope(qb[:, MLA_NOPE:], *tabs_b)], axis=1)
        qf_ref[...] = _norm192(qcat, gq_ref[...]).astype(BF16)
        kvb = kv_ref[...]
        kcat = jnp.concatenate([kvb[:, :MLA_NOPE], _rope(kr_ref[...], *tabs_b)], axis=1)
        kf_ref[...] = _norm192(kcat, gk_ref[...]).astype(BF16)
        v_ref[...] = kvb[:, MLA_NOPE:].astype(BF16)

    head = pl.BlockSpec((tm, MLA_QK_PAD), lambda i, h: (i, h))
    tok = pl.BlockSpec((tm, LANE), lambda i, h: (i, 0))
    vec = pl.BlockSpec((1, MLA_QK_PAD), lambda i, h: (0, 0))
    return pl.pallas_call(
        body, name="mla_prep_fwd", grid=(T // tm, H),
        in_specs=[head, head, tok, tok, tok, tok, vec, vec],
        out_specs=(head, head, pl.BlockSpec((tm, MLA_V), lambda i, h: (i, h))),
        out_shape=(jax.ShapeDtypeStruct((T, H * MLA_QK_PAD), BF16),
                   jax.ShapeDtypeStruct((T, H * MLA_QK_PAD), BF16),
                   jax.ShapeDtypeStruct((T, H * MLA_V), BF16)),
        compiler_params=_params(("parallel", "parallel")),
    )(qfull, kv, kr, *tabs, gq, gk)


def _mla_prep_bwd(qfull, kv, kr, tabs, gq, gk, dqf, dkf, dv, H, *, tm=512):
    T = qfull.shape[0]
    tm = _tile(T, tm)

    def body(q_ref, kv_ref, kr_ref, c_ref, su_ref, sd_ref, gq_ref, gk_ref, dqf_ref, dkf_ref, dv_ref,
             dq_ref, dkv_ref, dkr_ref, dgq_ref, dgk_ref, kr_acc):
        h = pl.program_id(1)

        @pl.when((pl.program_id(0) == 0) & (h == 0))
        def _():
            dgq_ref[...] = jnp.zeros_like(dgq_ref)
            dgk_ref[...] = jnp.zeros_like(dgk_ref)

        @pl.when(h == 0)
        def _():
            kr_acc[...] = jnp.zeros_like(kr_acc)

        tabs_b = (c_ref[...], su_ref[...], sd_ref[...])
        qb = q_ref[...]
        qcat = jnp.concatenate([qb[:, :MLA_NOPE], _rope(qb[:, MLA_NOPE:], *tabs_b)], axis=1)
        dqcat, dgq = _headnorm_bwd(qcat, gq_ref[...], dqf_ref[...], MLA_QK)
        dq_ref[...] = jnp.concatenate(
            [dqcat[:, :MLA_NOPE], _rope_t(dqcat[:, MLA_NOPE:], *tabs_b)], axis=1).astype(BF16)
        dgq_ref[...] += dgq
        kvb = kv_ref[...]
        kcat = jnp.concatenate([kvb[:, :MLA_NOPE], _rope(kr_ref[...], *tabs_b)], axis=1)
        dkcat, dgk = _headnorm_bwd(kcat, gk_ref[...], dkf_ref[...], MLA_QK)
        dkv_ref[...] = jnp.concatenate([dkcat[:, :MLA_NOPE], dv_ref[...]], axis=1).astype(BF16)
        dgk_ref[...] += dgk
        kr_acc[...] += dkcat[:, MLA_NOPE:]

        @pl.when(h == H - 1)
        def _():
            dkr_ref[...] = _rope_t(kr_acc[...], *tabs_b).astype(BF16)

    head = pl.BlockSpec((tm, MLA_QK_PAD), lambda i, h: (i, h))
    tok = pl.BlockSpec((tm, LANE), lambda i, h: (i, 0))
    vec = pl.BlockSpec((1, MLA_QK_PAD), lambda i, h: (0, 0))
    return pl.pallas_call(
        body, name="mla_prep_bwd", grid=(T // tm, H),
        in_specs=[head, head, tok, tok, tok, tok, vec, vec, head, head,
                  pl.BlockSpec((tm, MLA_V), lambda i, h: (i, h))],
        out_specs=(head, head, tok, vec, vec),
        out_shape=(jax.ShapeDtypeStruct((T, H * MLA_QK_PAD), BF16),
                   jax.ShapeDtypeStruct((T, H * MLA_QK_PAD), BF16),
                   jax.ShapeDtypeStruct((T, LANE), BF16),
                   jax.ShapeDtypeStruct((1, MLA_QK_PAD), F32), jax.ShapeDtypeStruct((1, MLA_QK_PAD), F32)),
        scratch_shapes=[pltpu.VMEM((tm, LANE), F32)],
        compiler_params=_params(("arbitrary", "arbitrary")),
    )(qfull, kv, kr, *tabs, gq, gk, dqf, dkf, dv)


def _cf_layouts(cf, H):
    cfh = cf[:, :H].T
    return cfh[:, :, None], cfh[:, None, :]


def _layer_fwd(i, x, P, tabs, H, hook=None):
    kind, j = i % 3, i // 3
    s = {"x_in": x}
    h = _rmsnorm_fwd(x, P["mix_norm"][i:i + 1], name="mix_norm_fwd")
    s["h"] = h
    if kind == 0:
        w = P["sb"][j]
        qkv = _mm(h, w["w_in"], out_dtype=BF16, name="sb_qkv")
        o, ctot = _sb_fwd(qkv, H)
        s.update(qkv=qkv, o=o, ctot=ctot)
    elif kind == 1:
        w = P["fox"][j]
        qkv = _mm(h, w["w_qkv"], name="fox_qkv")
        flog = _mm(h, w["w_f"], name="fox_gate_proj")
        qn, kn, vb = _fox_prep_fwd(qkv, w["gq"], w["gk"], H)
        cf = _cf_layouts(_fox_gate_fwd(flog, w["bf"]), H)
        o, of, lse = _softmax_fwd(qn, kn, vb, cf, H, HEAD_DIM, 1.0 / math.sqrt(HEAD_DIM), name="fox_attn_fwd")
        s.update(qkv=qkv, flog=flog, qn=qn, kn=kn, vb=vb, cf=cf, o=o, of=of, lse=lse)
    else:
        w = P["mla"][j]
        dq = _mm(h, w["w_dq"], name="mla_down_q")
        dkv = _mm(h, w["w_dkv"], name="mla_down_kv")
        kr = _mm(h, w["w_dr"], name="mla_down_rope")
        cq = _rmsnorm_fwd(dq, w["q_norm"], name="mla_q_norm_fwd")
        ckv = _rmsnorm_fwd(dkv, w["kv_norm"], name="mla_kv_norm_fwd")
        qfull = _mm(cq, w["w_uq"], name="mla_up_q")
        kv = _mm(ckv, w["w_ukv"], name="mla_up_kv")
        qf, kf, v = _mla_prep_fwd(qfull, kv, kr, tabs, w["gq"], w["gk"], H)
        o, of, lse = _softmax_fwd(qf, kf, v, None, H, MLA_QK_PAD, 1.0 / math.sqrt(MLA_QK), name="mla_attn_fwd")
        s.update(dq=dq, dkv=dkv, kr=kr, cq=cq, ckv=ckv, qfull=qfull, kv=kv, qf=qf, kf=kf, v=v,
                 o=o, of=of, lse=lse)
    x = _mm(s["o"], w["w_out"], res=x, name="mixer_out")
    s["x_mid"] = x
    deps = hook(x) if hook is not None else ()
    h2 = _rmsnorm_fwd(x, P["mlp_norm"][i:i + 1], name="mlp_norm_fwd")
    z, u = _mm(h2, P["mlp"][i]["w1"], act="relu2", name="mlp_up", deps=deps)
    x = _mm(u, P["mlp"][i]["w2"], res=x, name="mlp_down")
    s.update(h2=h2, z=z, u=u)
    return x, s


def _layer_bwd(i, dx, dxb, s, P, tabs, H, deps=(), hook=None, hook2=None):
    kind, j = i % 3, i // 3
    G = {}
    wm = P["mlp"][i]
    dz = _mm(dxb, wm["w2"], mode="nt", act="drelu2", z=s["z"], out_dtype=BF16, name="mlp_down_bwd",
             deps=deps)
    G["w2"] = _mm(s["u"], dxb, mode="tn", out_dtype=BF16, name="mlp_w2_grad")
    G["w1"] = _mm(s["h2"], dz, mode="tn", out_dtype=BF16, name="mlp_w1_grad")
    dh2 = _mm(dz, wm["w1"], mode="nt", name="mlp_up_bwd")
    dx, dxb, G["mlp_norm"] = _rmsnorm_bwd(s["x_mid"], P["mlp_norm"][i:i + 1], dh2, dx, name="mlp_norm_bwd")
    h = s["h"]
    w = P[("sb", "fox", "mla")[kind]][j]
    do = _mm(dxb, w["w_out"], mode="nt", out_dtype=BF16, name="mixer_out_bwd",
             deps=hook(dxb, G) if hook is not None else ())
    G["w_out"] = _mm(s["o"], dxb, mode="tn", out_dtype=BF16, name="mixer_out_grad")
    if kind == 0:
        dq, dk, dv = _sb_bwd(s["qkv"], do, s["ctot"], H)
        dproj = jnp.concatenate([dq, dk, dv], axis=1)
        G["w_in"] = _mm(h, dproj, mode="tn", out_dtype=BF16, name="sb_qkv_grad",
                        deps=hook2(dv) if hook2 is not None else ())
        dh = _mm(dproj, w["w_in"], mode="nt", name="sb_qkv_bwd")
    elif kind == 1:
        dqn, dkn, dv, dcc, dcr = _softmax_bwd(
            s["qn"], s["kn"], s["vb"], s["cf"], do, s["of"], s["lse"], H, HEAD_DIM,
            1.0 / math.sqrt(HEAD_DIM), name="fox_attn_bwd")
        dq, dk, dvb, G["gq"], G["gk"] = _fox_prep_bwd(s["qkv"], w["gq"], w["gk"], dqn, dkn, dv, H)
        dcf = (dcc[:, :, 0] + dcr[:, 0, :]).T
        dcf = jnp.pad(dcf, ((0, 0), (0, LANE - H)))
        dflog, G["bf"] = _fox_gate_bwd(s["flog"], w["bf"], dcf)
        dproj = jnp.concatenate([dq, dk, dvb], axis=1)
        G["w_qkv"] = _mm(h, dproj, mode="tn", out_dtype=BF16, name="fox_qkv_grad",
                         deps=hook2(dv) if hook2 is not None else ())
        G["w_f"] = _mm(h, dflog, mode="tn", out_dtype=BF16, name="fox_gate_grad")
        dh = _mm(dproj, w["w_qkv"], mode="nt", name="fox_qkv_bwd")
        dh = _mm(dflog, w["w_f"], mode="nt", res=dh, name="fox_gate_bwd_proj")
    else:
        dqf, dkf, dv = _softmax_bwd(
            s["qf"], s["kf"], s["v"], None, do, s["of"], s["lse"], H, MLA_QK_PAD,
            1.0 / math.sqrt(MLA_QK), name="mla_attn_bwd")
        dqfull, dkv, dkr, G["gq"], G["gk"] = _mla_prep_bwd(
            s["qfull"], s["kv"], s["kr"], tabs, w["gq"], w["gk"], dqf, dkf, dv, H)
        G["w_uq"] = _mm(s["cq"], dqfull, mode="tn", out_dtype=BF16, name="mla_up_q_grad",
                        deps=hook2(dv) if hook2 is not None else ())
        G["w_ukv"] = _mm(s["ckv"], dkv, mode="tn", out_dtype=BF16, name="mla_up_kv_grad")
        dcq = _mm(dqfull, w["w_uq"], mode="nt", name="mla_up_q_bwd")
        dckv = _mm(dkv, w["w_ukv"], mode="nt", name="mla_up_kv_bwd")
        ddq, G["q_norm"] = _rmsnorm_bwd(s["dq"], w["q_norm"], dcq, want_f32=False, name="mla_q_norm_bwd")
        ddkv, G["kv_norm"] = _rmsnorm_bwd(s["dkv"], w["kv_norm"], dckv, want_f32=False, name="mla_kv_norm_bwd")
        G["w_dq"] = _mm(h, ddq, mode="tn", out_dtype=BF16, name="mla_down_q_grad")
        G["w_dkv"] = _mm(h, ddkv, mode="tn", out_dtype=BF16, name="mla_down_kv_grad")
        G["w_dr"] = _mm(h, dkr, mode="tn", out_dtype=BF16, name="mla_down_rope_grad")
        dh = _mm(ddq, w["w_dq"], mode="nt", name="mla_down_q_bwd")
        dh = _mm(ddkv, w["w_dkv"], mode="nt", res=dh, name="mla_down_kv_bwd")
        dh = _mm(dkr, w["w_dr"], mode="nt", res=dh, name="mla_down_rope_bwd")
    dx, dxb, G["mix_norm"] = _rmsnorm_bwd(s["x_in"], P["mix_norm"][i:i + 1], dh, dx, name="mix_norm_bwd")
    return dx, dxb, G


def _local_step(x, positions, loss_target, P, depth, H):
    tabs = _rope_tables(positions)
    saved = []
    for i in range(depth):
        x, s = _layer_fwd(i, x, P, tabs, H)
        saved.append(s)
    dx, dxb, loss = _loss_head(x, loss_target)
    grads = [None] * depth
    for i in reversed(range(depth)):
        dx, dxb, grads[i] = _layer_bwd(i, dx, dxb, saved[i], P, tabs, H)
    return loss, dx, grads


def _pad_cols(a, n):
    return jnp.pad(a, ((0, 0), (0, n - a.shape[1])))


def _prepare_layer(P, full, i, H, parts=("mix", "mlp")):
    W = H * HEAD_DIM
    kind, j = i % 3, i // 3
    if "mlp" in parts:
        P["mlp"][i] = {"w1": full["mlp_w1"][i], "w2": full["mlp_w2"][i]}
    if "mix" not in parts:
        return
    if kind == 0:
        P["sb"][j] = {"w_in": full["sb_w_in"][j], "w_out": full["sb_w_out"][j]}
    elif kind == 1:
        w = full["fox_w_in"][j]
        P["fox"][j] = {
            "w_qkv": w[:, :3 * W], "w_f": _pad_cols(w[:, 3 * W:], LANE),
            "bf": _pad_cols(full["fox_b_f"][j:j + 1], LANE),
            "gq": full["fox_q_gain"][j:j + 1], "gk": full["fox_k_gain"][j:j + 1],
            "w_out": full["fox_w_out"][j]}
    else:
        w = full["mla_w_in"][j]
        q_norm, kv_norm = full["mla_q_norm"][j], full["mla_kv_norm"][j]
        rq, rkv = q_norm.shape[0], kv_norm.shape[0]
        w_uq = full["mla_w_uq"][j].reshape(rq, H, MLA_QK)
        w_uq = jnp.pad(w_uq, ((0, 0), (0, 0), (0, MLA_QK_PAD - MLA_QK))).reshape(rq, H * MLA_QK_PAD)
        P["mla"][j] = {
            "w_dq": w[:, :rq], "w_dkv": w[:, rq:rq + rkv], "w_dr": _pad_cols(w[:, rq + rkv:], LANE),
            "q_norm": q_norm[None], "kv_norm": kv_norm[None],
            "w_uq": w_uq, "w_ukv": full["mla_w_ukv"][j],
            "gq": _pad_cols(full["mla_q_gain"][j:j + 1], MLA_QK_PAD),
            "gk": _pad_cols(full["mla_k_gain"][j:j + 1], MLA_QK_PAD),
            "w_out": full["mla_w_out"][j]}


def _prepare(full, H):
    depth = len(full["mlp_w1"])
    P = {"mix_norm": full["mix_norm"], "mlp_norm": full["mlp_norm"], "mlp": [None] * depth,
         "sb": [None] * len(full["sb_w_in"]), "fox": [None] * len(full["fox_w_in"]),
         "mla": [None] * len(full["mla_w_in"])}
    for i in range(depth):
        _prepare_layer(P, full, i, H)
    return P


def _layer_grads(i, G, H):
    kind = i % 3
    out = {"mix_norm": G["mix_norm"], "mlp_norm": G["mlp_norm"], "mlp_w1": G["w1"], "mlp_w2": G["w2"]}
    if kind == 0:
        out.update(sb_w_in=G["w_in"], sb_w_out=G["w_out"])
    elif kind == 1:
        out.update(fox_w_in=jnp.concatenate([G["w_qkv"], G["w_f"][:, :H]], axis=1), fox_b_f=G["bf"][:, :H],
                   fox_q_gain=G["gq"], fox_k_gain=G["gk"], fox_w_out=G["w_out"])
    else:
        rq = G["w_uq"].shape[0]
        out.update(
            mla_w_in=jnp.concatenate([G["w_dq"], G["w_dkv"], G["w_dr"][:, :MLA_ROPE]], axis=1),
            mla_q_norm=G["q_norm"], mla_kv_norm=G["kv_norm"],
            mla_w_uq=G["w_uq"].reshape(rq, H, MLA_QK_PAD)[:, :, :MLA_QK].reshape(rq, H * MLA_QK),
            mla_w_ukv=G["w_ukv"], mla_q_gain=G["gq"][:, :MLA_QK], mla_k_gain=G["gk"][:, :MLA_QK],
            mla_w_out=G["w_out"])
    return out


def _layer_of(name, j):
    if name.startswith("sb_"):
        return 3 * j
    if name.startswith("fox_"):
        return 3 * j + 1
    if name.startswith("mla_"):
        return 3 * j + 2
    return j


def _unprepare(grads, H):
    out = {}
    for i, G in enumerate(grads):
        for n, g in _layer_grads(i, G, H).items():
            out.setdefault(n, []).append(g)
    return out


class _Place:
    def __init__(self, x, y, c):
        self.x, self.y, self.c = x, y, c
        self.dev = 4 * x + 2 * y + c
        self.chip = 2 * x + y
        self.id = (x, y, c)


def _peers(me, kind):
    if kind == "ici":
        return [_Place(1 - me.x, me.y, me.c), _Place(me.x, 1 - me.y, me.c), _Place(1 - me.x, 1 - me.y, me.c)]
    return [_Place(me.x, me.y, 1 - me.c)]


def _exchange(name, kind, operands, out_shapes, aliases, n_remote, n_local, plan):
    n_in, n_out = len(operands), len(out_shapes)

    def body(*refs):
        in_refs, out_refs = refs[:n_in], refs[n_in:n_in + n_out]
        send_sems, recv_sems, local_sems = refs[n_in + n_out:]
        me = _Place(lax.axis_index("x"), lax.axis_index("y"), lax.axis_index("c"))
        peers = _peers(me, kind)
        remote, local = plan(me, peers, in_refs, out_refs)
        assert len(remote) == n_remote and len(local) == n_local
        sends = []
        for n, (src, dst, k, _) in enumerate(remote):
            cp = pltpu.make_async_remote_copy(
                src_ref=src, dst_ref=dst, send_sem=send_sems.at[n], recv_sem=recv_sems.at[n],
                device_id=peers[k].id, device_id_type=MESH)
            cp.start()
            sends.append(cp)
        own = []
        for n, (src, dst) in enumerate(local):
            cp = pltpu.make_async_copy(src, dst, local_sems.at[n])
            cp.start()
            own.append(cp)
        for n, (src, _, k, landing) in enumerate(remote):
            pltpu.make_async_remote_copy(
                src_ref=src, dst_ref=landing, send_sem=send_sems.at[n], recv_sem=recv_sems.at[n],
                device_id=peers[k].id, device_id_type=MESH).wait_recv()
        for cp in sends:
            cp.wait_send()
        for cp in own:
            cp.wait()

    outs = pl.pallas_call(
        body, name=name, in_specs=[ANY] * n_in, out_specs=tuple([ANY] * n_out),
        out_shape=tuple(out_shapes), input_output_aliases=aliases,
        scratch_shapes=[pltpu.SemaphoreType.DMA((n_remote,)), pltpu.SemaphoreType.DMA((n_remote,)),
                        pltpu.SemaphoreType.DMA((max(n_local, 1),))],
    )(*operands)
    return list(outs)


def _window(ref, kind, d, shard_shape):
    r, c = shard_shape
    if kind == "col":
        return ref.at[:, pl.ds(pl.multiple_of(d * c, c), c)]
    return ref.at[pl.ds(pl.multiple_of(d * r, r), r), :]


def _full_shape(kind, shard_shape):
    r, c = shard_shape
    return (r, N_DEV * c) if kind == "col" else (N_DEV * r, c)


HBM = pl.BlockSpec(memory_space=pltpu.HBM)
SEM = pl.BlockSpec(memory_space=pltpu.SEMAPHORE)
EFFECT = pltpu.SideEffectType.DATAFLOW_SIDE_EFFECTING


class _Flight:
    def __init__(self, name, kind, n_remote, plan, send_sems, recv_sems, bufs, token):
        self.name, self.kind, self.n_remote, self.plan = name, kind, n_remote, plan
        self.send_sems, self.recv_sems, self.bufs, self.token = send_sems, recv_sems, bufs, token


def _start_copies(name, kind, bufs, n_remote, plan, after=None):
    nb = len(bufs)
    n_after = 0 if after is None else 1

    def body(*refs):
        in_refs = refs[:nb]
        send_sems, recv_sems = refs[nb + n_after], refs[nb + n_after + 1]
        token = refs[2 * nb + n_after + 2]
        me = _Place(lax.axis_index("x"), lax.axis_index("y"), lax.axis_index("c"))
        peers = _peers(me, kind)
        for n, (src, dst, k, _) in enumerate(plan(me, peers, in_refs)):
            pltpu.make_async_remote_copy(
                src_ref=src, dst_ref=dst, send_sem=send_sems.at[n], recv_sem=recv_sems.at[n],
                device_id=peers[k].id, device_id_type=MESH).start()
        token[...] = jnp.zeros_like(token)

    outs = pl.pallas_call(
        body, name=name, in_specs=[HBM] * nb + [ANY] * n_after,
        out_specs=(SEM, SEM) + (HBM,) * nb + (pl.BlockSpec(memory_space=pltpu.VMEM),),
        out_shape=(pltpu.SemaphoreType.DMA((n_remote,)), pltpu.SemaphoreType.DMA((n_remote,)))
        + tuple(pltpu.HBM(b.shape, b.dtype) for b in bufs) + (jax.ShapeDtypeStruct((8, LANE), F32),),
        input_output_aliases={a: 2 + a for a in range(nb)},
        compiler_params=pltpu.CompilerParams(has_side_effects=EFFECT),
    )(*[pltpu.with_memory_space_constraint(b, pltpu.HBM) for b in bufs], *([after] if n_after else []))
    return _Flight(name, kind, n_remote, plan, outs[0], outs[1], list(outs[2:2 + nb]), outs[2 + nb])


def _wait_copies(flight, after):
    nb = len(flight.bufs)
    plan, kind = flight.plan, flight.kind
    after = list(after) if isinstance(after, (tuple, list)) else [after]

    def body(*refs):
        in_refs = refs[:nb]
        send_sems, recv_sems = refs[nb], refs[nb + 1]
        me = _Place(lax.axis_index("x"), lax.axis_index("y"), lax.axis_index("c"))
        peers = _peers(me, kind)
        for n, (src, _, k, landing) in enumerate(plan(me, peers, in_refs)):
            cp = pltpu.make_async_remote_copy(
                src_ref=src, dst_ref=landing, send_sem=send_sems.at[n], recv_sem=recv_sems.at[n],
                device_id=peers[k].id, device_id_type=MESH)
            cp.wait_send()
            cp.wait_recv()

    outs = pl.pallas_call(
        body, name=flight.name.replace("_start", "_wait"),
        in_specs=[HBM] * nb + [SEM, SEM] + [ANY] * len(after), out_specs=(HBM,) * nb,
        out_shape=tuple(pltpu.HBM(b.shape, b.dtype) for b in flight.bufs),
        input_output_aliases={a: a for a in range(nb)},
        compiler_params=pltpu.CompilerParams(has_side_effects=EFFECT),
    )(*flight.bufs, flight.send_sems, flight.recv_sems, *after)
    return list(outs)


def _own_in_place(shard, kind, dev, dtype, deps=()):
    r, c = shard.shape
    tr = _row_tile(r, c)
    nb = r // tr

    def body(dev_ref, s_ref, *rest):
        o_ref = rest[-1]
        o_ref[...] = s_ref[...].astype(dtype)

    if kind == "col":
        o_spec = pl.BlockSpec((tr, c), lambda i, dev_ref: (i, dev_ref[0]))
    else:
        o_spec = pl.BlockSpec((tr, c), lambda i, dev_ref: (dev_ref[0] * nb + i, 0))
    return pl.pallas_call(
        body, name="own_in_place",
        grid_spec=pltpu.PrefetchScalarGridSpec(
            num_scalar_prefetch=1, grid=(nb,),
            in_specs=[pl.BlockSpec((tr, c), lambda i, dev_ref: (i, 0))] + [ANY] * len(deps), out_specs=o_spec),
        out_shape=jax.ShapeDtypeStruct(_full_shape(kind, shard.shape), dtype),
        compiler_params=_params(("parallel",)),
    )(dev, shard, *deps)


def _gather_ici_start(fulls, kinds, shapes, tag, after=None):
    n = len(fulls)

    def plan(me, peers, refs):
        remote = []
        for a in range(n):
            mine = _window(refs[a], kinds[a], me.dev, shapes[a])
            for k, p in enumerate(peers):
                remote.append((mine, mine, k, _window(refs[a], kinds[a], p.dev, shapes[a])))
        return remote

    return _start_copies("gather_ici_start_" + tag, "ici", list(fulls), 3 * n, plan, after)


def _gather_d2d_start(fulls, kinds, shapes, tag):
    n = len(fulls)

    def plan(me, peers, refs):
        remote = []
        for a in range(n):
            for ch in range(N_CHIP):
                held = _window(refs[a], kinds[a], 2 * ch + me.c, shapes[a])
                remote.append((held, held, 0, _window(refs[a], kinds[a], 2 * ch + 1 - me.c, shapes[a])))
        return remote

    return _start_copies("gather_d2d_start_" + tag, "d2d", fulls, N_CHIP * n, plan)


def _scatter_d2d_start(grads, kinds, shapes, tag):
    n = len(grads)
    lands = [lax.empty((N_CHIP,) + tuple(s), g.dtype) for g, s in zip(grads, shapes)]

    def plan(me, peers, refs):
        remote = []
        for a in range(n):
            for ch in range(N_CHIP):
                remote.append((_window(refs[a], kinds[a], 2 * ch + 1 - me.c, shapes[a]),
                               refs[n + a].at[ch], 0, refs[n + a].at[ch]))
        return remote

    return _start_copies("scatter_d2d_start_" + tag, "d2d", list(grads) + lands, N_CHIP * n, plan)


def _scatter_ici_start(sums, shapes, tag, after=None):
    n = len(sums)
    lands = [lax.empty((N_CHIP - 1,) + tuple(s), BF16) for s in shapes]

    def plan(me, peers, refs):
        remote = []
        for a in range(n):
            for k, p in enumerate(peers):
                remote.append((refs[a].at[p.chip], refs[n + a].at[k], k, refs[n + a].at[k]))
        return remote

    return _start_copies("scatter_ici_start_" + tag, "ici", list(sums) + lands, 3 * n, plan, after)


def _pair_add(g, r4, kind, shard_shape, core, *, name):
    r, c = shard_shape
    tr = _row_tile(r, c, 524288)
    nb = r // tr

    def body(core_ref, g_ref, r_ref, o_ref):
        o_ref[...] = (g_ref[...].astype(F32) + r_ref[...].astype(F32)).astype(BF16)

    if kind == "col":
        g_spec = pl.BlockSpec((tr, c), lambda ch, i, core_ref: (i, 2 * ch + core_ref[0]))
    else:
        g_spec = pl.BlockSpec((tr, c), lambda ch, i, core_ref: ((2 * ch + core_ref[0]) * nb + i, 0))
    slot = pl.BlockSpec((None, tr, c), lambda ch, i, core_ref: (ch, i, 0))
    return pl.pallas_call(
        body, name=name,
        grid_spec=pltpu.PrefetchScalarGridSpec(
            num_scalar_prefetch=1, grid=(N_CHIP, nb), in_specs=[g_spec, slot], out_specs=slot),
        out_shape=jax.ShapeDtypeStruct((N_CHIP, r, c), BF16),
        compiler_params=_params(("parallel", "parallel")),
    )(core, g, r4)


def _allreduce_small_start(part, chip):
    def plan_swap(me, peers, ins, outs):
        return [(ins[0], outs[0], 0, outs[0])], []

    (got,) = _exchange("allreduce_small_swap", "d2d", [part], [jax.ShapeDtypeStruct(part.shape, F32)],
                       {}, 1, 0, plan_swap)

    def body(chip_ref, a_ref, b_ref, o_ref):
        o_ref[...] = a_ref[...] + b_ref[...]

    whole = pl.BlockSpec(part.shape, lambda i, chip_ref: (0, 0))
    slots = pl.pallas_call(
        body, name="allreduce_small_pair",
        grid_spec=pltpu.PrefetchScalarGridSpec(
            num_scalar_prefetch=1, grid=(1,), in_specs=[whole, whole],
            out_specs=pl.BlockSpec((None,) + part.shape, lambda i, chip_ref: (chip_ref[0], 0, 0))),
        out_shape=jax.ShapeDtypeStruct((N_CHIP,) + part.shape, F32),
    )(chip, part, got)

    def plan(me, peers, refs):
        mine = refs[0].at[me.chip]
        return [(mine, mine, k, refs[0].at[p.chip]) for k, p in enumerate(peers)]

    return _start_copies("allreduce_small_start", "ici", [slots], 3, plan)


def _row_tile(r, c, limit=262144):
    best = None
    for t in range(8, r + 1, 8):
        if r % t == 0 and t * c <= limit:
            best = t
    return best if best is not None else r


def _adam_math(w, g, m, v):
    m = ADAM_B1 * m + (1.0 - ADAM_B1) * g
    v = ADAM_B2 * v + (1.0 - ADAM_B2) * (g * g)
    m_hat = m / (1.0 - ADAM_B1 ** ADAM_STEP)
    v_hat = v / (1.0 - ADAM_B2 ** ADAM_STEP)
    delta = -ADAM_LR * (m_hat / (jnp.sqrt(v_hat) + ADAM_EPS) + ADAM_WD * w)
    return delta, m, v


def _adamw(w, m, v, sums, lands, chip, layer, prev, *, name):
    L, r, c = w.shape
    tr = _row_tile(r, c)
    has_prev = prev is not None

    def body(chip_ref, *refs):
        w_ref, m_ref, v_ref, s_ref, l_ref = refs[:5]
        g_ref, d_ref, nm_ref, nv_ref = refs[-4:]
        g = (s_ref[...].astype(F32) + l_ref[0].astype(F32)) + l_ref[1].astype(F32) + l_ref[2].astype(F32)
        delta, nm, nv = _adam_math(w_ref[...], g, m_ref[...], v_ref[...])
        g_ref[...] = g
        d_ref[...] = delta
        nm_ref[...] = nm
        nv_ref[...] = nv

    slab = pl.BlockSpec((None, tr, c), lambda i, chip_ref: (layer, i, 0))
    in_specs = [slab, slab, slab, pl.BlockSpec((None, tr, c), lambda i, chip_ref: (chip_ref[0], i, 0)),
                pl.BlockSpec((N_CHIP - 1, tr, c), lambda i, chip_ref: (0, i, 0))]
    args = [chip, w, m, v, sums, lands]
    aliases = {}
    if has_prev:
        in_specs += [ANY] * 4
        args += list(prev)
        aliases = {6 + n: n for n in range(4)}
    return pl.pallas_call(
        body, name=name,
        grid_spec=pltpu.PrefetchScalarGridSpec(
            num_scalar_prefetch=1, grid=(r // tr,), in_specs=in_specs, out_specs=(slab,) * 4),
        out_shape=tuple(jax.ShapeDtypeStruct((L, r, c), F32) for _ in range(4)),
        input_output_aliases=aliases, compiler_params=_params(("parallel",)),
    )(*args)


def _adamw_small(w, g, m, v):
    def body(w_ref, g_ref, m_ref, v_ref, d_ref, nm_ref, nv_ref):
        d_ref[...], nm_ref[...], nv_ref[...] = _adam_math(w_ref[...], g_ref[...], m_ref[...], v_ref[...])

    return pl.pallas_call(
        body, name="adamw_small", out_shape=tuple(jax.ShapeDtypeStruct(w.shape, F32) for _ in range(3)),
    )(w, g, m, v)


def _sum4(parts):
    def body(p_ref, o_ref):
        o_ref[...] = (p_ref[0] + p_ref[1]) + p_ref[2] + p_ref[3]

    return pl.pallas_call(
        body, name="sum_chips", out_shape=jax.ShapeDtypeStruct(parts.shape[1:], F32))(parts)


_WEIGHTS = ["mix_norm", "mlp_norm", "sb_w_in", "sb_w_out", "fox_w_in", "fox_b_f", "fox_q_gain", "fox_k_gain",
            "fox_w_out", "mla_w_in", "mla_q_norm", "mla_kv_norm", "mla_w_uq", "mla_w_ukv", "mla_q_gain",
            "mla_k_gain", "mla_w_out", "mlp_w1", "mlp_w2"]
_BIG = {"sb_w_in": "col", "sb_w_out": "row", "fox_w_in": "row", "fox_w_out": "row", "mla_w_in": "row",
        "mla_w_uq": "col", "mla_w_ukv": "col", "mla_w_out": "row", "mlp_w1": "col", "mlp_w2": "row"}


def _layer_big(i):
    kind, j = i % 3, i // 3
    mixer = {0: ["sb_w_in", "sb_w_out"], 1: ["fox_w_in", "fox_w_out"],
             2: ["mla_w_in", "mla_w_uq", "mla_w_ukv", "mla_w_out"]}[kind]
    return [(n, j) for n in mixer] + [("mlp_w1", i), ("mlp_w2", i)]


def _stack_to_cols(a):
    r = a.shape[0] // N_DEV
    return a.reshape(N_DEV, r, a.shape[1]).transpose(1, 0, 2).reshape(r, N_DEV * a.shape[1])


def _cols_to_stack(a):
    c = a.shape[1] // N_DEV
    return a.reshape(a.shape[0], N_DEV, c).transpose(1, 0, 2).reshape(N_DEV * a.shape[0], c)


def _pack_rows(rows, width):
    rows = [jnp.pad(r.reshape(-1).astype(F32), (0, width - r.size)) for r in rows]
    pad = (-len(rows)) % 8
    rows += [jnp.zeros((width,), F32)] * pad
    return jnp.stack(rows)


def kernel(x, positions, mix_norm, mlp_norm, sb_w_in, sb_w_out, fox_w_in, fox_b_f, fox_q_gain, fox_k_gain, fox_w_out, mla_w_in, mla_q_norm, mla_kv_norm, mla_w_uq, mla_w_ukv, mla_q_gain, mla_k_gain, mla_w_out, mlp_w1, mlp_w2, loss_target, m_mix_norm, m_mlp_norm, m_sb_w_in, m_sb_w_out, m_fox_w_in, m_fox_b_f, m_fox_q_gain, m_fox_k_gain, m_fox_w_out, m_mla_w_in, m_mla_q_norm, m_mla_kv_norm, m_mla_w_uq, m_mla_w_ukv, m_mla_q_gain, m_mla_k_gain, m_mla_w_out, m_mlp_w1, m_mlp_w2, v_mix_norm, v_mlp_norm, v_sb_w_in, v_sb_w_out, v_fox_w_in, v_fox_b_f, v_fox_q_gain, v_fox_k_gain, v_fox_w_out, v_mla_w_in, v_mla_q_norm, v_mla_kv_norm, v_mla_w_uq, v_mla_w_ukv, v_mla_q_gain, v_mla_k_gain, v_mla_w_out, v_mlp_w1, v_mlp_w2):
    w_in = dict(zip(_WEIGHTS, (mix_norm, mlp_norm, sb_w_in, sb_w_out, fox_w_in, fox_b_f, fox_q_gain, fox_k_gain, fox_w_out, mla_w_in, mla_q_norm, mla_kv_norm, mla_w_uq, mla_w_ukv, mla_q_gain, mla_k_gain, mla_w_out, mlp_w1, mlp_w2)))
    m_in = dict(zip(_WEIGHTS, (m_mix_norm, m_mlp_norm, m_sb_w_in, m_sb_w_out, m_fox_w_in, m_fox_b_f, m_fox_q_gain, m_fox_k_gain, m_fox_w_out, m_mla_w_in, m_mla_q_norm, m_mla_kv_norm, m_mla_w_uq, m_mla_w_ukv, m_mla_q_gain, m_mla_k_gain, m_mla_w_out, m_mlp_w1, m_mlp_w2)))
    v_in = dict(zip(_WEIGHTS, (v_mix_norm, v_mlp_norm, v_sb_w_in, v_sb_w_out, v_fox_w_in, v_fox_b_f, v_fox_q_gain, v_fox_k_gain, v_fox_w_out, v_mla_w_in, v_mla_q_norm, v_mla_kv_norm, v_mla_w_uq, v_mla_w_ukv, v_mla_q_gain, v_mla_k_gain, v_mla_w_out, v_mlp_w1, v_mlp_w2)))
    depth, D = mix_norm.shape
    H = D // HEAD_DIM
    n_mla = mla_w_in.shape[0]
    dev = 4 * lax.axis_index("x") + 2 * lax.axis_index("y") + lax.axis_index("c")
    core = lax.axis_index("c").astype(jnp.int32).reshape(1)
    dev_arr = dev.astype(jnp.int32).reshape(1)

    chip = (2 * lax.axis_index("x") + lax.axis_index("y")).astype(jnp.int32).reshape(1)
    nq, nkv = mla_q_norm.shape[1], mla_kv_norm.shape[1]

    units, u_mix, u_mlp = [], {}, {}
    for i in range(depth):
        big = _layer_big(i)
        u_mix[i] = len(units)
        if i < SPLIT_LAYERS:
            units += [(i, ("mix",), big[:-2]), (i, ("mlp",), big[-2:])]
        else:
            units.append((i, ("mix", "mlp"), big))
        u_mlp[i] = len(units) - 1

    def split(i):
        return u_mix[i] != u_mlp[i]

    def tag_of(u):
        i, parts, _ = units[u]
        return f"l{i}" if len(parts) == 2 else f"l{i}_{parts[0]}"

    kinds_of, shapes_of, ici = [], [], []
    for i, parts, names in units:
        shards = [w_in[n][j] for n, j in names]
        kinds = [_BIG[n] for n, _ in names]
        behind = (ici[-1].token,) if ici else ()
        fulls = [_own_in_place(s, k, dev_arr, BF16, behind) for s, k in zip(shards, kinds)]
        if i % 3 == 2:
            shards.append(_pack_rows([mla_q_norm[i // 3], mla_kv_norm[i // 3]], LANE))
            kinds.append("row")
            fulls.append(_own_in_place(shards[-1], "row", dev_arr, F32, behind))
        kinds_of.append(kinds)
        shapes_of.append([s.shape for s in shards])
        ici.append(_gather_ici_start(fulls, kinds, shapes_of[-1], tag_of(len(ici)),
                                     ici[-1].token if ici else None))

    full = {n: [None] * w_in[n].shape[0] for n in _WEIGHTS}
    for n in ("mix_norm", "mlp_norm", "fox_b_f", "fox_q_gain", "fox_k_gain", "mla_q_gain", "mla_k_gain"):
        full[n] = w_in[n]
    P = {"mix_norm": mix_norm, "mlp_norm": mlp_norm, "mlp": [None] * depth,
         "sb": [None] * sb_w_in.shape[0], "fox": [None] * fox_w_in.shape[0], "mla": [None] * n_mla}

    def forward_to_sibling(u, after):
        arrived = _wait_copies(ici[u], after)
        return _gather_d2d_start(arrived, kinds_of[u], shapes_of[u], tag_of(u))

    def finish_gather(u, flight, after):
        i, parts, names = units[u]
        got = _wait_copies(flight, after)
        for (n, j), a in zip(names, got):
            full[n][j] = _stack_to_cols(a) if n == "fox_w_in" else a
        if i % 3 == 2:
            tiles = got[-1].reshape(N_DEV, 8, LANE)
            full["mla_q_norm"][i // 3] = tiles[:, 0, :nq].reshape(-1)
            full["mla_kv_norm"][i // 3] = tiles[:, 1, :nkv].reshape(-1)
        _prepare_layer(P, full, i, H, parts)

    tabs = _rope_tables(positions[0])
    d2d = forward_to_sibling(0, ici[-1].token)
    finish_gather(0, d2d, d2d.token)
    xs, saved = x[0], []
    for i in range(depth):
        nxt = {}

        def hook(x_mid, i=i, nxt=nxt):
            if split(i):
                mlp_d2d = forward_to_sibling(u_mlp[i], x_mid)
                finish_gather(u_mlp[i], mlp_d2d, mlp_d2d.token)
            if 1 <= i < depth - 1 and not split(i + 1):
                nxt["d2d"] = forward_to_sibling(u_mix[i + 1], x_mid)
                return (nxt["d2d"].token,)
            return ()

        xs, s = _layer_fwd(i, xs, P, tabs, H, hook)
        saved.append(s)
        if i + 1 < depth:
            if "d2d" in nxt:
                finish_gather(u_mix[i + 1], nxt["d2d"], xs)
            else:
                mix_d2d = forward_to_sibling(u_mix[i + 1], xs)
                finish_gather(u_mix[i + 1], mix_d2d, mix_d2d.token)
    dx, dxb, loss_part = _loss_head(xs, loss_target[0])

    def unit_meta(u):
        names = units[u][2]
        return [_BIG[n] for n, _ in names], [w_in[n].shape[1:] for n, _ in names]

    def to_sibling_start(u, Gl):
        grads = [(_cols_to_stack(Gl[n]) if n == "fox_w_in" else Gl[n]) for n, _ in units[u][2]]
        return _scatter_d2d_start(grads, *unit_meta(u), tag_of(u))

    def pair_and_send(u, flight, after, before=None):
        n = len(units[u][2])
        kinds, shapes = unit_meta(u)
        got = _wait_copies(flight, after)
        sums = [_pair_add(g, r4, k, s, core, name="pair_add_" + tag_of(u))
                for g, r4, k, s in zip(got[:n], got[n:], kinds, shapes)]
        return _scatter_ici_start(sums, shapes, tag_of(u), before)

    G, to_chips, deps, flying = [None] * depth, [None] * len(units), (), {}
    for i in reversed(range(depth)):
        def hook(dxb_mid, Gp, i=i):
            out = ()
            if i + 1 < depth:
                u = u_mix[i + 1]
                to_chips[u] = pair_and_send(u, flying.pop(u), dxb_mid)
                out = (to_chips[u].token,)
            if split(i):
                flying[u_mlp[i]] = to_sibling_start(u_mlp[i], {"mlp_w1": Gp["w1"], "mlp_w2": Gp["w2"]})
                out = out + (flying[u_mlp[i]].token,)
            return out

        def hook2(dv, i=i):
            if not split(i):
                return ()
            u = u_mlp[i]
            to_chips[u] = pair_and_send(u, flying.pop(u), dv)
            return (to_chips[u].token,)

        dx, dxb, Gp = _layer_bwd(i, dx, dxb, saved[i], P, tabs, H, deps, hook, hook2)
        G[i] = _layer_grads(i, Gp, H)
        flying[u_mix[i]] = to_sibling_start(u_mix[i], G[i])
        deps = (flying[u_mix[i]].token,)
    grad_x = dx
    to_chips[0] = pair_and_send(0, flying.pop(0), dx)

    small = [n for n in _WEIGHTS if n not in _BIG]
    rows = []
    for n in small:
        for j in range(w_in[n].shape[0]):
            rows.append(G[_layer_of(n, j)][n])
    loss_row = len(rows)
    rows.append(loss_part)
    small_flight = _allreduce_small_start(_pack_rows(rows, D) + to_chips[0].token[0, 0], chip)

    results, after = {}, [small_flight.token]
    for u in reversed(range(len(units))):
        names = units[u][2]
        got = _wait_copies(to_chips[u], after)
        for a, (n, j) in enumerate(names):
            results[n] = _adamw(w_in[n], m_in[n], v_in[n], got[a], got[len(names) + a], chip, j, results.get(n),
                                name="adamw_" + n)
        after = [results[n][0] for n, _ in names]

    (parts,) = _wait_copies(small_flight, after)
    sgr = _sum4(parts)
    g_rows, w_rows, m_rows, v_rows, at = [], [], [], [], 0
    for n in small:
        for j in range(w_in[n].shape[0]):
            width = w_in[n].shape[1]
            if n in ("mla_q_norm", "mla_kv_norm"):
                g_rows.append(lax.dynamic_slice(sgr[at], (dev * width,), (width,)))
            else:
                g_rows.append(sgr[at, :width])
            w_rows.append(w_in[n][j])
            m_rows.append(m_in[n][j])
            v_rows.append(v_in[n][j])
            at += 1
    g_pack = _pack_rows(g_rows, D)
    d_pack, nm_pack, nv_pack = _adamw_small(
        _pack_rows(w_rows, D), g_pack, _pack_rows(m_rows, D), _pack_rows(v_rows, D))
    at = 0
    for n in small:
        L, width = w_in[n].shape
        results[n] = tuple(p[at:at + L, :width] for p in (g_pack, d_pack, nm_pack, nv_pack))
        at += L

    out = [sgr[loss_row, 0], grad_x[None]]
    for part in range(4):
        out += [results[n][part] for n in _WEIGHTS]
    return tuple(out)
```
